```python
import jax, jax.numpy as jnp
from jax import lax
import numpy as np

D_MODEL = 2048
BATCH = 8
SEQ = 2048
DEPTH = 2

GRID_W = 64
BLOCK = 128
HEAD_DIM = 128
ATTN_WIDTH = D_MODEL // 2
N_Q_HEADS = ATTN_WIDTH // HEAD_DIM
N_KV_HEADS = N_Q_HEADS // 4
ROPE_THETA = 10000.0
ROPE_PAIRS = HEAD_DIM // 2
ROPE_FREQ_PER_AXIS = ROPE_PAIRS // 2
CONV_CH = D_MODEL // 2
CONV_WIDTH = 31
CONV_PAD = CONV_WIDTH // 2
SG_CH = D_MODEL // 2
SG_GROUP_CH = 128
SG_GROUPS = SG_CH // SG_GROUP_CH
SG_CHUNK = 128
N_BRANCH = 3
D_FF = -(-8 * D_MODEL // (3 * 256)) * 256

Q_COLS = N_Q_HEADS * HEAD_DIM
KV_COLS = N_KV_HEADS * HEAD_DIM
IN_WIDTHS = [Q_COLS, KV_COLS, KV_COLS, 2 * CONV_CH, 2 * SG_CH, N_BRANCH * D_MODEL]
IN_COLS = int(sum(IN_WIDTHS))
IN_SPLITS = [int(s) for s in np.cumsum(IN_WIDTHS)[:-1]]

kernel_name = "hybrid_gqa_conformer_sgu_encoder"


def rms_norm(x, g, eps=1e-6):
    xf = x.astype(jnp.float32)
    y = xf * lax.rsqrt(jnp.mean(xf * xf, axis=-1, keepdims=True) + eps)
    return (y * g.astype(jnp.float32)).astype(x.dtype)


def layer_norm(x, g, b, eps=1e-5):
    xf = x.astype(jnp.float32)
    mu = jnp.mean(xf, axis=-1, keepdims=True)
    xc = xf - mu
    y = xc * lax.rsqrt(jnp.mean(xc * xc, axis=-1, keepdims=True) + eps)
    return (y * g.astype(jnp.float32) + b.astype(jnp.float32)).astype(x.dtype)


def axial_rope(n):
    rows = n // GRID_W
    row = jnp.repeat(jnp.arange(rows, dtype=jnp.float32), GRID_W)
    col = jnp.tile(jnp.arange(GRID_W, dtype=jnp.float32), rows)
    inv = ROPE_THETA ** (-jnp.arange(ROPE_FREQ_PER_AXIS, dtype=jnp.float32) / ROPE_FREQ_PER_AXIS)
    ang = jnp.concatenate([row[:, None] * inv, col[:, None] * inv], axis=-1)
    return jnp.cos(ang), jnp.sin(ang)


def apply_rope(x, cos, sin):
    xf = x.astype(jnp.float32)
    c = cos[None, :, None, :]
    s = sin[None, :, None, :]
    x1, x2 = xf[..., :ROPE_PAIRS], xf[..., ROPE_PAIRS:]
    return jnp.concatenate([x1 * c - x2 * s, x2 * c + x1 * s], axis=-1).astype(x.dtype)


def blocked_gqa(q, k, v):
    B, S = q.shape[0], q.shape[1]
    nb = S // BLOCK
    grp = N_Q_HEADS // N_KV_HEADS
    qb = q.reshape(B, nb, BLOCK, N_KV_HEADS, grp, HEAD_DIM).transpose(1, 0, 2, 3, 4, 5)
    scale = HEAD_DIM ** -0.5

    def one_block(q_blk):
        s = jnp.einsum('bqhgd,bkhd->bhgqk', q_blk, k).astype(jnp.float32) * scale
        p = jax.nn.softmax(s, axis=-1).astype(v.dtype)
        return jnp.einsum('bhgqk,bkhd->bqhgd', p, v)

    o = lax.map(one_block, qb)
    return o.transpose(1, 0, 2, 3, 4, 5).reshape(B, S, N_Q_HEADS * HEAD_DIM)


def conformer_conv(glu_in, w_dw, b_dw, ln_g, ln_b):
    a, g = jnp.split(glu_in, 2, axis=-1)
    z = a * jax.nn.sigmoid(g)
    z = lax.conv_general_dilated(
        z, w_dw, window_strides=(1,), padding=((CONV_PAD, CONV_PAD),),
        dimension_numbers=('NWC', 'WIO', 'NWC'), feature_group_count=CONV_CH) + b_dw
    z = layer_norm(z, ln_g, ln_b)
    return jax.nn.silu(z)


def spatial_gating(uv, ln_g, ln_b, w_s, b_s):
    B, S = uv.shape[0], uv.shape[1]
    nc = S // SG_CHUNK
    u, v = jnp.split(jax.nn.gelu(uv), 2, axis=-1)
    v = layer_norm(v.reshape(B, S, SG_GROUPS, SG_GROUP_CH),
                   ln_g.reshape(SG_GROUPS, SG_GROUP_CH), ln_b.reshape(SG_GROUPS, SG_GROUP_CH))
    v = v.reshape(B, nc, SG_CHUNK, SG_GROUPS, SG_GROUP_CH)
    v = jnp.einsum('gpq,bnqgc->bnpgc', w_s, v) + b_s.T[None, None, :, :, None]
    return u * v.reshape(B, S, SG_CH)


def _fwd_setup_inputs(seed: int = 0) -> dict:
    key = jax.random.key(seed)
    ks = jax.random.split(key, 24)
    f32 = jnp.float32

    def nrm(k, shape, scale):
        return jax.random.normal(k, shape, f32) * scale

    L = DEPTH
    return {
        "x": nrm(ks[0], (BATCH, SEQ, D_MODEL), 1.0),
        "g_mix": 1.0 + nrm(ks[1], (L, D_MODEL), 0.02),
        "w_in": nrm(ks[2], (L, D_MODEL, IN_COLS), D_MODEL ** -0.5),
        "b_gate": nrm(ks[3], (L, N_BRANCH * D_MODEL), 0.01),
        "q_norm_g": 1.0 + nrm(ks[4], (L, HEAD_DIM), 0.02),
        "k_norm_g": 1.0 + nrm(ks[5], (L, HEAD_DIM), 0.02),
        "w_attn_o": nrm(ks[6], (L, Q_COLS, D_MODEL), Q_COLS ** -0.5),
        "w_dw": nrm(ks[7], (L, CONV_WIDTH, 1, CONV_CH), CONV_WIDTH ** -0.5),
        "b_dw": nrm(ks[8], (L, CONV_CH), 0.01),
        "conv_ln_g": 1.0 + nrm(ks[9], (L, CONV_CH), 0.02),
        "conv_ln_b": nrm(ks[10], (L, CONV_CH), 0.01),
        "w_conv_o": nrm(ks[11], (L, CONV_CH, D_MODEL), CONV_CH ** -0.5),
        "sg_ln_g": 1.0 + nrm(ks[12], (L, SG_CH), 0.02),
        "sg_ln_b": nrm(ks[13], (L, SG_CH), 0.01),
        "w_s": nrm(ks[14], (L, SG_GROUPS, SG_CHUNK, SG_CHUNK), SG_CHUNK ** -0.5),
        "b_s": 1.0 + nrm(ks[15], (L, SG_GROUPS, SG_CHUNK), 0.01),
        "w_sg_o": nrm(ks[16], (L, SG_CH, D_MODEL), SG_CH ** -0.5),
        "w_out": nrm(ks[17], (L, D_MODEL, D_MODEL), D_MODEL ** -0.5),
        "g_ffn": 1.0 + nrm(ks[18], (L, D_MODEL), 0.02),
        "w_ff_gate": nrm(ks[19], (L, D_MODEL, D_FF), D_MODEL ** -0.5),
        "w_ff_up": nrm(ks[20], (L, D_MODEL, D_FF), D_MODEL ** -0.5),
        "w_ff_down": nrm(ks[21], (L, D_FF, D_MODEL), D_FF ** -0.5),
        "g_final": 1.0 + nrm(ks[22], (D_MODEL,), 0.02),
    }


def _fwd_reference(x, g_mix, w_in, b_gate, q_norm_g, k_norm_g, w_attn_o, w_dw, b_dw,
              conv_ln_g, conv_ln_b, w_conv_o, sg_ln_g, sg_ln_b, w_s, b_s, w_sg_o,
              w_out, g_ffn, w_ff_gate, w_ff_up, w_ff_down, g_final):
    B, S, _ = x.shape
    cos, sin = axial_rope(S)
    for l in range(DEPTH):
        h = rms_norm(x, g_mix[l])
        proj = h @ w_in[l]
        q, k, v, conv_in, sg_in, gate_logits = jnp.split(proj, IN_SPLITS, axis=-1)

        q = rms_norm(q.reshape(B, S, N_Q_HEADS, HEAD_DIM), q_norm_g[l])
        k = rms_norm(k.reshape(B, S, N_KV_HEADS, HEAD_DIM), k_norm_g[l])
        v = v.reshape(B, S, N_KV_HEADS, HEAD_DIM)
        q = apply_rope(q, cos, sin)
        k = apply_rope(k, cos, sin)
        y_attn = blocked_gqa(q, k, v) @ w_attn_o[l]

        y_conv = conformer_conv(conv_in, w_dw[l], b_dw[l], conv_ln_g[l], conv_ln_b[l]) @ w_conv_o[l]

        y_sg = spatial_gating(sg_in, sg_ln_g[l], sg_ln_b[l], w_s[l], b_s[l]) @ w_sg_o[l]

        gates = jax.nn.sigmoid((gate_logits + b_gate[l]).reshape(B, S, N_BRANCH, D_MODEL))
        merged = gates[:, :, 0] * y_attn + gates[:, :, 1] * y_conv + gates[:, :, 2] * y_sg
        x = x + merged @ w_out[l]

        hf = rms_norm(x, g_ffn[l])
        x = x + (jax.nn.silu(hf @ w_ff_gate[l]) * (hf @ w_ff_up[l])) @ w_ff_down[l]
    return rms_norm(x, g_final)


import jax as _jax
import jax.numpy as _jnp

TWIN_FORMAT = 'train_step'
FWD_PARAMS = ['x', 'g_mix', 'w_in', 'b_gate', 'q_norm_g', 'k_norm_g', 'w_attn_o', 'w_dw', 'b_dw', 'conv_ln_g', 'conv_ln_b', 'w_conv_o', 'sg_ln_g', 'sg_ln_b', 'w_s', 'b_s', 'w_sg_o', 'w_out', 'g_ffn', 'w_ff_gate', 'w_ff_up', 'w_ff_down', 'g_final']
TWIN_WEIGHTS = ['g_mix', 'w_in', 'b_gate', 'q_norm_g', 'k_norm_g', 'w_attn_o', 'w_dw', 'b_dw', 'conv_ln_g', 'conv_ln_b', 'w_conv_o', 'sg_ln_g', 'sg_ln_b', 'w_s', 'b_s', 'w_sg_o', 'w_out', 'g_ffn', 'w_ff_gate', 'w_ff_up', 'w_ff_down', 'g_final']
TWIN_DIFF_INPUT = 'x'
TWIN_INPUTS = ['x', 'g_mix', 'w_in', 'b_gate', 'q_norm_g', 'k_norm_g', 'w_attn_o', 'w_dw', 'b_dw', 'conv_ln_g', 'conv_ln_b', 'w_conv_o', 'sg_ln_g', 'sg_ln_b', 'w_s', 'b_s', 'w_sg_o', 'w_out', 'g_ffn', 'w_ff_gate', 'w_ff_up', 'w_ff_down', 'g_final', 'loss_target', 'm_g_mix', 'm_w_in', 'm_b_gate', 'm_q_norm_g', 'm_k_norm_g', 'm_w_attn_o', 'm_w_dw', 'm_b_dw', 'm_conv_ln_g', 'm_conv_ln_b', 'm_w_conv_o', 'm_sg_ln_g', 'm_sg_ln_b', 'm_w_s', 'm_b_s', 'm_w_sg_o', 'm_w_out', 'm_g_ffn', 'm_w_ff_gate', 'm_w_ff_up', 'm_w_ff_down', 'm_g_final', 'v_g_mix', 'v_w_in', 'v_b_gate', 'v_q_norm_g', 'v_k_norm_g', 'v_w_attn_o', 'v_w_dw', 'v_b_dw', 'v_conv_ln_g', 'v_conv_ln_b', 'v_w_conv_o', 'v_sg_ln_g', 'v_sg_ln_b', 'v_w_s', 'v_b_s', 'v_w_sg_o', 'v_w_out', 'v_g_ffn', 'v_w_ff_gate', 'v_w_ff_up', 'v_w_ff_down', 'v_g_final']
TWIN_OUTPUTS = ['loss', 'grad_x', 'grad_g_mix', 'grad_w_in', 'grad_b_gate', 'grad_q_norm_g', 'grad_k_norm_g', 'grad_w_attn_o', 'grad_w_dw', 'grad_b_dw', 'grad_conv_ln_g', 'grad_conv_ln_b', 'grad_w_conv_o', 'grad_sg_ln_g', 'grad_sg_ln_b', 'grad_w_s', 'grad_b_s', 'grad_w_sg_o', 'grad_w_out', 'grad_g_ffn', 'grad_w_ff_gate', 'grad_w_ff_up', 'grad_w_ff_down', 'grad_g_final', 'delta_g_mix', 'delta_w_in', 'delta_b_gate', 'delta_q_norm_g', 'delta_k_norm_g', 'delta_w_attn_o', 'delta_w_dw', 'delta_b_dw', 'delta_conv_ln_g', 'delta_conv_ln_b', 'delta_w_conv_o', 'delta_sg_ln_g', 'delta_sg_ln_b', 'delta_w_s', 'delta_b_s', 'delta_w_sg_o', 'delta_w_out', 'delta_g_ffn', 'delta_w_ff_gate', 'delta_w_ff_up', 'delta_w_ff_down', 'delta_g_final', 'new_m_g_mix', 'new_m_w_in', 'new_m_b_gate', 'new_m_q_norm_g', 'new_m_k_norm_g', 'new_m_w_attn_o', 'new_m_w_dw', 'new_m_b_dw', 'new_m_conv_ln_g', 'new_m_conv_ln_b', 'new_m_w_conv_o', 'new_m_sg_ln_g', 'new_m_sg_ln_b', 'new_m_w_s', 'new_m_b_s', 'new_m_w_sg_o', 'new_m_w_out', 'new_m_g_ffn', 'new_m_w_ff_gate', 'new_m_w_ff_up', 'new_m_w_ff_down', 'new_m_g_final', 'new_v_g_mix', 'new_v_w_in', 'new_v_b_gate', 'new_v_q_norm_g', 'new_v_k_norm_g', 'new_v_w_attn_o', 'new_v_w_dw', 'new_v_b_dw', 'new_v_conv_ln_g', 'new_v_conv_ln_b', 'new_v_w_conv_o', 'new_v_sg_ln_g', 'new_v_sg_ln_b', 'new_v_w_s', 'new_v_b_s', 'new_v_w_sg_o', 'new_v_w_out', 'new_v_g_ffn', 'new_v_w_ff_gate', 'new_v_w_ff_up', 'new_v_w_ff_down', 'new_v_g_final']
TWIN_LEAF_KINDS = {'loss': 'loss', 'grad_x': 'grad_x', 'grad_g_mix': 'grad_w', 'grad_w_in': 'grad_w', 'grad_b_gate': 'grad_w', 'grad_q_norm_g': 'grad_w', 'grad_k_norm_g': 'grad_w', 'grad_w_attn_o': 'grad_w', 'grad_w_dw': 'grad_w', 'grad_b_dw': 'grad_w', 'grad_conv_ln_g': 'grad_w', 'grad_conv_ln_b': 'grad_w', 'grad_w_conv_o': 'grad_w', 'grad_sg_ln_g': 'grad_w', 'grad_sg_ln_b': 'grad_w', 'grad_w_s': 'grad_w', 'grad_b_s': 'grad_w', 'grad_w_sg_o': 'grad_w', 'grad_w_out': 'grad_w', 'grad_g_ffn': 'grad_w', 'grad_w_ff_gate': 'grad_w', 'grad_w_ff_up': 'grad_w', 'grad_w_ff_down': 'grad_w', 'grad_g_final': 'grad_w', 'delta_g_mix': 'delta_w', 'delta_w_in': 'delta_w', 'delta_b_gate': 'delta_w', 'delta_q_norm_g': 'delta_w', 'delta_k_norm_g': 'delta_w', 'delta_w_attn_o': 'delta_w', 'delta_w_dw': 'delta_w', 'delta_b_dw': 'delta_w', 'delta_conv_ln_g': 'delta_w', 'delta_conv_ln_b': 'delta_w', 'delta_w_conv_o': 'delta_w', 'delta_sg_ln_g': 'delta_w', 'delta_sg_ln_b': 'delta_w', 'delta_w_s': 'delta_w', 'delta_b_s': 'delta_w', 'delta_w_sg_o': 'delta_w', 'delta_w_out': 'delta_w', 'delta_g_ffn': 'delta_w', 'delta_w_ff_gate': 'delta_w', 'delta_w_ff_up': 'delta_w', 'delta_w_ff_down': 'delta_w', 'delta_g_final': 'delta_w', 'new_m_g_mix': 'new_m', 'new_m_w_in': 'new_m', 'new_m_b_gate': 'new_m', 'new_m_q_norm_g': 'new_m', 'new_m_k_norm_g': 'new_m', 'new_m_w_attn_o': 'new_m', 'new_m_w_dw': 'new_m', 'new_m_b_dw': 'new_m', 'new_m_conv_ln_g': 'new_m', 'new_m_conv_ln_b': 'new_m', 'new_m_w_conv_o': 'new_m', 'new_m_sg_ln_g': 'new_m', 'new_m_sg_ln_b': 'new_m', 'new_m_w_s': 'new_m', 'new_m_b_s': 'new_m', 'new_m_w_sg_o': 'new_m', 'new_m_w_out': 'new_m', 'new_m_g_ffn': 'new_m', 'new_m_w_ff_gate': 'new_m', 'new_m_w_ff_up': 'new_m', 'new_m_w_ff_down': 'new_m', 'new_m_g_final': 'new_m', 'new_v_g_mix': 'new_v', 'new_v_w_in': 'new_v', 'new_v_b_gate': 'new_v', 'new_v_q_norm_g': 'new_v', 'new_v_k_norm_g': 'new_v', 'new_v_w_attn_o': 'new_v', 'new_v_w_dw': 'new_v', 'new_v_b_dw': 'new_v', 'new_v_conv_ln_g': 'new_v', 'new_v_conv_ln_b': 'new_v', 'new_v_w_conv_o': 'new_v', 'new_v_sg_ln_g': 'new_v', 'new_v_sg_ln_b': 'new_v', 'new_v_w_s': 'new_v', 'new_v_b_s': 'new_v', 'new_v_w_sg_o': 'new_v', 'new_v_w_out': 'new_v', 'new_v_g_ffn': 'new_v', 'new_v_w_ff_gate': 'new_v', 'new_v_w_ff_up': 'new_v', 'new_v_w_ff_down': 'new_v', 'new_v_g_final': 'new_v'}


def _forward(args):
    return _fwd_reference(*[args[k] for k in FWD_PARAMS])


def _output_shape():
    out = _jax.eval_shape(lambda: _forward(_fwd_setup_inputs(0)))
    return out.shape, out.dtype

N_MICROBATCH = 1
ADAM_LR = 0.001
ADAM_B1 = 0.9
ADAM_B2 = 0.999
ADAM_EPS = 1e-08
ADAM_WD = 0.01
ADAM_STEP = 10
PER_EXAMPLE_BATCH_AXIS = {'x': 0, 'loss_target': 0}
SHARED_INPUTS = []
_WEIGHT_DTYPES = {'g_mix': _jnp.float32, 'w_in': _jnp.float32, 'b_gate': _jnp.float32, 'q_norm_g': _jnp.float32, 'k_norm_g': _jnp.float32, 'w_attn_o': _jnp.float32, 'w_dw': _jnp.float32, 'b_dw': _jnp.float32, 'conv_ln_g': _jnp.float32, 'conv_ln_b': _jnp.float32, 'w_conv_o': _jnp.float32, 'sg_ln_g': _jnp.float32, 'sg_ln_b': _jnp.float32, 'w_s': _jnp.float32, 'b_s': _jnp.float32, 'w_sg_o': _jnp.float32, 'w_out': _jnp.float32, 'g_ffn': _jnp.float32, 'w_ff_gate': _jnp.float32, 'w_ff_up': _jnp.float32, 'w_ff_down': _jnp.float32, 'g_final': _jnp.float32}
MOMENT_SCALE = {'g_mix': 4.629161e-02, 'w_in': 1.919311e-02, 'b_gate': 7.943405e-03, 'q_norm_g': 1.268859e-02, 'k_norm_g': 1.244125e-02, 'w_attn_o': 4.039145e-03, 'w_dw': 2.752938e-02, 'b_dw': 5.338640e-02, 'conv_ln_g': 3.383898e-02, 'conv_ln_b': 2.910756e-02, 'w_conv_o': 1.920514e-02, 'sg_ln_g': 2.932361e-02, 'sg_ln_b': 2.956564e-02, 'w_s': 2.908119e-02, 'b_s': 2.942092e-02, 'w_sg_o': 2.930981e-02, 'w_out': 3.503138e-02, 'g_ffn': 4.070514e-02, 'w_ff_gate': 1.782972e-02, 'w_ff_up': 1.728935e-02, 'w_ff_down': 2.866135e-02, 'g_final': 8.001806e+00}


def _to_microbatches(a, axis):
    t = _jnp.moveaxis(a, axis, 0)
    t = t.reshape((N_MICROBATCH, t.shape[0] // N_MICROBATCH) + t.shape[1:])
    return _jnp.moveaxis(t, 1, axis + 1)


def setup_inputs(seed: int = 0) -> dict:
    inp = _fwd_setup_inputs(seed)
    key = _jax.random.fold_in(_jax.random.key(seed), 7919)
    shape, _ = _output_shape()
    out = dict(inp)
    out["loss_target"] = _jax.random.normal(_jax.random.fold_in(key, 0), shape, _jnp.float32)
    for i, name in enumerate(TWIN_WEIGHTS):
        w = inp[name].astype(_jnp.float32)
        if MOMENT_SCALE is None:
            s = _jnp.sqrt(_jnp.mean(_jnp.square(w)) + 1e-30)
        else:
            s = MOMENT_SCALE[name]
        km, kv = _jax.random.split(_jax.random.fold_in(key, i + 1))
        out[name] = w
        out["m_" + name] = s * _jax.random.normal(km, w.shape, _jnp.float32)
        out["v_" + name] = (s * s) * _jax.random.uniform(kv, w.shape, _jnp.float32, 0.5, 1.5)
    if N_MICROBATCH > 1:
        for name, axis in PER_EXAMPLE_BATCH_AXIS.items():
            out[name] = _to_microbatches(out[name], axis)
    return {'x': out['x'], 'g_mix': out['g_mix'], 'w_in': out['w_in'], 'b_gate': out['b_gate'], 'q_norm_g': out['q_norm_g'], 'k_norm_g': out['k_norm_g'], 'w_attn_o': out['w_attn_o'], 'w_dw': out['w_dw'], 'b_dw': out['b_dw'], 'conv_ln_g': out['conv_ln_g'], 'conv_ln_b': out['conv_ln_b'], 'w_conv_o': out['w_conv_o'], 'sg_ln_g': out['sg_ln_g'], 'sg_ln_b': out['sg_ln_b'], 'w_s': out['w_s'], 'b_s': out['b_s'], 'w_sg_o': out['w_sg_o'], 'w_out': out['w_out'], 'g_ffn': out['g_ffn'], 'w_ff_gate': out['w_ff_gate'], 'w_ff_up': out['w_ff_up'], 'w_ff_down': out['w_ff_down'], 'g_final': out['g_final'], 'loss_target': out['loss_target'], 'm_g_mix': out['m_g_mix'], 'm_w_in': out['m_w_in'], 'm_b_gate': out['m_b_gate'], 'm_q_norm_g': out['m_q_norm_g'], 'm_k_norm_g': out['m_k_norm_g'], 'm_w_attn_o': out['m_w_attn_o'], 'm_w_dw': out['m_w_dw'], 'm_b_dw': out['m_b_dw'], 'm_conv_ln_g': out['m_conv_ln_g'], 'm_conv_ln_b': out['m_conv_ln_b'], 'm_w_conv_o': out['m_w_conv_o'], 'm_sg_ln_g': out['m_sg_ln_g'], 'm_sg_ln_b': out['m_sg_ln_b'], 'm_w_s': out['m_w_s'], 'm_b_s': out['m_b_s'], 'm_w_sg_o': out['m_w_sg_o'], 'm_w_out': out['m_w_out'], 'm_g_ffn': out['m_g_ffn'], 'm_w_ff_gate': out['m_w_ff_gate'], 'm_w_ff_up': out['m_w_ff_up'], 'm_w_ff_down': out['m_w_ff_down'], 'm_g_final': out['m_g_final'], 'v_g_mix': out['v_g_mix'], 'v_w_in': out['v_w_in'], 'v_b_gate': out['v_b_gate'], 'v_q_norm_g': out['v_q_norm_g'], 'v_k_norm_g': out['v_k_norm_g'], 'v_w_attn_o': out['v_w_attn_o'], 'v_w_dw': out['v_w_dw'], 'v_b_dw': out['v_b_dw'], 'v_conv_ln_g': out['v_conv_ln_g'], 'v_conv_ln_b': out['v_conv_ln_b'], 'v_w_conv_o': out['v_w_conv_o'], 'v_sg_ln_g': out['v_sg_ln_g'], 'v_sg_ln_b': out['v_sg_ln_b'], 'v_w_s': out['v_w_s'], 'v_b_s': out['v_b_s'], 'v_w_sg_o': out['v_w_sg_o'], 'v_w_out': out['v_w_out'], 'v_g_ffn': out['v_g_ffn'], 'v_w_ff_gate': out['v_w_ff_gate'], 'v_w_ff_up': out['v_w_ff_up'], 'v_w_ff_down': out['v_w_ff_down'], 'v_g_final': out['v_g_final']}


def _loss(weights, diff, rest, loss_target):
    with _jax.named_scope("forward"):
        args = {**rest, TWIN_DIFF_INPUT: diff, **{k: w.astype(_WEIGHT_DTYPES[k]) for k, w in weights.items()}}
        y = _forward(args)
    with _jax.named_scope("loss_head"):
        err = _jnp.square(y.astype(_jnp.float32) - loss_target)
        return 0.5 * _jnp.sum(_jnp.mean(err, axis=-1)) if err.ndim else 0.5 * err


def _adamw(w, g, m, v):
    m = ADAM_B1 * m + (1.0 - ADAM_B1) * g
    v = ADAM_B2 * v + (1.0 - ADAM_B2) * _jnp.square(g)
    m_hat = m / (1.0 - ADAM_B1 ** ADAM_STEP)
    v_hat = v / (1.0 - ADAM_B2 ** ADAM_STEP)
    delta = -ADAM_LR * (m_hat / (_jnp.sqrt(v_hat) + ADAM_EPS) + ADAM_WD * w)
    return delta, m, v


def reference(x, g_mix, w_in, b_gate, q_norm_g, k_norm_g, w_attn_o, w_dw, b_dw, conv_ln_g, conv_ln_b, w_conv_o, sg_ln_g, sg_ln_b, w_s, b_s, w_sg_o, w_out, g_ffn, w_ff_gate, w_ff_up, w_ff_down, g_final, loss_target, m_g_mix, m_w_in, m_b_gate, m_q_norm_g, m_k_norm_g, m_w_attn_o, m_w_dw, m_b_dw, m_conv_ln_g, m_conv_ln_b, m_w_conv_o, m_sg_ln_g, m_sg_ln_b, m_w_s, m_b_s, m_w_sg_o, m_w_out, m_g_ffn, m_w_ff_gate, m_w_ff_up, m_w_ff_down, m_g_final, v_g_mix, v_w_in, v_b_gate, v_q_norm_g, v_k_norm_g, v_w_attn_o, v_w_dw, v_b_dw, v_conv_ln_g, v_conv_ln_b, v_w_conv_o, v_sg_ln_g, v_sg_ln_b, v_w_s, v_b_s, v_w_sg_o, v_w_out, v_g_ffn, v_w_ff_gate, v_w_ff_up, v_w_ff_down, v_g_final):
    given = dict(x=x, g_mix=g_mix, w_in=w_in, b_gate=b_gate, q_norm_g=q_norm_g, k_norm_g=k_norm_g, w_attn_o=w_attn_o, w_dw=w_dw, b_dw=b_dw, conv_ln_g=conv_ln_g, conv_ln_b=conv_ln_b, w_conv_o=w_conv_o, sg_ln_g=sg_ln_g, sg_ln_b=sg_ln_b, w_s=w_s, b_s=b_s, w_sg_o=w_sg_o, w_out=w_out, g_ffn=g_ffn, w_ff_gate=w_ff_gate, w_ff_up=w_ff_up, w_ff_down=w_ff_down, g_final=g_final, loss_target=loss_target, m_g_mix=m_g_mix, m_w_in=m_w_in, m_b_gate=m_b_gate, m_q_norm_g=m_q_norm_g, m_k_norm_g=m_k_norm_g, m_w_attn_o=m_w_attn_o, m_w_dw=m_w_dw, m_b_dw=m_b_dw, m_conv_ln_g=m_conv_ln_g, m_conv_ln_b=m_conv_ln_b, m_w_conv_o=m_w_conv_o, m_sg_ln_g=m_sg_ln_g, m_sg_ln_b=m_sg_ln_b, m_w_s=m_w_s, m_b_s=m_b_s, m_w_sg_o=m_w_sg_o, m_w_out=m_w_out, m_g_ffn=m_g_ffn, m_w_ff_gate=m_w_ff_gate, m_w_ff_up=m_w_ff_up, m_w_ff_down=m_w_ff_down, m_g_final=m_g_final, v_g_mix=v_g_mix, v_w_in=v_w_in, v_b_gate=v_b_gate, v_q_norm_g=v_q_norm_g, v_k_norm_g=v_k_norm_g, v_w_attn_o=v_w_attn_o, v_w_dw=v_w_dw, v_b_dw=v_b_dw, v_conv_ln_g=v_conv_ln_g, v_conv_ln_b=v_conv_ln_b, v_w_conv_o=v_w_conv_o, v_sg_ln_g=v_sg_ln_g, v_sg_ln_b=v_sg_ln_b, v_w_s=v_w_s, v_b_s=v_b_s, v_w_sg_o=v_w_sg_o, v_w_out=v_w_out, v_g_ffn=v_g_ffn, v_w_ff_gate=v_w_ff_gate, v_w_ff_up=v_w_ff_up, v_w_ff_down=v_w_ff_down, v_g_final=v_g_final)
    weights = {n: given[n] for n in TWIN_WEIGHTS}
    shared = {n: given[n] for n in SHARED_INPUTS}
    per_example = {n: given[n] for n in ['x']}
    grad_fn = _jax.value_and_grad(_loss, argnums=(0, 1))

    def one_microbatch(ex, loss_target):
        ex = dict(ex)
        diff = ex.pop(TWIN_DIFF_INPUT)
        return grad_fn(weights, diff, {**shared, **ex}, loss_target)

    if N_MICROBATCH == 1:
        loss, (grad_w, grad_x) = one_microbatch(per_example, given["loss_target"])
    else:
        def body(carry, xs):
            loss_sum, grad_sum = carry
            l_k, (gw_k, gx_k) = one_microbatch(xs[0], xs[1])
            with _jax.named_scope("update"):
                return (loss_sum + l_k, _jax.tree.map(_jnp.add, grad_sum, gw_k)), gx_k

        init = (_jnp.zeros((), _jnp.float32), _jax.tree.map(_jnp.zeros_like, weights))
        (loss, grad_w), grad_x = _jax.lax.scan(body, init, (per_example, given["loss_target"]))
    with _jax.named_scope("update"):
        delta_w, new_m, new_v = {}, {}, {}
        for n in TWIN_WEIGHTS:
            delta_w[n], new_m[n], new_v[n] = _adamw(weights[n], grad_w[n], given["m_" + n], given["v_" + n])
    return (loss, grad_x, *[grad_w[n] for n in TWIN_WEIGHTS], *[delta_w[n] for n in TWIN_WEIGHTS],
            *[new_m[n] for n in TWIN_WEIGHTS], *[new_v[n] for n in TWIN_WEIGHTS])
```

```python
import functools
import math

import jax
import jax.numpy as jnp
from jax import lax
from jax.experimental import pallas as pl
from jax.experimental.pallas import tpu as pltpu

f32, bf16 = jnp.float32, jnp.bfloat16

D_MODEL = 2048
SEQ = 2048
DEPTH = 2
GRID_W = 64
HEAD_DIM = 128
LANES = 128
N_Q = (D_MODEL // 2) // HEAD_DIM
N_KV = N_Q // 4
GRP = N_Q // N_KV
Q_COLS = N_Q * HEAD_DIM
KV_COLS = N_KV * HEAD_DIM
CONV_CH = D_MODEL // 2
CONV_W = 31
CONV_PAD = CONV_W // 2
CONV_WP = 32
SG_CH = D_MODEL // 2
SG_G = SG_CH // LANES
SG_CHUNK = 128
D_FF = -(-8 * D_MODEL // (3 * 256)) * 256
OFF_KV = Q_COLS
OFF_CONV = OFF_KV + 2 * KV_COLS
OFF_SG = OFF_CONV + 2 * CONV_CH
OFF_GATE = OFF_SG + 2 * SG_CH
IN_COLS = OFF_GATE + 3 * D_MODEL
ROPE_THETA = 10000.0
SCALE = HEAD_DIM ** -0.5
N_DEV = 8
N_CHIP = 4

ADAM_LR, ADAM_B1, ADAM_B2, ADAM_EPS, ADAM_WD, ADAM_STEP = 0.001, 0.9, 0.999, 1e-08, 0.01, 10

VMEM_BYTES_V7X = 64 << 20
VMEM_CAP = VMEM_BYTES_V7X - (6 << 20)
MESH = pl.DeviceIdType.MESH
HBM = pl.BlockSpec(memory_space=pltpu.HBM)


def _call(body, **kw):
    return pl.pallas_call(body, **kw)


def _pick(n, cands):
    for c in cands:
        if n % c == 0:
            return c
    raise ValueError((n, cands))


def _params(sem, vmem_bytes):
    return pltpu.CompilerParams(dimension_semantics=sem, vmem_limit_bytes=int(min(max(vmem_bytes, 16 << 20), VMEM_CAP)))


def _mm(a, b, form, out_dtype, *, n=None, b_off=0, res=None, name):
    if form == "tn":
        K, M = a.shape
    else:
        M, K = a.shape
    N = n if n is not None else (b.shape[0] if form == "nt" else b.shape[1])
    if K <= 2048:
        tk = K
        if form == "tn":
            tm = _pick(M, (512, 256, 128))
            tn = N if N <= 2048 else _pick(N, (1024, 512, 256, 128))
        else:
            tm = M if M <= 2048 else _pick(M, (2048, 1024, 512))
            tn = _pick(math.gcd(N, b_off) if b_off else N, (256, 128) if res is not None else (512, 256, 128))
    else:
        tk = _pick(K, (512, 256, 128))
        tm = _pick(M, (1024, 512, 256, 128))
        tn = _pick(math.gcd(N, b_off) if b_off else N, (1024, 512, 256, 128))
    assert b_off % tn == 0
    off = b_off // tn
    nk = K // tk
    if form == "tn":
        a_spec = pl.BlockSpec((tk, tm), lambda i, j, k: (k, i))
    else:
        a_spec = pl.BlockSpec((tm, tk), lambda i, j, k: (i, k))
    if form == "nt":
        b_spec = pl.BlockSpec((tn, tk), lambda i, j, k: (j + off, k))
    else:
        b_spec = pl.BlockSpec((tk, tn), lambda i, j, k: (k, j + off))
    dims = {"nn": ((1,), (0,)), "nt": ((1,), (1,)), "tn": ((0,), (0,))}[form]
    has_res = res is not None

    def body(*refs):
        if has_res:
            a_ref, b_ref, r_ref, o_ref = refs[:4]
        else:
            a_ref, b_ref, o_ref = refs[:3]
        p = lax.dot_general(a_ref[...], b_ref[...], (dims, ((), ())), preferred_element_type=f32)

        def finish(acc):
            if has_res:
                acc = acc + r_ref[...].astype(f32)
            o_ref[...] = acc.astype(o_ref.dtype)

        if nk == 1:
            finish(p)
        else:
            acc_ref = refs[-1]
            k = pl.program_id(2)

            @pl.when(k == 0)
            def _():
                acc_ref[...] = p

            @pl.when(k > 0)
            def _():
                acc_ref[...] += p

            @pl.when(k == nk - 1)
            def _():
                finish(acc_ref[...])

    in_specs = [a_spec, b_spec]
    args = [a, b]
    osz = jnp.dtype(out_dtype).itemsize
    vmem = 2 * (tm * tk * 2 + tk * tn * 2 + tm * tn * osz) + 2 * tm * tn * 4
    if has_res:
        in_specs.append(pl.BlockSpec((tm, tn), lambda i, j, k: (i, j)))
        args.append(res)
        vmem += 2 * tm * tn * res.dtype.itemsize
    scratch = []
    if nk > 1:
        scratch.append(pltpu.VMEM((tm, tn), f32))
        vmem += tm * tn * 4
    return _call(
        body, name=name, grid=(M // tm, N // tn, nk),
        in_specs=in_specs, out_specs=pl.BlockSpec((tm, tn), lambda i, j, k: (i, j)),
        out_shape=jax.ShapeDtypeStruct((M, N), out_dtype), scratch_shapes=scratch,
        compiler_params=_params(("parallel", "parallel", "arbitrary"), vmem + (8 << 20)),
    )(*args)


def _rows(body, ins, outs, *, tm, name, vmem=40 << 20):
    nrows = next(s[1].shape[0] for s in ins if s[0] == "r")
    in_specs, args = [], []
    for s in ins:
        arr = s[1]
        if s[0] == "r":
            w = s[2] if len(s) > 2 else arr.shape[1]
            cb = s[3] if len(s) > 3 else 0
            in_specs.append(pl.BlockSpec((tm, w), functools.partial(lambda i, cb: (i, cb), cb=cb)))
        else:
            in_specs.append(pl.BlockSpec(arr.shape, functools.partial(lambda i, nd: (0,) * nd, nd=arr.ndim)))
        args.append(arr)
    out_specs, out_shape = [], []
    for s in outs:
        if s[0] == "r":
            out_specs.append(pl.BlockSpec((tm, s[1]), lambda i: (i, 0)))
            out_shape.append(jax.ShapeDtypeStruct((nrows, s[1]), s[2]))
        else:
            out_specs.append(pl.BlockSpec(s[1], functools.partial(lambda i, nd: (0,) * nd, nd=len(s[1]))))
            out_shape.append(jax.ShapeDtypeStruct(s[1], s[2]))
    return _call(body, name=name, grid=(nrows // tm,), in_specs=in_specs, out_specs=out_specs,
                 out_shape=out_shape, compiler_params=_params(("arbitrary",), vmem))(*args)


def _accumulate(ref, part):
    i = pl.program_id(0)

    @pl.when(i == 0)
    def _():
        ref[...] = part

    @pl.when(i > 0)
    def _():
        ref[...] += part


def _rms_stats(x):
    r = lax.rsqrt(jnp.mean(x * x, axis=-1, keepdims=True) + 1e-6)
    return r, x * r


def _rms_fwd(x, g, name):
    def body(x_ref, g_ref, o_ref):
        _, xn = _rms_stats(x_ref[...])
        o_ref[...] = (xn * g_ref[...]).astype(o_ref.dtype)

    return _rows(body, [("r", x), ("f", g)], [("r", x.shape[1], bf16)], tm=min(256, x.shape[0]), name=name)[0]


def _rms_bwd(x, g, dh, dres, name):
    D = x.shape[1]

    def body(x_ref, g_ref, dh_ref, dr_ref, dx_ref, dxb_ref, dg_ref):
        r, xn = _rms_stats(x_ref[...])
        dy = dh_ref[...].astype(f32)
        dxn = dy * g_ref[...]
        dx = dr_ref[...] + r * (dxn - xn * jnp.mean(dxn * xn, axis=-1, keepdims=True))
        dx_ref[...] = dx
        dxb_ref[...] = dx.astype(bf16)
        _accumulate(dg_ref, jnp.sum(dy * xn, axis=0, keepdims=True))

    return _rows(body, [("r", x), ("f", g), ("r", dh), ("r", dres)],
                 [("r", D, f32), ("r", D, bf16), ("a", (1, D), f32)], tm=min(256, x.shape[0]), name=name)


def _final_loss(x, g, tgt, name):
    D = x.shape[1]

    def body(x_ref, g_ref, t_ref, dx_ref, dxb_ref, sq_ref, dg_ref):
        r, xn = _rms_stats(x_ref[...])
        gain = g_ref[...]
        diff = xn * gain - t_ref[...]
        dy = diff * (1.0 / D)
        dxn = dy * gain
        dx = r * (dxn - xn * jnp.mean(dxn * xn, axis=-1, keepdims=True))
        dx_ref[...] = dx
        dxb_ref[...] = dx.astype(bf16)
        _accumulate(sq_ref, jnp.sum(diff * diff, axis=0, keepdims=True))
        _accumulate(dg_ref, jnp.sum(dy * xn, axis=0, keepdims=True))

    return _rows(body, [("r", x), ("f", g), ("r", tgt)],
                 [("r", D, f32), ("r", D, bf16), ("a", (1, D), f32), ("a", (1, D), f32)],
                 tm=min(256, x.shape[0]), name=name)


def _qk_fwd(q_raw, kv_raw, qg, kg, cos2, sin2, name):
    def body(q_ref, k_ref, qg_ref, kg_ref, c_ref, s_ref, qo_ref, ko_ref):
        c, s = c_ref[...], s_ref[...]

        def head(src, gain, dst, h):
            cols = slice(h * HEAD_DIM, (h + 1) * HEAD_DIM)
            _, xn = _rms_stats(src[:, cols].astype(f32))
            y = xn * gain
            dst[:, cols] = (y * c + pltpu.roll(y, HEAD_DIM // 2, 1) * s).astype(dst.dtype)

        for h in range(N_Q):
            head(q_ref, qg_ref[...], qo_ref, h)
        for h in range(N_KV):
            head(k_ref, kg_ref[...], ko_ref, h)

    return _rows(body, [("r", q_raw), ("r", kv_raw, KV_COLS, 0), ("f", qg), ("f", kg), ("r", cos2), ("r", sin2)],
                 [("r", Q_COLS, bf16), ("r", KV_COLS, bf16)], tm=min(256, q_raw.shape[0]), name=name)


def _qk_bwd(q_raw, kv_raw, dqr, dkr, qg, kg, cos2, sin2, name):
    def body(q_ref, k_ref, dq_ref, dk_ref, qg_ref, kg_ref, c_ref, s_ref, dqo_ref, dko_ref, dqg_ref, dkg_ref):
        c, s = c_ref[...], s_ref[...]

        def head(src, dsrc, gain, dst, h):
            cols = slice(h * HEAD_DIM, (h + 1) * HEAD_DIM)
            r, xn = _rms_stats(src[:, cols].astype(f32))
            do = dsrc[:, cols].astype(f32)
            dy = do * c + pltpu.roll(do * s, HEAD_DIM // 2, 1)
            dxn = dy * gain
            dst[:, cols] = (r * (dxn - xn * jnp.mean(dxn * xn, axis=-1, keepdims=True))).astype(dst.dtype)
            return jnp.sum(dy * xn, axis=0, keepdims=True)

        dq_gain = head(q_ref, dq_ref, qg_ref[...], dqo_ref, 0)
        for h in range(1, N_Q):
            dq_gain = dq_gain + head(q_ref, dq_ref, qg_ref[...], dqo_ref, h)
        dk_gain = head(k_ref, dk_ref, kg_ref[...], dko_ref, 0)
        for h in range(1, N_KV):
            dk_gain = dk_gain + head(k_ref, dk_ref, kg_ref[...], dko_ref, h)
        _accumulate(dqg_ref, dq_gain)
        _accumulate(dkg_ref, dk_gain)

    return _rows(body, [("r", q_raw), ("r", kv_raw, KV_COLS, 0), ("r", dqr), ("r", dkr), ("f", qg), ("f", kg),
                        ("r", cos2), ("r", sin2)],
                 [("r", Q_COLS, bf16), ("r", KV_COLS, bf16), ("a", (1, HEAD_DIM), f32), ("a", (1, HEAD_DIM), f32)],
                 tm=min(256, q_raw.shape[0]), name=name)


def _softmax_rows(q, k):
    s = lax.dot_general(q, k, (((1,), (1,)), ((), ())), preferred_element_type=f32) * SCALE
    p = jnp.exp(s - jnp.max(s, axis=-1, keepdims=True))
    return p * (1.0 / jnp.sum(p, axis=-1, keepdims=True))


def _attn_fwd(qr, kr, kv_raw, name):
    S = qr.shape[0]
    tq = min(512, S)

    def body(q_ref, k_ref, v_ref, o_ref):
        p = _softmax_rows(q_ref[...], k_ref[...])
        o_ref[...] = jnp.dot(p.astype(bf16), v_ref[...], preferred_element_type=f32).astype(o_ref.dtype)

    return _call(
        body, name=name, grid=(N_Q, S // tq),
        in_specs=[pl.BlockSpec((tq, HEAD_DIM), lambda h, i: (i, h)),
                  pl.BlockSpec((S, HEAD_DIM), lambda h, i: (0, h // GRP)),
                  pl.BlockSpec((S, HEAD_DIM), lambda h, i: (0, N_KV + h // GRP))],
        out_specs=pl.BlockSpec((tq, HEAD_DIM), lambda h, i: (i, h)),
        out_shape=jax.ShapeDtypeStruct((S, Q_COLS), bf16),
        compiler_params=_params(("parallel", "arbitrary"), 6 * tq * S * 4 + (8 << 20)),
    )(qr, kr, kv_raw)


def _attn_bwd(qr, kr, kv_raw, do, name):
    S = qr.shape[0]
    tq = min(256, S)

    def body(q_ref, k_ref, v_ref, do_ref, dq_ref, dk_ref, dv_ref):
        first = jnp.logical_and(pl.program_id(1) == 0, pl.program_id(2) == 0)
        q, k, v, do_ = q_ref[...], k_ref[...], v_ref[...], do_ref[...]
        p = _softmax_rows(q, k)
        dp = lax.dot_general(do_, v, (((1,), (1,)), ((), ())), preferred_element_type=f32)
        ds = (p * (dp - jnp.sum(dp * p, axis=-1, keepdims=True)) * SCALE).astype(bf16)
        dq_ref[...] = jnp.dot(ds, k, preferred_element_type=f32).astype(dq_ref.dtype)
        dv_part = lax.dot_general(p.astype(bf16), do_, (((0,), (0,)), ((), ())), preferred_element_type=f32)
        dk_part = lax.dot_general(ds, q, (((0,), (0,)), ((), ())), preferred_element_type=f32)

        @pl.when(first)
        def _():
            dv_ref[...] = dv_part
            dk_ref[...] = dk_part

        @pl.when(jnp.logical_not(first))
        def _():
            dv_ref[...] += dv_part
            dk_ref[...] += dk_part

    qmap = lambda kv, g, i: (i, kv * GRP + g)
    return _call(
        body, name=name, grid=(N_KV, GRP, S // tq),
        in_specs=[pl.BlockSpec((tq, HEAD_DIM), qmap),
                  pl.BlockSpec((S, HEAD_DIM), lambda kv, g, i: (0, kv)),
                  pl.BlockSpec((S, HEAD_DIM), lambda kv, g, i: (0, N_KV + kv)),
                  pl.BlockSpec((tq, HEAD_DIM), qmap)],
        out_specs=[pl.BlockSpec((tq, HEAD_DIM), qmap),
                   pl.BlockSpec((S, HEAD_DIM), lambda kv, g, i: (0, kv)),
                   pl.BlockSpec((S, HEAD_DIM), lambda kv, g, i: (0, kv))],
        out_shape=[jax.ShapeDtypeStruct((S, Q_COLS), bf16), jax.ShapeDtypeStruct((S, KV_COLS), f32),
                   jax.ShapeDtypeStruct((S, KV_COLS), f32)],
        compiler_params=_params(("parallel", "arbitrary", "arbitrary"), 8 * tq * S * 4 + (8 << 20)),
    )(qr, kr, kv_raw, do)


CONV_HALO = 16


def _fill_padded(pad_ref, val, S):
    pad_ref[pl.ds(0, CONV_HALO), :] = jnp.zeros((CONV_HALO, LANES), f32)
    pad_ref[pl.ds(CONV_HALO + S, CONV_HALO), :] = jnp.zeros((CONV_HALO, LANES), f32)
    pad_ref[pl.ds(CONV_HALO, S), :] = val


def _group_specs(S, n_groups, second_half):
    return pl.BlockSpec((S, LANES), functools.partial(lambda g, o: (0, g + o), o=n_groups if second_half else 0))


def _conv1_fwd(conv_in, wdw, b_dw, name):
    S = conv_in.shape[0]
    ng = CONV_CH // LANES
    R = min(256, S)

    def body(a_ref, g_ref, w_ref, b_ref, o_ref, pad_ref):
        z = a_ref[...].astype(f32) * jax.nn.sigmoid(g_ref[...].astype(f32))
        _fill_padded(pad_ref, z, S)
        for r in range(S // R):
            acc = jnp.zeros((R, LANES), f32) + b_ref[...]
            for j in range(CONV_W):
                acc = acc + w_ref[pl.ds(j, 1), :] * pad_ref[pl.ds(r * R + CONV_HALO - CONV_PAD + j, R), :]
            o_ref[pl.ds(r * R, R), :] = acc

    return _call(
        body, name=name, grid=(ng,),
        in_specs=[_group_specs(S, ng, False), _group_specs(S, ng, True),
                  pl.BlockSpec((CONV_WP, LANES), lambda g: (g, 0)), pl.BlockSpec((1, LANES), lambda g: (0, g))],
        out_specs=pl.BlockSpec((S, LANES), lambda g: (0, g)),
        out_shape=jax.ShapeDtypeStruct((S, CONV_CH), f32),
        scratch_shapes=[pltpu.VMEM((S + 2 * CONV_HALO, LANES), f32)],
        compiler_params=_params(("parallel",), 24 << 20),
    )(conv_in, conv_in, wdw, b_dw)


def _conv1_bwd(conv_in, dc, wdw, name):
    S = conv_in.shape[0]
    ng = CONV_CH // LANES
    R = min(256, S)

    def body(a_ref, g_ref, w_ref, dc_ref, da_ref, dg_ref, dw_ref, db_ref, padz_ref, padd_ref):
        a = a_ref[...].astype(f32)
        sg = jax.nn.sigmoid(g_ref[...].astype(f32))
        _fill_padded(padz_ref, a * sg, S)
        _fill_padded(padd_ref, dc_ref[...], S)
        for r in range(S // R):
            dz = jnp.zeros((R, LANES), f32)
            for j in range(CONV_W):
                dz = dz + w_ref[pl.ds(j, 1), :] * padd_ref[pl.ds(r * R + CONV_HALO + CONV_PAD - j, R), :]
            rows = pl.ds(r * R, R)
            ar, sr = a_ref[rows, :].astype(f32), jax.nn.sigmoid(g_ref[rows, :].astype(f32))
            da_ref[rows, :] = (dz * sr).astype(da_ref.dtype)
            dg_ref[rows, :] = (dz * ar * sr * (1.0 - sr)).astype(dg_ref.dtype)
        for j in range(CONV_W):
            tot = jnp.zeros((1, LANES), f32)
            for r in range(S // R):
                tot = tot + jnp.sum(dc_ref[pl.ds(r * R, R), :] * padz_ref[pl.ds(r * R + CONV_HALO - CONV_PAD + j, R), :],
                                    axis=0, keepdims=True)
            dw_ref[pl.ds(j, 1), :] = tot
        dw_ref[pl.ds(CONV_W, CONV_WP - CONV_W), :] = jnp.zeros((CONV_WP - CONV_W, LANES), f32)
        db_ref[...] = jnp.sum(dc_ref[...], axis=0, keepdims=True)

    return _call(
        body, name=name, grid=(ng,),
        in_specs=[_group_specs(S, ng, False), _group_specs(S, ng, True),
                  pl.BlockSpec((CONV_WP, LANES), lambda g: (g, 0)), pl.BlockSpec((S, LANES), lambda g: (0, g))],
        out_specs=[pl.BlockSpec((S, LANES), lambda g: (0, g)), pl.BlockSpec((S, LANES), lambda g: (0, g)),
                   pl.BlockSpec((CONV_WP, LANES), lambda g: (g, 0)), pl.BlockSpec((1, LANES), lambda g: (0, g))],
        out_shape=[jax.ShapeDtypeStruct((S, CONV_CH), bf16), jax.ShapeDtypeStruct((S, CONV_CH), bf16),
                   jax.ShapeDtypeStruct((ng * CONV_WP, LANES), f32), jax.ShapeDtypeStruct((1, CONV_CH), f32)],
        scratch_shapes=[pltpu.VMEM((S + 2 * CONV_HALO, LANES), f32), pltpu.VMEM((S + 2 * CONV_HALO, LANES), f32)],
        compiler_params=_params(("parallel",), 24 << 20),
    )(conv_in, conv_in, wdw, dc)


def _ln_stats(x, eps=1e-5):
    xc = x - jnp.mean(x, axis=-1, keepdims=True)
    r = lax.rsqrt(jnp.mean(xc * xc, axis=-1, keepdims=True) + eps)
    return r, xc * r


def _ln_bwd(r, xh, dxh):
    return r * (dxh - jnp.mean(dxh, axis=-1, keepdims=True) - xh * jnp.mean(dxh * xh, axis=-1, keepdims=True))


def _conv2_fwd(c, ln_g, ln_b, name):
    def body(c_ref, g_ref, b_ref, o_ref):
        _, xh = _ln_stats(c_ref[...])
        y = xh * g_ref[...] + b_ref[...]
        o_ref[...] = (y * jax.nn.sigmoid(y)).astype(o_ref.dtype)

    return _rows(body, [("r", c), ("f", ln_g), ("f", ln_b)], [("r", CONV_CH, bf16)], tm=min(256, c.shape[0]), name=name)[0]


def _conv2_bwd(c, dcz, ln_g, ln_b, name):
    def body(c_ref, d_ref, g_ref, b_ref, dc_ref, dg_ref, db_ref):
        r, xh = _ln_stats(c_ref[...])
        y = xh * g_ref[...] + b_ref[...]
        sg = jax.nn.sigmoid(y)
        dy = d_ref[...].astype(f32) * (sg * (1.0 + y * (1.0 - sg)))
        dc_ref[...] = _ln_bwd(r, xh, dy * g_ref[...])
        _accumulate(dg_ref, jnp.sum(dy * xh, axis=0, keepdims=True))
        _accumulate(db_ref, jnp.sum(dy, axis=0, keepdims=True))

    return _rows(body, [("r", c), ("r", dcz), ("f", ln_g), ("f", ln_b)],
                 [("r", CONV_CH, f32), ("a", (1, CONV_CH), f32), ("a", (1, CONV_CH), f32)],
                 tm=min(256, c.shape[0]), name=name)


GELU_K = math.sqrt(2.0 / math.pi)
GELU_C = 0.044715


def _gelu(x):
    return 0.5 * x * (1.0 + jnp.tanh(GELU_K * (x + GELU_C * x * x * x)))


def _gelu_grad(x):
    th = jnp.tanh(GELU_K * (x + GELU_C * x * x * x))
    return 0.5 * (1.0 + th) + 0.5 * x * (1.0 - th * th) * (GELU_K * (1.0 + 3.0 * GELU_C * x * x))


def _chunk_rows(n):
    return pl.ds(pl.multiple_of(n * SG_CHUNK, SG_CHUNK), SG_CHUNK)


def _sgu_fwd(sg_in, ln_g, ln_b, w_s, b_s, name):
    S = sg_in.shape[0]

    def body(u_ref, v_ref, lg_ref, lb_ref, w_ref, b_ref, o_ref):
        wb = w_ref[...].astype(bf16)

        def chunk(n, carry):
            rows = _chunk_rows(n)
            gu = _gelu(u_ref[rows, :].astype(f32))
            _, xh = _ln_stats(_gelu(v_ref[rows, :].astype(f32)))
            vl = xh * lg_ref[...] + lb_ref[...]
            t = jnp.dot(wb, vl.astype(bf16), preferred_element_type=f32) + b_ref[...]
            o_ref[rows, :] = (gu * t).astype(o_ref.dtype)
            return carry

        lax.fori_loop(0, S // SG_CHUNK, chunk, 0)

    return _call(
        body, name=name, grid=(SG_G,),
        in_specs=[_group_specs(S, SG_G, False), _group_specs(S, SG_G, True),
                  pl.BlockSpec((1, LANES), lambda g: (0, g)), pl.BlockSpec((1, LANES), lambda g: (0, g)),
                  pl.BlockSpec((None, SG_CHUNK, SG_CHUNK), lambda g: (g, 0, 0)),
                  pl.BlockSpec((None, SG_CHUNK, 1), lambda g: (g, 0, 0))],
        out_specs=pl.BlockSpec((S, LANES), lambda g: (0, g)),
        out_shape=jax.ShapeDtypeStruct((S, SG_CH), bf16),
        compiler_params=_params(("parallel",), 24 << 20),
    )(sg_in, sg_in, ln_g, ln_b, w_s, b_s)


def _sgu_bwd(sg_in, dsz, ln_g, ln_b, w_s, w_s_t, b_s, name):
    S = sg_in.shape[0]

    def body(u_ref, v_ref, lg_ref, lb_ref, w_ref, wt_ref, b_ref, d_ref, du_ref, dv_ref, dw_ref, db_ref, dlg_ref, dlb_ref):
        wb = w_ref[...].astype(bf16)
        wtb = wt_ref[...].astype(bf16)

        def chunk(n, carry):
            dwa, dba, dlga, dlba = carry
            rows = _chunk_rows(n)
            u = u_ref[rows, :].astype(f32)
            v = v_ref[rows, :].astype(f32)
            gu = _gelu(u)
            r, xh = _ln_stats(_gelu(v))
            vlb = (xh * lg_ref[...] + lb_ref[...]).astype(bf16)
            t = jnp.dot(wb, vlb, preferred_element_type=f32) + b_ref[...]
            d = d_ref[rows, :].astype(f32)
            dt = d * gu
            dtb = dt.astype(bf16)
            dwa = dwa + lax.dot_general(dtb, vlb, (((1,), (1,)), ((), ())), preferred_element_type=f32)
            dba = dba + jnp.sum(dt, axis=1, keepdims=True)
            dvl = jnp.dot(wtb, dtb, preferred_element_type=f32)
            dlga = dlga + jnp.sum(dvl * xh, axis=0, keepdims=True)
            dlba = dlba + jnp.sum(dvl, axis=0, keepdims=True)
            dgv = _ln_bwd(r, xh, dvl * lg_ref[...])
            du_ref[rows, :] = (d * t * _gelu_grad(u)).astype(du_ref.dtype)
            dv_ref[rows, :] = (dgv * _gelu_grad(v)).astype(dv_ref.dtype)
            return dwa, dba, dlga, dlba

        init = (jnp.zeros((SG_CHUNK, SG_CHUNK), f32), jnp.zeros((SG_CHUNK, 1), f32),
                jnp.zeros((1, LANES), f32), jnp.zeros((1, LANES), f32))
        dwa, dba, dlga, dlba = lax.fori_loop(0, S // SG_CHUNK, chunk, init)
        dw_ref[...] = dwa
        db_ref[...] = dba
        dlg_ref[...] = dlga
        dlb_ref[...] = dlba

    wspec = pl.BlockSpec((None, SG_CHUNK, SG_CHUNK), lambda g: (g, 0, 0))
    bspec = pl.BlockSpec((None, SG_CHUNK, 1), lambda g: (g, 0, 0))
    lspec = pl.BlockSpec((1, LANES), lambda g: (0, g))
    cspec = pl.BlockSpec((S, LANES), lambda g: (0, g))
    return _call(
        body, name=name, grid=(SG_G,),
        in_specs=[_group_specs(S, SG_G, False), _group_specs(S, SG_G, True), lspec, lspec, wspec, wspec, bspec, cspec],
        out_specs=[cspec, cspec, wspec, bspec, lspec, lspec],
        out_shape=[jax.ShapeDtypeStruct((S, SG_CH), bf16), jax.ShapeDtypeStruct((S, SG_CH), bf16),
                   jax.ShapeDtypeStruct((SG_G, SG_CHUNK, SG_CHUNK), f32), jax.ShapeDtypeStruct((SG_G, SG_CHUNK, 1), f32),
                   jax.ShapeDtypeStruct((1, SG_CH), f32), jax.ShapeDtypeStruct((1, SG_CH), f32)],
        compiler_params=_params(("parallel",), 24 << 20),
    )(sg_in, sg_in, ln_g, ln_b, w_s, w_s_t, b_s, dsz)


def _merge_fwd(gl, b_gate, ya, yc, ys, name):
    D = ya.shape[1]

    def body(gl_ref, b_ref, ya_ref, yc_ref, ys_ref, o_ref):
        acc = jnp.zeros(o_ref.shape, f32)
        for i, y_ref in enumerate((ya_ref, yc_ref, ys_ref)):
            cols = slice(i * D, (i + 1) * D)
            acc = acc + jax.nn.sigmoid(gl_ref[:, cols].astype(f32) + b_ref[:, cols]) * y_ref[...].astype(f32)
        o_ref[...] = acc.astype(o_ref.dtype)

    return _rows(body, [("r", gl), ("f", b_gate), ("r", ya), ("r", yc), ("r", ys)], [("r", D, bf16)],
                 tm=min(128, gl.shape[0]), name=name)[0]


def _merge_bwd(dm, gl, b_gate, ya, yc, ys, name):
    D = ya.shape[1]

    def body(dm_ref, gl_ref, b_ref, ya_ref, yc_ref, ys_ref, dgl_ref, dya_ref, dyc_ref, dys_ref, db_ref):
        dm_ = dm_ref[...].astype(f32)
        for i, (y_ref, dy_ref) in enumerate(((ya_ref, dya_ref), (yc_ref, dyc_ref), (ys_ref, dys_ref))):
            cols = slice(i * D, (i + 1) * D)
            gate = jax.nn.sigmoid(gl_ref[:, cols].astype(f32) + b_ref[:, cols])
            dy_ref[...] = (dm_ * gate).astype(dy_ref.dtype)
            dlog = dm_ * y_ref[...].astype(f32) * gate * (1.0 - gate)
            dgl_ref[:, cols] = dlog.astype(dgl_ref.dtype)
            part = jnp.sum(dlog, axis=0, keepdims=True)
            first = pl.program_id(0) == 0

            @pl.when(first)
            def _():
                db_ref[:, cols] = part

            @pl.when(jnp.logical_not(first))
            def _():
                db_ref[:, cols] += part

    return _rows(body, [("r", dm), ("r", gl), ("f", b_gate), ("r", ya), ("r", yc), ("r", ys)],
                 [("r", 3 * D, bf16), ("r", D, bf16), ("r", D, bf16), ("r", D, bf16), ("a", (1, 3 * D), f32)],
                 tm=min(128, gl.shape[0]), name=name)


def _swiglu_fwd(fg, fu, name):
    def body(g_ref, u_ref, o_ref):
        g = g_ref[...].astype(f32)
        o_ref[...] = (g * jax.nn.sigmoid(g) * u_ref[...].astype(f32)).astype(o_ref.dtype)

    return _rows(body, [("r", fg), ("r", fu)], [("r", fg.shape[1], bf16)], tm=min(128, fg.shape[0]), name=name)[0]


def _swiglu_bwd(dact, fg, fu, name):
    def body(d_ref, g_ref, u_ref, dg_ref, du_ref):
        d = d_ref[...].astype(f32)
        g = g_ref[...].astype(f32)
        sg = jax.nn.sigmoid(g)
        dg_ref[...] = (d * u_ref[...].astype(f32) * sg * (1.0 + g * (1.0 - sg))).astype(dg_ref.dtype)
        du_ref[...] = (d * g * sg).astype(du_ref.dtype)

    return _rows(body, [("r", dact), ("r", fg), ("r", fu)], [("r", fg.shape[1], bf16), ("r", fg.shape[1], bf16)],
                 tm=min(128, fg.shape[0]), name=name)


def _row_tile(r, c, n_arrays, itemsize=4):
    for tm in (512, 256, 128, 64, 32, 16, 8):
        if r % tm == 0 and 2 * n_arrays * tm * c * itemsize <= (24 << 20):
            return tm
    return r


def _sum_slots(slots, name):
    n, r, c = slots.shape
    tm = _row_tile(r, c, n + 2)

    def body(s_ref, o_ref):
        acc = s_ref[0].astype(f32)
        for k in range(1, n):
            acc = acc + s_ref[k].astype(f32)
        o_ref[...] = acc

    return _call(body, name=name, grid=(r // tm,),
                 in_specs=[pl.BlockSpec((n, tm, c), lambda i: (0, i, 0))],
                 out_specs=pl.BlockSpec((tm, c), lambda i: (i, 0)),
                 out_shape=jax.ShapeDtypeStruct((r, c), f32),
                 compiler_params=_params(("parallel",), 40 << 20))(slots)


def _add_sibling(g4, recv, core, name):
    _, _, r, c = g4.shape
    tm = _row_tile(r, c, 3, 2)

    def body(core_ref, g_ref, r_ref, o_ref):
        o_ref[...] = (g_ref[...].astype(f32) + r_ref[...].astype(f32)).astype(o_ref.dtype)

    grid_spec = pltpu.PrefetchScalarGridSpec(
        num_scalar_prefetch=1, grid=(N_CHIP, r // tm),
        in_specs=[pl.BlockSpec((None, None, tm, c), lambda k, i, core_ref: (k, core_ref[0], i, 0)),
                  pl.BlockSpec((None, tm, c), lambda k, i, core_ref: (k, i, 0))],
        out_specs=pl.BlockSpec((None, tm, c), lambda k, i, core_ref: (k, i, 0)))
    return _call(body, name=name, grid_spec=grid_spec, out_shape=jax.ShapeDtypeStruct((N_CHIP, r, c), bf16),
                 compiler_params=_params(("parallel", "parallel"), 40 << 20))(core, g4, recv)


def _adamw(w, g, m, v, name):
    L, r, c = w.shape
    tm = _row_tile(r, c, 7)
    c1 = 1.0 - ADAM_B1 ** ADAM_STEP
    c2 = 1.0 - ADAM_B2 ** ADAM_STEP

    def body(w_ref, g_ref, m_ref, v_ref, d_ref, mo_ref, vo_ref):
        g_ = g_ref[...]
        m_ = ADAM_B1 * m_ref[...] + (1.0 - ADAM_B1) * g_
        v_ = ADAM_B2 * v_ref[...] + (1.0 - ADAM_B2) * (g_ * g_)
        d_ref[...] = -ADAM_LR * ((m_ / c1) / (jnp.sqrt(v_ / c2) + ADAM_EPS) + ADAM_WD * w_ref[...])
        mo_ref[...] = m_
        vo_ref[...] = v_

    spec = pl.BlockSpec((None, tm, c), lambda l, i: (l, i, 0))
    shp = jax.ShapeDtypeStruct((L, r, c), f32)
    return _call(body, name=name, grid=(L, r // tm), in_specs=[spec] * 4, out_specs=[spec] * 3,
                 out_shape=[shp] * 3, compiler_params=_params(("parallel", "parallel"), 40 << 20))(w, g, m, v)


def _mesh_pos():
    return lax.axis_index("x"), lax.axis_index("y"), lax.axis_index("c")


def _all_gather(shards, name):
    n = len(shards)

    def body(*refs):
        x_refs, o_refs = refs[:n], refs[n:2 * n]
        send_sems, recv_sems, local_sems = refs[2 * n:]
        x, y, c = _mesh_pos()
        me, sibling = (x, y, c), (x, y, 1 - c)
        chips = [(1 - x, y), (x, 1 - y), (1 - x, 1 - y)]

        def rows(k, px, py, pc):
            r = shards[k].shape[0]
            return o_refs[k].at[pl.ds((4 * px + 2 * py + pc) * r, r), :]

        def copy(k, s, block, to, src=None):
            return pltpu.make_async_remote_copy(
                src_ref=rows(k, *block) if src is None else src, dst_ref=rows(k, *block),
                send_sem=send_sems.at[k, s], recv_sem=recv_sems.at[k, s], device_id=to, device_id_type=MESH)

        mine = [pltpu.make_async_copy(x_refs[k], rows(k, *me), local_sems.at[k]) for k in range(n)]
        for cp in mine:
            cp.start()
        first = [copy(k, 0, me, sibling, src=x_refs[k]) for k in range(n)]
        for j, chip in enumerate(chips):
            first += [copy(k, 1 + j, me, (*chip, c), src=x_refs[k]) for k in range(n)]
        for cp in first:
            cp.start()
        passed = []
        for j, chip in enumerate(chips):
            for k in range(n):
                copy(k, 1 + j, (*chip, c), me).wait_recv()
                fwd = copy(k, 4 + j, (*chip, c), sibling)
                fwd.start()
                passed.append(fwd)
        for k in range(n):
            copy(k, 0, sibling, me).wait_recv()
        for j, chip in enumerate(chips):
            for k in range(n):
                copy(k, 4 + j, (*chip, 1 - c), me).wait_recv()
        for cp in first + passed:
            cp.wait_send()
        for cp in mine:
            cp.wait()

    return _call(
        body, name=name, in_specs=[HBM] * n, out_specs=[HBM] * n,
        out_shape=[jax.ShapeDtypeStruct((N_DEV * s.shape[0], s.shape[1]), s.dtype) for s in shards],
        scratch_shapes=[pltpu.SemaphoreType.DMA((n, 7)), pltpu.SemaphoreType.DMA((n, 7)), pltpu.SemaphoreType.DMA((n,))],
    )(*shards)


def _send_to_sibling(g4s, name):
    n = len(g4s)

    def body(*refs):
        g_refs, o_refs = refs[:n], refs[n:2 * n]
        send_sems, recv_sems = refs[2 * n:]
        x, y, c = _mesh_pos()
        copies = [pltpu.make_async_remote_copy(
            src_ref=g_refs[k].at[:, 1 - c], dst_ref=o_refs[k], send_sem=send_sems.at[k], recv_sem=recv_sems.at[k],
            device_id=(x, y, 1 - c), device_id_type=MESH) for k in range(n)]
        for cp in copies:
            cp.start()
        for cp in copies:
            cp.wait()

    return _call(
        body, name=name, in_specs=[HBM] * n, out_specs=[HBM] * n,
        out_shape=[jax.ShapeDtypeStruct((N_CHIP,) + g.shape[2:], g.dtype) for g in g4s],
        scratch_shapes=[pltpu.SemaphoreType.DMA((n,)), pltpu.SemaphoreType.DMA((n,))],
    )(*g4s)


def _send_to_chips(sums, name):
    n = len(sums)

    def body(*refs):
        s_refs, o_refs = refs[:n], refs[n:2 * n]
        send_sems, recv_sems, local_sems = refs[2 * n:]
        x, y, c = _mesh_pos()
        my_chip = 2 * x + y
        chips = [(1 - x, y), (x, 1 - y), (1 - x, 1 - y)]
        mine = [pltpu.make_async_copy(s_refs[k].at[my_chip], o_refs[k].at[my_chip], local_sems.at[k]) for k in range(n)]
        for cp in mine:
            cp.start()
        copies = []
        for j, (px, py) in enumerate(chips):
            copies += [pltpu.make_async_remote_copy(
                src_ref=s_refs[k].at[2 * px + py], dst_ref=o_refs[k].at[my_chip], send_sem=send_sems.at[k, j],
                recv_sem=recv_sems.at[k, j], device_id=(px, py, c), device_id_type=MESH) for k in range(n)]
        for cp in copies:
            cp.start()
        for cp in copies:
            cp.wait()
        for cp in mine:
            cp.wait()

    return _call(
        body, name=name, in_specs=[HBM] * n, out_specs=[HBM] * n,
        out_shape=[jax.ShapeDtypeStruct(s.shape, s.dtype) for s in sums],
        scratch_shapes=[pltpu.SemaphoreType.DMA((n, 3)), pltpu.SemaphoreType.DMA((n, 3)), pltpu.SemaphoreType.DMA((n,))],
    )(*sums)


def _reduce_scatter(grads, core, tag):
    g4s = [g.reshape(N_CHIP, 2, g.shape[0] // N_DEV, g.shape[1]) for g in grads]
    recv = _send_to_sibling(g4s, name="rs_sibling" + tag)
    sums = [_add_sibling(g4, rv, core, name="rs_add" + tag) for g4, rv in zip(g4s, recv)]
    slots = _send_to_chips(sums, name="rs_chips" + tag)
    return [_sum_slots(s, name="rs_sum" + tag) for s in slots]


def _rope_tables(S):
    rows = S // GRID_W
    row = jnp.repeat(jnp.arange(rows, dtype=f32), GRID_W)
    col = jnp.tile(jnp.arange(GRID_W, dtype=f32), rows)
    nf = HEAD_DIM // 4
    inv = ROPE_THETA ** (-jnp.arange(nf, dtype=f32) / nf)
    ang = jnp.concatenate([row[:, None] * inv, col[:, None] * inv], axis=-1)
    cos, sin = jnp.cos(ang), jnp.sin(ang)
    return jnp.concatenate([cos, cos], axis=-1), jnp.concatenate([-sin, sin], axis=-1)


def _layer_fwd(xin, p, w, cos2, sin2):
    sv = {"xin": xin}
    h = sv["h"] = _rms_fwd(xin, p["g_mix"], name="rms_mix")
    proj = functools.partial(_mm, h, w["in"], "nt", bf16)
    q_raw = sv["q_raw"] = proj(n=Q_COLS, b_off=0, name="proj_q")
    kv_raw = sv["kv_raw"] = proj(n=2 * KV_COLS, b_off=OFF_KV, name="proj_kv")
    conv_in = sv["conv_in"] = proj(n=2 * CONV_CH, b_off=OFF_CONV, name="proj_conv")
    sg_in = sv["sg_in"] = proj(n=2 * SG_CH, b_off=OFF_SG, name="proj_sg")
    gl = sv["gl"] = proj(n=3 * D_MODEL, b_off=OFF_GATE, name="proj_gate")
    qr, kr = sv["qr"], sv["kr"] = _qk_fwd(q_raw, kv_raw, p["q_norm_g"], p["k_norm_g"], cos2, sin2, name="qk_fwd")
    o = sv["o"] = _attn_fwd(qr, kr, kv_raw, name="attn_fwd")
    c = sv["c"] = _conv1_fwd(conv_in, w["dw"], p["b_dw"], name="conv1_fwd")
    cz = sv["cz"] = _conv2_fwd(c, p["conv_ln_g"], p["conv_ln_b"], name="conv2_fwd")
    sz = sv["sz"] = _sgu_fwd(sg_in, p["sg_ln_g"], p["sg_ln_b"], p["w_s"], p["b_s"], name="sgu_fwd")
    ya = sv["ya"] = _mm(o, w["attn_o"], "nt", bf16, name="out_attn")
    yc = sv["yc"] = _mm(cz, w["conv_o"], "nt", bf16, name="out_conv")
    ys = sv["ys"] = _mm(sz, w["sg_o"], "nt", bf16, name="out_sg")
    merged = sv["merged"] = _merge_fwd(gl, p["b_gate"], ya, yc, ys, name="merge_fwd")
    x1 = sv["x1"] = _mm(merged, w["out"], "nn", f32, res=xin, name="out_proj")
    hf = sv["hf"] = _rms_fwd(x1, p["g_ffn"], name="rms_ffn")
    fg = sv["fg"] = _mm(hf, w["ff_gate"], "nt", bf16, name="ff_gate")
    fu = sv["fu"] = _mm(hf, w["ff_up"], "nt", bf16, name="ff_up")
    act = sv["act"] = _swiglu_fwd(fg, fu, name="swiglu_fwd")
    x2 = _mm(act, w["ff_down"], "nn", f32, res=x1, name="ff_down")
    return x2, sv


def _layer_bwd(dx2, dx2b, sv, p, w, cos2, sin2):
    small = {}
    dact = _mm(dx2b, w["ff_down"], "nt", bf16, name="d_act")
    g_down = _mm(sv["act"], dx2b, "tn", bf16, name="g_ff_down")
    dfg, dfu = _swiglu_bwd(dact, sv["fg"], sv["fu"], name="swiglu_bwd")
    dhf = _mm(dfg, w["ff_gate"], "nn", f32, name="d_hf_gate")
    dhf = _mm(dfu, w["ff_up"], "nn", f32, res=dhf, name="d_hf_up")
    g_gate = _mm(dfg, sv["hf"], "tn", bf16, name="g_ff_gate")
    g_up = _mm(dfu, sv["hf"], "tn", bf16, name="g_ff_up")
    dx1, dx1b, small["g_ffn"] = _rms_bwd(sv["x1"], p["g_ffn"], dhf, dx2, name="rms_ffn_bwd")
    dmerged = _mm(dx1b, w["out"], "nt", bf16, name="d_merged")
    g_out = _mm(sv["merged"], dx1b, "tn", bf16, name="g_out")
    dgl, dya, dyc, dys, small["b_gate"] = _merge_bwd(dmerged, sv["gl"], p["b_gate"], sv["ya"], sv["yc"], sv["ys"],
                                                    name="merge_bwd")
    do = _mm(dya, w["attn_o"], "nn", bf16, name="d_o")
    g_ao = _mm(dya, sv["o"], "tn", bf16, name="g_attn_o")
    dcz = _mm(dyc, w["conv_o"], "nn", bf16, name="d_cz")
    g_co = _mm(dyc, sv["cz"], "tn", bf16, name="g_conv_o")
    dsz = _mm(dys, w["sg_o"], "nn", bf16, name="d_sz")
    g_so = _mm(dys, sv["sz"], "tn", bf16, name="g_sg_o")
    dsu, dsv, small["w_s"], small["b_s"], small["sg_ln_g"], small["sg_ln_b"] = _sgu_bwd(
        sv["sg_in"], dsz, p["sg_ln_g"], p["sg_ln_b"], p["w_s"], p["w_s_t"], p["b_s"], name="sgu_bwd")
    dc, small["conv_ln_g"], small["conv_ln_b"] = _conv2_bwd(sv["c"], dcz, p["conv_ln_g"], p["conv_ln_b"], name="conv2_bwd")
    da, dgt, small["w_dw"], small["b_dw"] = _conv1_bwd(sv["conv_in"], dc, w["dw"], name="conv1_bwd")
    dqr, dkr, dv = _attn_bwd(sv["qr"], sv["kr"], sv["kv_raw"], do, name="attn_bwd")
    dq_raw, dk_raw, small["q_norm_g"], small["k_norm_g"] = _qk_bwd(
        sv["q_raw"], sv["kv_raw"], dqr, dkr, p["q_norm_g"], p["k_norm_g"], cos2, sin2, name="qk_bwd")
    dproj = jnp.concatenate([dq_raw, dk_raw, dv.astype(bf16), da, dgt, dsu, dsv, dgl], axis=1)
    dh = _mm(dproj, w["in"], "nn", f32, name="d_h")
    g_in = _mm(dproj, sv["h"], "tn", bf16, name="g_in")
    dx, dxb, small["g_mix"] = _rms_bwd(sv["xin"], p["g_mix"], dh, dx1, name="rms_mix_bwd")
    return dx, dxb, [g_in, g_ao, g_co, g_so, g_out, g_gate, g_up, g_down], small


SMALL = ("g_mix", "b_gate", "q_norm_g", "k_norm_g", "b_dw", "conv_ln_g", "conv_ln_b", "sg_ln_g", "sg_ln_b",
         "w_s", "b_s", "g_ffn")
PACK_ALIGN = 8 * LANES


def _pack(parts):
    flat = jnp.concatenate([a.reshape(-1).astype(f32) for a in parts])
    pad = -flat.shape[0] % PACK_ALIGN
    return jnp.pad(flat, (0, pad)).reshape(-1, LANES)


def _unpack(buf, shapes):
    flat = buf.reshape(-1)
    out, pos = [], 0
    for shp in shapes:
        size = math.prod(shp)
        out.append(flat[pos:pos + size].reshape(shp))
        pos += size
    return out


def kernel(x, g_mix, w_in, b_gate, q_norm_g, k_norm_g, w_attn_o, w_dw, b_dw, conv_ln_g, conv_ln_b, w_conv_o, sg_ln_g, sg_ln_b, w_s, b_s, w_sg_o, w_out, g_ffn, w_ff_gate, w_ff_up, w_ff_down, g_final, loss_target, m_g_mix, m_w_in, m_b_gate, m_q_norm_g, m_k_norm_g, m_w_attn_o, m_w_dw, m_b_dw, m_conv_ln_g, m_conv_ln_b, m_w_conv_o, m_sg_ln_g, m_sg_ln_b, m_w_s, m_b_s, m_w_sg_o, m_w_out, m_g_ffn, m_w_ff_gate, m_w_ff_up, m_w_ff_down, m_g_final, v_g_mix, v_w_in, v_b_gate, v_q_norm_g, v_k_norm_g, v_w_attn_o, v_w_dw, v_b_dw, v_conv_ln_g, v_conv_ln_b, v_w_conv_o, v_sg_ln_g, v_sg_ln_b, v_w_s, v_b_s, v_w_sg_o, v_w_out, v_g_ffn, v_w_ff_gate, v_w_ff_up, v_w_ff_down, v_g_final):
    weights = dict(g_mix=g_mix, w_in=w_in, b_gate=b_gate, q_norm_g=q_norm_g, k_norm_g=k_norm_g, w_attn_o=w_attn_o,
                   w_dw=w_dw, b_dw=b_dw, conv_ln_g=conv_ln_g, conv_ln_b=conv_ln_b, w_conv_o=w_conv_o, sg_ln_g=sg_ln_g,
                   sg_ln_b=sg_ln_b, w_s=w_s, b_s=b_s, w_sg_o=w_sg_o, w_out=w_out, g_ffn=g_ffn, w_ff_gate=w_ff_gate,
                   w_ff_up=w_ff_up, w_ff_down=w_ff_down, g_final=g_final)
    mom_m = dict(g_mix=m_g_mix, w_in=m_w_in, b_gate=m_b_gate, q_norm_g=m_q_norm_g, k_norm_g=m_k_norm_g,
                 w_attn_o=m_w_attn_o, w_dw=m_w_dw, b_dw=m_b_dw, conv_ln_g=m_conv_ln_g, conv_ln_b=m_conv_ln_b,
                 w_conv_o=m_w_conv_o, sg_ln_g=m_sg_ln_g, sg_ln_b=m_sg_ln_b, w_s=m_w_s, b_s=m_b_s, w_sg_o=m_w_sg_o,
                 w_out=m_w_out, g_ffn=m_g_ffn, w_ff_gate=m_w_ff_gate, w_ff_up=m_w_ff_up, w_ff_down=m_w_ff_down,
                 g_final=m_g_final)
    mom_v = dict(g_mix=v_g_mix, w_in=v_w_in, b_gate=v_b_gate, q_norm_g=v_q_norm_g, k_norm_g=v_k_norm_g,
                 w_attn_o=v_w_attn_o, w_dw=v_w_dw, b_dw=v_b_dw, conv_ln_g=v_conv_ln_g, conv_ln_b=v_conv_ln_b,
                 w_conv_o=v_w_conv_o, sg_ln_g=v_sg_ln_g, sg_ln_b=v_sg_ln_b, w_s=v_w_s, b_s=v_b_s, w_sg_o=v_w_sg_o,
                 w_out=v_w_out, g_ffn=v_g_ffn, w_ff_gate=v_w_ff_gate, w_ff_up=v_w_ff_up, w_ff_down=v_w_ff_down,
                 g_final=v_g_final)
    S, D = x.shape[1], x.shape[2]
    xi, yi, ci = _mesh_pos()
    me = 4 * xi + 2 * yi + ci
    core = jnp.reshape(ci, (1,)).astype(jnp.int32)
    cos2, sin2 = _rope_tables(S)

    big = ("w_in", "w_attn_o", "w_conv_o", "w_sg_o", "w_out", "w_ff_gate", "w_ff_up", "w_ff_down")
    transposed = {"w_in", "w_attn_o", "w_conv_o", "w_sg_o", "w_ff_gate", "w_ff_up"}
    keys = ("in", "attn_o", "conv_o", "sg_o", "out", "ff_gate", "ff_up", "ff_down", "dw")
    W, P = [], []
    for l in range(DEPTH):
        shards = [(weights[n][l].T if n in transposed else weights[n][l]).astype(bf16) for n in big]
        shards.append(jnp.pad(w_dw[l].reshape(CONV_W, LANES), ((0, CONV_WP - CONV_W), (0, 0))))
        W.append(dict(zip(keys, _all_gather(shards, name="gather_weights"))))
        p = {n: weights[n][l].reshape(1, -1) for n in SMALL if n not in ("w_s", "b_s")}
        p["w_s"] = w_s[l]
        p["w_s_t"] = jnp.swapaxes(w_s[l], 1, 2)
        p["b_s"] = b_s[l].reshape(SG_G, SG_CHUNK, 1)
        P.append(p)

    h = x.reshape(S, D)
    saved = []
    for l in range(DEPTH):
        h, sv = _layer_fwd(h, P[l], W[l], cos2, sin2)
        saved.append(sv)
    dx, dxb, sq, g_final_part = _final_loss(h, g_final.reshape(1, D), loss_target.reshape(S, D), name="final_loss")
    loss = lax.psum(0.5 * jnp.sum(sq) / D, ("x", "y", "c"))

    big_grads, small_grads = [None] * DEPTH, [None] * DEPTH
    for l in reversed(range(DEPTH)):
        dx, dxb, grads, small_grads[l] = _layer_bwd(dx, dxb, saved[l], P[l], W[l], cos2, sin2)
        big_grads[l] = _reduce_scatter(grads, core, tag="")
    grad_x = dx.reshape(x.shape)

    small_shapes = [weights[n].shape for n in SMALL] + [g_final.shape, (DEPTH, CONV_CH // LANES, CONV_WP, LANES)]
    parts = [jnp.stack([small_grads[l][n].reshape(weights[n].shape[1:]) for l in range(DEPTH)]) for n in SMALL]
    parts += [g_final_part.reshape(g_final.shape), jnp.stack([small_grads[l]["w_dw"] for l in range(DEPTH)])]
    packed = _pack(parts)
    gathered = _all_gather([packed], name="gather_small")[0]
    total = _sum_slots(gathered.reshape(N_DEV, packed.shape[0], LANES), name="sum_small")
    small_total = _unpack(total, small_shapes)
    grads_out = dict(zip(SMALL + ("g_final",), small_total[:-1]))
    dw_full = small_total[-1]
    grads_out["w_dw"] = lax.dynamic_index_in_dim(dw_full, me, axis=1, keepdims=False)[:, :CONV_W].reshape(w_dw.shape)

    for i, n in enumerate(big):
        per_layer = [big_grads[l][i] for l in range(DEPTH)]
        grads_out[n] = jnp.stack([g.T if n in transposed else g for g in per_layer])

    delta, new_m, new_v = {}, {}, {}
    for n in big:
        delta[n], new_m[n], new_v[n] = _adamw(weights[n], grads_out[n], mom_m[n], mom_v[n], name="adamw_" + n)
    rep = SMALL + ("g_final",)
    rep_shapes = [weights[n].shape for n in rep]
    packs = [_pack([src[n] for n in rep])[None] for src in (weights, grads_out, mom_m, mom_v)]
    for dst, buf in zip((delta, new_m, new_v), _adamw(*packs, name="adamw_small")):
        dst.update(zip(rep, _unpack(buf[0], rep_shapes)))
    flat = lambda a: a.reshape(1, DEPTH * CONV_W, LANES)
    for dst, buf in zip((delta, new_m, new_v),
                        _adamw(flat(w_dw), flat(grads_out["w_dw"]), flat(m_w_dw), flat(v_w_dw), name="adamw_w_dw")):
        dst["w_dw"] = buf.reshape(w_dw.shape)

    order = ("g_mix", "w_in", "b_gate", "q_norm_g", "k_norm_g", "w_attn_o", "w_dw", "b_dw", "conv_ln_g", "conv_ln_b",
             "w_conv_o", "sg_ln_g", "sg_ln_b", "w_s", "b_s", "w_sg_o", "w_out", "g_ffn", "w_ff_gate", "w_ff_up",
             "w_ff_down", "g_final")
    return (loss, grad_x, *[grads_out[n] for n in order], *[delta[n] for n in order],
            *[new_m[n] for n in order], *[new_v[n] for n in order])
```

```python
import functools
import math

import jax
import jax.numpy as jnp
from jax import lax
from jax.experimental import pallas as pl
from jax.experimental.pallas import tpu as pltpu

f32, bf16 = jnp.float32, jnp.bfloat16

D_MODEL = 2048
SEQ = 2048
DEPTH = 2
GRID_W = 64
HEAD_DIM = 128
LANES = 128
N_Q = (D_MODEL // 2) // HEAD_DIM
N_KV = N_Q // 4
GRP = N_Q // N_KV
Q_COLS = N_Q * HEAD_DIM
KV_COLS = N_KV * HEAD_DIM
CONV_CH = D_MODEL // 2
CONV_W = 31
CONV_PAD = CONV_W // 2
CONV_WP = 32
SG_CH = D_MODEL // 2
SG_G = SG_CH // LANES
SG_CHUNK = 128
D_FF = -(-8 * D_MODEL // (3 * 256)) * 256
OFF_KV = Q_COLS
OFF_CONV = OFF_KV + 2 * KV_COLS
OFF_SG = OFF_CONV + 2 * CONV_CH
OFF_GATE = OFF_SG + 2 * SG_CH
IN_COLS = OFF_GATE + 3 * D_MODEL
ROPE_THETA = 10000.0
SCALE = HEAD_DIM ** -0.5
N_DEV = 8
N_CHIP = 4

ADAM_LR, ADAM_B1, ADAM_B2, ADAM_EPS, ADAM_WD, ADAM_STEP = 0.001, 0.9, 0.999, 1e-08, 0.01, 10

VMEM_BYTES_V7X = 64 << 20
VMEM_CAP = VMEM_BYTES_V7X - (6 << 20)
MESH = pl.DeviceIdType.MESH
HBM = pl.BlockSpec(memory_space=pltpu.HBM)


def _call(body, **kw):
    return pl.pallas_call(body, **kw)


def _pick(n, cands):
    for c in cands:
        if n % c == 0:
            return c
    raise ValueError((n, cands))


def _params(sem, vmem_bytes):
    return pltpu.CompilerParams(dimension_semantics=sem, vmem_limit_bytes=int(min(max(vmem_bytes, 16 << 20), VMEM_CAP)))


def _mm(a, b, form, out_dtype, *, n=None, b_off=0, res=None, name):
    if form == "tn":
        K, M = a.shape
    else:
        M, K = a.shape
    N = n if n is not None else (b.shape[0] if form == "nt" else b.shape[1])
    if K <= 2048:
        tk = K
        if form == "tn":
            tm = _pick(M, (512, 256, 128))
            tn = N if N <= 2048 else _pick(N, (1024, 512, 256, 128))
        else:
            tm = M if M <= 2048 else _pick(M, (2048, 1024, 512))
            tn = _pick(math.gcd(N, b_off) if b_off else N, (256, 128) if res is not None else (512, 256, 128))
    else:
        tk = _pick(K, (512, 256, 128))
        tm = _pick(M, (1024, 512, 256, 128))
        tn = _pick(math.gcd(N, b_off) if b_off else N, (1024, 512, 256, 128))
    assert b_off % tn == 0
    off = b_off // tn
    nk = K // tk
    if form == "tn":
        a_spec = pl.BlockSpec((tk, tm), lambda i, j, k: (k, i))
    else:
        a_spec = pl.BlockSpec((tm, tk), lambda i, j, k: (i, k))
    if form == "nt":
        b_spec = pl.BlockSpec((tn, tk), lambda i, j, k: (j + off, k))
    else:
        b_spec = pl.BlockSpec((tk, tn), lambda i, j, k: (k, j + off))
    dims = {"nn": ((1,), (0,)), "nt": ((1,), (1,)), "tn": ((0,), (0,))}[form]
    has_res = res is not None

    def body(*refs):
        if has_res:
            a_ref, b_ref, r_ref, o_ref = refs[:4]
        else:
            a_ref, b_ref, o_ref = refs[:3]
        p = lax.dot_general(a_ref[...], b_ref[...], (dims, ((), ())), preferred_element_type=f32)

        def finish(acc):
            if has_res:
                acc = acc + r_ref[...].astype(f32)
            o_ref[...] = acc.astype(o_ref.dtype)

        if nk == 1:
            finish(p)
        else:
            acc_ref = refs[-1]
            k = pl.program_id(2)

            @pl.when(k == 0)
            def _():
                acc_ref[...] = p

            @pl.when(k > 0)
            def _():
                acc_ref[...] += p

            @pl.when(k == nk - 1)
            def _():
                finish(acc_ref[...])

    in_specs = [a_spec, b_spec]
    args = [a, b]
    osz = jnp.dtype(out_dtype).itemsize
    vmem = 2 * (tm * tk * 2 + tk * tn * 2 + tm * tn * osz) + 2 * tm * tn * 4
    if has_res:
        in_specs.append(pl.BlockSpec((tm, tn), lambda i, j, k: (i, j)))
        args.append(res)
        vmem += 2 * tm * tn * res.dtype.itemsize
    scratch = []
    if nk > 1:
        scratch.append(pltpu.VMEM((tm, tn), f32))
        vmem += tm * tn * 4
    return _call(
        body, name=name, grid=(M // tm, N // tn, nk),
        in_specs=in_specs, out_specs=pl.BlockSpec((tm, tn), lambda i, j, k: (i, j)),
        out_shape=jax.ShapeDtypeStruct((M, N), out_dtype), scratch_shapes=scratch,
        compiler_params=_params(("parallel", "parallel", "arbitrary"), vmem + (8 << 20)),
    )(*args)


def _rows(body, ins, outs, *, tm, name, vmem=40 << 20):
    nrows = next(s[1].shape[0] for s in ins if s[0] == "r")
    in_specs, args = [], []
    for s in ins:
        arr = s[1]
        if s[0] == "r":
            w = s[2] if len(s) > 2 else arr.shape[1]
            cb = s[3] if len(s) > 3 else 0
            in_specs.append(pl.BlockSpec((tm, w), functools.partial(lambda i, cb: (i, cb), cb=cb)))
        else:
            in_specs.append(pl.BlockSpec(arr.shape, functools.partial(lambda i, nd: (0,) * nd, nd=arr.ndim)))
        args.append(arr)
    out_specs, out_shape = [], []
    for s in outs:
        if s[0] == "r":
            out_specs.append(pl.BlockSpec((tm, s[1]), lambda i: (i, 0)))
            out_shape.append(jax.ShapeDtypeStruct((nrows, s[1]), s[2]))
        else:
            out_specs.append(pl.BlockSpec(s[1], functools.partial(lambda i, nd: (0,) * nd, nd=len(s[1]))))
            out_shape.append(jax.ShapeDtypeStruct(s[1], s[2]))
    return _call(body, name=name, grid=(nrows // tm,), in_specs=in_specs, out_specs=out_specs,
                 out_shape=out_shape, compiler_params=_params(("arbitrary",), vmem))(*args)


def _accumulate(ref, part):
    i = pl.program_id(0)

    @pl.when(i == 0)
    def _():
        ref[...] = part

    @pl.when(i > 0)
    def _():
        ref[...] += part


def _rms_stats(x):
    r = lax.rsqrt(jnp.mean(x * x, axis=-1, keepdims=True) + 1e-6)
    return r, x * r


def _rms_fwd(x, g, name):
    def body(x_ref, g_ref, o_ref):
        _, xn = _rms_stats(x_ref[...])
        o_ref[...] = (xn * g_ref[...]).astype(o_ref.dtype)

    return _rows(body, [("r", x), ("f", g)], [("r", x.shape[1], bf16)], tm=min(256, x.shape[0]), name=name)[0]


def _rms_bwd(x, g, dh, dres, name):
    D = x.shape[1]

    def body(x_ref, g_ref, dh_ref, dr_ref, dx_ref, dxb_ref, dg_ref):
        r, xn = _rms_stats(x_ref[...])
        dy = dh_ref[...].astype(f32)
        dxn = dy * g_ref[...]
        dx = dr_ref[...] + r * (dxn - xn * jnp.mean(dxn * xn, axis=-1, keepdims=True))
        dx_ref[...] = dx
        dxb_ref[...] = dx.astype(bf16)
        _accumulate(dg_ref, jnp.sum(dy * xn, axis=0, keepdims=True))

    return _rows(body, [("r", x), ("f", g), ("r", dh), ("r", dres)],
                 [("r", D, f32), ("r", D, bf16), ("a", (1, D), f32)], tm=min(256, x.shape[0]), name=name)


def _final_loss(x, g, tgt, name):
    D = x.shape[1]

    def body(x_ref, g_ref, t_ref, dx_ref, dxb_ref, sq_ref, dg_ref):
        r, xn = _rms_stats(x_ref[...])
        gain = g_ref[...]
        diff = xn * gain - t_ref[...]
        dy = diff * (1.0 / D)
        dxn = dy * gain
        dx = r * (dxn - xn * jnp.mean(dxn * xn, axis=-1, keepdims=True))
        dx_ref[...] = dx
        dxb_ref[...] = dx.astype(bf16)
        _accumulate(sq_ref, jnp.sum(diff * diff, axis=0, keepdims=True))
        _accumulate(dg_ref, jnp.sum(dy * xn, axis=0, keepdims=True))

    return _rows(body, [("r", x), ("f", g), ("r", tgt)],
                 [("r", D, f32), ("r", D, bf16), ("a", (1, D), f32), ("a", (1, D), f32)],
                 tm=min(256, x.shape[0]), name=name)


def _qk_fwd(q_raw, kv_raw, qg, kg, cos2, sin2, name):
    def body(q_ref, k_ref, qg_ref, kg_ref, c_ref, s_ref, qo_ref, ko_ref):
        c, s = c_ref[...], s_ref[...]

        def head(src, gain, dst, h):
            cols = slice(h * HEAD_DIM, (h + 1) * HEAD_DIM)
            _, xn = _rms_stats(src[:, cols].astype(f32))
            y = xn * gain
            dst[:, cols] = (y * c + pltpu.roll(y, HEAD_DIM // 2, 1) * s).astype(dst.dtype)

        for h in range(N_Q):
            head(q_ref, qg_ref[...], qo_ref, h)
        for h in range(N_KV):
            head(k_ref, kg_ref[...], ko_ref, h)

    return _rows(body, [("r", q_raw), ("r", kv_raw, KV_COLS, 0), ("f", qg), ("f", kg), ("r", cos2), ("r", sin2)],
                 [("r", Q_COLS, bf16), ("r", KV_COLS, bf16)], tm=min(256, q_raw.shape[0]), name=name)


def _qk_bwd(q_raw, kv_raw, dqr, dkr, qg, kg, cos2, sin2, name):
    def body(q_ref, k_ref, dq_ref, dk_ref, qg_ref, kg_ref, c_ref, s_ref, dqo_ref, dko_ref, dqg_ref, dkg_ref):
        c, s = c_ref[...], s_ref[...]

        def head(src, dsrc, gain, dst, h):
            cols = slice(h * HEAD_DIM, (h + 1) * HEAD_DIM)
            r, xn = _rms_stats(src[:, cols].astype(f32))
            do = dsrc[:, cols].astype(f32)
            dy = do * c + pltpu.roll(do * s, HEAD_DIM // 2, 1)
            dxn = dy * gain
            dst[:, cols] = (r * (dxn - xn * jnp.mean(dxn * xn, axis=-1, keepdims=True))).astype(dst.dtype)
            return jnp.sum(dy * xn, axis=0, keepdims=True)

        dq_gain = head(q_ref, dq_ref, qg_ref[...], dqo_ref, 0)
        for h in range(1, N_Q):
            dq_gain = dq_gain + head(q_ref, dq_ref, qg_ref[...], dqo_ref, h)
        dk_gain = head(k_ref, dk_ref, kg_ref[...], dko_ref, 0)
        for h in range(1, N_KV):
            dk_gain = dk_gain + head(k_ref, dk_ref, kg_ref[...], dko_ref, h)
        _accumulate(dqg_ref, dq_gain)
        _accumulate(dkg_ref, dk_gain)

    return _rows(body, [("r", q_raw), ("r", kv_raw, KV_COLS, 0), ("r", dqr), ("r", dkr), ("f", qg), ("f", kg),
                        ("r", cos2), ("r", sin2)],
                 [("r", Q_COLS, bf16), ("r", KV_COLS, bf16), ("a", (1, HEAD_DIM), f32), ("a", (1, HEAD_DIM), f32)],
                 tm=min(256, q_raw.shape[0]), name=name)


def _softmax_rows(q, k):
    s = lax.dot_general(q, k, (((1,), (1,)), ((), ())), preferred_element_type=f32) * SCALE
    p = jnp.exp(s - jnp.max(s, axis=-1, keepdims=True))
    return p * (1.0 / jnp.sum(p, axis=-1, keepdims=True))


def _attn_fwd(qr, kr, kv_raw, name):
    S = qr.shape[0]
    tq = min(512, S)

    def body(q_ref, k_ref, v_ref, o_ref):
        p = _softmax_rows(q_ref[...], k_ref[...])
        o_ref[...] = jnp.dot(p.astype(bf16), v_ref[...], preferred_element_type=f32).astype(o_ref.dtype)

    return _call(
        body, name=name, grid=(N_Q, S // tq),
        in_specs=[pl.BlockSpec((tq, HEAD_DIM), lambda h, i: (i, h)),
                  pl.BlockSpec((S, HEAD_DIM), lambda h, i: (0, h // GRP)),
                  pl.BlockSpec((S, HEAD_DIM), lambda h, i: (0, N_KV + h // GRP))],
        out_specs=pl.BlockSpec((tq, HEAD_DIM), lambda h, i: (i, h)),
        out_shape=jax.ShapeDtypeStruct((S, Q_COLS), bf16),
        compiler_params=_params(("parallel", "arbitrary"), 6 * tq * S * 4 + (8 << 20)),
    )(qr, kr, kv_raw)


def _attn_bwd(qr, kr, kv_raw, do, name):
    S = qr.shape[0]
    tq = min(256, S)

    def body(q_ref, k_ref, v_ref, do_ref, dq_ref, dk_ref, dv_ref):
        first = jnp.logical_and(pl.program_id(1) == 0, pl.program_id(2) == 0)
        q, k, v, do_ = q_ref[...], k_ref[...], v_ref[...], do_ref[...]
        p = _softmax_rows(q, k)
        dp = lax.dot_general(do_, v, (((1,), (1,)), ((), ())), preferred_element_type=f32)
        ds = (p * (dp - jnp.sum(dp * p, axis=-1, keepdims=True)) * SCALE).astype(bf16)
        dq_ref[...] = jnp.dot(ds, k, preferred_element_type=f32).astype(dq_ref.dtype)
        dv_part = lax.dot_general(p.astype(bf16), do_, (((0,), (0,)), ((), ())), preferred_element_type=f32)
        dk_part = lax.dot_general(ds, q, (((0,), (0,)), ((), ())), preferred_element_type=f32)

        @pl.when(first)
        def _():
            dv_ref[...] = dv_part
            dk_ref[...] = dk_part

        @pl.when(jnp.logical_not(first))
        def _():
            dv_ref[...] += dv_part
            dk_ref[...] += dk_part

    qmap = lambda kv, g, i: (i, kv * GRP + g)
    return _call(
        body, name=name, grid=(N_KV, GRP, S // tq),
        in_specs=[pl.BlockSpec((tq, HEAD_DIM), qmap),
                  pl.BlockSpec((S, HEAD_DIM), lambda kv, g, i: (0, kv)),
                  pl.BlockSpec((S, HEAD_DIM), lambda kv, g, i: (0, N_KV + kv)),
                  pl.BlockSpec((tq, HEAD_DIM), qmap)],
        out_specs=[pl.BlockSpec((tq, HEAD_DIM), qmap),
                   pl.BlockSpec((S, HEAD_DIM), lambda kv, g, i: (0, kv)),
                   pl.BlockSpec((S, HEAD_DIM), lambda kv, g, i: (0, kv))],
        out_shape=[jax.ShapeDtypeStruct((S, Q_COLS), bf16), jax.ShapeDtypeStruct((S, KV_COLS), f32),
                   jax.ShapeDtypeStruct((S, KV_COLS), f32)],
        compiler_params=_params(("parallel", "arbitrary", "arbitrary"), 8 * tq * S * 4 + (8 << 20)),
    )(qr, kr, kv_raw, do)


CONV_HALO = 16


def _fill_padded(pad_ref, val, S):
    pad_ref[pl.ds(0, CONV_HALO), :] = jnp.zeros((CONV_HALO, LANES), f32)
    pad_ref[pl.ds(CONV_HALO + S, CONV_HALO), :] = jnp.zeros((CONV_HALO, LANES), f32)
    pad_ref[pl.ds(CONV_HALO, S), :] = val


def _group_specs(S, n_groups, second_half):
    return pl.BlockSpec((S, LANES), functools.partial(lambda g, o: (0, g + o), o=n_groups if second_half else 0))


def _conv1_fwd(conv_in, wdw, b_dw, name):
    S = conv_in.shape[0]
    ng = CONV_CH // LANES
    R = min(256, S)

    def body(a_ref, g_ref, w_ref, b_ref, o_ref, pad_ref):
        z = a_ref[...].astype(f32) * jax.nn.sigmoid(g_ref[...].astype(f32))
        _fill_padded(pad_ref, z, S)
        for r in range(S // R):
            acc = jnp.zeros((R, LANES), f32) + b_ref[...]
            for j in range(CONV_W):
                acc = acc + w_ref[pl.ds(j, 1), :] * pad_ref[pl.ds(r * R + CONV_HALO - CONV_PAD + j, R), :]
            o_ref[pl.ds(r * R, R), :] = acc

    return _call(
        body, name=name, grid=(ng,),
        in_specs=[_group_specs(S, ng, False), _group_specs(S, ng, True),
                  pl.BlockSpec((CONV_WP, LANES), lambda g: (g, 0)), pl.BlockSpec((1, LANES), lambda g: (0, g))],
        out_specs=pl.BlockSpec((S, LANES), lambda g: (0, g)),
        out_shape=jax.ShapeDtypeStruct((S, CONV_CH), f32),
        scratch_shapes=[pltpu.VMEM((S + 2 * CONV_HALO, LANES), f32)],
        compiler_params=_params(("parallel",), 24 << 20),
    )(conv_in, conv_in, wdw, b_dw)


def _conv1_bwd(conv_in, dc, wdw, name):
    S = conv_in.shape[0]
    ng = CONV_CH // LANES
    R = min(256, S)

    def body(a_ref, g_ref, w_ref, dc_ref, da_ref, dg_ref, dw_ref, db_ref, padz_ref, padd_ref):
        a = a_ref[...].astype(f32)
        sg = jax.nn.sigmoid(g_ref[...].astype(f32))
        _fill_padded(padz_ref, a * sg, S)
        _fill_padded(padd_ref, dc_ref[...], S)
        for r in range(S // R):
            dz = jnp.zeros((R, LANES), f32)
            for j in range(CONV_W):
                dz = dz + w_ref[pl.ds(j, 1), :] * padd_ref[pl.ds(r * R + CONV_HALO + CONV_PAD - j, R), :]
            rows = pl.ds(r * R, R)
            ar, sr = a_ref[rows, :].astype(f32), jax.nn.sigmoid(g_ref[rows, :].astype(f32))
            da_ref[rows, :] = (dz * sr).astype(da_ref.dtype)
            dg_ref[rows, :] = (dz * ar * sr * (1.0 - sr)).astype(dg_ref.dtype)
        for j in range(CONV_W):
            tot = jnp.zeros((1, LANES), f32)
            for r in range(S // R):
                tot = tot + jnp.sum(dc_ref[pl.ds(r * R, R), :] * padz_ref[pl.ds(r * R + CONV_HALO - CONV_PAD + j, R), :],
                                    axis=0, keepdims=True)
            dw_ref[pl.ds(j, 1), :] = tot
        dw_ref[pl.ds(CONV_W, CONV_WP - CONV_W), :] = jnp.zeros((CONV_WP - CONV_W, LANES), f32)
        db_ref[...] = jnp.sum(dc_ref[...], axis=0, keepdims=True)

    return _call(
        body, name=name, grid=(ng,),
        in_specs=[_group_specs(S, ng, False), _group_specs(S, ng, True),
                  pl.BlockSpec((CONV_WP, LANES), lambda g: (g, 0)), pl.BlockSpec((S, LANES), lambda g: (0, g))],
        out_specs=[pl.BlockSpec((S, LANES), lambda g: (0, g)), pl.BlockSpec((S, LANES), lambda g: (0, g)),
                   pl.BlockSpec((CONV_WP, LANES), lambda g: (g, 0)), pl.BlockSpec((1, LANES), lambda g: (0, g))],
        out_shape=[jax.ShapeDtypeStruct((S, CONV_CH), bf16), jax.ShapeDtypeStruct((S, CONV_CH), bf16),
                   jax.ShapeDtypeStruct((ng * CONV_WP, LANES), f32), jax.ShapeDtypeStruct((1, CONV_CH), f32)],
        scratch_shapes=[pltpu.VMEM((S + 2 * CONV_HALO, LANES), f32), pltpu.VMEM((S + 2 * CONV_HALO, LANES), f32)],
        compiler_params=_params(("parallel",), 24 << 20),
    )(conv_in, conv_in, wdw, dc)


def _ln_stats(x, eps=1e-5):
    xc = x - jnp.mean(x, axis=-1, keepdims=True)
    r = lax.rsqrt(jnp.mean(xc * xc, axis=-1, keepdims=True) + eps)
    return r, xc * r


def _ln_bwd(r, xh, dxh):
    return r * (dxh - jnp.mean(dxh, axis=-1, keepdims=True) - xh * jnp.mean(dxh * xh, axis=-1, keepdims=True))


def _conv2_fwd(c, ln_g, ln_b, name):
    def body(c_ref, g_ref, b_ref, o_ref):
        _, xh = _ln_stats(c_ref[...])
        y = xh * g_ref[...] + b_ref[...]
        o_ref[...] = (y * jax.nn.sigmoid(y)).astype(o_ref.dtype)

    return _rows(body, [("r", c), ("f", ln_g), ("f", ln_b)], [("r", CONV_CH, bf16)], tm=min(256, c.shape[0]), name=name)[0]


def _conv2_bwd(c, dcz, ln_g, ln_b, name):
    def body(c_ref, d_ref, g_ref, b_ref, dc_ref, dg_ref, db_ref):
        r, xh = _ln_stats(c_ref[...])
        y = xh * g_ref[...] + b_ref[...]
        sg = jax.nn.sigmoid(y)
        dy = d_ref[...].astype(f32) * (sg * (1.0 + y * (1.0 - sg)))
        dc_ref[...] = _ln_bwd(r, xh, dy * g_ref[...])
        _accumulate(dg_ref, jnp.sum(dy * xh, axis=0, keepdims=True))
        _accumulate(db_ref, jnp.sum(dy, axis=0, keepdims=True))

    return _rows(body, [("r", c), ("r", dcz), ("f", ln_g), ("f", ln_b)],
                 [("r", CONV_CH, f32), ("a", (1, CONV_CH), f32), ("a", (1, CONV_CH), f32)],
                 tm=min(256, c.shape[0]), name=name)


GELU_K = math.sqrt(2.0 / math.pi)
GELU_C = 0.044715


def _gelu(x):
    return 0.5 * x * (1.0 + jnp.tanh(GELU_K * (x + GELU_C * x * x * x)))


def _gelu_grad(x):
    th = jnp.tanh(GELU_K * (x + GELU_C * x * x * x))
    return 0.5 * (1.0 + th) + 0.5 * x * (1.0 - th * th) * (GELU_K * (1.0 + 3.0 * GELU_C * x * x))


def _chunk_rows(n):
    return pl.ds(pl.multiple_of(n * SG_CHUNK, SG_CHUNK), SG_CHUNK)


def _sgu_fwd(sg_in, ln_g, ln_b, w_s, b_s, name):
    S = sg_in.shape[0]

    def body(u_ref, v_ref, lg_ref, lb_ref, w_ref, b_ref, o_ref):
        wb = w_ref[...].astype(bf16)

        def chunk(n, carry):
            rows = _chunk_rows(n)
            gu = _gelu(u_ref[rows, :].astype(f32))
            _, xh = _ln_stats(_gelu(v_ref[rows, :].astype(f32)))
            vl = xh * lg_ref[...] + lb_ref[...]
            t = jnp.dot(wb, vl.astype(bf16), preferred_element_type=f32) + b_ref[...]
            o_ref[rows, :] = (gu * t).astype(o_ref.dtype)
            return carry

        lax.fori_loop(0, S // SG_CHUNK, chunk, 0)

    return _call(
        body, name=name, grid=(SG_G,),
        in_specs=[_group_specs(S, SG_G, False), _group_specs(S, SG_G, True),
                  pl.BlockSpec((1, LANES), lambda g: (0, g)), pl.BlockSpec((1, LANES), lambda g: (0, g)),
                  pl.BlockSpec((None, SG_CHUNK, SG_CHUNK), lambda g: (g, 0, 0)),
                  pl.BlockSpec((None, SG_CHUNK, 1), lambda g: (g, 0, 0))],
        out_specs=pl.BlockSpec((S, LANES), lambda g: (0, g)),
        out_shape=jax.ShapeDtypeStruct((S, SG_CH), bf16),
        compiler_params=_params(("parallel",), 24 << 20),
    )(sg_in, sg_in, ln_g, ln_b, w_s, b_s)


def _sgu_bwd(sg_in, dsz, ln_g, ln_b, w_s, w_s_t, b_s, name):
    S = sg_in.shape[0]

    def body(u_ref, v_ref, lg_ref, lb_ref, w_ref, wt_ref, b_ref, d_ref, du_ref, dv_ref, dw_ref, db_ref, dlg_ref, dlb_ref):
        wb = w_ref[...].astype(bf16)
        wtb = wt_ref[...].astype(bf16)

        def chunk(n, carry):
            dwa, dba, dlga, dlba = carry
            rows = _chunk_rows(n)
            u = u_ref[rows, :].astype(f32)
            v = v_ref[rows, :].astype(f32)
            gu = _gelu(u)
            r, xh = _ln_stats(_gelu(v))
            vlb = (xh * lg_ref[...] + lb_ref[...]).astype(bf16)
            t = jnp.dot(wb, vlb, preferred_element_type=f32) + b_ref[...]
            d = d_ref[rows, :].astype(f32)
            dt = d * gu
            dtb = dt.astype(bf16)
            dwa = dwa + lax.dot_general(dtb, vlb, (((1,), (1,)), ((), ())), preferred_element_type=f32)
            dba = dba + jnp.sum(dt, axis=1, keepdims=True)
            dvl = jnp.dot(wtb, dtb, preferred_element_type=f32)
            dlga = dlga + jnp.sum(dvl * xh, axis=0, keepdims=True)
            dlba = dlba + jnp.sum(dvl, axis=0, keepdims=True)
            dgv = _ln_bwd(r, xh, dvl * lg_ref[...])
            du_ref[rows, :] = (d * t * _gelu_grad(u)).astype(du_ref.dtype)
            dv_ref[rows, :] = (dgv * _gelu_grad(v)).astype(dv_ref.dtype)
            return dwa, dba, dlga, dlba

        init = (jnp.zeros((SG_CHUNK, SG_CHUNK), f32), jnp.zeros((SG_CHUNK, 1), f32),
                jnp.zeros((1, LANES), f32), jnp.zeros((1, LANES), f32))
        dwa, dba, dlga, dlba = lax.fori_loop(0, S // SG_CHUNK, chunk, init)
        dw_ref[...] = dwa
        db_ref[...] = dba
        dlg_ref[...] = dlga
        dlb_ref[...] = dlba

    wspec = pl.BlockSpec((None, SG_CHUNK, SG_CHUNK), lambda g: (g, 0, 0))
    bspec = pl.BlockSpec((None, SG_CHUNK, 1), lambda g: (g, 0, 0))
    lspec = pl.BlockSpec((1, LANES), lambda g: (0, g))
    cspec = pl.BlockSpec((S, LANES), lambda g: (0, g))
    return _call(
        body, name=name, grid=(SG_G,),
        in_specs=[_group_specs(S, SG_G, False), _group_specs(S, SG_G, True), lspec, lspec, wspec, wspec, bspec, cspec],
        out_specs=[cspec, cspec, wspec, bspec, lspec, lspec],
        out_shape=[jax.ShapeDtypeStruct((S, SG_CH), bf16), jax.ShapeDtypeStruct((S, SG_CH), bf16),
                   jax.ShapeDtypeStruct((SG_G, SG_CHUNK, SG_CHUNK), f32), jax.ShapeDtypeStruct((SG_G, SG_CHUNK, 1), f32),
                   jax.ShapeDtypeStruct((1, SG_CH), f32), jax.ShapeDtypeStruct((1, SG_CH), f32)],
        compiler_params=_params(("parallel",), 24 << 20),
    )(sg_in, sg_in, ln_g, ln_b, w_s, w_s_t, b_s, dsz)


def _merge_fwd(gl, b_gate, ya, yc, ys, name):
    D = ya.shape[1]

    def body(gl_ref, b_ref, ya_ref, yc_ref, ys_ref, o_ref):
        acc = jnp.zeros(o_ref.shape, f32)
        for i, y_ref in enumerate((ya_ref, yc_ref, ys_ref)):
            cols = slice(i * D, (i + 1) * D)
            acc = acc + jax.nn.sigmoid(gl_ref[:, cols].astype(f32) + b_ref[:, cols]) * y_ref[...].astype(f32)
        o_ref[...] = acc.astype(o_ref.dtype)

    return _rows(body, [("r", gl), ("f", b_gate), ("r", ya), ("r", yc), ("r", ys)], [("r", D, bf16)],
                 tm=min(128, gl.shape[0]), name=name)[0]


def _merge_bwd(dm, gl, b_gate, ya, yc, ys, name):
    D = ya.shape[1]

    def body(dm_ref, gl_ref, b_ref, ya_ref, yc_ref, ys_ref, dgl_ref, dya_ref, dyc_ref, dys_ref, db_ref):
        dm_ = dm_ref[...].astype(f32)
        for i, (y_ref, dy_ref) in enumerate(((ya_ref, dya_ref), (yc_ref, dyc_ref), (ys_ref, dys_ref))):
            cols = slice(i * D, (i + 1) * D)
            gate = jax.nn.sigmoid(gl_ref[:, cols].astype(f32) + b_ref[:, cols])
            dy_ref[...] = (dm_ * gate).astype(dy_ref.dtype)
            dlog = dm_ * y_ref[...].astype(f32) * gate * (1.0 - gate)
            dgl_ref[:, cols] = dlog.astype(dgl_ref.dtype)
            part = jnp.sum(dlog, axis=0, keepdims=True)
            first = pl.program_id(0) == 0

            @pl.when(first)
            def _():
                db_ref[:, cols] = part

            @pl.when(jnp.logical_not(first))
            def _():
                db_ref[:, cols] += part

    return _rows(body, [("r", dm), ("r", gl), ("f", b_gate), ("r", ya), ("r", yc), ("r", ys)],
                 [("r", 3 * D, bf16), ("r", D, bf16), ("r", D, bf16), ("r", D, bf16), ("a", (1, 3 * D), f32)],
                 tm=min(128, gl.shape[0]), name=name)


def _swiglu_fwd(fg, fu, name):
    def body(g_ref, u_ref, o_ref):
        g = g_ref[...].astype(f32)
        o_ref[...] = (g * jax.nn.sigmoid(g) * u_ref[...].astype(f32)).astype(o_ref.dtype)

    return _rows(body, [("r", fg), ("r", fu)], [("r", fg.shape[1], bf16)], tm=min(128, fg.shape[0]), name=name)[0]


def _swiglu_bwd(dact, fg, fu, name):
    def body(d_ref, g_ref, u_ref, dg_ref, du_ref):
        d = d_ref[...].astype(f32)
        g = g_ref[...].astype(f32)
        sg = jax.nn.sigmoid(g)
        dg_ref[...] = (d * u_ref[...].astype(f32) * sg * (1.0 + g * (1.0 - sg))).astype(dg_ref.dtype)
        du_ref[...] = (d * g * sg).astype(du_ref.dtype)

    return _rows(body, [("r", dact), ("r", fg), ("r", fu)], [("r", fg.shape[1], bf16), ("r", fg.shape[1], bf16)],
                 tm=min(128, fg.shape[0]), name=name)


def _row_tile(r, c, n_arrays, itemsize=4):
    for tm in (512, 256, 128, 64, 32, 16, 8):
        if r % tm == 0 and 2 * n_arrays * tm * c * itemsize <= (24 << 20):
            return tm
    return r


def _sum_slots(slots, name):
    n, r, c = slots.shape
    tm = _row_tile(r, c, n + 2)

    def body(s_ref, o_ref):
        acc = s_ref[0].astype(f32)
        for k in range(1, n):
            acc = acc + s_ref[k].astype(f32)
        o_ref[...] = acc

    return _call(body, name=name, grid=(r // tm,),
                 in_specs=[pl.BlockSpec((n, tm, c), lambda i: (0, i, 0))],
                 out_specs=pl.BlockSpec((tm, c), lambda i: (i, 0)),
                 out_shape=jax.ShapeDtypeStruct((r, c), f32),
                 compiler_params=_params(("parallel",), 40 << 20))(slots)


def _add_sibling(g4, recv, core, name):
    _, _, r, c = g4.shape
    tm = _row_tile(r, c, 3, 2)

    def body(core_ref, g_ref, r_ref, o_ref):
        o_ref[...] = (g_ref[...].astype(f32) + r_ref[...].astype(f32)).astype(o_ref.dtype)

    grid_spec = pltpu.PrefetchScalarGridSpec(
        num_scalar_prefetch=1, grid=(N_CHIP, r // tm),
        in_specs=[pl.BlockSpec((None, None, tm, c), lambda k, i, core_ref: (k, core_ref[0], i, 0)),
                  pl.BlockSpec((None, tm, c), lambda k, i, core_ref: (k, i, 0))],
        out_specs=pl.BlockSpec((None, tm, c), lambda k, i, core_ref: (k, i, 0)))
    return _call(body, name=name, grid_spec=grid_spec, out_shape=jax.ShapeDtypeStruct((N_CHIP, r, c), bf16),
                 compiler_params=_params(("parallel", "parallel"), 40 << 20))(core, g4, recv)


def _adamw(w, g, m, v, name):
    L, r, c = w.shape
    tm = _row_tile(r, c, 7)
    c1 = 1.0 - ADAM_B1 ** ADAM_STEP
    c2 = 1.0 - ADAM_B2 ** ADAM_STEP

    def body(w_ref, g_ref, m_ref, v_ref, d_ref, mo_ref, vo_ref):
        g_ = g_ref[...]
        m_ = ADAM_B1 * m_ref[...] + (1.0 - ADAM_B1) * g_
        v_ = ADAM_B2 * v_ref[...] + (1.0 - ADAM_B2) * (g_ * g_)
        d_ref[...] = -ADAM_LR * ((m_ / c1) / (jnp.sqrt(v_ / c2) + ADAM_EPS) + ADAM_WD * w_ref[...])
        mo_ref[...] = m_
        vo_ref[...] = v_

    spec = pl.BlockSpec((None, tm, c), lambda l, i: (l, i, 0))
    shp = jax.ShapeDtypeStruct((L, r, c), f32)
    return _call(body, name=name, grid=(L, r // tm), in_specs=[spec] * 4, out_specs=[spec] * 3,
                 out_shape=[shp] * 3, compiler_params=_params(("parallel", "parallel"), 40 << 20))(w, g, m, v)


def _mesh_pos():
    return lax.axis_index("x"), lax.axis_index("y"), lax.axis_index("c")


def _all_gather(shards, name):
    n = len(shards)

    def body(*refs):
        x_refs, o_refs = refs[:n], refs[n:2 * n]
        send_sems, recv_sems, local_sems = refs[2 * n:]
        x, y, c = _mesh_pos()
        me, sibling = (x, y, c), (x, y, 1 - c)
        chips = [(1 - x, y), (x, 1 - y), (1 - x, 1 - y)]

        def rows(k, px, py, pc):
            r = shards[k].shape[0]
            return o_refs[k].at[pl.ds((4 * px + 2 * py + pc) * r, r), :]

        def copy(k, s, block, to, src=None):
            return pltpu.make_async_remote_copy(
                src_ref=rows(k, *block) if src is None else src, dst_ref=rows(k, *block),
                send_sem=send_sems.at[k, s], recv_sem=recv_sems.at[k, s], device_id=to, device_id_type=MESH)

        mine = [pltpu.make_async_copy(x_refs[k], rows(k, *me), local_sems.at[k]) for k in range(n)]
        for cp in mine:
            cp.start()
        first = [copy(k, 0, me, sibling, src=x_refs[k]) for k in range(n)]
        for j, chip in enumerate(chips):
            first += [copy(k, 1 + j, me, (*chip, c), src=x_refs[k]) for k in range(n)]
        for cp in first:
            cp.start()
        passed = []
        for j, chip in enumerate(chips):
            for k in range(n):
                copy(k, 1 + j, (*chip, c), me).wait_recv()
                fwd = copy(k, 4 + j, (*chip, c), sibling)
                fwd.start()
                passed.append(fwd)
        for k in range(n):
            copy(k, 0, sibling, me).wait_recv()
        for j, chip in enumerate(chips):
            for k in range(n):
                copy(k, 4 + j, (*chip, 1 - c), me).wait_recv()
        for cp in first + passed:
            cp.wait_send()
        for cp in mine:
            cp.wait()

    return _call(
        body, name=name, in_specs=[HBM] * n, out_specs=[HBM] * n,
        out_shape=[jax.ShapeDtypeStruct((N_DEV * s.shape[0], s.shape[1]), s.dtype) for s in shards],
        scratch_shapes=[pltpu.SemaphoreType.DMA((n, 7)), pltpu.SemaphoreType.DMA((n, 7)), pltpu.SemaphoreType.DMA((n,))],
    )(*shards)


def _send_to_sibling(g4s, name):
    n = len(g4s)

    def body(*refs):
        g_refs, o_refs = refs[:n], refs[n:2 * n]
        send_sems, recv_sems = refs[2 * n:]
        x, y, c = _mesh_pos()
        copies = [pltpu.make_async_remote_copy(
            src_ref=g_refs[k].at[:, 1 - c], dst_ref=o_refs[k], send_sem=send_sems.at[k], recv_sem=recv_sems.at[k],
            device_id=(x, y, 1 - c), device_id_type=MESH) for k in range(n)]
        for cp in copies:
            cp.start()
        for cp in copies:
            cp.wait()

    return _call(
        body, name=name, in_specs=[HBM] * n, out_specs=[HBM] * n,
        out_shape=[jax.ShapeDtypeStruct((N_CHIP,) + g.shape[2:], g.dtype) for g in g4s],
        scratch_shapes=[pltpu.SemaphoreType.DMA((n,)), pltpu.SemaphoreType.DMA((n,))],
    )(*g4s)


SEM = pl.BlockSpec(memory_space=pltpu.SEMAPHORE)
ANY = pl.BlockSpec(memory_space=pl.ANY)
EFFECT = pltpu.SideEffectType.DATAFLOW_SIDE_EFFECTING


def _other_chips(x, y):
    return [(1 - x, y), (x, 1 - y), (1 - x, 1 - y)]


def _ici_copy(kind, src_ref, land_ref, send_sem, recv_sem, sender, target, c):
    (sx, sy), (tx, ty) = sender, target
    if kind == "gather":
        r = src_ref.shape[0]
        src, dst = src_ref, land_ref.at[pl.ds((4 * sx + 2 * sy + c) * r, r), :]
    else:
        src, dst = src_ref.at[2 * tx + ty], land_ref.at[2 * sx + sy]
    return pltpu.make_async_remote_copy(src_ref=src, dst_ref=dst, send_sem=send_sem, recv_sem=recv_sem,
                                        device_id=(tx, ty, c), device_id_type=MESH)


def _ici_start(kind, srcs, lands, after, name):
    n = len(srcs)

    def body(*refs):
        src_refs, land_refs = refs[:n], refs[n:2 * n]
        send_sems, recv_sems = refs[2 * n + 1], refs[2 * n + 2]
        token = refs[-1]
        x, y, c = _mesh_pos()
        for j, chip in enumerate(_other_chips(x, y)):
            for k in range(n):
                _ici_copy(kind, src_refs[k], land_refs[k], send_sems.at[3 * k + j], recv_sems.at[3 * k + j],(x, y), chip, c).start()
        token[...] = jnp.zeros_like(token)

    bufs = list(srcs) + list(lands)
    return _call(
        body, name=name,
        out_shape=(pltpu.SemaphoreType.DMA((3 * n,)), pltpu.SemaphoreType.DMA((3 * n,)),
                   *[pltpu.HBM(b.shape, b.dtype) for b in bufs], jax.ShapeDtypeStruct((8, LANES), f32)),
        in_specs=[HBM] * (2 * n) + [ANY], out_specs=(SEM, SEM, *[HBM] * (2 * n), pl.BlockSpec(memory_space=pltpu.VMEM)),
        input_output_aliases={i: 2 + i for i in range(2 * n)},
        compiler_params=pltpu.CompilerParams(has_side_effects=EFFECT),
    )(*[pltpu.with_memory_space_constraint(b, pltpu.HBM) for b in bufs], after)


def _ici_wait(kind, started, after, name):
    send_sems, recv_sems, *bufs = started[:-1]
    n = len(bufs) // 2

    def body(*refs):
        src_refs, land_refs = refs[:n], refs[n:2 * n]
        send_sems, recv_sems = refs[2 * n], refs[2 * n + 1]
        x, y, c = _mesh_pos()
        for j, chip in enumerate(_other_chips(x, y)):
            for k in range(n):
                _ici_copy(kind, src_refs[k], land_refs[k], send_sems.at[3 * k + j], recv_sems.at[3 * k + j],(x, y), chip, c).wait_send()
                _ici_copy(kind, src_refs[k], land_refs[k], send_sems.at[3 * k + j], recv_sems.at[3 * k + j],chip, (x, y), c).wait_recv()

    out = _call(
        body, name=name, out_shape=[pltpu.HBM(b.shape, b.dtype) for b in bufs],
        in_specs=[HBM] * (2 * n) + [SEM, SEM, ANY], out_specs=[HBM] * (2 * n),
        input_output_aliases={i: i for i in range(2 * n)},
        compiler_params=pltpu.CompilerParams(has_side_effects=EFFECT),
    )(*bufs, send_sems, recv_sems, after)
    return out[:n], out[n:]


def _d2d_gather(shards, lands, name):
    n = len(shards)

    def body(*refs):
        x_refs, in_refs, o_refs = refs[:n], refs[n:2 * n], refs[2 * n:3 * n]
        send_sems, recv_sems, local_sems = refs[3 * n:]
        x, y, c = _mesh_pos()
        sibling = (x, y, 1 - c)
        blocks = [(x, y)] + _other_chips(x, y)

        def rows(ref, k, px, py, pc):
            r = shards[k].shape[0]
            return ref.at[pl.ds((4 * px + 2 * py + pc) * r, r), :]

        mine = [pltpu.make_async_copy(x_refs[k], rows(o_refs[k], k, x, y, c), local_sems.at[k]) for k in range(n)]
        for cp in mine:
            cp.start()
        copies = []
        for s, (px, py) in enumerate(blocks):
            for k in range(n):
                src = x_refs[k] if s == 0 else rows(in_refs[k], k, px, py, c)
                copies.append(pltpu.make_async_remote_copy(
                    src_ref=src, dst_ref=rows(o_refs[k], k, px, py, c), send_sem=send_sems.at[k, s],
                    recv_sem=recv_sems.at[k, s], device_id=sibling, device_id_type=MESH))
        for cp in copies:
            cp.start()
        for s, (px, py) in enumerate(blocks):
            for k in range(n):
                pltpu.make_async_remote_copy(
                    src_ref=x_refs[k], dst_ref=rows(o_refs[k], k, px, py, 1 - c), send_sem=send_sems.at[k, s],
                    recv_sem=recv_sems.at[k, s], device_id=sibling, device_id_type=MESH).wait_recv()
        for cp in copies:
            cp.wait_send()
        for cp in mine:
            cp.wait()

    return _call(
        body, name=name, in_specs=[HBM] * (2 * n), out_specs=[HBM] * n,
        out_shape=[jax.ShapeDtypeStruct(b.shape, b.dtype) for b in lands],
        input_output_aliases={n + k: k for k in range(n)},
        scratch_shapes=[pltpu.SemaphoreType.DMA((n, 4)), pltpu.SemaphoreType.DMA((n, 4)), pltpu.SemaphoreType.DMA((n,))],
    )(*shards, *lands)


def _sum_chip_slots(lands, sums, chip, name):
    _, r, c = lands.shape
    tm = _row_tile(r, c, 10, 2)

    def body(chip_ref, l_ref, s_ref, o_ref):
        acc = None
        for k in range(N_CHIP):
            part = jnp.where(chip_ref[0] == k, s_ref[k], l_ref[k]).astype(f32)
            acc = part if acc is None else acc + part
        o_ref[...] = acc

    grid_spec = pltpu.PrefetchScalarGridSpec(
        num_scalar_prefetch=1, grid=(r // tm,),
        in_specs=[pl.BlockSpec((N_CHIP, tm, c), lambda i, chip_ref: (0, i, 0)),
                  pl.BlockSpec((N_CHIP, tm, c), lambda i, chip_ref: (0, i, 0))],
        out_specs=pl.BlockSpec((tm, c), lambda i, chip_ref: (i, 0)))
    return _call(body, name=name, grid_spec=grid_spec, out_shape=jax.ShapeDtypeStruct((r, c), f32),
                 compiler_params=_params(("parallel",), 40 << 20))(chip, lands, sums)


def _reduce_start(grads, core, after, tag):
    g4s = [g.reshape(N_CHIP, 2, g.shape[0] // N_DEV, g.shape[1]) for g in grads]
    recv = _send_to_sibling(g4s, name="rs_sibling_" + tag)
    sums = [_add_sibling(g4, rv, core, name="rs_add_" + tag) for g4, rv in zip(g4s, recv)]
    lands = [lax.empty(s.shape, s.dtype) for s in sums]
    return _ici_start("reduce", sums, lands, after, name="rs_start_" + tag)


def _reduce_finish(started, chip, after, tag):
    sums, lands = _ici_wait("reduce", started, after, name="rs_wait_" + tag)
    return [_sum_chip_slots(ld, s, chip, name="rs_sum_" + tag) for ld, s in zip(lands, sums)]


def _rope_tables(S):
    rows = S // GRID_W
    row = jnp.repeat(jnp.arange(rows, dtype=f32), GRID_W)
    col = jnp.tile(jnp.arange(GRID_W, dtype=f32), rows)
    nf = HEAD_DIM // 4
    inv = ROPE_THETA ** (-jnp.arange(nf, dtype=f32) / nf)
    ang = jnp.concatenate([row[:, None] * inv, col[:, None] * inv], axis=-1)
    cos, sin = jnp.cos(ang), jnp.sin(ang)
    return jnp.concatenate([cos, cos], axis=-1), jnp.concatenate([-sin, sin], axis=-1)


def _layer_fwd(xin, p, w, rest_of_weights, cos2, sin2):
    sv = {"xin": xin}
    h = sv["h"] = _rms_fwd(xin, p["g_mix"], name="rms_mix")
    proj = functools.partial(_mm, h, w["in"], "nt", bf16)
    q_raw = sv["q_raw"] = proj(n=Q_COLS, b_off=0, name="proj_q")
    kv_raw = sv["kv_raw"] = proj(n=2 * KV_COLS, b_off=OFF_KV, name="proj_kv")
    conv_in = sv["conv_in"] = proj(n=2 * CONV_CH, b_off=OFF_CONV, name="proj_conv")
    sg_in = sv["sg_in"] = proj(n=2 * SG_CH, b_off=OFF_SG, name="proj_sg")
    gl = sv["gl"] = proj(n=3 * D_MODEL, b_off=OFF_GATE, name="proj_gate")
    qr, kr = sv["qr"], sv["kr"] = _qk_fwd(q_raw, kv_raw, p["q_norm_g"], p["k_norm_g"], cos2, sin2, name="qk_fwd")
    o = sv["o"] = _attn_fwd(qr, kr, kv_raw, name="attn_fwd")
    c = sv["c"] = _conv1_fwd(conv_in, w["dw"], p["b_dw"], name="conv1_fwd")
    cz = sv["cz"] = _conv2_fwd(c, p["conv_ln_g"], p["conv_ln_b"], name="conv2_fwd")
    sz = sv["sz"] = _sgu_fwd(sg_in, p["sg_ln_g"], p["sg_ln_b"], p["w_s"], p["b_s"], name="sgu_fwd")
    w = rest_of_weights(sz)
    ya = sv["ya"] = _mm(o, w["attn_o"], "nt", bf16, name="out_attn")
    yc = sv["yc"] = _mm(cz, w["conv_o"], "nt", bf16, name="out_conv")
    ys = sv["ys"] = _mm(sz, w["sg_o"], "nt", bf16, name="out_sg")
    merged = sv["merged"] = _merge_fwd(gl, p["b_gate"], ya, yc, ys, name="merge_fwd")
    x1 = sv["x1"] = _mm(merged, w["out"], "nn", f32, res=xin, name="out_proj")
    hf = sv["hf"] = _rms_fwd(x1, p["g_ffn"], name="rms_ffn")
    fg = sv["fg"] = _mm(hf, w["ff_gate"], "nt", bf16, name="ff_gate")
    fu = sv["fu"] = _mm(hf, w["ff_up"], "nt", bf16, name="ff_up")
    act = sv["act"] = _swiglu_fwd(fg, fu, name="swiglu_fwd")
    x2 = _mm(act, w["ff_down"], "nn", f32, res=x1, name="ff_down")
    return x2, sv, w


def _layer_bwd(dx2, dx2b, sv, p, w, cos2, sin2, reduce_start):
    small = {}
    dact = _mm(dx2b, w["ff_down"], "nt", bf16, name="d_act")
    g_down = _mm(sv["act"], dx2b, "tn", bf16, name="g_ff_down")
    dfg, dfu = _swiglu_bwd(dact, sv["fg"], sv["fu"], name="swiglu_bwd")
    dhf = _mm(dfg, w["ff_gate"], "nn", f32, name="d_hf_gate")
    dhf = _mm(dfu, w["ff_up"], "nn", f32, res=dhf, name="d_hf_up")
    g_gate = _mm(dfg, sv["hf"], "tn", bf16, name="g_ff_gate")
    g_up = _mm(dfu, sv["hf"], "tn", bf16, name="g_ff_up")
    zero = reduce_start("ffn", dict(w_ff_gate=g_gate, w_ff_up=g_up, w_ff_down=g_down))
    dx1, dx1b, small["g_ffn"] = _rms_bwd(sv["x1"], p["g_ffn"] + zero, dhf, dx2, name="rms_ffn_bwd")
    dmerged = _mm(dx1b, w["out"], "nt", bf16, name="d_merged")
    g_out = _mm(sv["merged"], dx1b, "tn", bf16, name="g_out")
    dgl, dya, dyc, dys, small["b_gate"] = _merge_bwd(dmerged, sv["gl"], p["b_gate"], sv["ya"], sv["yc"], sv["ys"],
                                                    name="merge_bwd")
    do = _mm(dya, w["attn_o"], "nn", bf16, name="d_o")
    g_ao = _mm(dya, sv["o"], "tn", bf16, name="g_attn_o")
    dcz = _mm(dyc, w["conv_o"], "nn", bf16, name="d_cz")
    g_co = _mm(dyc, sv["cz"], "tn", bf16, name="g_conv_o")
    dsz = _mm(dys, w["sg_o"], "nn", bf16, name="d_sz")
    g_so = _mm(dys, sv["sz"], "tn", bf16, name="g_sg_o")
    zero = reduce_start("mix", dict(w_attn_o=g_ao, w_conv_o=g_co, w_sg_o=g_so, w_out=g_out))
    dsu, dsv, small["w_s"], small["b_s"], small["sg_ln_g"], small["sg_ln_b"] = _sgu_bwd(
        sv["sg_in"], dsz, p["sg_ln_g"] + zero, p["sg_ln_b"], p["w_s"], p["w_s_t"], p["b_s"], name="sgu_bwd")
    dc, small["conv_ln_g"], small["conv_ln_b"] = _conv2_bwd(sv["c"], dcz, p["conv_ln_g"], p["conv_ln_b"], name="conv2_bwd")
    da, dgt, small["w_dw"], small["b_dw"] = _conv1_bwd(sv["conv_in"], dc, w["dw"], name="conv1_bwd")
    dqr, dkr, dv = _attn_bwd(sv["qr"], sv["kr"], sv["kv_raw"], do, name="attn_bwd")
    dq_raw, dk_raw, small["q_norm_g"], small["k_norm_g"] = _qk_bwd(
        sv["q_raw"], sv["kv_raw"], dqr, dkr, p["q_norm_g"], p["k_norm_g"], cos2, sin2, name="qk_bwd")
    dproj = jnp.concatenate([dq_raw, dk_raw, dv.astype(bf16), da, dgt, dsu, dsv, dgl], axis=1)
    dh = _mm(dproj, w["in"], "nn", f32, name="d_h")
    g_in = _mm(dproj, sv["h"], "tn", bf16, name="g_in")
    zero = reduce_start("in", dict(w_in=g_in))
    dx, dxb, small["g_mix"] = _rms_bwd(sv["xin"], p["g_mix"] + zero, dh, dx1, name="rms_mix_bwd")
    return dx, dxb, small


SMALL = ("g_mix", "b_gate", "q_norm_g", "k_norm_g", "b_dw", "conv_ln_g", "conv_ln_b", "sg_ln_g", "sg_ln_b",
         "w_s", "b_s", "g_ffn")
PACK_ALIGN = 8 * LANES


def _pack(parts):
    flat = jnp.concatenate([a.reshape(-1).astype(f32) for a in parts])
    pad = -flat.shape[0] % PACK_ALIGN
    return jnp.pad(flat, (0, pad)).reshape(-1, LANES)


def _unpack(buf, shapes):
    flat = buf.reshape(-1)
    out, pos = [], 0
    for shp in shapes:
        size = math.prod(shp)
        out.append(flat[pos:pos + size].reshape(shp))
        pos += size
    return out


def kernel(x, g_mix, w_in, b_gate, q_norm_g, k_norm_g, w_attn_o, w_dw, b_dw, conv_ln_g, conv_ln_b, w_conv_o, sg_ln_g, sg_ln_b, w_s, b_s, w_sg_o, w_out, g_ffn, w_ff_gate, w_ff_up, w_ff_down, g_final, loss_target, m_g_mix, m_w_in, m_b_gate, m_q_norm_g, m_k_norm_g, m_w_attn_o, m_w_dw, m_b_dw, m_conv_ln_g, m_conv_ln_b, m_w_conv_o, m_sg_ln_g, m_sg_ln_b, m_w_s, m_b_s, m_w_sg_o, m_w_out, m_g_ffn, m_w_ff_gate, m_w_ff_up, m_w_ff_down, m_g_final, v_g_mix, v_w_in, v_b_gate, v_q_norm_g, v_k_norm_g, v_w_attn_o, v_w_dw, v_b_dw, v_conv_ln_g, v_conv_ln_b, v_w_conv_o, v_sg_ln_g, v_sg_ln_b, v_w_s, v_b_s, v_w_sg_o, v_w_out, v_g_ffn, v_w_ff_gate, v_w_ff_up, v_w_ff_down, v_g_final):
    weights = dict(g_mix=g_mix, w_in=w_in, b_gate=b_gate, q_norm_g=q_norm_g, k_norm_g=k_norm_g, w_attn_o=w_attn_o,
                   w_dw=w_dw, b_dw=b_dw, conv_ln_g=conv_ln_g, conv_ln_b=conv_ln_b, w_conv_o=w_conv_o, sg_ln_g=sg_ln_g,
                   sg_ln_b=sg_ln_b, w_s=w_s, b_s=b_s, w_sg_o=w_sg_o, w_out=w_out, g_ffn=g_ffn, w_ff_gate=w_ff_gate,
                   w_ff_up=w_ff_up, w_ff_down=w_ff_down, g_final=g_final)
    mom_m = dict(g_mix=m_g_mix, w_in=m_w_in, b_gate=m_b_gate, q_norm_g=m_q_norm_g, k_norm_g=m_k_norm_g,
                 w_attn_o=m_w_attn_o, w_dw=m_w_dw, b_dw=m_b_dw, conv_ln_g=m_conv_ln_g, conv_ln_b=m_conv_ln_b,
                 w_conv_o=m_w_conv_o, sg_ln_g=m_sg_ln_g, sg_ln_b=m_sg_ln_b, w_s=m_w_s, b_s=m_b_s, w_sg_o=m_w_sg_o,
                 w_out=m_w_out, g_ffn=m_g_ffn, w_ff_gate=m_w_ff_gate, w_ff_up=m_w_ff_up, w_ff_down=m_w_ff_down,
                 g_final=m_g_final)
    mom_v = dict(g_mix=v_g_mix, w_in=v_w_in, b_gate=v_b_gate, q_norm_g=v_q_norm_g, k_norm_g=v_k_norm_g,
                 w_attn_o=v_w_attn_o, w_dw=v_w_dw, b_dw=v_b_dw, conv_ln_g=v_conv_ln_g, conv_ln_b=v_conv_ln_b,
                 w_conv_o=v_w_conv_o, sg_ln_g=v_sg_ln_g, sg_ln_b=v_sg_ln_b, w_s=v_w_s, b_s=v_b_s, w_sg_o=v_w_sg_o,
                 w_out=v_w_out, g_ffn=v_g_ffn, w_ff_gate=v_w_ff_gate, w_ff_up=v_w_ff_up, w_ff_down=v_w_ff_down,
                 g_final=v_g_final)
    S, D = x.shape[1], x.shape[2]
    xi, yi, ci = _mesh_pos()
    me = 4 * xi + 2 * yi + ci
    core = jnp.reshape(ci, (1,)).astype(jnp.int32)
    cos2, sin2 = _rope_tables(S)

    big = ("w_in", "w_attn_o", "w_conv_o", "w_sg_o", "w_out", "w_ff_gate", "w_ff_up", "w_ff_down")
    transposed = {"w_in", "w_attn_o", "w_conv_o", "w_sg_o", "w_ff_gate", "w_ff_up"}
    chip = jnp.reshape(2 * xi + yi, (1,)).astype(jnp.int32)
    groups = (("in", "dw"), ("attn_o", "conv_o", "sg_o", "out", "ff_gate", "ff_up", "ff_down"))
    P, shards = [], []
    for l in range(DEPTH):
        sh = {n[2:]: (weights[n][l].T if n in transposed else weights[n][l]).astype(bf16) for n in big}
        sh["dw"] = jnp.pad(w_dw[l].reshape(CONV_W, LANES), ((0, CONV_WP - CONV_W), (0, 0)))
        shards.append(sh)
        p = {n: weights[n][l].reshape(1, -1) for n in SMALL if n not in ("w_s", "b_s")}
        p["w_s"] = w_s[l]
        p["w_s_t"] = jnp.swapaxes(w_s[l], 1, 2)
        p["b_s"] = b_s[l].reshape(SG_G, SG_CHUNK, 1)
        P.append(p)

    gathers, after = {}, cos2
    for l in range(DEPTH):
        for gi, names in enumerate(groups):
            srcs = [shards[l][n] for n in names]
            lands = [lax.empty((N_DEV * s.shape[0], s.shape[1]), s.dtype) for s in srcs]
            gathers[l, gi] = _ici_start("gather", srcs, lands, after, name=f"ag_start_{l}{gi}")
            after = gathers[l, gi][-1]
    all_started = after[0, 0]

    def gathered(l, gi, after):
        srcs, lands = _ici_wait("gather", gathers[l, gi], after, name=f"ag_wait_{l}{gi}")
        return dict(zip(groups[gi], _d2d_gather(srcs, lands, name=f"ag_d2d_{gi}")))

    h = x.reshape(S, D)
    saved, W = [], []
    for l in range(DEPTH):
        first = gathered(l, 0, after if l == 0 else h)
        if l == 0:
            P[l]["g_mix"] = P[l]["g_mix"] + all_started
        h, sv, w = _layer_fwd(h, P[l], first, lambda z, l=l, first=first: {**first, **gathered(l, 1, z)}, cos2, sin2)
        saved.append(sv)
        W.append(w)
    dx, dxb, sq, g_final_part = _final_loss(h, g_final.reshape(1, D), loss_target.reshape(S, D), name="final_loss")
    loss = lax.psum(0.5 * jnp.sum(sq) / D, ("x", "y", "c"))

    reductions, small_grads = {}, [None] * DEPTH
    for l in reversed(range(DEPTH)):
        def reduce_start(group, grads, l=l):
            names = tuple(grads)
            started = _reduce_start([grads[n] for n in names], core, grads[names[0]], tag=f"{group}{l}")
            reductions[l, group] = (names, started)
            return started[-1][0, 0]

        dx, dxb, small_grads[l] = _layer_bwd(dx, dxb, saved[l], P[l], W[l], cos2, sin2, reduce_start)
    grad_x = dx.reshape(x.shape)
    big_grads = [{} for _ in range(DEPTH)]
    for (l, group), (names, started) in reductions.items():
        big_grads[l].update(zip(names, _reduce_finish(started, chip, dx, tag=f"{group}{l}")))

    small_shapes = [weights[n].shape for n in SMALL] + [g_final.shape, (DEPTH, CONV_CH // LANES, CONV_WP, LANES)]
    parts = [jnp.stack([small_grads[l][n].reshape(weights[n].shape[1:]) for l in range(DEPTH)]) for n in SMALL]
    parts += [g_final_part.reshape(g_final.shape), jnp.stack([small_grads[l]["w_dw"] for l in range(DEPTH)])]
    packed = _pack(parts)
    gathered = _all_gather([packed], name="gather_small")[0]
    total = _sum_slots(gathered.reshape(N_DEV, packed.shape[0], LANES), name="sum_small")
    small_total = _unpack(total, small_shapes)
    grads_out = dict(zip(SMALL + ("g_final",), small_total[:-1]))
    dw_full = small_total[-1]
    grads_out["w_dw"] = lax.dynamic_index_in_dim(dw_full, me, axis=1, keepdims=False)[:, :CONV_W].reshape(w_dw.shape)

    for n in big:
        per_layer = [big_grads[l][n] for l in range(DEPTH)]
        grads_out[n] = jnp.stack([g.T if n in transposed else g for g in per_layer])

    delta, new_m, new_v = {}, {}, {}
    for n in big:
        delta[n], new_m[n], new_v[n] = _adamw(weights[n], grads_out[n], mom_m[n], mom_v[n], name="adamw_" + n)
    rep = SMALL + ("g_final",)
    rep_shapes = [weights[n].shape for n in rep]
    packs = [_pack([src[n] for n in rep])[None] for src in (weights, grads_out, mom_m, mom_v)]
    for dst, buf in zip((delta, new_m, new_v), _adamw(*packs, name="adamw_small")):
        dst.update(zip(rep, _unpack(buf[0], rep_shapes)))
    flat = lambda a: a.reshape(1, DEPTH * CONV_W, LANES)
    for dst, buf in zip((delta, new_m, new_v),
                        _adamw(flat(w_dw), flat(grads_out["w_dw"]), flat(m_w_dw), flat(v_w_dw), name="adamw_w_dw")):
        dst["w_dw"] = buf.reshape(w_dw.shape)

    order = ("g_mix", "w_in", "b_gate", "q_norm_g", "k_norm_g", "w_attn_o", "w_dw", "b_dw", "conv_ln_g", "conv_ln_b",
             "w_conv_o", "sg_ln_g", "sg_ln_b", "w_s", "b_s", "w_sg_o", "w_out", "g_ffn", "w_ff_gate", "w_ff_up",
             "w_ff_down", "g_final")
    return (loss, grad_x, *[grads_out[n] for n in order], *[delta[n] for n in order],
            *[new_m[n] for n in order], *[new_v[n] for n in order])
```

```python
import functools
import math

import jax
import jax.numpy as jnp
from jax import lax
from jax.experimental import pallas as pl
from jax.experimental.pallas import tpu as pltpu

f32, bf16 = jnp.float32, jnp.bfloat16

D_MODEL = 2048
SEQ = 2048
DEPTH = 2
GRID_W = 64
HEAD_DIM = 128
LANES = 128
N_Q = (D_MODEL // 2) // HEAD_DIM
N_KV = N_Q // 4
GRP = N_Q // N_KV
Q_COLS = N_Q * HEAD_DIM
KV_COLS = N_KV * HEAD_DIM
CONV_CH = D_MODEL // 2
CONV_W = 31
CONV_PAD = CONV_W // 2
CONV_WP = 32
SG_CH = D_MODEL // 2
SG_G = SG_CH // LANES
SG_CHUNK = 128
D_FF = -(-8 * D_MODEL // (3 * 256)) * 256
OFF_KV = Q_COLS
OFF_CONV = OFF_KV + 2 * KV_COLS
OFF_SG = OFF_CONV + 2 * CONV_CH
OFF_GATE = OFF_SG + 2 * SG_CH
IN_COLS = OFF_GATE + 3 * D_MODEL
ROPE_THETA = 10000.0
SCALE = HEAD_DIM ** -0.5
N_DEV = 8
N_CHIP = 4

ADAM_LR, ADAM_B1, ADAM_B2, ADAM_EPS, ADAM_WD, ADAM_STEP = 0.001, 0.9, 0.999, 1e-08, 0.01, 10

VMEM_BYTES_V7X = 64 << 20
VMEM_CAP = VMEM_BYTES_V7X - (6 << 20)
MESH = pl.DeviceIdType.MESH
HBM = pl.BlockSpec(memory_space=pltpu.HBM)


def _in_hbm(a):
    if isinstance(a, jax.Array) and jnp.issubdtype(a.dtype, jnp.floating) and a.size * a.dtype.itemsize >= (1 << 20):
        return pltpu.with_memory_space_constraint(a, pltpu.HBM)
    return a


def _call(body, **kw):
    call = pl.pallas_call(body, **kw)
    return lambda *args: call(*[_in_hbm(a) for a in args])


def _pick(n, cands):
    for c in cands:
        if n % c == 0:
            return c
    raise ValueError((n, cands))


def _params(sem, vmem_bytes):
    return pltpu.CompilerParams(dimension_semantics=sem, vmem_limit_bytes=int(min(max(vmem_bytes, 16 << 20), VMEM_CAP)))


def _mm(a, b, form, out_dtype, *, n=None, b_off=0, res=None, name):
    if form == "tn":
        K, M = a.shape
    else:
        M, K = a.shape
    N = n if n is not None else (b.shape[0] if form == "nt" else b.shape[1])
    if K <= 2048:
        tk = K
        if form == "tn":
            tm = _pick(M, (512, 256, 128))
            tn = N if N <= 2048 else _pick(N, (1024, 512, 256, 128))
        else:
            tm = M if M <= 2048 else _pick(M, (2048, 1024, 512))
            tn = _pick(math.gcd(N, b_off) if b_off else N, (256, 128) if res is not None else (512, 256, 128))
    else:
        tk = _pick(K, (512, 256, 128))
        tm = _pick(M, (1024, 512, 256, 128))
        tn = _pick(math.gcd(N, b_off) if b_off else N, (1024, 512, 256, 128))
    assert b_off % tn == 0
    off = b_off // tn
    nk = K // tk
    if form == "tn":
        a_spec = pl.BlockSpec((tk, tm), lambda i, j, k: (k, i))
    else:
        a_spec = pl.BlockSpec((tm, tk), lambda i, j, k: (i, k))
    if form == "nt":
        b_spec = pl.BlockSpec((tn, tk), lambda i, j, k: (j + off, k))
    else:
        b_spec = pl.BlockSpec((tk, tn), lambda i, j, k: (k, j + off))
    dims = {"nn": ((1,), (0,)), "nt": ((1,), (1,)), "tn": ((0,), (0,))}[form]
    has_res = res is not None

    def body(*refs):
        if has_res:
            a_ref, b_ref, r_ref, o_ref = refs[:4]
        else:
            a_ref, b_ref, o_ref = refs[:3]
        p = lax.dot_general(a_ref[...], b_ref[...], (dims, ((), ())), preferred_element_type=f32)

        def finish(acc):
            if has_res:
                acc = acc + r_ref[...].astype(f32)
            o_ref[...] = acc.astype(o_ref.dtype)

        if nk == 1:
            finish(p)
        else:
            acc_ref = refs[-1]
            k = pl.program_id(2)

            @pl.when(k == 0)
            def _():
                acc_ref[...] = p

            @pl.when(k > 0)
            def _():
                acc_ref[...] += p

            @pl.when(k == nk - 1)
            def _():
                finish(acc_ref[...])

    in_specs = [a_spec, b_spec]
    args = [a, b]
    osz = jnp.dtype(out_dtype).itemsize
    vmem = 2 * (tm * tk * 2 + tk * tn * 2 + tm * tn * osz) + 2 * tm * tn * 4
    if has_res:
        in_specs.append(pl.BlockSpec((tm, tn), lambda i, j, k: (i, j)))
        args.append(res)
        vmem += 2 * tm * tn * res.dtype.itemsize
    scratch = []
    if nk > 1:
        scratch.append(pltpu.VMEM((tm, tn), f32))
        vmem += tm * tn * 4
    return _call(
        body, name=name, grid=(M // tm, N // tn, nk),
        in_specs=in_specs, out_specs=pl.BlockSpec((tm, tn), lambda i, j, k: (i, j)),
        out_shape=jax.ShapeDtypeStruct((M, N), out_dtype), scratch_shapes=scratch,
        compiler_params=_params(("parallel", "parallel", "arbitrary"), vmem + (8 << 20)),
    )(*args)


def _rows(body, ins, outs, *, tm, name, vmem=40 << 20):
    nrows = next(s[1].shape[0] for s in ins if s[0] == "r")
    in_specs, args = [], []
    for s in ins:
        arr = s[1]
        if s[0] == "r":
            w = s[2] if len(s) > 2 else arr.shape[1]
            cb = s[3] if len(s) > 3 else 0
            in_specs.append(pl.BlockSpec((tm, w), functools.partial(lambda i, cb: (i, cb), cb=cb)))
        else:
            in_specs.append(pl.BlockSpec(arr.shape, functools.partial(lambda i, nd: (0,) * nd, nd=arr.ndim)))
        args.append(arr)
    out_specs, out_shape = [], []
    for s in outs:
        if s[0] == "r":
            out_specs.append(pl.BlockSpec((tm, s[1]), lambda i: (i, 0)))
            out_shape.append(jax.ShapeDtypeStruct((nrows, s[1]), s[2]))
        else:
            out_specs.append(pl.BlockSpec(s[1], functools.partial(lambda i, nd: (0,) * nd, nd=len(s[1]))))
            out_shape.append(jax.ShapeDtypeStruct(s[1], s[2]))
    return _call(body, name=name, grid=(nrows // tm,), in_specs=in_specs, out_specs=out_specs,
                 out_shape=out_shape, compiler_params=_params(("arbitrary",), vmem))(*args)


def _accumulate(ref, part):
    i = pl.program_id(0)

    @pl.when(i == 0)
    def _():
        ref[...] = part

    @pl.when(i > 0)
    def _():
        ref[...] += part


def _rms_stats(x):
    r = lax.rsqrt(jnp.mean(x * x, axis=-1, keepdims=True) + 1e-6)
    return r, x * r


def _rms_fwd(x, g, name):
    def body(x_ref, g_ref, o_ref):
        _, xn = _rms_stats(x_ref[...])
        o_ref[...] = (xn * g_ref[...]).astype(o_ref.dtype)

    return _rows(body, [("r", x), ("f", g)], [("r", x.shape[1], bf16)], tm=min(256, x.shape[0]), name=name)[0]


def _rms_bwd(x, g, dh, dres, name):
    D = x.shape[1]

    def body(x_ref, g_ref, dh_ref, dr_ref, dx_ref, dxb_ref, dg_ref):
        r, xn = _rms_stats(x_ref[...])
        dy = dh_ref[...].astype(f32)
        dxn = dy * g_ref[...]
        dx = dr_ref[...] + r * (dxn - xn * jnp.mean(dxn * xn, axis=-1, keepdims=True))
        dx_ref[...] = dx
        dxb_ref[...] = dx.astype(bf16)
        _accumulate(dg_ref, jnp.sum(dy * xn, axis=0, keepdims=True))

    return _rows(body, [("r", x), ("f", g), ("r", dh), ("r", dres)],
                 [("r", D, f32), ("r", D, bf16), ("a", (1, D), f32)], tm=min(256, x.shape[0]), name=name)


def _final_loss(x, g, tgt, name):
    D = x.shape[1]

    def body(x_ref, g_ref, t_ref, dx_ref, dxb_ref, sq_ref, dg_ref):
        r, xn = _rms_stats(x_ref[...])
        gain = g_ref[...]
        diff = xn * gain - t_ref[...]
        dy = diff * (1.0 / D)
        dxn = dy * gain
        dx = r * (dxn - xn * jnp.mean(dxn * xn, axis=-1, keepdims=True))
        dx_ref[...] = dx
        dxb_ref[...] = dx.astype(bf16)
        _accumulate(sq_ref, jnp.sum(diff * diff, axis=0, keepdims=True))
        _accumulate(dg_ref, jnp.sum(dy * xn, axis=0, keepdims=True))

    return _rows(body, [("r", x), ("f", g), ("r", tgt)],
                 [("r", D, f32), ("r", D, bf16), ("a", (1, D), f32), ("a", (1, D), f32)],
                 tm=min(256, x.shape[0]), name=name)


def _qk_fwd(q_raw, kv_raw, qg, kg, cos2, sin2, name):
    def body(q_ref, k_ref, qg_ref, kg_ref, c_ref, s_ref, qo_ref, ko_ref):
        c, s = c_ref[...], s_ref[...]

        def head(src, gain, dst, h):
            cols = slice(h * HEAD_DIM, (h + 1) * HEAD_DIM)
            _, xn = _rms_stats(src[:, cols].astype(f32))
            y = xn * gain
            dst[:, cols] = (y * c + pltpu.roll(y, HEAD_DIM // 2, 1) * s).astype(dst.dtype)

        for h in range(N_Q):
            head(q_ref, qg_ref[...], qo_ref, h)
        for h in range(N_KV):
            head(k_ref, kg_ref[...], ko_ref, h)

    return _rows(body, [("r", q_raw), ("r", kv_raw, KV_COLS, 0), ("f", qg), ("f", kg), ("r", cos2), ("r", sin2)],
                 [("r", Q_COLS, bf16), ("r", KV_COLS, bf16)], tm=min(256, q_raw.shape[0]), name=name)


def _qk_bwd(q_raw, kv_raw, dqr, dkr, qg, kg, cos2, sin2, name):
    def body(q_ref, k_ref, dq_ref, dk_ref, qg_ref, kg_ref, c_ref, s_ref, dqo_ref, dko_ref, dqg_ref, dkg_ref):
        c, s = c_ref[...], s_ref[...]

        def head(src, dsrc, gain, dst, h):
            cols = slice(h * HEAD_DIM, (h + 1) * HEAD_DIM)
            r, xn = _rms_stats(src[:, cols].astype(f32))
            do = dsrc[:, cols].astype(f32)
            dy = do * c + pltpu.roll(do * s, HEAD_DIM // 2, 1)
            dxn = dy * gain
            dst[:, cols] = (r * (dxn - xn * jnp.mean(dxn * xn, axis=-1, keepdims=True))).astype(dst.dtype)
            return jnp.sum(dy * xn, axis=0, keepdims=True)

        dq_gain = head(q_ref, dq_ref, qg_ref[...], dqo_ref, 0)
        for h in range(1, N_Q):
            dq_gain = dq_gain + head(q_ref, dq_ref, qg_ref[...], dqo_ref, h)
        dk_gain = head(k_ref, dk_ref, kg_ref[...], dko_ref, 0)
        for h in range(1, N_KV):
            dk_gain = dk_gain + head(k_ref, dk_ref, kg_ref[...], dko_ref, h)
        _accumulate(dqg_ref, dq_gain)
        _accumulate(dkg_ref, dk_gain)

    return _rows(body, [("r", q_raw), ("r", kv_raw, KV_COLS, 0), ("r", dqr), ("r", dkr), ("f", qg), ("f", kg),
                        ("r", cos2), ("r", sin2)],
                 [("r", Q_COLS, bf16), ("r", KV_COLS, bf16), ("a", (1, HEAD_DIM), f32), ("a", (1, HEAD_DIM), f32)],
                 tm=min(256, q_raw.shape[0]), name=name)


def _softmax_rows(q, k):
    s = lax.dot_general(q, k, (((1,), (1,)), ((), ())), preferred_element_type=f32) * SCALE
    p = jnp.exp(s - jnp.max(s, axis=-1, keepdims=True))
    return p * (1.0 / jnp.sum(p, axis=-1, keepdims=True))


def _attn_fwd(qr, kr, kv_raw, name):
    S = qr.shape[0]
    tq = min(512, S)

    def body(q_ref, k_ref, v_ref, o_ref):
        p = _softmax_rows(q_ref[...], k_ref[...])
        o_ref[...] = jnp.dot(p.astype(bf16), v_ref[...], preferred_element_type=f32).astype(o_ref.dtype)

    return _call(
        body, name=name, grid=(N_Q, S // tq),
        in_specs=[pl.BlockSpec((tq, HEAD_DIM), lambda h, i: (i, h)),
                  pl.BlockSpec((S, HEAD_DIM), lambda h, i: (0, h // GRP)),
                  pl.BlockSpec((S, HEAD_DIM), lambda h, i: (0, N_KV + h // GRP))],
        out_specs=pl.BlockSpec((tq, HEAD_DIM), lambda h, i: (i, h)),
        out_shape=jax.ShapeDtypeStruct((S, Q_COLS), bf16),
        compiler_params=_params(("parallel", "arbitrary"), 6 * tq * S * 4 + (8 << 20)),
    )(qr, kr, kv_raw)


def _attn_bwd(qr, kr, kv_raw, do, name):
    S = qr.shape[0]
    tq = min(256, S)

    def body(q_ref, k_ref, v_ref, do_ref, dq_ref, dk_ref, dv_ref):
        first = jnp.logical_and(pl.program_id(1) == 0, pl.program_id(2) == 0)
        q, k, v, do_ = q_ref[...], k_ref[...], v_ref[...], do_ref[...]
        p = _softmax_rows(q, k)
        dp = lax.dot_general(do_, v, (((1,), (1,)), ((), ())), preferred_element_type=f32)
        ds = (p * (dp - jnp.sum(dp * p, axis=-1, keepdims=True)) * SCALE).astype(bf16)
        dq_ref[...] = jnp.dot(ds, k, preferred_element_type=f32).astype(dq_ref.dtype)
        dv_part = lax.dot_general(p.astype(bf16), do_, (((0,), (0,)), ((), ())), preferred_element_type=f32)
        dk_part = lax.dot_general(ds, q, (((0,), (0,)), ((), ())), preferred_element_type=f32)

        @pl.when(first)
        def _():
            dv_ref[...] = dv_part
            dk_ref[...] = dk_part

        @pl.when(jnp.logical_not(first))
        def _():
            dv_ref[...] += dv_part
            dk_ref[...] += dk_part

    qmap = lambda kv, g, i: (i, kv * GRP + g)
    return _call(
        body, name=name, grid=(N_KV, GRP, S // tq),
        in_specs=[pl.BlockSpec((tq, HEAD_DIM), qmap),
                  pl.BlockSpec((S, HEAD_DIM), lambda kv, g, i: (0, kv)),
                  pl.BlockSpec((S, HEAD_DIM), lambda kv, g, i: (0, N_KV + kv)),
                  pl.BlockSpec((tq, HEAD_DIM), qmap)],
        out_specs=[pl.BlockSpec((tq, HEAD_DIM), qmap),
                   pl.BlockSpec((S, HEAD_DIM), lambda kv, g, i: (0, kv)),
                   pl.BlockSpec((S, HEAD_DIM), lambda kv, g, i: (0, kv))],
        out_shape=[jax.ShapeDtypeStruct((S, Q_COLS), bf16), jax.ShapeDtypeStruct((S, KV_COLS), f32),
                   jax.ShapeDtypeStruct((S, KV_COLS), f32)],
        compiler_params=_params(("parallel", "arbitrary", "arbitrary"), 8 * tq * S * 4 + (8 << 20)),
    )(qr, kr, kv_raw, do)


CONV_HALO = 16


def _fill_padded(pad_ref, val, S):
    pad_ref[pl.ds(0, CONV_HALO), :] = jnp.zeros((CONV_HALO, LANES), f32)
    pad_ref[pl.ds(CONV_HALO + S, CONV_HALO), :] = jnp.zeros((CONV_HALO, LANES), f32)
    pad_ref[pl.ds(CONV_HALO, S), :] = val


def _group_specs(S, n_groups, second_half):
    return pl.BlockSpec((S, LANES), functools.partial(lambda g, o: (0, g + o), o=n_groups if second_half else 0))


def _conv1_fwd(conv_in, wdw, b_dw, name):
    S = conv_in.shape[0]
    ng = CONV_CH // LANES
    R = min(256, S)

    def body(a_ref, g_ref, w_ref, b_ref, o_ref, pad_ref):
        z = a_ref[...].astype(f32) * jax.nn.sigmoid(g_ref[...].astype(f32))
        _fill_padded(pad_ref, z, S)
        for r in range(S // R):
            acc = jnp.zeros((R, LANES), f32) + b_ref[...]
            for j in range(CONV_W):
                acc = acc + w_ref[pl.ds(j, 1), :] * pad_ref[pl.ds(r * R + CONV_HALO - CONV_PAD + j, R), :]
            o_ref[pl.ds(r * R, R), :] = acc

    return _call(
        body, name=name, grid=(ng,),
        in_specs=[_group_specs(S, ng, False), _group_specs(S, ng, True),
                  pl.BlockSpec((CONV_WP, LANES), lambda g: (g, 0)), pl.BlockSpec((1, LANES), lambda g: (0, g))],
        out_specs=pl.BlockSpec((S, LANES), lambda g: (0, g)),
        out_shape=jax.ShapeDtypeStruct((S, CONV_CH), f32),
        scratch_shapes=[pltpu.VMEM((S + 2 * CONV_HALO, LANES), f32)],
        compiler_params=_params(("parallel",), 24 << 20),
    )(conv_in, conv_in, wdw, b_dw)


def _conv1_bwd(conv_in, dc, wdw, name):
    S = conv_in.shape[0]
    ng = CONV_CH // LANES
    R = min(256, S)

    def body(a_ref, g_ref, w_ref, dc_ref, da_ref, dg_ref, dw_ref, db_ref, padz_ref, padd_ref):
        a = a_ref[...].astype(f32)
        sg = jax.nn.sigmoid(g_ref[...].astype(f32))
        _fill_padded(padz_ref, a * sg, S)
        _fill_padded(padd_ref, dc_ref[...], S)
        for r in range(S // R):
            dz = jnp.zeros((R, LANES), f32)
            for j in range(CONV_W):
                dz = dz + w_ref[pl.ds(j, 1), :] * padd_ref[pl.ds(r * R + CONV_HALO + CONV_PAD - j, R), :]
            rows = pl.ds(r * R, R)
            ar, sr = a_ref[rows, :].astype(f32), jax.nn.sigmoid(g_ref[rows, :].astype(f32))
            da_ref[rows, :] = (dz * sr).astype(da_ref.dtype)
            dg_ref[rows, :] = (dz * ar * sr * (1.0 - sr)).astype(dg_ref.dtype)
        for j in range(CONV_W):
            tot = jnp.zeros((1, LANES), f32)
            for r in range(S // R):
                tot = tot + jnp.sum(dc_ref[pl.ds(r * R, R), :] * padz_ref[pl.ds(r * R + CONV_HALO - CONV_PAD + j, R), :],
                                    axis=0, keepdims=True)
            dw_ref[pl.ds(j, 1), :] = tot
        dw_ref[pl.ds(CONV_W, CONV_WP - CONV_W), :] = jnp.zeros((CONV_WP - CONV_W, LANES), f32)
        db_ref[...] = jnp.sum(dc_ref[...], axis=0, keepdims=True)

    return _call(
        body, name=name, grid=(ng,),
        in_specs=[_group_specs(S, ng, False), _group_specs(S, ng, True),
                  pl.BlockSpec((CONV_WP, LANES), lambda g: (g, 0)), pl.BlockSpec((S, LANES), lambda g: (0, g))],
        out_specs=[pl.BlockSpec((S, LANES), lambda g: (0, g)), pl.BlockSpec((S, LANES), lambda g: (0, g)),
                   pl.BlockSpec((CONV_WP, LANES), lambda g: (g, 0)), pl.BlockSpec((1, LANES), lambda g: (0, g))],
        out_shape=[jax.ShapeDtypeStruct((S, CONV_CH), bf16), jax.ShapeDtypeStruct((S, CONV_CH), bf16),
                   jax.ShapeDtypeStruct((ng * CONV_WP, LANES), f32), jax.ShapeDtypeStruct((1, CONV_CH), f32)],
        scratch_shapes=[pltpu.VMEM((S + 2 * CONV_HALO, LANES), f32), pltpu.VMEM((S + 2 * CONV_HALO, LANES), f32)],
        compiler_params=_params(("parallel",), 24 << 20),
    )(conv_in, conv_in, wdw, dc)


def _ln_stats(x, eps=1e-5):
    xc = x - jnp.mean(x, axis=-1, keepdims=True)
    r = lax.rsqrt(jnp.mean(xc * xc, axis=-1, keepdims=True) + eps)
    return r, xc * r


def _ln_bwd(r, xh, dxh):
    return r * (dxh - jnp.mean(dxh, axis=-1, keepdims=True) - xh * jnp.mean(dxh * xh, axis=-1, keepdims=True))


def _conv2_fwd(c, ln_g, ln_b, name):
    def body(c_ref, g_ref, b_ref, o_ref):
        _, xh = _ln_stats(c_ref[...])
        y = xh * g_ref[...] + b_ref[...]
        o_ref[...] = (y * jax.nn.sigmoid(y)).astype(o_ref.dtype)

    return _rows(body, [("r", c), ("f", ln_g), ("f", ln_b)], [("r", CONV_CH, bf16)], tm=min(256, c.shape[0]), name=name)[0]


def _conv2_bwd(c, dcz, ln_g, ln_b, name):
    def body(c_ref, d_ref, g_ref, b_ref, dc_ref, dg_ref, db_ref):
        r, xh = _ln_stats(c_ref[...])
        y = xh * g_ref[...] + b_ref[...]
        sg = jax.nn.sigmoid(y)
        dy = d_ref[...].astype(f32) * (sg * (1.0 + y * (1.0 - sg)))
        dc_ref[...] = _ln_bwd(r, xh, dy * g_ref[...])
        _accumulate(dg_ref, jnp.sum(dy * xh, axis=0, keepdims=True))
        _accumulate(db_ref, jnp.sum(dy, axis=0, keepdims=True))

    return _rows(body, [("r", c), ("r", dcz), ("f", ln_g), ("f", ln_b)],
                 [("r", CONV_CH, f32), ("a", (1, CONV_CH), f32), ("a", (1, CONV_CH), f32)],
                 tm=min(256, c.shape[0]), name=name)


GELU_K = math.sqrt(2.0 / math.pi)
GELU_C = 0.044715


def _gelu(x):
    return 0.5 * x * (1.0 + jnp.tanh(GELU_K * (x + GELU_C * x * x * x)))


def _gelu_grad(x):
    th = jnp.tanh(GELU_K * (x + GELU_C * x * x * x))
    return 0.5 * (1.0 + th) + 0.5 * x * (1.0 - th * th) * (GELU_K * (1.0 + 3.0 * GELU_C * x * x))


def _chunk_rows(n):
    return pl.ds(pl.multiple_of(n * SG_CHUNK, SG_CHUNK), SG_CHUNK)


def _sgu_fwd(sg_in, ln_g, ln_b, w_s, b_s, name):
    S = sg_in.shape[0]

    def body(u_ref, v_ref, lg_ref, lb_ref, w_ref, b_ref, o_ref):
        wb = w_ref[...].astype(bf16)

        def chunk(n, carry):
            rows = _chunk_rows(n)
            gu = _gelu(u_ref[rows, :].astype(f32))
            _, xh = _ln_stats(_gelu(v_ref[rows, :].astype(f32)))
            vl = xh * lg_ref[...] + lb_ref[...]
            t = jnp.dot(wb, vl.astype(bf16), preferred_element_type=f32) + b_ref[...]
            o_ref[rows, :] = (gu * t).astype(o_ref.dtype)
            return carry

        lax.fori_loop(0, S // SG_CHUNK, chunk, 0)

    return _call(
        body, name=name, grid=(SG_G,),
        in_specs=[_group_specs(S, SG_G, False), _group_specs(S, SG_G, True),
                  pl.BlockSpec((1, LANES), lambda g: (0, g)), pl.BlockSpec((1, LANES), lambda g: (0, g)),
                  pl.BlockSpec((None, SG_CHUNK, SG_CHUNK), lambda g: (g, 0, 0)),
                  pl.BlockSpec((None, SG_CHUNK, 1), lambda g: (g, 0, 0))],
        out_specs=pl.BlockSpec((S, LANES), lambda g: (0, g)),
        out_shape=jax.ShapeDtypeStruct((S, SG_CH), bf16),
        compiler_params=_params(("parallel",), 24 << 20),
    )(sg_in, sg_in, ln_g, ln_b, w_s, b_s)


def _sgu_bwd(sg_in, dsz, ln_g, ln_b, w_s, w_s_t, b_s, name):
    S = sg_in.shape[0]

    def body(u_ref, v_ref, lg_ref, lb_ref, w_ref, wt_ref, b_ref, d_ref, du_ref, dv_ref, dw_ref, db_ref, dlg_ref, dlb_ref):
        wb = w_ref[...].astype(bf16)
        wtb = wt_ref[...].astype(bf16)

        def chunk(n, carry):
            dwa, dba, dlga, dlba = carry
            rows = _chunk_rows(n)
            u = u_ref[rows, :].astype(f32)
            v = v_ref[rows, :].astype(f32)
            gu = _gelu(u)
            r, xh = _ln_stats(_gelu(v))
            vlb = (xh * lg_ref[...] + lb_ref[...]).astype(bf16)
            t = jnp.dot(wb, vlb, preferred_element_type=f32) + b_ref[...]
            d = d_ref[rows, :].astype(f32)
            dt = d * gu
            dtb = dt.astype(bf16)
            dwa = dwa + lax.dot_general(dtb, vlb, (((1,), (1,)), ((), ())), preferred_element_type=f32)
            dba = dba + jnp.sum(dt, axis=1, keepdims=True)
            dvl = jnp.dot(wtb, dtb, preferred_element_type=f32)
            dlga = dlga + jnp.sum(dvl * xh, axis=0, keepdims=True)
            dlba = dlba + jnp.sum(dvl, axis=0, keepdims=True)
            dgv = _ln_bwd(r, xh, dvl * lg_ref[...])
            du_ref[rows, :] = (d * t * _gelu_grad(u)).astype(du_ref.dtype)
            dv_ref[rows, :] = (dgv * _gelu_grad(v)).astype(dv_ref.dtype)
            return dwa, dba, dlga, dlba

        init = (jnp.zeros((SG_CHUNK, SG_CHUNK), f32), jnp.zeros((SG_CHUNK, 1), f32),
                jnp.zeros((1, LANES), f32), jnp.zeros((1, LANES), f32))
        dwa, dba, dlga, dlba = lax.fori_loop(0, S // SG_CHUNK, chunk, init)
        dw_ref[...] = dwa
        db_ref[...] = dba
        dlg_ref[...] = dlga
        dlb_ref[...] = dlba

    wspec = pl.BlockSpec((None, SG_CHUNK, SG_CHUNK), lambda g: (g, 0, 0))
    bspec = pl.BlockSpec((None, SG_CHUNK, 1), lambda g: (g, 0, 0))
    lspec = pl.BlockSpec((1, LANES), lambda g: (0, g))
    cspec = pl.BlockSpec((S, LANES), lambda g: (0, g))
    return _call(
        body, name=name, grid=(SG_G,),
        in_specs=[_group_specs(S, SG_G, False), _group_specs(S, SG_G, True), lspec, lspec, wspec, wspec, bspec, cspec],
        out_specs=[cspec, cspec, wspec, bspec, lspec, lspec],
        out_shape=[jax.ShapeDtypeStruct((S, SG_CH), bf16), jax.ShapeDtypeStruct((S, SG_CH), bf16),
                   jax.ShapeDtypeStruct((SG_G, SG_CHUNK, SG_CHUNK), f32), jax.ShapeDtypeStruct((SG_G, SG_CHUNK, 1), f32),
                   jax.ShapeDtypeStruct((1, SG_CH), f32), jax.ShapeDtypeStruct((1, SG_CH), f32)],
        compiler_params=_params(("parallel",), 24 << 20),
    )(sg_in, sg_in, ln_g, ln_b, w_s, w_s_t, b_s, dsz)


def _merge_fwd(gl, b_gate, ya, yc, ys, name):
    D = ya.shape[1]

    def body(gl_ref, b_ref, ya_ref, yc_ref, ys_ref, o_ref):
        acc = jnp.zeros(o_ref.shape, f32)
        for i, y_ref in enumerate((ya_ref, yc_ref, ys_ref)):
            cols = slice(i * D, (i + 1) * D)
            acc = acc + jax.nn.sigmoid(gl_ref[:, cols].astype(f32) + b_ref[:, cols]) * y_ref[...].astype(f32)
        o_ref[...] = acc.astype(o_ref.dtype)

    return _rows(body, [("r", gl), ("f", b_gate), ("r", ya), ("r", yc), ("r", ys)], [("r", D, bf16)],
                 tm=min(128, gl.shape[0]), name=name)[0]


def _merge_bwd(dm, gl, b_gate, ya, yc, ys, name):
    D = ya.shape[1]

    def body(dm_ref, gl_ref, b_ref, ya_ref, yc_ref, ys_ref, dgl_ref, dya_ref, dyc_ref, dys_ref, db_ref):
        dm_ = dm_ref[...].astype(f32)
        for i, (y_ref, dy_ref) in enumerate(((ya_ref, dya_ref), (yc_ref, dyc_ref), (ys_ref, dys_ref))):
            cols = slice(i * D, (i + 1) * D)
            gate = jax.nn.sigmoid(gl_ref[:, cols].astype(f32) + b_ref[:, cols])
            dy_ref[...] = (dm_ * gate).astype(dy_ref.dtype)
            dlog = dm_ * y_ref[...].astype(f32) * gate * (1.0 - gate)
            dgl_ref[:, cols] = dlog.astype(dgl_ref.dtype)
            part = jnp.sum(dlog, axis=0, keepdims=True)
            first = pl.program_id(0) == 0

            @pl.when(first)
            def _():
                db_ref[:, cols] = part

            @pl.when(jnp.logical_not(first))
            def _():
                db_ref[:, cols] += part

    return _rows(body, [("r", dm), ("r", gl), ("f", b_gate), ("r", ya), ("r", yc), ("r", ys)],
                 [("r", 3 * D, bf16), ("r", D, bf16), ("r", D, bf16), ("r", D, bf16), ("a", (1, 3 * D), f32)],
                 tm=min(128, gl.shape[0]), name=name)


def _swiglu_fwd(fg, fu, name):
    def body(g_ref, u_ref, o_ref):
        g = g_ref[...].astype(f32)
        o_ref[...] = (g * jax.nn.sigmoid(g) * u_ref[...].astype(f32)).astype(o_ref.dtype)

    return _rows(body, [("r", fg), ("r", fu)], [("r", fg.shape[1], bf16)], tm=min(128, fg.shape[0]), name=name)[0]


def _swiglu_bwd(dact, fg, fu, name):
    def body(d_ref, g_ref, u_ref, dg_ref, du_ref):
        d = d_ref[...].astype(f32)
        g = g_ref[...].astype(f32)
        sg = jax.nn.sigmoid(g)
        dg_ref[...] = (d * u_ref[...].astype(f32) * sg * (1.0 + g * (1.0 - sg))).astype(dg_ref.dtype)
        du_ref[...] = (d * g * sg).astype(du_ref.dtype)

    return _rows(body, [("r", dact), ("r", fg), ("r", fu)], [("r", fg.shape[1], bf16), ("r", fg.shape[1], bf16)],
                 tm=min(128, fg.shape[0]), name=name)


def _row_tile(r, c, n_arrays, itemsize=4):
    fits = [tm for tm in range(16, r + 1, 16) if r % tm == 0 and 2 * n_arrays * tm * c * itemsize <= (24 << 20)]
    return fits[-1] if fits else r


def _sum_slots(slots, name):
    n, r, c = slots.shape
    tm = _row_tile(r, c, n + 2)

    def body(s_ref, o_ref):
        acc = s_ref[0].astype(f32)
        for k in range(1, n):
            acc = acc + s_ref[k].astype(f32)
        o_ref[...] = acc

    return _call(body, name=name, grid=(r // tm,),
                 in_specs=[pl.BlockSpec((n, tm, c), lambda i: (0, i, 0))],
                 out_specs=pl.BlockSpec((tm, c), lambda i: (i, 0)),
                 out_shape=jax.ShapeDtypeStruct((r, c), f32),
                 compiler_params=_params(("parallel",), 40 << 20))(slots)


def _add_sibling(g4, recv, core, name):
    _, _, r, c = g4.shape
    tm = _row_tile(r, c, 3, 2)

    def body(core_ref, g_ref, r_ref, o_ref):
        o_ref[...] = (g_ref[...].astype(f32) + r_ref[...].astype(f32)).astype(o_ref.dtype)

    grid_spec = pltpu.PrefetchScalarGridSpec(
        num_scalar_prefetch=1, grid=(N_CHIP, r // tm),
        in_specs=[pl.BlockSpec((None, None, tm, c), lambda k, i, core_ref: (k, core_ref[0], i, 0)),
                  pl.BlockSpec((None, tm, c), lambda k, i, core_ref: (k, i, 0))],
        out_specs=pl.BlockSpec((None, tm, c), lambda k, i, core_ref: (k, i, 0)))
    return _call(body, name=name, grid_spec=grid_spec, out_shape=jax.ShapeDtypeStruct((N_CHIP, r, c), bf16),
                 compiler_params=_params(("parallel", "parallel"), 40 << 20))(core, g4, recv)


def _adamw(w, g, m, v, name):
    L, r, c = w.shape
    tm = _row_tile(r, c, 7)
    c1 = 1.0 - ADAM_B1 ** ADAM_STEP
    c2 = 1.0 - ADAM_B2 ** ADAM_STEP

    def body(w_ref, g_ref, m_ref, v_ref, d_ref, mo_ref, vo_ref):
        g_ = g_ref[...]
        m_ = ADAM_B1 * m_ref[...] + (1.0 - ADAM_B1) * g_
        v_ = ADAM_B2 * v_ref[...] + (1.0 - ADAM_B2) * (g_ * g_)
        d_ref[...] = -ADAM_LR * ((m_ / c1) / (jnp.sqrt(v_ / c2) + ADAM_EPS) + ADAM_WD * w_ref[...])
        mo_ref[...] = m_
        vo_ref[...] = v_

    spec = pl.BlockSpec((None, tm, c), lambda l, i: (l, i, 0))
    shp = jax.ShapeDtypeStruct((L, r, c), f32)
    return _call(body, name=name, grid=(L, r // tm), in_specs=[spec] * 4, out_specs=[spec] * 3,
                 out_shape=[shp] * 3, compiler_params=_params(("parallel", "parallel"), 40 << 20))(w, g, m, v)


def _mesh_pos():
    return lax.axis_index("x"), lax.axis_index("y"), lax.axis_index("c")


def _all_gather(shards, name):
    n = len(shards)

    def body(*refs):
        x_refs, o_refs = refs[:n], refs[n:2 * n]
        send_sems, recv_sems, local_sems = refs[2 * n:]
        x, y, c = _mesh_pos()
        me, sibling = (x, y, c), (x, y, 1 - c)
        chips = [(1 - x, y), (x, 1 - y), (1 - x, 1 - y)]

        def rows(k, px, py, pc):
            r = shards[k].shape[0]
            return o_refs[k].at[pl.ds((4 * px + 2 * py + pc) * r, r), :]

        def copy(k, s, block, to, src=None):
            return pltpu.make_async_remote_copy(
                src_ref=rows(k, *block) if src is None else src, dst_ref=rows(k, *block),
                send_sem=send_sems.at[k, s], recv_sem=recv_sems.at[k, s], device_id=to, device_id_type=MESH)

        mine = [pltpu.make_async_copy(x_refs[k], rows(k, *me), local_sems.at[k]) for k in range(n)]
        for cp in mine:
            cp.start()
        first = [copy(k, 0, me, sibling, src=x_refs[k]) for k in range(n)]
        for j, chip in enumerate(chips):
            first += [copy(k, 1 + j, me, (*chip, c), src=x_refs[k]) for k in range(n)]
        for cp in first:
            cp.start()
        passed = []
        for j, chip in enumerate(chips):
            for k in range(n):
                copy(k, 1 + j, (*chip, c), me).wait_recv()
                fwd = copy(k, 4 + j, (*chip, c), sibling)
                fwd.start()
                passed.append(fwd)
        for k in range(n):
            copy(k, 0, sibling, me).wait_recv()
        for j, chip in enumerate(chips):
            for k in range(n):
                copy(k, 4 + j, (*chip, 1 - c), me).wait_recv()
        for cp in first + passed:
            cp.wait_send()
        for cp in mine:
            cp.wait()

    return _call(
        body, name=name, in_specs=[HBM] * n, out_specs=[HBM] * n,
        out_shape=[jax.ShapeDtypeStruct((N_DEV * s.shape[0], s.shape[1]), s.dtype) for s in shards],
        scratch_shapes=[pltpu.SemaphoreType.DMA((n, 7)), pltpu.SemaphoreType.DMA((n, 7)), pltpu.SemaphoreType.DMA((n,))],
    )(*shards)


def _send_to_sibling(g4s, name):
    n = len(g4s)

    def body(*refs):
        g_refs, o_refs = refs[:n], refs[n:2 * n]
        send_sems, recv_sems = refs[2 * n:]
        x, y, c = _mesh_pos()
        copies = [pltpu.make_async_remote_copy(
            src_ref=g_refs[k].at[:, 1 - c], dst_ref=o_refs[k], send_sem=send_sems.at[k], recv_sem=recv_sems.at[k],
            device_id=(x, y, 1 - c), device_id_type=MESH) for k in range(n)]
        for cp in copies:
            cp.start()
        for cp in copies:
            cp.wait()

    return _call(
        body, name=name, in_specs=[HBM] * n, out_specs=[HBM] * n,
        out_shape=[jax.ShapeDtypeStruct((N_CHIP,) + g.shape[2:], g.dtype) for g in g4s],
        scratch_shapes=[pltpu.SemaphoreType.DMA((n,)), pltpu.SemaphoreType.DMA((n,))],
    )(*g4s)


SEM = pl.BlockSpec(memory_space=pltpu.SEMAPHORE)
ANY = pl.BlockSpec(memory_space=pl.ANY)
EFFECT = pltpu.SideEffectType.DATAFLOW_SIDE_EFFECTING


def _other_chips(x, y):
    return [(1 - x, y), (x, 1 - y), (1 - x, 1 - y)]


def _ici_copy(kind, src_ref, land_ref, send_sem, recv_sem, sender, target, c):
    (sx, sy), (tx, ty) = sender, target
    if kind == "gather":
        r = src_ref.shape[0]
        src, dst = src_ref, land_ref.at[pl.ds((4 * sx + 2 * sy + c) * r, r), :]
    else:
        src, dst = src_ref.at[2 * tx + ty], land_ref.at[2 * sx + sy]
    return pltpu.make_async_remote_copy(src_ref=src, dst_ref=dst, send_sem=send_sem, recv_sem=recv_sem,
                                        device_id=(tx, ty, c), device_id_type=MESH)


def _ici_start(kind, srcs, lands, after, name):
    n = len(srcs)

    def body(*refs):
        src_refs, land_refs = refs[:n], refs[n:2 * n]
        send_sems, recv_sems = refs[2 * n + 1], refs[2 * n + 2]
        token = refs[-1]
        x, y, c = _mesh_pos()
        for j, chip in enumerate(_other_chips(x, y)):
            for k in range(n):
                _ici_copy(kind, src_refs[k], land_refs[k], send_sems.at[3 * k + j], recv_sems.at[3 * k + j],(x, y), chip, c).start()
        token[...] = jnp.zeros_like(token)

    bufs = list(srcs) + list(lands)
    return _call(
        body, name=name,
        out_shape=(pltpu.SemaphoreType.DMA((3 * n,)), pltpu.SemaphoreType.DMA((3 * n,)),
                   *[pltpu.HBM(b.shape, b.dtype) for b in bufs], jax.ShapeDtypeStruct((8, LANES), f32)),
        in_specs=[HBM] * (2 * n) + [ANY], out_specs=(SEM, SEM, *[HBM] * (2 * n), pl.BlockSpec(memory_space=pltpu.VMEM)),
        input_output_aliases={i: 2 + i for i in range(2 * n)},
        compiler_params=pltpu.CompilerParams(has_side_effects=EFFECT),
    )(*[pltpu.with_memory_space_constraint(b, pltpu.HBM) for b in bufs], after)


def _ici_wait(kind, started, after, name):
    send_sems, recv_sems, *bufs = started[:-1]
    n = len(bufs) // 2

    def body(*refs):
        src_refs, land_refs = refs[:n], refs[n:2 * n]
        send_sems, recv_sems = refs[2 * n], refs[2 * n + 1]
        x, y, c = _mesh_pos()
        for j, chip in enumerate(_other_chips(x, y)):
            for k in range(n):
                _ici_copy(kind, src_refs[k], land_refs[k], send_sems.at[3 * k + j], recv_sems.at[3 * k + j],(x, y), chip, c).wait_send()
                _ici_copy(kind, src_refs[k], land_refs[k], send_sems.at[3 * k + j], recv_sems.at[3 * k + j],chip, (x, y), c).wait_recv()

    out = _call(
        body, name=name, out_shape=[pltpu.HBM(b.shape, b.dtype) for b in bufs],
        in_specs=[HBM] * (2 * n) + [SEM, SEM, ANY], out_specs=[HBM] * (2 * n),
        input_output_aliases={i: i for i in range(2 * n)},
        compiler_params=pltpu.CompilerParams(has_side_effects=EFFECT),
    )(*bufs, send_sems, recv_sems, after)
    return out[:n], out[n:]


def _d2d_gather(shards, lands, after, name):
    n = len(shards)

    def body(*refs):
        x_refs, in_refs, o_refs = refs[:n], refs[n:2 * n], refs[2 * n + 1:3 * n + 1]
        send_sems, recv_sems, local_sems = refs[3 * n + 1:]
        x, y, c = _mesh_pos()
        sibling = (x, y, 1 - c)
        blocks = [(x, y)] + _other_chips(x, y)

        def rows(ref, k, px, py, pc):
            r = shards[k].shape[0]
            return ref.at[pl.ds((4 * px + 2 * py + pc) * r, r), :]

        mine = [pltpu.make_async_copy(x_refs[k], rows(o_refs[k], k, x, y, c), local_sems.at[k]) for k in range(n)]
        for cp in mine:
            cp.start()
        copies = []
        for s, (px, py) in enumerate(blocks):
            for k in range(n):
                src = x_refs[k] if s == 0 else rows(in_refs[k], k, px, py, c)
                copies.append(pltpu.make_async_remote_copy(
                    src_ref=src, dst_ref=rows(o_refs[k], k, px, py, c), send_sem=send_sems.at[k, s],
                    recv_sem=recv_sems.at[k, s], device_id=sibling, device_id_type=MESH))
        for cp in copies:
            cp.start()
        for s, (px, py) in enumerate(blocks):
            for k in range(n):
                pltpu.make_async_remote_copy(
                    src_ref=x_refs[k], dst_ref=rows(o_refs[k], k, px, py, 1 - c), send_sem=send_sems.at[k, s],
                    recv_sem=recv_sems.at[k, s], device_id=sibling, device_id_type=MESH).wait_recv()
        for cp in copies:
            cp.wait_send()
        for cp in mine:
            cp.wait()

    return _call(
        body, name=name, in_specs=[HBM] * (2 * n) + [ANY], out_specs=[HBM] * n,
        out_shape=[jax.ShapeDtypeStruct(b.shape, b.dtype) for b in lands],
        input_output_aliases={n + k: k for k in range(n)},
        scratch_shapes=[pltpu.SemaphoreType.DMA((n, 4)), pltpu.SemaphoreType.DMA((n, 4)), pltpu.SemaphoreType.DMA((n,))],
    )(*shards, *lands, after)


def _sum_chip_slots(lands, sums, chip, name):
    _, r, c = lands.shape
    tm = _row_tile(r, c, 10, 2)

    def body(chip_ref, l_ref, s_ref, o_ref):
        acc = None
        for k in range(N_CHIP):
            part = jnp.where(chip_ref[0] == k, s_ref[k], l_ref[k]).astype(f32)
            acc = part if acc is None else acc + part
        o_ref[...] = acc

    grid_spec = pltpu.PrefetchScalarGridSpec(
        num_scalar_prefetch=1, grid=(r // tm,),
        in_specs=[pl.BlockSpec((N_CHIP, tm, c), lambda i, chip_ref: (0, i, 0)),
                  pl.BlockSpec((N_CHIP, tm, c), lambda i, chip_ref: (0, i, 0))],
        out_specs=pl.BlockSpec((tm, c), lambda i, chip_ref: (i, 0)))
    return _call(body, name=name, grid_spec=grid_spec, out_shape=jax.ShapeDtypeStruct((r, c), f32),
                 compiler_params=_params(("parallel",), 40 << 20))(chip, lands, sums)


def _reduce_start(grads, core, after, tag):
    g4s = [g.reshape(N_CHIP, 2, g.shape[0] // N_DEV, g.shape[1]) for g in grads]
    recv = _send_to_sibling(g4s, name="rs_sibling_" + tag)
    sums = [_add_sibling(g4, rv, core, name="rs_add_" + tag) for g4, rv in zip(g4s, recv)]
    lands = [lax.empty(s.shape, s.dtype) for s in sums]
    return _ici_start("reduce", sums, lands, after, name="rs_start_" + tag)


def _reduce_finish(started, chip, after, tag):
    sums, lands = _ici_wait("reduce", started, after, name="rs_wait_" + tag)
    return [_sum_chip_slots(ld, s, chip, name="rs_sum_" + tag) for ld, s in zip(lands, sums)]


def _rope_tables(S):
    rows = S // GRID_W
    row = jnp.repeat(jnp.arange(rows, dtype=f32), GRID_W)
    col = jnp.tile(jnp.arange(GRID_W, dtype=f32), rows)
    nf = HEAD_DIM // 4
    inv = ROPE_THETA ** (-jnp.arange(nf, dtype=f32) / nf)
    ang = jnp.concatenate([row[:, None] * inv, col[:, None] * inv], axis=-1)
    cos, sin = jnp.cos(ang), jnp.sin(ang)
    return jnp.concatenate([cos, cos], axis=-1), jnp.concatenate([-sin, sin], axis=-1)


def _layer_fwd(xin, p, w, rest_of_weights, cos2, sin2):
    sv = {"xin": xin}
    h = sv["h"] = _rms_fwd(xin, p["g_mix"], name="rms_mix")
    proj = functools.partial(_mm, h, w["in"], "nt", bf16)
    q_raw = sv["q_raw"] = proj(n=Q_COLS, b_off=0, name="proj_q")
    kv_raw = sv["kv_raw"] = proj(n=2 * KV_COLS, b_off=OFF_KV, name="proj_kv")
    conv_in = sv["conv_in"] = proj(n=2 * CONV_CH, b_off=OFF_CONV, name="proj_conv")
    sg_in = sv["sg_in"] = proj(n=2 * SG_CH, b_off=OFF_SG, name="proj_sg")
    gl = sv["gl"] = proj(n=3 * D_MODEL, b_off=OFF_GATE, name="proj_gate")
    qr, kr = sv["qr"], sv["kr"] = _qk_fwd(q_raw, kv_raw, p["q_norm_g"], p["k_norm_g"], cos2, sin2, name="qk_fwd")
    o = sv["o"] = _attn_fwd(qr, kr, kv_raw, name="attn_fwd")
    c = sv["c"] = _conv1_fwd(conv_in, w["dw"], p["b_dw"], name="conv1_fwd")
    cz = sv["cz"] = _conv2_fwd(c, p["conv_ln_g"], p["conv_ln_b"], name="conv2_fwd")
    sz = sv["sz"] = _sgu_fwd(sg_in, p["sg_ln_g"], p["sg_ln_b"], p["w_s"], p["b_s"], name="sgu_fwd")
    w = rest_of_weights(sz)
    ya = sv["ya"] = _mm(o, w["attn_o"], "nt", bf16, name="out_attn")
    yc = sv["yc"] = _mm(cz, w["conv_o"], "nt", bf16, name="out_conv")
    ys = sv["ys"] = _mm(sz, w["sg_o"], "nt", bf16, name="out_sg")
    merged = sv["merged"] = _merge_fwd(gl, p["b_gate"], ya, yc, ys, name="merge_fwd")
    x1 = sv["x1"] = _mm(merged, w["out"], "nn", f32, res=xin, name="out_proj")
    hf = sv["hf"] = _rms_fwd(x1, p["g_ffn"], name="rms_ffn")
    fg = sv["fg"] = _mm(hf, w["ff_gate"], "nt", bf16, name="ff_gate")
    fu = sv["fu"] = _mm(hf, w["ff_up"], "nt", bf16, name="ff_up")
    act = sv["act"] = _swiglu_fwd(fg, fu, name="swiglu_fwd")
    x2 = _mm(act, w["ff_down"], "nn", f32, res=x1, name="ff_down")
    return x2, sv, w


def _layer_bwd(dx2, dx2b, sv, p, w, cos2, sin2, reduce_start):
    small = {}
    dact = _mm(dx2b, w["ff_down"], "nt", bf16, name="d_act")
    g_down = _mm(sv["act"], dx2b, "tn", bf16, name="g_ff_down")
    dfg, dfu = _swiglu_bwd(dact, sv["fg"], sv["fu"], name="swiglu_bwd")
    dhf = _mm(dfg, w["ff_gate"], "nn", f32, name="d_hf_gate")
    dhf = _mm(dfu, w["ff_up"], "nn", f32, res=dhf, name="d_hf_up")
    g_gate = _mm(dfg, sv["hf"], "tn", bf16, name="g_ff_gate")
    g_up = _mm(dfu, sv["hf"], "tn", bf16, name="g_ff_up")
    zero = reduce_start("ffn", dict(w_ff_gate=g_gate, w_ff_up=g_up, w_ff_down=g_down))
    dx1, dx1b, small["g_ffn"] = _rms_bwd(sv["x1"], p["g_ffn"] + zero, dhf, dx2, name="rms_ffn_bwd")
    dmerged = _mm(dx1b, w["out"], "nt", bf16, name="d_merged")
    g_out = _mm(sv["merged"], dx1b, "tn", bf16, name="g_out")
    dgl, dya, dyc, dys, small["b_gate"] = _merge_bwd(dmerged, sv["gl"], p["b_gate"], sv["ya"], sv["yc"], sv["ys"],
                                                    name="merge_bwd")
    do = _mm(dya, w["attn_o"], "nn", bf16, name="d_o")
    g_ao = _mm(dya, sv["o"], "tn", bf16, name="g_attn_o")
    dcz = _mm(dyc, w["conv_o"], "nn", bf16, name="d_cz")
    g_co = _mm(dyc, sv["cz"], "tn", bf16, name="g_conv_o")
    dsz = _mm(dys, w["sg_o"], "nn", bf16, name="d_sz")
    g_so = _mm(dys, sv["sz"], "tn", bf16, name="g_sg_o")
    zero = reduce_start("mix", dict(w_attn_o=g_ao, w_conv_o=g_co, w_sg_o=g_so, w_out=g_out))
    dsu, dsv, small["w_s"], small["b_s"], small["sg_ln_g"], small["sg_ln_b"] = _sgu_bwd(
        sv["sg_in"], dsz, p["sg_ln_g"] + zero, p["sg_ln_b"], p["w_s"], p["w_s_t"], p["b_s"], name="sgu_bwd")
    dc, small["conv_ln_g"], small["conv_ln_b"] = _conv2_bwd(sv["c"], dcz, p["conv_ln_g"], p["conv_ln_b"], name="conv2_bwd")
    da, dgt, small["w_dw"], small["b_dw"] = _conv1_bwd(sv["conv_in"], dc, w["dw"], name="conv1_bwd")
    dqr, dkr, dv = _attn_bwd(sv["qr"], sv["kr"], sv["kv_raw"], do, name="attn_bwd")
    dq_raw, dk_raw, small["q_norm_g"], small["k_norm_g"] = _qk_bwd(
        sv["q_raw"], sv["kv_raw"], dqr, dkr, p["q_norm_g"], p["k_norm_g"], cos2, sin2, name="qk_bwd")
    dproj = jnp.concatenate([dq_raw, dk_raw, dv.astype(bf16), da, dgt, dsu, dsv, dgl], axis=1)
    dh = _mm(dproj, w["in"], "nn", f32, name="d_h")
    g_in = _mm(dproj, sv["h"], "tn", bf16, name="g_in")
    zero = reduce_start("in", dict(w_in=g_in))
    dx, dxb, small["g_mix"] = _rms_bwd(sv["xin"], p["g_mix"] + zero, dh, dx1, name="rms_mix_bwd")
    return dx, dxb, small


SMALL = ("g_mix", "b_gate", "q_norm_g", "k_norm_g", "b_dw", "conv_ln_g", "conv_ln_b", "sg_ln_g", "sg_ln_b",
         "w_s", "b_s", "g_ffn")
PACK_ALIGN = 8 * LANES


def _pack(parts):
    flat = jnp.concatenate([a.reshape(-1).astype(f32) for a in parts])
    pad = -flat.shape[0] % PACK_ALIGN
    return jnp.pad(flat, (0, pad)).reshape(-1, LANES)


def _unpack(buf, shapes):
    flat = buf.reshape(-1)
    out, pos = [], 0
    for shp in shapes:
        size = math.prod(shp)
        out.append(flat[pos:pos + size].reshape(shp))
        pos += size
    return out


def kernel(x, g_mix, w_in, b_gate, q_norm_g, k_norm_g, w_attn_o, w_dw, b_dw, conv_ln_g, conv_ln_b, w_conv_o, sg_ln_g, sg_ln_b, w_s, b_s, w_sg_o, w_out, g_ffn, w_ff_gate, w_ff_up, w_ff_down, g_final, loss_target, m_g_mix, m_w_in, m_b_gate, m_q_norm_g, m_k_norm_g, m_w_attn_o, m_w_dw, m_b_dw, m_conv_ln_g, m_conv_ln_b, m_w_conv_o, m_sg_ln_g, m_sg_ln_b, m_w_s, m_b_s, m_w_sg_o, m_w_out, m_g_ffn, m_w_ff_gate, m_w_ff_up, m_w_ff_down, m_g_final, v_g_mix, v_w_in, v_b_gate, v_q_norm_g, v_k_norm_g, v_w_attn_o, v_w_dw, v_b_dw, v_conv_ln_g, v_conv_ln_b, v_w_conv_o, v_sg_ln_g, v_sg_ln_b, v_w_s, v_b_s, v_w_sg_o, v_w_out, v_g_ffn, v_w_ff_gate, v_w_ff_up, v_w_ff_down, v_g_final):
    weights = dict(g_mix=g_mix, w_in=w_in, b_gate=b_gate, q_norm_g=q_norm_g, k_norm_g=k_norm_g, w_attn_o=w_attn_o,
                   w_dw=w_dw, b_dw=b_dw, conv_ln_g=conv_ln_g, conv_ln_b=conv_ln_b, w_conv_o=w_conv_o, sg_ln_g=sg_ln_g,
                   sg_ln_b=sg_ln_b, w_s=w_s, b_s=b_s, w_sg_o=w_sg_o, w_out=w_out, g_ffn=g_ffn, w_ff_gate=w_ff_gate,
                   w_ff_up=w_ff_up, w_ff_down=w_ff_down, g_final=g_final)
    mom_m = dict(g_mix=m_g_mix, w_in=m_w_in, b_gate=m_b_gate, q_norm_g=m_q_norm_g, k_norm_g=m_k_norm_g,
                 w_attn_o=m_w_attn_o, w_dw=m_w_dw, b_dw=m_b_dw, conv_ln_g=m_conv_ln_g, conv_ln_b=m_conv_ln_b,
                 w_conv_o=m_w_conv_o, sg_ln_g=m_sg_ln_g, sg_ln_b=m_sg_ln_b, w_s=m_w_s, b_s=m_b_s, w_sg_o=m_w_sg_o,
                 w_out=m_w_out, g_ffn=m_g_ffn, w_ff_gate=m_w_ff_gate, w_ff_up=m_w_ff_up, w_ff_down=m_w_ff_down,
                 g_final=m_g_final)
    mom_v = dict(g_mix=v_g_mix, w_in=v_w_in, b_gate=v_b_gate, q_norm_g=v_q_norm_g, k_norm_g=v_k_norm_g,
                 w_attn_o=v_w_attn_o, w_dw=v_w_dw, b_dw=v_b_dw, conv_ln_g=v_conv_ln_g, conv_ln_b=v_conv_ln_b,
                 w_conv_o=v_w_conv_o, sg_ln_g=v_sg_ln_g, sg_ln_b=v_sg_ln_b, w_s=v_w_s, b_s=v_b_s, w_sg_o=v_w_sg_o,
                 w_out=v_w_out, g_ffn=v_g_ffn, w_ff_gate=v_w_ff_gate, w_ff_up=v_w_ff_up, w_ff_down=v_w_ff_down,
                 g_final=v_g_final)
    S, D = x.shape[1], x.shape[2]
    xi, yi, ci = _mesh_pos()
    me = 4 * xi + 2 * yi + ci
    core = jnp.reshape(ci, (1,)).astype(jnp.int32)
    cos2, sin2 = _rope_tables(S)

    big = ("w_in", "w_attn_o", "w_conv_o", "w_sg_o", "w_out", "w_ff_gate", "w_ff_up", "w_ff_down")
    transposed = {"w_in", "w_attn_o", "w_conv_o", "w_sg_o", "w_ff_gate", "w_ff_up"}
    chip = jnp.reshape(2 * xi + yi, (1,)).astype(jnp.int32)
    groups = (("in", "dw"), ("attn_o", "conv_o", "sg_o", "out", "ff_gate", "ff_up", "ff_down"))
    P, shards = [], []
    for l in range(DEPTH):
        sh = {n[2:]: (weights[n][l].T if n in transposed else weights[n][l]).astype(bf16) for n in big}
        sh["dw"] = jnp.pad(w_dw[l].reshape(CONV_W, LANES), ((0, CONV_WP - CONV_W), (0, 0)))
        shards.append(sh)
        p = {n: weights[n][l].reshape(1, -1) for n in SMALL if n not in ("w_s", "b_s")}
        p["w_s"] = w_s[l]
        p["w_s_t"] = jnp.swapaxes(w_s[l], 1, 2)
        p["b_s"] = b_s[l].reshape(SG_G, SG_CHUNK, 1)
        P.append(p)

    gathers = {}

    def start_gather(l, gi, after):
        srcs = [shards[l][n] for n in groups[gi]]
        lands = [lax.empty((N_DEV * s.shape[0], s.shape[1]), s.dtype) for s in srcs]
        gathers[l, gi] = _ici_start("gather", srcs, lands, after, name=f"ag_start_{l}{gi}")
        return gathers[l, gi][-1]

    def gathered(l, gi, after):
        srcs, lands = _ici_wait("gather", gathers[l, gi], after, name=f"ag_wait_{l}{gi}")
        after = start_gather(l + 1, gi, srcs[0]) if l + 1 < DEPTH else srcs[0]
        return dict(zip(groups[gi], _d2d_gather(srcs, lands, after, name=f"ag_d2d_{gi}")))

    all_started = start_gather(0, 1, start_gather(0, 0, cos2))

    h = x.reshape(S, D)
    saved, W = [], []
    for l in range(DEPTH):
        first = gathered(l, 0, all_started if l == 0 else h)
        if l == 0:
            P[l]["g_mix"] = P[l]["g_mix"] + all_started[0, 0]
        h, sv, w = _layer_fwd(h, P[l], first, lambda z, l=l, first=first: {**first, **gathered(l, 1, z)}, cos2, sin2)
        saved.append(sv)
        W.append(w)
    dx, dxb, sq, g_final_part = _final_loss(h, g_final.reshape(1, D), loss_target.reshape(S, D), name="final_loss")
    loss = lax.psum(0.5 * jnp.sum(sq) / D, ("x", "y", "c"))

    reductions, small_grads = {}, [None] * DEPTH
    for l in reversed(range(DEPTH)):
        def reduce_start(group, grads, l=l):
            names = tuple(grads)
            started = _reduce_start([grads[n] for n in names], core, grads[names[0]], tag=f"{group}{l}")
            reductions[l, group] = (names, started)
            return started[-1][0, 0]

        dx, dxb, small_grads[l] = _layer_bwd(dx, dxb, saved[l], P[l], W[l], cos2, sin2, reduce_start)
    grad_x = dx.reshape(x.shape)
    big_grads = [{} for _ in range(DEPTH)]
    for (l, group), (names, started) in reductions.items():
        big_grads[l].update(zip(names, _reduce_finish(started, chip, dx, tag=f"{group}{l}")))

    small_shapes = [weights[n].shape for n in SMALL] + [g_final.shape, (DEPTH, CONV_CH // LANES, CONV_WP, LANES)]
    parts = [jnp.stack([small_grads[l][n].reshape(weights[n].shape[1:]) for l in range(DEPTH)]) for n in SMALL]
    parts += [g_final_part.reshape(g_final.shape), jnp.stack([small_grads[l]["w_dw"] for l in range(DEPTH)])]
    packed = _pack(parts)
    gathered = _all_gather([packed], name="gather_small")[0]
    total = _sum_slots(gathered.reshape(N_DEV, packed.shape[0], LANES), name="sum_small")
    small_total = _unpack(total, small_shapes)
    grads_out = dict(zip(SMALL + ("g_final",), small_total[:-1]))
    dw_full = small_total[-1]
    grads_out["w_dw"] = lax.dynamic_index_in_dim(dw_full, me, axis=1, keepdims=False)[:, :CONV_W].reshape(w_dw.shape)

    delta, new_m, new_v = {}, {}, {}
    swap = lambda a: jnp.swapaxes(a, 1, 2)
    for n in big:
        g = jnp.stack([big_grads[l][n] for l in range(DEPTH)])
        if n in transposed and weights[n].shape[2] % LANES:
            upd = _adamw(swap(weights[n]), g, swap(mom_m[n]), swap(mom_v[n]), name="adamw_" + n)
            grads_out[n], (delta[n], new_m[n], new_v[n]) = swap(g), [swap(u) for u in upd]
        else:
            grads_out[n] = swap(g) if n in transposed else g
            delta[n], new_m[n], new_v[n] = _adamw(weights[n], grads_out[n], mom_m[n], mom_v[n], name="adamw_" + n)
    rep = SMALL + ("g_final",)
    rep_shapes = [weights[n].shape for n in rep]
    packs = [_pack([src[n] for n in rep])[None] for src in (weights, grads_out, mom_m, mom_v)]
    for dst, buf in zip((delta, new_m, new_v), _adamw(*packs, name="adamw_small")):
        dst.update(zip(rep, _unpack(buf[0], rep_shapes)))
    flat = lambda a: a.reshape(1, DEPTH * CONV_W, LANES)
    for dst, buf in zip((delta, new_m, new_v),
                        _adamw(flat(w_dw), flat(grads_out["w_dw"]), flat(m_w_dw), flat(v_w_dw), name="adamw_w_dw")):
        dst["w_dw"] = buf.reshape(w_dw.shape)

    order = ("g_mix", "w_in", "b_gate", "q_norm_g", "k_norm_g", "w_attn_o", "w_dw", "b_dw", "conv_ln_g", "conv_ln_b",
             "w_conv_o", "sg_ln_g", "sg_ln_b", "w_s", "b_s", "w_sg_o", "w_out", "g_ffn", "w_ff_gate", "w_ff_up",
             "w_ff_down", "g_final")
    return (loss, grad_x, *[grads_out[n] for n in order], *[delta[n] for n in order],
            *[new_m[n] for n in order], *[new_v[n] for n in order])
```

```python
import functools
import math

import jax
import jax.numpy as jnp
from jax import lax
from jax.experimental import pallas as pl
from jax.experimental.pallas import tpu as pltpu

f32, bf16 = jnp.float32, jnp.bfloat16

D_MODEL = 2048
SEQ = 2048
DEPTH = 2
GRID_W = 64
HEAD_DIM = 128
LANES = 128
N_Q = (D_MODEL // 2) // HEAD_DIM
N_KV = N_Q // 4
GRP = N_Q // N_KV
Q_COLS = N_Q * HEAD_DIM
KV_COLS = N_KV * HEAD_DIM
CONV_CH = D_MODEL // 2
CONV_W = 31
CONV_PAD = CONV_W // 2
CONV_WP = 32
SG_CH = D_MODEL // 2
SG_G = SG_CH // LANES
SG_CHUNK = 128
D_FF = -(-8 * D_MODEL // (3 * 256)) * 256
OFF_KV = Q_COLS
OFF_CONV = OFF_KV + 2 * KV_COLS
OFF_SG = OFF_CONV + 2 * CONV_CH
OFF_GATE = OFF_SG + 2 * SG_CH
IN_COLS = OFF_GATE + 3 * D_MODEL
ROPE_THETA = 10000.0
SCALE = HEAD_DIM ** -0.5
N_DEV = 8
N_CHIP = 4

ADAM_LR, ADAM_B1, ADAM_B2, ADAM_EPS, ADAM_WD, ADAM_STEP = 0.001, 0.9, 0.999, 1e-08, 0.01, 10

VMEM_BYTES_V7X = 64 << 20
VMEM_CAP = VMEM_BYTES_V7X - (6 << 20)
MESH = pl.DeviceIdType.MESH
HBM = pl.BlockSpec(memory_space=pltpu.HBM)


def _in_hbm(a):
    if isinstance(a, jax.Array) and jnp.issubdtype(a.dtype, jnp.floating) and a.size * a.dtype.itemsize >= (1 << 20):
        return pltpu.with_memory_space_constraint(a, pltpu.HBM)
    return a


def _out_hbm(s):
    if isinstance(s, jax.ShapeDtypeStruct) and math.prod(s.shape) * jnp.dtype(s.dtype).itemsize >= (1 << 20):
        return pltpu.HBM(s.shape, s.dtype)
    return s


def _call(body, **kw):
    shapes = kw.pop("out_shape")
    shapes = type(shapes)(_out_hbm(s) for s in shapes) if isinstance(shapes, (list, tuple)) else _out_hbm(shapes)
    call = pl.pallas_call(body, out_shape=shapes, **kw)
    return lambda *args: call(*[_in_hbm(a) for a in args])


def _pick(n, cands):
    for c in cands:
        if n % c == 0:
            return c
    raise ValueError((n, cands))


def _params(sem, vmem_bytes):
    return pltpu.CompilerParams(dimension_semantics=sem, vmem_limit_bytes=int(min(max(vmem_bytes, 16 << 20), VMEM_CAP)))


def _mm(a, b, form, out_dtype, *, n=None, b_off=0, res=None, after=None, name):
    if form == "tn":
        K, M = a.shape
    else:
        M, K = a.shape
    N = n if n is not None else (b.shape[0] if form == "nt" else b.shape[1])
    if K <= 2048:
        tk = K
        if form == "tn":
            tm = _pick(M, (512, 256, 128))
            tn = N if N <= 2048 else _pick(N, (1024, 512, 256, 128))
        else:
            tm = M if M <= 2048 else _pick(M, (2048, 1024, 512))
            tn = _pick(math.gcd(N, b_off) if b_off else N, (256, 128) if res is not None else (512, 256, 128))
    else:
        tk = max(t for t in range(LANES, 3072 + 1, LANES) if K % t == 0)
        tm = _pick(M, (1024, 512, 256, 128))
        tn = _pick(math.gcd(N, b_off) if b_off else N, (1024, 512, 256, 128))
    assert b_off % tn == 0
    off = b_off // tn
    nk = K // tk
    if form == "tn":
        a_spec = pl.BlockSpec((tk, tm), lambda i, j, k: (k, i))
    else:
        a_spec = pl.BlockSpec((tm, tk), lambda i, j, k: (i, k))
    if form == "nt":
        b_spec = pl.BlockSpec((tn, tk), lambda i, j, k: (j + off, k))
    else:
        b_spec = pl.BlockSpec((tk, tn), lambda i, j, k: (k, j + off))
    dims = {"nn": ((1,), (0,)), "nt": ((1,), (1,)), "tn": ((0,), (0,))}[form]
    has_res = res is not None

    def body(*refs):
        if after is not None:
            refs = refs[1:]
        if has_res:
            a_ref, b_ref, r_ref, o_ref = refs[:4]
        else:
            a_ref, b_ref, o_ref = refs[:3]
        p = lax.dot_general(a_ref[...], b_ref[...], (dims, ((), ())), preferred_element_type=f32)

        def finish(acc):
            if has_res:
                acc = acc + r_ref[...].astype(f32)
            o_ref[...] = acc.astype(o_ref.dtype)

        if nk == 1:
            finish(p)
        else:
            acc_ref = refs[-1]
            k = pl.program_id(2)

            @pl.when(k == 0)
            def _():
                acc_ref[...] = p

            @pl.when(k > 0)
            def _():
                acc_ref[...] += p

            @pl.when(k == nk - 1)
            def _():
                finish(acc_ref[...])

    in_specs = [a_spec, b_spec]
    args = [a, b]
    osz = jnp.dtype(out_dtype).itemsize
    vmem = 2 * (tm * tk * 2 + tk * tn * 2 + tm * tn * osz) + 2 * tm * tn * 4
    if has_res:
        in_specs.append(pl.BlockSpec((tm, tn), lambda i, j, k: (i, j)))
        args.append(res)
        vmem += 2 * tm * tn * res.dtype.itemsize
    scratch = []
    if nk > 1:
        scratch.append(pltpu.VMEM((tm, tn), f32))
        vmem += tm * tn * 4
    if after is not None:
        in_specs.insert(0, pl.BlockSpec(memory_space=pl.ANY))
        args.insert(0, after)
    return _call(
        body, name=name, grid=(M // tm, N // tn, nk),
        in_specs=in_specs, out_specs=pl.BlockSpec((tm, tn), lambda i, j, k: (i, j)),
        out_shape=jax.ShapeDtypeStruct((M, N), out_dtype), scratch_shapes=scratch,
        compiler_params=_params(("parallel", "parallel", "arbitrary"), vmem + (8 << 20)),
    )(*args)


def _rows(body, ins, outs, *, tm, name, vmem=40 << 20):
    nrows = next(s[1].shape[0] for s in ins if s[0] == "r")
    in_specs, args = [], []
    for s in ins:
        arr = s[1]
        if s[0] == "r":
            w = s[2] if len(s) > 2 else arr.shape[1]
            cb = s[3] if len(s) > 3 else 0
            in_specs.append(pl.BlockSpec((tm, w), functools.partial(lambda i, cb: (i, cb), cb=cb)))
        else:
            in_specs.append(pl.BlockSpec(arr.shape, functools.partial(lambda i, nd: (0,) * nd, nd=arr.ndim)))
        args.append(arr)
    out_specs, out_shape = [], []
    for s in outs:
        if s[0] == "r":
            out_specs.append(pl.BlockSpec((tm, s[1]), lambda i: (i, 0)))
            out_shape.append(jax.ShapeDtypeStruct((nrows, s[1]), s[2]))
        else:
            out_specs.append(pl.BlockSpec(s[1], functools.partial(lambda i, nd: (0,) * nd, nd=len(s[1]))))
            out_shape.append(jax.ShapeDtypeStruct(s[1], s[2]))
    return _call(body, name=name, grid=(nrows // tm,), in_specs=in_specs, out_specs=out_specs,
                 out_shape=out_shape, compiler_params=_params(("arbitrary",), vmem))(*args)


def _accumulate(ref, part):
    i = pl.program_id(0)

    @pl.when(i == 0)
    def _():
        ref[...] = part

    @pl.when(i > 0)
    def _():
        ref[...] += part


def _rms_stats(x):
    r = lax.rsqrt(jnp.mean(x * x, axis=-1, keepdims=True) + 1e-6)
    return r, x * r


def _rms_fwd(x, g, name):
    def body(x_ref, g_ref, o_ref):
        _, xn = _rms_stats(x_ref[...])
        o_ref[...] = (xn * g_ref[...]).astype(o_ref.dtype)

    return _rows(body, [("r", x), ("f", g)], [("r", x.shape[1], bf16)], tm=min(256, x.shape[0]), name=name)[0]


def _rms_bwd(x, g, dh, dres, name):
    D = x.shape[1]

    def body(x_ref, g_ref, dh_ref, dr_ref, dx_ref, dxb_ref, dg_ref):
        r, xn = _rms_stats(x_ref[...])
        dy = dh_ref[...].astype(f32)
        dxn = dy * g_ref[...]
        dx = dr_ref[...] + r * (dxn - xn * jnp.mean(dxn * xn, axis=-1, keepdims=True))
        dx_ref[...] = dx
        dxb_ref[...] = dx.astype(bf16)
        _accumulate(dg_ref, jnp.sum(dy * xn, axis=0, keepdims=True))

    return _rows(body, [("r", x), ("f", g), ("r", dh), ("r", dres)],
                 [("r", D, f32), ("r", D, bf16), ("a", (1, D), f32)], tm=min(256, x.shape[0]), name=name)


def _final_loss(x, g, tgt, name):
    D = x.shape[1]

    def body(x_ref, g_ref, t_ref, dx_ref, dxb_ref, sq_ref, dg_ref):
        r, xn = _rms_stats(x_ref[...])
        gain = g_ref[...]
        diff = xn * gain - t_ref[...]
        dy = diff * (1.0 / D)
        dxn = dy * gain
        dx = r * (dxn - xn * jnp.mean(dxn * xn, axis=-1, keepdims=True))
        dx_ref[...] = dx
        dxb_ref[...] = dx.astype(bf16)
        _accumulate(sq_ref, jnp.sum(diff * diff, axis=0, keepdims=True))
        _accumulate(dg_ref, jnp.sum(dy * xn, axis=0, keepdims=True))

    return _rows(body, [("r", x), ("f", g), ("r", tgt)],
                 [("r", D, f32), ("r", D, bf16), ("a", (1, D), f32), ("a", (1, D), f32)],
                 tm=min(256, x.shape[0]), name=name)


def _qk_fwd(q_raw, kv_raw, qg, kg, cos2, sin2, name):
    def body(q_ref, k_ref, qg_ref, kg_ref, c_ref, s_ref, qo_ref, ko_ref):
        c, s = c_ref[...], s_ref[...]

        def head(src, gain, dst, h):
            cols = slice(h * HEAD_DIM, (h + 1) * HEAD_DIM)
            _, xn = _rms_stats(src[:, cols].astype(f32))
            y = xn * gain
            dst[:, cols] = (y * c + pltpu.roll(y, HEAD_DIM // 2, 1) * s).astype(dst.dtype)

        for h in range(N_Q):
            head(q_ref, qg_ref[...], qo_ref, h)
        for h in range(N_KV):
            head(k_ref, kg_ref[...], ko_ref, h)

    return _rows(body, [("r", q_raw), ("r", kv_raw, KV_COLS, 0), ("f", qg), ("f", kg), ("r", cos2), ("r", sin2)],
                 [("r", Q_COLS, bf16), ("r", KV_COLS, bf16)], tm=min(256, q_raw.shape[0]), name=name)


def _qk_bwd(q_raw, kv_raw, dqr, dkr, qg, kg, cos2, sin2, name):
    def body(q_ref, k_ref, dq_ref, dk_ref, qg_ref, kg_ref, c_ref, s_ref, dqo_ref, dko_ref, dqg_ref, dkg_ref):
        c, s = c_ref[...], s_ref[...]

        def head(src, dsrc, gain, dst, h):
            cols = slice(h * HEAD_DIM, (h + 1) * HEAD_DIM)
            r, xn = _rms_stats(src[:, cols].astype(f32))
            do = dsrc[:, cols].astype(f32)
            dy = do * c + pltpu.roll(do * s, HEAD_DIM // 2, 1)
            dxn = dy * gain
            dst[:, cols] = (r * (dxn - xn * jnp.mean(dxn * xn, axis=-1, keepdims=True))).astype(dst.dtype)
            return jnp.sum(dy * xn, axis=0, keepdims=True)

        dq_gain = head(q_ref, dq_ref, qg_ref[...], dqo_ref, 0)
        for h in range(1, N_Q):
            dq_gain = dq_gain + head(q_ref, dq_ref, qg_ref[...], dqo_ref, h)
        dk_gain = head(k_ref, dk_ref, kg_ref[...], dko_ref, 0)
        for h in range(1, N_KV):
            dk_gain = dk_gain + head(k_ref, dk_ref, kg_ref[...], dko_ref, h)
        _accumulate(dqg_ref, dq_gain)
        _accumulate(dkg_ref, dk_gain)

    return _rows(body, [("r", q_raw), ("r", kv_raw, KV_COLS, 0), ("r", dqr), ("r", dkr), ("f", qg), ("f", kg),
                        ("r", cos2), ("r", sin2)],
                 [("r", Q_COLS, bf16), ("r", KV_COLS, bf16), ("a", (1, HEAD_DIM), f32), ("a", (1, HEAD_DIM), f32)],
                 tm=min(256, q_raw.shape[0]), name=name)


def _softmax_rows(q, k):
    s = lax.dot_general(q, k, (((1,), (1,)), ((), ())), preferred_element_type=f32) * SCALE
    p = jnp.exp(s - jnp.max(s, axis=-1, keepdims=True))
    return p * (1.0 / jnp.sum(p, axis=-1, keepdims=True))


def _attn_fwd(qr, kr, kv_raw, name):
    S = qr.shape[0]
    tq = min(512, S)

    def body(q_ref, k_ref, v_ref, o_ref):
        p = _softmax_rows(q_ref[...], k_ref[...])
        o_ref[...] = jnp.dot(p.astype(bf16), v_ref[...], preferred_element_type=f32).astype(o_ref.dtype)

    return _call(
        body, name=name, grid=(N_Q, S // tq),
        in_specs=[pl.BlockSpec((tq, HEAD_DIM), lambda h, i: (i, h)),
                  pl.BlockSpec((S, HEAD_DIM), lambda h, i: (0, h // GRP)),
                  pl.BlockSpec((S, HEAD_DIM), lambda h, i: (0, N_KV + h // GRP))],
        out_specs=pl.BlockSpec((tq, HEAD_DIM), lambda h, i: (i, h)),
        out_shape=jax.ShapeDtypeStruct((S, Q_COLS), bf16),
        compiler_params=_params(("parallel", "arbitrary"), 6 * tq * S * 4 + (8 << 20)),
    )(qr, kr, kv_raw)


def _attn_bwd(qr, kr, kv_raw, do, name):
    S = qr.shape[0]
    tq = min(256, S)

    def body(q_ref, k_ref, v_ref, do_ref, dq_ref, dk_ref, dv_ref):
        first = jnp.logical_and(pl.program_id(1) == 0, pl.program_id(2) == 0)
        q, k, v, do_ = q_ref[...], k_ref[...], v_ref[...], do_ref[...]
        p = _softmax_rows(q, k)
        dp = lax.dot_general(do_, v, (((1,), (1,)), ((), ())), preferred_element_type=f32)
        ds = (p * (dp - jnp.sum(dp * p, axis=-1, keepdims=True)) * SCALE).astype(bf16)
        dq_ref[...] = jnp.dot(ds, k, preferred_element_type=f32).astype(dq_ref.dtype)
        dv_part = lax.dot_general(p.astype(bf16), do_, (((0,), (0,)), ((), ())), preferred_element_type=f32)
        dk_part = lax.dot_general(ds, q, (((0,), (0,)), ((), ())), preferred_element_type=f32)

        @pl.when(first)
        def _():
            dv_ref[...] = dv_part
            dk_ref[...] = dk_part

        @pl.when(jnp.logical_not(first))
        def _():
            dv_ref[...] += dv_part
            dk_ref[...] += dk_part

    qmap = lambda kv, g, i: (i, kv * GRP + g)
    return _call(
        body, name=name, grid=(N_KV, GRP, S // tq),
        in_specs=[pl.BlockSpec((tq, HEAD_DIM), qmap),
                  pl.BlockSpec((S, HEAD_DIM), lambda kv, g, i: (0, kv)),
                  pl.BlockSpec((S, HEAD_DIM), lambda kv, g, i: (0, N_KV + kv)),
                  pl.BlockSpec((tq, HEAD_DIM), qmap)],
        out_specs=[pl.BlockSpec((tq, HEAD_DIM), qmap),
                   pl.BlockSpec((S, HEAD_DIM), lambda kv, g, i: (0, kv)),
                   pl.BlockSpec((S, HEAD_DIM), lambda kv, g, i: (0, kv))],
        out_shape=[jax.ShapeDtypeStruct((S, Q_COLS), bf16), jax.ShapeDtypeStruct((S, KV_COLS), f32),
                   jax.ShapeDtypeStruct((S, KV_COLS), f32)],
        compiler_params=_params(("parallel", "arbitrary", "arbitrary"), 8 * tq * S * 4 + (8 << 20)),
    )(qr, kr, kv_raw, do)


CONV_HALO = 16


def _fill_padded(pad_ref, val, S):
    pad_ref[pl.ds(0, CONV_HALO), :] = jnp.zeros((CONV_HALO, LANES), f32)
    pad_ref[pl.ds(CONV_HALO + S, CONV_HALO), :] = jnp.zeros((CONV_HALO, LANES), f32)
    pad_ref[pl.ds(CONV_HALO, S), :] = val


def _group_specs(S, n_groups, second_half):
    return pl.BlockSpec((S, LANES), functools.partial(lambda g, o: (0, g + o), o=n_groups if second_half else 0))


def _conv1_fwd(conv_in, wdw, b_dw, name):
    S = conv_in.shape[0]
    ng = CONV_CH // LANES
    R = min(256, S)

    def body(a_ref, g_ref, w_ref, b_ref, o_ref, pad_ref):
        z = a_ref[...].astype(f32) * jax.nn.sigmoid(g_ref[...].astype(f32))
        _fill_padded(pad_ref, z, S)
        for r in range(S // R):
            acc = jnp.zeros((R, LANES), f32) + b_ref[...]
            for j in range(CONV_W):
                acc = acc + w_ref[pl.ds(j, 1), :] * pad_ref[pl.ds(r * R + CONV_HALO - CONV_PAD + j, R), :]
            o_ref[pl.ds(r * R, R), :] = acc

    return _call(
        body, name=name, grid=(ng,),
        in_specs=[_group_specs(S, ng, False), _group_specs(S, ng, True),
                  pl.BlockSpec((CONV_WP, LANES), lambda g: (g, 0)), pl.BlockSpec((1, LANES), lambda g: (0, g))],
        out_specs=pl.BlockSpec((S, LANES), lambda g: (0, g)),
        out_shape=jax.ShapeDtypeStruct((S, CONV_CH), f32),
        scratch_shapes=[pltpu.VMEM((S + 2 * CONV_HALO, LANES), f32)],
        compiler_params=_params(("parallel",), 24 << 20),
    )(conv_in, conv_in, wdw, b_dw)


def _conv1_bwd(conv_in, dc, wdw, name):
    S = conv_in.shape[0]
    ng = CONV_CH // LANES
    R = min(256, S)

    def body(a_ref, g_ref, w_ref, dc_ref, da_ref, dg_ref, dw_ref, db_ref, padz_ref, padd_ref):
        a = a_ref[...].astype(f32)
        sg = jax.nn.sigmoid(g_ref[...].astype(f32))
        _fill_padded(padz_ref, a * sg, S)
        _fill_padded(padd_ref, dc_ref[...], S)
        for r in range(S // R):
            dz = jnp.zeros((R, LANES), f32)
            for j in range(CONV_W):
                dz = dz + w_ref[pl.ds(j, 1), :] * padd_ref[pl.ds(r * R + CONV_HALO + CONV_PAD - j, R), :]
            rows = pl.ds(r * R, R)
            ar, sr = a_ref[rows, :].astype(f32), jax.nn.sigmoid(g_ref[rows, :].astype(f32))
            da_ref[rows, :] = (dz * sr).astype(da_ref.dtype)
            dg_ref[rows, :] = (dz * ar * sr * (1.0 - sr)).astype(dg_ref.dtype)
        for j in range(CONV_W):
            tot = jnp.zeros((1, LANES), f32)
            for r in range(S // R):
                tot = tot + jnp.sum(dc_ref[pl.ds(r * R, R), :] * padz_ref[pl.ds(r * R + CONV_HALO - CONV_PAD + j, R), :],
                                    axis=0, keepdims=True)
            dw_ref[pl.ds(j, 1), :] = tot
        dw_ref[pl.ds(CONV_W, CONV_WP - CONV_W), :] = jnp.zeros((CONV_WP - CONV_W, LANES), f32)
        db_ref[...] = jnp.sum(dc_ref[...], axis=0, keepdims=True)

    return _call(
        body, name=name, grid=(ng,),
        in_specs=[_group_specs(S, ng, False), _group_specs(S, ng, True),
                  pl.BlockSpec((CONV_WP, LANES), lambda g: (g, 0)), pl.BlockSpec((S, LANES), lambda g: (0, g))],
        out_specs=[pl.BlockSpec((S, LANES), lambda g: (0, g)), pl.BlockSpec((S, LANES), lambda g: (0, g)),
                   pl.BlockSpec((CONV_WP, LANES), lambda g: (g, 0)), pl.BlockSpec((1, LANES), lambda g: (0, g))],
        out_shape=[jax.ShapeDtypeStruct((S, CONV_CH), bf16), jax.ShapeDtypeStruct((S, CONV_CH), bf16),
                   jax.ShapeDtypeStruct((ng * CONV_WP, LANES), f32), jax.ShapeDtypeStruct((1, CONV_CH), f32)],
        scratch_shapes=[pltpu.VMEM((S + 2 * CONV_HALO, LANES), f32), pltpu.VMEM((S + 2 * CONV_HALO, LANES), f32)],
        compiler_params=_params(("parallel",), 24 << 20),
    )(conv_in, conv_in, wdw, dc)


def _ln_stats(x, eps=1e-5):
    xc = x - jnp.mean(x, axis=-1, keepdims=True)
    r = lax.rsqrt(jnp.mean(xc * xc, axis=-1, keepdims=True) + eps)
    return r, xc * r


def _ln_bwd(r, xh, dxh):
    return r * (dxh - jnp.mean(dxh, axis=-1, keepdims=True) - xh * jnp.mean(dxh * xh, axis=-1, keepdims=True))


def _conv2_fwd(c, ln_g, ln_b, name):
    def body(c_ref, g_ref, b_ref, o_ref):
        _, xh = _ln_stats(c_ref[...])
        y = xh * g_ref[...] + b_ref[...]
        o_ref[...] = (y * jax.nn.sigmoid(y)).astype(o_ref.dtype)

    return _rows(body, [("r", c), ("f", ln_g), ("f", ln_b)], [("r", CONV_CH, bf16)], tm=min(256, c.shape[0]), name=name)[0]


def _conv2_bwd(c, dcz, ln_g, ln_b, name):
    def body(c_ref, d_ref, g_ref, b_ref, dc_ref, dg_ref, db_ref):
        r, xh = _ln_stats(c_ref[...])
        y = xh * g_ref[...] + b_ref[...]
        sg = jax.nn.sigmoid(y)
        dy = d_ref[...].astype(f32) * (sg * (1.0 + y * (1.0 - sg)))
        dc_ref[...] = _ln_bwd(r, xh, dy * g_ref[...])
        _accumulate(dg_ref, jnp.sum(dy * xh, axis=0, keepdims=True))
        _accumulate(db_ref, jnp.sum(dy, axis=0, keepdims=True))

    return _rows(body, [("r", c), ("r", dcz), ("f", ln_g), ("f", ln_b)],
                 [("r", CONV_CH, f32), ("a", (1, CONV_CH), f32), ("a", (1, CONV_CH), f32)],
                 tm=min(256, c.shape[0]), name=name)


GELU_K = math.sqrt(2.0 / math.pi)
GELU_C = 0.044715


def _gelu(x):
    return 0.5 * x * (1.0 + jnp.tanh(GELU_K * (x + GELU_C * x * x * x)))


def _gelu_grad(x):
    th = jnp.tanh(GELU_K * (x + GELU_C * x * x * x))
    return 0.5 * (1.0 + th) + 0.5 * x * (1.0 - th * th) * (GELU_K * (1.0 + 3.0 * GELU_C * x * x))


def _chunk_rows(n):
    return pl.ds(pl.multiple_of(n * SG_CHUNK, SG_CHUNK), SG_CHUNK)


def _sgu_fwd(sg_in, ln_g, ln_b, w_s, b_s, name):
    S = sg_in.shape[0]

    def body(u_ref, v_ref, lg_ref, lb_ref, w_ref, b_ref, o_ref):
        wb = w_ref[...].astype(bf16)

        def chunk(n, carry):
            rows = _chunk_rows(n)
            gu = _gelu(u_ref[rows, :].astype(f32))
            _, xh = _ln_stats(_gelu(v_ref[rows, :].astype(f32)))
            vl = xh * lg_ref[...] + lb_ref[...]
            t = jnp.dot(wb, vl.astype(bf16), preferred_element_type=f32) + b_ref[...]
            o_ref[rows, :] = (gu * t).astype(o_ref.dtype)
            return carry

        lax.fori_loop(0, S // SG_CHUNK, chunk, 0)

    return _call(
        body, name=name, grid=(SG_G,),
        in_specs=[_group_specs(S, SG_G, False), _group_specs(S, SG_G, True),
                  pl.BlockSpec((1, LANES), lambda g: (0, g)), pl.BlockSpec((1, LANES), lambda g: (0, g)),
                  pl.BlockSpec((None, SG_CHUNK, SG_CHUNK), lambda g: (g, 0, 0)),
                  pl.BlockSpec((None, SG_CHUNK, 1), lambda g: (g, 0, 0))],
        out_specs=pl.BlockSpec((S, LANES), lambda g: (0, g)),
        out_shape=jax.ShapeDtypeStruct((S, SG_CH), bf16),
        compiler_params=_params(("parallel",), 24 << 20),
    )(sg_in, sg_in, ln_g, ln_b, w_s, b_s)


def _sgu_bwd(sg_in, dsz, ln_g, ln_b, w_s, w_s_t, b_s, name):
    S = sg_in.shape[0]

    def body(u_ref, v_ref, lg_ref, lb_ref, w_ref, wt_ref, b_ref, d_ref, du_ref, dv_ref, dw_ref, db_ref, dlg_ref, dlb_ref):
        wb = w_ref[...].astype(bf16)
        wtb = wt_ref[...].astype(bf16)

        def chunk(n, carry):
            dwa, dba, dlga, dlba = carry
            rows = _chunk_rows(n)
            u = u_ref[rows, :].astype(f32)
            v = v_ref[rows, :].astype(f32)
            gu = _gelu(u)
            r, xh = _ln_stats(_gelu(v))
            vlb = (xh * lg_ref[...] + lb_ref[...]).astype(bf16)
            t = jnp.dot(wb, vlb, preferred_element_type=f32) + b_ref[...]
            d = d_ref[rows, :].astype(f32)
            dt = d * gu
            dtb = dt.astype(bf16)
            dwa = dwa + lax.dot_general(dtb, vlb, (((1,), (1,)), ((), ())), preferred_element_type=f32)
            dba = dba + jnp.sum(dt, axis=1, keepdims=True)
            dvl = jnp.dot(wtb, dtb, preferred_element_type=f32)
            dlga = dlga + jnp.sum(dvl * xh, axis=0, keepdims=True)
            dlba = dlba + jnp.sum(dvl, axis=0, keepdims=True)
            dgv = _ln_bwd(r, xh, dvl * lg_ref[...])
            du_ref[rows, :] = (d * t * _gelu_grad(u)).astype(du_ref.dtype)
            dv_ref[rows, :] = (dgv * _gelu_grad(v)).astype(dv_ref.dtype)
            return dwa, dba, dlga, dlba

        init = (jnp.zeros((SG_CHUNK, SG_CHUNK), f32), jnp.zeros((SG_CHUNK, 1), f32),
                jnp.zeros((1, LANES), f32), jnp.zeros((1, LANES), f32))
        dwa, dba, dlga, dlba = lax.fori_loop(0, S // SG_CHUNK, chunk, init)
        dw_ref[...] = dwa
        db_ref[...] = dba
        dlg_ref[...] = dlga
        dlb_ref[...] = dlba

    wspec = pl.BlockSpec((None, SG_CHUNK, SG_CHUNK), lambda g: (g, 0, 0))
    bspec = pl.BlockSpec((None, SG_CHUNK, 1), lambda g: (g, 0, 0))
    lspec = pl.BlockSpec((1, LANES), lambda g: (0, g))
    cspec = pl.BlockSpec((S, LANES), lambda g: (0, g))
    return _call(
        body, name=name, grid=(SG_G,),
        in_specs=[_group_specs(S, SG_G, False), _group_specs(S, SG_G, True), lspec, lspec, wspec, wspec, bspec, cspec],
        out_specs=[cspec, cspec, wspec, bspec, lspec, lspec],
        out_shape=[jax.ShapeDtypeStruct((S, SG_CH), bf16), jax.ShapeDtypeStruct((S, SG_CH), bf16),
                   jax.ShapeDtypeStruct((SG_G, SG_CHUNK, SG_CHUNK), f32), jax.ShapeDtypeStruct((SG_G, SG_CHUNK, 1), f32),
                   jax.ShapeDtypeStruct((1, SG_CH), f32), jax.ShapeDtypeStruct((1, SG_CH), f32)],
        compiler_params=_params(("parallel",), 24 << 20),
    )(sg_in, sg_in, ln_g, ln_b, w_s, w_s_t, b_s, dsz)


def _merge_fwd(gl, b_gate, ya, yc, ys, name):
    D = ya.shape[1]

    def body(gl_ref, b_ref, ya_ref, yc_ref, ys_ref, o_ref):
        acc = jnp.zeros(o_ref.shape, f32)
        for i, y_ref in enumerate((ya_ref, yc_ref, ys_ref)):
            cols = slice(i * D, (i + 1) * D)
            acc = acc + jax.nn.sigmoid(gl_ref[:, cols].astype(f32) + b_ref[:, cols]) * y_ref[...].astype(f32)
        o_ref[...] = acc.astype(o_ref.dtype)

    return _rows(body, [("r", gl), ("f", b_gate), ("r", ya), ("r", yc), ("r", ys)], [("r", D, bf16)],
                 tm=min(128, gl.shape[0]), name=name)[0]


def _merge_bwd(dm, gl, b_gate, ya, yc, ys, name):
    D = ya.shape[1]

    def body(dm_ref, gl_ref, b_ref, ya_ref, yc_ref, ys_ref, dgl_ref, dya_ref, dyc_ref, dys_ref, db_ref):
        dm_ = dm_ref[...].astype(f32)
        for i, (y_ref, dy_ref) in enumerate(((ya_ref, dya_ref), (yc_ref, dyc_ref), (ys_ref, dys_ref))):
            cols = slice(i * D, (i + 1) * D)
            gate = jax.nn.sigmoid(gl_ref[:, cols].astype(f32) + b_ref[:, cols])
            dy_ref[...] = (dm_ * gate).astype(dy_ref.dtype)
            dlog = dm_ * y_ref[...].astype(f32) * gate * (1.0 - gate)
            dgl_ref[:, cols] = dlog.astype(dgl_ref.dtype)
            part = jnp.sum(dlog, axis=0, keepdims=True)
            first = pl.program_id(0) == 0

            @pl.when(first)
            def _():
                db_ref[:, cols] = part

            @pl.when(jnp.logical_not(first))
            def _():
                db_ref[:, cols] += part

    return _rows(body, [("r", dm), ("r", gl), ("f", b_gate), ("r", ya), ("r", yc), ("r", ys)],
                 [("r", 3 * D, bf16), ("r", D, bf16), ("r", D, bf16), ("r", D, bf16), ("a", (1, 3 * D), f32)],
                 tm=min(128, gl.shape[0]), name=name)


def _swiglu_fwd(fg, fu, name):
    def body(g_ref, u_ref, o_ref):
        g = g_ref[...].astype(f32)
        o_ref[...] = (g * jax.nn.sigmoid(g) * u_ref[...].astype(f32)).astype(o_ref.dtype)

    return _rows(body, [("r", fg), ("r", fu)], [("r", fg.shape[1], bf16)], tm=min(128, fg.shape[0]), name=name)[0]


def _swiglu_bwd(dact, fg, fu, name):
    def body(d_ref, g_ref, u_ref, dg_ref, du_ref):
        d = d_ref[...].astype(f32)
        g = g_ref[...].astype(f32)
        sg = jax.nn.sigmoid(g)
        dg_ref[...] = (d * u_ref[...].astype(f32) * sg * (1.0 + g * (1.0 - sg))).astype(dg_ref.dtype)
        du_ref[...] = (d * g * sg).astype(du_ref.dtype)

    return _rows(body, [("r", dact), ("r", fg), ("r", fu)], [("r", fg.shape[1], bf16), ("r", fg.shape[1], bf16)],
                 tm=min(128, fg.shape[0]), name=name)


def _row_tile(r, c, n_arrays, itemsize=4):
    fits = [tm for tm in range(16, r + 1, 16) if r % tm == 0 and 2 * n_arrays * tm * c * itemsize <= (24 << 20)]
    return fits[-1] if fits else r


def _sum_slots(slots, name):
    n, r, c = slots.shape
    tm = _row_tile(r, c, n + 2)

    def body(s_ref, o_ref):
        acc = s_ref[0].astype(f32)
        for k in range(1, n):
            acc = acc + s_ref[k].astype(f32)
        o_ref[...] = acc

    return _call(body, name=name, grid=(r // tm,),
                 in_specs=[pl.BlockSpec((n, tm, c), lambda i: (0, i, 0))],
                 out_specs=pl.BlockSpec((tm, c), lambda i: (i, 0)),
                 out_shape=jax.ShapeDtypeStruct((r, c), f32),
                 compiler_params=_params(("parallel",), 40 << 20))(slots)


def _add_sibling(g4, recv, core, name):
    _, _, r, c = g4.shape
    tm = _row_tile(r, c, 3, 2)

    def body(core_ref, g_ref, r_ref, o_ref):
        o_ref[...] = (g_ref[...].astype(f32) + r_ref[...].astype(f32)).astype(o_ref.dtype)

    grid_spec = pltpu.PrefetchScalarGridSpec(
        num_scalar_prefetch=1, grid=(N_CHIP, r // tm),
        in_specs=[pl.BlockSpec((None, None, tm, c), lambda k, i, core_ref: (k, core_ref[0], i, 0)),
                  pl.BlockSpec((None, tm, c), lambda k, i, core_ref: (k, i, 0))],
        out_specs=pl.BlockSpec((None, tm, c), lambda k, i, core_ref: (k, i, 0)))
    return _call(body, name=name, grid_spec=grid_spec, out_shape=jax.ShapeDtypeStruct((N_CHIP, r, c), bf16),
                 compiler_params=_params(("parallel", "parallel"), 40 << 20))(core, g4, recv)


def _adamw(w, g, m, v, name):
    L, r, c = w.shape
    tm = _row_tile(r, c, 7)
    c1 = 1.0 - ADAM_B1 ** ADAM_STEP
    c2 = 1.0 - ADAM_B2 ** ADAM_STEP

    def body(w_ref, g_ref, m_ref, v_ref, d_ref, mo_ref, vo_ref):
        g_ = g_ref[...]
        m_ = ADAM_B1 * m_ref[...] + (1.0 - ADAM_B1) * g_
        v_ = ADAM_B2 * v_ref[...] + (1.0 - ADAM_B2) * (g_ * g_)
        d_ref[...] = -ADAM_LR * ((m_ / c1) / (jnp.sqrt(v_ / c2) + ADAM_EPS) + ADAM_WD * w_ref[...])
        mo_ref[...] = m_
        vo_ref[...] = v_

    spec = pl.BlockSpec((None, tm, c), lambda l, i: (l, i, 0))
    shp = jax.ShapeDtypeStruct((L, r, c), f32)
    return _call(body, name=name, grid=(L, r // tm), in_specs=[spec] * 4, out_specs=[spec] * 3,
                 out_shape=[shp] * 3, compiler_params=_params(("parallel", "parallel"), 40 << 20))(w, g, m, v)


def _mesh_pos():
    return lax.axis_index("x"), lax.axis_index("y"), lax.axis_index("c")


def _all_gather(shards, after, name):
    n = len(shards)

    def body(*refs):
        x_refs, o_refs = refs[:n], refs[n + 1:2 * n + 1]
        send_sems, recv_sems, local_sems = refs[2 * n + 1:]
        x, y, c = _mesh_pos()
        me, sibling = (x, y, c), (x, y, 1 - c)
        chips = [(1 - x, y), (x, 1 - y), (1 - x, 1 - y)]

        def rows(k, px, py, pc):
            return o_refs[k].at[4 * px + 2 * py + pc]

        def copy(k, s, block, to, src=None):
            return pltpu.make_async_remote_copy(
                src_ref=rows(k, *block) if src is None else src, dst_ref=rows(k, *block),
                send_sem=send_sems.at[k, s], recv_sem=recv_sems.at[k, s], device_id=to, device_id_type=MESH)

        mine = [pltpu.make_async_copy(x_refs[k], rows(k, *me), local_sems.at[k]) for k in range(n)]
        for cp in mine:
            cp.start()
        first = [copy(k, 0, me, sibling, src=x_refs[k]) for k in range(n)]
        for j, chip in enumerate(chips):
            first += [copy(k, 1 + j, me, (*chip, c), src=x_refs[k]) for k in range(n)]
        for cp in first:
            cp.start()
        passed = []
        for j, chip in enumerate(chips):
            for k in range(n):
                copy(k, 1 + j, (*chip, c), me).wait_recv()
                fwd = copy(k, 4 + j, (*chip, c), sibling)
                fwd.start()
                passed.append(fwd)
        for k in range(n):
            copy(k, 0, sibling, me).wait_recv()
        for j, chip in enumerate(chips):
            for k in range(n):
                copy(k, 4 + j, (*chip, 1 - c), me).wait_recv()
        for cp in first + passed:
            cp.wait_send()
        for cp in mine:
            cp.wait()

    return _call(
        body, name=name, in_specs=[HBM] * n + [ANY], out_specs=[HBM] * n,
        out_shape=[jax.ShapeDtypeStruct((N_DEV,) + s.shape, s.dtype) for s in shards],
        scratch_shapes=[pltpu.SemaphoreType.DMA((n, 7)), pltpu.SemaphoreType.DMA((n, 7)), pltpu.SemaphoreType.DMA((n,))],
    )(*shards, after)


def _send_to_sibling(g4s, name):
    n = len(g4s)

    def body(*refs):
        g_refs, o_refs = refs[:n], refs[n:2 * n]
        send_sems, recv_sems = refs[2 * n:]
        x, y, c = _mesh_pos()
        copies = [pltpu.make_async_remote_copy(
            src_ref=g_refs[k].at[:, 1 - c], dst_ref=o_refs[k], send_sem=send_sems.at[k], recv_sem=recv_sems.at[k],
            device_id=(x, y, 1 - c), device_id_type=MESH) for k in range(n)]
        for cp in copies:
            cp.start()
        for cp in copies:
            cp.wait()

    return _call(
        body, name=name, in_specs=[HBM] * n, out_specs=[HBM] * n,
        out_shape=[jax.ShapeDtypeStruct((N_CHIP,) + g.shape[2:], g.dtype) for g in g4s],
        scratch_shapes=[pltpu.SemaphoreType.DMA((n,)), pltpu.SemaphoreType.DMA((n,))],
    )(*g4s)


SEM = pl.BlockSpec(memory_space=pltpu.SEMAPHORE)
ANY = pl.BlockSpec(memory_space=pl.ANY)
EFFECT = pltpu.SideEffectType.DATAFLOW_SIDE_EFFECTING


def _other_chips(x, y):
    return [(1 - x, y), (x, 1 - y), (1 - x, 1 - y)]


def _ici_copy(kind, src_ref, land_ref, send_sem, recv_sem, sender, target, c):
    (sx, sy), (tx, ty) = sender, target
    if kind == "gather":
        src, dst = src_ref, land_ref.at[4 * sx + 2 * sy + c]
    else:
        src, dst = src_ref.at[2 * tx + ty], land_ref.at[2 * sx + sy]
    return pltpu.make_async_remote_copy(src_ref=src, dst_ref=dst, send_sem=send_sem, recv_sem=recv_sem,
                                        device_id=(tx, ty, c), device_id_type=MESH)


def _ici_start(kind, srcs, lands, after, name):
    n = len(srcs)

    def body(*refs):
        src_refs, land_refs = refs[:n], refs[n:2 * n]
        send_sems, recv_sems = refs[2 * n + 1], refs[2 * n + 2]
        token = refs[-1]
        x, y, c = _mesh_pos()
        for j, chip in enumerate(_other_chips(x, y)):
            for k in range(n):
                _ici_copy(kind, src_refs[k], land_refs[k], send_sems.at[3 * k + j], recv_sems.at[3 * k + j],(x, y), chip, c).start()
        token[...] = jnp.zeros_like(token)

    bufs = list(srcs) + list(lands)
    return _call(
        body, name=name,
        out_shape=(pltpu.SemaphoreType.DMA((3 * n,)), pltpu.SemaphoreType.DMA((3 * n,)),
                   *[pltpu.HBM(b.shape, b.dtype) for b in bufs], jax.ShapeDtypeStruct((8, LANES), f32)),
        in_specs=[HBM] * (2 * n) + [ANY], out_specs=(SEM, SEM, *[HBM] * (2 * n), pl.BlockSpec(memory_space=pltpu.VMEM)),
        input_output_aliases={i: 2 + i for i in range(2 * n)},
        compiler_params=pltpu.CompilerParams(has_side_effects=EFFECT),
    )(*[pltpu.with_memory_space_constraint(b, pltpu.HBM) for b in bufs], after)


def _ici_wait(kind, started, after, name):
    send_sems, recv_sems, *bufs = started[:-1]
    n = len(bufs) // 2

    def body(*refs):
        src_refs, land_refs = refs[:n], refs[n:2 * n]
        send_sems, recv_sems = refs[2 * n], refs[2 * n + 1]
        x, y, c = _mesh_pos()
        for j, chip in enumerate(_other_chips(x, y)):
            for k in range(n):
                _ici_copy(kind, src_refs[k], land_refs[k], send_sems.at[3 * k + j], recv_sems.at[3 * k + j],(x, y), chip, c).wait_send()
                _ici_copy(kind, src_refs[k], land_refs[k], send_sems.at[3 * k + j], recv_sems.at[3 * k + j],chip, (x, y), c).wait_recv()

    out = _call(
        body, name=name, out_shape=[pltpu.HBM(b.shape, b.dtype) for b in bufs],
        in_specs=[HBM] * (2 * n) + [SEM, SEM, ANY], out_specs=[HBM] * (2 * n),
        input_output_aliases={i: i for i in range(2 * n)},
        compiler_params=pltpu.CompilerParams(has_side_effects=EFFECT),
    )(*bufs, send_sems, recv_sems, after)
    return out[:n], out[n:]


def _d2d_gather(shards, lands, after, name):
    n = len(shards)

    def body(*refs):
        x_refs, in_refs, o_refs = refs[:n], refs[n:2 * n], refs[2 * n + 1:3 * n + 1]
        send_sems, recv_sems, local_sems = refs[3 * n + 1:]
        x, y, c = _mesh_pos()
        sibling = (x, y, 1 - c)
        blocks = [(x, y)] + _other_chips(x, y)

        def rows(ref, k, px, py, pc):
            return ref.at[4 * px + 2 * py + pc]

        mine = [pltpu.make_async_copy(x_refs[k], rows(o_refs[k], k, x, y, c), local_sems.at[k]) for k in range(n)]
        for cp in mine:
            cp.start()
        copies = []
        for s, (px, py) in enumerate(blocks):
            for k in range(n):
                src = x_refs[k] if s == 0 else rows(in_refs[k], k, px, py, c)
                copies.append(pltpu.make_async_remote_copy(
                    src_ref=src, dst_ref=rows(o_refs[k], k, px, py, c), send_sem=send_sems.at[k, s],
                    recv_sem=recv_sems.at[k, s], device_id=sibling, device_id_type=MESH))
        for cp in copies:
            cp.start()
        for s, (px, py) in enumerate(blocks):
            for k in range(n):
                pltpu.make_async_remote_copy(
                    src_ref=x_refs[k], dst_ref=rows(o_refs[k], k, px, py, 1 - c), send_sem=send_sems.at[k, s],
                    recv_sem=recv_sems.at[k, s], device_id=sibling, device_id_type=MESH).wait_recv()
        for cp in copies:
            cp.wait_send()
        for cp in mine:
            cp.wait()

    return _call(
        body, name=name, in_specs=[HBM] * (2 * n) + [ANY], out_specs=[HBM] * n,
        out_shape=[jax.ShapeDtypeStruct(b.shape, b.dtype) for b in lands],
        input_output_aliases={n + k: k for k in range(n)},
        scratch_shapes=[pltpu.SemaphoreType.DMA((n, 4)), pltpu.SemaphoreType.DMA((n, 4)), pltpu.SemaphoreType.DMA((n,))],
    )(*shards, *lands, after)


def _sum_chip_slots(lands, sums, chip, name):
    _, r, c = lands.shape
    tm = _row_tile(r, c, 10, 2)

    def body(chip_ref, l_ref, s_ref, o_ref):
        acc = None
        for k in range(N_CHIP):
            part = jnp.where(chip_ref[0] == k, s_ref[k], l_ref[k]).astype(f32)
            acc = part if acc is None else acc + part
        o_ref[...] = acc

    grid_spec = pltpu.PrefetchScalarGridSpec(
        num_scalar_prefetch=1, grid=(r // tm,),
        in_specs=[pl.BlockSpec((N_CHIP, tm, c), lambda i, chip_ref: (0, i, 0)),
                  pl.BlockSpec((N_CHIP, tm, c), lambda i, chip_ref: (0, i, 0))],
        out_specs=pl.BlockSpec((tm, c), lambda i, chip_ref: (i, 0)))
    return _call(body, name=name, grid_spec=grid_spec, out_shape=jax.ShapeDtypeStruct((r, c), f32),
                 compiler_params=_params(("parallel",), 40 << 20))(chip, lands, sums)


def _reduce_start(grads, core, after, tag):
    g4s = [g.reshape(N_CHIP, 2, g.shape[0] // N_DEV, g.shape[1]) for g in grads]
    recv = _send_to_sibling(g4s, name="rs_sibling_" + tag)
    sums = [_add_sibling(g4, rv, core, name="rs_add_" + tag) for g4, rv in zip(g4s, recv)]
    lands = [lax.empty(s.shape, s.dtype) for s in sums]
    return _ici_start("reduce", sums, lands, after, name="rs_start_" + tag)


def _reduce_finish(started, chip, after, tag):
    sums, lands = _ici_wait("reduce", started, after, name="rs_wait_" + tag)
    return [_sum_chip_slots(ld, s, chip, name="rs_sum_" + tag) for ld, s in zip(lands, sums)]


def _rope_tables(S):
    rows = S // GRID_W
    row = jnp.repeat(jnp.arange(rows, dtype=f32), GRID_W)
    col = jnp.tile(jnp.arange(GRID_W, dtype=f32), rows)
    nf = HEAD_DIM // 4
    inv = ROPE_THETA ** (-jnp.arange(nf, dtype=f32) / nf)
    ang = jnp.concatenate([row[:, None] * inv, col[:, None] * inv], axis=-1)
    cos, sin = jnp.cos(ang), jnp.sin(ang)
    return jnp.concatenate([cos, cos], axis=-1), jnp.concatenate([-sin, sin], axis=-1)


def _layer_fwd(xin, p, w, rest_of_weights, cos2, sin2):
    sv = {"xin": xin}
    h = sv["h"] = _rms_fwd(xin, p["g_mix"], name="rms_mix")
    proj = functools.partial(_mm, h, w["in"], "nt", bf16)
    q_raw = sv["q_raw"] = proj(n=Q_COLS, b_off=0, name="proj_q")
    kv_raw = sv["kv_raw"] = proj(n=2 * KV_COLS, b_off=OFF_KV, name="proj_kv")
    conv_in = sv["conv_in"] = proj(n=2 * CONV_CH, b_off=OFF_CONV, name="proj_conv")
    sg_in = sv["sg_in"] = proj(n=2 * SG_CH, b_off=OFF_SG, name="proj_sg")
    gl = sv["gl"] = proj(n=3 * D_MODEL, b_off=OFF_GATE, name="proj_gate")
    qr, kr = sv["qr"], sv["kr"] = _qk_fwd(q_raw, kv_raw, p["q_norm_g"], p["k_norm_g"], cos2, sin2, name="qk_fwd")
    o = sv["o"] = _attn_fwd(qr, kr, kv_raw, name="attn_fwd")
    c = sv["c"] = _conv1_fwd(conv_in, w["dw"], p["b_dw"], name="conv1_fwd")
    cz = sv["cz"] = _conv2_fwd(c, p["conv_ln_g"], p["conv_ln_b"], name="conv2_fwd")
    sz = sv["sz"] = _sgu_fwd(sg_in, p["sg_ln_g"], p["sg_ln_b"], p["w_s"], p["b_s"], name="sgu_fwd")
    w = rest_of_weights(sz)
    ya = sv["ya"] = _mm(o, w["attn_o"], "nt", bf16, name="out_attn")
    yc = sv["yc"] = _mm(cz, w["conv_o"], "nt", bf16, name="out_conv")
    ys = sv["ys"] = _mm(sz, w["sg_o"], "nt", bf16, name="out_sg")
    merged = sv["merged"] = _merge_fwd(gl, p["b_gate"], ya, yc, ys, name="merge_fwd")
    x1 = sv["x1"] = _mm(merged, w["out"], "nn", f32, res=xin, name="out_proj")
    hf = sv["hf"] = _rms_fwd(x1, p["g_ffn"], name="rms_ffn")
    fg = sv["fg"] = _mm(hf, w["ff_gate"], "nt", bf16, name="ff_gate")
    fu = sv["fu"] = _mm(hf, w["ff_up"], "nt", bf16, name="ff_up")
    act = sv["act"] = _swiglu_fwd(fg, fu, name="swiglu_fwd")
    x2 = _mm(act, w["ff_down"], "nn", f32, res=x1, name="ff_down")
    return x2, sv, w


def _layer_bwd(dx2, dx2b, sv, p, w, cos2, sin2, reduce_start):
    small = {}
    dact = _mm(dx2b, w["ff_down"], "nt", bf16, name="d_act")
    g_down = _mm(sv["act"], dx2b, "tn", bf16, name="g_ff_down")
    dfg, dfu = _swiglu_bwd(dact, sv["fg"], sv["fu"], name="swiglu_bwd")
    dhf = _mm(dfg, w["ff_gate"], "nn", f32, name="d_hf_gate")
    dhf = _mm(dfu, w["ff_up"], "nn", f32, res=dhf, name="d_hf_up")
    g_gate = _mm(dfg, sv["hf"], "tn", bf16, name="g_ff_gate")
    g_up = _mm(dfu, sv["hf"], "tn", bf16, name="g_ff_up")
    zero = reduce_start("ffn", dict(w_ff_gate=g_gate, w_ff_up=g_up, w_ff_down=g_down))[0, 0]
    dx1, dx1b, small["g_ffn"] = _rms_bwd(sv["x1"], p["g_ffn"] + zero, dhf, dx2, name="rms_ffn_bwd")
    dmerged = _mm(dx1b, w["out"], "nt", bf16, name="d_merged")
    g_out = _mm(sv["merged"], dx1b, "tn", bf16, name="g_out")
    dgl, dya, dyc, dys, small["b_gate"] = _merge_bwd(dmerged, sv["gl"], p["b_gate"], sv["ya"], sv["yc"], sv["ys"],
                                                    name="merge_bwd")
    do = _mm(dya, w["attn_o"], "nn", bf16, name="d_o")
    g_ao = _mm(dya, sv["o"], "tn", bf16, name="g_attn_o")
    dcz = _mm(dyc, w["conv_o"], "nn", bf16, name="d_cz")
    g_co = _mm(dyc, sv["cz"], "tn", bf16, name="g_conv_o")
    dsz = _mm(dys, w["sg_o"], "nn", bf16, name="d_sz")
    g_so = _mm(dys, sv["sz"], "tn", bf16, name="g_sg_o")
    zero = reduce_start("mix", dict(w_attn_o=g_ao, w_conv_o=g_co, w_sg_o=g_so, w_out=g_out))[0, 0]
    dsu, dsv, small["w_s"], small["b_s"], small["sg_ln_g"], small["sg_ln_b"] = _sgu_bwd(
        sv["sg_in"], dsz, p["sg_ln_g"] + zero, p["sg_ln_b"], p["w_s"], p["w_s_t"], p["b_s"], name="sgu_bwd")
    dc, small["conv_ln_g"], small["conv_ln_b"] = _conv2_bwd(sv["c"], dcz, p["conv_ln_g"], p["conv_ln_b"], name="conv2_bwd")
    da, dgt, small["w_dw"], small["b_dw"] = _conv1_bwd(sv["conv_in"], dc, w["dw"], name="conv1_bwd")
    dqr, dkr, dv = _attn_bwd(sv["qr"], sv["kr"], sv["kv_raw"], do, name="attn_bwd")
    dq_raw, dk_raw, small["q_norm_g"], small["k_norm_g"] = _qk_bwd(
        sv["q_raw"], sv["kv_raw"], dqr, dkr, p["q_norm_g"], p["k_norm_g"], cos2, sin2, name="qk_bwd")
    dproj = jnp.concatenate([dq_raw, dk_raw, dv.astype(bf16), da, dgt, dsu, dsv, dgl], axis=1)
    g_in = _mm(dproj, sv["h"], "tn", bf16, name="g_in")
    started = reduce_start("in", dict(w_in=g_in))
    dh = _mm(dproj, w["in"], "nn", f32, after=started, name="d_h")
    dx, dxb, small["g_mix"] = _rms_bwd(sv["xin"], p["g_mix"], dh, dx1, name="rms_mix_bwd")
    return dx, dxb, small


SMALL = ("g_mix", "b_gate", "q_norm_g", "k_norm_g", "b_dw", "conv_ln_g", "conv_ln_b", "sg_ln_g", "sg_ln_b",
         "w_s", "b_s", "g_ffn")
PACK_ALIGN = 8 * LANES


def _pack(parts):
    flat = jnp.concatenate([a.reshape(-1).astype(f32) for a in parts])
    pad = -flat.shape[0] % PACK_ALIGN
    return jnp.pad(flat, (0, pad)).reshape(-1, LANES)


def _unpack(buf, shapes):
    flat = buf.reshape(-1)
    out, pos = [], 0
    for shp in shapes:
        size = math.prod(shp)
        out.append(flat[pos:pos + size].reshape(shp))
        pos += size
    return out


def kernel(x, g_mix, w_in, b_gate, q_norm_g, k_norm_g, w_attn_o, w_dw, b_dw, conv_ln_g, conv_ln_b, w_conv_o, sg_ln_g, sg_ln_b, w_s, b_s, w_sg_o, w_out, g_ffn, w_ff_gate, w_ff_up, w_ff_down, g_final, loss_target, m_g_mix, m_w_in, m_b_gate, m_q_norm_g, m_k_norm_g, m_w_attn_o, m_w_dw, m_b_dw, m_conv_ln_g, m_conv_ln_b, m_w_conv_o, m_sg_ln_g, m_sg_ln_b, m_w_s, m_b_s, m_w_sg_o, m_w_out, m_g_ffn, m_w_ff_gate, m_w_ff_up, m_w_ff_down, m_g_final, v_g_mix, v_w_in, v_b_gate, v_q_norm_g, v_k_norm_g, v_w_attn_o, v_w_dw, v_b_dw, v_conv_ln_g, v_conv_ln_b, v_w_conv_o, v_sg_ln_g, v_sg_ln_b, v_w_s, v_b_s, v_w_sg_o, v_w_out, v_g_ffn, v_w_ff_gate, v_w_ff_up, v_w_ff_down, v_g_final):
    weights = dict(g_mix=g_mix, w_in=w_in, b_gate=b_gate, q_norm_g=q_norm_g, k_norm_g=k_norm_g, w_attn_o=w_attn_o,
                   w_dw=w_dw, b_dw=b_dw, conv_ln_g=conv_ln_g, conv_ln_b=conv_ln_b, w_conv_o=w_conv_o, sg_ln_g=sg_ln_g,
                   sg_ln_b=sg_ln_b, w_s=w_s, b_s=b_s, w_sg_o=w_sg_o, w_out=w_out, g_ffn=g_ffn, w_ff_gate=w_ff_gate,
                   w_ff_up=w_ff_up, w_ff_down=w_ff_down, g_final=g_final)
    mom_m = dict(g_mix=m_g_mix, w_in=m_w_in, b_gate=m_b_gate, q_norm_g=m_q_norm_g, k_norm_g=m_k_norm_g,
                 w_attn_o=m_w_attn_o, w_dw=m_w_dw, b_dw=m_b_dw, conv_ln_g=m_conv_ln_g, conv_ln_b=m_conv_ln_b,
                 w_conv_o=m_w_conv_o, sg_ln_g=m_sg_ln_g, sg_ln_b=m_sg_ln_b, w_s=m_w_s, b_s=m_b_s, w_sg_o=m_w_sg_o,
                 w_out=m_w_out, g_ffn=m_g_ffn, w_ff_gate=m_w_ff_gate, w_ff_up=m_w_ff_up, w_ff_down=m_w_ff_down,
                 g_final=m_g_final)
    mom_v = dict(g_mix=v_g_mix, w_in=v_w_in, b_gate=v_b_gate, q_norm_g=v_q_norm_g, k_norm_g=v_k_norm_g,
                 w_attn_o=v_w_attn_o, w_dw=v_w_dw, b_dw=v_b_dw, conv_ln_g=v_conv_ln_g, conv_ln_b=v_conv_ln_b,
                 w_conv_o=v_w_conv_o, sg_ln_g=v_sg_ln_g, sg_ln_b=v_sg_ln_b, w_s=v_w_s, b_s=v_b_s, w_sg_o=v_w_sg_o,
                 w_out=v_w_out, g_ffn=v_g_ffn, w_ff_gate=v_w_ff_gate, w_ff_up=v_w_ff_up, w_ff_down=v_w_ff_down,
                 g_final=v_g_final)
    S, D = x.shape[1], x.shape[2]
    xi, yi, ci = _mesh_pos()
    me = 4 * xi + 2 * yi + ci
    core = jnp.reshape(ci, (1,)).astype(jnp.int32)
    cos2, sin2 = _rope_tables(S)

    big = ("w_in", "w_attn_o", "w_conv_o", "w_sg_o", "w_out", "w_ff_gate", "w_ff_up", "w_ff_down")
    transposed = {"w_in", "w_attn_o", "w_conv_o", "w_sg_o", "w_ff_gate", "w_ff_up"}
    chip = jnp.reshape(2 * xi + yi, (1,)).astype(jnp.int32)
    groups = (("in", "dw"), ("attn_o", "conv_o", "sg_o", "out", "ff_gate", "ff_up", "ff_down"))
    P, shards = [], []
    for l in range(DEPTH):
        sh = {n[2:]: (weights[n][l].T if n in transposed else weights[n][l]).astype(bf16) for n in big}
        sh["dw"] = jnp.pad(w_dw[l].reshape(CONV_W, LANES), ((0, CONV_WP - CONV_W), (0, 0)))
        shards.append(sh)
        p = {n: weights[n][l].reshape(1, -1) for n in SMALL if n not in ("w_s", "b_s")}
        p["w_s"] = w_s[l]
        p["w_s_t"] = jnp.swapaxes(w_s[l], 1, 2)
        p["b_s"] = b_s[l].reshape(SG_G, SG_CHUNK, 1)
        P.append(p)

    gathers = {}

    def start_gather(l, gi, after):
        srcs = [shards[l][n] for n in groups[gi]]
        lands = [lax.empty((N_DEV,) + s.shape, s.dtype) for s in srcs]
        gathers[l, gi] = _ici_start("gather", srcs, lands, after, name=f"ag_start_{l}{gi}")
        return gathers[l, gi][-1]

    def gathered(l, gi, after):
        srcs, lands = _ici_wait("gather", gathers[l, gi], after, name=f"ag_wait_{l}{gi}")
        after = start_gather(l + 1, gi, srcs[0]) if l + 1 < DEPTH else srcs[0]
        full = _d2d_gather(srcs, lands, after, name=f"ag_d2d_{gi}")
        return {n: f.reshape(-1, f.shape[2]) for n, f in zip(groups[gi], full)}

    all_started = start_gather(0, 1, start_gather(0, 0, cos2))

    h = x.reshape(S, D)
    saved, W = [], []
    for l in range(DEPTH):
        first = gathered(l, 0, all_started if l == 0 else h)
        if l == 0:
            P[l]["g_mix"] = P[l]["g_mix"] + all_started[0, 0]
        h, sv, w = _layer_fwd(h, P[l], first, lambda z, l=l, first=first: {**first, **gathered(l, 1, z)}, cos2, sin2)
        saved.append(sv)
        W.append(w)
    dx, dxb, sq, g_final_part = _final_loss(h, g_final.reshape(1, D), loss_target.reshape(S, D), name="final_loss")
    loss = lax.psum(0.5 * jnp.sum(sq) / D, ("x", "y", "c"))

    reductions, small_grads = {}, [None] * DEPTH
    for l in reversed(range(DEPTH)):
        def reduce_start(group, grads, l=l):
            names = tuple(grads)
            started = _reduce_start([grads[n] for n in names], core, grads[names[0]], tag=f"{group}{l}")
            reductions[l, group] = (names, started)
            return started[-1]

        dx, dxb, small_grads[l] = _layer_bwd(dx, dxb, saved[l], P[l], W[l], cos2, sin2, reduce_start)
    grad_x = dx.reshape(x.shape)

    grads_out, delta, new_m, new_v = {}, {}, {}, {}
    swap = lambda a: jnp.swapaxes(a, 1, 2)

    def update(n, per_layer):
        g = jnp.stack(per_layer)
        if n in transposed and weights[n].shape[2] % LANES:
            upd = _adamw(swap(weights[n]), g, swap(mom_m[n]), swap(mom_v[n]), name="adamw_" + n)
            grads_out[n], (delta[n], new_m[n], new_v[n]) = swap(g), [swap(u) for u in upd]
        else:
            grads_out[n] = swap(g) if n in transposed else g
            delta[n], new_m[n], new_v[n] = _adamw(weights[n], grads_out[n], mom_m[n], mom_v[n], name="adamw_" + n)
        return delta[n]

    after = dx
    for group in ("ffn", "mix", "in"):
        done = [dict(zip(reductions[l, group][0], _reduce_finish(reductions[l, group][1], chip, after, tag=f"{group}{l}")))
                for l in range(DEPTH)]
        for n in reductions[0, group][0]:
            after = update(n, [done[l][n] for l in range(DEPTH)])

    small_shapes = [weights[n].shape for n in SMALL] + [g_final.shape, (DEPTH, CONV_CH // LANES, CONV_WP, LANES)]
    parts = [jnp.stack([small_grads[l][n].reshape(weights[n].shape[1:]) for l in range(DEPTH)]) for n in SMALL]
    parts += [g_final_part.reshape(g_final.shape), jnp.stack([small_grads[l]["w_dw"] for l in range(DEPTH)])]
    packed = _pack(parts)
    gathered = _all_gather([packed], after, name="gather_small")[0]
    total = _sum_slots(gathered, name="sum_small")
    small_total = _unpack(total, small_shapes)
    grads_out.update(zip(SMALL + ("g_final",), small_total[:-1]))
    dw_full = small_total[-1]
    grads_out["w_dw"] = lax.dynamic_index_in_dim(dw_full, me, axis=1, keepdims=False)[:, :CONV_W].reshape(w_dw.shape)

    rep = SMALL + ("g_final",)
    rep_shapes = [weights[n].shape for n in rep]
    packs = [_pack([src[n] for n in rep])[None] for src in (weights, grads_out, mom_m, mom_v)]
    for dst, buf in zip((delta, new_m, new_v), _adamw(*packs, name="adamw_small")):
        dst.update(zip(rep, _unpack(buf[0], rep_shapes)))
    flat = lambda a: a.reshape(1, DEPTH * CONV_W, LANES)
    for dst, buf in zip((delta, new_m, new_v),
                        _adamw(flat(w_dw), flat(grads_out["w_dw"]), flat(m_w_dw), flat(v_w_dw), name="adamw_w_dw")):
        dst["w_dw"] = buf.reshape(w_dw.shape)

    order = ("g_mix", "w_in", "b_gate", "q_norm_g", "k_norm_g", "w_attn_o", "w_dw", "b_dw", "conv_ln_g", "conv_ln_b",
             "w_conv_o", "sg_ln_g", "sg_ln_b", "w_s", "b_s", "w_sg_o", "w_out", "g_ffn", "w_ff_gate", "w_ff_up",
             "w_ff_down", "g_final")
    return (loss, grad_x, *[grads_out[n] for n in order], *[delta[n] for n in order],
            *[new_m[n] for n in order], *[new_v[n] for n in order])
```

```python
import functools
import math

import jax
import jax.numpy as jnp
from jax import lax
from jax.experimental import pallas as pl
from jax.experimental.pallas import tpu as pltpu

f32, bf16 = jnp.float32, jnp.bfloat16

D_MODEL = 2048
SEQ = 2048
DEPTH = 2
GRID_W = 64
HEAD_DIM = 128
LANES = 128
N_Q = (D_MODEL // 2) // HEAD_DIM
N_KV = N_Q // 4
GRP = N_Q // N_KV
Q_COLS = N_Q * HEAD_DIM
KV_COLS = N_KV * HEAD_DIM
CONV_CH = D_MODEL // 2
CONV_W = 31
CONV_PAD = CONV_W // 2
CONV_WP = 32
SG_CH = D_MODEL // 2
SG_G = SG_CH // LANES
SG_CHUNK = 128
D_FF = -(-8 * D_MODEL // (3 * 256)) * 256
OFF_KV = Q_COLS
OFF_CONV = OFF_KV + 2 * KV_COLS
OFF_SG = OFF_CONV + 2 * CONV_CH
OFF_GATE = OFF_SG + 2 * SG_CH
IN_COLS = OFF_GATE + 3 * D_MODEL
ROPE_THETA = 10000.0
SCALE = HEAD_DIM ** -0.5
N_DEV = 8
N_CHIP = 4

ADAM_LR, ADAM_B1, ADAM_B2, ADAM_EPS, ADAM_WD, ADAM_STEP = 0.001, 0.9, 0.999, 1e-08, 0.01, 10

VMEM_BYTES_V7X = 64 << 20
VMEM_CAP = VMEM_BYTES_V7X - (6 << 20)
MESH = pl.DeviceIdType.MESH
HBM = pl.BlockSpec(memory_space=pltpu.HBM)


def _in_hbm(a):
    if isinstance(a, jax.Array) and jnp.issubdtype(a.dtype, jnp.floating) and a.size * a.dtype.itemsize >= (1 << 20):
        return pltpu.with_memory_space_constraint(a, pltpu.HBM)
    return a


def _out_hbm(s):
    if isinstance(s, jax.ShapeDtypeStruct) and math.prod(s.shape) * jnp.dtype(s.dtype).itemsize >= (1 << 20):
        return pltpu.HBM(s.shape, s.dtype)
    return s


def _call(body, **kw):
    shapes = kw.pop("out_shape")
    shapes = type(shapes)(_out_hbm(s) for s in shapes) if isinstance(shapes, (list, tuple)) else _out_hbm(shapes)
    call = pl.pallas_call(body, out_shape=shapes, **kw)
    return lambda *args: call(*[_in_hbm(a) for a in args])


def _pick(n, cands):
    for c in cands:
        if n % c == 0:
            return c
    raise ValueError((n, cands))


def _params(sem, vmem_bytes):
    return pltpu.CompilerParams(dimension_semantics=sem, vmem_limit_bytes=int(min(max(vmem_bytes, 16 << 20), VMEM_CAP)))


def _mm(a, b, form, out_dtype, *, n=None, b_off=0, res=None, after=None, name):
    if form == "tn":
        K, M = a.shape
    else:
        M, K = a.shape
    N = n if n is not None else (b.shape[0] if form == "nt" else b.shape[1])
    if K <= 2048:
        tk = K
        if form == "tn":
            tm = _pick(M, (512, 256, 128))
            tn = N if N <= 2048 else _pick(N, (1024, 512, 256, 128))
        else:
            tm = M if M <= 2048 else _pick(M, (2048, 1024, 512))
            tn = _pick(math.gcd(N, b_off) if b_off else N, (256, 128) if res is not None else (512, 256, 128))
    else:
        tk = max(t for t in range(LANES, 3072 + 1, LANES) if K % t == 0)
        tm = _pick(M, (1024, 512, 256, 128))
        tn = _pick(math.gcd(N, b_off) if b_off else N, (1024, 512, 256, 128))
    assert b_off % tn == 0
    off = b_off // tn
    nk = K // tk
    if form == "tn":
        a_spec = pl.BlockSpec((tk, tm), lambda i, j, k: (k, i))
    else:
        a_spec = pl.BlockSpec((tm, tk), lambda i, j, k: (i, k))
    if form == "nt":
        b_spec = pl.BlockSpec((tn, tk), lambda i, j, k: (j + off, k))
    else:
        b_spec = pl.BlockSpec((tk, tn), lambda i, j, k: (k, j + off))
    dims = {"nn": ((1,), (0,)), "nt": ((1,), (1,)), "tn": ((0,), (0,))}[form]
    has_res = res is not None

    def body(*refs):
        if after is not None:
            refs = refs[1:]
        if has_res:
            a_ref, b_ref, r_ref, o_ref = refs[:4]
        else:
            a_ref, b_ref, o_ref = refs[:3]
        p = lax.dot_general(a_ref[...], b_ref[...], (dims, ((), ())), preferred_element_type=f32)

        def finish(acc):
            if has_res:
                acc = acc + r_ref[...].astype(f32)
            o_ref[...] = acc.astype(o_ref.dtype)

        if nk == 1:
            finish(p)
        else:
            acc_ref = refs[-1]
            k = pl.program_id(2)

            @pl.when(k == 0)
            def _():
                acc_ref[...] = p

            @pl.when(k > 0)
            def _():
                acc_ref[...] += p

            @pl.when(k == nk - 1)
            def _():
                finish(acc_ref[...])

    in_specs = [a_spec, b_spec]
    args = [a, b]
    osz = jnp.dtype(out_dtype).itemsize
    vmem = 2 * (tm * tk * 2 + tk * tn * 2 + tm * tn * osz) + 2 * tm * tn * 4
    if has_res:
        in_specs.append(pl.BlockSpec((tm, tn), lambda i, j, k: (i, j)))
        args.append(res)
        vmem += 2 * tm * tn * res.dtype.itemsize
    scratch = []
    if nk > 1:
        scratch.append(pltpu.VMEM((tm, tn), f32))
        vmem += tm * tn * 4
    if after is not None:
        in_specs.insert(0, pl.BlockSpec(memory_space=pl.ANY))
        args.insert(0, after)
    return _call(
        body, name=name, grid=(M // tm, N // tn, nk),
        in_specs=in_specs, out_specs=pl.BlockSpec((tm, tn), lambda i, j, k: (i, j)),
        out_shape=jax.ShapeDtypeStruct((M, N), out_dtype), scratch_shapes=scratch,
        compiler_params=_params(("parallel", "parallel", "arbitrary"), vmem + (8 << 20)),
    )(*args)


def _rows(body, ins, outs, *, tm, name, vmem=40 << 20):
    nrows = next(s[1].shape[0] for s in ins if s[0] == "r")
    in_specs, args = [], []
    for s in ins:
        arr = s[1]
        if s[0] == "r":
            w = s[2] if len(s) > 2 else arr.shape[1]
            cb = s[3] if len(s) > 3 else 0
            in_specs.append(pl.BlockSpec((tm, w), functools.partial(lambda i, cb: (i, cb), cb=cb)))
        else:
            in_specs.append(pl.BlockSpec(arr.shape, functools.partial(lambda i, nd: (0,) * nd, nd=arr.ndim)))
        args.append(arr)
    out_specs, out_shape = [], []
    for s in outs:
        if s[0] == "r":
            out_specs.append(pl.BlockSpec((tm, s[1]), lambda i: (i, 0)))
            out_shape.append(jax.ShapeDtypeStruct((nrows, s[1]), s[2]))
        else:
            out_specs.append(pl.BlockSpec(s[1], functools.partial(lambda i, nd: (0,) * nd, nd=len(s[1]))))
            out_shape.append(jax.ShapeDtypeStruct(s[1], s[2]))
    return _call(body, name=name, grid=(nrows // tm,), in_specs=in_specs, out_specs=out_specs,
                 out_shape=out_shape, compiler_params=_params(("arbitrary",), vmem))(*args)


def _accumulate(ref, part):
    i = pl.program_id(0)

    @pl.when(i == 0)
    def _():
        ref[...] = part

    @pl.when(i > 0)
    def _():
        ref[...] += part


def _rms_stats(x):
    r = lax.rsqrt(jnp.mean(x * x, axis=-1, keepdims=True) + 1e-6)
    return r, x * r


def _rms_fwd(x, g, name):
    def body(x_ref, g_ref, o_ref):
        _, xn = _rms_stats(x_ref[...])
        o_ref[...] = (xn * g_ref[...]).astype(o_ref.dtype)

    return _rows(body, [("r", x), ("f", g)], [("r", x.shape[1], bf16)], tm=min(256, x.shape[0]), name=name)[0]


def _rms_bwd(x, g, dh, dres, name):
    D = x.shape[1]

    def body(x_ref, g_ref, dh_ref, dr_ref, dx_ref, dxb_ref, dg_ref):
        r, xn = _rms_stats(x_ref[...])
        dy = dh_ref[...].astype(f32)
        dxn = dy * g_ref[...]
        dx = dr_ref[...] + r * (dxn - xn * jnp.mean(dxn * xn, axis=-1, keepdims=True))
        dx_ref[...] = dx
        dxb_ref[...] = dx.astype(bf16)
        _accumulate(dg_ref, jnp.sum(dy * xn, axis=0, keepdims=True))

    return _rows(body, [("r", x), ("f", g), ("r", dh), ("r", dres)],
                 [("r", D, f32), ("r", D, bf16), ("a", (1, D), f32)], tm=min(256, x.shape[0]), name=name)


def _final_loss(x, g, tgt, name):
    D = x.shape[1]

    def body(x_ref, g_ref, t_ref, dx_ref, dxb_ref, sq_ref, dg_ref):
        r, xn = _rms_stats(x_ref[...])
        gain = g_ref[...]
        diff = xn * gain - t_ref[...]
        dy = diff * (1.0 / D)
        dxn = dy * gain
        dx = r * (dxn - xn * jnp.mean(dxn * xn, axis=-1, keepdims=True))
        dx_ref[...] = dx
        dxb_ref[...] = dx.astype(bf16)
        _accumulate(sq_ref, jnp.sum(diff * diff, axis=0, keepdims=True))
        _accumulate(dg_ref, jnp.sum(dy * xn, axis=0, keepdims=True))

    return _rows(body, [("r", x), ("f", g), ("r", tgt)],
                 [("r", D, f32), ("r", D, bf16), ("a", (1, D), f32), ("a", (1, D), f32)],
                 tm=min(256, x.shape[0]), name=name)


def _qk_fwd(q_raw, kv_raw, qg, kg, cos2, sin2, name):
    def body(q_ref, k_ref, qg_ref, kg_ref, c_ref, s_ref, qo_ref, ko_ref):
        c, s = c_ref[...], s_ref[...]

        def head(src, gain, dst, h):
            cols = slice(h * HEAD_DIM, (h + 1) * HEAD_DIM)
            _, xn = _rms_stats(src[:, cols].astype(f32))
            y = xn * gain
            dst[:, cols] = (y * c + pltpu.roll(y, HEAD_DIM // 2, 1) * s).astype(dst.dtype)

        for h in range(N_Q):
            head(q_ref, qg_ref[...], qo_ref, h)
        for h in range(N_KV):
            head(k_ref, kg_ref[...], ko_ref, h)

    return _rows(body, [("r", q_raw), ("r", kv_raw, KV_COLS, 0), ("f", qg), ("f", kg), ("r", cos2), ("r", sin2)],
                 [("r", Q_COLS, bf16), ("r", KV_COLS, bf16)], tm=min(256, q_raw.shape[0]), name=name)


def _qk_bwd(q_raw, kv_raw, dqr, dkr, qg, kg, cos2, sin2, name):
    def body(q_ref, k_ref, dq_ref, dk_ref, qg_ref, kg_ref, c_ref, s_ref, dqo_ref, dko_ref, dqg_ref, dkg_ref):
        c, s = c_ref[...], s_ref[...]

        def head(src, dsrc, gain, dst, h):
            cols = slice(h * HEAD_DIM, (h + 1) * HEAD_DIM)
            r, xn = _rms_stats(src[:, cols].astype(f32))
            do = dsrc[:, cols].astype(f32)
            dy = do * c + pltpu.roll(do * s, HEAD_DIM // 2, 1)
            dxn = dy * gain
            dst[:, cols] = (r * (dxn - xn * jnp.mean(dxn * xn, axis=-1, keepdims=True))).astype(dst.dtype)
            return jnp.sum(dy * xn, axis=0, keepdims=True)

        dq_gain = head(q_ref, dq_ref, qg_ref[...], dqo_ref, 0)
        for h in range(1, N_Q):
            dq_gain = dq_gain + head(q_ref, dq_ref, qg_ref[...], dqo_ref, h)
        dk_gain = head(k_ref, dk_ref, kg_ref[...], dko_ref, 0)
        for h in range(1, N_KV):
            dk_gain = dk_gain + head(k_ref, dk_ref, kg_ref[...], dko_ref, h)
        _accumulate(dqg_ref, dq_gain)
        _accumulate(dkg_ref, dk_gain)

    return _rows(body, [("r", q_raw), ("r", kv_raw, KV_COLS, 0), ("r", dqr), ("r", dkr), ("f", qg), ("f", kg),
                        ("r", cos2), ("r", sin2)],
                 [("r", Q_COLS, bf16), ("r", KV_COLS, bf16), ("a", (1, HEAD_DIM), f32), ("a", (1, HEAD_DIM), f32)],
                 tm=min(256, q_raw.shape[0]), name=name)


def _softmax_rows(q, k):
    s = lax.dot_general(q, k, (((1,), (1,)), ((), ())), preferred_element_type=f32) * SCALE
    p = jnp.exp(s - jnp.max(s, axis=-1, keepdims=True))
    return p * (1.0 / jnp.sum(p, axis=-1, keepdims=True))


def _attn_fwd(qr, kr, kv_raw, name):
    S = qr.shape[0]
    tq = min(512, S)

    def body(q_ref, k_ref, v_ref, o_ref):
        p = _softmax_rows(q_ref[...], k_ref[...])
        o_ref[...] = jnp.dot(p.astype(bf16), v_ref[...], preferred_element_type=f32).astype(o_ref.dtype)

    return _call(
        body, name=name, grid=(N_Q, S // tq),
        in_specs=[pl.BlockSpec((tq, HEAD_DIM), lambda h, i: (i, h)),
                  pl.BlockSpec((S, HEAD_DIM), lambda h, i: (0, h // GRP)),
                  pl.BlockSpec((S, HEAD_DIM), lambda h, i: (0, N_KV + h // GRP))],
        out_specs=pl.BlockSpec((tq, HEAD_DIM), lambda h, i: (i, h)),
        out_shape=jax.ShapeDtypeStruct((S, Q_COLS), bf16),
        compiler_params=_params(("parallel", "arbitrary"), 6 * tq * S * 4 + (8 << 20)),
    )(qr, kr, kv_raw)


def _attn_bwd(qr, kr, kv_raw, do, name):
    S = qr.shape[0]
    tq = min(256, S)

    def body(q_ref, k_ref, v_ref, do_ref, dq_ref, dk_ref, dv_ref):
        first = jnp.logical_and(pl.program_id(1) == 0, pl.program_id(2) == 0)
        q, k, v, do_ = q_ref[...], k_ref[...], v_ref[...], do_ref[...]
        p = _softmax_rows(q, k)
        dp = lax.dot_general(do_, v, (((1,), (1,)), ((), ())), preferred_element_type=f32)
        ds = (p * (dp - jnp.sum(dp * p, axis=-1, keepdims=True)) * SCALE).astype(bf16)
        dq_ref[...] = jnp.dot(ds, k, preferred_element_type=f32).astype(dq_ref.dtype)
        dv_part = lax.dot_general(p.astype(bf16), do_, (((0,), (0,)), ((), ())), preferred_element_type=f32)
        dk_part = lax.dot_general(ds, q, (((0,), (0,)), ((), ())), preferred_element_type=f32)

        @pl.when(first)
        def _():
            dv_ref[...] = dv_part
            dk_ref[...] = dk_part

        @pl.when(jnp.logical_not(first))
        def _():
            dv_ref[...] += dv_part
            dk_ref[...] += dk_part

    qmap = lambda kv, g, i: (i, kv * GRP + g)
    return _call(
        body, name=name, grid=(N_KV, GRP, S // tq),
        in_specs=[pl.BlockSpec((tq, HEAD_DIM), qmap),
                  pl.BlockSpec((S, HEAD_DIM), lambda kv, g, i: (0, kv)),
                  pl.BlockSpec((S, HEAD_DIM), lambda kv, g, i: (0, N_KV + kv)),
                  pl.BlockSpec((tq, HEAD_DIM), qmap)],
        out_specs=[pl.BlockSpec((tq, HEAD_DIM), qmap),
                   pl.BlockSpec((S, HEAD_DIM), lambda kv, g, i: (0, kv)),
                   pl.BlockSpec((S, HEAD_DIM), lambda kv, g, i: (0, kv))],
        out_shape=[jax.ShapeDtypeStruct((S, Q_COLS), bf16), jax.ShapeDtypeStruct((S, KV_COLS), f32),
                   jax.ShapeDtypeStruct((S, KV_COLS), f32)],
        compiler_params=_params(("parallel", "arbitrary", "arbitrary"), 8 * tq * S * 4 + (8 << 20)),
    )(qr, kr, kv_raw, do)


CONV_HALO = 16


def _fill_padded(pad_ref, val, S):
    pad_ref[pl.ds(0, CONV_HALO), :] = jnp.zeros((CONV_HALO, LANES), f32)
    pad_ref[pl.ds(CONV_HALO + S, CONV_HALO), :] = jnp.zeros((CONV_HALO, LANES), f32)
    pad_ref[pl.ds(CONV_HALO, S), :] = val


def _group_specs(S, n_groups, second_half):
    return pl.BlockSpec((S, LANES), functools.partial(lambda g, o: (0, g + o), o=n_groups if second_half else 0))


def _conv1_fwd(conv_in, wdw, b_dw, name):
    S = conv_in.shape[0]
    ng = CONV_CH // LANES
    R = min(256, S)

    def body(a_ref, g_ref, w_ref, b_ref, o_ref, pad_ref):
        z = a_ref[...].astype(f32) * jax.nn.sigmoid(g_ref[...].astype(f32))
        _fill_padded(pad_ref, z, S)
        for r in range(S // R):
            acc = jnp.zeros((R, LANES), f32) + b_ref[...]
            for j in range(CONV_W):
                acc = acc + w_ref[pl.ds(j, 1), :] * pad_ref[pl.ds(r * R + CONV_HALO - CONV_PAD + j, R), :]
            o_ref[pl.ds(r * R, R), :] = acc

    return _call(
        body, name=name, grid=(ng,),
        in_specs=[_group_specs(S, ng, False), _group_specs(S, ng, True),
                  pl.BlockSpec((CONV_WP, LANES), lambda g: (g, 0)), pl.BlockSpec((1, LANES), lambda g: (0, g))],
        out_specs=pl.BlockSpec((S, LANES), lambda g: (0, g)),
        out_shape=jax.ShapeDtypeStruct((S, CONV_CH), f32),
        scratch_shapes=[pltpu.VMEM((S + 2 * CONV_HALO, LANES), f32)],
        compiler_params=_params(("parallel",), 24 << 20),
    )(conv_in, conv_in, wdw, b_dw)


def _conv1_bwd(conv_in, dc, wdw, name):
    S = conv_in.shape[0]
    ng = CONV_CH // LANES
    R = min(256, S)

    def body(a_ref, g_ref, w_ref, dc_ref, da_ref, dg_ref, dw_ref, db_ref, padz_ref, padd_ref):
        a = a_ref[...].astype(f32)
        sg = jax.nn.sigmoid(g_ref[...].astype(f32))
        _fill_padded(padz_ref, a * sg, S)
        _fill_padded(padd_ref, dc_ref[...], S)
        for r in range(S // R):
            dz = jnp.zeros((R, LANES), f32)
            for j in range(CONV_W):
                dz = dz + w_ref[pl.ds(j, 1), :] * padd_ref[pl.ds(r * R + CONV_HALO + CONV_PAD - j, R), :]
            rows = pl.ds(r * R, R)
            ar, sr = a_ref[rows, :].astype(f32), jax.nn.sigmoid(g_ref[rows, :].astype(f32))
            da_ref[rows, :] = (dz * sr).astype(da_ref.dtype)
            dg_ref[rows, :] = (dz * ar * sr * (1.0 - sr)).astype(dg_ref.dtype)
        for j in range(CONV_W):
            tot = jnp.zeros((1, LANES), f32)
            for r in range(S // R):
                tot = tot + jnp.sum(dc_ref[pl.ds(r * R, R), :] * padz_ref[pl.ds(r * R + CONV_HALO - CONV_PAD + j, R), :],
                                    axis=0, keepdims=True)
            dw_ref[pl.ds(j, 1), :] = tot
        dw_ref[pl.ds(CONV_W, CONV_WP - CONV_W), :] = jnp.zeros((CONV_WP - CONV_W, LANES), f32)
        db_ref[...] = jnp.sum(dc_ref[...], axis=0, keepdims=True)

    return _call(
        body, name=name, grid=(ng,),
        in_specs=[_group_specs(S, ng, False), _group_specs(S, ng, True),
                  pl.BlockSpec((CONV_WP, LANES), lambda g: (g, 0)), pl.BlockSpec((S, LANES), lambda g: (0, g))],
        out_specs=[pl.BlockSpec((S, LANES), lambda g: (0, g)), pl.BlockSpec((S, LANES), lambda g: (0, g)),
                   pl.BlockSpec((CONV_WP, LANES), lambda g: (g, 0)), pl.BlockSpec((1, LANES), lambda g: (0, g))],
        out_shape=[jax.ShapeDtypeStruct((S, CONV_CH), bf16), jax.ShapeDtypeStruct((S, CONV_CH), bf16),
                   jax.ShapeDtypeStruct((ng * CONV_WP, LANES), f32), jax.ShapeDtypeStruct((1, CONV_CH), f32)],
        scratch_shapes=[pltpu.VMEM((S + 2 * CONV_HALO, LANES), f32), pltpu.VMEM((S + 2 * CONV_HALO, LANES), f32)],
        compiler_params=_params(("parallel",), 24 << 20),
    )(conv_in, conv_in, wdw, dc)


def _ln_stats(x, eps=1e-5):
    xc = x - jnp.mean(x, axis=-1, keepdims=True)
    r = lax.rsqrt(jnp.mean(xc * xc, axis=-1, keepdims=True) + eps)
    return r, xc * r


def _ln_bwd(r, xh, dxh):
    return r * (dxh - jnp.mean(dxh, axis=-1, keepdims=True) - xh * jnp.mean(dxh * xh, axis=-1, keepdims=True))


def _conv2_fwd(c, ln_g, ln_b, name):
    def body(c_ref, g_ref, b_ref, o_ref):
        _, xh = _ln_stats(c_ref[...])
        y = xh * g_ref[...] + b_ref[...]
        o_ref[...] = (y * jax.nn.sigmoid(y)).astype(o_ref.dtype)

    return _rows(body, [("r", c), ("f", ln_g), ("f", ln_b)], [("r", CONV_CH, bf16)], tm=min(256, c.shape[0]), name=name)[0]


def _conv2_bwd(c, dcz, ln_g, ln_b, name):
    def body(c_ref, d_ref, g_ref, b_ref, dc_ref, dg_ref, db_ref):
        r, xh = _ln_stats(c_ref[...])
        y = xh * g_ref[...] + b_ref[...]
        sg = jax.nn.sigmoid(y)
        dy = d_ref[...].astype(f32) * (sg * (1.0 + y * (1.0 - sg)))
        dc_ref[...] = _ln_bwd(r, xh, dy * g_ref[...])
        _accumulate(dg_ref, jnp.sum(dy * xh, axis=0, keepdims=True))
        _accumulate(db_ref, jnp.sum(dy, axis=0, keepdims=True))

    return _rows(body, [("r", c), ("r", dcz), ("f", ln_g), ("f", ln_b)],
                 [("r", CONV_CH, f32), ("a", (1, CONV_CH), f32), ("a", (1, CONV_CH), f32)],
                 tm=min(256, c.shape[0]), name=name)


GELU_K = math.sqrt(2.0 / math.pi)
GELU_C = 0.044715


def _gelu(x):
    return 0.5 * x * (1.0 + jnp.tanh(GELU_K * (x + GELU_C * x * x * x)))


def _gelu_grad(x):
    th = jnp.tanh(GELU_K * (x + GELU_C * x * x * x))
    return 0.5 * (1.0 + th) + 0.5 * x * (1.0 - th * th) * (GELU_K * (1.0 + 3.0 * GELU_C * x * x))


def _chunk_rows(n):
    return pl.ds(pl.multiple_of(n * SG_CHUNK, SG_CHUNK), SG_CHUNK)


def _sgu_fwd(sg_in, ln_g, ln_b, w_s, b_s, name):
    S = sg_in.shape[0]

    def body(u_ref, v_ref, lg_ref, lb_ref, w_ref, b_ref, o_ref):
        wb = w_ref[...].astype(bf16)

        def chunk(n, carry):
            rows = _chunk_rows(n)
            gu = _gelu(u_ref[rows, :].astype(f32))
            _, xh = _ln_stats(_gelu(v_ref[rows, :].astype(f32)))
            vl = xh * lg_ref[...] + lb_ref[...]
            t = jnp.dot(wb, vl.astype(bf16), preferred_element_type=f32) + b_ref[...]
            o_ref[rows, :] = (gu * t).astype(o_ref.dtype)
            return carry

        lax.fori_loop(0, S // SG_CHUNK, chunk, 0)

    return _call(
        body, name=name, grid=(SG_G,),
        in_specs=[_group_specs(S, SG_G, False), _group_specs(S, SG_G, True),
                  pl.BlockSpec((1, LANES), lambda g: (0, g)), pl.BlockSpec((1, LANES), lambda g: (0, g)),
                  pl.BlockSpec((None, SG_CHUNK, SG_CHUNK), lambda g: (g, 0, 0)),
                  pl.BlockSpec((None, SG_CHUNK, 1), lambda g: (g, 0, 0))],
        out_specs=pl.BlockSpec((S, LANES), lambda g: (0, g)),
        out_shape=jax.ShapeDtypeStruct((S, SG_CH), bf16),
        compiler_params=_params(("parallel",), 24 << 20),
    )(sg_in, sg_in, ln_g, ln_b, w_s, b_s)


def _sgu_bwd(sg_in, dsz, ln_g, ln_b, w_s, w_s_t, b_s, name):
    S = sg_in.shape[0]

    def body(u_ref, v_ref, lg_ref, lb_ref, w_ref, wt_ref, b_ref, d_ref, du_ref, dv_ref, dw_ref, db_ref, dlg_ref, dlb_ref):
        wb = w_ref[...].astype(bf16)
        wtb = wt_ref[...].astype(bf16)

        def chunk(n, carry):
            dwa, dba, dlga, dlba = carry
            rows = _chunk_rows(n)
            u = u_ref[rows, :].astype(f32)
            v = v_ref[rows, :].astype(f32)
            gu = _gelu(u)
            r, xh = _ln_stats(_gelu(v))
            vlb = (xh * lg_ref[...] + lb_ref[...]).astype(bf16)
            t = jnp.dot(wb, vlb, preferred_element_type=f32) + b_ref[...]
            d = d_ref[rows, :].astype(f32)
            dt = d * gu
            dtb = dt.astype(bf16)
            dwa = dwa + lax.dot_general(dtb, vlb, (((1,), (1,)), ((), ())), preferred_element_type=f32)
            dba = dba + jnp.sum(dt, axis=1, keepdims=True)
            dvl = jnp.dot(wtb, dtb, preferred_element_type=f32)
            dlga = dlga + jnp.sum(dvl * xh, axis=0, keepdims=True)
            dlba = dlba + jnp.sum(dvl, axis=0, keepdims=True)
            dgv = _ln_bwd(r, xh, dvl * lg_ref[...])
            du_ref[rows, :] = (d * t * _gelu_grad(u)).astype(du_ref.dtype)
            dv_ref[rows, :] = (dgv * _gelu_grad(v)).astype(dv_ref.dtype)
            return dwa, dba, dlga, dlba

        init = (jnp.zeros((SG_CHUNK, SG_CHUNK), f32), jnp.zeros((SG_CHUNK, 1), f32),
                jnp.zeros((1, LANES), f32), jnp.zeros((1, LANES), f32))
        dwa, dba, dlga, dlba = lax.fori_loop(0, S // SG_CHUNK, chunk, init)
        dw_ref[...] = dwa
        db_ref[...] = dba
        dlg_ref[...] = dlga
        dlb_ref[...] = dlba

    wspec = pl.BlockSpec((None, SG_CHUNK, SG_CHUNK), lambda g: (g, 0, 0))
    bspec = pl.BlockSpec((None, SG_CHUNK, 1), lambda g: (g, 0, 0))
    lspec = pl.BlockSpec((1, LANES), lambda g: (0, g))
    cspec = pl.BlockSpec((S, LANES), lambda g: (0, g))
    return _call(
        body, name=name, grid=(SG_G,),
        in_specs=[_group_specs(S, SG_G, False), _group_specs(S, SG_G, True), lspec, lspec, wspec, wspec, bspec, cspec],
        out_specs=[cspec, cspec, wspec, bspec, lspec, lspec],
        out_shape=[jax.ShapeDtypeStruct((S, SG_CH), bf16), jax.ShapeDtypeStruct((S, SG_CH), bf16),
                   jax.ShapeDtypeStruct((SG_G, SG_CHUNK, SG_CHUNK), f32), jax.ShapeDtypeStruct((SG_G, SG_CHUNK, 1), f32),
                   jax.ShapeDtypeStruct((1, SG_CH), f32), jax.ShapeDtypeStruct((1, SG_CH), f32)],
        compiler_params=_params(("parallel",), 24 << 20),
    )(sg_in, sg_in, ln_g, ln_b, w_s, w_s_t, b_s, dsz)


def _merge_fwd(gl, b_gate, ya, yc, ys, name):
    D = ya.shape[1]

    def body(gl_ref, b_ref, ya_ref, yc_ref, ys_ref, o_ref):
        acc = jnp.zeros(o_ref.shape, f32)
        for i, y_ref in enumerate((ya_ref, yc_ref, ys_ref)):
            cols = slice(i * D, (i + 1) * D)
            acc = acc + jax.nn.sigmoid(gl_ref[:, cols].astype(f32) + b_ref[:, cols]) * y_ref[...].astype(f32)
        o_ref[...] = acc.astype(o_ref.dtype)

    return _rows(body, [("r", gl), ("f", b_gate), ("r", ya), ("r", yc), ("r", ys)], [("r", D, bf16)],
                 tm=min(128, gl.shape[0]), name=name)[0]


def _merge_bwd(dm, gl, b_gate, ya, yc, ys, name):
    D = ya.shape[1]

    def body(dm_ref, gl_ref, b_ref, ya_ref, yc_ref, ys_ref, dgl_ref, dya_ref, dyc_ref, dys_ref, db_ref):
        dm_ = dm_ref[...].astype(f32)
        for i, (y_ref, dy_ref) in enumerate(((ya_ref, dya_ref), (yc_ref, dyc_ref), (ys_ref, dys_ref))):
            cols = slice(i * D, (i + 1) * D)
            gate = jax.nn.sigmoid(gl_ref[:, cols].astype(f32) + b_ref[:, cols])
            dy_ref[...] = (dm_ * gate).astype(dy_ref.dtype)
            dlog = dm_ * y_ref[...].astype(f32) * gate * (1.0 - gate)
            dgl_ref[:, cols] = dlog.astype(dgl_ref.dtype)
            part = jnp.sum(dlog, axis=0, keepdims=True)
            first = pl.program_id(0) == 0

            @pl.when(first)
            def _():
                db_ref[:, cols] = part

            @pl.when(jnp.logical_not(first))
            def _():
                db_ref[:, cols] += part

    return _rows(body, [("r", dm), ("r", gl), ("f", b_gate), ("r", ya), ("r", yc), ("r", ys)],
                 [("r", 3 * D, bf16), ("r", D, bf16), ("r", D, bf16), ("r", D, bf16), ("a", (1, 3 * D), f32)],
                 tm=min(128, gl.shape[0]), name=name)


def _swiglu_fwd(fg, fu, name):
    def body(g_ref, u_ref, o_ref):
        g = g_ref[...].astype(f32)
        o_ref[...] = (g * jax.nn.sigmoid(g) * u_ref[...].astype(f32)).astype(o_ref.dtype)

    return _rows(body, [("r", fg), ("r", fu)], [("r", fg.shape[1], bf16)], tm=min(128, fg.shape[0]), name=name)[0]


def _swiglu_bwd(dact, fg, fu, name):
    def body(d_ref, g_ref, u_ref, dg_ref, du_ref):
        d = d_ref[...].astype(f32)
        g = g_ref[...].astype(f32)
        sg = jax.nn.sigmoid(g)
        dg_ref[...] = (d * u_ref[...].astype(f32) * sg * (1.0 + g * (1.0 - sg))).astype(dg_ref.dtype)
        du_ref[...] = (d * g * sg).astype(du_ref.dtype)

    return _rows(body, [("r", dact), ("r", fg), ("r", fu)], [("r", fg.shape[1], bf16), ("r", fg.shape[1], bf16)],
                 tm=min(128, fg.shape[0]), name=name)


def _row_tile(r, c, n_arrays, itemsize=4):
    fits = [tm for tm in range(16, r + 1, 16) if r % tm == 0 and 2 * n_arrays * tm * c * itemsize <= (24 << 20)]
    return fits[-1] if fits else r


def _sum_slots(slots, name):
    n, r, c = slots.shape
    tm = _row_tile(r, c, n + 2)

    def body(s_ref, o_ref):
        acc = s_ref[0].astype(f32)
        for k in range(1, n):
            acc = acc + s_ref[k].astype(f32)
        o_ref[...] = acc

    return _call(body, name=name, grid=(r // tm,),
                 in_specs=[pl.BlockSpec((n, tm, c), lambda i: (0, i, 0))],
                 out_specs=pl.BlockSpec((tm, c), lambda i: (i, 0)),
                 out_shape=jax.ShapeDtypeStruct((r, c), f32),
                 compiler_params=_params(("parallel",), 40 << 20))(slots)


def _add_sibling(g4, recv, core, name):
    _, _, r, c = g4.shape
    tm = _row_tile(r, c, 3, 2)

    def body(core_ref, g_ref, r_ref, o_ref):
        o_ref[...] = (g_ref[...].astype(f32) + r_ref[...].astype(f32)).astype(o_ref.dtype)

    grid_spec = pltpu.PrefetchScalarGridSpec(
        num_scalar_prefetch=1, grid=(N_CHIP, r // tm),
        in_specs=[pl.BlockSpec((None, None, tm, c), lambda k, i, core_ref: (k, core_ref[0], i, 0)),
                  pl.BlockSpec((None, tm, c), lambda k, i, core_ref: (k, i, 0))],
        out_specs=pl.BlockSpec((None, tm, c), lambda k, i, core_ref: (k, i, 0)))
    return _call(body, name=name, grid_spec=grid_spec, out_shape=jax.ShapeDtypeStruct((N_CHIP, r, c), bf16),
                 compiler_params=_params(("parallel", "parallel"), 40 << 20))(core, g4, recv)


def _adamw(w, g, m, v, name):
    L, r, c = w.shape
    tm = _row_tile(r, c, 7)
    c1 = 1.0 - ADAM_B1 ** ADAM_STEP
    c2 = 1.0 - ADAM_B2 ** ADAM_STEP

    def body(w_ref, g_ref, m_ref, v_ref, d_ref, mo_ref, vo_ref):
        g_ = g_ref[...]
        m_ = ADAM_B1 * m_ref[...] + (1.0 - ADAM_B1) * g_
        v_ = ADAM_B2 * v_ref[...] + (1.0 - ADAM_B2) * (g_ * g_)
        d_ref[...] = -ADAM_LR * ((m_ / c1) / (jnp.sqrt(v_ / c2) + ADAM_EPS) + ADAM_WD * w_ref[...])
        mo_ref[...] = m_
        vo_ref[...] = v_

    spec = pl.BlockSpec((None, tm, c), lambda l, i: (l, i, 0))
    shp = jax.ShapeDtypeStruct((L, r, c), f32)
    return _call(body, name=name, grid=(L, r // tm), in_specs=[spec] * 4, out_specs=[spec] * 3,
                 out_shape=[shp] * 3, compiler_params=_params(("parallel", "parallel"), 40 << 20))(w, g, m, v)


def _mesh_pos():
    return lax.axis_index("x"), lax.axis_index("y"), lax.axis_index("c")


def _all_gather(shards, after, name):
    n = len(shards)

    def body(*refs):
        x_refs, o_refs = refs[:n], refs[n + 1:2 * n + 1]
        send_sems, recv_sems, local_sems = refs[2 * n + 1:]
        x, y, c = _mesh_pos()
        me, sibling = (x, y, c), (x, y, 1 - c)
        chips = [(1 - x, y), (x, 1 - y), (1 - x, 1 - y)]

        def rows(k, px, py, pc):
            return o_refs[k].at[4 * px + 2 * py + pc]

        def copy(k, s, block, to, src=None):
            return pltpu.make_async_remote_copy(
                src_ref=rows(k, *block) if src is None else src, dst_ref=rows(k, *block),
                send_sem=send_sems.at[k, s], recv_sem=recv_sems.at[k, s], device_id=to, device_id_type=MESH)

        mine = [pltpu.make_async_copy(x_refs[k], rows(k, *me), local_sems.at[k]) for k in range(n)]
        for cp in mine:
            cp.start()
        first = [copy(k, 0, me, sibling, src=x_refs[k]) for k in range(n)]
        for j, chip in enumerate(chips):
            first += [copy(k, 1 + j, me, (*chip, c), src=x_refs[k]) for k in range(n)]
        for cp in first:
            cp.start()
        passed = []
        for j, chip in enumerate(chips):
            for k in range(n):
                copy(k, 1 + j, (*chip, c), me).wait_recv()
                fwd = copy(k, 4 + j, (*chip, c), sibling)
                fwd.start()
                passed.append(fwd)
        for k in range(n):
            copy(k, 0, sibling, me).wait_recv()
        for j, chip in enumerate(chips):
            for k in range(n):
                copy(k, 4 + j, (*chip, 1 - c), me).wait_recv()
        for cp in first + passed:
            cp.wait_send()
        for cp in mine:
            cp.wait()

    return _call(
        body, name=name, in_specs=[HBM] * n + [ANY], out_specs=[HBM] * n,
        out_shape=[jax.ShapeDtypeStruct((N_DEV,) + s.shape, s.dtype) for s in shards],
        scratch_shapes=[pltpu.SemaphoreType.DMA((n, 7)), pltpu.SemaphoreType.DMA((n, 7)), pltpu.SemaphoreType.DMA((n,))],
    )(*shards, after)


def _send_to_sibling(g4s, name):
    n = len(g4s)

    def body(*refs):
        g_refs, o_refs = refs[:n], refs[n:2 * n]
        send_sems, recv_sems = refs[2 * n:]
        x, y, c = _mesh_pos()
        copies = [pltpu.make_async_remote_copy(
            src_ref=g_refs[k].at[:, 1 - c], dst_ref=o_refs[k], send_sem=send_sems.at[k], recv_sem=recv_sems.at[k],
            device_id=(x, y, 1 - c), device_id_type=MESH) for k in range(n)]
        for cp in copies:
            cp.start()
        for cp in copies:
            cp.wait()

    return _call(
        body, name=name, in_specs=[HBM] * n, out_specs=[HBM] * n,
        out_shape=[jax.ShapeDtypeStruct((N_CHIP,) + g.shape[2:], g.dtype) for g in g4s],
        scratch_shapes=[pltpu.SemaphoreType.DMA((n,)), pltpu.SemaphoreType.DMA((n,))],
    )(*g4s)


SEM = pl.BlockSpec(memory_space=pltpu.SEMAPHORE)
ANY = pl.BlockSpec(memory_space=pl.ANY)
EFFECT = pltpu.SideEffectType.DATAFLOW_SIDE_EFFECTING


def _other_chips(x, y):
    return [(1 - x, y), (x, 1 - y), (1 - x, 1 - y)]


def _ici_copy(kind, src_ref, land_ref, send_sem, recv_sem, sender, target, c):
    (sx, sy), (tx, ty) = sender, target
    if kind == "gather":
        src, dst = src_ref, land_ref.at[4 * sx + 2 * sy + c]
    else:
        src, dst = src_ref.at[2 * tx + ty], land_ref.at[2 * sx + sy]
    return pltpu.make_async_remote_copy(src_ref=src, dst_ref=dst, send_sem=send_sem, recv_sem=recv_sem,
                                        device_id=(tx, ty, c), device_id_type=MESH)


def _ici_start(kind, srcs, lands, after, name):
    n = len(srcs)

    def body(*refs):
        src_refs, land_refs = refs[:n], refs[n:2 * n]
        send_sems, recv_sems = refs[2 * n + 1], refs[2 * n + 2]
        token = refs[-1]
        x, y, c = _mesh_pos()
        for j, chip in enumerate(_other_chips(x, y)):
            for k in range(n):
                _ici_copy(kind, src_refs[k], land_refs[k], send_sems.at[3 * k + j], recv_sems.at[3 * k + j],(x, y), chip, c).start()
        token[...] = jnp.zeros_like(token)

    bufs = list(srcs) + list(lands)
    return _call(
        body, name=name,
        out_shape=(pltpu.SemaphoreType.DMA((3 * n,)), pltpu.SemaphoreType.DMA((3 * n,)),
                   *[pltpu.HBM(b.shape, b.dtype) for b in bufs], jax.ShapeDtypeStruct((8, LANES), f32)),
        in_specs=[HBM] * (2 * n) + [ANY], out_specs=(SEM, SEM, *[HBM] * (2 * n), pl.BlockSpec(memory_space=pltpu.VMEM)),
        input_output_aliases={i: 2 + i for i in range(2 * n)},
        compiler_params=pltpu.CompilerParams(has_side_effects=EFFECT),
    )(*[pltpu.with_memory_space_constraint(b, pltpu.HBM) for b in bufs], after)


def _ici_wait(kind, started, after, name):
    send_sems, recv_sems, *bufs = started[:-1]
    n = len(bufs) // 2

    def body(*refs):
        src_refs, land_refs = refs[:n], refs[n:2 * n]
        send_sems, recv_sems = refs[2 * n], refs[2 * n + 1]
        x, y, c = _mesh_pos()
        for j, chip in enumerate(_other_chips(x, y)):
            for k in range(n):
                _ici_copy(kind, src_refs[k], land_refs[k], send_sems.at[3 * k + j], recv_sems.at[3 * k + j],(x, y), chip, c).wait_send()
                _ici_copy(kind, src_refs[k], land_refs[k], send_sems.at[3 * k + j], recv_sems.at[3 * k + j],chip, (x, y), c).wait_recv()

    out = _call(
        body, name=name, out_shape=[pltpu.HBM(b.shape, b.dtype) for b in bufs],
        in_specs=[HBM] * (2 * n) + [SEM, SEM, ANY], out_specs=[HBM] * (2 * n),
        input_output_aliases={i: i for i in range(2 * n)},
        compiler_params=pltpu.CompilerParams(has_side_effects=EFFECT),
    )(*bufs, send_sems, recv_sems, after)
    return out[:n], out[n:]


def _d2d_gather(lands, after, name):
    n = len(lands)

    def body(*refs):
        in_refs, o_refs = refs[:n], refs[n + 1:2 * n + 1]
        send_sems, recv_sems = refs[2 * n + 1:]
        x, y, c = _mesh_pos()
        copies = [pltpu.make_async_remote_copy(
            src_ref=in_refs[k].at[:, c], dst_ref=o_refs[k].at[:, c], send_sem=send_sems.at[k], recv_sem=recv_sems.at[k],
            device_id=(x, y, 1 - c), device_id_type=MESH) for k in range(n)]
        for cp in copies:
            cp.start()
        for k, cp in enumerate(copies):
            cp.wait_send()
            pltpu.make_async_remote_copy(
                src_ref=in_refs[k].at[:, c], dst_ref=o_refs[k].at[:, 1 - c], send_sem=send_sems.at[k],
                recv_sem=recv_sems.at[k], device_id=(x, y, 1 - c), device_id_type=MESH).wait_recv()

    return _call(
        body, name=name, in_specs=[HBM] * n + [ANY], out_specs=[HBM] * n,
        out_shape=[jax.ShapeDtypeStruct(b.shape, b.dtype) for b in lands],
        input_output_aliases={k: k for k in range(n)},
        scratch_shapes=[pltpu.SemaphoreType.DMA((n,)), pltpu.SemaphoreType.DMA((n,))],
    )(*lands, after)


def _sum_chip_slots(lands, sums, chip, name):
    _, r, c = lands.shape
    tm = _row_tile(r, c, 10, 2)

    def body(chip_ref, l_ref, s_ref, o_ref):
        acc = None
        for k in range(N_CHIP):
            part = jnp.where(chip_ref[0] == k, s_ref[k], l_ref[k]).astype(f32)
            acc = part if acc is None else acc + part
        o_ref[...] = acc

    grid_spec = pltpu.PrefetchScalarGridSpec(
        num_scalar_prefetch=1, grid=(r // tm,),
        in_specs=[pl.BlockSpec((N_CHIP, tm, c), lambda i, chip_ref: (0, i, 0)),
                  pl.BlockSpec((N_CHIP, tm, c), lambda i, chip_ref: (0, i, 0))],
        out_specs=pl.BlockSpec((tm, c), lambda i, chip_ref: (i, 0)))
    return _call(body, name=name, grid_spec=grid_spec, out_shape=jax.ShapeDtypeStruct((r, c), f32),
                 compiler_params=_params(("parallel",), 40 << 20))(chip, lands, sums)


def _reduce_start(grads, core, after, tag):
    g4s = [g.reshape(N_CHIP, 2, g.shape[0] // N_DEV, g.shape[1]) for g in grads]
    recv = _send_to_sibling(g4s, name="rs_sibling_" + tag)
    sums = [_add_sibling(g4, rv, core, name="rs_add_" + tag) for g4, rv in zip(g4s, recv)]
    lands = [lax.empty(s.shape, s.dtype) for s in sums]
    return _ici_start("reduce", sums, lands, after, name="rs_start_" + tag)


def _reduce_finish(started, chip, after, tag):
    sums, lands = _ici_wait("reduce", started, after, name="rs_wait_" + tag)
    return [_sum_chip_slots(ld, s, chip, name="rs_sum_" + tag) for ld, s in zip(lands, sums)]


def _rope_tables(S):
    rows = S // GRID_W
    row = jnp.repeat(jnp.arange(rows, dtype=f32), GRID_W)
    col = jnp.tile(jnp.arange(GRID_W, dtype=f32), rows)
    nf = HEAD_DIM // 4
    inv = ROPE_THETA ** (-jnp.arange(nf, dtype=f32) / nf)
    ang = jnp.concatenate([row[:, None] * inv, col[:, None] * inv], axis=-1)
    cos, sin = jnp.cos(ang), jnp.sin(ang)
    return jnp.concatenate([cos, cos], axis=-1), jnp.concatenate([-sin, sin], axis=-1)


def _layer_fwd(xin, p, w, rest_of_weights, cos2, sin2):
    sv = {"xin": xin}
    h = sv["h"] = _rms_fwd(xin, p["g_mix"], name="rms_mix")
    proj = functools.partial(_mm, h, w["in"], "nt", bf16)
    q_raw = sv["q_raw"] = proj(n=Q_COLS, b_off=0, name="proj_q")
    kv_raw = sv["kv_raw"] = proj(n=2 * KV_COLS, b_off=OFF_KV, name="proj_kv")
    conv_in = sv["conv_in"] = proj(n=2 * CONV_CH, b_off=OFF_CONV, name="proj_conv")
    sg_in = sv["sg_in"] = proj(n=2 * SG_CH, b_off=OFF_SG, name="proj_sg")
    gl = sv["gl"] = proj(n=3 * D_MODEL, b_off=OFF_GATE, name="proj_gate")
    qr, kr = sv["qr"], sv["kr"] = _qk_fwd(q_raw, kv_raw, p["q_norm_g"], p["k_norm_g"], cos2, sin2, name="qk_fwd")
    o = sv["o"] = _attn_fwd(qr, kr, kv_raw, name="attn_fwd")
    c = sv["c"] = _conv1_fwd(conv_in, w["dw"], p["b_dw"], name="conv1_fwd")
    cz = sv["cz"] = _conv2_fwd(c, p["conv_ln_g"], p["conv_ln_b"], name="conv2_fwd")
    sz = sv["sz"] = _sgu_fwd(sg_in, p["sg_ln_g"], p["sg_ln_b"], p["w_s"], p["b_s"], name="sgu_fwd")
    w = rest_of_weights(sz)
    ya = sv["ya"] = _mm(o, w["attn_o"], "nt", bf16, name="out_attn")
    yc = sv["yc"] = _mm(cz, w["conv_o"], "nt", bf16, name="out_conv")
    ys = sv["ys"] = _mm(sz, w["sg_o"], "nt", bf16, name="out_sg")
    merged = sv["merged"] = _merge_fwd(gl, p["b_gate"], ya, yc, ys, name="merge_fwd")
    x1 = sv["x1"] = _mm(merged, w["out"], "nn", f32, res=xin, name="out_proj")
    hf = sv["hf"] = _rms_fwd(x1, p["g_ffn"], name="rms_ffn")
    fg = sv["fg"] = _mm(hf, w["ff_gate"], "nt", bf16, name="ff_gate")
    fu = sv["fu"] = _mm(hf, w["ff_up"], "nt", bf16, name="ff_up")
    act = sv["act"] = _swiglu_fwd(fg, fu, name="swiglu_fwd")
    x2 = _mm(act, w["ff_down"], "nn", f32, res=x1, name="ff_down")
    return x2, sv, w


def _layer_bwd(dx2, dx2b, sv, p, w, cos2, sin2, reduce_start):
    small = {}
    dact = _mm(dx2b, w["ff_down"], "nt", bf16, name="d_act")
    g_down = _mm(sv["act"], dx2b, "tn", bf16, name="g_ff_down")
    dfg, dfu = _swiglu_bwd(dact, sv["fg"], sv["fu"], name="swiglu_bwd")
    dhf = _mm(dfg, w["ff_gate"], "nn", f32, name="d_hf_gate")
    dhf = _mm(dfu, w["ff_up"], "nn", f32, res=dhf, name="d_hf_up")
    g_gate = _mm(dfg, sv["hf"], "tn", bf16, name="g_ff_gate")
    g_up = _mm(dfu, sv["hf"], "tn", bf16, name="g_ff_up")
    zero = reduce_start("ffn", dict(w_ff_gate=g_gate, w_ff_up=g_up, w_ff_down=g_down))[0, 0]
    dx1, dx1b, small["g_ffn"] = _rms_bwd(sv["x1"], p["g_ffn"] + zero, dhf, dx2, name="rms_ffn_bwd")
    dmerged = _mm(dx1b, w["out"], "nt", bf16, name="d_merged")
    g_out = _mm(sv["merged"], dx1b, "tn", bf16, name="g_out")
    dgl, dya, dyc, dys, small["b_gate"] = _merge_bwd(dmerged, sv["gl"], p["b_gate"], sv["ya"], sv["yc"], sv["ys"],
                                                    name="merge_bwd")
    do = _mm(dya, w["attn_o"], "nn", bf16, name="d_o")
    g_ao = _mm(dya, sv["o"], "tn", bf16, name="g_attn_o")
    dcz = _mm(dyc, w["conv_o"], "nn", bf16, name="d_cz")
    g_co = _mm(dyc, sv["cz"], "tn", bf16, name="g_conv_o")
    dsz = _mm(dys, w["sg_o"], "nn", bf16, name="d_sz")
    g_so = _mm(dys, sv["sz"], "tn", bf16, name="g_sg_o")
    zero = reduce_start("mix", dict(w_attn_o=g_ao, w_conv_o=g_co, w_sg_o=g_so, w_out=g_out))[0, 0]
    dsu, dsv, small["w_s"], small["b_s"], small["sg_ln_g"], small["sg_ln_b"] = _sgu_bwd(
        sv["sg_in"], dsz, p["sg_ln_g"] + zero, p["sg_ln_b"], p["w_s"], p["w_s_t"], p["b_s"], name="sgu_bwd")
    dc, small["conv_ln_g"], small["conv_ln_b"] = _conv2_bwd(sv["c"], dcz, p["conv_ln_g"], p["conv_ln_b"], name="conv2_bwd")
    da, dgt, small["w_dw"], small["b_dw"] = _conv1_bwd(sv["conv_in"], dc, w["dw"], name="conv1_bwd")
    dqr, dkr, dv = _attn_bwd(sv["qr"], sv["kr"], sv["kv_raw"], do, name="attn_bwd")
    dq_raw, dk_raw, small["q_norm_g"], small["k_norm_g"] = _qk_bwd(
        sv["q_raw"], sv["kv_raw"], dqr, dkr, p["q_norm_g"], p["k_norm_g"], cos2, sin2, name="qk_bwd")
    dproj = jnp.concatenate([dq_raw, dk_raw, dv.astype(bf16), da, dgt, dsu, dsv, dgl], axis=1)
    g_in = _mm(dproj, sv["h"], "tn", bf16, name="g_in")
    started = reduce_start("in", dict(w_in=g_in))
    dh = _mm(dproj, w["in"], "nn", f32, after=started, name="d_h")
    dx, dxb, small["g_mix"] = _rms_bwd(sv["xin"], p["g_mix"], dh, dx1, name="rms_mix_bwd")
    return dx, dxb, small


SMALL = ("g_mix", "b_gate", "q_norm_g", "k_norm_g", "b_dw", "conv_ln_g", "conv_ln_b", "sg_ln_g", "sg_ln_b",
         "w_s", "b_s", "g_ffn")
PACK_ALIGN = 8 * LANES


def _pack(parts):
    flat = jnp.concatenate([a.reshape(-1).astype(f32) for a in parts])
    pad = -flat.shape[0] % PACK_ALIGN
    return jnp.pad(flat, (0, pad)).reshape(-1, LANES)


def _unpack(buf, shapes):
    flat = buf.reshape(-1)
    out, pos = [], 0
    for shp in shapes:
        size = math.prod(shp)
        out.append(flat[pos:pos + size].reshape(shp))
        pos += size
    return out


def kernel(x, g_mix, w_in, b_gate, q_norm_g, k_norm_g, w_attn_o, w_dw, b_dw, conv_ln_g, conv_ln_b, w_conv_o, sg_ln_g, sg_ln_b, w_s, b_s, w_sg_o, w_out, g_ffn, w_ff_gate, w_ff_up, w_ff_down, g_final, loss_target, m_g_mix, m_w_in, m_b_gate, m_q_norm_g, m_k_norm_g, m_w_attn_o, m_w_dw, m_b_dw, m_conv_ln_g, m_conv_ln_b, m_w_conv_o, m_sg_ln_g, m_sg_ln_b, m_w_s, m_b_s, m_w_sg_o, m_w_out, m_g_ffn, m_w_ff_gate, m_w_ff_up, m_w_ff_down, m_g_final, v_g_mix, v_w_in, v_b_gate, v_q_norm_g, v_k_norm_g, v_w_attn_o, v_w_dw, v_b_dw, v_conv_ln_g, v_conv_ln_b, v_w_conv_o, v_sg_ln_g, v_sg_ln_b, v_w_s, v_b_s, v_w_sg_o, v_w_out, v_g_ffn, v_w_ff_gate, v_w_ff_up, v_w_ff_down, v_g_final):
    weights = dict(g_mix=g_mix, w_in=w_in, b_gate=b_gate, q_norm_g=q_norm_g, k_norm_g=k_norm_g, w_attn_o=w_attn_o,
                   w_dw=w_dw, b_dw=b_dw, conv_ln_g=conv_ln_g, conv_ln_b=conv_ln_b, w_conv_o=w_conv_o, sg_ln_g=sg_ln_g,
                   sg_ln_b=sg_ln_b, w_s=w_s, b_s=b_s, w_sg_o=w_sg_o, w_out=w_out, g_ffn=g_ffn, w_ff_gate=w_ff_gate,
                   w_ff_up=w_ff_up, w_ff_down=w_ff_down, g_final=g_final)
    mom_m = dict(g_mix=m_g_mix, w_in=m_w_in, b_gate=m_b_gate, q_norm_g=m_q_norm_g, k_norm_g=m_k_norm_g,
                 w_attn_o=m_w_attn_o, w_dw=m_w_dw, b_dw=m_b_dw, conv_ln_g=m_conv_ln_g, conv_ln_b=m_conv_ln_b,
                 w_conv_o=m_w_conv_o, sg_ln_g=m_sg_ln_g, sg_ln_b=m_sg_ln_b, w_s=m_w_s, b_s=m_b_s, w_sg_o=m_w_sg_o,
                 w_out=m_w_out, g_ffn=m_g_ffn, w_ff_gate=m_w_ff_gate, w_ff_up=m_w_ff_up, w_ff_down=m_w_ff_down,
                 g_final=m_g_final)
    mom_v = dict(g_mix=v_g_mix, w_in=v_w_in, b_gate=v_b_gate, q_norm_g=v_q_norm_g, k_norm_g=v_k_norm_g,
                 w_attn_o=v_w_attn_o, w_dw=v_w_dw, b_dw=v_b_dw, conv_ln_g=v_conv_ln_g, conv_ln_b=v_conv_ln_b,
                 w_conv_o=v_w_conv_o, sg_ln_g=v_sg_ln_g, sg_ln_b=v_sg_ln_b, w_s=v_w_s, b_s=v_b_s, w_sg_o=v_w_sg_o,
                 w_out=v_w_out, g_ffn=v_g_ffn, w_ff_gate=v_w_ff_gate, w_ff_up=v_w_ff_up, w_ff_down=v_w_ff_down,
                 g_final=v_g_final)
    S, D = x.shape[1], x.shape[2]
    xi, yi, ci = _mesh_pos()
    me = 4 * xi + 2 * yi + ci
    core = jnp.reshape(ci, (1,)).astype(jnp.int32)
    cos2, sin2 = _rope_tables(S)

    big = ("w_in", "w_attn_o", "w_conv_o", "w_sg_o", "w_out", "w_ff_gate", "w_ff_up", "w_ff_down")
    transposed = {"w_in", "w_attn_o", "w_conv_o", "w_sg_o", "w_ff_gate", "w_ff_up"}
    chip = jnp.reshape(2 * xi + yi, (1,)).astype(jnp.int32)
    groups = (("in", "dw"), ("attn_o", "conv_o", "sg_o", "out", "ff_gate", "ff_up", "ff_down"))
    P, shards = [], []
    for l in range(DEPTH):
        sh = {n[2:]: (weights[n][l].T if n in transposed else weights[n][l]).astype(bf16) for n in big}
        sh["dw"] = jnp.pad(w_dw[l].reshape(CONV_W, LANES), ((0, CONV_WP - CONV_W), (0, 0)))
        shards.append(sh)
        p = {n: weights[n][l].reshape(1, -1) for n in SMALL if n not in ("w_s", "b_s")}
        p["w_s"] = w_s[l]
        p["w_s_t"] = jnp.swapaxes(w_s[l], 1, 2)
        p["b_s"] = b_s[l].reshape(SG_G, SG_CHUNK, 1)
        P.append(p)

    gathers = {}

    def start_gather(l, gi, after):
        srcs = [shards[l][n] for n in groups[gi]]
        lands = [lax.dynamic_update_index_in_dim(lax.empty((N_DEV,) + s.shape, s.dtype), s, me, 0) for s in srcs]
        gathers[l, gi] = _ici_start("gather", srcs, lands, after, name=f"ag_start_{l}{gi}")
        return gathers[l, gi][-1]

    def gathered(l, gi, after):
        srcs, lands = _ici_wait("gather", gathers[l, gi], after, name=f"ag_wait_{l}{gi}")
        after = srcs[0]
        if gi == len(groups) - 1 and l + 1 < DEPTH:
            for gj in range(len(groups)):
                after = start_gather(l + 1, gj, after)
        full = _d2d_gather([b.reshape(N_CHIP, 2, *b.shape[1:]) for b in lands], after, name=f"ag_d2d_{gi}")
        return {n: f.reshape(-1, f.shape[3]) for n, f in zip(groups[gi], full)}

    all_started = start_gather(0, 1, start_gather(0, 0, cos2))

    h = x.reshape(S, D)
    saved, W = [], []
    for l in range(DEPTH):
        first = gathered(l, 0, all_started if l == 0 else h)
        if l == 0:
            P[l]["g_mix"] = P[l]["g_mix"] + all_started[0, 0]
        h, sv, w = _layer_fwd(h, P[l], first, lambda z, l=l, first=first: {**first, **gathered(l, 1, z)}, cos2, sin2)
        saved.append(sv)
        W.append(w)
    dx, dxb, sq, g_final_part = _final_loss(h, g_final.reshape(1, D), loss_target.reshape(S, D), name="final_loss")
    loss = lax.psum(0.5 * jnp.sum(sq) / D, ("x", "y", "c"))

    reductions, small_grads = {}, [None] * DEPTH
    for l in reversed(range(DEPTH)):
        def reduce_start(group, grads, l=l):
            names = tuple(grads)
            started = _reduce_start([grads[n] for n in names], core, grads[names[0]], tag=f"{group}{l}")
            reductions[l, group] = (names, started)
            return started[-1]

        dx, dxb, small_grads[l] = _layer_bwd(dx, dxb, saved[l], P[l], W[l], cos2, sin2, reduce_start)
    grad_x = dx.reshape(x.shape)

    grads_out, delta, new_m, new_v = {}, {}, {}, {}
    swap = lambda a: jnp.swapaxes(a, 1, 2)

    def update(n, per_layer):
        g = jnp.stack(per_layer)
        if n in transposed and weights[n].shape[2] % LANES:
            upd = _adamw(swap(weights[n]), g, swap(mom_m[n]), swap(mom_v[n]), name="adamw_" + n)
            grads_out[n], (delta[n], new_m[n], new_v[n]) = swap(g), [swap(u) for u in upd]
        else:
            grads_out[n] = swap(g) if n in transposed else g
            delta[n], new_m[n], new_v[n] = _adamw(weights[n], grads_out[n], mom_m[n], mom_v[n], name="adamw_" + n)
        return delta[n]

    after = dx
    for group in ("ffn", "mix", "in"):
        done = [dict(zip(reductions[l, group][0], _reduce_finish(reductions[l, group][1], chip, after, tag=f"{group}{l}")))
                for l in range(DEPTH)]
        for n in reductions[0, group][0]:
            after = update(n, [done[l][n] for l in range(DEPTH)])

    small_shapes = [weights[n].shape for n in SMALL] + [g_final.shape, (DEPTH, CONV_CH // LANES, CONV_WP, LANES)]
    parts = [jnp.stack([small_grads[l][n].reshape(weights[n].shape[1:]) for l in range(DEPTH)]) for n in SMALL]
    parts += [g_final_part.reshape(g_final.shape), jnp.stack([small_grads[l]["w_dw"] for l in range(DEPTH)])]
    packed = _pack(parts)
    gathered = _all_gather([packed], after, name="gather_small")[0]
    total = _sum_slots(gathered, name="sum_small")
    small_total = _unpack(total, small_shapes)
    grads_out.update(zip(SMALL + ("g_final",), small_total[:-1]))
    dw_full = small_total[-1]
    grads_out["w_dw"] = lax.dynamic_index_in_dim(dw_full, me, axis=1, keepdims=False)[:, :CONV_W].reshape(w_dw.shape)

    rep = SMALL + ("g_final",)
    rep_shapes = [weights[n].shape for n in rep]
    packs = [_pack([src[n] for n in rep])[None] for src in (weights, grads_out, mom_m, mom_v)]
    for dst, buf in zip((delta, new_m, new_v), _adamw(*packs, name="adamw_small")):
        dst.update(zip(rep, _unpack(buf[0], rep_shapes)))
    flat = lambda a: a.reshape(1, DEPTH * CONV_W, LANES)
    for dst, buf in zip((delta, new_m, new_v),
                        _adamw(flat(w_dw), flat(grads_out["w_dw"]), flat(m_w_dw), flat(v_w_dw), name="adamw_w_dw")):
        dst["w_dw"] = buf.reshape(w_dw.shape)

    order = ("g_mix", "w_in", "b_gate", "q_norm_g", "k_norm_g", "w_attn_o", "w_dw", "b_dw", "conv_ln_g", "conv_ln_b",
             "w_conv_o", "sg_ln_g", "sg_ln_b", "w_s", "b_s", "w_sg_o", "w_out", "g_ffn", "w_ff_gate", "w_ff_up",
             "w_ff_down", "g_final")
    return (loss, grad_x, *[grads_out[n] for n in order], *[delta[n] for n in order],
            *[new_m[n] for n in order], *[new_v[n] for n in order])
```

```python
import functools
import math

import jax
import jax.numpy as jnp
from jax import lax
from jax.experimental import pallas as pl
from jax.experimental.pallas import tpu as pltpu

f32, bf16 = jnp.float32, jnp.bfloat16

D_MODEL = 2048
SEQ = 2048
DEPTH = 2
GRID_W = 64
HEAD_DIM = 128
LANES = 128
N_Q = (D_MODEL // 2) // HEAD_DIM
N_KV = N_Q // 4
GRP = N_Q // N_KV
Q_COLS = N_Q * HEAD_DIM
KV_COLS = N_KV * HEAD_DIM
CONV_CH = D_MODEL // 2
CONV_W = 31
CONV_PAD = CONV_W // 2
CONV_WP = 32
SG_CH = D_MODEL // 2
SG_G = SG_CH // LANES
SG_CHUNK = 128
D_FF = -(-8 * D_MODEL // (3 * 256)) * 256
OFF_KV = Q_COLS
OFF_CONV = OFF_KV + 2 * KV_COLS
OFF_SG = OFF_CONV + 2 * CONV_CH
OFF_GATE = OFF_SG + 2 * SG_CH
IN_COLS = OFF_GATE + 3 * D_MODEL
ROPE_THETA = 10000.0
SCALE = HEAD_DIM ** -0.5
N_DEV = 8
N_CHIP = 4

ADAM_LR, ADAM_B1, ADAM_B2, ADAM_EPS, ADAM_WD, ADAM_STEP = 0.001, 0.9, 0.999, 1e-08, 0.01, 10

VMEM_BYTES_V7X = 64 << 20
VMEM_CAP = VMEM_BYTES_V7X - (6 << 20)
MESH = pl.DeviceIdType.MESH
HBM = pl.BlockSpec(memory_space=pltpu.HBM)


def _in_hbm(a):
    if isinstance(a, jax.Array) and jnp.issubdtype(a.dtype, jnp.floating) and a.size * a.dtype.itemsize >= (1 << 20):
        return pltpu.with_memory_space_constraint(a, pltpu.HBM)
    return a


def _out_hbm(s):
    if isinstance(s, jax.ShapeDtypeStruct) and math.prod(s.shape) * jnp.dtype(s.dtype).itemsize >= (1 << 20):
        return pltpu.HBM(s.shape, s.dtype)
    return s


def _call(body, **kw):
    shapes = kw.pop("out_shape")
    shapes = type(shapes)(_out_hbm(s) for s in shapes) if isinstance(shapes, (list, tuple)) else _out_hbm(shapes)
    call = pl.pallas_call(body, out_shape=shapes, **kw)
    return lambda *args: call(*[_in_hbm(a) for a in args])


def _pick(n, cands):
    for c in cands:
        if n % c == 0:
            return c
    raise ValueError((n, cands))


def _params(sem, vmem_bytes):
    return pltpu.CompilerParams(dimension_semantics=sem, vmem_limit_bytes=int(min(max(vmem_bytes, 16 << 20), VMEM_CAP)))


def _mm(a, b, form, out_dtype, *, n=None, b_off=0, res=None, after=None, name):
    if form == "tn":
        K, M = a.shape
    else:
        M, K = a.shape
    N = n if n is not None else (b.shape[0] if form == "nt" else b.shape[1])
    if K <= 2048:
        tk = K
        if form == "tn":
            tm = _pick(M, (512, 256, 128))
            tn = N if N <= 2048 else _pick(N, (1024, 512, 256, 128))
        else:
            tm = M if M <= 2048 else _pick(M, (2048, 1024, 512))
            tn = _pick(math.gcd(N, b_off) if b_off else N, (256, 128) if res is not None else (512, 256, 128))
    else:
        tk = max(t for t in range(LANES, 3072 + 1, LANES) if K % t == 0)
        tm = _pick(M, (1024, 512, 256, 128))
        tn = _pick(math.gcd(N, b_off) if b_off else N, (1024, 512, 256, 128))
    assert b_off % tn == 0
    off = b_off // tn
    nk = K // tk
    if form == "tn":
        a_spec = pl.BlockSpec((tk, tm), lambda i, j, k: (k, i))
    else:
        a_spec = pl.BlockSpec((tm, tk), lambda i, j, k: (i, k))
    if form == "nt":
        b_spec = pl.BlockSpec((tn, tk), lambda i, j, k: (j + off, k))
    else:
        b_spec = pl.BlockSpec((tk, tn), lambda i, j, k: (k, j + off))
    dims = {"nn": ((1,), (0,)), "nt": ((1,), (1,)), "tn": ((0,), (0,))}[form]
    has_res = res is not None

    def body(*refs):
        if after is not None:
            refs = refs[1:]
        if has_res:
            a_ref, b_ref, r_ref, o_ref = refs[:4]
        else:
            a_ref, b_ref, o_ref = refs[:3]
        p = lax.dot_general(a_ref[...], b_ref[...], (dims, ((), ())), preferred_element_type=f32)

        def finish(acc):
            if has_res:
                acc = acc + r_ref[...].astype(f32)
            o_ref[...] = acc.astype(o_ref.dtype)

        if nk == 1:
            finish(p)
        else:
            acc_ref = refs[-1]
            k = pl.program_id(2)

            @pl.when(k == 0)
            def _():
                acc_ref[...] = p

            @pl.when(k > 0)
            def _():
                acc_ref[...] += p

            @pl.when(k == nk - 1)
            def _():
                finish(acc_ref[...])

    in_specs = [a_spec, b_spec]
    args = [a, b]
    osz = jnp.dtype(out_dtype).itemsize
    vmem = 2 * (tm * tk * 2 + tk * tn * 2 + tm * tn * osz) + 2 * tm * tn * 4
    if has_res:
        in_specs.append(pl.BlockSpec((tm, tn), lambda i, j, k: (i, j)))
        args.append(res)
        vmem += 2 * tm * tn * res.dtype.itemsize
    scratch = []
    if nk > 1:
        scratch.append(pltpu.VMEM((tm, tn), f32))
        vmem += tm * tn * 4
    if after is not None:
        in_specs.insert(0, pl.BlockSpec(memory_space=pl.ANY))
        args.insert(0, after)
    return _call(
        body, name=name, grid=(M // tm, N // tn, nk),
        in_specs=in_specs, out_specs=pl.BlockSpec((tm, tn), lambda i, j, k: (i, j)),
        out_shape=jax.ShapeDtypeStruct((M, N), out_dtype), scratch_shapes=scratch,
        compiler_params=_params(("parallel", "parallel", "arbitrary"), vmem + (8 << 20)),
    )(*args)


def _rows(body, ins, outs, *, tm, name, vmem=40 << 20):
    nrows = next(s[1].shape[0] for s in ins if s[0] == "r")
    in_specs, args = [], []
    for s in ins:
        arr = s[1]
        if s[0] == "r":
            w = s[2] if len(s) > 2 else arr.shape[1]
            cb = s[3] if len(s) > 3 else 0
            in_specs.append(pl.BlockSpec((tm, w), functools.partial(lambda i, cb: (i, cb), cb=cb)))
        else:
            in_specs.append(pl.BlockSpec(arr.shape, functools.partial(lambda i, nd: (0,) * nd, nd=arr.ndim)))
        args.append(arr)
    out_specs, out_shape = [], []
    for s in outs:
        if s[0] == "r":
            out_specs.append(pl.BlockSpec((tm, s[1]), lambda i: (i, 0)))
            out_shape.append(jax.ShapeDtypeStruct((nrows, s[1]), s[2]))
        else:
            out_specs.append(pl.BlockSpec(s[1], functools.partial(lambda i, nd: (0,) * nd, nd=len(s[1]))))
            out_shape.append(jax.ShapeDtypeStruct(s[1], s[2]))
    return _call(body, name=name, grid=(nrows // tm,), in_specs=in_specs, out_specs=out_specs,
                 out_shape=out_shape, compiler_params=_params(("arbitrary",), vmem))(*args)


def _accumulate(ref, part):
    i = pl.program_id(0)

    @pl.when(i == 0)
    def _():
        ref[...] = part

    @pl.when(i > 0)
    def _():
        ref[...] += part


def _rms_stats(x):
    r = lax.rsqrt(jnp.mean(x * x, axis=-1, keepdims=True) + 1e-6)
    return r, x * r


def _rms_fwd(x, g, name):
    def body(x_ref, g_ref, o_ref):
        _, xn = _rms_stats(x_ref[...])
        o_ref[...] = (xn * g_ref[...]).astype(o_ref.dtype)

    return _rows(body, [("r", x), ("f", g)], [("r", x.shape[1], bf16)], tm=min(256, x.shape[0]), name=name)[0]


def _rms_bwd(x, g, dh, dres, name):
    D = x.shape[1]

    def body(x_ref, g_ref, dh_ref, dr_ref, dx_ref, dxb_ref, dg_ref):
        r, xn = _rms_stats(x_ref[...])
        dy = dh_ref[...].astype(f32)
        dxn = dy * g_ref[...]
        dx = dr_ref[...] + r * (dxn - xn * jnp.mean(dxn * xn, axis=-1, keepdims=True))
        dx_ref[...] = dx
        dxb_ref[...] = dx.astype(bf16)
        _accumulate(dg_ref, jnp.sum(dy * xn, axis=0, keepdims=True))

    return _rows(body, [("r", x), ("f", g), ("r", dh), ("r", dres)],
                 [("r", D, f32), ("r", D, bf16), ("a", (1, D), f32)], tm=min(256, x.shape[0]), name=name)


def _final_loss(x, g, tgt, name):
    D = x.shape[1]

    def body(x_ref, g_ref, t_ref, dx_ref, dxb_ref, sq_ref, dg_ref):
        r, xn = _rms_stats(x_ref[...])
        gain = g_ref[...]
        diff = xn * gain - t_ref[...]
        dy = diff * (1.0 / D)
        dxn = dy * gain
        dx = r * (dxn - xn * jnp.mean(dxn * xn, axis=-1, keepdims=True))
        dx_ref[...] = dx
        dxb_ref[...] = dx.astype(bf16)
        _accumulate(sq_ref, jnp.sum(diff * diff, axis=0, keepdims=True))
        _accumulate(dg_ref, jnp.sum(dy * xn, axis=0, keepdims=True))

    return _rows(body, [("r", x), ("f", g), ("r", tgt)],
                 [("r", D, f32), ("r", D, bf16), ("a", (1, D), f32), ("a", (1, D), f32)],
                 tm=min(256, x.shape[0]), name=name)


def _qk_fwd(q_raw, kv_raw, qg, kg, cos2, sin2, name):
    def body(q_ref, k_ref, qg_ref, kg_ref, c_ref, s_ref, qo_ref, ko_ref):
        c, s = c_ref[...], s_ref[...]

        def head(src, gain, dst, h):
            cols = slice(h * HEAD_DIM, (h + 1) * HEAD_DIM)
            _, xn = _rms_stats(src[:, cols].astype(f32))
            y = xn * gain
            dst[:, cols] = (y * c + pltpu.roll(y, HEAD_DIM // 2, 1) * s).astype(dst.dtype)

        for h in range(N_Q):
            head(q_ref, qg_ref[...], qo_ref, h)
        for h in range(N_KV):
            head(k_ref, kg_ref[...], ko_ref, h)

    return _rows(body, [("r", q_raw), ("r", kv_raw, KV_COLS, 0), ("f", qg), ("f", kg), ("r", cos2), ("r", sin2)],
                 [("r", Q_COLS, bf16), ("r", KV_COLS, bf16)], tm=min(256, q_raw.shape[0]), name=name)


def _qk_bwd(q_raw, kv_raw, dqr, dkr, qg, kg, cos2, sin2, name):
    def body(q_ref, k_ref, dq_ref, dk_ref, qg_ref, kg_ref, c_ref, s_ref, dqo_ref, dko_ref, dqg_ref, dkg_ref):
        c, s = c_ref[...], s_ref[...]

        def head(src, dsrc, gain, dst, h):
            cols = slice(h * HEAD_DIM, (h + 1) * HEAD_DIM)
            r, xn = _rms_stats(src[:, cols].astype(f32))
            do = dsrc[:, cols].astype(f32)
            dy = do * c + pltpu.roll(do * s, HEAD_DIM // 2, 1)
            dxn = dy * gain
            dst[:, cols] = (r * (dxn - xn * jnp.mean(dxn * xn, axis=-1, keepdims=True))).astype(dst.dtype)
            return jnp.sum(dy * xn, axis=0, keepdims=True)

        dq_gain = head(q_ref, dq_ref, qg_ref[...], dqo_ref, 0)
        for h in range(1, N_Q):
            dq_gain = dq_gain + head(q_ref, dq_ref, qg_ref[...], dqo_ref, h)
        dk_gain = head(k_ref, dk_ref, kg_ref[...], dko_ref, 0)
        for h in range(1, N_KV):
            dk_gain = dk_gain + head(k_ref, dk_ref, kg_ref[...], dko_ref, h)
        _accumulate(dqg_ref, dq_gain)
        _accumulate(dkg_ref, dk_gain)

    return _rows(body, [("r", q_raw), ("r", kv_raw, KV_COLS, 0), ("r", dqr), ("r", dkr), ("f", qg), ("f", kg),
                        ("r", cos2), ("r", sin2)],
                 [("r", Q_COLS, bf16), ("r", KV_COLS, bf16), ("a", (1, HEAD_DIM), f32), ("a", (1, HEAD_DIM), f32)],
                 tm=min(256, q_raw.shape[0]), name=name)


def _softmax_rows(q, k):
    s = lax.dot_general(q, k, (((1,), (1,)), ((), ())), preferred_element_type=f32) * SCALE
    p = jnp.exp(s - jnp.max(s, axis=-1, keepdims=True))
    return p * (1.0 / jnp.sum(p, axis=-1, keepdims=True))


def _head_cols(g):
    return slice(g * HEAD_DIM, (g + 1) * HEAD_DIM)


def _attn_fwd(qr, kr, kv_raw, name):
    S = qr.shape[0]
    tq = min(256, S)

    def body(q_ref, k_ref, v_ref, o_ref):
        k, v = k_ref[...], v_ref[...]
        for g in range(GRP):
            p = _softmax_rows(q_ref[:, _head_cols(g)], k)
            o_ref[:, _head_cols(g)] = jnp.dot(p.astype(bf16), v, preferred_element_type=f32).astype(o_ref.dtype)

    return _call(
        body, name=name, grid=(N_KV, S // tq),
        in_specs=[pl.BlockSpec((tq, GRP * HEAD_DIM), lambda kv, i: (i, kv)),
                  pl.BlockSpec((S, HEAD_DIM), lambda kv, i: (0, kv)),
                  pl.BlockSpec((S, HEAD_DIM), lambda kv, i: (0, N_KV + kv))],
        out_specs=pl.BlockSpec((tq, GRP * HEAD_DIM), lambda kv, i: (i, kv)),
        out_shape=jax.ShapeDtypeStruct((S, Q_COLS), bf16),
        compiler_params=_params(("parallel", "arbitrary"), 4 * GRP * tq * S * 4 + (8 << 20)),
    )(qr, kr, kv_raw)


def _attn_bwd(qr, kr, kv_raw, do, name):
    S = qr.shape[0]
    tq = min(256, S)

    def body(q_ref, k_ref, v_ref, do_ref, dq_ref, dk_ref, dv_ref):
        first = pl.program_id(1) == 0
        k, v = k_ref[...], v_ref[...]
        dv_part = dk_part = None
        for g in range(GRP):
            q, do_ = q_ref[:, _head_cols(g)], do_ref[:, _head_cols(g)]
            p = _softmax_rows(q, k)
            dp = lax.dot_general(do_, v, (((1,), (1,)), ((), ())), preferred_element_type=f32)
            ds = (p * (dp - jnp.sum(dp * p, axis=-1, keepdims=True)) * SCALE).astype(bf16)
            dq_ref[:, _head_cols(g)] = jnp.dot(ds, k, preferred_element_type=f32).astype(dq_ref.dtype)
            dv_g = lax.dot_general(p.astype(bf16), do_, (((0,), (0,)), ((), ())), preferred_element_type=f32)
            dk_g = lax.dot_general(ds, q, (((0,), (0,)), ((), ())), preferred_element_type=f32)
            dv_part = dv_g if g == 0 else dv_part + dv_g
            dk_part = dk_g if g == 0 else dk_part + dk_g

        @pl.when(first)
        def _():
            dv_ref[...] = dv_part
            dk_ref[...] = dk_part

        @pl.when(jnp.logical_not(first))
        def _():
            dv_ref[...] += dv_part
            dk_ref[...] += dk_part

    qspec = pl.BlockSpec((tq, GRP * HEAD_DIM), lambda kv, i: (i, kv))
    return _call(
        body, name=name, grid=(N_KV, S // tq),
        in_specs=[qspec, pl.BlockSpec((S, HEAD_DIM), lambda kv, i: (0, kv)),
                  pl.BlockSpec((S, HEAD_DIM), lambda kv, i: (0, N_KV + kv)), qspec],
        out_specs=[qspec, pl.BlockSpec((S, HEAD_DIM), lambda kv, i: (0, kv)),
                   pl.BlockSpec((S, HEAD_DIM), lambda kv, i: (0, kv))],
        out_shape=[jax.ShapeDtypeStruct((S, Q_COLS), bf16), jax.ShapeDtypeStruct((S, KV_COLS), f32),
                   jax.ShapeDtypeStruct((S, KV_COLS), f32)],
        compiler_params=_params(("parallel", "arbitrary"), 6 * GRP * tq * S * 4 + (8 << 20)),
    )(qr, kr, kv_raw, do)


CONV_HALO = 16


def _fill_padded(pad_ref, val, S):
    pad_ref[pl.ds(0, CONV_HALO), :] = jnp.zeros((CONV_HALO, LANES), f32)
    pad_ref[pl.ds(CONV_HALO + S, CONV_HALO), :] = jnp.zeros((CONV_HALO, LANES), f32)
    pad_ref[pl.ds(CONV_HALO, S), :] = val


def _group_specs(S, n_groups, second_half):
    return pl.BlockSpec((S, LANES), functools.partial(lambda g, o: (0, g + o), o=n_groups if second_half else 0))


def _conv1_fwd(conv_in, wdw, b_dw, name):
    S = conv_in.shape[0]
    ng = CONV_CH // LANES
    R = min(256, S)

    def body(a_ref, g_ref, w_ref, b_ref, o_ref, pad_ref):
        z = a_ref[...].astype(f32) * jax.nn.sigmoid(g_ref[...].astype(f32))
        _fill_padded(pad_ref, z, S)
        for r in range(S // R):
            acc = jnp.zeros((R, LANES), f32) + b_ref[...]
            for j in range(CONV_W):
                acc = acc + w_ref[pl.ds(j, 1), :] * pad_ref[pl.ds(r * R + CONV_HALO - CONV_PAD + j, R), :]
            o_ref[pl.ds(r * R, R), :] = acc

    return _call(
        body, name=name, grid=(ng,),
        in_specs=[_group_specs(S, ng, False), _group_specs(S, ng, True),
                  pl.BlockSpec((CONV_WP, LANES), lambda g: (g, 0)), pl.BlockSpec((1, LANES), lambda g: (0, g))],
        out_specs=pl.BlockSpec((S, LANES), lambda g: (0, g)),
        out_shape=jax.ShapeDtypeStruct((S, CONV_CH), f32),
        scratch_shapes=[pltpu.VMEM((S + 2 * CONV_HALO, LANES), f32)],
        compiler_params=_params(("parallel",), 24 << 20),
    )(conv_in, conv_in, wdw, b_dw)


def _conv1_bwd(conv_in, dc, wdw, name):
    S = conv_in.shape[0]
    ng = CONV_CH // LANES
    R = min(256, S)

    def body(a_ref, g_ref, w_ref, dc_ref, da_ref, dg_ref, dw_ref, db_ref, padz_ref, padd_ref):
        a = a_ref[...].astype(f32)
        sg = jax.nn.sigmoid(g_ref[...].astype(f32))
        _fill_padded(padz_ref, a * sg, S)
        _fill_padded(padd_ref, dc_ref[...], S)
        for r in range(S // R):
            dz = jnp.zeros((R, LANES), f32)
            for j in range(CONV_W):
                dz = dz + w_ref[pl.ds(j, 1), :] * padd_ref[pl.ds(r * R + CONV_HALO + CONV_PAD - j, R), :]
            rows = pl.ds(r * R, R)
            ar, sr = a_ref[rows, :].astype(f32), jax.nn.sigmoid(g_ref[rows, :].astype(f32))
            da_ref[rows, :] = (dz * sr).astype(da_ref.dtype)
            dg_ref[rows, :] = (dz * ar * sr * (1.0 - sr)).astype(dg_ref.dtype)
        for j in range(CONV_W):
            tot = jnp.zeros((1, LANES), f32)
            for r in range(S // R):
                tot = tot + jnp.sum(dc_ref[pl.ds(r * R, R), :] * padz_ref[pl.ds(r * R + CONV_HALO - CONV_PAD + j, R), :],
                                    axis=0, keepdims=True)
            dw_ref[pl.ds(j, 1), :] = tot
        dw_ref[pl.ds(CONV_W, CONV_WP - CONV_W), :] = jnp.zeros((CONV_WP - CONV_W, LANES), f32)
        db_ref[...] = jnp.sum(dc_ref[...], axis=0, keepdims=True)

    return _call(
        body, name=name, grid=(ng,),
        in_specs=[_group_specs(S, ng, False), _group_specs(S, ng, True),
                  pl.BlockSpec((CONV_WP, LANES), lambda g: (g, 0)), pl.BlockSpec((S, LANES), lambda g: (0, g))],
        out_specs=[pl.BlockSpec((S, LANES), lambda g: (0, g)), pl.BlockSpec((S, LANES), lambda g: (0, g)),
                   pl.BlockSpec((CONV_WP, LANES), lambda g: (g, 0)), pl.BlockSpec((1, LANES), lambda g: (0, g))],
        out_shape=[jax.ShapeDtypeStruct((S, CONV_CH), bf16), jax.ShapeDtypeStruct((S, CONV_CH), bf16),
                   jax.ShapeDtypeStruct((ng * CONV_WP, LANES), f32), jax.ShapeDtypeStruct((1, CONV_CH), f32)],
        scratch_shapes=[pltpu.VMEM((S + 2 * CONV_HALO, LANES), f32), pltpu.VMEM((S + 2 * CONV_HALO, LANES), f32)],
        compiler_params=_params(("parallel",), 24 << 20),
    )(conv_in, conv_in, wdw, dc)


def _ln_stats(x, eps=1e-5):
    xc = x - jnp.mean(x, axis=-1, keepdims=True)
    r = lax.rsqrt(jnp.mean(xc * xc, axis=-1, keepdims=True) + eps)
    return r, xc * r


def _ln_bwd(r, xh, dxh):
    return r * (dxh - jnp.mean(dxh, axis=-1, keepdims=True) - xh * jnp.mean(dxh * xh, axis=-1, keepdims=True))


def _conv2_fwd(c, ln_g, ln_b, name):
    def body(c_ref, g_ref, b_ref, o_ref):
        _, xh = _ln_stats(c_ref[...])
        y = xh * g_ref[...] + b_ref[...]
        o_ref[...] = (y * jax.nn.sigmoid(y)).astype(o_ref.dtype)

    return _rows(body, [("r", c), ("f", ln_g), ("f", ln_b)], [("r", CONV_CH, bf16)], tm=min(256, c.shape[0]), name=name)[0]


def _conv2_bwd(c, dcz, ln_g, ln_b, name):
    def body(c_ref, d_ref, g_ref, b_ref, dc_ref, dg_ref, db_ref):
        r, xh = _ln_stats(c_ref[...])
        y = xh * g_ref[...] + b_ref[...]
        sg = jax.nn.sigmoid(y)
        dy = d_ref[...].astype(f32) * (sg * (1.0 + y * (1.0 - sg)))
        dc_ref[...] = _ln_bwd(r, xh, dy * g_ref[...])
        _accumulate(dg_ref, jnp.sum(dy * xh, axis=0, keepdims=True))
        _accumulate(db_ref, jnp.sum(dy, axis=0, keepdims=True))

    return _rows(body, [("r", c), ("r", dcz), ("f", ln_g), ("f", ln_b)],
                 [("r", CONV_CH, f32), ("a", (1, CONV_CH), f32), ("a", (1, CONV_CH), f32)],
                 tm=min(256, c.shape[0]), name=name)


GELU_K = math.sqrt(2.0 / math.pi)
GELU_C = 0.044715


def _gelu(x):
    return 0.5 * x * (1.0 + jnp.tanh(GELU_K * (x + GELU_C * x * x * x)))


def _gelu_grad(x):
    th = jnp.tanh(GELU_K * (x + GELU_C * x * x * x))
    return 0.5 * (1.0 + th) + 0.5 * x * (1.0 - th * th) * (GELU_K * (1.0 + 3.0 * GELU_C * x * x))


def _chunk_rows(n):
    return pl.ds(pl.multiple_of(n * SG_CHUNK, SG_CHUNK), SG_CHUNK)


def _sgu_fwd(sg_in, ln_g, ln_b, w_s, b_s, name):
    S = sg_in.shape[0]

    def body(u_ref, v_ref, lg_ref, lb_ref, w_ref, b_ref, o_ref):
        wb = w_ref[...].astype(bf16)

        def chunk(n, carry):
            rows = _chunk_rows(n)
            gu = _gelu(u_ref[rows, :].astype(f32))
            _, xh = _ln_stats(_gelu(v_ref[rows, :].astype(f32)))
            vl = xh * lg_ref[...] + lb_ref[...]
            t = jnp.dot(wb, vl.astype(bf16), preferred_element_type=f32) + b_ref[...]
            o_ref[rows, :] = (gu * t).astype(o_ref.dtype)
            return carry

        lax.fori_loop(0, S // SG_CHUNK, chunk, 0, unroll=2)

    return _call(
        body, name=name, grid=(SG_G,),
        in_specs=[_group_specs(S, SG_G, False), _group_specs(S, SG_G, True),
                  pl.BlockSpec((1, LANES), lambda g: (0, g)), pl.BlockSpec((1, LANES), lambda g: (0, g)),
                  pl.BlockSpec((None, SG_CHUNK, SG_CHUNK), lambda g: (g, 0, 0)),
                  pl.BlockSpec((None, SG_CHUNK, 1), lambda g: (g, 0, 0))],
        out_specs=pl.BlockSpec((S, LANES), lambda g: (0, g)),
        out_shape=jax.ShapeDtypeStruct((S, SG_CH), bf16),
        compiler_params=_params(("parallel",), 24 << 20),
    )(sg_in, sg_in, ln_g, ln_b, w_s, b_s)


def _sgu_bwd(sg_in, dsz, ln_g, ln_b, w_s, w_s_t, b_s, name):
    S = sg_in.shape[0]

    def body(u_ref, v_ref, lg_ref, lb_ref, w_ref, wt_ref, b_ref, d_ref, du_ref, dv_ref, dw_ref, db_ref, dlg_ref, dlb_ref):
        wb = w_ref[...].astype(bf16)
        wtb = wt_ref[...].astype(bf16)

        def chunk(n, carry):
            dwa, dba, dlga, dlba = carry
            rows = _chunk_rows(n)
            u = u_ref[rows, :].astype(f32)
            v = v_ref[rows, :].astype(f32)
            gu = _gelu(u)
            r, xh = _ln_stats(_gelu(v))
            vlb = (xh * lg_ref[...] + lb_ref[...]).astype(bf16)
            t = jnp.dot(wb, vlb, preferred_element_type=f32) + b_ref[...]
            d = d_ref[rows, :].astype(f32)
            dt = d * gu
            dtb = dt.astype(bf16)
            dwa = dwa + lax.dot_general(dtb, vlb, (((1,), (1,)), ((), ())), preferred_element_type=f32)
            dba = dba + jnp.sum(dt, axis=1, keepdims=True)
            dvl = jnp.dot(wtb, dtb, preferred_element_type=f32)
            dlga = dlga + jnp.sum(dvl * xh, axis=0, keepdims=True)
            dlba = dlba + jnp.sum(dvl, axis=0, keepdims=True)
            dgv = _ln_bwd(r, xh, dvl * lg_ref[...])
            du_ref[rows, :] = (d * t * _gelu_grad(u)).astype(du_ref.dtype)
            dv_ref[rows, :] = (dgv * _gelu_grad(v)).astype(dv_ref.dtype)
            return dwa, dba, dlga, dlba

        init = (jnp.zeros((SG_CHUNK, SG_CHUNK), f32), jnp.zeros((SG_CHUNK, 1), f32),
                jnp.zeros((1, LANES), f32), jnp.zeros((1, LANES), f32))
        dwa, dba, dlga, dlba = lax.fori_loop(0, S // SG_CHUNK, chunk, init, unroll=2)
        dw_ref[...] = dwa
        db_ref[...] = dba
        dlg_ref[...] = dlga
        dlb_ref[...] = dlba

    wspec = pl.BlockSpec((None, SG_CHUNK, SG_CHUNK), lambda g: (g, 0, 0))
    bspec = pl.BlockSpec((None, SG_CHUNK, 1), lambda g: (g, 0, 0))
    lspec = pl.BlockSpec((1, LANES), lambda g: (0, g))
    cspec = pl.BlockSpec((S, LANES), lambda g: (0, g))
    return _call(
        body, name=name, grid=(SG_G,),
        in_specs=[_group_specs(S, SG_G, False), _group_specs(S, SG_G, True), lspec, lspec, wspec, wspec, bspec, cspec],
        out_specs=[cspec, cspec, wspec, bspec, lspec, lspec],
        out_shape=[jax.ShapeDtypeStruct((S, SG_CH), bf16), jax.ShapeDtypeStruct((S, SG_CH), bf16),
                   jax.ShapeDtypeStruct((SG_G, SG_CHUNK, SG_CHUNK), f32), jax.ShapeDtypeStruct((SG_G, SG_CHUNK, 1), f32),
                   jax.ShapeDtypeStruct((1, SG_CH), f32), jax.ShapeDtypeStruct((1, SG_CH), f32)],
        compiler_params=_params(("parallel",), 24 << 20),
    )(sg_in, sg_in, ln_g, ln_b, w_s, w_s_t, b_s, dsz)


def _merge_fwd(gl, b_gate, ya, yc, ys, name):
    D = ya.shape[1]

    def body(gl_ref, b_ref, ya_ref, yc_ref, ys_ref, o_ref):
        acc = jnp.zeros(o_ref.shape, f32)
        for i, y_ref in enumerate((ya_ref, yc_ref, ys_ref)):
            cols = slice(i * D, (i + 1) * D)
            acc = acc + jax.nn.sigmoid(gl_ref[:, cols].astype(f32) + b_ref[:, cols]) * y_ref[...].astype(f32)
        o_ref[...] = acc.astype(o_ref.dtype)

    return _rows(body, [("r", gl), ("f", b_gate), ("r", ya), ("r", yc), ("r", ys)], [("r", D, bf16)],
                 tm=min(128, gl.shape[0]), name=name)[0]


def _merge_bwd(dm, gl, b_gate, ya, yc, ys, name):
    D = ya.shape[1]

    def body(dm_ref, gl_ref, b_ref, ya_ref, yc_ref, ys_ref, dgl_ref, dya_ref, dyc_ref, dys_ref, db_ref):
        dm_ = dm_ref[...].astype(f32)
        for i, (y_ref, dy_ref) in enumerate(((ya_ref, dya_ref), (yc_ref, dyc_ref), (ys_ref, dys_ref))):
            cols = slice(i * D, (i + 1) * D)
            gate = jax.nn.sigmoid(gl_ref[:, cols].astype(f32) + b_ref[:, cols])
            dy_ref[...] = (dm_ * gate).astype(dy_ref.dtype)
            dlog = dm_ * y_ref[...].astype(f32) * gate * (1.0 - gate)
            dgl_ref[:, cols] = dlog.astype(dgl_ref.dtype)
            part = jnp.sum(dlog, axis=0, keepdims=True)
            first = pl.program_id(0) == 0

            @pl.when(first)
            def _():
                db_ref[:, cols] = part

            @pl.when(jnp.logical_not(first))
            def _():
                db_ref[:, cols] += part

    return _rows(body, [("r", dm), ("r", gl), ("f", b_gate), ("r", ya), ("r", yc), ("r", ys)],
                 [("r", 3 * D, bf16), ("r", D, bf16), ("r", D, bf16), ("r", D, bf16), ("a", (1, 3 * D), f32)],
                 tm=min(128, gl.shape[0]), name=name)


def _swiglu_fwd(fg, fu, name):
    def body(g_ref, u_ref, o_ref):
        g = g_ref[...].astype(f32)
        o_ref[...] = (g * jax.nn.sigmoid(g) * u_ref[...].astype(f32)).astype(o_ref.dtype)

    return _rows(body, [("r", fg), ("r", fu)], [("r", fg.shape[1], bf16)], tm=min(128, fg.shape[0]), name=name)[0]


def _swiglu_bwd(dact, fg, fu, name):
    def body(d_ref, g_ref, u_ref, dg_ref, du_ref):
        d = d_ref[...].astype(f32)
        g = g_ref[...].astype(f32)
        sg = jax.nn.sigmoid(g)
        dg_ref[...] = (d * u_ref[...].astype(f32) * sg * (1.0 + g * (1.0 - sg))).astype(dg_ref.dtype)
        du_ref[...] = (d * g * sg).astype(du_ref.dtype)

    return _rows(body, [("r", dact), ("r", fg), ("r", fu)], [("r", fg.shape[1], bf16), ("r", fg.shape[1], bf16)],
                 tm=min(128, fg.shape[0]), name=name)


def _row_tile(r, c, n_arrays, itemsize=4):
    fits = [tm for tm in range(16, r + 1, 16) if r % tm == 0 and 2 * n_arrays * tm * c * itemsize <= (24 << 20)]
    return fits[-1] if fits else r


def _sum_slots(slots, name):
    n, r, c = slots.shape
    tm = _row_tile(r, c, n + 2)

    def body(s_ref, o_ref):
        acc = s_ref[0].astype(f32)
        for k in range(1, n):
            acc = acc + s_ref[k].astype(f32)
        o_ref[...] = acc

    return _call(body, name=name, grid=(r // tm,),
                 in_specs=[pl.BlockSpec((n, tm, c), lambda i: (0, i, 0))],
                 out_specs=pl.BlockSpec((tm, c), lambda i: (i, 0)),
                 out_shape=jax.ShapeDtypeStruct((r, c), f32),
                 compiler_params=_params(("parallel",), 40 << 20))(slots)


def _add_sibling(g4, recv, core, name):
    _, _, r, c = g4.shape
    tm = _row_tile(r, c, 3, 2)

    def body(core_ref, g_ref, r_ref, o_ref):
        o_ref[...] = (g_ref[...].astype(f32) + r_ref[...].astype(f32)).astype(o_ref.dtype)

    grid_spec = pltpu.PrefetchScalarGridSpec(
        num_scalar_prefetch=1, grid=(N_CHIP, r // tm),
        in_specs=[pl.BlockSpec((None, None, tm, c), lambda k, i, core_ref: (k, core_ref[0], i, 0)),
                  pl.BlockSpec((None, tm, c), lambda k, i, core_ref: (k, i, 0))],
        out_specs=pl.BlockSpec((None, tm, c), lambda k, i, core_ref: (k, i, 0)))
    return _call(body, name=name, grid_spec=grid_spec, out_shape=jax.ShapeDtypeStruct((N_CHIP, r, c), bf16),
                 compiler_params=_params(("parallel", "parallel"), 40 << 20))(core, g4, recv)


def _adamw(w, g, m, v, name):
    L, r, c = w.shape
    tm = _row_tile(r, c, 7)
    c1 = 1.0 - ADAM_B1 ** ADAM_STEP
    c2 = 1.0 - ADAM_B2 ** ADAM_STEP

    def body(w_ref, g_ref, m_ref, v_ref, d_ref, mo_ref, vo_ref):
        g_ = g_ref[...]
        m_ = ADAM_B1 * m_ref[...] + (1.0 - ADAM_B1) * g_
        v_ = ADAM_B2 * v_ref[...] + (1.0 - ADAM_B2) * (g_ * g_)
        d_ref[...] = -ADAM_LR * ((m_ / c1) / (jnp.sqrt(v_ / c2) + ADAM_EPS) + ADAM_WD * w_ref[...])
        mo_ref[...] = m_
        vo_ref[...] = v_

    spec = pl.BlockSpec((None, tm, c), lambda l, i: (l, i, 0))
    shp = jax.ShapeDtypeStruct((L, r, c), f32)
    return _call(body, name=name, grid=(L, r // tm), in_specs=[spec] * 4, out_specs=[spec] * 3,
                 out_shape=[shp] * 3, compiler_params=_params(("parallel", "parallel"), 40 << 20))(w, g, m, v)


def _mesh_pos():
    return lax.axis_index("x"), lax.axis_index("y"), lax.axis_index("c")


def _all_gather(shards, after, name):
    n = len(shards)

    def body(*refs):
        x_refs, o_refs = refs[:n], refs[n + 1:2 * n + 1]
        send_sems, recv_sems, local_sems = refs[2 * n + 1:]
        x, y, c = _mesh_pos()
        me, sibling = (x, y, c), (x, y, 1 - c)
        chips = [(1 - x, y), (x, 1 - y), (1 - x, 1 - y)]

        def rows(k, px, py, pc):
            return o_refs[k].at[4 * px + 2 * py + pc]

        def copy(k, s, block, to, src=None):
            return pltpu.make_async_remote_copy(
                src_ref=rows(k, *block) if src is None else src, dst_ref=rows(k, *block),
                send_sem=send_sems.at[k, s], recv_sem=recv_sems.at[k, s], device_id=to, device_id_type=MESH)

        mine = [pltpu.make_async_copy(x_refs[k], rows(k, *me), local_sems.at[k]) for k in range(n)]
        for cp in mine:
            cp.start()
        first = [copy(k, 0, me, sibling, src=x_refs[k]) for k in range(n)]
        for j, chip in enumerate(chips):
            first += [copy(k, 1 + j, me, (*chip, c), src=x_refs[k]) for k in range(n)]
        for cp in first:
            cp.start()
        passed = []
        for j, chip in enumerate(chips):
            for k in range(n):
                copy(k, 1 + j, (*chip, c), me).wait_recv()
                fwd = copy(k, 4 + j, (*chip, c), sibling)
                fwd.start()
                passed.append(fwd)
        for k in range(n):
            copy(k, 0, sibling, me).wait_recv()
        for j, chip in enumerate(chips):
            for k in range(n):
                copy(k, 4 + j, (*chip, 1 - c), me).wait_recv()
        for cp in first + passed:
            cp.wait_send()
        for cp in mine:
            cp.wait()

    return _call(
        body, name=name, in_specs=[HBM] * n + [ANY], out_specs=[HBM] * n,
        out_shape=[jax.ShapeDtypeStruct((N_DEV,) + s.shape, s.dtype) for s in shards],
        scratch_shapes=[pltpu.SemaphoreType.DMA((n, 7)), pltpu.SemaphoreType.DMA((n, 7)), pltpu.SemaphoreType.DMA((n,))],
    )(*shards, after)


def _send_to_sibling(g4s, name):
    n = len(g4s)

    def body(*refs):
        g_refs, o_refs = refs[:n], refs[n:2 * n]
        send_sems, recv_sems = refs[2 * n:]
        x, y, c = _mesh_pos()
        copies = [pltpu.make_async_remote_copy(
            src_ref=g_refs[k].at[:, 1 - c], dst_ref=o_refs[k], send_sem=send_sems.at[k], recv_sem=recv_sems.at[k],
            device_id=(x, y, 1 - c), device_id_type=MESH) for k in range(n)]
        for cp in copies:
            cp.start()
        for cp in copies:
            cp.wait()

    return _call(
        body, name=name, in_specs=[HBM] * n, out_specs=[HBM] * n,
        out_shape=[jax.ShapeDtypeStruct((N_CHIP,) + g.shape[2:], g.dtype) for g in g4s],
        scratch_shapes=[pltpu.SemaphoreType.DMA((n,)), pltpu.SemaphoreType.DMA((n,))],
    )(*g4s)


SEM = pl.BlockSpec(memory_space=pltpu.SEMAPHORE)
ANY = pl.BlockSpec(memory_space=pl.ANY)
EFFECT = pltpu.SideEffectType.DATAFLOW_SIDE_EFFECTING


def _other_chips(x, y):
    return [(1 - x, y), (x, 1 - y), (1 - x, 1 - y)]


def _ici_copy(kind, src_ref, land_ref, send_sem, recv_sem, sender, target, c):
    (sx, sy), (tx, ty) = sender, target
    if kind == "gather":
        src, dst = src_ref, land_ref.at[4 * sx + 2 * sy + c]
    else:
        src, dst = src_ref.at[2 * tx + ty], land_ref.at[2 * sx + sy]
    return pltpu.make_async_remote_copy(src_ref=src, dst_ref=dst, send_sem=send_sem, recv_sem=recv_sem,
                                        device_id=(tx, ty, c), device_id_type=MESH)


def _ici_start(kind, srcs, lands, after, name):
    n = len(srcs)

    def body(*refs):
        src_refs, land_refs = refs[:n], refs[n:2 * n]
        send_sems, recv_sems = refs[2 * n + 1], refs[2 * n + 2]
        token = refs[-1]
        x, y, c = _mesh_pos()
        for j, chip in enumerate(_other_chips(x, y)):
            for k in range(n):
                _ici_copy(kind, src_refs[k], land_refs[k], send_sems.at[3 * k + j], recv_sems.at[3 * k + j],(x, y), chip, c).start()
        token[...] = jnp.zeros_like(token)

    bufs = list(srcs) + list(lands)
    return _call(
        body, name=name,
        out_shape=(pltpu.SemaphoreType.DMA((3 * n,)), pltpu.SemaphoreType.DMA((3 * n,)),
                   *[pltpu.HBM(b.shape, b.dtype) for b in bufs], jax.ShapeDtypeStruct((8, LANES), f32)),
        in_specs=[HBM] * (2 * n) + [ANY], out_specs=(SEM, SEM, *[HBM] * (2 * n), pl.BlockSpec(memory_space=pltpu.VMEM)),
        input_output_aliases={i: 2 + i for i in range(2 * n)},
        compiler_params=pltpu.CompilerParams(has_side_effects=EFFECT),
    )(*[pltpu.with_memory_space_constraint(b, pltpu.HBM) for b in bufs], after)


def _ici_wait(kind, started, after, name):
    send_sems, recv_sems, *bufs = started[:-1]
    n = len(bufs) // 2

    def body(*refs):
        src_refs, land_refs = refs[:n], refs[n:2 * n]
        send_sems, recv_sems = refs[2 * n], refs[2 * n + 1]
        x, y, c = _mesh_pos()
        for j, chip in enumerate(_other_chips(x, y)):
            for k in range(n):
                _ici_copy(kind, src_refs[k], land_refs[k], send_sems.at[3 * k + j], recv_sems.at[3 * k + j],(x, y), chip, c).wait_send()
                _ici_copy(kind, src_refs[k], land_refs[k], send_sems.at[3 * k + j], recv_sems.at[3 * k + j],chip, (x, y), c).wait_recv()

    out = _call(
        body, name=name, out_shape=[pltpu.HBM(b.shape, b.dtype) for b in bufs],
        in_specs=[HBM] * (2 * n) + [SEM, SEM, ANY], out_specs=[HBM] * (2 * n),
        input_output_aliases={i: i for i in range(2 * n)},
        compiler_params=pltpu.CompilerParams(has_side_effects=EFFECT),
    )(*bufs, send_sems, recv_sems, after)
    return out[:n], out[n:]


def _d2d_gather(lands, after, name):
    n = len(lands)

    def body(*refs):
        in_refs, o_refs = refs[:n], refs[n + 1:2 * n + 1]
        send_sems, recv_sems = refs[2 * n + 1:]
        x, y, c = _mesh_pos()
        copies = [pltpu.make_async_remote_copy(
            src_ref=in_refs[k].at[:, c], dst_ref=o_refs[k].at[:, c], send_sem=send_sems.at[k], recv_sem=recv_sems.at[k],
            device_id=(x, y, 1 - c), device_id_type=MESH) for k in range(n)]
        for cp in copies:
            cp.start()
        for k, cp in enumerate(copies):
            cp.wait_send()
            pltpu.make_async_remote_copy(
                src_ref=in_refs[k].at[:, c], dst_ref=o_refs[k].at[:, 1 - c], send_sem=send_sems.at[k],
                recv_sem=recv_sems.at[k], device_id=(x, y, 1 - c), device_id_type=MESH).wait_recv()

    return _call(
        body, name=name, in_specs=[HBM] * n + [ANY], out_specs=[HBM] * n,
        out_shape=[jax.ShapeDtypeStruct(b.shape, b.dtype) for b in lands],
        input_output_aliases={k: k for k in range(n)},
        scratch_shapes=[pltpu.SemaphoreType.DMA((n,)), pltpu.SemaphoreType.DMA((n,))],
    )(*lands, after)


def _sum_chip_slots(lands, sums, chip, name):
    _, r, c = lands.shape
    tm = _row_tile(r, c, 10, 2)

    def body(chip_ref, l_ref, s_ref, o_ref):
        acc = None
        for k in range(N_CHIP):
            part = jnp.where(chip_ref[0] == k, s_ref[k], l_ref[k]).astype(f32)
            acc = part if acc is None else acc + part
        o_ref[...] = acc

    grid_spec = pltpu.PrefetchScalarGridSpec(
        num_scalar_prefetch=1, grid=(r // tm,),
        in_specs=[pl.BlockSpec((N_CHIP, tm, c), lambda i, chip_ref: (0, i, 0)),
                  pl.BlockSpec((N_CHIP, tm, c), lambda i, chip_ref: (0, i, 0))],
        out_specs=pl.BlockSpec((tm, c), lambda i, chip_ref: (i, 0)))
    return _call(body, name=name, grid_spec=grid_spec, out_shape=jax.ShapeDtypeStruct((r, c), f32),
                 compiler_params=_params(("parallel",), 40 << 20))(chip, lands, sums)


def _reduce_start(grads, core, after, tag):
    g4s = [g.reshape(N_CHIP, 2, g.shape[0] // N_DEV, g.shape[1]) for g in grads]
    recv = _send_to_sibling(g4s, name="rs_sibling_" + tag)
    sums = [_add_sibling(g4, rv, core, name="rs_add_" + tag) for g4, rv in zip(g4s, recv)]
    lands = [lax.empty(s.shape, s.dtype) for s in sums]
    return _ici_start("reduce", sums, lands, after, name="rs_start_" + tag)


def _reduce_finish(started, chip, after, tag):
    sums, lands = _ici_wait("reduce", started, after, name="rs_wait_" + tag)
    return [_sum_chip_slots(ld, s, chip, name="rs_sum_" + tag) for ld, s in zip(lands, sums)]


def _rope_tables(S):
    rows = S // GRID_W
    row = jnp.repeat(jnp.arange(rows, dtype=f32), GRID_W)
    col = jnp.tile(jnp.arange(GRID_W, dtype=f32), rows)
    nf = HEAD_DIM // 4
    inv = ROPE_THETA ** (-jnp.arange(nf, dtype=f32) / nf)
    ang = jnp.concatenate([row[:, None] * inv, col[:, None] * inv], axis=-1)
    cos, sin = jnp.cos(ang), jnp.sin(ang)
    return jnp.concatenate([cos, cos], axis=-1), jnp.concatenate([-sin, sin], axis=-1)


def _layer_fwd(xin, p, w, more_weights, cos2, sin2):
    sv = {"xin": xin}
    h = sv["h"] = _rms_fwd(xin, p["g_mix"], name="rms_mix")
    proj = functools.partial(_mm, h, w["in"], "nt", bf16)
    q_raw = sv["q_raw"] = proj(n=Q_COLS, b_off=0, name="proj_q")
    kv_raw = sv["kv_raw"] = proj(n=2 * KV_COLS, b_off=OFF_KV, name="proj_kv")
    conv_in = sv["conv_in"] = proj(n=2 * CONV_CH, b_off=OFF_CONV, name="proj_conv")
    sg_in = sv["sg_in"] = proj(n=2 * SG_CH, b_off=OFF_SG, name="proj_sg")
    gl = sv["gl"] = proj(n=3 * D_MODEL, b_off=OFF_GATE, name="proj_gate")
    qr, kr = sv["qr"], sv["kr"] = _qk_fwd(q_raw, kv_raw, p["q_norm_g"], p["k_norm_g"], cos2, sin2, name="qk_fwd")
    o = sv["o"] = _attn_fwd(qr, kr, kv_raw, name="attn_fwd")
    c = sv["c"] = _conv1_fwd(conv_in, w["dw"], p["b_dw"], name="conv1_fwd")
    cz = sv["cz"] = _conv2_fwd(c, p["conv_ln_g"], p["conv_ln_b"], name="conv2_fwd")
    sz = sv["sz"] = _sgu_fwd(sg_in, p["sg_ln_g"], p["sg_ln_b"], p["w_s"], p["b_s"], name="sgu_fwd")
    w = {**w, **more_weights(1, sz)}
    ya = sv["ya"] = _mm(o, w["attn_o"], "nt", bf16, name="out_attn")
    yc = sv["yc"] = _mm(cz, w["conv_o"], "nt", bf16, name="out_conv")
    ys = sv["ys"] = _mm(sz, w["sg_o"], "nt", bf16, name="out_sg")
    merged = sv["merged"] = _merge_fwd(gl, p["b_gate"], ya, yc, ys, name="merge_fwd")
    x1 = sv["x1"] = _mm(merged, w["out"], "nn", f32, res=xin, name="out_proj")
    w = {**w, **more_weights(2, x1)}
    hf = sv["hf"] = _rms_fwd(x1, p["g_ffn"], name="rms_ffn")
    fg = sv["fg"] = _mm(hf, w["ff_gate"], "nt", bf16, name="ff_gate")
    fu = sv["fu"] = _mm(hf, w["ff_up"], "nt", bf16, name="ff_up")
    act = sv["act"] = _swiglu_fwd(fg, fu, name="swiglu_fwd")
    x2 = _mm(act, w["ff_down"], "nn", f32, res=x1, name="ff_down")
    return x2, sv, w


def _layer_bwd(dx2, dx2b, sv, p, w, cos2, sin2, reduce_start):
    small = {}
    dact = _mm(dx2b, w["ff_down"], "nt", bf16, name="d_act")
    g_down = _mm(sv["act"], dx2b, "tn", bf16, name="g_ff_down")
    dfg, dfu = _swiglu_bwd(dact, sv["fg"], sv["fu"], name="swiglu_bwd")
    dhf = _mm(dfg, w["ff_gate"], "nn", f32, name="d_hf_gate")
    dhf = _mm(dfu, w["ff_up"], "nn", f32, res=dhf, name="d_hf_up")
    g_gate = _mm(dfg, sv["hf"], "tn", bf16, name="g_ff_gate")
    g_up = _mm(dfu, sv["hf"], "tn", bf16, name="g_ff_up")
    zero = reduce_start("ffn", dict(w_ff_gate=g_gate, w_ff_up=g_up, w_ff_down=g_down))[0, 0]
    dx1, dx1b, small["g_ffn"] = _rms_bwd(sv["x1"], p["g_ffn"] + zero, dhf, dx2, name="rms_ffn_bwd")
    dmerged = _mm(dx1b, w["out"], "nt", bf16, name="d_merged")
    g_out = _mm(sv["merged"], dx1b, "tn", bf16, name="g_out")
    dgl, dya, dyc, dys, small["b_gate"] = _merge_bwd(dmerged, sv["gl"], p["b_gate"], sv["ya"], sv["yc"], sv["ys"],
                                                    name="merge_bwd")
    do = _mm(dya, w["attn_o"], "nn", bf16, name="d_o")
    g_ao = _mm(dya, sv["o"], "tn", bf16, name="g_attn_o")
    dcz = _mm(dyc, w["conv_o"], "nn", bf16, name="d_cz")
    g_co = _mm(dyc, sv["cz"], "tn", bf16, name="g_conv_o")
    dsz = _mm(dys, w["sg_o"], "nn", bf16, name="d_sz")
    g_so = _mm(dys, sv["sz"], "tn", bf16, name="g_sg_o")
    zero = reduce_start("mix", dict(w_attn_o=g_ao, w_conv_o=g_co, w_sg_o=g_so, w_out=g_out))[0, 0]
    dsu, dsv, small["w_s"], small["b_s"], small["sg_ln_g"], small["sg_ln_b"] = _sgu_bwd(
        sv["sg_in"], dsz, p["sg_ln_g"] + zero, p["sg_ln_b"], p["w_s"], p["w_s_t"], p["b_s"], name="sgu_bwd")
    dc, small["conv_ln_g"], small["conv_ln_b"] = _conv2_bwd(sv["c"], dcz, p["conv_ln_g"], p["conv_ln_b"], name="conv2_bwd")
    da, dgt, small["w_dw"], small["b_dw"] = _conv1_bwd(sv["conv_in"], dc, w["dw"], name="conv1_bwd")
    dqr, dkr, dv = _attn_bwd(sv["qr"], sv["kr"], sv["kv_raw"], do, name="attn_bwd")
    dq_raw, dk_raw, small["q_norm_g"], small["k_norm_g"] = _qk_bwd(
        sv["q_raw"], sv["kv_raw"], dqr, dkr, p["q_norm_g"], p["k_norm_g"], cos2, sin2, name="qk_bwd")
    dproj = jnp.concatenate([dq_raw, dk_raw, dv.astype(bf16), da, dgt, dsu, dsv, dgl], axis=1)
    g_in = _mm(dproj, sv["h"], "tn", bf16, name="g_in")
    started = reduce_start("in", dict(w_in=g_in))
    dh = _mm(dproj, w["in"], "nn", f32, after=started, name="d_h")
    dx, dxb, small["g_mix"] = _rms_bwd(sv["xin"], p["g_mix"], dh, dx1, name="rms_mix_bwd")
    return dx, dxb, small


SMALL = ("g_mix", "b_gate", "q_norm_g", "k_norm_g", "b_dw", "conv_ln_g", "conv_ln_b", "sg_ln_g", "sg_ln_b",
         "w_s", "b_s", "g_ffn")
PACK_ALIGN = 8 * LANES


def _pack(parts):
    flat = jnp.concatenate([a.reshape(-1).astype(f32) for a in parts])
    pad = -flat.shape[0] % PACK_ALIGN
    return jnp.pad(flat, (0, pad)).reshape(-1, LANES)


def _unpack(buf, shapes):
    flat = buf.reshape(-1)
    out, pos = [], 0
    for shp in shapes:
        size = math.prod(shp)
        out.append(flat[pos:pos + size].reshape(shp))
        pos += size
    return out


def kernel(x, g_mix, w_in, b_gate, q_norm_g, k_norm_g, w_attn_o, w_dw, b_dw, conv_ln_g, conv_ln_b, w_conv_o, sg_ln_g, sg_ln_b, w_s, b_s, w_sg_o, w_out, g_ffn, w_ff_gate, w_ff_up, w_ff_down, g_final, loss_target, m_g_mix, m_w_in, m_b_gate, m_q_norm_g, m_k_norm_g, m_w_attn_o, m_w_dw, m_b_dw, m_conv_ln_g, m_conv_ln_b, m_w_conv_o, m_sg_ln_g, m_sg_ln_b, m_w_s, m_b_s, m_w_sg_o, m_w_out, m_g_ffn, m_w_ff_gate, m_w_ff_up, m_w_ff_down, m_g_final, v_g_mix, v_w_in, v_b_gate, v_q_norm_g, v_k_norm_g, v_w_attn_o, v_w_dw, v_b_dw, v_conv_ln_g, v_conv_ln_b, v_w_conv_o, v_sg_ln_g, v_sg_ln_b, v_w_s, v_b_s, v_w_sg_o, v_w_out, v_g_ffn, v_w_ff_gate, v_w_ff_up, v_w_ff_down, v_g_final):
    weights = dict(g_mix=g_mix, w_in=w_in, b_gate=b_gate, q_norm_g=q_norm_g, k_norm_g=k_norm_g, w_attn_o=w_attn_o,
                   w_dw=w_dw, b_dw=b_dw, conv_ln_g=conv_ln_g, conv_ln_b=conv_ln_b, w_conv_o=w_conv_o, sg_ln_g=sg_ln_g,
                   sg_ln_b=sg_ln_b, w_s=w_s, b_s=b_s, w_sg_o=w_sg_o, w_out=w_out, g_ffn=g_ffn, w_ff_gate=w_ff_gate,
                   w_ff_up=w_ff_up, w_ff_down=w_ff_down, g_final=g_final)
    mom_m = dict(g_mix=m_g_mix, w_in=m_w_in, b_gate=m_b_gate, q_norm_g=m_q_norm_g, k_norm_g=m_k_norm_g,
                 w_attn_o=m_w_attn_o, w_dw=m_w_dw, b_dw=m_b_dw, conv_ln_g=m_conv_ln_g, conv_ln_b=m_conv_ln_b,
                 w_conv_o=m_w_conv_o, sg_ln_g=m_sg_ln_g, sg_ln_b=m_sg_ln_b, w_s=m_w_s, b_s=m_b_s, w_sg_o=m_w_sg_o,
                 w_out=m_w_out, g_ffn=m_g_ffn, w_ff_gate=m_w_ff_gate, w_ff_up=m_w_ff_up, w_ff_down=m_w_ff_down,
                 g_final=m_g_final)
    mom_v = dict(g_mix=v_g_mix, w_in=v_w_in, b_gate=v_b_gate, q_norm_g=v_q_norm_g, k_norm_g=v_k_norm_g,
                 w_attn_o=v_w_attn_o, w_dw=v_w_dw, b_dw=v_b_dw, conv_ln_g=v_conv_ln_g, conv_ln_b=v_conv_ln_b,
                 w_conv_o=v_w_conv_o, sg_ln_g=v_sg_ln_g, sg_ln_b=v_sg_ln_b, w_s=v_w_s, b_s=v_b_s, w_sg_o=v_w_sg_o,
                 w_out=v_w_out, g_ffn=v_g_ffn, w_ff_gate=v_w_ff_gate, w_ff_up=v_w_ff_up, w_ff_down=v_w_ff_down,
                 g_final=v_g_final)
    S, D = x.shape[1], x.shape[2]
    xi, yi, ci = _mesh_pos()
    me = 4 * xi + 2 * yi + ci
    core = jnp.reshape(ci, (1,)).astype(jnp.int32)
    cos2, sin2 = _rope_tables(S)

    big = ("w_in", "w_attn_o", "w_conv_o", "w_sg_o", "w_out", "w_ff_gate", "w_ff_up", "w_ff_down")
    transposed = {"w_in", "w_attn_o", "w_conv_o", "w_sg_o", "w_ff_gate", "w_ff_up"}
    chip = jnp.reshape(2 * xi + yi, (1,)).astype(jnp.int32)
    groups = (("in", "dw"), ("attn_o", "conv_o", "sg_o", "out"), ("ff_gate", "ff_up", "ff_down"))
    P, shards = [], []
    for l in range(DEPTH):
        sh = {n[2:]: (weights[n][l].T if n in transposed else weights[n][l]).astype(bf16) for n in big}
        sh["dw"] = jnp.pad(w_dw[l].reshape(CONV_W, LANES), ((0, CONV_WP - CONV_W), (0, 0)))
        shards.append(sh)
        p = {n: weights[n][l].reshape(1, -1) for n in SMALL if n not in ("w_s", "b_s")}
        p["w_s"] = w_s[l]
        p["w_s_t"] = jnp.swapaxes(w_s[l], 1, 2)
        p["b_s"] = b_s[l].reshape(SG_G, SG_CHUNK, 1)
        P.append(p)

    gathers = {}

    def start_gather(l, gi, after):
        srcs = [shards[l][n] for n in groups[gi]]
        lands = [lax.dynamic_update_index_in_dim(lax.empty((N_DEV,) + s.shape, s.dtype), s, me, 0) for s in srcs]
        gathers[l, gi] = _ici_start("gather", srcs, lands, after, name=f"ag_start_{l}{gi}")
        return gathers[l, gi][-1]

    def gathered(l, gi, after):
        srcs, lands = _ici_wait("gather", gathers[l, gi], after, name=f"ag_wait_{l}{gi}")
        after = srcs[0]
        if gi == len(groups) - 1 and l + 1 < DEPTH:
            for gj in range(len(groups)):
                after = start_gather(l + 1, gj, after)
        full = _d2d_gather([b.reshape(N_CHIP, 2, *b.shape[1:]) for b in lands], after, name=f"ag_d2d_{gi}")
        return {n: f.reshape(-1, f.shape[3]) for n, f in zip(groups[gi], full)}

    all_started = cos2
    for gi in range(len(groups)):
        all_started = start_gather(0, gi, all_started)

    h = x.reshape(S, D)
    saved, W = [], []
    for l in range(DEPTH):
        first = gathered(l, 0, all_started if l == 0 else h)
        if l == 0:
            P[l]["g_mix"] = P[l]["g_mix"] + all_started[0, 0]
        h, sv, w = _layer_fwd(h, P[l], first, functools.partial(lambda gi, z, l: gathered(l, gi, z), l=l), cos2, sin2)
        saved.append(sv)
        W.append(w)
    dx, dxb, sq, g_final_part = _final_loss(h, g_final.reshape(1, D), loss_target.reshape(S, D), name="final_loss")
    loss = lax.psum(0.5 * jnp.sum(sq) / D, ("x", "y", "c"))

    reductions, small_grads = {}, [None] * DEPTH
    for l in reversed(range(DEPTH)):
        def reduce_start(group, grads, l=l):
            names = tuple(grads)
            started = _reduce_start([grads[n] for n in names], core, grads[names[0]], tag=f"{group}{l}")
            reductions[l, group] = (names, started)
            return started[-1]

        dx, dxb, small_grads[l] = _layer_bwd(dx, dxb, saved[l], P[l], W[l], cos2, sin2, reduce_start)
    grad_x = dx.reshape(x.shape)

    grads_out, delta, new_m, new_v = {}, {}, {}, {}
    swap = lambda a: jnp.swapaxes(a, 1, 2)

    def update(n, per_layer):
        g = jnp.stack(per_layer)
        if n in transposed and weights[n].shape[2] % LANES:
            upd = _adamw(swap(weights[n]), g, swap(mom_m[n]), swap(mom_v[n]), name="adamw_" + n)
            grads_out[n], (delta[n], new_m[n], new_v[n]) = swap(g), [swap(u) for u in upd]
        else:
            grads_out[n] = swap(g) if n in transposed else g
            delta[n], new_m[n], new_v[n] = _adamw(weights[n], grads_out[n], mom_m[n], mom_v[n], name="adamw_" + n)
        return delta[n]

    after = dx
    for group in ("ffn", "mix", "in"):
        done = [dict(zip(reductions[l, group][0], _reduce_finish(reductions[l, group][1], chip, after, tag=f"{group}{l}")))
                for l in range(DEPTH)]
        for n in reductions[0, group][0]:
            after = update(n, [done[l][n] for l in range(DEPTH)])

    small_shapes = [weights[n].shape for n in SMALL] + [g_final.shape, (DEPTH, CONV_CH // LANES, CONV_WP, LANES)]
    parts = [jnp.stack([small_grads[l][n].reshape(weights[n].shape[1:]) for l in range(DEPTH)]) for n in SMALL]
    parts += [g_final_part.reshape(g_final.shape), jnp.stack([small_grads[l]["w_dw"] for l in range(DEPTH)])]
    packed = _pack(parts)
    gathered = _all_gather([packed], after, name="gather_small")[0]
    total = _sum_slots(gathered, name="sum_small")
    small_total = _unpack(total, small_shapes)
    grads_out.update(zip(SMALL + ("g_final",), small_total[:-1]))
    dw_full = small_total[-1]
    grads_out["w_dw"] = lax.dynamic_index_in_dim(dw_full, me, axis=1, keepdims=False)[:, :CONV_W].reshape(w_dw.shape)

    rep = tuple(n for n in SMALL if n != "w_s") + ("g_final",)
    rep_shapes = [weights[n].shape for n in rep]
    packs = [_pack([src[n] for n in rep])[None] for src in (weights, grads_out, mom_m, mom_v)]
    for dst, buf in zip((delta, new_m, new_v), _adamw(*packs, name="adamw_small")):
        dst.update(zip(rep, _unpack(buf[0], rep_shapes)))
    for n, shp in (("w_dw", (1, DEPTH * CONV_W, LANES)), ("w_s", (DEPTH, SG_G * SG_CHUNK, SG_CHUNK))):
        upd = _adamw(*[src[n].reshape(shp) for src in (weights, grads_out, mom_m, mom_v)], name="adamw_" + n)
        for dst, buf in zip((delta, new_m, new_v), upd):
            dst[n] = buf.reshape(weights[n].shape)

    order = ("g_mix", "w_in", "b_gate", "q_norm_g", "k_norm_g", "w_attn_o", "w_dw", "b_dw", "conv_ln_g", "conv_ln_b",
             "w_conv_o", "sg_ln_g", "sg_ln_b", "w_s", "b_s", "w_sg_o", "w_out", "g_ffn", "w_ff_gate", "w_ff_up",
             "w_ff_down", "g_final")
    return (loss, grad_x, *[grads_out[n] for n in order], *[delta[n] for n in order],
            *[new_m[n] for n in order], *[new_v[n] for n in order])
```

```python
import functools
import math

import jax
import jax.numpy as jnp
from jax import lax
from jax.experimental import pallas as pl
from jax.experimental.pallas import tpu as pltpu

f32, bf16 = jnp.float32, jnp.bfloat16

D_MODEL = 2048
SEQ = 2048
DEPTH = 2
GRID_W = 64
HEAD_DIM = 128
LANES = 128
N_Q = (D_MODEL // 2) // HEAD_DIM
N_KV = N_Q // 4
GRP = N_Q // N_KV
Q_COLS = N_Q * HEAD_DIM
KV_COLS = N_KV * HEAD_DIM
CONV_CH = D_MODEL // 2
CONV_W = 31
CONV_PAD = CONV_W // 2
CONV_WP = 32
SG_CH = D_MODEL // 2
SG_G = SG_CH // LANES
SG_CHUNK = 128
D_FF = -(-8 * D_MODEL // (3 * 256)) * 256
OFF_KV = Q_COLS
OFF_CONV = OFF_KV + 2 * KV_COLS
OFF_SG = OFF_CONV + 2 * CONV_CH
OFF_GATE = OFF_SG + 2 * SG_CH
IN_COLS = OFF_GATE + 3 * D_MODEL
ROPE_THETA = 10000.0
SCALE = HEAD_DIM ** -0.5
N_DEV = 8
N_CHIP = 4

ADAM_LR, ADAM_B1, ADAM_B2, ADAM_EPS, ADAM_WD, ADAM_STEP = 0.001, 0.9, 0.999, 1e-08, 0.01, 10

VMEM_BYTES_V7X = 64 << 20
VMEM_CAP = VMEM_BYTES_V7X - (6 << 20)
MESH = pl.DeviceIdType.MESH
HBM = pl.BlockSpec(memory_space=pltpu.HBM)


def _in_hbm(a):
    if isinstance(a, jax.Array) and jnp.issubdtype(a.dtype, jnp.floating) and a.size * a.dtype.itemsize >= (1 << 20):
        return pltpu.with_memory_space_constraint(a, pltpu.HBM)
    return a


def _out_hbm(s):
    if isinstance(s, jax.ShapeDtypeStruct) and math.prod(s.shape) * jnp.dtype(s.dtype).itemsize >= (1 << 20):
        return pltpu.HBM(s.shape, s.dtype)
    return s


def _call(body, **kw):
    shapes = kw.pop("out_shape")
    shapes = type(shapes)(_out_hbm(s) for s in shapes) if isinstance(shapes, (list, tuple)) else _out_hbm(shapes)
    call = pl.pallas_call(body, out_shape=shapes, **kw)
    return lambda *args: call(*[_in_hbm(a) for a in args])


def _pick(n, cands):
    for c in cands:
        if n % c == 0:
            return c
    raise ValueError((n, cands))


def _params(sem, vmem_bytes):
    return pltpu.CompilerParams(dimension_semantics=sem, vmem_limit_bytes=int(min(max(vmem_bytes, 16 << 20), VMEM_CAP)))


def _mm(a, b, form, out_dtype, *, n=None, b_off=0, res=None, after=None, name):
    if form == "tn":
        K, M = a.shape
    else:
        M, K = a.shape
    N = n if n is not None else (b.shape[0] if form == "nt" else b.shape[1])
    if K <= 2048:
        tk = K
        if form == "tn":
            tm = _pick(M, (512, 256, 128))
            tn = N if N <= 2048 else _pick(N, (1024, 512, 256, 128))
        else:
            tm = M if M <= 2048 else _pick(M, (2048, 1024, 512))
            tn = _pick(math.gcd(N, b_off) if b_off else N, (256, 128) if res is not None else (512, 256, 128))
    else:
        tk = max(t for t in range(LANES, 3072 + 1, LANES) if K % t == 0)
        tm = _pick(M, (1024, 512, 256, 128))
        tn = _pick(math.gcd(N, b_off) if b_off else N, (1024, 512, 256, 128))
    assert b_off % tn == 0
    off = b_off // tn
    nk = K // tk
    if form == "tn":
        a_spec = pl.BlockSpec((tk, tm), lambda i, j, k: (k, i))
    else:
        a_spec = pl.BlockSpec((tm, tk), lambda i, j, k: (i, k))
    if form == "nt":
        b_spec = pl.BlockSpec((tn, tk), lambda i, j, k: (j + off, k))
    else:
        b_spec = pl.BlockSpec((tk, tn), lambda i, j, k: (k, j + off))
    dims = {"nn": ((1,), (0,)), "nt": ((1,), (1,)), "tn": ((0,), (0,))}[form]
    has_res = res is not None

    def body(*refs):
        if after is not None:
            refs = refs[1:]
        if has_res:
            a_ref, b_ref, r_ref, o_ref = refs[:4]
        else:
            a_ref, b_ref, o_ref = refs[:3]
        p = lax.dot_general(a_ref[...], b_ref[...], (dims, ((), ())), preferred_element_type=f32)

        def finish(acc):
            if has_res:
                acc = acc + r_ref[...].astype(f32)
            o_ref[...] = acc.astype(o_ref.dtype)

        if nk == 1:
            finish(p)
        else:
            acc_ref = refs[-1]
            k = pl.program_id(2)

            @pl.when(k == 0)
            def _():
                acc_ref[...] = p

            @pl.when(k > 0)
            def _():
                acc_ref[...] += p

            @pl.when(k == nk - 1)
            def _():
                finish(acc_ref[...])

    in_specs = [a_spec, b_spec]
    args = [a, b]
    osz = jnp.dtype(out_dtype).itemsize
    vmem = 2 * (tm * tk * 2 + tk * tn * 2 + tm * tn * osz) + 2 * tm * tn * 4
    if has_res:
        in_specs.append(pl.BlockSpec((tm, tn), lambda i, j, k: (i, j)))
        args.append(res)
        vmem += 2 * tm * tn * res.dtype.itemsize
    scratch = []
    if nk > 1:
        scratch.append(pltpu.VMEM((tm, tn), f32))
        vmem += tm * tn * 4
    if after is not None:
        in_specs.insert(0, pl.BlockSpec(memory_space=pl.ANY))
        args.insert(0, after)
    return _call(
        body, name=name, grid=(M // tm, N // tn, nk),
        in_specs=in_specs, out_specs=pl.BlockSpec((tm, tn), lambda i, j, k: (i, j)),
        out_shape=jax.ShapeDtypeStruct((M, N), out_dtype), scratch_shapes=scratch,
        compiler_params=_params(("parallel", "parallel", "arbitrary"), vmem + (8 << 20)),
    )(*args)


EPI_TN = 256


def _mm_epi(a, bs, form, extras, out_dtypes, n_sums, fn, name):
    M, K = a.shape
    N = bs[0].shape[0] if form == "nt" else bs[0].shape[1]
    tn = EPI_TN
    assert K <= 2048 and N % tn == 0
    dims = ((1,), (1,)) if form == "nt" else ((1,), (0,))
    nb, ne = len(bs), len(extras)

    def body(*refs):
        a_ref, b_refs, e_refs, o_refs = refs[0], refs[1:1 + nb], refs[1 + nb:1 + nb + ne], refs[1 + nb + ne:]
        av = a_ref[...]
        ps = [lax.dot_general(av, b[...], (dims, ((), ())), preferred_element_type=f32) for b in b_refs]
        for o_ref, o in zip(o_refs, fn(ps, [e[...] for e in e_refs])):
            o_ref[...] = o.astype(o_ref.dtype)

    in_specs = [pl.BlockSpec((M, K), lambda j: (0, 0), pipeline_mode=pl.Buffered(1))]
    in_specs += [pl.BlockSpec((tn, K), lambda j: (j, 0)) if form == "nt" else pl.BlockSpec((K, tn), lambda j: (0, j))
                 for _ in bs]
    for arr, first in extras:
        assert first % tn == 0
        in_specs.append(pl.BlockSpec((arr.shape[0], tn), functools.partial(lambda j, o: (0, j + o), o=first // tn)))
    out_specs = [pl.BlockSpec((M, tn), lambda j: (0, j))] * len(out_dtypes) + [pl.BlockSpec((1, tn), lambda j: (0, j))] * n_sums
    out_shape = [jax.ShapeDtypeStruct((M, N), dt) for dt in out_dtypes] + [jax.ShapeDtypeStruct((1, N), f32)] * n_sums
    tiles = sum(arr.shape[0] * tn * arr.dtype.itemsize for arr, _ in extras) + sum(M * tn * jnp.dtype(dt).itemsize for dt in out_dtypes)
    vmem = M * K * 2 + 2 * nb * tn * K * 2 + 2 * tiles + (nb + 6) * M * tn * 4
    return _call(body, name=name, grid=(N // tn,), in_specs=in_specs, out_specs=out_specs, out_shape=out_shape,
                 compiler_params=_params(("parallel",), vmem + (8 << 20)))(a, *bs, *[arr for arr, _ in extras])


def _ffn_up(hf, wt_gate, wt_up, name):
    def fn(ps, _):
        g, u = ps[0].astype(bf16), ps[1].astype(bf16)
        gf = g.astype(f32)
        return g, u, gf * jax.nn.sigmoid(gf) * u.astype(f32)

    return _mm_epi(hf, [wt_gate, wt_up], "nt", [], [bf16] * 3, 0, fn, name)


def _ffn_down_bwd(dx2b, w_down, fg, fu, name):
    def fn(ps, es):
        d, g = ps[0], es[0].astype(f32)
        sg = jax.nn.sigmoid(g)
        return d * es[1].astype(f32) * sg * (1.0 + g * (1.0 - sg)), d * g * sg

    return _mm_epi(dx2b, [w_down], "nt", [(fg, 0), (fu, 0)], [bf16] * 2, 0, fn, name)


def _merge_bwd_fused(dx1b, w_out, gl, b_gate, ya, yc, ys, name):
    D = ya.shape[1]

    def fn(ps, es):
        dm_, outs, sums = ps[0], [], []
        for i in range(3):
            gate = jax.nn.sigmoid(es[i].astype(f32) + es[3 + i])
            dlog = dm_ * es[6 + i].astype(f32) * gate * (1.0 - gate)
            outs.append((dlog, dm_ * gate))
            sums.append(jnp.sum(dlog, axis=0, keepdims=True))
        return [o[0] for o in outs] + [o[1] for o in outs] + sums

    extras = [(gl, i * D) for i in range(3)] + [(b_gate, i * D) for i in range(3)] + [(ya, 0), (yc, 0), (ys, 0)]
    return _mm_epi(dx1b, [w_out], "nt", extras, [bf16] * 6, 3, fn, name)


def _rows(body, ins, outs, *, tm, name, vmem=40 << 20):
    nrows = next(s[1].shape[0] for s in ins if s[0] == "r")
    in_specs, args = [], []
    for s in ins:
        arr = s[1]
        if s[0] == "r":
            w = s[2] if len(s) > 2 else arr.shape[1]
            cb = s[3] if len(s) > 3 else 0
            in_specs.append(pl.BlockSpec((tm, w), functools.partial(lambda i, cb: (i, cb), cb=cb)))
        else:
            in_specs.append(pl.BlockSpec(arr.shape, functools.partial(lambda i, nd: (0,) * nd, nd=arr.ndim)))
        args.append(arr)
    out_specs, out_shape = [], []
    for s in outs:
        if s[0] == "r":
            out_specs.append(pl.BlockSpec((tm, s[1]), lambda i: (i, 0)))
            out_shape.append(jax.ShapeDtypeStruct((nrows, s[1]), s[2]))
        else:
            out_specs.append(pl.BlockSpec(s[1], functools.partial(lambda i, nd: (0,) * nd, nd=len(s[1]))))
            out_shape.append(jax.ShapeDtypeStruct(s[1], s[2]))
    return _call(body, name=name, grid=(nrows // tm,), in_specs=in_specs, out_specs=out_specs,
                 out_shape=out_shape, compiler_params=_params(("arbitrary",), vmem))(*args)


def _accumulate(ref, part):
    i = pl.program_id(0)

    @pl.when(i == 0)
    def _():
        ref[...] = part

    @pl.when(i > 0)
    def _():
        ref[...] += part


def _rms_stats(x):
    r = lax.rsqrt(jnp.mean(x * x, axis=-1, keepdims=True) + 1e-6)
    return r, x * r


def _rms_fwd(x, g, name):
    def body(x_ref, g_ref, o_ref):
        _, xn = _rms_stats(x_ref[...])
        o_ref[...] = (xn * g_ref[...]).astype(o_ref.dtype)

    return _rows(body, [("r", x), ("f", g)], [("r", x.shape[1], bf16)], tm=min(256, x.shape[0]), name=name)[0]


def _rms_bwd(x, g, dh, dres, name):
    D = x.shape[1]

    def body(x_ref, g_ref, dh_ref, dr_ref, dx_ref, dxb_ref, dg_ref):
        r, xn = _rms_stats(x_ref[...])
        dy = dh_ref[...].astype(f32)
        dxn = dy * g_ref[...]
        dx = dr_ref[...] + r * (dxn - xn * jnp.mean(dxn * xn, axis=-1, keepdims=True))
        dx_ref[...] = dx
        dxb_ref[...] = dx.astype(bf16)
        _accumulate(dg_ref, jnp.sum(dy * xn, axis=0, keepdims=True))

    return _rows(body, [("r", x), ("f", g), ("r", dh), ("r", dres)],
                 [("r", D, f32), ("r", D, bf16), ("a", (1, D), f32)], tm=min(256, x.shape[0]), name=name)


def _final_loss(x, g, tgt, name):
    D = x.shape[1]

    def body(x_ref, g_ref, t_ref, dx_ref, dxb_ref, sq_ref, dg_ref):
        r, xn = _rms_stats(x_ref[...])
        gain = g_ref[...]
        diff = xn * gain - t_ref[...]
        dy = diff * (1.0 / D)
        dxn = dy * gain
        dx = r * (dxn - xn * jnp.mean(dxn * xn, axis=-1, keepdims=True))
        dx_ref[...] = dx
        dxb_ref[...] = dx.astype(bf16)
        _accumulate(sq_ref, jnp.sum(diff * diff, axis=0, keepdims=True))
        _accumulate(dg_ref, jnp.sum(dy * xn, axis=0, keepdims=True))

    return _rows(body, [("r", x), ("f", g), ("r", tgt)],
                 [("r", D, f32), ("r", D, bf16), ("a", (1, D), f32), ("a", (1, D), f32)],
                 tm=min(256, x.shape[0]), name=name)


def _qk_fwd(q_raw, kv_raw, qg, kg, cos2, sin2, name):
    def body(q_ref, k_ref, qg_ref, kg_ref, c_ref, s_ref, qo_ref, ko_ref):
        c, s = c_ref[...], s_ref[...]

        def head(src, gain, dst, h):
            cols = slice(h * HEAD_DIM, (h + 1) * HEAD_DIM)
            _, xn = _rms_stats(src[:, cols].astype(f32))
            y = xn * gain
            dst[:, cols] = (y * c + pltpu.roll(y, HEAD_DIM // 2, 1) * s).astype(dst.dtype)

        for h in range(N_Q):
            head(q_ref, qg_ref[...], qo_ref, h)
        for h in range(N_KV):
            head(k_ref, kg_ref[...], ko_ref, h)

    return _rows(body, [("r", q_raw), ("r", kv_raw, KV_COLS, 0), ("f", qg), ("f", kg), ("r", cos2), ("r", sin2)],
                 [("r", Q_COLS, bf16), ("r", KV_COLS, bf16)], tm=min(256, q_raw.shape[0]), name=name)


def _qk_bwd(q_raw, kv_raw, dqr, dkr, qg, kg, cos2, sin2, name):
    def body(q_ref, k_ref, dq_ref, dk_ref, qg_ref, kg_ref, c_ref, s_ref, dqo_ref, dko_ref, dqg_ref, dkg_ref):
        c, s = c_ref[...], s_ref[...]

        def head(src, dsrc, gain, dst, h):
            cols = slice(h * HEAD_DIM, (h + 1) * HEAD_DIM)
            r, xn = _rms_stats(src[:, cols].astype(f32))
            do = dsrc[:, cols].astype(f32)
            dy = do * c + pltpu.roll(do * s, HEAD_DIM // 2, 1)
            dxn = dy * gain
            dst[:, cols] = (r * (dxn - xn * jnp.mean(dxn * xn, axis=-1, keepdims=True))).astype(dst.dtype)
            return jnp.sum(dy * xn, axis=0, keepdims=True)

        dq_gain = head(q_ref, dq_ref, qg_ref[...], dqo_ref, 0)
        for h in range(1, N_Q):
            dq_gain = dq_gain + head(q_ref, dq_ref, qg_ref[...], dqo_ref, h)
        dk_gain = head(k_ref, dk_ref, kg_ref[...], dko_ref, 0)
        for h in range(1, N_KV):
            dk_gain = dk_gain + head(k_ref, dk_ref, kg_ref[...], dko_ref, h)
        _accumulate(dqg_ref, dq_gain)
        _accumulate(dkg_ref, dk_gain)

    return _rows(body, [("r", q_raw), ("r", kv_raw, KV_COLS, 0), ("r", dqr), ("r", dkr), ("f", qg), ("f", kg),
                        ("r", cos2), ("r", sin2)],
                 [("r", Q_COLS, bf16), ("r", KV_COLS, bf16), ("a", (1, HEAD_DIM), f32), ("a", (1, HEAD_DIM), f32)],
                 tm=min(256, q_raw.shape[0]), name=name)


def _softmax_rows(q, k):
    s = lax.dot_general(q, k, (((1,), (1,)), ((), ())), preferred_element_type=f32) * SCALE
    p = jnp.exp(s - jnp.max(s, axis=-1, keepdims=True))
    return p * (1.0 / jnp.sum(p, axis=-1, keepdims=True))


def _head_cols(g):
    return slice(g * HEAD_DIM, (g + 1) * HEAD_DIM)


def _attn_fwd(qr, kr, kv_raw, name):
    S = qr.shape[0]
    tq = min(256, S)

    def body(q_ref, k_ref, v_ref, o_ref):
        k, v = k_ref[...], v_ref[...]
        for g in range(GRP):
            p = _softmax_rows(q_ref[:, _head_cols(g)], k)
            o_ref[:, _head_cols(g)] = jnp.dot(p.astype(bf16), v, preferred_element_type=f32).astype(o_ref.dtype)

    return _call(
        body, name=name, grid=(N_KV, S // tq),
        in_specs=[pl.BlockSpec((tq, GRP * HEAD_DIM), lambda kv, i: (i, kv)),
                  pl.BlockSpec((S, HEAD_DIM), lambda kv, i: (0, kv)),
                  pl.BlockSpec((S, HEAD_DIM), lambda kv, i: (0, N_KV + kv))],
        out_specs=pl.BlockSpec((tq, GRP * HEAD_DIM), lambda kv, i: (i, kv)),
        out_shape=jax.ShapeDtypeStruct((S, Q_COLS), bf16),
        compiler_params=_params(("parallel", "arbitrary"), 4 * GRP * tq * S * 4 + (8 << 20)),
    )(qr, kr, kv_raw)


def _attn_bwd(qr, kr, kv_raw, do, name):
    S = qr.shape[0]
    tq = min(256, S)

    def body(q_ref, k_ref, v_ref, do_ref, dq_ref, dk_ref, dv_ref):
        first = pl.program_id(1) == 0
        k, v = k_ref[...], v_ref[...]
        dv_part = dk_part = None
        for g in range(GRP):
            q, do_ = q_ref[:, _head_cols(g)], do_ref[:, _head_cols(g)]
            p = _softmax_rows(q, k)
            dp = lax.dot_general(do_, v, (((1,), (1,)), ((), ())), preferred_element_type=f32)
            ds = (p * (dp - jnp.sum(dp * p, axis=-1, keepdims=True)) * SCALE).astype(bf16)
            dq_ref[:, _head_cols(g)] = jnp.dot(ds, k, preferred_element_type=f32).astype(dq_ref.dtype)
            dv_g = lax.dot_general(p.astype(bf16), do_, (((0,), (0,)), ((), ())), preferred_element_type=f32)
            dk_g = lax.dot_general(ds, q, (((0,), (0,)), ((), ())), preferred_element_type=f32)
            dv_part = dv_g if g == 0 else dv_part + dv_g
            dk_part = dk_g if g == 0 else dk_part + dk_g

        @pl.when(first)
        def _():
            dv_ref[...] = dv_part
            dk_ref[...] = dk_part

        @pl.when(jnp.logical_not(first))
        def _():
            dv_ref[...] += dv_part
            dk_ref[...] += dk_part

    qspec = pl.BlockSpec((tq, GRP * HEAD_DIM), lambda kv, i: (i, kv))
    return _call(
        body, name=name, grid=(N_KV, S // tq),
        in_specs=[qspec, pl.BlockSpec((S, HEAD_DIM), lambda kv, i: (0, kv)),
                  pl.BlockSpec((S, HEAD_DIM), lambda kv, i: (0, N_KV + kv)), qspec],
        out_specs=[qspec, pl.BlockSpec((S, HEAD_DIM), lambda kv, i: (0, kv)),
                   pl.BlockSpec((S, HEAD_DIM), lambda kv, i: (0, kv))],
        out_shape=[jax.ShapeDtypeStruct((S, Q_COLS), bf16), jax.ShapeDtypeStruct((S, KV_COLS), f32),
                   jax.ShapeDtypeStruct((S, KV_COLS), f32)],
        compiler_params=_params(("parallel", "arbitrary"), 6 * GRP * tq * S * 4 + (8 << 20)),
    )(qr, kr, kv_raw, do)


CONV_HALO = 16


def _fill_padded(pad_ref, val, S):
    pad_ref[pl.ds(0, CONV_HALO), :] = jnp.zeros((CONV_HALO, LANES), f32)
    pad_ref[pl.ds(CONV_HALO + S, CONV_HALO), :] = jnp.zeros((CONV_HALO, LANES), f32)
    pad_ref[pl.ds(CONV_HALO, S), :] = val


def _group_specs(S, n_groups, second_half):
    return pl.BlockSpec((S, LANES), functools.partial(lambda g, o: (0, g + o), o=n_groups if second_half else 0))


def _conv1_fwd(conv_in, wdw, b_dw, name):
    S = conv_in.shape[0]
    ng = CONV_CH // LANES
    R = min(256, S)

    def body(a_ref, g_ref, w_ref, b_ref, o_ref, pad_ref):
        z = a_ref[...].astype(f32) * jax.nn.sigmoid(g_ref[...].astype(f32))
        _fill_padded(pad_ref, z, S)
        for r in range(S // R):
            acc = jnp.zeros((R, LANES), f32) + b_ref[...]
            for j in range(CONV_W):
                acc = acc + w_ref[pl.ds(j, 1), :] * pad_ref[pl.ds(r * R + CONV_HALO - CONV_PAD + j, R), :]
            o_ref[pl.ds(r * R, R), :] = acc

    return _call(
        body, name=name, grid=(ng,),
        in_specs=[_group_specs(S, ng, False), _group_specs(S, ng, True),
                  pl.BlockSpec((CONV_WP, LANES), lambda g: (g, 0)), pl.BlockSpec((1, LANES), lambda g: (0, g))],
        out_specs=pl.BlockSpec((S, LANES), lambda g: (0, g)),
        out_shape=jax.ShapeDtypeStruct((S, CONV_CH), f32),
        scratch_shapes=[pltpu.VMEM((S + 2 * CONV_HALO, LANES), f32)],
        compiler_params=_params(("parallel",), 24 << 20),
    )(conv_in, conv_in, wdw, b_dw)


def _conv1_bwd(conv_in, dc, wdw, name):
    S = conv_in.shape[0]
    ng = CONV_CH // LANES
    R = min(256, S)

    def body(a_ref, g_ref, w_ref, dc_ref, da_ref, dg_ref, dw_ref, db_ref, padz_ref, padd_ref):
        a = a_ref[...].astype(f32)
        sg = jax.nn.sigmoid(g_ref[...].astype(f32))
        _fill_padded(padz_ref, a * sg, S)
        _fill_padded(padd_ref, dc_ref[...], S)
        for r in range(S // R):
            dz = jnp.zeros((R, LANES), f32)
            for j in range(CONV_W):
                dz = dz + w_ref[pl.ds(j, 1), :] * padd_ref[pl.ds(r * R + CONV_HALO + CONV_PAD - j, R), :]
            rows = pl.ds(r * R, R)
            ar, sr = a_ref[rows, :].astype(f32), jax.nn.sigmoid(g_ref[rows, :].astype(f32))
            da_ref[rows, :] = (dz * sr).astype(da_ref.dtype)
            dg_ref[rows, :] = (dz * ar * sr * (1.0 - sr)).astype(dg_ref.dtype)
        for j in range(CONV_W):
            tot = jnp.zeros((1, LANES), f32)
            for r in range(S // R):
                tot = tot + jnp.sum(dc_ref[pl.ds(r * R, R), :] * padz_ref[pl.ds(r * R + CONV_HALO - CONV_PAD + j, R), :],
                                    axis=0, keepdims=True)
            dw_ref[pl.ds(j, 1), :] = tot
        dw_ref[pl.ds(CONV_W, CONV_WP - CONV_W), :] = jnp.zeros((CONV_WP - CONV_W, LANES), f32)
        db_ref[...] = jnp.sum(dc_ref[...], axis=0, keepdims=True)

    return _call(
        body, name=name, grid=(ng,),
        in_specs=[_group_specs(S, ng, False), _group_specs(S, ng, True),
                  pl.BlockSpec((CONV_WP, LANES), lambda g: (g, 0)), pl.BlockSpec((S, LANES), lambda g: (0, g))],
        out_specs=[pl.BlockSpec((S, LANES), lambda g: (0, g)), pl.BlockSpec((S, LANES), lambda g: (0, g)),
                   pl.BlockSpec((CONV_WP, LANES), lambda g: (g, 0)), pl.BlockSpec((1, LANES), lambda g: (0, g))],
        out_shape=[jax.ShapeDtypeStruct((S, CONV_CH), bf16), jax.ShapeDtypeStruct((S, CONV_CH), bf16),
                   jax.ShapeDtypeStruct((ng * CONV_WP, LANES), f32), jax.ShapeDtypeStruct((1, CONV_CH), f32)],
        scratch_shapes=[pltpu.VMEM((S + 2 * CONV_HALO, LANES), f32), pltpu.VMEM((S + 2 * CONV_HALO, LANES), f32)],
        compiler_params=_params(("parallel",), 24 << 20),
    )(conv_in, conv_in, wdw, dc)


def _ln_stats(x, eps=1e-5):
    xc = x - jnp.mean(x, axis=-1, keepdims=True)
    r = lax.rsqrt(jnp.mean(xc * xc, axis=-1, keepdims=True) + eps)
    return r, xc * r


def _ln_bwd(r, xh, dxh):
    return r * (dxh - jnp.mean(dxh, axis=-1, keepdims=True) - xh * jnp.mean(dxh * xh, axis=-1, keepdims=True))


def _conv2_fwd(c, ln_g, ln_b, name):
    def body(c_ref, g_ref, b_ref, o_ref):
        _, xh = _ln_stats(c_ref[...])
        y = xh * g_ref[...] + b_ref[...]
        o_ref[...] = (y * jax.nn.sigmoid(y)).astype(o_ref.dtype)

    return _rows(body, [("r", c), ("f", ln_g), ("f", ln_b)], [("r", CONV_CH, bf16)], tm=min(256, c.shape[0]), name=name)[0]


def _conv2_bwd(c, dcz, ln_g, ln_b, name):
    def body(c_ref, d_ref, g_ref, b_ref, dc_ref, dg_ref, db_ref):
        r, xh = _ln_stats(c_ref[...])
        y = xh * g_ref[...] + b_ref[...]
        sg = jax.nn.sigmoid(y)
        dy = d_ref[...].astype(f32) * (sg * (1.0 + y * (1.0 - sg)))
        dc_ref[...] = _ln_bwd(r, xh, dy * g_ref[...])
        _accumulate(dg_ref, jnp.sum(dy * xh, axis=0, keepdims=True))
        _accumulate(db_ref, jnp.sum(dy, axis=0, keepdims=True))

    return _rows(body, [("r", c), ("r", dcz), ("f", ln_g), ("f", ln_b)],
                 [("r", CONV_CH, f32), ("a", (1, CONV_CH), f32), ("a", (1, CONV_CH), f32)],
                 tm=min(256, c.shape[0]), name=name)


GELU_K = math.sqrt(2.0 / math.pi)
GELU_C = 0.044715


def _gelu(x):
    return 0.5 * x * (1.0 + jnp.tanh(GELU_K * (x + GELU_C * x * x * x)))


def _gelu_grad(x):
    th = jnp.tanh(GELU_K * (x + GELU_C * x * x * x))
    return 0.5 * (1.0 + th) + 0.5 * x * (1.0 - th * th) * (GELU_K * (1.0 + 3.0 * GELU_C * x * x))


def _chunk_rows(n):
    return pl.ds(pl.multiple_of(n * SG_CHUNK, SG_CHUNK), SG_CHUNK)


def _sgu_fwd(sg_in, ln_g, ln_b, w_s, b_s, name):
    S = sg_in.shape[0]

    def body(u_ref, v_ref, lg_ref, lb_ref, w_ref, b_ref, o_ref):
        wb = w_ref[...].astype(bf16)

        def chunk(n, carry):
            rows = _chunk_rows(n)
            gu = _gelu(u_ref[rows, :].astype(f32))
            _, xh = _ln_stats(_gelu(v_ref[rows, :].astype(f32)))
            vl = xh * lg_ref[...] + lb_ref[...]
            t = jnp.dot(wb, vl.astype(bf16), preferred_element_type=f32) + b_ref[...]
            o_ref[rows, :] = (gu * t).astype(o_ref.dtype)
            return carry

        lax.fori_loop(0, S // SG_CHUNK, chunk, 0, unroll=2)

    return _call(
        body, name=name, grid=(SG_G,),
        in_specs=[_group_specs(S, SG_G, False), _group_specs(S, SG_G, True),
                  pl.BlockSpec((1, LANES), lambda g: (0, g)), pl.BlockSpec((1, LANES), lambda g: (0, g)),
                  pl.BlockSpec((None, SG_CHUNK, SG_CHUNK), lambda g: (g, 0, 0)),
                  pl.BlockSpec((None, SG_CHUNK, 1), lambda g: (g, 0, 0))],
        out_specs=pl.BlockSpec((S, LANES), lambda g: (0, g)),
        out_shape=jax.ShapeDtypeStruct((S, SG_CH), bf16),
        compiler_params=_params(("parallel",), 24 << 20),
    )(sg_in, sg_in, ln_g, ln_b, w_s, b_s)


def _sgu_bwd(sg_in, dsz, ln_g, ln_b, w_s, w_s_t, b_s, name):
    S = sg_in.shape[0]

    def body(u_ref, v_ref, lg_ref, lb_ref, w_ref, wt_ref, b_ref, d_ref, du_ref, dv_ref, dw_ref, db_ref, dlg_ref, dlb_ref):
        wb = w_ref[...].astype(bf16)
        wtb = wt_ref[...].astype(bf16)

        def chunk(n, carry):
            dwa, dba, dlga, dlba = carry
            rows = _chunk_rows(n)
            u = u_ref[rows, :].astype(f32)
            v = v_ref[rows, :].astype(f32)
            gu = _gelu(u)
            r, xh = _ln_stats(_gelu(v))
            vlb = (xh * lg_ref[...] + lb_ref[...]).astype(bf16)
            t = jnp.dot(wb, vlb, preferred_element_type=f32) + b_ref[...]
            d = d_ref[rows, :].astype(f32)
            dt = d * gu
            dtb = dt.astype(bf16)
            dwa = dwa + lax.dot_general(dtb, vlb, (((1,), (1,)), ((), ())), preferred_element_type=f32)
            dba = dba + jnp.sum(dt, axis=1, keepdims=True)
            dvl = jnp.dot(wtb, dtb, preferred_element_type=f32)
            dlga = dlga + jnp.sum(dvl * xh, axis=0, keepdims=True)
            dlba = dlba + jnp.sum(dvl, axis=0, keepdims=True)
            dgv = _ln_bwd(r, xh, dvl * lg_ref[...])
            du_ref[rows, :] = (d * t * _gelu_grad(u)).astype(du_ref.dtype)
            dv_ref[rows, :] = (dgv * _gelu_grad(v)).astype(dv_ref.dtype)
            return dwa, dba, dlga, dlba

        init = (jnp.zeros((SG_CHUNK, SG_CHUNK), f32), jnp.zeros((SG_CHUNK, 1), f32),
                jnp.zeros((1, LANES), f32), jnp.zeros((1, LANES), f32))
        dwa, dba, dlga, dlba = lax.fori_loop(0, S // SG_CHUNK, chunk, init, unroll=2)
        dw_ref[...] = dwa
        db_ref[...] = dba
        dlg_ref[...] = dlga
        dlb_ref[...] = dlba

    wspec = pl.BlockSpec((None, SG_CHUNK, SG_CHUNK), lambda g: (g, 0, 0))
    bspec = pl.BlockSpec((None, SG_CHUNK, 1), lambda g: (g, 0, 0))
    lspec = pl.BlockSpec((1, LANES), lambda g: (0, g))
    cspec = pl.BlockSpec((S, LANES), lambda g: (0, g))
    return _call(
        body, name=name, grid=(SG_G,),
        in_specs=[_group_specs(S, SG_G, False), _group_specs(S, SG_G, True), lspec, lspec, wspec, wspec, bspec, cspec],
        out_specs=[cspec, cspec, wspec, bspec, lspec, lspec],
        out_shape=[jax.ShapeDtypeStruct((S, SG_CH), bf16), jax.ShapeDtypeStruct((S, SG_CH), bf16),
                   jax.ShapeDtypeStruct((SG_G, SG_CHUNK, SG_CHUNK), f32), jax.ShapeDtypeStruct((SG_G, SG_CHUNK, 1), f32),
                   jax.ShapeDtypeStruct((1, SG_CH), f32), jax.ShapeDtypeStruct((1, SG_CH), f32)],
        compiler_params=_params(("parallel",), 24 << 20),
    )(sg_in, sg_in, ln_g, ln_b, w_s, w_s_t, b_s, dsz)


def _merge_fwd(gl, b_gate, ya, yc, ys, name):
    D = ya.shape[1]

    def body(gl_ref, b_ref, ya_ref, yc_ref, ys_ref, o_ref):
        acc = jnp.zeros(o_ref.shape, f32)
        for i, y_ref in enumerate((ya_ref, yc_ref, ys_ref)):
            cols = slice(i * D, (i + 1) * D)
            acc = acc + jax.nn.sigmoid(gl_ref[:, cols].astype(f32) + b_ref[:, cols]) * y_ref[...].astype(f32)
        o_ref[...] = acc.astype(o_ref.dtype)

    return _rows(body, [("r", gl), ("f", b_gate), ("r", ya), ("r", yc), ("r", ys)], [("r", D, bf16)],
                 tm=min(128, gl.shape[0]), name=name)[0]


def _merge_bwd(dm, gl, b_gate, ya, yc, ys, name):
    D = ya.shape[1]

    def body(dm_ref, gl_ref, b_ref, ya_ref, yc_ref, ys_ref, dgl_ref, dya_ref, dyc_ref, dys_ref, db_ref):
        dm_ = dm_ref[...].astype(f32)
        for i, (y_ref, dy_ref) in enumerate(((ya_ref, dya_ref), (yc_ref, dyc_ref), (ys_ref, dys_ref))):
            cols = slice(i * D, (i + 1) * D)
            gate = jax.nn.sigmoid(gl_ref[:, cols].astype(f32) + b_ref[:, cols])
            dy_ref[...] = (dm_ * gate).astype(dy_ref.dtype)
            dlog = dm_ * y_ref[...].astype(f32) * gate * (1.0 - gate)
            dgl_ref[:, cols] = dlog.astype(dgl_ref.dtype)
            part = jnp.sum(dlog, axis=0, keepdims=True)
            first = pl.program_id(0) == 0

            @pl.when(first)
            def _():
                db_ref[:, cols] = part

            @pl.when(jnp.logical_not(first))
            def _():
                db_ref[:, cols] += part

    return _rows(body, [("r", dm), ("r", gl), ("f", b_gate), ("r", ya), ("r", yc), ("r", ys)],
                 [("r", 3 * D, bf16), ("r", D, bf16), ("r", D, bf16), ("r", D, bf16), ("a", (1, 3 * D), f32)],
                 tm=min(128, gl.shape[0]), name=name)


def _swiglu_fwd(fg, fu, name):
    def body(g_ref, u_ref, o_ref):
        g = g_ref[...].astype(f32)
        o_ref[...] = (g * jax.nn.sigmoid(g) * u_ref[...].astype(f32)).astype(o_ref.dtype)

    return _rows(body, [("r", fg), ("r", fu)], [("r", fg.shape[1], bf16)], tm=min(128, fg.shape[0]), name=name)[0]


def _swiglu_bwd(dact, fg, fu, name):
    def body(d_ref, g_ref, u_ref, dg_ref, du_ref):
        d = d_ref[...].astype(f32)
        g = g_ref[...].astype(f32)
        sg = jax.nn.sigmoid(g)
        dg_ref[...] = (d * u_ref[...].astype(f32) * sg * (1.0 + g * (1.0 - sg))).astype(dg_ref.dtype)
        du_ref[...] = (d * g * sg).astype(du_ref.dtype)

    return _rows(body, [("r", dact), ("r", fg), ("r", fu)], [("r", fg.shape[1], bf16), ("r", fg.shape[1], bf16)],
                 tm=min(128, fg.shape[0]), name=name)


def _row_tile(r, c, n_arrays, itemsize=4):
    fits = [tm for tm in range(16, r + 1, 16) if r % tm == 0 and 2 * n_arrays * tm * c * itemsize <= (24 << 20)]
    return fits[-1] if fits else r


def _sum_slots(slots, name):
    n, r, c = slots.shape
    tm = _row_tile(r, c, n + 2)

    def body(s_ref, o_ref):
        acc = s_ref[0].astype(f32)
        for k in range(1, n):
            acc = acc + s_ref[k].astype(f32)
        o_ref[...] = acc

    return _call(body, name=name, grid=(r // tm,),
                 in_specs=[pl.BlockSpec((n, tm, c), lambda i: (0, i, 0))],
                 out_specs=pl.BlockSpec((tm, c), lambda i: (i, 0)),
                 out_shape=jax.ShapeDtypeStruct((r, c), f32),
                 compiler_params=_params(("parallel",), 40 << 20))(slots)


def _add_sibling(g4, recv, core, name):
    _, _, r, c = g4.shape
    tm = _row_tile(r, c, 3, 2)

    def body(core_ref, g_ref, r_ref, o_ref):
        o_ref[...] = (g_ref[...].astype(f32) + r_ref[...].astype(f32)).astype(o_ref.dtype)

    grid_spec = pltpu.PrefetchScalarGridSpec(
        num_scalar_prefetch=1, grid=(N_CHIP, r // tm),
        in_specs=[pl.BlockSpec((None, None, tm, c), lambda k, i, core_ref: (k, core_ref[0], i, 0)),
                  pl.BlockSpec((None, tm, c), lambda k, i, core_ref: (k, i, 0))],
        out_specs=pl.BlockSpec((None, tm, c), lambda k, i, core_ref: (k, i, 0)))
    return _call(body, name=name, grid_spec=grid_spec, out_shape=jax.ShapeDtypeStruct((N_CHIP, r, c), bf16),
                 compiler_params=_params(("parallel", "parallel"), 40 << 20))(core, g4, recv)


def _adamw(w, g, m, v, name):
    L, r, c = w.shape
    tm = _row_tile(r, c, 7)
    c1 = 1.0 - ADAM_B1 ** ADAM_STEP
    c2 = 1.0 - ADAM_B2 ** ADAM_STEP

    def body(w_ref, g_ref, m_ref, v_ref, d_ref, mo_ref, vo_ref):
        g_ = g_ref[...]
        m_ = ADAM_B1 * m_ref[...] + (1.0 - ADAM_B1) * g_
        v_ = ADAM_B2 * v_ref[...] + (1.0 - ADAM_B2) * (g_ * g_)
        d_ref[...] = -ADAM_LR * ((m_ / c1) / (jnp.sqrt(v_ / c2) + ADAM_EPS) + ADAM_WD * w_ref[...])
        mo_ref[...] = m_
        vo_ref[...] = v_

    spec = pl.BlockSpec((None, tm, c), lambda l, i: (l, i, 0))
    shp = jax.ShapeDtypeStruct((L, r, c), f32)
    return _call(body, name=name, grid=(L, r // tm), in_specs=[spec] * 4, out_specs=[spec] * 3,
                 out_shape=[shp] * 3, compiler_params=_params(("parallel", "parallel"), 40 << 20))(w, g, m, v)


def _mesh_pos():
    return lax.axis_index("x"), lax.axis_index("y"), lax.axis_index("c")


def _all_gather(shards, after, name):
    n = len(shards)

    def body(*refs):
        x_refs, o_refs = refs[:n], refs[n + 1:2 * n + 1]
        send_sems, recv_sems, local_sems = refs[2 * n + 1:]
        x, y, c = _mesh_pos()
        me, sibling = (x, y, c), (x, y, 1 - c)
        chips = [(1 - x, y), (x, 1 - y), (1 - x, 1 - y)]

        def rows(k, px, py, pc):
            return o_refs[k].at[4 * px + 2 * py + pc]

        def copy(k, s, block, to, src=None):
            return pltpu.make_async_remote_copy(
                src_ref=rows(k, *block) if src is None else src, dst_ref=rows(k, *block),
                send_sem=send_sems.at[k, s], recv_sem=recv_sems.at[k, s], device_id=to, device_id_type=MESH)

        mine = [pltpu.make_async_copy(x_refs[k], rows(k, *me), local_sems.at[k]) for k in range(n)]
        for cp in mine:
            cp.start()
        first = [copy(k, 0, me, sibling, src=x_refs[k]) for k in range(n)]
        for j, chip in enumerate(chips):
            first += [copy(k, 1 + j, me, (*chip, c), src=x_refs[k]) for k in range(n)]
        for cp in first:
            cp.start()
        passed = []
        for j, chip in enumerate(chips):
            for k in range(n):
                copy(k, 1 + j, (*chip, c), me).wait_recv()
                fwd = copy(k, 4 + j, (*chip, c), sibling)
                fwd.start()
                passed.append(fwd)
        for k in range(n):
            copy(k, 0, sibling, me).wait_recv()
        for j, chip in enumerate(chips):
            for k in range(n):
                copy(k, 4 + j, (*chip, 1 - c), me).wait_recv()
        for cp in first + passed:
            cp.wait_send()
        for cp in mine:
            cp.wait()

    return _call(
        body, name=name, in_specs=[HBM] * n + [ANY], out_specs=[HBM] * n,
        out_shape=[jax.ShapeDtypeStruct((N_DEV,) + s.shape, s.dtype) for s in shards],
        scratch_shapes=[pltpu.SemaphoreType.DMA((n, 7)), pltpu.SemaphoreType.DMA((n, 7)), pltpu.SemaphoreType.DMA((n,))],
    )(*shards, after)


def _send_to_sibling(g4s, name):
    n = len(g4s)

    def body(*refs):
        g_refs, o_refs = refs[:n], refs[n:2 * n]
        send_sems, recv_sems = refs[2 * n:]
        x, y, c = _mesh_pos()
        copies = [pltpu.make_async_remote_copy(
            src_ref=g_refs[k].at[:, 1 - c], dst_ref=o_refs[k], send_sem=send_sems.at[k], recv_sem=recv_sems.at[k],
            device_id=(x, y, 1 - c), device_id_type=MESH) for k in range(n)]
        for cp in copies:
            cp.start()
        for cp in copies:
            cp.wait()

    return _call(
        body, name=name, in_specs=[HBM] * n, out_specs=[HBM] * n,
        out_shape=[jax.ShapeDtypeStruct((N_CHIP,) + g.shape[2:], g.dtype) for g in g4s],
        scratch_shapes=[pltpu.SemaphoreType.DMA((n,)), pltpu.SemaphoreType.DMA((n,))],
    )(*g4s)


SEM = pl.BlockSpec(memory_space=pltpu.SEMAPHORE)
ANY = pl.BlockSpec(memory_space=pl.ANY)
EFFECT = pltpu.SideEffectType.DATAFLOW_SIDE_EFFECTING


def _other_chips(x, y):
    return [(1 - x, y), (x, 1 - y), (1 - x, 1 - y)]


def _ici_copy(kind, src_ref, land_ref, send_sem, recv_sem, sender, target, c):
    (sx, sy), (tx, ty) = sender, target
    if kind == "gather":
        src, dst = src_ref, land_ref.at[4 * sx + 2 * sy + c]
    else:
        src, dst = src_ref.at[2 * tx + ty], land_ref.at[2 * sx + sy]
    return pltpu.make_async_remote_copy(src_ref=src, dst_ref=dst, send_sem=send_sem, recv_sem=recv_sem,
                                        device_id=(tx, ty, c), device_id_type=MESH)


def _ici_start(kind, srcs, lands, after, name):
    n = len(srcs)

    def body(*refs):
        src_refs, land_refs = refs[:n], refs[n:2 * n]
        send_sems, recv_sems = refs[2 * n + 1], refs[2 * n + 2]
        token = refs[-1]
        x, y, c = _mesh_pos()
        for j, chip in enumerate(_other_chips(x, y)):
            for k in range(n):
                _ici_copy(kind, src_refs[k], land_refs[k], send_sems.at[3 * k + j], recv_sems.at[3 * k + j],(x, y), chip, c).start()
        token[...] = jnp.zeros_like(token)

    bufs = list(srcs) + list(lands)
    return _call(
        body, name=name,
        out_shape=(pltpu.SemaphoreType.DMA((3 * n,)), pltpu.SemaphoreType.DMA((3 * n,)),
                   *[pltpu.HBM(b.shape, b.dtype) for b in bufs], jax.ShapeDtypeStruct((8, LANES), f32)),
        in_specs=[HBM] * (2 * n) + [ANY], out_specs=(SEM, SEM, *[HBM] * (2 * n), pl.BlockSpec(memory_space=pltpu.VMEM)),
        input_output_aliases={i: 2 + i for i in range(2 * n)},
        compiler_params=pltpu.CompilerParams(has_side_effects=EFFECT),
    )(*[pltpu.with_memory_space_constraint(b, pltpu.HBM) for b in bufs], after)


def _ici_wait(kind, started, after, name):
    send_sems, recv_sems, *bufs = started[:-1]
    n = len(bufs) // 2

    def body(*refs):
        src_refs, land_refs = refs[:n], refs[n:2 * n]
        send_sems, recv_sems = refs[2 * n], refs[2 * n + 1]
        x, y, c = _mesh_pos()
        for j, chip in enumerate(_other_chips(x, y)):
            for k in range(n):
                _ici_copy(kind, src_refs[k], land_refs[k], send_sems.at[3 * k + j], recv_sems.at[3 * k + j],(x, y), chip, c).wait_send()
                _ici_copy(kind, src_refs[k], land_refs[k], send_sems.at[3 * k + j], recv_sems.at[3 * k + j],chip, (x, y), c).wait_recv()

    out = _call(
        body, name=name, out_shape=[pltpu.HBM(b.shape, b.dtype) for b in bufs],
        in_specs=[HBM] * (2 * n) + [SEM, SEM, ANY], out_specs=[HBM] * (2 * n),
        input_output_aliases={i: i for i in range(2 * n)},
        compiler_params=pltpu.CompilerParams(has_side_effects=EFFECT),
    )(*bufs, send_sems, recv_sems, after)
    return out[:n], out[n:]


def _d2d_gather(lands, after, name):
    n = len(lands)

    def body(*refs):
        in_refs, o_refs = refs[:n], refs[n + 1:2 * n + 1]
        send_sems, recv_sems = refs[2 * n + 1:]
        x, y, c = _mesh_pos()
        copies = [pltpu.make_async_remote_copy(
            src_ref=in_refs[k].at[:, c], dst_ref=o_refs[k].at[:, c], send_sem=send_sems.at[k], recv_sem=recv_sems.at[k],
            device_id=(x, y, 1 - c), device_id_type=MESH) for k in range(n)]
        for cp in copies:
            cp.start()
        for k, cp in enumerate(copies):
            cp.wait_send()
            pltpu.make_async_remote_copy(
                src_ref=in_refs[k].at[:, c], dst_ref=o_refs[k].at[:, 1 - c], send_sem=send_sems.at[k],
                recv_sem=recv_sems.at[k], device_id=(x, y, 1 - c), device_id_type=MESH).wait_recv()

    return _call(
        body, name=name, in_specs=[HBM] * n + [ANY], out_specs=[HBM] * n,
        out_shape=[jax.ShapeDtypeStruct(b.shape, b.dtype) for b in lands],
        input_output_aliases={k: k for k in range(n)},
        scratch_shapes=[pltpu.SemaphoreType.DMA((n,)), pltpu.SemaphoreType.DMA((n,))],
    )(*lands, after)


def _sum_chip_slots(lands, sums, chip, name):
    _, r, c = lands.shape
    tm = _row_tile(r, c, 10, 2)

    def body(chip_ref, l_ref, s_ref, o_ref):
        acc = None
        for k in range(N_CHIP):
            part = jnp.where(chip_ref[0] == k, s_ref[k], l_ref[k]).astype(f32)
            acc = part if acc is None else acc + part
        o_ref[...] = acc

    grid_spec = pltpu.PrefetchScalarGridSpec(
        num_scalar_prefetch=1, grid=(r // tm,),
        in_specs=[pl.BlockSpec((N_CHIP, tm, c), lambda i, chip_ref: (0, i, 0)),
                  pl.BlockSpec((N_CHIP, tm, c), lambda i, chip_ref: (0, i, 0))],
        out_specs=pl.BlockSpec((tm, c), lambda i, chip_ref: (i, 0)))
    return _call(body, name=name, grid_spec=grid_spec, out_shape=jax.ShapeDtypeStruct((r, c), f32),
                 compiler_params=_params(("parallel",), 40 << 20))(chip, lands, sums)


def _reduce_start(grads, core, after, tag):
    g4s = [g.reshape(N_CHIP, 2, g.shape[0] // N_DEV, g.shape[1]) for g in grads]
    recv = _send_to_sibling(g4s, name="rs_sibling_" + tag)
    sums = [_add_sibling(g4, rv, core, name="rs_add_" + tag) for g4, rv in zip(g4s, recv)]
    lands = [lax.empty(s.shape, s.dtype) for s in sums]
    return _ici_start("reduce", sums, lands, after, name="rs_start_" + tag)


def _adamw_reduced(layer, w, m, v, lands, sums, chip, prev, name):
    L, r, c = w.shape
    tm = _row_tile(r, c, 11)
    c1 = 1.0 - ADAM_B1 ** ADAM_STEP
    c2 = 1.0 - ADAM_B2 ** ADAM_STEP
    n_prev = 0 if prev is None else 4

    def body(chip_ref, w_ref, m_ref, v_ref, l_ref, s_ref, *refs):
        g_ref, d_ref, mo_ref, vo_ref = refs[n_prev:]
        g_ = None
        for k in range(N_CHIP):
            part = jnp.where(chip_ref[0] == k, s_ref[k], l_ref[k]).astype(f32)
            g_ = part if g_ is None else g_ + part
        m_ = ADAM_B1 * m_ref[...] + (1.0 - ADAM_B1) * g_
        v_ = ADAM_B2 * v_ref[...] + (1.0 - ADAM_B2) * (g_ * g_)
        g_ref[...] = g_
        d_ref[...] = -ADAM_LR * ((m_ / c1) / (jnp.sqrt(v_ / c2) + ADAM_EPS) + ADAM_WD * w_ref[...])
        mo_ref[...] = m_
        vo_ref[...] = v_

    wspec = pl.BlockSpec((None, tm, c), lambda i, chip_ref: (layer, i, 0))
    sspec = pl.BlockSpec((N_CHIP, tm, c), lambda i, chip_ref: (0, i, 0))
    grid_spec = pltpu.PrefetchScalarGridSpec(
        num_scalar_prefetch=1, grid=(r // tm,), in_specs=[wspec] * 3 + [sspec] * 2 + [ANY] * n_prev, out_specs=[wspec] * 4)
    return _call(body, name=name, grid_spec=grid_spec, out_shape=[jax.ShapeDtypeStruct((L, r, c), f32)] * 4,
                 input_output_aliases={6 + i: i for i in range(n_prev)},
                 compiler_params=_params(("parallel",), 40 << 20))(chip, w, m, v, lands, sums, *(prev or ()))


def _rope_tables(S):
    rows = S // GRID_W
    row = jnp.repeat(jnp.arange(rows, dtype=f32), GRID_W)
    col = jnp.tile(jnp.arange(GRID_W, dtype=f32), rows)
    nf = HEAD_DIM // 4
    inv = ROPE_THETA ** (-jnp.arange(nf, dtype=f32) / nf)
    ang = jnp.concatenate([row[:, None] * inv, col[:, None] * inv], axis=-1)
    cos, sin = jnp.cos(ang), jnp.sin(ang)
    return jnp.concatenate([cos, cos], axis=-1), jnp.concatenate([-sin, sin], axis=-1)


def _layer_fwd(xin, p, w, more_weights, cos2, sin2):
    sv = {"xin": xin}
    h = sv["h"] = _rms_fwd(xin, p["g_mix"], name="rms_mix")
    proj = functools.partial(_mm, h, w["in"], "nt", bf16)
    q_raw = sv["q_raw"] = proj(n=Q_COLS, b_off=0, name="proj_q")
    kv_raw = sv["kv_raw"] = proj(n=2 * KV_COLS, b_off=OFF_KV, name="proj_kv")
    conv_in = sv["conv_in"] = proj(n=2 * CONV_CH, b_off=OFF_CONV, name="proj_conv")
    sg_in = sv["sg_in"] = proj(n=2 * SG_CH, b_off=OFF_SG, name="proj_sg")
    gl = sv["gl"] = proj(n=3 * D_MODEL, b_off=OFF_GATE, name="proj_gate")
    qr, kr = sv["qr"], sv["kr"] = _qk_fwd(q_raw, kv_raw, p["q_norm_g"], p["k_norm_g"], cos2, sin2, name="qk_fwd")
    o = sv["o"] = _attn_fwd(qr, kr, kv_raw, name="attn_fwd")
    c = sv["c"] = _conv1_fwd(conv_in, w["dw"], p["b_dw"], name="conv1_fwd")
    cz = sv["cz"] = _conv2_fwd(c, p["conv_ln_g"], p["conv_ln_b"], name="conv2_fwd")
    sz = sv["sz"] = _sgu_fwd(sg_in, p["sg_ln_g"], p["sg_ln_b"], p["w_s"], p["b_s"], name="sgu_fwd")
    w = {**w, **more_weights(1, sz)}
    ya = sv["ya"] = _mm(o, w["attn_o"], "nt", bf16, name="out_attn")
    yc = sv["yc"] = _mm(cz, w["conv_o"], "nt", bf16, name="out_conv")
    ys = sv["ys"] = _mm(sz, w["sg_o"], "nt", bf16, name="out_sg")
    merged = sv["merged"] = _merge_fwd(gl, p["b_gate"], ya, yc, ys, name="merge_fwd")
    x1 = sv["x1"] = _mm(merged, w["out"], "nn", f32, res=xin, name="out_proj")
    w = {**w, **more_weights(2, x1)}
    hf = sv["hf"] = _rms_fwd(x1, p["g_ffn"], name="rms_ffn")
    sv["fg"], sv["fu"], act = _ffn_up(hf, w["ff_gate"], w["ff_up"], name="ffn_up")
    sv["act"] = act
    x2 =_mm(act, w["ff_down"], "nn", f32, res=x1, name="ff_down")
    return x2, sv, w


def _layer_bwd(dx2, dx2b, sv, p, w, cos2, sin2, reduce_start):
    small = {}
    dfg, dfu = _ffn_down_bwd(dx2b, w["ff_down"], sv["fg"], sv["fu"], name="ffn_down_bwd")
    g_down = _mm(sv["act"], dx2b, "tn", bf16, name="g_ff_down")
    dhf = _mm(dfg, w["ff_gate"], "nn", f32, name="d_hf_gate")
    dhf = _mm(dfu, w["ff_up"], "nn", f32, res=dhf, name="d_hf_up")
    g_gate = _mm(dfg, sv["hf"], "tn", bf16, name="g_ff_gate")
    g_up = _mm(dfu, sv["hf"], "tn", bf16, name="g_ff_up")
    zero = reduce_start("ffn", dict(w_ff_gate=g_gate, w_ff_up=g_up, w_ff_down=g_down))[0, 0]
    dx1, dx1b, small["g_ffn"] = _rms_bwd(sv["x1"], p["g_ffn"] + zero, dhf, dx2, name="rms_ffn_bwd")
    g_out = _mm(sv["merged"], dx1b, "tn", bf16, name="g_out")
    *dgl, dya, dyc, dys, db0, db1, db2 = _merge_bwd_fused(dx1b, w["out"], sv["gl"], p["b_gate"], sv["ya"], sv["yc"], sv["ys"],
                                                        name="merge_bwd")
    small["b_gate"] = jnp.concatenate([db0, db1, db2], axis=1)
    do = _mm(dya, w["attn_o"], "nn", bf16, name="d_o")
    g_ao = _mm(dya, sv["o"], "tn", bf16, name="g_attn_o")
    dcz = _mm(dyc, w["conv_o"], "nn", bf16, name="d_cz")
    g_co = _mm(dyc, sv["cz"], "tn", bf16, name="g_conv_o")
    dsz = _mm(dys, w["sg_o"], "nn", bf16, name="d_sz")
    g_so = _mm(dys, sv["sz"], "tn", bf16, name="g_sg_o")
    zero = reduce_start("mix", dict(w_attn_o=g_ao, w_conv_o=g_co, w_sg_o=g_so, w_out=g_out))[0, 0]
    dsu, dsv, small["w_s"], small["b_s"], small["sg_ln_g"], small["sg_ln_b"] = _sgu_bwd(
        sv["sg_in"], dsz, p["sg_ln_g"] + zero, p["sg_ln_b"], p["w_s"], p["w_s_t"], p["b_s"], name="sgu_bwd")
    dc, small["conv_ln_g"], small["conv_ln_b"] = _conv2_bwd(sv["c"], dcz, p["conv_ln_g"], p["conv_ln_b"], name="conv2_bwd")
    da, dgt, small["w_dw"], small["b_dw"] = _conv1_bwd(sv["conv_in"], dc, w["dw"], name="conv1_bwd")
    dqr, dkr, dv = _attn_bwd(sv["qr"], sv["kr"], sv["kv_raw"], do, name="attn_bwd")
    dq_raw, dk_raw, small["q_norm_g"], small["k_norm_g"] = _qk_bwd(
        sv["q_raw"], sv["kv_raw"], dqr, dkr, p["q_norm_g"], p["k_norm_g"], cos2, sin2, name="qk_bwd")
    dproj = jnp.concatenate([dq_raw, dk_raw, dv.astype(bf16), da, dgt, dsu, dsv, *dgl], axis=1)
    g_in = _mm(dproj, sv["h"], "tn", bf16, name="g_in")
    started = reduce_start("in", dict(w_in=g_in))
    dh = _mm(dproj, w["in"], "nn", f32, after=started, name="d_h")
    dx, dxb, small["g_mix"] = _rms_bwd(sv["xin"], p["g_mix"], dh, dx1, name="rms_mix_bwd")
    return dx, dxb, small


SMALL = ("g_mix", "b_gate", "q_norm_g", "k_norm_g", "b_dw", "conv_ln_g", "conv_ln_b", "sg_ln_g", "sg_ln_b",
         "w_s", "b_s", "g_ffn")
PACK_ALIGN = 8 * LANES


def _pack(parts):
    flat = jnp.concatenate([a.reshape(-1).astype(f32) for a in parts])
    pad = -flat.shape[0] % PACK_ALIGN
    return jnp.pad(flat, (0, pad)).reshape(-1, LANES)


def _unpack(buf, shapes):
    flat = buf.reshape(-1)
    out, pos = [], 0
    for shp in shapes:
        size = math.prod(shp)
        out.append(flat[pos:pos + size].reshape(shp))
        pos += size
    return out


def kernel(x, g_mix, w_in, b_gate, q_norm_g, k_norm_g, w_attn_o, w_dw, b_dw, conv_ln_g, conv_ln_b, w_conv_o, sg_ln_g, sg_ln_b, w_s, b_s, w_sg_o, w_out, g_ffn, w_ff_gate, w_ff_up, w_ff_down, g_final, loss_target, m_g_mix, m_w_in, m_b_gate, m_q_norm_g, m_k_norm_g, m_w_attn_o, m_w_dw, m_b_dw, m_conv_ln_g, m_conv_ln_b, m_w_conv_o, m_sg_ln_g, m_sg_ln_b, m_w_s, m_b_s, m_w_sg_o, m_w_out, m_g_ffn, m_w_ff_gate, m_w_ff_up, m_w_ff_down, m_g_final, v_g_mix, v_w_in, v_b_gate, v_q_norm_g, v_k_norm_g, v_w_attn_o, v_w_dw, v_b_dw, v_conv_ln_g, v_conv_ln_b, v_w_conv_o, v_sg_ln_g, v_sg_ln_b, v_w_s, v_b_s, v_w_sg_o, v_w_out, v_g_ffn, v_w_ff_gate, v_w_ff_up, v_w_ff_down, v_g_final):
    weights = dict(g_mix=g_mix, w_in=w_in, b_gate=b_gate, q_norm_g=q_norm_g, k_norm_g=k_norm_g, w_attn_o=w_attn_o,
                   w_dw=w_dw, b_dw=b_dw, conv_ln_g=conv_ln_g, conv_ln_b=conv_ln_b, w_conv_o=w_conv_o, sg_ln_g=sg_ln_g,
                   sg_ln_b=sg_ln_b, w_s=w_s, b_s=b_s, w_sg_o=w_sg_o, w_out=w_out, g_ffn=g_ffn, w_ff_gate=w_ff_gate,
                   w_ff_up=w_ff_up, w_ff_down=w_ff_down, g_final=g_final)
    mom_m = dict(g_mix=m_g_mix, w_in=m_w_in, b_gate=m_b_gate, q_norm_g=m_q_norm_g, k_norm_g=m_k_norm_g,
                 w_attn_o=m_w_attn_o, w_dw=m_w_dw, b_dw=m_b_dw, conv_ln_g=m_conv_ln_g, conv_ln_b=m_conv_ln_b,
                 w_conv_o=m_w_conv_o, sg_ln_g=m_sg_ln_g, sg_ln_b=m_sg_ln_b, w_s=m_w_s, b_s=m_b_s, w_sg_o=m_w_sg_o,
                 w_out=m_w_out, g_ffn=m_g_ffn, w_ff_gate=m_w_ff_gate, w_ff_up=m_w_ff_up, w_ff_down=m_w_ff_down,
                 g_final=m_g_final)
    mom_v = dict(g_mix=v_g_mix, w_in=v_w_in, b_gate=v_b_gate, q_norm_g=v_q_norm_g, k_norm_g=v_k_norm_g,
                 w_attn_o=v_w_attn_o, w_dw=v_w_dw, b_dw=v_b_dw, conv_ln_g=v_conv_ln_g, conv_ln_b=v_conv_ln_b,
                 w_conv_o=v_w_conv_o, sg_ln_g=v_sg_ln_g, sg_ln_b=v_sg_ln_b, w_s=v_w_s, b_s=v_b_s, w_sg_o=v_w_sg_o,
                 w_out=v_w_out, g_ffn=v_g_ffn, w_ff_gate=v_w_ff_gate, w_ff_up=v_w_ff_up, w_ff_down=v_w_ff_down,
                 g_final=v_g_final)
    S, D = x.shape[1], x.shape[2]
    xi, yi, ci = _mesh_pos()
    me = 4 * xi + 2 * yi + ci
    core = jnp.reshape(ci, (1,)).astype(jnp.int32)
    cos2, sin2 = _rope_tables(S)

    big = ("w_in", "w_attn_o", "w_conv_o", "w_sg_o", "w_out", "w_ff_gate", "w_ff_up", "w_ff_down")
    transposed = {"w_in", "w_attn_o", "w_conv_o", "w_sg_o", "w_ff_gate", "w_ff_up"}
    chip = jnp.reshape(2 * xi + yi, (1,)).astype(jnp.int32)
    groups = (("in", "dw"), ("attn_o", "conv_o", "sg_o", "out"), ("ff_gate", "ff_up", "ff_down"))
    P, shards = [], []
    for l in range(DEPTH):
        sh = {n[2:]: (weights[n][l].T if n in transposed else weights[n][l]).astype(bf16) for n in big}
        sh["dw"] = jnp.pad(w_dw[l].reshape(CONV_W, LANES), ((0, CONV_WP - CONV_W), (0, 0)))
        shards.append(sh)
        p = {n: weights[n][l].reshape(1, -1) for n in SMALL if n not in ("w_s", "b_s")}
        p["w_s"] = w_s[l]
        p["w_s_t"] = jnp.swapaxes(w_s[l], 1, 2)
        p["b_s"] = b_s[l].reshape(SG_G, SG_CHUNK, 1)
        P.append(p)

    gathers = {}

    def start_gather(l, gi, after):
        srcs = [shards[l][n] for n in groups[gi]]
        lands = [lax.dynamic_update_index_in_dim(lax.empty((N_DEV,) + s.shape, s.dtype), s, me, 0) for s in srcs]
        gathers[l, gi] = _ici_start("gather", srcs, lands, after, name=f"ag_start_{l}{gi}")
        return gathers[l, gi][-1]

    def gathered(l, gi, after):
        srcs, lands = _ici_wait("gather", gathers[l, gi], after, name=f"ag_wait_{l}{gi}")
        after = srcs[0]
        if gi == len(groups) - 1 and l + 1 < DEPTH:
            for gj in range(len(groups)):
                after = start_gather(l + 1, gj, after)
        full = _d2d_gather([b.reshape(N_CHIP, 2, *b.shape[1:]) for b in lands], after, name=f"ag_d2d_{gi}")
        return {n: f.reshape(-1, f.shape[3]) for n, f in zip(groups[gi], full)}

    all_started = cos2
    for gi in range(len(groups)):
        all_started = start_gather(0, gi, all_started)

    h = x.reshape(S, D)
    saved, W = [], []
    for l in range(DEPTH):
        first = gathered(l, 0, all_started if l == 0 else h)
        if l == 0:
            P[l]["g_mix"] = P[l]["g_mix"] + all_started[0, 0]
        h, sv, w = _layer_fwd(h, P[l], first, functools.partial(lambda gi, z, l: gathered(l, gi, z), l=l), cos2, sin2)
        saved.append(sv)
        W.append(w)
    dx, dxb, sq, g_final_part = _final_loss(h, g_final.reshape(1, D), loss_target.reshape(S, D), name="final_loss")
    loss = lax.psum(0.5 * jnp.sum(sq) / D, ("x", "y", "c"))

    reductions, small_grads = {}, [None] * DEPTH
    for l in reversed(range(DEPTH)):
        def reduce_start(group, grads, l=l):
            names = tuple(grads)
            started = _reduce_start([grads[n] for n in names], core, grads[names[0]], tag=f"{group}{l}")
            reductions[l, group] = (names, started)
            return started[-1]

        dx, dxb, small_grads[l] = _layer_bwd(dx, dxb, saved[l], P[l], W[l], cos2, sin2, reduce_start)
    grad_x = dx.reshape(x.shape)

    grads_out, delta, new_m, new_v = {}, {}, {}, {}
    swap = lambda a: jnp.swapaxes(a, 1, 2)

    def update(n, lands, sums):
        as_arrives = n not in transposed or weights[n].shape[2] % LANES != 0
        if as_arrives:
            to_arrival = swap if n in transposed else (lambda a: a)
            out = None
            for l in reversed(range(DEPTH)):
                out = _adamw_reduced(l, to_arrival(weights[n]), to_arrival(mom_m[n]), to_arrival(mom_v[n]),
                                     lands[l], sums[l], chip, out, name=f"adamw_{n}_{l}")
            grads_out[n], delta[n], new_m[n], new_v[n] = [to_arrival(o) for o in out]
        else:
            g = jnp.stack([_sum_chip_slots(lands[l], sums[l], chip, name="rs_sum_" + n) for l in range(DEPTH)])
            grads_out[n] = swap(g)
            delta[n], new_m[n], new_v[n] = _adamw(weights[n], grads_out[n], mom_m[n], mom_v[n], name="adamw_" + n)
        return delta[n]

    after = dx
    for group in ("ffn", "mix", "in"):
        names = reductions[0, group][0]
        arrived = [_ici_wait("reduce", reductions[l, group][1], after, name=f"rs_wait_{group}{l}") for l in range(DEPTH)]
        for i, n in enumerate(names):
            after = update(n, [arrived[l][1][i] for l in range(DEPTH)], [arrived[l][0][i] for l in range(DEPTH)])

    small_shapes = [weights[n].shape for n in SMALL] + [g_final.shape, (DEPTH, CONV_CH // LANES, CONV_WP, LANES)]
    parts = [jnp.stack([small_grads[l][n].reshape(weights[n].shape[1:]) for l in range(DEPTH)]) for n in SMALL]
    parts += [g_final_part.reshape(g_final.shape), jnp.stack([small_grads[l]["w_dw"] for l in range(DEPTH)])]
    packed = _pack(parts)
    gathered = _all_gather([packed], after, name="gather_small")[0]
    total = _sum_slots(gathered, name="sum_small")
    small_total = _unpack(total, small_shapes)
    grads_out.update(zip(SMALL + ("g_final",), small_total[:-1]))
    dw_full = small_total[-1]
    grads_out["w_dw"] = lax.dynamic_index_in_dim(dw_full, me, axis=1, keepdims=False)[:, :CONV_W].reshape(w_dw.shape)

    rep = tuple(n for n in SMALL if n != "w_s") + ("g_final",)
    rep_shapes = [weights[n].shape for n in rep]
    packs = [_pack([src[n] for n in rep])[None] for src in (weights, grads_out, mom_m, mom_v)]
    for dst, buf in zip((delta, new_m, new_v), _adamw(*packs, name="adamw_small")):
        dst.update(zip(rep, _unpack(buf[0], rep_shapes)))
    for n, shp in (("w_dw", (1, DEPTH * CONV_W, LANES)), ("w_s", (DEPTH, SG_G * SG_CHUNK, SG_CHUNK))):
        upd = _adamw(*[src[n].reshape(shp) for src in (weights, grads_out, mom_m, mom_v)], name="adamw_" + n)
        for dst, buf in zip((delta, new_m, new_v), upd):
            dst[n] = buf.reshape(weights[n].shape)

    order = ("g_mix", "w_in", "b_gate", "q_norm_g", "k_norm_g", "w_attn_o", "w_dw", "b_dw", "conv_ln_g", "conv_ln_b",
             "w_conv_o", "sg_ln_g", "sg_ln_b", "w_s", "b_s", "w_sg_o", "w_out", "g_ffn", "w_ff_gate", "w_ff_up",
             "w_ff_down", "g_final")
    return (loss, grad_x, *[grads_out[n] for n in order], *[delta[n] for n in order],
            *[new_m[n] for n in order], *[new_v[n] for n in order])
```

```python
import functools
import math

import jax
import jax.numpy as jnp
from jax import lax
from jax.experimental import pallas as pl
from jax.experimental.pallas import tpu as pltpu

f32, bf16 = jnp.float32, jnp.bfloat16

D_MODEL = 2048
SEQ = 2048
DEPTH = 2
GRID_W = 64
HEAD_DIM = 128
LANES = 128
N_Q = (D_MODEL // 2) // HEAD_DIM
N_KV = N_Q // 4
GRP = N_Q // N_KV
Q_COLS = N_Q * HEAD_DIM
KV_COLS = N_KV * HEAD_DIM
CONV_CH = D_MODEL // 2
CONV_W = 31
CONV_PAD = CONV_W // 2
CONV_WP = 32
SG_CH = D_MODEL // 2
SG_G = SG_CH // LANES
SG_CHUNK = 128
D_FF = -(-8 * D_MODEL // (3 * 256)) * 256
OFF_KV = Q_COLS
OFF_CONV = OFF_KV + 2 * KV_COLS
OFF_SG = OFF_CONV + 2 * CONV_CH
OFF_GATE = OFF_SG + 2 * SG_CH
IN_COLS = OFF_GATE + 3 * D_MODEL
ROPE_THETA = 10000.0
SCALE = HEAD_DIM ** -0.5
N_DEV = 8
N_CHIP = 4

ADAM_LR, ADAM_B1, ADAM_B2, ADAM_EPS, ADAM_WD, ADAM_STEP = 0.001, 0.9, 0.999, 1e-08, 0.01, 10

VMEM_BYTES_V7X = 64 << 20
VMEM_CAP = VMEM_BYTES_V7X - (6 << 20)
MESH = pl.DeviceIdType.MESH
HBM = pl.BlockSpec(memory_space=pltpu.HBM)


def _in_hbm(a):
    if isinstance(a, jax.Array) and jnp.issubdtype(a.dtype, jnp.floating) and a.size * a.dtype.itemsize >= (1 << 20):
        return pltpu.with_memory_space_constraint(a, pltpu.HBM)
    return a


def _out_hbm(s):
    if isinstance(s, jax.ShapeDtypeStruct) and math.prod(s.shape) * jnp.dtype(s.dtype).itemsize >= (1 << 20):
        return pltpu.HBM(s.shape, s.dtype)
    return s


def _call(body, **kw):
    shapes = kw.pop("out_shape")
    shapes = type(shapes)(_out_hbm(s) for s in shapes) if isinstance(shapes, (list, tuple)) else _out_hbm(shapes)
    call = pl.pallas_call(body, out_shape=shapes, **kw)
    return lambda *args: call(*[_in_hbm(a) for a in args])


def _pick(n, cands):
    for c in cands:
        if n % c == 0:
            return c
    raise ValueError((n, cands))


def _params(sem, vmem_bytes):
    return pltpu.CompilerParams(dimension_semantics=sem, vmem_limit_bytes=int(min(max(vmem_bytes, 16 << 20), VMEM_CAP)))


def _mm(a, b, form, out_dtype, *, n=None, b_off=0, res=None, after=None, name):
    if form == "tn":
        K, M = a.shape
    else:
        M, K = a.shape
    N = n if n is not None else (b.shape[0] if form == "nt" else b.shape[1])
    if K <= 2048:
        tk = K
        if form == "tn":
            tm = _pick(M, (512, 256, 128))
            tn = N if N <= 2048 else _pick(N, (1024, 512, 256, 128))
        else:
            tm = M if M <= 2048 else _pick(M, (2048, 1024, 512))
            tn = _pick(math.gcd(N, b_off) if b_off else N, (256, 128) if res is not None else (512, 256, 128))
    else:
        tk = max(t for t in range(LANES, 3072 + 1, LANES) if K % t == 0)
        tm = _pick(M, (1024, 512, 256, 128))
        tn = _pick(math.gcd(N, b_off) if b_off else N, (1024, 512, 256, 128))
    assert b_off % tn == 0
    off = b_off // tn
    nk = K // tk
    if form == "tn":
        a_spec = pl.BlockSpec((tk, tm), lambda i, j, k: (k, i))
    else:
        a_spec = pl.BlockSpec((tm, tk), lambda i, j, k: (i, k))
    if form == "nt":
        b_spec = pl.BlockSpec((tn, tk), lambda i, j, k: (j + off, k))
    else:
        b_spec = pl.BlockSpec((tk, tn), lambda i, j, k: (k, j + off))
    dims = {"nn": ((1,), (0,)), "nt": ((1,), (1,)), "tn": ((0,), (0,))}[form]
    has_res = res is not None

    def body(*refs):
        if after is not None:
            refs = refs[1:]
        if has_res:
            a_ref, b_ref, r_ref, o_ref = refs[:4]
        else:
            a_ref, b_ref, o_ref = refs[:3]
        p = lax.dot_general(a_ref[...], b_ref[...], (dims, ((), ())), preferred_element_type=f32)

        def finish(acc):
            if has_res:
                acc = acc + r_ref[...].astype(f32)
            o_ref[...] = acc.astype(o_ref.dtype)

        if nk == 1:
            finish(p)
        else:
            acc_ref = refs[-1]
            k = pl.program_id(2)

            @pl.when(k == 0)
            def _():
                acc_ref[...] = p

            @pl.when(k > 0)
            def _():
                acc_ref[...] += p

            @pl.when(k == nk - 1)
            def _():
                finish(acc_ref[...])

    in_specs = [a_spec, b_spec]
    args = [a, b]
    osz = jnp.dtype(out_dtype).itemsize
    vmem = 2 * (tm * tk * 2 + tk * tn * 2 + tm * tn * osz) + 2 * tm * tn * 4
    if has_res:
        in_specs.append(pl.BlockSpec((tm, tn), lambda i, j, k: (i, j)))
        args.append(res)
        vmem += 2 * tm * tn * res.dtype.itemsize
    scratch = []
    if nk > 1:
        scratch.append(pltpu.VMEM((tm, tn), f32))
        vmem += tm * tn * 4
    if after is not None:
        in_specs.insert(0, pl.BlockSpec(memory_space=pl.ANY))
        args.insert(0, after)
    return _call(
        body, name=name, grid=(M // tm, N // tn, nk),
        in_specs=in_specs, out_specs=pl.BlockSpec((tm, tn), lambda i, j, k: (i, j)),
        out_shape=jax.ShapeDtypeStruct((M, N), out_dtype), scratch_shapes=scratch,
        compiler_params=_params(("parallel", "parallel", "arbitrary"), vmem + (8 << 20)),
    )(*args)


EPI_TN = 256


def _mm_epi(a, bs, form, extras, out_dtypes, n_sums, fn, name):
    M, K = a.shape
    N = bs[0].shape[0] if form == "nt" else bs[0].shape[1]
    tn = EPI_TN
    assert K <= 2048 and N % tn == 0
    dims = ((1,), (1,)) if form == "nt" else ((1,), (0,))
    nb, ne = len(bs), len(extras)

    def body(*refs):
        a_ref, b_refs, e_refs, o_refs = refs[0], refs[1:1 + nb], refs[1 + nb:1 + nb + ne], refs[1 + nb + ne:]
        av = a_ref[...]
        ps = [lax.dot_general(av, b[...], (dims, ((), ())), preferred_element_type=f32) for b in b_refs]
        for o_ref, o in zip(o_refs, fn(ps, [e[...] for e in e_refs])):
            o_ref[...] = o.astype(o_ref.dtype)

    in_specs = [pl.BlockSpec((M, K), lambda j: (0, 0), pipeline_mode=pl.Buffered(1))]
    in_specs += [pl.BlockSpec((tn, K), lambda j: (j, 0)) if form == "nt" else pl.BlockSpec((K, tn), lambda j: (0, j))
                 for _ in bs]
    for arr, first in extras:
        assert first % tn == 0
        in_specs.append(pl.BlockSpec((arr.shape[0], tn), functools.partial(lambda j, o: (0, j + o), o=first // tn)))
    out_specs = [pl.BlockSpec((M, tn), lambda j: (0, j))] * len(out_dtypes) + [pl.BlockSpec((1, tn), lambda j: (0, j))] * n_sums
    out_shape = [jax.ShapeDtypeStruct((M, N), dt) for dt in out_dtypes] + [jax.ShapeDtypeStruct((1, N), f32)] * n_sums
    tiles = sum(arr.shape[0] * tn * arr.dtype.itemsize for arr, _ in extras) + sum(M * tn * jnp.dtype(dt).itemsize for dt in out_dtypes)
    vmem = M * K * 2 + 2 * nb * tn * K * 2 + 2 * tiles + (nb + 6) * M * tn * 4
    return _call(body, name=name, grid=(N // tn,), in_specs=in_specs, out_specs=out_specs, out_shape=out_shape,
                 compiler_params=_params(("parallel",), vmem + (8 << 20)))(a, *bs, *[arr for arr, _ in extras])


def _ffn_up(hf, wt_gate, wt_up, name):
    def fn(ps, _):
        g, u = ps[0].astype(bf16), ps[1].astype(bf16)
        gf = g.astype(f32)
        return g, u, gf * jax.nn.sigmoid(gf) * u.astype(f32)

    return _mm_epi(hf, [wt_gate, wt_up], "nt", [], [bf16] * 3, 0, fn, name)


def _ffn_down_bwd(dx2b, w_down, fg, fu, name):
    def fn(ps, es):
        d, g = ps[0], es[0].astype(f32)
        sg = jax.nn.sigmoid(g)
        return d * es[1].astype(f32) * sg * (1.0 + g * (1.0 - sg)), d * g * sg

    return _mm_epi(dx2b, [w_down], "nt", [(fg, 0), (fu, 0)], [bf16] * 2, 0, fn, name)


def _merge_bwd_fused(dx1b, w_out, gl, b_gate, ya, yc, ys, name):
    D = ya.shape[1]

    def fn(ps, es):
        dm_, outs, sums = ps[0], [], []
        for i in range(3):
            gate = jax.nn.sigmoid(es[i].astype(f32) + es[3 + i])
            dlog = dm_ * es[6 + i].astype(f32) * gate * (1.0 - gate)
            outs.append((dlog, dm_ * gate))
            sums.append(jnp.sum(dlog, axis=0, keepdims=True))
        return [o[0] for o in outs] + [o[1] for o in outs] + sums

    extras = [(gl, i * D) for i in range(3)] + [(b_gate, i * D) for i in range(3)] + [(ya, 0), (yc, 0), (ys, 0)]
    return _mm_epi(dx1b, [w_out], "nt", extras, [bf16] * 6, 3, fn, name)


def _rows(body, ins, outs, *, tm, name, vmem=40 << 20):
    nrows = next(s[1].shape[0] for s in ins if s[0] == "r")
    in_specs, args = [], []
    for s in ins:
        arr = s[1]
        if s[0] == "r":
            w = s[2] if len(s) > 2 else arr.shape[1]
            cb = s[3] if len(s) > 3 else 0
            in_specs.append(pl.BlockSpec((tm, w), functools.partial(lambda i, cb: (i, cb), cb=cb)))
        else:
            in_specs.append(pl.BlockSpec(arr.shape, functools.partial(lambda i, nd: (0,) * nd, nd=arr.ndim)))
        args.append(arr)
    out_specs, out_shape = [], []
    for s in outs:
        if s[0] == "r":
            out_specs.append(pl.BlockSpec((tm, s[1]), lambda i: (i, 0)))
            out_shape.append(jax.ShapeDtypeStruct((nrows, s[1]), s[2]))
        else:
            out_specs.append(pl.BlockSpec(s[1], functools.partial(lambda i, nd: (0,) * nd, nd=len(s[1]))))
            out_shape.append(jax.ShapeDtypeStruct(s[1], s[2]))
    return _call(body, name=name, grid=(nrows // tm,), in_specs=in_specs, out_specs=out_specs,
                 out_shape=out_shape, compiler_params=_params(("arbitrary",), vmem))(*args)


def _accumulate(ref, part):
    i = pl.program_id(0)

    @pl.when(i == 0)
    def _():
        ref[...] = part

    @pl.when(i > 0)
    def _():
        ref[...] += part


def _rms_stats(x):
    r = lax.rsqrt(jnp.mean(x * x, axis=-1, keepdims=True) + 1e-6)
    return r, x * r


def _rms_fwd(x, g, name):
    def body(x_ref, g_ref, o_ref):
        _, xn = _rms_stats(x_ref[...])
        o_ref[...] = (xn * g_ref[...]).astype(o_ref.dtype)

    return _rows(body, [("r", x), ("f", g)], [("r", x.shape[1], bf16)], tm=min(256, x.shape[0]), name=name)[0]


def _rms_bwd(x, g, dh, dres, name):
    D = x.shape[1]

    def body(x_ref, g_ref, dh_ref, dr_ref, dx_ref, dxb_ref, dg_ref):
        r, xn = _rms_stats(x_ref[...])
        dy = dh_ref[...].astype(f32)
        dxn = dy * g_ref[...]
        dx = dr_ref[...] + r * (dxn - xn * jnp.mean(dxn * xn, axis=-1, keepdims=True))
        dx_ref[...] = dx
        dxb_ref[...] = dx.astype(bf16)
        _accumulate(dg_ref, jnp.sum(dy * xn, axis=0, keepdims=True))

    return _rows(body, [("r", x), ("f", g), ("r", dh), ("r", dres)],
                 [("r", D, f32), ("r", D, bf16), ("a", (1, D), f32)], tm=min(256, x.shape[0]), name=name)


def _final_loss(x, g, tgt, name):
    D = x.shape[1]

    def body(x_ref, g_ref, t_ref, dx_ref, dxb_ref, sq_ref, dg_ref):
        r, xn = _rms_stats(x_ref[...])
        gain = g_ref[...]
        diff = xn * gain - t_ref[...]
        dy = diff * (1.0 / D)
        dxn = dy * gain
        dx = r * (dxn - xn * jnp.mean(dxn * xn, axis=-1, keepdims=True))
        dx_ref[...] = dx
        dxb_ref[...] = dx.astype(bf16)
        _accumulate(sq_ref, jnp.sum(diff * diff, axis=0, keepdims=True))
        _accumulate(dg_ref, jnp.sum(dy * xn, axis=0, keepdims=True))

    return _rows(body, [("r", x), ("f", g), ("r", tgt)],
                 [("r", D, f32), ("r", D, bf16), ("a", (1, D), f32), ("a", (1, D), f32)],
                 tm=min(256, x.shape[0]), name=name)


def _qk_fwd(q_raw, kv_raw, qg, kg, cos2, sin2, name):
    def body(q_ref, k_ref, qg_ref, kg_ref, c_ref, s_ref, qo_ref, ko_ref):
        c, s = c_ref[...], s_ref[...]

        def head(src, gain, dst, h):
            cols = slice(h * HEAD_DIM, (h + 1) * HEAD_DIM)
            _, xn = _rms_stats(src[:, cols].astype(f32))
            y = xn * gain
            dst[:, cols] = (y * c + pltpu.roll(y, HEAD_DIM // 2, 1) * s).astype(dst.dtype)

        for h in range(N_Q):
            head(q_ref, qg_ref[...], qo_ref, h)
        for h in range(N_KV):
            head(k_ref, kg_ref[...], ko_ref, h)

    return _rows(body, [("r", q_raw), ("r", kv_raw, KV_COLS, 0), ("f", qg), ("f", kg), ("r", cos2), ("r", sin2)],
                 [("r", Q_COLS, bf16), ("r", KV_COLS, bf16)], tm=min(256, q_raw.shape[0]), name=name)


def _qk_bwd(q_raw, kv_raw, dqr, dkr, qg, kg, cos2, sin2, name):
    def body(q_ref, k_ref, dq_ref, dk_ref, qg_ref, kg_ref, c_ref, s_ref, dqo_ref, dko_ref, dqg_ref, dkg_ref):
        c, s = c_ref[...], s_ref[...]

        def head(src, dsrc, gain, dst, h):
            cols = slice(h * HEAD_DIM, (h + 1) * HEAD_DIM)
            r, xn = _rms_stats(src[:, cols].astype(f32))
            do = dsrc[:, cols].astype(f32)
            dy = do * c + pltpu.roll(do * s, HEAD_DIM // 2, 1)
            dxn = dy * gain
            dst[:, cols] = (r * (dxn - xn * jnp.mean(dxn * xn, axis=-1, keepdims=True))).astype(dst.dtype)
            return jnp.sum(dy * xn, axis=0, keepdims=True)

        dq_gain = head(q_ref, dq_ref, qg_ref[...], dqo_ref, 0)
        for h in range(1, N_Q):
            dq_gain = dq_gain + head(q_ref, dq_ref, qg_ref[...], dqo_ref, h)
        dk_gain = head(k_ref, dk_ref, kg_ref[...], dko_ref, 0)
        for h in range(1, N_KV):
            dk_gain = dk_gain + head(k_ref, dk_ref, kg_ref[...], dko_ref, h)
        _accumulate(dqg_ref, dq_gain)
        _accumulate(dkg_ref, dk_gain)

    return _rows(body, [("r", q_raw), ("r", kv_raw, KV_COLS, 0), ("r", dqr), ("r", dkr), ("f", qg), ("f", kg),
                        ("r", cos2), ("r", sin2)],
                 [("r", Q_COLS, bf16), ("r", KV_COLS, bf16), ("a", (1, HEAD_DIM), f32), ("a", (1, HEAD_DIM), f32)],
                 tm=min(256, q_raw.shape[0]), name=name)


def _softmax_rows(q, k):
    s = lax.dot_general(q, k, (((1,), (1,)), ((), ())), preferred_element_type=f32) * SCALE
    p = jnp.exp(s - jnp.max(s, axis=-1, keepdims=True))
    return p * (1.0 / jnp.sum(p, axis=-1, keepdims=True))


def _head_cols(g):
    return slice(g * HEAD_DIM, (g + 1) * HEAD_DIM)


def _attn_fwd(qr, kr, kv_raw, name):
    S = qr.shape[0]
    tq = min(256, S)

    def body(q_ref, k_ref, v_ref, o_ref):
        k, v = k_ref[...], v_ref[...]
        for g in range(GRP):
            p = _softmax_rows(q_ref[:, _head_cols(g)], k)
            o_ref[:, _head_cols(g)] = jnp.dot(p.astype(bf16), v, preferred_element_type=f32).astype(o_ref.dtype)

    return _call(
        body, name=name, grid=(N_KV, S // tq),
        in_specs=[pl.BlockSpec((tq, GRP * HEAD_DIM), lambda kv, i: (i, kv)),
                  pl.BlockSpec((S, HEAD_DIM), lambda kv, i: (0, kv)),
                  pl.BlockSpec((S, HEAD_DIM), lambda kv, i: (0, N_KV + kv))],
        out_specs=pl.BlockSpec((tq, GRP * HEAD_DIM), lambda kv, i: (i, kv)),
        out_shape=jax.ShapeDtypeStruct((S, Q_COLS), bf16),
        compiler_params=_params(("parallel", "arbitrary"), 4 * GRP * tq * S * 4 + (8 << 20)),
    )(qr, kr, kv_raw)


def _attn_bwd(qr, kr, kv_raw, do, name):
    S = qr.shape[0]
    tq = min(256, S)

    def body(q_ref, k_ref, v_ref, do_ref, dq_ref, dk_ref, dv_ref):
        first = pl.program_id(1) == 0
        k, v = k_ref[...], v_ref[...]
        dv_part = dk_part = None
        for g in range(GRP):
            q, do_ = q_ref[:, _head_cols(g)], do_ref[:, _head_cols(g)]
            p = _softmax_rows(q, k)
            dp = lax.dot_general(do_, v, (((1,), (1,)), ((), ())), preferred_element_type=f32)
            ds = (p * (dp - jnp.sum(dp * p, axis=-1, keepdims=True)) * SCALE).astype(bf16)
            dq_ref[:, _head_cols(g)] = jnp.dot(ds, k, preferred_element_type=f32).astype(dq_ref.dtype)
            dv_g = lax.dot_general(p.astype(bf16), do_, (((0,), (0,)), ((), ())), preferred_element_type=f32)
            dk_g = lax.dot_general(ds, q, (((0,), (0,)), ((), ())), preferred_element_type=f32)
            dv_part = dv_g if g == 0 else dv_part + dv_g
            dk_part = dk_g if g == 0 else dk_part + dk_g

        @pl.when(first)
        def _():
            dv_ref[...] = dv_part
            dk_ref[...] = dk_part

        @pl.when(jnp.logical_not(first))
        def _():
            dv_ref[...] += dv_part
            dk_ref[...] += dk_part

    qspec = pl.BlockSpec((tq, GRP * HEAD_DIM), lambda kv, i: (i, kv))
    return _call(
        body, name=name, grid=(N_KV, S // tq),
        in_specs=[qspec, pl.BlockSpec((S, HEAD_DIM), lambda kv, i: (0, kv)),
                  pl.BlockSpec((S, HEAD_DIM), lambda kv, i: (0, N_KV + kv)), qspec],
        out_specs=[qspec, pl.BlockSpec((S, HEAD_DIM), lambda kv, i: (0, kv)),
                   pl.BlockSpec((S, HEAD_DIM), lambda kv, i: (0, kv))],
        out_shape=[jax.ShapeDtypeStruct((S, Q_COLS), bf16), jax.ShapeDtypeStruct((S, KV_COLS), f32),
                   jax.ShapeDtypeStruct((S, KV_COLS), f32)],
        compiler_params=_params(("parallel", "arbitrary"), 6 * GRP * tq * S * 4 + (8 << 20)),
    )(qr, kr, kv_raw, do)


CONV_HALO = 16


def _fill_padded(pad_ref, val, S):
    pad_ref[pl.ds(0, CONV_HALO), :] = jnp.zeros((CONV_HALO, LANES), f32)
    pad_ref[pl.ds(CONV_HALO + S, CONV_HALO), :] = jnp.zeros((CONV_HALO, LANES), f32)
    pad_ref[pl.ds(CONV_HALO, S), :] = val


def _group_specs(S, n_groups, second_half):
    return pl.BlockSpec((S, LANES), functools.partial(lambda g, o: (0, g + o), o=n_groups if second_half else 0))


def _conv1_fwd(conv_in, wdw, b_dw, name):
    S = conv_in.shape[0]
    ng = CONV_CH // LANES
    R = min(256, S)

    def body(a_ref, g_ref, w_ref, b_ref, o_ref, pad_ref):
        z = a_ref[...].astype(f32) * jax.nn.sigmoid(g_ref[...].astype(f32))
        _fill_padded(pad_ref, z, S)
        for r in range(S // R):
            acc = jnp.zeros((R, LANES), f32) + b_ref[...]
            for j in range(CONV_W):
                acc = acc + w_ref[pl.ds(j, 1), :] * pad_ref[pl.ds(r * R + CONV_HALO - CONV_PAD + j, R), :]
            o_ref[pl.ds(r * R, R), :] = acc

    return _call(
        body, name=name, grid=(ng,),
        in_specs=[_group_specs(S, ng, False), _group_specs(S, ng, True),
                  pl.BlockSpec((CONV_WP, LANES), lambda g: (g, 0)), pl.BlockSpec((1, LANES), lambda g: (0, g))],
        out_specs=pl.BlockSpec((S, LANES), lambda g: (0, g)),
        out_shape=jax.ShapeDtypeStruct((S, CONV_CH), f32),
        scratch_shapes=[pltpu.VMEM((S + 2 * CONV_HALO, LANES), f32)],
        compiler_params=_params(("parallel",), 24 << 20),
    )(conv_in, conv_in, wdw, b_dw)


def _conv1_bwd(conv_in, dc, wdw, name):
    S = conv_in.shape[0]
    ng = CONV_CH // LANES
    R = min(256, S)

    def body(a_ref, g_ref, w_ref, dc_ref, da_ref, dg_ref, dw_ref, db_ref, padz_ref, padd_ref):
        a = a_ref[...].astype(f32)
        sg = jax.nn.sigmoid(g_ref[...].astype(f32))
        _fill_padded(padz_ref, a * sg, S)
        _fill_padded(padd_ref, dc_ref[...], S)
        for r in range(S // R):
            dz = jnp.zeros((R, LANES), f32)
            for j in range(CONV_W):
                dz = dz + w_ref[pl.ds(j, 1), :] * padd_ref[pl.ds(r * R + CONV_HALO + CONV_PAD - j, R), :]
            rows = pl.ds(r * R, R)
            ar, sr = a_ref[rows, :].astype(f32), jax.nn.sigmoid(g_ref[rows, :].astype(f32))
            da_ref[rows, :] = (dz * sr).astype(da_ref.dtype)
            dg_ref[rows, :] = (dz * ar * sr * (1.0 - sr)).astype(dg_ref.dtype)
        for j in range(CONV_W):
            tot = jnp.zeros((1, LANES), f32)
            for r in range(S // R):
                tot = tot + jnp.sum(dc_ref[pl.ds(r * R, R), :] * padz_ref[pl.ds(r * R + CONV_HALO - CONV_PAD + j, R), :],
                                    axis=0, keepdims=True)
            dw_ref[pl.ds(j, 1), :] = tot
        dw_ref[pl.ds(CONV_W, CONV_WP - CONV_W), :] = jnp.zeros((CONV_WP - CONV_W, LANES), f32)
        db_ref[...] = jnp.sum(dc_ref[...], axis=0, keepdims=True)

    return _call(
        body, name=name, grid=(ng,),
        in_specs=[_group_specs(S, ng, False), _group_specs(S, ng, True),
                  pl.BlockSpec((CONV_WP, LANES), lambda g: (g, 0)), pl.BlockSpec((S, LANES), lambda g: (0, g))],
        out_specs=[pl.BlockSpec((S, LANES), lambda g: (0, g)), pl.BlockSpec((S, LANES), lambda g: (0, g)),
                   pl.BlockSpec((CONV_WP, LANES), lambda g: (g, 0)), pl.BlockSpec((1, LANES), lambda g: (0, g))],
        out_shape=[jax.ShapeDtypeStruct((S, CONV_CH), bf16), jax.ShapeDtypeStruct((S, CONV_CH), bf16),
                   jax.ShapeDtypeStruct((ng * CONV_WP, LANES), f32), jax.ShapeDtypeStruct((1, CONV_CH), f32)],
        scratch_shapes=[pltpu.VMEM((S + 2 * CONV_HALO, LANES), f32), pltpu.VMEM((S + 2 * CONV_HALO, LANES), f32)],
        compiler_params=_params(("parallel",), 24 << 20),
    )(conv_in, conv_in, wdw, dc)


def _ln_stats(x, eps=1e-5):
    xc = x - jnp.mean(x, axis=-1, keepdims=True)
    r = lax.rsqrt(jnp.mean(xc * xc, axis=-1, keepdims=True) + eps)
    return r, xc * r


def _ln_bwd(r, xh, dxh):
    return r * (dxh - jnp.mean(dxh, axis=-1, keepdims=True) - xh * jnp.mean(dxh * xh, axis=-1, keepdims=True))


def _conv2_fwd(c, ln_g, ln_b, name):
    def body(c_ref, g_ref, b_ref, o_ref):
        _, xh = _ln_stats(c_ref[...])
        y = xh * g_ref[...] + b_ref[...]
        o_ref[...] = (y * jax.nn.sigmoid(y)).astype(o_ref.dtype)

    return _rows(body, [("r", c), ("f", ln_g), ("f", ln_b)], [("r", CONV_CH, bf16)], tm=min(256, c.shape[0]), name=name)[0]


def _conv2_bwd(c, dcz, ln_g, ln_b, name):
    def body(c_ref, d_ref, g_ref, b_ref, dc_ref, dg_ref, db_ref):
        r, xh = _ln_stats(c_ref[...])
        y = xh * g_ref[...] + b_ref[...]
        sg = jax.nn.sigmoid(y)
        dy = d_ref[...].astype(f32) * (sg * (1.0 + y * (1.0 - sg)))
        dc_ref[...] = _ln_bwd(r, xh, dy * g_ref[...])
        _accumulate(dg_ref, jnp.sum(dy * xh, axis=0, keepdims=True))
        _accumulate(db_ref, jnp.sum(dy, axis=0, keepdims=True))

    return _rows(body, [("r", c), ("r", dcz), ("f", ln_g), ("f", ln_b)],
                 [("r", CONV_CH, f32), ("a", (1, CONV_CH), f32), ("a", (1, CONV_CH), f32)],
                 tm=min(256, c.shape[0]), name=name)


GELU_K = math.sqrt(2.0 / math.pi)
GELU_C = 0.044715


def _gelu(x):
    return 0.5 * x * (1.0 + jnp.tanh(GELU_K * (x + GELU_C * x * x * x)))


def _gelu_grad(x):
    th = jnp.tanh(GELU_K * (x + GELU_C * x * x * x))
    return 0.5 * (1.0 + th) + 0.5 * x * (1.0 - th * th) * (GELU_K * (1.0 + 3.0 * GELU_C * x * x))


def _chunk_rows(n):
    return pl.ds(pl.multiple_of(n * SG_CHUNK, SG_CHUNK), SG_CHUNK)


def _sgu_fwd(sg_in, ln_g, ln_b, w_s, b_s, name):
    S = sg_in.shape[0]

    def body(u_ref, v_ref, lg_ref, lb_ref, w_ref, b_ref, o_ref):
        wb = w_ref[...].astype(bf16)

        def chunk(n, carry):
            rows = _chunk_rows(n)
            gu = _gelu(u_ref[rows, :].astype(f32))
            _, xh = _ln_stats(_gelu(v_ref[rows, :].astype(f32)))
            vl = xh * lg_ref[...] + lb_ref[...]
            t = jnp.dot(wb, vl.astype(bf16), preferred_element_type=f32) + b_ref[...]
            o_ref[rows, :] = (gu * t).astype(o_ref.dtype)
            return carry

        lax.fori_loop(0, S // SG_CHUNK, chunk, 0, unroll=2)

    return _call(
        body, name=name, grid=(SG_G,),
        in_specs=[_group_specs(S, SG_G, False), _group_specs(S, SG_G, True),
                  pl.BlockSpec((1, LANES), lambda g: (0, g)), pl.BlockSpec((1, LANES), lambda g: (0, g)),
                  pl.BlockSpec((None, SG_CHUNK, SG_CHUNK), lambda g: (g, 0, 0)),
                  pl.BlockSpec((None, SG_CHUNK, 1), lambda g: (g, 0, 0))],
        out_specs=pl.BlockSpec((S, LANES), lambda g: (0, g)),
        out_shape=jax.ShapeDtypeStruct((S, SG_CH), bf16),
        compiler_params=_params(("parallel",), 24 << 20),
    )(sg_in, sg_in, ln_g, ln_b, w_s, b_s)


def _sgu_bwd(sg_in, dsz, ln_g, ln_b, w_s, w_s_t, b_s, name):
    S = sg_in.shape[0]

    def body(u_ref, v_ref, lg_ref, lb_ref, w_ref, wt_ref, b_ref, d_ref, du_ref, dv_ref, dw_ref, db_ref, dlg_ref, dlb_ref):
        wb = w_ref[...].astype(bf16)
        wtb = wt_ref[...].astype(bf16)

        def chunk(n, carry):
            dwa, dba, dlga, dlba = carry
            rows = _chunk_rows(n)
            u = u_ref[rows, :].astype(f32)
            v = v_ref[rows, :].astype(f32)
            gu = _gelu(u)
            r, xh = _ln_stats(_gelu(v))
            vlb = (xh * lg_ref[...] + lb_ref[...]).astype(bf16)
            t = jnp.dot(wb, vlb, preferred_element_type=f32) + b_ref[...]
            d = d_ref[rows, :].astype(f32)
            dt = d * gu
            dtb = dt.astype(bf16)
            dwa = dwa + lax.dot_general(dtb, vlb, (((1,), (1,)), ((), ())), preferred_element_type=f32)
            dba = dba + jnp.sum(dt, axis=1, keepdims=True)
            dvl = jnp.dot(wtb, dtb, preferred_element_type=f32)
            dlga = dlga + jnp.sum(dvl * xh, axis=0, keepdims=True)
            dlba = dlba + jnp.sum(dvl, axis=0, keepdims=True)
            dgv = _ln_bwd(r, xh, dvl * lg_ref[...])
            du_ref[rows, :] = (d * t * _gelu_grad(u)).astype(du_ref.dtype)
            dv_ref[rows, :] = (dgv * _gelu_grad(v)).astype(dv_ref.dtype)
            return dwa, dba, dlga, dlba

        init = (jnp.zeros((SG_CHUNK, SG_CHUNK), f32), jnp.zeros((SG_CHUNK, 1), f32),
                jnp.zeros((1, LANES), f32), jnp.zeros((1, LANES), f32))
        dwa, dba, dlga, dlba = lax.fori_loop(0, S // SG_CHUNK, chunk, init, unroll=2)
        dw_ref[...] = dwa
        db_ref[...] = dba
        dlg_ref[...] = dlga
        dlb_ref[...] = dlba

    wspec = pl.BlockSpec((None, SG_CHUNK, SG_CHUNK), lambda g: (g, 0, 0))
    bspec = pl.BlockSpec((None, SG_CHUNK, 1), lambda g: (g, 0, 0))
    lspec = pl.BlockSpec((1, LANES), lambda g: (0, g))
    cspec = pl.BlockSpec((S, LANES), lambda g: (0, g))
    return _call(
        body, name=name, grid=(SG_G,),
        in_specs=[_group_specs(S, SG_G, False), _group_specs(S, SG_G, True), lspec, lspec, wspec, wspec, bspec, cspec],
        out_specs=[cspec, cspec, wspec, bspec, lspec, lspec],
        out_shape=[jax.ShapeDtypeStruct((S, SG_CH), bf16), jax.ShapeDtypeStruct((S, SG_CH), bf16),
                   jax.ShapeDtypeStruct((SG_G, SG_CHUNK, SG_CHUNK), f32), jax.ShapeDtypeStruct((SG_G, SG_CHUNK, 1), f32),
                   jax.ShapeDtypeStruct((1, SG_CH), f32), jax.ShapeDtypeStruct((1, SG_CH), f32)],
        compiler_params=_params(("parallel",), 24 << 20),
    )(sg_in, sg_in, ln_g, ln_b, w_s, w_s_t, b_s, dsz)


def _merge_fwd(gl, b_gate, ya, yc, ys, name):
    D = ya.shape[1]

    def body(gl_ref, b_ref, ya_ref, yc_ref, ys_ref, o_ref):
        acc = jnp.zeros(o_ref.shape, f32)
        for i, y_ref in enumerate((ya_ref, yc_ref, ys_ref)):
            cols = slice(i * D, (i + 1) * D)
            acc = acc + jax.nn.sigmoid(gl_ref[:, cols].astype(f32) + b_ref[:, cols]) * y_ref[...].astype(f32)
        o_ref[...] = acc.astype(o_ref.dtype)

    return _rows(body, [("r", gl), ("f", b_gate), ("r", ya), ("r", yc), ("r", ys)], [("r", D, bf16)],
                 tm=min(128, gl.shape[0]), name=name)[0]


def _merge_bwd(dm, gl, b_gate, ya, yc, ys, name):
    D = ya.shape[1]

    def body(dm_ref, gl_ref, b_ref, ya_ref, yc_ref, ys_ref, dgl_ref, dya_ref, dyc_ref, dys_ref, db_ref):
        dm_ = dm_ref[...].astype(f32)
        for i, (y_ref, dy_ref) in enumerate(((ya_ref, dya_ref), (yc_ref, dyc_ref), (ys_ref, dys_ref))):
            cols = slice(i * D, (i + 1) * D)
            gate = jax.nn.sigmoid(gl_ref[:, cols].astype(f32) + b_ref[:, cols])
            dy_ref[...] = (dm_ * gate).astype(dy_ref.dtype)
            dlog = dm_ * y_ref[...].astype(f32) * gate * (1.0 - gate)
            dgl_ref[:, cols] = dlog.astype(dgl_ref.dtype)
            part = jnp.sum(dlog, axis=0, keepdims=True)
            first = pl.program_id(0) == 0

            @pl.when(first)
            def _():
                db_ref[:, cols] = part

            @pl.when(jnp.logical_not(first))
            def _():
                db_ref[:, cols] += part

    return _rows(body, [("r", dm), ("r", gl), ("f", b_gate), ("r", ya), ("r", yc), ("r", ys)],
                 [("r", 3 * D, bf16), ("r", D, bf16), ("r", D, bf16), ("r", D, bf16), ("a", (1, 3 * D), f32)],
                 tm=min(128, gl.shape[0]), name=name)


def _swiglu_fwd(fg, fu, name):
    def body(g_ref, u_ref, o_ref):
        g = g_ref[...].astype(f32)
        o_ref[...] = (g * jax.nn.sigmoid(g) * u_ref[...].astype(f32)).astype(o_ref.dtype)

    return _rows(body, [("r", fg), ("r", fu)], [("r", fg.shape[1], bf16)], tm=min(128, fg.shape[0]), name=name)[0]


def _swiglu_bwd(dact, fg, fu, name):
    def body(d_ref, g_ref, u_ref, dg_ref, du_ref):
        d = d_ref[...].astype(f32)
        g = g_ref[...].astype(f32)
        sg = jax.nn.sigmoid(g)
        dg_ref[...] = (d * u_ref[...].astype(f32) * sg * (1.0 + g * (1.0 - sg))).astype(dg_ref.dtype)
        du_ref[...] = (d * g * sg).astype(du_ref.dtype)

    return _rows(body, [("r", dact), ("r", fg), ("r", fu)], [("r", fg.shape[1], bf16), ("r", fg.shape[1], bf16)],
                 tm=min(128, fg.shape[0]), name=name)


def _row_tile(r, c, n_arrays, itemsize=4):
    fits = [tm for tm in range(16, r + 1, 16) if r % tm == 0 and 2 * n_arrays * tm * c * itemsize <= (24 << 20)]
    return fits[-1] if fits else r


def _sum_slots(slots, name):
    n, r, c = slots.shape
    tm = _row_tile(r, c, n + 2)

    def body(s_ref, o_ref):
        acc = s_ref[0].astype(f32)
        for k in range(1, n):
            acc = acc + s_ref[k].astype(f32)
        o_ref[...] = acc

    return _call(body, name=name, grid=(r // tm,),
                 in_specs=[pl.BlockSpec((n, tm, c), lambda i: (0, i, 0))],
                 out_specs=pl.BlockSpec((tm, c), lambda i: (i, 0)),
                 out_shape=jax.ShapeDtypeStruct((r, c), f32),
                 compiler_params=_params(("parallel",), 40 << 20))(slots)


def _add_sibling(g4, recv, core, name):
    _, _, r, c = g4.shape
    tm = _row_tile(r, c, 3, 2)

    def body(core_ref, g_ref, r_ref, o_ref):
        o_ref[...] = (g_ref[...].astype(f32) + r_ref[...].astype(f32)).astype(o_ref.dtype)

    grid_spec = pltpu.PrefetchScalarGridSpec(
        num_scalar_prefetch=1, grid=(N_CHIP, r // tm),
        in_specs=[pl.BlockSpec((None, None, tm, c), lambda k, i, core_ref: (k, core_ref[0], i, 0)),
                  pl.BlockSpec((None, tm, c), lambda k, i, core_ref: (k, i, 0))],
        out_specs=pl.BlockSpec((None, tm, c), lambda k, i, core_ref: (k, i, 0)))
    return _call(body, name=name, grid_spec=grid_spec, out_shape=jax.ShapeDtypeStruct((N_CHIP, r, c), bf16),
                 compiler_params=_params(("parallel", "parallel"), 40 << 20))(core, g4, recv)


def _adamw(w, g, m, v, name):
    L, r, c = w.shape
    tm = _row_tile(r, c, 7)
    c1 = 1.0 - ADAM_B1 ** ADAM_STEP
    c2 = 1.0 - ADAM_B2 ** ADAM_STEP

    def body(w_ref, g_ref, m_ref, v_ref, d_ref, mo_ref, vo_ref):
        g_ = g_ref[...]
        m_ = ADAM_B1 * m_ref[...] + (1.0 - ADAM_B1) * g_
        v_ = ADAM_B2 * v_ref[...] + (1.0 - ADAM_B2) * (g_ * g_)
        d_ref[...] = -ADAM_LR * ((m_ / c1) / (jnp.sqrt(v_ / c2) + ADAM_EPS) + ADAM_WD * w_ref[...])
        mo_ref[...] = m_
        vo_ref[...] = v_

    spec = pl.BlockSpec((None, tm, c), lambda l, i: (l, i, 0))
    shp = jax.ShapeDtypeStruct((L, r, c), f32)
    return _call(body, name=name, grid=(L, r // tm), in_specs=[spec] * 4, out_specs=[spec] * 3,
                 out_shape=[shp] * 3, compiler_params=_params(("parallel", "parallel"), 40 << 20))(w, g, m, v)


def _mesh_pos():
    return lax.axis_index("x"), lax.axis_index("y"), lax.axis_index("c")


def _all_gather(shards, after, name):
    n = len(shards)

    def body(*refs):
        x_refs, o_refs = refs[:n], refs[n + 1:2 * n + 1]
        send_sems, recv_sems, local_sems = refs[2 * n + 1:]
        x, y, c = _mesh_pos()
        me, sibling = (x, y, c), (x, y, 1 - c)
        chips = [(1 - x, y), (x, 1 - y), (1 - x, 1 - y)]

        def rows(k, px, py, pc):
            return o_refs[k].at[4 * px + 2 * py + pc]

        def copy(k, s, block, to, src=None):
            return pltpu.make_async_remote_copy(
                src_ref=rows(k, *block) if src is None else src, dst_ref=rows(k, *block),
                send_sem=send_sems.at[k, s], recv_sem=recv_sems.at[k, s], device_id=to, device_id_type=MESH)

        mine = [pltpu.make_async_copy(x_refs[k], rows(k, *me), local_sems.at[k]) for k in range(n)]
        for cp in mine:
            cp.start()
        first = [copy(k, 0, me, sibling, src=x_refs[k]) for k in range(n)]
        for j, chip in enumerate(chips):
            first += [copy(k, 1 + j, me, (*chip, c), src=x_refs[k]) for k in range(n)]
        for cp in first:
            cp.start()
        passed = []
        for j, chip in enumerate(chips):
            for k in range(n):
                copy(k, 1 + j, (*chip, c), me).wait_recv()
                fwd = copy(k, 4 + j, (*chip, c), sibling)
                fwd.start()
                passed.append(fwd)
        for k in range(n):
            copy(k, 0, sibling, me).wait_recv()
        for j, chip in enumerate(chips):
            for k in range(n):
                copy(k, 4 + j, (*chip, 1 - c), me).wait_recv()
        for cp in first + passed:
            cp.wait_send()
        for cp in mine:
            cp.wait()

    return _call(
        body, name=name, in_specs=[HBM] * n + [ANY], out_specs=[HBM] * n,
        out_shape=[jax.ShapeDtypeStruct((N_DEV,) + s.shape, s.dtype) for s in shards],
        scratch_shapes=[pltpu.SemaphoreType.DMA((n, 7)), pltpu.SemaphoreType.DMA((n, 7)), pltpu.SemaphoreType.DMA((n,))],
    )(*shards, after)


SEM =pl.BlockSpec(memory_space=pltpu.SEMAPHORE)
ANY = pl.BlockSpec(memory_space=pl.ANY)
EFFECT = pltpu.SideEffectType.DATAFLOW_SIDE_EFFECTING


def _other_chips(x, y):
    return [(1 - x, y), (x, 1 - y), (1 - x, 1 - y)]


def _peers(kind, x, y):
    return [(x, y)] if kind == "sibling" else _other_chips(x, y)


def _ici_copy(kind, src_ref, land_ref, send_sem, recv_sem, sender, target, c):
    (sx, sy), (tx, ty) = sender, target
    if kind == "sibling":
        return pltpu.make_async_remote_copy(src_ref=src_ref.at[:, 1 - c], dst_ref=land_ref, send_sem=send_sem,
                                            recv_sem=recv_sem, device_id=(tx, ty, 1 - c), device_id_type=MESH)
    if kind == "gather":
        src, dst = src_ref, land_ref.at[4 * sx + 2 * sy + c]
    else:
        src, dst = src_ref.at[2 * tx + ty], land_ref.at[2 * sx + sy]
    return pltpu.make_async_remote_copy(src_ref=src, dst_ref=dst, send_sem=send_sem, recv_sem=recv_sem,
                                        device_id=(tx, ty, c), device_id_type=MESH)


def _ici_start(kind, srcs, lands, after, name):
    n = len(srcs)
    npeer = 1 if kind == "sibling" else 3

    def body(*refs):
        src_refs, land_refs = refs[:n], refs[n:2 * n]
        send_sems, recv_sems = refs[2 * n + 1], refs[2 * n + 2]
        token = refs[-1]
        x, y, c = _mesh_pos()
        for j, chip in enumerate(_peers(kind, x, y)):
            for k in range(n):
                _ici_copy(kind, src_refs[k], land_refs[k], send_sems.at[npeer * k + j], recv_sems.at[npeer * k + j],(x, y), chip, c).start()
        token[...] = jnp.zeros_like(token)

    bufs = list(srcs) + list(lands)
    return _call(
        body, name=name,
        out_shape=(pltpu.SemaphoreType.DMA((npeer * n,)), pltpu.SemaphoreType.DMA((npeer * n,)),
                   *[pltpu.HBM(b.shape, b.dtype) for b in bufs], jax.ShapeDtypeStruct((8, LANES), f32)),
        in_specs=[HBM] * (2 * n) + [ANY], out_specs=(SEM, SEM, *[HBM] * (2 * n), pl.BlockSpec(memory_space=pltpu.VMEM)),
        input_output_aliases={i: 2 + i for i in range(2 * n)},
        compiler_params=pltpu.CompilerParams(has_side_effects=EFFECT),
    )(*[pltpu.with_memory_space_constraint(b, pltpu.HBM) for b in bufs], after)


def _ici_wait(kind, started, after, name):
    send_sems, recv_sems, *bufs = started[:-1]
    n = len(bufs) // 2
    npeer = 1 if kind == "sibling" else 3

    def body(*refs):
        src_refs, land_refs = refs[:n], refs[n:2 * n]
        send_sems, recv_sems = refs[2 * n], refs[2 * n + 1]
        x, y, c = _mesh_pos()
        for j, chip in enumerate(_peers(kind, x, y)):
            for k in range(n):
                _ici_copy(kind, src_refs[k], land_refs[k], send_sems.at[npeer * k + j], recv_sems.at[npeer * k + j],(x, y), chip, c).wait_send()
                _ici_copy(kind, src_refs[k], land_refs[k], send_sems.at[npeer * k + j], recv_sems.at[npeer * k + j],chip, (x, y), c).wait_recv()

    out = _call(
        body, name=name, out_shape=[pltpu.HBM(b.shape, b.dtype) for b in bufs],
        in_specs=[HBM] * (2 * n) + [SEM, SEM, ANY], out_specs=[HBM] * (2 * n),
        input_output_aliases={i: i for i in range(2 * n)},
        compiler_params=pltpu.CompilerParams(has_side_effects=EFFECT),
    )(*bufs, send_sems, recv_sems, after)
    return out[:n], out[n:]


def _d2d_gather(lands, after, name):
    n = len(lands)

    def body(*refs):
        in_refs, o_refs = refs[:n], refs[n + 1:2 * n + 1]
        send_sems, recv_sems = refs[2 * n + 1:]
        x, y, c = _mesh_pos()
        copies = [pltpu.make_async_remote_copy(
            src_ref=in_refs[k].at[:, c], dst_ref=o_refs[k].at[:, c], send_sem=send_sems.at[k], recv_sem=recv_sems.at[k],
            device_id=(x, y, 1 - c), device_id_type=MESH) for k in range(n)]
        for cp in copies:
            cp.start()
        for k, cp in enumerate(copies):
            cp.wait_send()
            pltpu.make_async_remote_copy(
                src_ref=in_refs[k].at[:, c], dst_ref=o_refs[k].at[:, 1 - c], send_sem=send_sems.at[k],
                recv_sem=recv_sems.at[k], device_id=(x, y, 1 - c), device_id_type=MESH).wait_recv()

    return _call(
        body, name=name, in_specs=[HBM] * n + [ANY], out_specs=[HBM] * n,
        out_shape=[jax.ShapeDtypeStruct(b.shape, b.dtype) for b in lands],
        input_output_aliases={k: k for k in range(n)},
        scratch_shapes=[pltpu.SemaphoreType.DMA((n,)), pltpu.SemaphoreType.DMA((n,))],
    )(*lands, after)


def _sum_chip_slots(lands, sums, chip, name):
    _, r, c = lands.shape
    tm = _row_tile(r, c, 10, 2)

    def body(chip_ref, l_ref, s_ref, o_ref):
        acc = None
        for k in range(N_CHIP):
            part = jnp.where(chip_ref[0] == k, s_ref[k], l_ref[k]).astype(f32)
            acc = part if acc is None else acc + part
        o_ref[...] = acc

    grid_spec = pltpu.PrefetchScalarGridSpec(
        num_scalar_prefetch=1, grid=(r // tm,),
        in_specs=[pl.BlockSpec((N_CHIP, tm, c), lambda i, chip_ref: (0, i, 0)),
                  pl.BlockSpec((N_CHIP, tm, c), lambda i, chip_ref: (0, i, 0))],
        out_specs=pl.BlockSpec((tm, c), lambda i, chip_ref: (i, 0)))
    return _call(body, name=name, grid_spec=grid_spec, out_shape=jax.ShapeDtypeStruct((r, c), f32),
                 compiler_params=_params(("parallel",), 40 << 20))(chip, lands, sums)


def _reduce_begin(grads, tag):
    g4s = [g.reshape(N_CHIP, 2, g.shape[0] // N_DEV, g.shape[1]) for g in grads]
    recvs = [lax.empty((N_CHIP,) + g.shape[2:], g.dtype) for g in g4s]
    return _ici_start("sibling", g4s, recvs, grads[0], name="rs_d2d_start_" + tag)


def _reduce_continue(begun, core, after, tag):
    g4s, recvs = _ici_wait("sibling", begun, after, name="rs_d2d_wait_" + tag)
    sums = [_add_sibling(g4, rv, core, name="rs_add_" + tag) for g4, rv in zip(g4s, recvs)]
    lands = [lax.empty(s.shape, s.dtype) for s in sums]
    return _ici_start("reduce", sums, lands, sums[0], name="rs_start_" + tag)


def _adamw_reduced(layer, w, m, v, lands, sums, chip, prev, name):
    L, r, c = w.shape
    tm = _row_tile(r, c, 11)
    c1 = 1.0 - ADAM_B1 ** ADAM_STEP
    c2 = 1.0 - ADAM_B2 ** ADAM_STEP
    n_prev = 0 if prev is None else 4

    def body(chip_ref, w_ref, m_ref, v_ref, l_ref, s_ref, *refs):
        g_ref, d_ref, mo_ref, vo_ref = refs[n_prev:]
        g_ = None
        for k in range(N_CHIP):
            part = jnp.where(chip_ref[0] == k, s_ref[k], l_ref[k]).astype(f32)
            g_ = part if g_ is None else g_ + part
        m_ = ADAM_B1 * m_ref[...] + (1.0 - ADAM_B1) * g_
        v_ = ADAM_B2 * v_ref[...] + (1.0 - ADAM_B2) * (g_ * g_)
        g_ref[...] = g_
        d_ref[...] = -ADAM_LR * ((m_ / c1) / (jnp.sqrt(v_ / c2) + ADAM_EPS) + ADAM_WD * w_ref[...])
        mo_ref[...] = m_
        vo_ref[...] = v_

    wspec = pl.BlockSpec((None, tm, c), lambda i, chip_ref: (layer, i, 0))
    sspec = pl.BlockSpec((N_CHIP, tm, c), lambda i, chip_ref: (0, i, 0))
    grid_spec = pltpu.PrefetchScalarGridSpec(
        num_scalar_prefetch=1, grid=(r // tm,), in_specs=[wspec] * 3 + [sspec] * 2 + [ANY] * n_prev, out_specs=[wspec] * 4)
    return _call(body, name=name, grid_spec=grid_spec, out_shape=[jax.ShapeDtypeStruct((L, r, c), f32)] * 4,
                 input_output_aliases={6 + i: i for i in range(n_prev)},
                 compiler_params=_params(("parallel",), 40 << 20))(chip, w, m, v, lands, sums, *(prev or ()))


def _rope_tables(S):
    rows = S // GRID_W
    row = jnp.repeat(jnp.arange(rows, dtype=f32), GRID_W)
    col = jnp.tile(jnp.arange(GRID_W, dtype=f32), rows)
    nf = HEAD_DIM // 4
    inv = ROPE_THETA ** (-jnp.arange(nf, dtype=f32) / nf)
    ang = jnp.concatenate([row[:, None] * inv, col[:, None] * inv], axis=-1)
    cos, sin = jnp.cos(ang), jnp.sin(ang)
    return jnp.concatenate([cos, cos], axis=-1), jnp.concatenate([-sin, sin], axis=-1)


def _layer_fwd(xin, p, w, more_weights, cos2, sin2):
    sv = {"xin": xin}
    h = sv["h"] = _rms_fwd(xin, p["g_mix"], name="rms_mix")
    proj = functools.partial(_mm, h, w["in"], "nt", bf16)
    q_raw = sv["q_raw"] = proj(n=Q_COLS, b_off=0, name="proj_q")
    kv_raw = sv["kv_raw"] = proj(n=2 * KV_COLS, b_off=OFF_KV, name="proj_kv")
    conv_in = sv["conv_in"] = proj(n=2 * CONV_CH, b_off=OFF_CONV, name="proj_conv")
    sg_in = sv["sg_in"] = proj(n=2 * SG_CH, b_off=OFF_SG, name="proj_sg")
    gl = sv["gl"] = proj(n=3 * D_MODEL, b_off=OFF_GATE, name="proj_gate")
    qr, kr = sv["qr"], sv["kr"] = _qk_fwd(q_raw, kv_raw, p["q_norm_g"], p["k_norm_g"], cos2, sin2, name="qk_fwd")
    o = sv["o"] = _attn_fwd(qr, kr, kv_raw, name="attn_fwd")
    c = sv["c"] = _conv1_fwd(conv_in, w["dw"], p["b_dw"], name="conv1_fwd")
    cz = sv["cz"] = _conv2_fwd(c, p["conv_ln_g"], p["conv_ln_b"], name="conv2_fwd")
    sz = sv["sz"] = _sgu_fwd(sg_in, p["sg_ln_g"], p["sg_ln_b"], p["w_s"], p["b_s"], name="sgu_fwd")
    w = {**w, **more_weights(1, sz)}
    ya = sv["ya"] = _mm(o, w["attn_o"], "nt", bf16, name="out_attn")
    yc = sv["yc"] = _mm(cz, w["conv_o"], "nt", bf16, name="out_conv")
    ys = sv["ys"] = _mm(sz, w["sg_o"], "nt", bf16, name="out_sg")
    merged = sv["merged"] = _merge_fwd(gl, p["b_gate"], ya, yc, ys, name="merge_fwd")
    x1 = sv["x1"] = _mm(merged, w["out"], "nn", f32, res=xin, name="out_proj")
    w = {**w, **more_weights(2, x1)}
    hf = sv["hf"] = _rms_fwd(x1, p["g_ffn"], name="rms_ffn")
    sv["fg"], sv["fu"], act = _ffn_up(hf, w["ff_gate"], w["ff_up"], name="ffn_up")
    sv["act"] = act
    x2 =_mm(act, w["ff_down"], "nn", f32, res=x1, name="ff_down")
    return x2, sv, w


def _layer_bwd(dx2, dx2b, sv, p, w, cos2, sin2, reduce_begin, reduce_continue, last):
    small = {}
    dfg, dfu = _ffn_down_bwd(dx2b, w["ff_down"], sv["fg"], sv["fu"], name="ffn_down_bwd")
    g_down = _mm(sv["act"], dx2b, "tn", bf16, name="g_ff_down")
    dhf = _mm(dfg, w["ff_gate"], "nn", f32, name="d_hf_gate")
    dhf = _mm(dfu, w["ff_up"], "nn", f32, res=dhf, name="d_hf_up")
    g_gate = _mm(dfg, sv["hf"], "tn", bf16, name="g_ff_gate")
    g_up = _mm(dfu, sv["hf"], "tn", bf16, name="g_ff_up")
    zero = reduce_begin("ffn", dict(w_ff_gate=g_gate, w_ff_up=g_up, w_ff_down=g_down))[0, 0]
    dx1, dx1b, small["g_ffn"] = _rms_bwd(sv["x1"], p["g_ffn"] + zero, dhf, dx2, name="rms_ffn_bwd")
    g_out = _mm(sv["merged"], dx1b, "tn", bf16, name="g_out")
    *dgl, dya, dyc, dys, db0, db1, db2 = _merge_bwd_fused(dx1b, w["out"], sv["gl"], p["b_gate"], sv["ya"], sv["yc"], sv["ys"],
                                                        name="merge_bwd")
    small["b_gate"] = jnp.concatenate([db0, db1, db2], axis=1)
    do = _mm(dya, w["attn_o"], "nn", bf16, after=reduce_continue("ffn", dya), name="d_o")
    g_ao = _mm(dya, sv["o"], "tn", bf16, name="g_attn_o")
    dcz = _mm(dyc, w["conv_o"], "nn", bf16, name="d_cz")
    g_co = _mm(dyc, sv["cz"], "tn", bf16, name="g_conv_o")
    dsz = _mm(dys, w["sg_o"], "nn", bf16, name="d_sz")
    g_so = _mm(dys, sv["sz"], "tn", bf16, name="g_sg_o")
    zero = reduce_begin("mix", dict(w_attn_o=g_ao, w_conv_o=g_co, w_sg_o=g_so, w_out=g_out))[0, 0]
    dsu, dsv, small["w_s"], small["b_s"], small["sg_ln_g"], small["sg_ln_b"] = _sgu_bwd(
        sv["sg_in"], dsz, p["sg_ln_g"] + zero, p["sg_ln_b"], p["w_s"], p["w_s_t"], p["b_s"], name="sgu_bwd")
    dc, small["conv_ln_g"], small["conv_ln_b"] = _conv2_bwd(sv["c"], dcz, p["conv_ln_g"], p["conv_ln_b"], name="conv2_bwd")
    da, dgt, small["w_dw"], small["b_dw"] = _conv1_bwd(sv["conv_in"], dc, w["dw"], name="conv1_bwd")
    zero = reduce_continue("mix", da)[0, 0]
    dqr, dkr, dv = _attn_bwd(sv["qr"], sv["kr"], sv["kv_raw"], do, name="attn_bwd")
    dq_raw, dk_raw, small["q_norm_g"], small["k_norm_g"] = _qk_bwd(
        sv["q_raw"], sv["kv_raw"], dqr, dkr, p["q_norm_g"] + zero, p["k_norm_g"], cos2, sin2, name="qk_bwd")
    dproj = jnp.concatenate([dq_raw, dk_raw, dv.astype(bf16), da, dgt, dsu, dsv, *dgl], axis=1)
    g_in = _mm(dproj, sv["h"], "tn", bf16, name="g_in")
    begun = reduce_begin("in", dict(w_in=g_in))
    if last:
        begun = reduce_continue("in", begun)
    dh = _mm(dproj, w["in"], "nn", f32, after=begun, name="d_h")
    zero = begun[0, 0] if last else reduce_continue("in", dh)[0, 0]
    dx, dxb, small["g_mix"] = _rms_bwd(sv["xin"], p["g_mix"] + zero, dh, dx1, name="rms_mix_bwd")
    return dx, dxb, small


SMALL = ("g_mix", "b_gate", "q_norm_g", "k_norm_g", "b_dw", "conv_ln_g", "conv_ln_b", "sg_ln_g", "sg_ln_b",
         "w_s", "b_s", "g_ffn")
PACK_ALIGN = 8 * LANES


def _pack(parts):
    flat = jnp.concatenate([a.reshape(-1).astype(f32) for a in parts])
    pad = -flat.shape[0] % PACK_ALIGN
    return jnp.pad(flat, (0, pad)).reshape(-1, LANES)


def _unpack(buf, shapes):
    flat = buf.reshape(-1)
    out, pos = [], 0
    for shp in shapes:
        size = math.prod(shp)
        out.append(flat[pos:pos + size].reshape(shp))
        pos += size
    return out


def kernel(x, g_mix, w_in, b_gate, q_norm_g, k_norm_g, w_attn_o, w_dw, b_dw, conv_ln_g, conv_ln_b, w_conv_o, sg_ln_g, sg_ln_b, w_s, b_s, w_sg_o, w_out, g_ffn, w_ff_gate, w_ff_up, w_ff_down, g_final, loss_target, m_g_mix, m_w_in, m_b_gate, m_q_norm_g, m_k_norm_g, m_w_attn_o, m_w_dw, m_b_dw, m_conv_ln_g, m_conv_ln_b, m_w_conv_o, m_sg_ln_g, m_sg_ln_b, m_w_s, m_b_s, m_w_sg_o, m_w_out, m_g_ffn, m_w_ff_gate, m_w_ff_up, m_w_ff_down, m_g_final, v_g_mix, v_w_in, v_b_gate, v_q_norm_g, v_k_norm_g, v_w_attn_o, v_w_dw, v_b_dw, v_conv_ln_g, v_conv_ln_b, v_w_conv_o, v_sg_ln_g, v_sg_ln_b, v_w_s, v_b_s, v_w_sg_o, v_w_out, v_g_ffn, v_w_ff_gate, v_w_ff_up, v_w_ff_down, v_g_final):
    weights = dict(g_mix=g_mix, w_in=w_in, b_gate=b_gate, q_norm_g=q_norm_g, k_norm_g=k_norm_g, w_attn_o=w_attn_o,
                   w_dw=w_dw, b_dw=b_dw, conv_ln_g=conv_ln_g, conv_ln_b=conv_ln_b, w_conv_o=w_conv_o, sg_ln_g=sg_ln_g,
                   sg_ln_b=sg_ln_b, w_s=w_s, b_s=b_s, w_sg_o=w_sg_o, w_out=w_out, g_ffn=g_ffn, w_ff_gate=w_ff_gate,
                   w_ff_up=w_ff_up, w_ff_down=w_ff_down, g_final=g_final)
    mom_m = dict(g_mix=m_g_mix, w_in=m_w_in, b_gate=m_b_gate, q_norm_g=m_q_norm_g, k_norm_g=m_k_norm_g,
                 w_attn_o=m_w_attn_o, w_dw=m_w_dw, b_dw=m_b_dw, conv_ln_g=m_conv_ln_g, conv_ln_b=m_conv_ln_b,
                 w_conv_o=m_w_conv_o, sg_ln_g=m_sg_ln_g, sg_ln_b=m_sg_ln_b, w_s=m_w_s, b_s=m_b_s, w_sg_o=m_w_sg_o,
                 w_out=m_w_out, g_ffn=m_g_ffn, w_ff_gate=m_w_ff_gate, w_ff_up=m_w_ff_up, w_ff_down=m_w_ff_down,
                 g_final=m_g_final)
    mom_v = dict(g_mix=v_g_mix, w_in=v_w_in, b_gate=v_b_gate, q_norm_g=v_q_norm_g, k_norm_g=v_k_norm_g,
                 w_attn_o=v_w_attn_o, w_dw=v_w_dw, b_dw=v_b_dw, conv_ln_g=v_conv_ln_g, conv_ln_b=v_conv_ln_b,
                 w_conv_o=v_w_conv_o, sg_ln_g=v_sg_ln_g, sg_ln_b=v_sg_ln_b, w_s=v_w_s, b_s=v_b_s, w_sg_o=v_w_sg_o,
                 w_out=v_w_out, g_ffn=v_g_ffn, w_ff_gate=v_w_ff_gate, w_ff_up=v_w_ff_up, w_ff_down=v_w_ff_down,
                 g_final=v_g_final)
    S, D = x.shape[1], x.shape[2]
    xi, yi, ci = _mesh_pos()
    me = 4 * xi + 2 * yi + ci
    core = jnp.reshape(ci, (1,)).astype(jnp.int32)
    cos2, sin2 = _rope_tables(S)

    big = ("w_in", "w_attn_o", "w_conv_o", "w_sg_o", "w_out", "w_ff_gate", "w_ff_up", "w_ff_down")
    transposed = {"w_in", "w_attn_o", "w_conv_o", "w_sg_o", "w_ff_gate", "w_ff_up"}
    chip = jnp.reshape(2 * xi + yi, (1,)).astype(jnp.int32)
    groups = (("in", "dw"), ("attn_o", "conv_o", "sg_o", "out"), ("ff_gate", "ff_up", "ff_down"))
    P, shards = [], []
    for l in range(DEPTH):
        sh = {n[2:]: (weights[n][l].T if n in transposed else weights[n][l]).astype(bf16) for n in big}
        sh["dw"] = jnp.pad(w_dw[l].reshape(CONV_W, LANES), ((0, CONV_WP - CONV_W), (0, 0)))
        shards.append(sh)
        p = {n: weights[n][l].reshape(1, -1) for n in SMALL if n not in ("w_s", "b_s")}
        p["w_s"] = w_s[l]
        p["w_s_t"] = jnp.swapaxes(w_s[l], 1, 2)
        p["b_s"] = b_s[l].reshape(SG_G, SG_CHUNK, 1)
        P.append(p)

    gathers = {}

    def start_gather(l, gi, after):
        srcs = [shards[l][n] for n in groups[gi]]
        lands = [lax.dynamic_update_index_in_dim(lax.empty((N_DEV,) + s.shape, s.dtype), s, me, 0) for s in srcs]
        gathers[l, gi] = _ici_start("gather", srcs, lands, after, name=f"ag_start_{l}{gi}")
        return gathers[l, gi][-1]

    def gathered(l, gi, after):
        srcs, lands = _ici_wait("gather", gathers[l, gi], after, name=f"ag_wait_{l}{gi}")
        after = srcs[0]
        if gi == len(groups) - 1 and l + 1 < DEPTH:
            for gj in range(len(groups)):
                after = start_gather(l + 1, gj, after)
        full = _d2d_gather([b.reshape(N_CHIP, 2, *b.shape[1:]) for b in lands], after, name=f"ag_d2d_{gi}")
        return {n: f.reshape(-1, f.shape[3]) for n, f in zip(groups[gi], full)}

    all_started = cos2
    for gi in range(len(groups)):
        all_started = start_gather(0, gi, all_started)

    h = x.reshape(S, D)
    saved, W = [], []
    for l in range(DEPTH):
        first = gathered(l, 0, all_started if l == 0 else h)
        if l == 0:
            P[l]["g_mix"] = P[l]["g_mix"] + all_started[0, 0]
        h, sv, w = _layer_fwd(h, P[l], first, functools.partial(lambda gi, z, l: gathered(l, gi, z), l=l), cos2, sin2)
        saved.append(sv)
        W.append(w)
    dx, dxb, sq, g_final_part = _final_loss(h, g_final.reshape(1, D), loss_target.reshape(S, D), name="final_loss")
    loss = lax.psum(0.5 * jnp.sum(sq) / D, ("x", "y", "c"))

    begun, reductions, small_grads = {}, {}, [None] * DEPTH
    for l in reversed(range(DEPTH)):
        def reduce_begin(group, grads, l=l):
            begun[l, group] = (tuple(grads), _reduce_begin(list(grads.values()), tag=f"{group}{l}"))
            return begun[l, group][1][-1]

        def reduce_continue(group, after, l=l):
            names, started = begun[l, group]
            reductions[l, group] = (names, _reduce_continue(started, core, after, tag=f"{group}{l}"))
            return reductions[l, group][1][-1]

        dx, dxb, small_grads[l] = _layer_bwd(dx, dxb, saved[l], P[l], W[l], cos2, sin2, reduce_begin, reduce_continue,
                                            last=(l == 0))
    grad_x = dx.reshape(x.shape)

    grads_out, delta, new_m, new_v = {}, {}, {}, {}
    swap = lambda a: jnp.swapaxes(a, 1, 2)

    def update(n, lands, sums):
        as_arrives = n not in transposed or weights[n].shape[2] % LANES != 0
        if as_arrives:
            to_arrival = swap if n in transposed else (lambda a: a)
            out = None
            for l in reversed(range(DEPTH)):
                out = _adamw_reduced(l, to_arrival(weights[n]), to_arrival(mom_m[n]), to_arrival(mom_v[n]),
                                     lands[l], sums[l], chip, out, name=f"adamw_{n}_{l}")
            grads_out[n], delta[n], new_m[n], new_v[n] = [to_arrival(o) for o in out]
        else:
            g = jnp.stack([_sum_chip_slots(lands[l], sums[l], chip, name="rs_sum_" + n) for l in range(DEPTH)])
            grads_out[n] = swap(g)
            delta[n], new_m[n], new_v[n] = _adamw(weights[n], grads_out[n], mom_m[n], mom_v[n], name="adamw_" + n)
        return delta[n]

    after = dx
    for group in ("ffn", "mix", "in"):
        names = reductions[0, group][0]
        arrived = [_ici_wait("reduce", reductions[l, group][1], after, name=f"rs_wait_{group}{l}") for l in range(DEPTH)]
        for i, n in enumerate(names):
            after = update(n, [arrived[l][1][i] for l in range(DEPTH)], [arrived[l][0][i] for l in range(DEPTH)])

    small_shapes = [weights[n].shape for n in SMALL] + [g_final.shape, (DEPTH, CONV_CH // LANES, CONV_WP, LANES)]
    parts = [jnp.stack([small_grads[l][n].reshape(weights[n].shape[1:]) for l in range(DEPTH)]) for n in SMALL]
    parts += [g_final_part.reshape(g_final.shape), jnp.stack([small_grads[l]["w_dw"] for l in range(DEPTH)])]
    packed = _pack(parts)
    gathered = _all_gather([packed], after, name="gather_small")[0]
    total = _sum_slots(gathered, name="sum_small")
    small_total = _unpack(total, small_shapes)
    grads_out.update(zip(SMALL + ("g_final",), small_total[:-1]))
    dw_full = small_total[-1]
    grads_out["w_dw"] = lax.dynamic_index_in_dim(dw_full, me, axis=1, keepdims=False)[:, :CONV_W].reshape(w_dw.shape)

    rep = tuple(n for n in SMALL if n != "w_s") + ("g_final",)
    rep_shapes = [weights[n].shape for n in rep]
    packs = [_pack([src[n] for n in rep])[None] for src in (weights, grads_out, mom_m, mom_v)]
    for dst, buf in zip((delta, new_m, new_v), _adamw(*packs, name="adamw_small")):
        dst.update(zip(rep, _unpack(buf[0], rep_shapes)))
    for n, shp in (("w_dw", (1, DEPTH * CONV_W, LANES)), ("w_s", (DEPTH, SG_G * SG_CHUNK, SG_CHUNK))):
        upd = _adamw(*[src[n].reshape(shp) for src in (weights, grads_out, mom_m, mom_v)], name="adamw_" + n)
        for dst, buf in zip((delta, new_m, new_v), upd):
            dst[n] = buf.reshape(weights[n].shape)

    order = ("g_mix", "w_in", "b_gate", "q_norm_g", "k_norm_g", "w_attn_o", "w_dw", "b_dw", "conv_ln_g", "conv_ln_b",
             "w_conv_o", "sg_ln_g", "sg_ln_b", "w_s", "b_s", "w_sg_o", "w_out", "g_ffn", "w_ff_gate", "w_ff_up",
             "w_ff_down", "g_final")
    return (loss, grad_x, *[grads_out[n] for n in order], *[delta[n] for n in order],
            *[new_m[n] for n in order], *[new_v[n] for n in order])
```

```python
import functools
import math

import jax
import jax.numpy as jnp
from jax import lax
from jax.experimental import pallas as pl
from jax.experimental.pallas import tpu as pltpu

f32, bf16 = jnp.float32, jnp.bfloat16

D_MODEL = 2048
SEQ = 2048
DEPTH = 2
GRID_W = 64
HEAD_DIM = 128
LANES = 128
N_Q = (D_MODEL // 2) // HEAD_DIM
N_KV = N_Q // 4
GRP = N_Q // N_KV
Q_COLS = N_Q * HEAD_DIM
KV_COLS = N_KV * HEAD_DIM
CONV_CH = D_MODEL // 2
CONV_W = 31
CONV_PAD = CONV_W // 2
CONV_WP = 32
SG_CH = D_MODEL // 2
SG_G = SG_CH // LANES
SG_CHUNK = 128
D_FF = -(-8 * D_MODEL // (3 * 256)) * 256
OFF_KV = Q_COLS
OFF_CONV = OFF_KV + 2 * KV_COLS
OFF_SG = OFF_CONV + 2 * CONV_CH
OFF_GATE = OFF_SG + 2 * SG_CH
IN_COLS = OFF_GATE + 3 * D_MODEL
ROPE_THETA = 10000.0
SCALE = HEAD_DIM ** -0.5
N_DEV = 8
N_CHIP = 4

ADAM_LR, ADAM_B1, ADAM_B2, ADAM_EPS, ADAM_WD, ADAM_STEP = 0.001, 0.9, 0.999, 1e-08, 0.01, 10

VMEM_BYTES_V7X = 64 << 20
VMEM_CAP = VMEM_BYTES_V7X - (6 << 20)
MESH = pl.DeviceIdType.MESH
HBM = pl.BlockSpec(memory_space=pltpu.HBM)


def _in_hbm(a):
    if isinstance(a, jax.Array) and jnp.issubdtype(a.dtype, jnp.floating) and a.size * a.dtype.itemsize >= (1 << 20):
        return pltpu.with_memory_space_constraint(a, pltpu.HBM)
    return a


def _out_hbm(s):
    if isinstance(s, jax.ShapeDtypeStruct) and math.prod(s.shape) * jnp.dtype(s.dtype).itemsize >= (1 << 20):
        return pltpu.HBM(s.shape, s.dtype)
    return s


def _call(body, **kw):
    shapes = kw.pop("out_shape")
    shapes = type(shapes)(_out_hbm(s) for s in shapes) if isinstance(shapes, (list, tuple)) else _out_hbm(shapes)
    call = pl.pallas_call(body, out_shape=shapes, **kw)
    return lambda *args: call(*[_in_hbm(a) for a in args])


def _pick(n, cands):
    for c in cands:
        if n % c == 0:
            return c
    raise ValueError((n, cands))


def _params(sem, vmem_bytes):
    return pltpu.CompilerParams(dimension_semantics=sem, vmem_limit_bytes=int(min(max(vmem_bytes, 16 << 20), VMEM_CAP)))


def _mm(a, b, form, out_dtype, *, n=None, b_off=0, res=None, after=None, name):
    if form == "tn":
        K, M = a.shape
    else:
        M, K = a.shape
    N = n if n is not None else (b.shape[0] if form == "nt" else b.shape[1])
    if K <= 2048:
        tk = K
        if form == "tn":
            tm = _pick(M, (512, 256, 128))
            tn = N if N <= 2048 else _pick(N, (1024, 512, 256, 128))
        else:
            tm = M if M <= 2048 else _pick(M, (2048, 1024, 512))
            tn = _pick(math.gcd(N, b_off) if b_off else N, (256, 128) if res is not None else (512, 256, 128))
    else:
        tk = max(t for t in range(LANES, 3072 + 1, LANES) if K % t == 0)
        tm = _pick(M, (1024, 512, 256, 128))
        tn = _pick(math.gcd(N, b_off) if b_off else N, (1024, 512, 256, 128))
    assert b_off % tn == 0
    off = b_off // tn
    nk = K // tk
    if form == "tn":
        a_spec = pl.BlockSpec((tk, tm), lambda i, j, k: (k, i))
    else:
        a_spec = pl.BlockSpec((tm, tk), lambda i, j, k: (i, k))
    if form == "nt":
        b_spec = pl.BlockSpec((tn, tk), lambda i, j, k: (j + off, k))
    else:
        b_spec = pl.BlockSpec((tk, tn), lambda i, j, k: (k, j + off))
    dims = {"nn": ((1,), (0,)), "nt": ((1,), (1,)), "tn": ((0,), (0,))}[form]
    has_res = res is not None

    def body(*refs):
        if after is not None:
            refs = refs[1:]
        if has_res:
            a_ref, b_ref, r_ref, o_ref = refs[:4]
        else:
            a_ref, b_ref, o_ref = refs[:3]
        p = lax.dot_general(a_ref[...], b_ref[...], (dims, ((), ())), preferred_element_type=f32)

        def finish(acc):
            if has_res:
                acc = acc + r_ref[...].astype(f32)
            o_ref[...] = acc.astype(o_ref.dtype)

        if nk == 1:
            finish(p)
        else:
            acc_ref = refs[-1]
            k = pl.program_id(2)

            @pl.when(k == 0)
            def _():
                acc_ref[...] = p

            @pl.when(k > 0)
            def _():
                acc_ref[...] += p

            @pl.when(k == nk - 1)
            def _():
                finish(acc_ref[...])

    in_specs = [a_spec, b_spec]
    args = [a, b]
    osz = jnp.dtype(out_dtype).itemsize
    vmem = 2 * (tm * tk * 2 + tk * tn * 2 + tm * tn * osz) + 2 * tm * tn * 4
    if has_res:
        in_specs.append(pl.BlockSpec((tm, tn), lambda i, j, k: (i, j)))
        args.append(res)
        vmem += 2 * tm * tn * res.dtype.itemsize
    scratch = []
    if nk > 1:
        scratch.append(pltpu.VMEM((tm, tn), f32))
        vmem += tm * tn * 4
    if after is not None:
        in_specs.insert(0, pl.BlockSpec(memory_space=pl.ANY))
        args.insert(0, after)
    return _call(
        body, name=name, grid=(M // tm, N // tn, nk),
        in_specs=in_specs, out_specs=pl.BlockSpec((tm, tn), lambda i, j, k: (i, j)),
        out_shape=jax.ShapeDtypeStruct((M, N), out_dtype), scratch_shapes=scratch,
        compiler_params=_params(("parallel", "parallel", "arbitrary"), vmem + (8 << 20)),
    )(*args)


EPI_TN = 256


def _mm_epi(a, bs, form, extras, out_dtypes, n_sums, fn, name):
    M, K = a.shape
    N = bs[0].shape[0] if form == "nt" else bs[0].shape[1]
    tn = EPI_TN
    assert K <= 2048 and N % tn == 0
    dims = ((1,), (1,)) if form == "nt" else ((1,), (0,))
    nb, ne = len(bs), len(extras)

    def body(*refs):
        a_ref, b_refs, e_refs, o_refs = refs[0], refs[1:1 + nb], refs[1 + nb:1 + nb + ne], refs[1 + nb + ne:]
        av = a_ref[...]
        ps = [lax.dot_general(av, b[...], (dims, ((), ())), preferred_element_type=f32) for b in b_refs]
        for o_ref, o in zip(o_refs, fn(ps, [e[...] for e in e_refs])):
            o_ref[...] = o.astype(o_ref.dtype)

    in_specs = [pl.BlockSpec((M, K), lambda j: (0, 0), pipeline_mode=pl.Buffered(1))]
    in_specs += [pl.BlockSpec((tn, K), lambda j: (j, 0)) if form == "nt" else pl.BlockSpec((K, tn), lambda j: (0, j))
                 for _ in bs]
    for arr, first in extras:
        assert first % tn == 0
        in_specs.append(pl.BlockSpec((arr.shape[0], tn), functools.partial(lambda j, o: (0, j + o), o=first // tn)))
    out_specs = [pl.BlockSpec((M, tn), lambda j: (0, j))] * len(out_dtypes) + [pl.BlockSpec((1, tn), lambda j: (0, j))] * n_sums
    out_shape = [jax.ShapeDtypeStruct((M, N), dt) for dt in out_dtypes] + [jax.ShapeDtypeStruct((1, N), f32)] * n_sums
    tiles = sum(arr.shape[0] * tn * arr.dtype.itemsize for arr, _ in extras) + sum(M * tn * jnp.dtype(dt).itemsize for dt in out_dtypes)
    vmem = M * K * 2 + 2 * nb * tn * K * 2 + 2 * tiles + (nb + 6) * M * tn * 4
    return _call(body, name=name, grid=(N // tn,), in_specs=in_specs, out_specs=out_specs, out_shape=out_shape,
                 compiler_params=_params(("parallel",), vmem + (8 << 20)))(a, *bs, *[arr for arr, _ in extras])


def _ffn_up(hf, wt_gate, wt_up, name):
    def fn(ps, _):
        g, u = ps[0].astype(bf16), ps[1].astype(bf16)
        gf = g.astype(f32)
        return g, u, gf * jax.nn.sigmoid(gf) * u.astype(f32)

    return _mm_epi(hf, [wt_gate, wt_up], "nt", [], [bf16] * 3, 0, fn, name)


def _ffn_down_bwd(dx2b, w_down, fg, fu, name):
    def fn(ps, es):
        d, g = ps[0], es[0].astype(f32)
        sg = jax.nn.sigmoid(g)
        return d * es[1].astype(f32) * sg * (1.0 + g * (1.0 - sg)), d * g * sg

    return _mm_epi(dx2b, [w_down], "nt", [(fg, 0), (fu, 0)], [bf16] * 2, 0, fn, name)


def _merge_bwd_fused(dx1b, w_out, gl, b_gate, ya, yc, ys, name):
    D = ya.shape[1]

    def fn(ps, es):
        dm_, outs, sums = ps[0], [], []
        for i in range(3):
            gate = jax.nn.sigmoid(es[i].astype(f32) + es[3 + i])
            dlog = dm_ * es[6 + i].astype(f32) * gate * (1.0 - gate)
            outs.append((dlog, dm_ * gate))
            sums.append(jnp.sum(dlog, axis=0, keepdims=True))
        return [o[0] for o in outs] + [o[1] for o in outs] + sums

    extras = [(gl, i * D) for i in range(3)] + [(b_gate, i * D) for i in range(3)] + [(ya, 0), (yc, 0), (ys, 0)]
    return _mm_epi(dx1b, [w_out], "nt", extras, [bf16] * 6, 3, fn, name)


def _rows(body, ins, outs, *, tm, name, vmem=40 << 20):
    nrows = next(s[1].shape[0] for s in ins if s[0] == "r")
    in_specs, args = [], []
    for s in ins:
        arr = s[1]
        if s[0] == "r":
            w = s[2] if len(s) > 2 else arr.shape[1]
            cb = s[3] if len(s) > 3 else 0
            in_specs.append(pl.BlockSpec((tm, w), functools.partial(lambda i, cb: (i, cb), cb=cb)))
        else:
            in_specs.append(pl.BlockSpec(arr.shape, functools.partial(lambda i, nd: (0,) * nd, nd=arr.ndim)))
        args.append(arr)
    out_specs, out_shape = [], []
    for s in outs:
        if s[0] == "r":
            out_specs.append(pl.BlockSpec((tm, s[1]), lambda i: (i, 0)))
            out_shape.append(jax.ShapeDtypeStruct((nrows, s[1]), s[2]))
        else:
            out_specs.append(pl.BlockSpec(s[1], functools.partial(lambda i, nd: (0,) * nd, nd=len(s[1]))))
            out_shape.append(jax.ShapeDtypeStruct(s[1], s[2]))
    return _call(body, name=name, grid=(nrows // tm,), in_specs=in_specs, out_specs=out_specs,
                 out_shape=out_shape, compiler_params=_params(("arbitrary",), vmem))(*args)


def _accumulate(ref, part):
    i = pl.program_id(0)

    @pl.when(i == 0)
    def _():
        ref[...] = part

    @pl.when(i > 0)
    def _():
        ref[...] += part


def _rms_stats(x):
    r = lax.rsqrt(jnp.mean(x * x, axis=-1, keepdims=True) + 1e-6)
    return r, x * r


def _rms_fwd(x, g, name):
    def body(x_ref, g_ref, o_ref):
        _, xn = _rms_stats(x_ref[...])
        o_ref[...] = (xn * g_ref[...]).astype(o_ref.dtype)

    return _rows(body, [("r", x), ("f", g)], [("r", x.shape[1], bf16)], tm=min(256, x.shape[0]), name=name)[0]


def _rms_bwd(x, g, dh, dres, name):
    D = x.shape[1]

    def body(x_ref, g_ref, dh_ref, dr_ref, dx_ref, dxb_ref, dg_ref):
        r, xn = _rms_stats(x_ref[...])
        dy = dh_ref[...].astype(f32)
        dxn = dy * g_ref[...]
        dx = dr_ref[...] + r * (dxn - xn * jnp.mean(dxn * xn, axis=-1, keepdims=True))
        dx_ref[...] = dx
        dxb_ref[...] = dx.astype(bf16)
        _accumulate(dg_ref, jnp.sum(dy * xn, axis=0, keepdims=True))

    return _rows(body, [("r", x), ("f", g), ("r", dh), ("r", dres)],
                 [("r", D, f32), ("r", D, bf16), ("a", (1, D), f32)], tm=min(256, x.shape[0]), name=name)


def _final_loss(x, g, tgt, name):
    D = x.shape[1]

    def body(x_ref, g_ref, t_ref, dx_ref, dxb_ref, sq_ref, dg_ref):
        r, xn = _rms_stats(x_ref[...])
        gain = g_ref[...]
        diff = xn * gain - t_ref[...]
        dy = diff * (1.0 / D)
        dxn = dy * gain
        dx = r * (dxn - xn * jnp.mean(dxn * xn, axis=-1, keepdims=True))
        dx_ref[...] = dx
        dxb_ref[...] = dx.astype(bf16)
        _accumulate(sq_ref, jnp.sum(diff * diff, axis=0, keepdims=True))
        _accumulate(dg_ref, jnp.sum(dy * xn, axis=0, keepdims=True))

    return _rows(body, [("r", x), ("f", g), ("r", tgt)],
                 [("r", D, f32), ("r", D, bf16), ("a", (1, D), f32), ("a", (1, D), f32)],
                 tm=min(256, x.shape[0]), name=name)


def _qk_fwd(q_raw, kv_raw, qg, kg, cos2, sin2, name):
    def body(q_ref, k_ref, qg_ref, kg_ref, c_ref, s_ref, qo_ref, ko_ref):
        c, s = c_ref[...], s_ref[...]

        def head(src, gain, dst, h):
            cols = slice(h * HEAD_DIM, (h + 1) * HEAD_DIM)
            _, xn = _rms_stats(src[:, cols].astype(f32))
            y = xn * gain
            dst[:, cols] = (y * c + pltpu.roll(y, HEAD_DIM // 2, 1) * s).astype(dst.dtype)

        for h in range(N_Q):
            head(q_ref, qg_ref[...], qo_ref, h)
        for h in range(N_KV):
            head(k_ref, kg_ref[...], ko_ref, h)

    return _rows(body, [("r", q_raw), ("r", kv_raw, KV_COLS, 0), ("f", qg), ("f", kg), ("r", cos2), ("r", sin2)],
                 [("r", Q_COLS, bf16), ("r", KV_COLS, bf16)], tm=min(256, q_raw.shape[0]), name=name)


def _qk_bwd(q_raw, kv_raw, dqr, dkr, qg, kg, cos2, sin2, name):
    def body(q_ref, k_ref, dq_ref, dk_ref, qg_ref, kg_ref, c_ref, s_ref, dqo_ref, dko_ref, dqg_ref, dkg_ref):
        c, s = c_ref[...], s_ref[...]

        def head(src, dsrc, gain, dst, h):
            cols = slice(h * HEAD_DIM, (h + 1) * HEAD_DIM)
            r, xn = _rms_stats(src[:, cols].astype(f32))
            do = dsrc[:, cols].astype(f32)
            dy = do * c + pltpu.roll(do * s, HEAD_DIM // 2, 1)
            dxn = dy * gain
            dst[:, cols] = (r * (dxn - xn * jnp.mean(dxn * xn, axis=-1, keepdims=True))).astype(dst.dtype)
            return jnp.sum(dy * xn, axis=0, keepdims=True)

        dq_gain = head(q_ref, dq_ref, qg_ref[...], dqo_ref, 0)
        for h in range(1, N_Q):
            dq_gain = dq_gain + head(q_ref, dq_ref, qg_ref[...], dqo_ref, h)
        dk_gain = head(k_ref, dk_ref, kg_ref[...], dko_ref, 0)
        for h in range(1, N_KV):
            dk_gain = dk_gain + head(k_ref, dk_ref, kg_ref[...], dko_ref, h)
        _accumulate(dqg_ref, dq_gain)
        _accumulate(dkg_ref, dk_gain)

    return _rows(body, [("r", q_raw), ("r", kv_raw, KV_COLS, 0), ("r", dqr), ("r", dkr), ("f", qg), ("f", kg),
                        ("r", cos2), ("r", sin2)],
                 [("r", Q_COLS, bf16), ("r", KV_COLS, bf16), ("a", (1, HEAD_DIM), f32), ("a", (1, HEAD_DIM), f32)],
                 tm=min(256, q_raw.shape[0]), name=name)


def _softmax_rows(q, k):
    s = lax.dot_general(q, k, (((1,), (1,)), ((), ())), preferred_element_type=f32) * SCALE
    p = jnp.exp(s - jnp.max(s, axis=-1, keepdims=True))
    return p * (1.0 / jnp.sum(p, axis=-1, keepdims=True))


def _head_cols(g):
    return slice(g * HEAD_DIM, (g + 1) * HEAD_DIM)


def _attn_fwd(qr, kr, kv_raw, name):
    S = qr.shape[0]
    tq = min(256, S)

    def body(q_ref, k_ref, v_ref, o_ref):
        k, v = k_ref[...], v_ref[...]
        for g in range(GRP):
            p = _softmax_rows(q_ref[:, _head_cols(g)], k)
            o_ref[:, _head_cols(g)] = jnp.dot(p.astype(bf16), v, preferred_element_type=f32).astype(o_ref.dtype)

    return _call(
        body, name=name, grid=(N_KV, S // tq),
        in_specs=[pl.BlockSpec((tq, GRP * HEAD_DIM), lambda kv, i: (i, kv)),
                  pl.BlockSpec((S, HEAD_DIM), lambda kv, i: (0, kv)),
                  pl.BlockSpec((S, HEAD_DIM), lambda kv, i: (0, N_KV + kv))],
        out_specs=pl.BlockSpec((tq, GRP * HEAD_DIM), lambda kv, i: (i, kv)),
        out_shape=jax.ShapeDtypeStruct((S, Q_COLS), bf16),
        compiler_params=_params(("parallel", "arbitrary"), 4 * GRP * tq * S * 4 + (8 << 20)),
    )(qr, kr, kv_raw)


def _attn_bwd(qr, kr, kv_raw, do, name):
    S = qr.shape[0]
    tq = min(256, S)

    def body(q_ref, k_ref, v_ref, do_ref, dq_ref, dk_ref, dv_ref):
        first = pl.program_id(1) == 0
        k, v = k_ref[...], v_ref[...]
        dv_part = dk_part = None
        for g in range(GRP):
            q, do_ = q_ref[:, _head_cols(g)], do_ref[:, _head_cols(g)]
            p = _softmax_rows(q, k)
            dp = lax.dot_general(do_, v, (((1,), (1,)), ((), ())), preferred_element_type=f32)
            ds = (p * (dp - jnp.sum(dp * p, axis=-1, keepdims=True)) * SCALE).astype(bf16)
            dq_ref[:, _head_cols(g)] = jnp.dot(ds, k, preferred_element_type=f32).astype(dq_ref.dtype)
            dv_g = lax.dot_general(p.astype(bf16), do_, (((0,), (0,)), ((), ())), preferred_element_type=f32)
            dk_g = lax.dot_general(ds, q, (((0,), (0,)), ((), ())), preferred_element_type=f32)
            dv_part = dv_g if g == 0 else dv_part + dv_g
            dk_part = dk_g if g == 0 else dk_part + dk_g

        @pl.when(first)
        def _():
            dv_ref[...] = dv_part
            dk_ref[...] = dk_part

        @pl.when(jnp.logical_not(first))
        def _():
            dv_ref[...] += dv_part
            dk_ref[...] += dk_part

    qspec = pl.BlockSpec((tq, GRP * HEAD_DIM), lambda kv, i: (i, kv))
    return _call(
        body, name=name, grid=(N_KV, S // tq),
        in_specs=[qspec, pl.BlockSpec((S, HEAD_DIM), lambda kv, i: (0, kv)),
                  pl.BlockSpec((S, HEAD_DIM), lambda kv, i: (0, N_KV + kv)), qspec],
        out_specs=[qspec, pl.BlockSpec((S, HEAD_DIM), lambda kv, i: (0, kv)),
                   pl.BlockSpec((S, HEAD_DIM), lambda kv, i: (0, kv))],
        out_shape=[jax.ShapeDtypeStruct((S, Q_COLS), bf16), jax.ShapeDtypeStruct((S, KV_COLS), f32),
                   jax.ShapeDtypeStruct((S, KV_COLS), f32)],
        compiler_params=_params(("parallel", "arbitrary"), 6 * GRP * tq * S * 4 + (8 << 20)),
    )(qr, kr, kv_raw, do)


CONV_HALO = 16


def _fill_padded(pad_ref, val, S):
    pad_ref[pl.ds(0, CONV_HALO), :] = jnp.zeros((CONV_HALO, LANES), f32)
    pad_ref[pl.ds(CONV_HALO + S, CONV_HALO), :] = jnp.zeros((CONV_HALO, LANES), f32)
    pad_ref[pl.ds(CONV_HALO, S), :] = val


def _group_specs(S, n_groups, second_half):
    return pl.BlockSpec((S, LANES), functools.partial(lambda g, o: (0, g + o), o=n_groups if second_half else 0))


def _conv1_fwd(conv_in, wdw, b_dw, name):
    S = conv_in.shape[0]
    ng = CONV_CH // LANES
    R = min(256, S)

    def body(a_ref, g_ref, w_ref, b_ref, o_ref, pad_ref):
        z = a_ref[...].astype(f32) * jax.nn.sigmoid(g_ref[...].astype(f32))
        _fill_padded(pad_ref, z, S)
        for r in range(S // R):
            acc = jnp.zeros((R, LANES), f32) + b_ref[...]
            for j in range(CONV_W):
                acc = acc + w_ref[pl.ds(j, 1), :] * pad_ref[pl.ds(r * R + CONV_HALO - CONV_PAD + j, R), :]
            o_ref[pl.ds(r * R, R), :] = acc

    return _call(
        body, name=name, grid=(ng,),
        in_specs=[_group_specs(S, ng, False), _group_specs(S, ng, True),
                  pl.BlockSpec((CONV_WP, LANES), lambda g: (g, 0)), pl.BlockSpec((1, LANES), lambda g: (0, g))],
        out_specs=pl.BlockSpec((S, LANES), lambda g: (0, g)),
        out_shape=jax.ShapeDtypeStruct((S, CONV_CH), f32),
        scratch_shapes=[pltpu.VMEM((S + 2 * CONV_HALO, LANES), f32)],
        compiler_params=_params(("parallel",), 24 << 20),
    )(conv_in, conv_in, wdw, b_dw)


def _conv1_bwd(conv_in, dc, wdw, name):
    S = conv_in.shape[0]
    ng = CONV_CH // LANES
    R = min(256, S)

    def body(a_ref, g_ref, w_ref, dc_ref, da_ref, dg_ref, dw_ref, db_ref, padz_ref, padd_ref):
        a = a_ref[...].astype(f32)
        sg = jax.nn.sigmoid(g_ref[...].astype(f32))
        _fill_padded(padz_ref, a * sg, S)
        _fill_padded(padd_ref, dc_ref[...], S)
        for r in range(S // R):
            dz = jnp.zeros((R, LANES), f32)
            for j in range(CONV_W):
                dz = dz + w_ref[pl.ds(j, 1), :] * padd_ref[pl.ds(r * R + CONV_HALO + CONV_PAD - j, R), :]
            rows = pl.ds(r * R, R)
            ar, sr = a_ref[rows, :].astype(f32), jax.nn.sigmoid(g_ref[rows, :].astype(f32))
            da_ref[rows, :] = (dz * sr).astype(da_ref.dtype)
            dg_ref[rows, :] = (dz * ar * sr * (1.0 - sr)).astype(dg_ref.dtype)
        for j in range(CONV_W):
            tot = jnp.zeros((1, LANES), f32)
            for r in range(S // R):
                tot = tot + jnp.sum(dc_ref[pl.ds(r * R, R), :] * padz_ref[pl.ds(r * R + CONV_HALO - CONV_PAD + j, R), :],
                                    axis=0, keepdims=True)
            dw_ref[pl.ds(j, 1), :] = tot
        dw_ref[pl.ds(CONV_W, CONV_WP - CONV_W), :] = jnp.zeros((CONV_WP - CONV_W, LANES), f32)
        db_ref[...] = jnp.sum(dc_ref[...], axis=0, keepdims=True)

    return _call(
        body, name=name, grid=(ng,),
        in_specs=[_group_specs(S, ng, False), _group_specs(S, ng, True),
                  pl.BlockSpec((CONV_WP, LANES), lambda g: (g, 0)), pl.BlockSpec((S, LANES), lambda g: (0, g))],
        out_specs=[pl.BlockSpec((S, LANES), lambda g: (0, g)), pl.BlockSpec((S, LANES), lambda g: (0, g)),
                   pl.BlockSpec((CONV_WP, LANES), lambda g: (g, 0)), pl.BlockSpec((1, LANES), lambda g: (0, g))],
        out_shape=[jax.ShapeDtypeStruct((S, CONV_CH), bf16), jax.ShapeDtypeStruct((S, CONV_CH), bf16),
                   jax.ShapeDtypeStruct((ng * CONV_WP, LANES), f32), jax.ShapeDtypeStruct((1, CONV_CH), f32)],
        scratch_shapes=[pltpu.VMEM((S + 2 * CONV_HALO, LANES), f32), pltpu.VMEM((S + 2 * CONV_HALO, LANES), f32)],
        compiler_params=_params(("parallel",), 24 << 20),
    )(conv_in, conv_in, wdw, dc)


def _ln_stats(x, eps=1e-5):
    xc = x - jnp.mean(x, axis=-1, keepdims=True)
    r = lax.rsqrt(jnp.mean(xc * xc, axis=-1, keepdims=True) + eps)
    return r, xc * r


def _ln_bwd(r, xh, dxh):
    return r * (dxh - jnp.mean(dxh, axis=-1, keepdims=True) - xh * jnp.mean(dxh * xh, axis=-1, keepdims=True))


def _conv2_fwd(c, ln_g, ln_b, name):
    def body(c_ref, g_ref, b_ref, o_ref):
        _, xh = _ln_stats(c_ref[...])
        y = xh * g_ref[...] + b_ref[...]
        o_ref[...] = (y * jax.nn.sigmoid(y)).astype(o_ref.dtype)

    return _rows(body, [("r", c), ("f", ln_g), ("f", ln_b)], [("r", CONV_CH, bf16)], tm=min(256, c.shape[0]), name=name)[0]


def _conv2_bwd(c, dcz, ln_g, ln_b, name):
    def body(c_ref, d_ref, g_ref, b_ref, dc_ref, dg_ref, db_ref):
        r, xh = _ln_stats(c_ref[...])
        y = xh * g_ref[...] + b_ref[...]
        sg = jax.nn.sigmoid(y)
        dy = d_ref[...].astype(f32) * (sg * (1.0 + y * (1.0 - sg)))
        dc_ref[...] = _ln_bwd(r, xh, dy * g_ref[...])
        _accumulate(dg_ref, jnp.sum(dy * xh, axis=0, keepdims=True))
        _accumulate(db_ref, jnp.sum(dy, axis=0, keepdims=True))

    return _rows(body, [("r", c), ("r", dcz), ("f", ln_g), ("f", ln_b)],
                 [("r", CONV_CH, f32), ("a", (1, CONV_CH), f32), ("a", (1, CONV_CH), f32)],
                 tm=min(256, c.shape[0]), name=name)


GELU_K = math.sqrt(2.0 / math.pi)
GELU_C = 0.044715


def _gelu(x):
    return 0.5 * x * (1.0 + jnp.tanh(GELU_K * (x + GELU_C * x * x * x)))


def _gelu_grad(x):
    th = jnp.tanh(GELU_K * (x + GELU_C * x * x * x))
    return 0.5 * (1.0 + th) + 0.5 * x * (1.0 - th * th) * (GELU_K * (1.0 + 3.0 * GELU_C * x * x))


def _chunk_rows(n):
    return pl.ds(pl.multiple_of(n * SG_CHUNK, SG_CHUNK), SG_CHUNK)


def _sgu_fwd(sg_in, ln_g, ln_b, w_s, b_s, name):
    S = sg_in.shape[0]

    def body(u_ref, v_ref, lg_ref, lb_ref, w_ref, b_ref, o_ref):
        wb = w_ref[...].astype(bf16)

        def chunk(n, carry):
            rows = _chunk_rows(n)
            gu = _gelu(u_ref[rows, :].astype(f32))
            _, xh = _ln_stats(_gelu(v_ref[rows, :].astype(f32)))
            vl = xh * lg_ref[...] + lb_ref[...]
            t = jnp.dot(wb, vl.astype(bf16), preferred_element_type=f32) + b_ref[...]
            o_ref[rows, :] = (gu * t).astype(o_ref.dtype)
            return carry

        lax.fori_loop(0, S // SG_CHUNK, chunk, 0, unroll=2)

    return _call(
        body, name=name, grid=(SG_G,),
        in_specs=[_group_specs(S, SG_G, False), _group_specs(S, SG_G, True),
                  pl.BlockSpec((1, LANES), lambda g: (0, g)), pl.BlockSpec((1, LANES), lambda g: (0, g)),
                  pl.BlockSpec((None, SG_CHUNK, SG_CHUNK), lambda g: (g, 0, 0)),
                  pl.BlockSpec((None, SG_CHUNK, 1), lambda g: (g, 0, 0))],
        out_specs=pl.BlockSpec((S, LANES), lambda g: (0, g)),
        out_shape=jax.ShapeDtypeStruct((S, SG_CH), bf16),
        compiler_params=_params(("parallel",), 24 << 20),
    )(sg_in, sg_in, ln_g, ln_b, w_s, b_s)


def _sgu_bwd(sg_in, dsz, ln_g, ln_b, w_s, w_s_t, b_s, name):
    S = sg_in.shape[0]

    def body(u_ref, v_ref, lg_ref, lb_ref, w_ref, wt_ref, b_ref, d_ref, du_ref, dv_ref, dw_ref, db_ref, dlg_ref, dlb_ref):
        wb = w_ref[...].astype(bf16)
        wtb = wt_ref[...].astype(bf16)

        def chunk(n, carry):
            dwa, dba, dlga, dlba = carry
            rows = _chunk_rows(n)
            u = u_ref[rows, :].astype(f32)
            v = v_ref[rows, :].astype(f32)
            gu = _gelu(u)
            r, xh = _ln_stats(_gelu(v))
            vlb = (xh * lg_ref[...] + lb_ref[...]).astype(bf16)
            t = jnp.dot(wb, vlb, preferred_element_type=f32) + b_ref[...]
            d = d_ref[rows, :].astype(f32)
            dt = d * gu
            dtb = dt.astype(bf16)
            dwa = dwa + lax.dot_general(dtb, vlb, (((1,), (1,)), ((), ())), preferred_element_type=f32)
            dba = dba + jnp.sum(dt, axis=1, keepdims=True)
            dvl = jnp.dot(wtb, dtb, preferred_element_type=f32)
            dlga = dlga + jnp.sum(dvl * xh, axis=0, keepdims=True)
            dlba = dlba + jnp.sum(dvl, axis=0, keepdims=True)
            dgv = _ln_bwd(r, xh, dvl * lg_ref[...])
            du_ref[rows, :] = (d * t * _gelu_grad(u)).astype(du_ref.dtype)
            dv_ref[rows, :] = (dgv * _gelu_grad(v)).astype(dv_ref.dtype)
            return dwa, dba, dlga, dlba

        init = (jnp.zeros((SG_CHUNK, SG_CHUNK), f32), jnp.zeros((SG_CHUNK, 1), f32),
                jnp.zeros((1, LANES), f32), jnp.zeros((1, LANES), f32))
        dwa, dba, dlga, dlba = lax.fori_loop(0, S // SG_CHUNK, chunk, init, unroll=2)
        dw_ref[...] = dwa
        db_ref[...] = dba
        dlg_ref[...] = dlga
        dlb_ref[...] = dlba

    wspec = pl.BlockSpec((None, SG_CHUNK, SG_CHUNK), lambda g: (g, 0, 0))
    bspec = pl.BlockSpec((None, SG_CHUNK, 1), lambda g: (g, 0, 0))
    lspec = pl.BlockSpec((1, LANES), lambda g: (0, g))
    cspec = pl.BlockSpec((S, LANES), lambda g: (0, g))
    return _call(
        body, name=name, grid=(SG_G,),
        in_specs=[_group_specs(S, SG_G, False), _group_specs(S, SG_G, True), lspec, lspec, wspec, wspec, bspec, cspec],
        out_specs=[cspec, cspec, wspec, bspec, lspec, lspec],
        out_shape=[jax.ShapeDtypeStruct((S, SG_CH), bf16), jax.ShapeDtypeStruct((S, SG_CH), bf16),
                   jax.ShapeDtypeStruct((SG_G, SG_CHUNK, SG_CHUNK), f32), jax.ShapeDtypeStruct((SG_G, SG_CHUNK, 1), f32),
                   jax.ShapeDtypeStruct((1, SG_CH), f32), jax.ShapeDtypeStruct((1, SG_CH), f32)],
        compiler_params=_params(("parallel",), 24 << 20),
    )(sg_in, sg_in, ln_g, ln_b, w_s, w_s_t, b_s, dsz)


def _merge_fwd(gl, b_gate, ya, yc, ys, name):
    D = ya.shape[1]

    def body(gl_ref, b_ref, ya_ref, yc_ref, ys_ref, o_ref):
        acc = jnp.zeros(o_ref.shape, f32)
        for i, y_ref in enumerate((ya_ref, yc_ref, ys_ref)):
            cols = slice(i * D, (i + 1) * D)
            acc = acc + jax.nn.sigmoid(gl_ref[:, cols].astype(f32) + b_ref[:, cols]) * y_ref[...].astype(f32)
        o_ref[...] = acc.astype(o_ref.dtype)

    return _rows(body, [("r", gl), ("f", b_gate), ("r", ya), ("r", yc), ("r", ys)], [("r", D, bf16)],
                 tm=min(128, gl.shape[0]), name=name)[0]


def _merge_bwd(dm, gl, b_gate, ya, yc, ys, name):
    D = ya.shape[1]

    def body(dm_ref, gl_ref, b_ref, ya_ref, yc_ref, ys_ref, dgl_ref, dya_ref, dyc_ref, dys_ref, db_ref):
        dm_ = dm_ref[...].astype(f32)
        for i, (y_ref, dy_ref) in enumerate(((ya_ref, dya_ref), (yc_ref, dyc_ref), (ys_ref, dys_ref))):
            cols = slice(i * D, (i + 1) * D)
            gate = jax.nn.sigmoid(gl_ref[:, cols].astype(f32) + b_ref[:, cols])
            dy_ref[...] = (dm_ * gate).astype(dy_ref.dtype)
            dlog = dm_ * y_ref[...].astype(f32) * gate * (1.0 - gate)
            dgl_ref[:, cols] = dlog.astype(dgl_ref.dtype)
            part = jnp.sum(dlog, axis=0, keepdims=True)
            first = pl.program_id(0) == 0

            @pl.when(first)
            def _():
                db_ref[:, cols] = part

            @pl.when(jnp.logical_not(first))
            def _():
                db_ref[:, cols] += part

    return _rows(body, [("r", dm), ("r", gl), ("f", b_gate), ("r", ya), ("r", yc), ("r", ys)],
                 [("r", 3 * D, bf16), ("r", D, bf16), ("r", D, bf16), ("r", D, bf16), ("a", (1, 3 * D), f32)],
                 tm=min(128, gl.shape[0]), name=name)


def _swiglu_fwd(fg, fu, name):
    def body(g_ref, u_ref, o_ref):
        g = g_ref[...].astype(f32)
        o_ref[...] = (g * jax.nn.sigmoid(g) * u_ref[...].astype(f32)).astype(o_ref.dtype)

    return _rows(body, [("r", fg), ("r", fu)], [("r", fg.shape[1], bf16)], tm=min(128, fg.shape[0]), name=name)[0]


def _swiglu_bwd(dact, fg, fu, name):
    def body(d_ref, g_ref, u_ref, dg_ref, du_ref):
        d = d_ref[...].astype(f32)
        g = g_ref[...].astype(f32)
        sg = jax.nn.sigmoid(g)
        dg_ref[...] = (d * u_ref[...].astype(f32) * sg * (1.0 + g * (1.0 - sg))).astype(dg_ref.dtype)
        du_ref[...] = (d * g * sg).astype(du_ref.dtype)

    return _rows(body, [("r", dact), ("r", fg), ("r", fu)], [("r", fg.shape[1], bf16), ("r", fg.shape[1], bf16)],
                 tm=min(128, fg.shape[0]), name=name)


def _row_tile(r, c, n_arrays, itemsize=4):
    fits = [tm for tm in range(16, r + 1, 16) if r % tm == 0 and 2 * n_arrays * tm * c * itemsize <= (24 << 20)]
    return fits[-1] if fits else r


def _sum_slots(slots, name):
    n, r, c = slots.shape
    tm = _row_tile(r, c, n + 2)

    def body(s_ref, o_ref):
        acc = s_ref[0].astype(f32)
        for k in range(1, n):
            acc = acc + s_ref[k].astype(f32)
        o_ref[...] = acc

    return _call(body, name=name, grid=(r // tm,),
                 in_specs=[pl.BlockSpec((n, tm, c), lambda i: (0, i, 0))],
                 out_specs=pl.BlockSpec((tm, c), lambda i: (i, 0)),
                 out_shape=jax.ShapeDtypeStruct((r, c), f32),
                 compiler_params=_params(("parallel",), 40 << 20))(slots)


def _add_sibling(g4, recv, core, name):
    _, _, r, c = g4.shape
    tm = _row_tile(r, c, 3, 2)

    def body(core_ref, g_ref, r_ref, o_ref):
        o_ref[...] = (g_ref[...].astype(f32) + r_ref[...].astype(f32)).astype(o_ref.dtype)

    grid_spec = pltpu.PrefetchScalarGridSpec(
        num_scalar_prefetch=1, grid=(N_CHIP, r // tm),
        in_specs=[pl.BlockSpec((None, None, tm, c), lambda k, i, core_ref: (k, core_ref[0], i, 0)),
                  pl.BlockSpec((None, tm, c), lambda k, i, core_ref: (k, i, 0))],
        out_specs=pl.BlockSpec((None, tm, c), lambda k, i, core_ref: (k, i, 0)))
    return _call(body, name=name, grid_spec=grid_spec, out_shape=jax.ShapeDtypeStruct((N_CHIP, r, c), bf16),
                 compiler_params=_params(("parallel", "parallel"), 40 << 20))(core, g4, recv)


def _adamw(w, g, m, v, name):
    L, r, c = w.shape
    tm = _row_tile(r, c, 7)
    c1 = 1.0 - ADAM_B1 ** ADAM_STEP
    c2 = 1.0 - ADAM_B2 ** ADAM_STEP

    def body(w_ref, g_ref, m_ref, v_ref, d_ref, mo_ref, vo_ref):
        g_ = g_ref[...]
        m_ = ADAM_B1 * m_ref[...] + (1.0 - ADAM_B1) * g_
        v_ = ADAM_B2 * v_ref[...] + (1.0 - ADAM_B2) * (g_ * g_)
        d_ref[...] = -ADAM_LR * ((m_ / c1) / (jnp.sqrt(v_ / c2) + ADAM_EPS) + ADAM_WD * w_ref[...])
        mo_ref[...] = m_
        vo_ref[...] = v_

    spec = pl.BlockSpec((None, tm, c), lambda l, i: (l, i, 0))
    shp = jax.ShapeDtypeStruct((L, r, c), f32)
    return _call(body, name=name, grid=(L, r // tm), in_specs=[spec] * 4, out_specs=[spec] * 3,
                 out_shape=[shp] * 3, compiler_params=_params(("parallel", "parallel"), 40 << 20))(w, g, m, v)


def _mesh_pos():
    return lax.axis_index("x"), lax.axis_index("y"), lax.axis_index("c")


def _all_gather(shards, after, name):
    n = len(shards)

    def body(*refs):
        x_refs, o_refs = refs[:n], refs[n + 1:2 * n + 1]
        send_sems, recv_sems, local_sems = refs[2 * n + 1:]
        x, y, c = _mesh_pos()
        me, sibling = (x, y, c), (x, y, 1 - c)
        chips = [(1 - x, y), (x, 1 - y), (1 - x, 1 - y)]

        def rows(k, px, py, pc):
            return o_refs[k].at[4 * px + 2 * py + pc]

        def copy(k, s, block, to, src=None):
            return pltpu.make_async_remote_copy(
                src_ref=rows(k, *block) if src is None else src, dst_ref=rows(k, *block),
                send_sem=send_sems.at[k, s], recv_sem=recv_sems.at[k, s], device_id=to, device_id_type=MESH)

        mine = [pltpu.make_async_copy(x_refs[k], rows(k, *me), local_sems.at[k]) for k in range(n)]
        for cp in mine:
            cp.start()
        first = [copy(k, 0, me, sibling, src=x_refs[k]) for k in range(n)]
        for j, chip in enumerate(chips):
            first += [copy(k, 1 + j, me, (*chip, c), src=x_refs[k]) for k in range(n)]
        for cp in first:
            cp.start()
        passed = []
        for j, chip in enumerate(chips):
            for k in range(n):
                copy(k, 1 + j, (*chip, c), me).wait_recv()
                fwd = copy(k, 4 + j, (*chip, c), sibling)
                fwd.start()
                passed.append(fwd)
        for k in range(n):
            copy(k, 0, sibling, me).wait_recv()
        for j, chip in enumerate(chips):
            for k in range(n):
                copy(k, 4 + j, (*chip, 1 - c), me).wait_recv()
        for cp in first + passed:
            cp.wait_send()
        for cp in mine:
            cp.wait()

    return _call(
        body, name=name, in_specs=[HBM] * n + [ANY], out_specs=[HBM] * n,
        out_shape=[jax.ShapeDtypeStruct((N_DEV,) + s.shape, s.dtype) for s in shards],
        scratch_shapes=[pltpu.SemaphoreType.DMA((n, 7)), pltpu.SemaphoreType.DMA((n, 7)), pltpu.SemaphoreType.DMA((n,))],
    )(*shards, after)


SEM =pl.BlockSpec(memory_space=pltpu.SEMAPHORE)
ANY = pl.BlockSpec(memory_space=pl.ANY)
EFFECT = pltpu.SideEffectType.DATAFLOW_SIDE_EFFECTING


def _other_chips(x, y):
    return [(1 - x, y), (x, 1 - y), (1 - x, 1 - y)]


def _peers(kind, x, y):
    return [(x, y)] if kind == "sibling" else _other_chips(x, y)


def _ici_copy(kind, src_ref, land_ref, send_sem, recv_sem, sender, target, c):
    (sx, sy), (tx, ty) = sender, target
    if kind == "sibling":
        return pltpu.make_async_remote_copy(src_ref=src_ref.at[:, 1 - c], dst_ref=land_ref, send_sem=send_sem,
                                            recv_sem=recv_sem, device_id=(tx, ty, 1 - c), device_id_type=MESH)
    if kind == "gather":
        src, dst = src_ref, land_ref.at[4 * sx + 2 * sy + c]
    else:
        src, dst = src_ref.at[2 * tx + ty], land_ref.at[2 * sx + sy]
    return pltpu.make_async_remote_copy(src_ref=src, dst_ref=dst, send_sem=send_sem, recv_sem=recv_sem,
                                        device_id=(tx, ty, c), device_id_type=MESH)


def _ici_start(kind, srcs, lands, after, name):
    n = len(srcs)
    npeer = 1 if kind == "sibling" else 3

    def body(*refs):
        src_refs, land_refs = refs[:n], refs[n:2 * n]
        send_sems, recv_sems = refs[2 * n + 1], refs[2 * n + 2]
        token = refs[-1]
        x, y, c = _mesh_pos()
        for j, chip in enumerate(_peers(kind, x, y)):
            for k in range(n):
                _ici_copy(kind, src_refs[k], land_refs[k], send_sems.at[npeer * k + j], recv_sems.at[npeer * k + j],(x, y), chip, c).start()
        token[...] = jnp.zeros_like(token)

    bufs = list(srcs) + list(lands)
    return _call(
        body, name=name,
        out_shape=(pltpu.SemaphoreType.DMA((npeer * n,)), pltpu.SemaphoreType.DMA((npeer * n,)),
                   *[pltpu.HBM(b.shape, b.dtype) for b in bufs], jax.ShapeDtypeStruct((8, LANES), f32)),
        in_specs=[HBM] * (2 * n) + [ANY], out_specs=(SEM, SEM, *[HBM] * (2 * n), pl.BlockSpec(memory_space=pltpu.VMEM)),
        input_output_aliases={i: 2 + i for i in range(2 * n)},
        compiler_params=pltpu.CompilerParams(has_side_effects=EFFECT),
    )(*[pltpu.with_memory_space_constraint(b, pltpu.HBM) for b in bufs], after)


def _ici_wait(kind, started, after, name):
    send_sems, recv_sems, *bufs = started[:-1]
    n = len(bufs) // 2
    npeer = 1 if kind == "sibling" else 3

    def body(*refs):
        src_refs, land_refs = refs[:n], refs[n:2 * n]
        send_sems, recv_sems = refs[2 * n], refs[2 * n + 1]
        x, y, c = _mesh_pos()
        for j, chip in enumerate(_peers(kind, x, y)):
            for k in range(n):
                _ici_copy(kind, src_refs[k], land_refs[k], send_sems.at[npeer * k + j], recv_sems.at[npeer * k + j],(x, y), chip, c).wait_send()
                _ici_copy(kind, src_refs[k], land_refs[k], send_sems.at[npeer * k + j], recv_sems.at[npeer * k + j],chip, (x, y), c).wait_recv()

    out = _call(
        body, name=name, out_shape=[pltpu.HBM(b.shape, b.dtype) for b in bufs],
        in_specs=[HBM] * (2 * n) + [SEM, SEM, ANY], out_specs=[HBM] * (2 * n),
        input_output_aliases={i: i for i in range(2 * n)},
        compiler_params=pltpu.CompilerParams(has_side_effects=EFFECT),
    )(*bufs, send_sems, recv_sems, after)
    return out[:n], out[n:]


def _d2d_gather(lands, after, name):
    n = len(lands)

    def body(*refs):
        in_refs, o_refs = refs[:n], refs[n + 1:2 * n + 1]
        send_sems, recv_sems = refs[2 * n + 1:]
        x, y, c = _mesh_pos()
        copies = [pltpu.make_async_remote_copy(
            src_ref=in_refs[k].at[:, c], dst_ref=o_refs[k].at[:, c], send_sem=send_sems.at[k], recv_sem=recv_sems.at[k],
            device_id=(x, y, 1 - c), device_id_type=MESH) for k in range(n)]
        for cp in copies:
            cp.start()
        for k, cp in enumerate(copies):
            cp.wait_send()
            pltpu.make_async_remote_copy(
                src_ref=in_refs[k].at[:, c], dst_ref=o_refs[k].at[:, 1 - c], send_sem=send_sems.at[k],
                recv_sem=recv_sems.at[k], device_id=(x, y, 1 - c), device_id_type=MESH).wait_recv()

    return _call(
        body, name=name, in_specs=[HBM] * n + [ANY], out_specs=[HBM] * n,
        out_shape=[jax.ShapeDtypeStruct(b.shape, b.dtype) for b in lands],
        input_output_aliases={k: k for k in range(n)},
        scratch_shapes=[pltpu.SemaphoreType.DMA((n,)), pltpu.SemaphoreType.DMA((n,))],
    )(*lands, after)


def _sum_chip_slots(lands, sums, chip, name):
    _, r, c = lands.shape
    tm = _row_tile(r, c, 10, 2)

    def body(chip_ref, l_ref, s_ref, o_ref):
        acc = None
        for k in range(N_CHIP):
            part = jnp.where(chip_ref[0] == k, s_ref[k], l_ref[k]).astype(f32)
            acc = part if acc is None else acc + part
        o_ref[...] = acc

    grid_spec = pltpu.PrefetchScalarGridSpec(
        num_scalar_prefetch=1, grid=(r // tm,),
        in_specs=[pl.BlockSpec((N_CHIP, tm, c), lambda i, chip_ref: (0, i, 0)),
                  pl.BlockSpec((N_CHIP, tm, c), lambda i, chip_ref: (0, i, 0))],
        out_specs=pl.BlockSpec((tm, c), lambda i, chip_ref: (i, 0)))
    return _call(body, name=name, grid_spec=grid_spec, out_shape=jax.ShapeDtypeStruct((r, c), f32),
                 compiler_params=_params(("parallel",), 40 << 20))(chip, lands, sums)


def _reduce_begin(grads, core, tag):
    g4s = [g.reshape(N_CHIP, 2, g.shape[0] // N_DEV, g.shape[1]) for g in grads]
    recvs = [lax.empty((N_CHIP,) + g.shape[2:], g.dtype) for g in g4s]
    return _ici_start("sibling", g4s, recvs, core, name="rs_d2d_start_" + tag)


def _reduce_continue(begun, core, after, tag):
    g4s, recvs = _ici_wait("sibling", begun, after, name="rs_d2d_wait_" + tag)
    sums = [_add_sibling(g4, rv, core, name="rs_add_" + tag) for g4, rv in zip(g4s, recvs)]
    lands = [lax.empty(s.shape, s.dtype) for s in sums]
    return _ici_start("reduce", sums, lands, core, name="rs_start_" + tag)


def _adamw_reduced(layer, w, m, v, lands, sums, chip, prev, name):
    L, r, c = w.shape
    tm = _row_tile(r, c, 11)
    c1 = 1.0 - ADAM_B1 ** ADAM_STEP
    c2 = 1.0 - ADAM_B2 ** ADAM_STEP
    n_prev = 0 if prev is None else 4

    def body(chip_ref, w_ref, m_ref, v_ref, l_ref, s_ref, *refs):
        g_ref, d_ref, mo_ref, vo_ref = refs[n_prev:]
        g_ = None
        for k in range(N_CHIP):
            part = jnp.where(chip_ref[0] == k, s_ref[k], l_ref[k]).astype(f32)
            g_ = part if g_ is None else g_ + part
        m_ = ADAM_B1 * m_ref[...] + (1.0 - ADAM_B1) * g_
        v_ = ADAM_B2 * v_ref[...] + (1.0 - ADAM_B2) * (g_ * g_)
        g_ref[...] = g_
        d_ref[...] = -ADAM_LR * ((m_ / c1) / (jnp.sqrt(v_ / c2) + ADAM_EPS) + ADAM_WD * w_ref[...])
        mo_ref[...] = m_
        vo_ref[...] = v_

    wspec = pl.BlockSpec((None, tm, c), lambda i, chip_ref: (layer, i, 0))
    sspec = pl.BlockSpec((N_CHIP, tm, c), lambda i, chip_ref: (0, i, 0))
    grid_spec = pltpu.PrefetchScalarGridSpec(
        num_scalar_prefetch=1, grid=(r // tm,), in_specs=[wspec] * 3 + [sspec] * 2 + [ANY] * n_prev, out_specs=[wspec] * 4)
    return _call(body, name=name, grid_spec=grid_spec, out_shape=[jax.ShapeDtypeStruct((L, r, c), f32)] * 4,
                 input_output_aliases={6 + i: i for i in range(n_prev)},
                 compiler_params=_params(("parallel",), 40 << 20))(chip, w, m, v, lands, sums, *(prev or ()))


def _rope_tables(S):
    rows = S // GRID_W
    row = jnp.repeat(jnp.arange(rows, dtype=f32), GRID_W)
    col = jnp.tile(jnp.arange(GRID_W, dtype=f32), rows)
    nf = HEAD_DIM // 4
    inv = ROPE_THETA ** (-jnp.arange(nf, dtype=f32) / nf)
    ang = jnp.concatenate([row[:, None] * inv, col[:, None] * inv], axis=-1)
    cos, sin = jnp.cos(ang), jnp.sin(ang)
    return jnp.concatenate([cos, cos], axis=-1), jnp.concatenate([-sin, sin], axis=-1)


def _layer_fwd(xin, p, w, more_weights, cos2, sin2):
    sv = {"xin": xin}
    h = sv["h"] = _rms_fwd(xin, p["g_mix"], name="rms_mix")
    proj = functools.partial(_mm, h, w["in"], "nt", bf16)
    q_raw = sv["q_raw"] = proj(n=Q_COLS, b_off=0, name="proj_q")
    kv_raw = sv["kv_raw"] = proj(n=2 * KV_COLS, b_off=OFF_KV, name="proj_kv")
    conv_in = sv["conv_in"] = proj(n=2 * CONV_CH, b_off=OFF_CONV, name="proj_conv")
    sg_in = sv["sg_in"] = proj(n=2 * SG_CH, b_off=OFF_SG, name="proj_sg")
    gl = sv["gl"] = proj(n=3 * D_MODEL, b_off=OFF_GATE, name="proj_gate")
    qr, kr = sv["qr"], sv["kr"] = _qk_fwd(q_raw, kv_raw, p["q_norm_g"], p["k_norm_g"], cos2, sin2, name="qk_fwd")
    o = sv["o"] = _attn_fwd(qr, kr, kv_raw, name="attn_fwd")
    c = sv["c"] = _conv1_fwd(conv_in, w["dw"], p["b_dw"], name="conv1_fwd")
    cz = sv["cz"] = _conv2_fwd(c, p["conv_ln_g"], p["conv_ln_b"], name="conv2_fwd")
    sz = sv["sz"] = _sgu_fwd(sg_in, p["sg_ln_g"], p["sg_ln_b"], p["w_s"], p["b_s"], name="sgu_fwd")
    w = {**w, **more_weights(1, sz)}
    ya = sv["ya"] = _mm(o, w["attn_o"], "nt", bf16, name="out_attn")
    yc = sv["yc"] = _mm(cz, w["conv_o"], "nt", bf16, name="out_conv")
    ys = sv["ys"] = _mm(sz, w["sg_o"], "nt", bf16, name="out_sg")
    merged = sv["merged"] = _merge_fwd(gl, p["b_gate"], ya, yc, ys, name="merge_fwd")
    x1 = sv["x1"] = _mm(merged, w["out"], "nn", f32, res=xin, name="out_proj")
    w = {**w, **more_weights(2, x1)}
    hf = sv["hf"] = _rms_fwd(x1, p["g_ffn"], name="rms_ffn")
    sv["fg"], sv["fu"], act = _ffn_up(hf, w["ff_gate"], w["ff_up"], name="ffn_up")
    sv["act"] = act
    x2 =_mm(act, w["ff_down"], "nn", f32, res=x1, name="ff_down")
    return x2, sv, w


def _layer_bwd(dx2, dx2b, sv, p, w, cos2, sin2, reduce_begin, reduce_continue, last):
    small = {}
    dfg, dfu = _ffn_down_bwd(dx2b, w["ff_down"], sv["fg"], sv["fu"], name="ffn_down_bwd")
    g_down = _mm(sv["act"], dx2b, "tn", bf16, name="g_ff_down")
    dhf = _mm(dfg, w["ff_gate"], "nn", f32, name="d_hf_gate")
    dhf = _mm(dfu, w["ff_up"], "nn", f32, res=dhf, name="d_hf_up")
    g_gate = _mm(dfg, sv["hf"], "tn", bf16, name="g_ff_gate")
    g_up = _mm(dfu, sv["hf"], "tn", bf16, name="g_ff_up")
    zero = reduce_begin("ffn", dict(w_ff_gate=g_gate, w_ff_up=g_up, w_ff_down=g_down))[0, 0]
    dx1, dx1b, small["g_ffn"] = _rms_bwd(sv["x1"], p["g_ffn"] + zero, dhf, dx2, name="rms_ffn_bwd")
    g_out = _mm(sv["merged"], dx1b, "tn", bf16, name="g_out")
    *dgl, dya, dyc, dys, db0, db1, db2 = _merge_bwd_fused(dx1b, w["out"], sv["gl"], p["b_gate"], sv["ya"], sv["yc"], sv["ys"],
                                                        name="merge_bwd")
    small["b_gate"] = jnp.concatenate([db0, db1, db2], axis=1)
    do = _mm(dya, w["attn_o"], "nn", bf16, after=reduce_continue("ffn", dya), name="d_o")
    g_ao = _mm(dya, sv["o"], "tn", bf16, name="g_attn_o")
    dcz = _mm(dyc, w["conv_o"], "nn", bf16, name="d_cz")
    g_co = _mm(dyc, sv["cz"], "tn", bf16, name="g_conv_o")
    dsz = _mm(dys, w["sg_o"], "nn", bf16, name="d_sz")
    g_so = _mm(dys, sv["sz"], "tn", bf16, name="g_sg_o")
    zero = reduce_begin("mix", dict(w_attn_o=g_ao, w_conv_o=g_co, w_sg_o=g_so, w_out=g_out))[0, 0]
    dsu, dsv, small["w_s"], small["b_s"], small["sg_ln_g"], small["sg_ln_b"] = _sgu_bwd(
        sv["sg_in"], dsz, p["sg_ln_g"] + zero, p["sg_ln_b"], p["w_s"], p["w_s_t"], p["b_s"], name="sgu_bwd")
    dc, small["conv_ln_g"], small["conv_ln_b"] = _conv2_bwd(sv["c"], dcz, p["conv_ln_g"], p["conv_ln_b"], name="conv2_bwd")
    da, dgt, small["w_dw"], small["b_dw"] = _conv1_bwd(sv["conv_in"], dc, w["dw"], name="conv1_bwd")
    zero = reduce_continue("mix", da)[0, 0]
    dqr, dkr, dv = _attn_bwd(sv["qr"], sv["kr"], sv["kv_raw"], do, name="attn_bwd")
    dq_raw, dk_raw, small["q_norm_g"], small["k_norm_g"] = _qk_bwd(
        sv["q_raw"], sv["kv_raw"], dqr, dkr, p["q_norm_g"] + zero, p["k_norm_g"], cos2, sin2, name="qk_bwd")
    dproj = jnp.concatenate([dq_raw, dk_raw, dv.astype(bf16), da, dgt, dsu, dsv, *dgl], axis=1)
    g_in = _mm(dproj, sv["h"], "tn", bf16, name="g_in")
    begun = reduce_begin("in", dict(w_in=g_in))
    if last:
        begun = reduce_continue("in", begun)
    dh = _mm(dproj, w["in"], "nn", f32, after=begun, name="d_h")
    zero = begun[0, 0] if last else reduce_continue("in", dh)[0, 0]
    dx, dxb, small["g_mix"] = _rms_bwd(sv["xin"], p["g_mix"] + zero, dh, dx1, name="rms_mix_bwd")
    return dx, dxb, small


SMALL = ("g_mix", "b_gate", "q_norm_g", "k_norm_g", "b_dw", "conv_ln_g", "conv_ln_b", "sg_ln_g", "sg_ln_b",
         "w_s", "b_s", "g_ffn")
PACK_ALIGN = 8 * LANES


def _pack(parts):
    flat = jnp.concatenate([a.reshape(-1).astype(f32) for a in parts])
    pad = -flat.shape[0] % PACK_ALIGN
    return jnp.pad(flat, (0, pad)).reshape(-1, LANES)


def _unpack(buf, shapes):
    flat = buf.reshape(-1)
    out, pos = [], 0
    for shp in shapes:
        size = math.prod(shp)
        out.append(flat[pos:pos + size].reshape(shp))
        pos += size
    return out


def kernel(x, g_mix, w_in, b_gate, q_norm_g, k_norm_g, w_attn_o, w_dw, b_dw, conv_ln_g, conv_ln_b, w_conv_o, sg_ln_g, sg_ln_b, w_s, b_s, w_sg_o, w_out, g_ffn, w_ff_gate, w_ff_up, w_ff_down, g_final, loss_target, m_g_mix, m_w_in, m_b_gate, m_q_norm_g, m_k_norm_g, m_w_attn_o, m_w_dw, m_b_dw, m_conv_ln_g, m_conv_ln_b, m_w_conv_o, m_sg_ln_g, m_sg_ln_b, m_w_s, m_b_s, m_w_sg_o, m_w_out, m_g_ffn, m_w_ff_gate, m_w_ff_up, m_w_ff_down, m_g_final, v_g_mix, v_w_in, v_b_gate, v_q_norm_g, v_k_norm_g, v_w_attn_o, v_w_dw, v_b_dw, v_conv_ln_g, v_conv_ln_b, v_w_conv_o, v_sg_ln_g, v_sg_ln_b, v_w_s, v_b_s, v_w_sg_o, v_w_out, v_g_ffn, v_w_ff_gate, v_w_ff_up, v_w_ff_down, v_g_final):
    weights = dict(g_mix=g_mix, w_in=w_in, b_gate=b_gate, q_norm_g=q_norm_g, k_norm_g=k_norm_g, w_attn_o=w_attn_o,
                   w_dw=w_dw, b_dw=b_dw, conv_ln_g=conv_ln_g, conv_ln_b=conv_ln_b, w_conv_o=w_conv_o, sg_ln_g=sg_ln_g,
                   sg_ln_b=sg_ln_b, w_s=w_s, b_s=b_s, w_sg_o=w_sg_o, w_out=w_out, g_ffn=g_ffn, w_ff_gate=w_ff_gate,
                   w_ff_up=w_ff_up, w_ff_down=w_ff_down, g_final=g_final)
    mom_m = dict(g_mix=m_g_mix, w_in=m_w_in, b_gate=m_b_gate, q_norm_g=m_q_norm_g, k_norm_g=m_k_norm_g,
                 w_attn_o=m_w_attn_o, w_dw=m_w_dw, b_dw=m_b_dw, conv_ln_g=m_conv_ln_g, conv_ln_b=m_conv_ln_b,
                 w_conv_o=m_w_conv_o, sg_ln_g=m_sg_ln_g, sg_ln_b=m_sg_ln_b, w_s=m_w_s, b_s=m_b_s, w_sg_o=m_w_sg_o,
                 w_out=m_w_out, g_ffn=m_g_ffn, w_ff_gate=m_w_ff_gate, w_ff_up=m_w_ff_up, w_ff_down=m_w_ff_down,
                 g_final=m_g_final)
    mom_v = dict(g_mix=v_g_mix, w_in=v_w_in, b_gate=v_b_gate, q_norm_g=v_q_norm_g, k_norm_g=v_k_norm_g,
                 w_attn_o=v_w_attn_o, w_dw=v_w_dw, b_dw=v_b_dw, conv_ln_g=v_conv_ln_g, conv_ln_b=v_conv_ln_b,
                 w_conv_o=v_w_conv_o, sg_ln_g=v_sg_ln_g, sg_ln_b=v_sg_ln_b, w_s=v_w_s, b_s=v_b_s, w_sg_o=v_w_sg_o,
                 w_out=v_w_out, g_ffn=v_g_ffn, w_ff_gate=v_w_ff_gate, w_ff_up=v_w_ff_up, w_ff_down=v_w_ff_down,
                 g_final=v_g_final)
    S, D = x.shape[1], x.shape[2]
    xi, yi, ci = _mesh_pos()
    me = 4 * xi + 2 * yi + ci
    core = jnp.reshape(ci, (1,)).astype(jnp.int32)
    cos2, sin2 = _rope_tables(S)

    big = ("w_in", "w_attn_o", "w_conv_o", "w_sg_o", "w_out", "w_ff_gate", "w_ff_up", "w_ff_down")
    transposed = {"w_in", "w_attn_o", "w_conv_o", "w_sg_o", "w_ff_gate", "w_ff_up"}
    chip = jnp.reshape(2 * xi + yi, (1,)).astype(jnp.int32)
    groups = (("in", "dw"), ("attn_o", "conv_o", "sg_o", "out"), ("ff_gate", "ff_up", "ff_down"))
    P, shards = [], []
    for l in range(DEPTH):
        sh = {n[2:]: (weights[n][l].T if n in transposed else weights[n][l]).astype(bf16) for n in big}
        sh["dw"] = jnp.pad(w_dw[l].reshape(CONV_W, LANES), ((0, CONV_WP - CONV_W), (0, 0)))
        shards.append(sh)
        p = {n: weights[n][l].reshape(1, -1) for n in SMALL if n not in ("w_s", "b_s")}
        p["w_s"] = w_s[l]
        p["w_s_t"] = jnp.swapaxes(w_s[l], 1, 2)
        p["b_s"] = b_s[l].reshape(SG_G, SG_CHUNK, 1)
        P.append(p)

    gathers = {}

    def start_gather(l, gi, after):
        srcs = [shards[l][n] for n in groups[gi]]
        lands = [lax.dynamic_update_index_in_dim(lax.empty((N_DEV,) + s.shape, s.dtype), s, me, 0) for s in srcs]
        gathers[l, gi] = _ici_start("gather", srcs, lands, after, name=f"ag_start_{l}{gi}")
        return gathers[l, gi][-1]

    def gathered(l, gi, after):
        srcs, lands = _ici_wait("gather", gathers[l, gi], after, name=f"ag_wait_{l}{gi}")
        after = srcs[0]
        if gi == len(groups) - 1 and l + 1 < DEPTH:
            for gj in range(len(groups)):
                after = start_gather(l + 1, gj, after)
        full = _d2d_gather([b.reshape(N_CHIP, 2, *b.shape[1:]) for b in lands], after, name=f"ag_d2d_{gi}")
        return {n: f.reshape(-1, f.shape[3]) for n, f in zip(groups[gi], full)}

    all_started = cos2
    for gi in range(len(groups)):
        all_started = start_gather(0, gi, all_started)

    h = x.reshape(S, D)
    saved, W = [], []
    for l in range(DEPTH):
        first = gathered(l, 0, all_started if l == 0 else h)
        if l == 0:
            P[l]["g_mix"] = P[l]["g_mix"] + all_started[0, 0]
        h, sv, w = _layer_fwd(h, P[l], first, functools.partial(lambda gi, z, l: gathered(l, gi, z), l=l), cos2, sin2)
        saved.append(sv)
        W.append(w)
    dx, dxb, sq, g_final_part = _final_loss(h, g_final.reshape(1, D), loss_target.reshape(S, D), name="final_loss")
    loss = lax.psum(0.5 * jnp.sum(sq) / D, ("x", "y", "c"))

    begun, reductions, small_grads = {}, {}, [None] * DEPTH
    for l in reversed(range(DEPTH)):
        def reduce_begin(group, grads, l=l):
            begun[l, group] = (tuple(grads), _reduce_begin(list(grads.values()), core, tag=f"{group}{l}"))
            return begun[l, group][1][-1]

        def reduce_continue(group, after, l=l):
            names, started = begun[l, group]
            reductions[l, group] = (names, _reduce_continue(started, core, after, tag=f"{group}{l}"))
            return reductions[l, group][1][-1]

        dx, dxb, small_grads[l] = _layer_bwd(dx, dxb, saved[l], P[l], W[l], cos2, sin2, reduce_begin, reduce_continue,
                                            last=(l == 0))
    grad_x = dx.reshape(x.shape)

    grads_out, delta, new_m, new_v = {}, {}, {}, {}
    swap = lambda a: jnp.swapaxes(a, 1, 2)

    def update(n, lands, sums):
        as_arrives = n not in transposed or weights[n].shape[2] % LANES != 0
        if as_arrives:
            to_arrival = swap if n in transposed else (lambda a: a)
            out = None
            for l in reversed(range(DEPTH)):
                out = _adamw_reduced(l, to_arrival(weights[n]), to_arrival(mom_m[n]), to_arrival(mom_v[n]),
                                     lands[l], sums[l], chip, out, name=f"adamw_{n}_{l}")
            grads_out[n], delta[n], new_m[n], new_v[n] = [to_arrival(o) for o in out]
        else:
            g = jnp.stack([_sum_chip_slots(lands[l], sums[l], chip, name="rs_sum_" + n) for l in range(DEPTH)])
            grads_out[n] = swap(g)
            delta[n], new_m[n], new_v[n] = _adamw(weights[n], grads_out[n], mom_m[n], mom_v[n], name="adamw_" + n)
        return delta[n]

    after = dx
    for group in ("ffn", "mix", "in"):
        names = reductions[0, group][0]
        arrived = [_ici_wait("reduce", reductions[l, group][1], after, name=f"rs_wait_{group}{l}") for l in range(DEPTH)]
        for i, n in enumerate(names):
            after = update(n, [arrived[l][1][i] for l in range(DEPTH)], [arrived[l][0][i] for l in range(DEPTH)])

    small_shapes = [weights[n].shape for n in SMALL] + [g_final.shape, (DEPTH, CONV_CH // LANES, CONV_WP, LANES)]
    parts = [jnp.stack([small_grads[l][n].reshape(weights[n].shape[1:]) for l in range(DEPTH)]) for n in SMALL]
    parts += [g_final_part.reshape(g_final.shape), jnp.stack([small_grads[l]["w_dw"] for l in range(DEPTH)])]
    packed = _pack(parts)
    gathered = _all_gather([packed], after, name="gather_small")[0]
    total = _sum_slots(gathered, name="sum_small")
    small_total = _unpack(total, small_shapes)
    grads_out.update(zip(SMALL + ("g_final",), small_total[:-1]))
    dw_full = small_total[-1]
    grads_out["w_dw"] = lax.dynamic_index_in_dim(dw_full, me, axis=1, keepdims=False)[:, :CONV_W].reshape(w_dw.shape)

    rep = tuple(n for n in SMALL if n != "w_s") + ("g_final",)
    rep_shapes = [weights[n].shape for n in rep]
    packs = [_pack([src[n] for n in rep])[None] for src in (weights, grads_out, mom_m, mom_v)]
    for dst, buf in zip((delta, new_m, new_v), _adamw(*packs, name="adamw_small")):
        dst.update(zip(rep, _unpack(buf[0], rep_shapes)))
    for n, shp in (("w_dw", (1, DEPTH * CONV_W, LANES)), ("w_s", (DEPTH, SG_G * SG_CHUNK, SG_CHUNK))):
        upd = _adamw(*[src[n].reshape(shp) for src in (weights, grads_out, mom_m, mom_v)], name="adamw_" + n)
        for dst, buf in zip((delta, new_m, new_v), upd):
            dst[n] = buf.reshape(weights[n].shape)

    order = ("g_mix", "w_in", "b_gate", "q_norm_g", "k_norm_g", "w_attn_o", "w_dw", "b_dw", "conv_ln_g", "conv_ln_b",
             "w_conv_o", "sg_ln_g", "sg_ln_b", "w_s", "b_s", "w_sg_o", "w_out", "g_ffn", "w_ff_gate", "w_ff_up",
             "w_ff_down", "g_final")
    return (loss, grad_x, *[grads_out[n] for n in order], *[delta[n] for n in order],
            *[new_m[n] for n in order], *[new_v[n] for n in order])
```

```python
import functools
import math

import jax
import jax.numpy as jnp
from jax import lax
from jax.experimental import pallas as pl
from jax.experimental.pallas import tpu as pltpu

f32, bf16 = jnp.float32, jnp.bfloat16

D_MODEL = 2048
SEQ = 2048
DEPTH = 2
GRID_W = 64
HEAD_DIM = 128
LANES = 128
N_Q = (D_MODEL // 2) // HEAD_DIM
N_KV = N_Q // 4
GRP = N_Q // N_KV
Q_COLS = N_Q * HEAD_DIM
KV_COLS = N_KV * HEAD_DIM
CONV_CH = D_MODEL // 2
CONV_W = 31
CONV_PAD = CONV_W // 2
CONV_WP = 32
SG_CH = D_MODEL // 2
SG_G = SG_CH // LANES
SG_CHUNK = 128
D_FF = -(-8 * D_MODEL // (3 * 256)) * 256
OFF_KV = Q_COLS
OFF_CONV = OFF_KV + 2 * KV_COLS
OFF_SG = OFF_CONV + 2 * CONV_CH
OFF_GATE = OFF_SG + 2 * SG_CH
IN_COLS = OFF_GATE + 3 * D_MODEL
ROPE_THETA = 10000.0
SCALE = HEAD_DIM ** -0.5
N_DEV = 8
N_CHIP = 4

ADAM_LR, ADAM_B1, ADAM_B2, ADAM_EPS, ADAM_WD, ADAM_STEP = 0.001, 0.9, 0.999, 1e-08, 0.01, 10

VMEM_BYTES_V7X = 64 << 20
VMEM_CAP = VMEM_BYTES_V7X - (6 << 20)
MESH = pl.DeviceIdType.MESH
HBM = pl.BlockSpec(memory_space=pltpu.HBM)


def _in_hbm(a):
    if isinstance(a, jax.Array) and jnp.issubdtype(a.dtype, jnp.floating) and a.size * a.dtype.itemsize >= (1 << 20):
        return pltpu.with_memory_space_constraint(a, pltpu.HBM)
    return a


def _out_hbm(s):
    if isinstance(s, jax.ShapeDtypeStruct) and math.prod(s.shape) * jnp.dtype(s.dtype).itemsize >= (1 << 20):
        return pltpu.HBM(s.shape, s.dtype)
    return s


def _call(body, **kw):
    shapes = kw.pop("out_shape")
    shapes = type(shapes)(_out_hbm(s) for s in shapes) if isinstance(shapes, (list, tuple)) else _out_hbm(shapes)
    call = pl.pallas_call(body, out_shape=shapes, **kw)
    return lambda *args: call(*[_in_hbm(a) for a in args])


def _pick(n, cands):
    for c in cands:
        if n % c == 0:
            return c
    raise ValueError((n, cands))


def _params(sem, vmem_bytes):
    return pltpu.CompilerParams(dimension_semantics=sem, vmem_limit_bytes=int(min(max(vmem_bytes, 16 << 20), VMEM_CAP)))


def _mm(a, b, form, out_dtype, *, n=None, b_off=0, res=None, after=None, name):
    if form == "tn":
        K, M = a.shape
    else:
        M, K = a.shape
    N = n if n is not None else (b.shape[0] if form == "nt" else b.shape[1])
    if K <= 2048:
        tk = K
        if form == "tn":
            tm = _pick(M, (512, 256, 128))
            tn = N if N <= 2048 else _pick(N, (1024, 512, 256, 128))
        else:
            tm = M if M <= 2048 else _pick(M, (2048, 1024, 512))
            tn = _pick(math.gcd(N, b_off) if b_off else N, (256, 128) if res is not None else (512, 256, 128))
    else:
        tk = max(t for t in range(LANES, 3072 + 1, LANES) if K % t == 0)
        tm = _pick(M, (1024, 512, 256, 128))
        tn = _pick(math.gcd(N, b_off) if b_off else N, (1024, 512, 256, 128))
    assert b_off % tn == 0
    off = b_off // tn
    nk = K // tk
    if form == "tn":
        a_spec = pl.BlockSpec((tk, tm), lambda i, j, k: (k, i))
    else:
        a_spec = pl.BlockSpec((tm, tk), lambda i, j, k: (i, k))
    if form == "nt":
        b_spec = pl.BlockSpec((tn, tk), lambda i, j, k: (j + off, k))
    else:
        b_spec = pl.BlockSpec((tk, tn), lambda i, j, k: (k, j + off))
    dims = {"nn": ((1,), (0,)), "nt": ((1,), (1,)), "tn": ((0,), (0,))}[form]
    has_res = res is not None

    def body(*refs):
        if after is not None:
            refs = refs[1:]
        if has_res:
            a_ref, b_ref, r_ref, o_ref = refs[:4]
        else:
            a_ref, b_ref, o_ref = refs[:3]
        p = lax.dot_general(a_ref[...], b_ref[...], (dims, ((), ())), preferred_element_type=f32)

        def finish(acc):
            if has_res:
                acc = acc + r_ref[...].astype(f32)
            o_ref[...] = acc.astype(o_ref.dtype)

        if nk == 1:
            finish(p)
        else:
            acc_ref = refs[-1]
            k = pl.program_id(2)

            @pl.when(k == 0)
            def _():
                acc_ref[...] = p

            @pl.when(k > 0)
            def _():
                acc_ref[...] += p

            @pl.when(k == nk - 1)
            def _():
                finish(acc_ref[...])

    in_specs = [a_spec, b_spec]
    args = [a, b]
    osz = jnp.dtype(out_dtype).itemsize
    vmem = 2 * (tm * tk * 2 + tk * tn * 2 + tm * tn * osz) + 2 * tm * tn * 4
    if has_res:
        in_specs.append(pl.BlockSpec((tm, tn), lambda i, j, k: (i, j)))
        args.append(res)
        vmem += 2 * tm * tn * res.dtype.itemsize
    scratch = []
    if nk > 1:
        scratch.append(pltpu.VMEM((tm, tn), f32))
        vmem += tm * tn * 4
    if after is not None:
        in_specs.insert(0, pl.BlockSpec(memory_space=pl.ANY))
        args.insert(0, after)
    return _call(
        body, name=name, grid=(M // tm, N // tn, nk),
        in_specs=in_specs, out_specs=pl.BlockSpec((tm, tn), lambda i, j, k: (i, j)),
        out_shape=jax.ShapeDtypeStruct((M, N), out_dtype), scratch_shapes=scratch,
        compiler_params=_params(("parallel", "parallel", "arbitrary"), vmem + (8 << 20)),
    )(*args)


EPI_TN = 256


def _mm_epi(a, bs, form, extras, out_dtypes, n_sums, fn, name):
    a_list = list(a) if isinstance(a, (list, tuple)) else [a]
    M, K = a_list[0].shape
    N = bs[0].shape[0] if form == "nt" else bs[0].shape[1]
    tn = EPI_TN
    assert K <= 2048 and N % tn == 0 and len(a_list) in (1, len(bs))
    dims = ((1,), (1,)) if form == "nt" else ((1,), (0,))
    na, nb, ne = len(a_list), len(bs), len(extras)

    def body(*refs):
        a_refs, b_refs = refs[:na], refs[na:na + nb]
        e_refs, o_refs = refs[na + nb:na + nb + ne], refs[na + nb + ne:]
        avs = [r[...] for r in a_refs] * (nb // na)
        ps = [lax.dot_general(av, b[...], (dims, ((), ())), preferred_element_type=f32) for av, b in zip(avs, b_refs)]
        for o_ref, o in zip(o_refs, fn(ps, [e[...] for e in e_refs])):
            o_ref[...] = o.astype(o_ref.dtype)

    in_specs = [pl.BlockSpec((M, K), lambda j: (0, 0), pipeline_mode=pl.Buffered(1)) for _ in a_list]
    in_specs += [pl.BlockSpec((tn, K), lambda j: (j, 0)) if form == "nt" else pl.BlockSpec((K, tn), lambda j: (0, j))
                 for _ in bs]
    for arr, first in extras:
        assert first % tn == 0
        in_specs.append(pl.BlockSpec((arr.shape[0], tn), functools.partial(lambda j, o: (0, j + o), o=first // tn)))
    out_specs = [pl.BlockSpec((M, tn), lambda j: (0, j))] * len(out_dtypes) + [pl.BlockSpec((1, tn), lambda j: (0, j))] * n_sums
    out_shape = [jax.ShapeDtypeStruct((M, N), dt) for dt in out_dtypes] + [jax.ShapeDtypeStruct((1, N), f32)] * n_sums
    tiles = sum(arr.shape[0] * tn * arr.dtype.itemsize for arr, _ in extras) + sum(M * tn * jnp.dtype(dt).itemsize for dt in out_dtypes)
    vmem = na * M * K * 2 + 2 * nb * tn * K * 2 + 2 * tiles + (nb + 6) * M * tn * 4
    return _call(body, name=name, grid=(N // tn,), in_specs=in_specs, out_specs=out_specs, out_shape=out_shape,
                 compiler_params=_params(("parallel",), vmem + (8 << 20)))(*a_list, *bs, *[arr for arr, _ in extras])


def _ffn_up(hf, wt_gate, wt_up, name):
    def fn(ps, _):
        g, u = ps[0].astype(bf16), ps[1].astype(bf16)
        gf = g.astype(f32)
        return g, u, gf * jax.nn.sigmoid(gf) * u.astype(f32)

    return _mm_epi(hf, [wt_gate, wt_up], "nt", [], [bf16] * 3, 0, fn, name)


def _ffn_down_bwd(dx2b, w_down, fg, fu, name):
    def fn(ps, es):
        d, g = ps[0], es[0].astype(f32)
        sg = jax.nn.sigmoid(g)
        return d * es[1].astype(f32) * sg * (1.0 + g * (1.0 - sg)), d * g * sg

    return _mm_epi(dx2b, [w_down], "nt", [(fg, 0), (fu, 0)], [bf16] * 2, 0, fn, name)


def _mixer_out(branches, wts, gl, b_gate, name):
    D = wts[0].shape[0]

    def fn(ps, es):
        ys = [p_.astype(bf16) for p_ in ps]
        merged = None
        for i in range(3):
            term = jax.nn.sigmoid(es[i].astype(f32) + es[3 + i]) * ys[i].astype(f32)
            merged = term if merged is None else merged + term
        return ys + [merged]

    extras = [(gl, i * D) for i in range(3)] + [(b_gate, i * D) for i in range(3)]
    return _mm_epi(branches, wts, "nt", extras, [bf16] * 4, 0, fn, name)


def _merge_bwd_fused(dx1b, w_out, gl, b_gate, ya, yc, ys, name):
    D = ya.shape[1]

    def fn(ps, es):
        dm_, outs, sums = ps[0], [], []
        for i in range(3):
            gate = jax.nn.sigmoid(es[i].astype(f32) + es[3 + i])
            dlog = dm_ * es[6 + i].astype(f32) * gate * (1.0 - gate)
            outs.append((dlog, dm_ * gate))
            sums.append(jnp.sum(dlog, axis=0, keepdims=True))
        return [o[0] for o in outs] + [o[1] for o in outs] + sums

    extras = [(gl, i * D) for i in range(3)] + [(b_gate, i * D) for i in range(3)] + [(ya, 0), (yc, 0), (ys, 0)]
    return _mm_epi(dx1b, [w_out], "nt", extras, [bf16] * 6, 3, fn, name)


def _rows(body, ins, outs, *, tm, name, vmem=40 << 20):
    nrows = next(s[1].shape[0] for s in ins if s[0] == "r")
    in_specs, args = [], []
    for s in ins:
        arr = s[1]
        if s[0] == "r":
            w = s[2] if len(s) > 2 else arr.shape[1]
            cb = s[3] if len(s) > 3 else 0
            in_specs.append(pl.BlockSpec((tm, w), functools.partial(lambda i, cb: (i, cb), cb=cb)))
        else:
            in_specs.append(pl.BlockSpec(arr.shape, functools.partial(lambda i, nd: (0,) * nd, nd=arr.ndim)))
        args.append(arr)
    out_specs, out_shape = [], []
    for s in outs:
        if s[0] == "r":
            out_specs.append(pl.BlockSpec((tm, s[1]), lambda i: (i, 0)))
            out_shape.append(jax.ShapeDtypeStruct((nrows, s[1]), s[2]))
        else:
            out_specs.append(pl.BlockSpec(s[1], functools.partial(lambda i, nd: (0,) * nd, nd=len(s[1]))))
            out_shape.append(jax.ShapeDtypeStruct(s[1], s[2]))
    return _call(body, name=name, grid=(nrows // tm,), in_specs=in_specs, out_specs=out_specs,
                 out_shape=out_shape, compiler_params=_params(("arbitrary",), vmem))(*args)


def _accumulate(ref, part):
    i = pl.program_id(0)

    @pl.when(i == 0)
    def _():
        ref[...] = part

    @pl.when(i > 0)
    def _():
        ref[...] += part


def _rms_stats(x):
    r = lax.rsqrt(jnp.mean(x * x, axis=-1, keepdims=True) + 1e-6)
    return r, x * r


def _rms_fwd(x, g, name):
    def body(x_ref, g_ref, o_ref):
        _, xn = _rms_stats(x_ref[...])
        o_ref[...] = (xn * g_ref[...]).astype(o_ref.dtype)

    return _rows(body, [("r", x), ("f", g)], [("r", x.shape[1], bf16)], tm=min(256, x.shape[0]), name=name)[0]


def _rms_bwd(x, g, dh, dres, name):
    D = x.shape[1]

    def body(x_ref, g_ref, dh_ref, dr_ref, dx_ref, dxb_ref, dg_ref):
        r, xn = _rms_stats(x_ref[...])
        dy = dh_ref[...].astype(f32)
        dxn = dy * g_ref[...]
        dx = dr_ref[...] + r * (dxn - xn * jnp.mean(dxn * xn, axis=-1, keepdims=True))
        dx_ref[...] = dx
        dxb_ref[...] = dx.astype(bf16)
        _accumulate(dg_ref, jnp.sum(dy * xn, axis=0, keepdims=True))

    return _rows(body, [("r", x), ("f", g), ("r", dh), ("r", dres)],
                 [("r", D, f32), ("r", D, bf16), ("a", (1, D), f32)], tm=min(256, x.shape[0]), name=name)


def _final_loss(x, g, tgt, name):
    D = x.shape[1]

    def body(x_ref, g_ref, t_ref, dx_ref, dxb_ref, sq_ref, dg_ref):
        r, xn = _rms_stats(x_ref[...])
        gain = g_ref[...]
        diff = xn * gain - t_ref[...]
        dy = diff * (1.0 / D)
        dxn = dy * gain
        dx = r * (dxn - xn * jnp.mean(dxn * xn, axis=-1, keepdims=True))
        dx_ref[...] = dx
        dxb_ref[...] = dx.astype(bf16)
        _accumulate(sq_ref, jnp.sum(diff * diff, axis=0, keepdims=True))
        _accumulate(dg_ref, jnp.sum(dy * xn, axis=0, keepdims=True))

    return _rows(body, [("r", x), ("f", g), ("r", tgt)],
                 [("r", D, f32), ("r", D, bf16), ("a", (1, D), f32), ("a", (1, D), f32)],
                 tm=min(256, x.shape[0]), name=name)


def _qk_fwd(q_raw, kv_raw, qg, kg, cos2, sin2, name):
    def body(q_ref, k_ref, qg_ref, kg_ref, c_ref, s_ref, qo_ref, ko_ref):
        c, s = c_ref[...], s_ref[...]

        def head(src, gain, dst, h):
            cols = slice(h * HEAD_DIM, (h + 1) * HEAD_DIM)
            _, xn = _rms_stats(src[:, cols].astype(f32))
            y = xn * gain
            dst[:, cols] = (y * c + pltpu.roll(y, HEAD_DIM // 2, 1) * s).astype(dst.dtype)

        for h in range(N_Q):
            head(q_ref, qg_ref[...], qo_ref, h)
        for h in range(N_KV):
            head(k_ref, kg_ref[...], ko_ref, h)

    return _rows(body, [("r", q_raw), ("r", kv_raw, KV_COLS, 0), ("f", qg), ("f", kg), ("r", cos2), ("r", sin2)],
                 [("r", Q_COLS, bf16), ("r", KV_COLS, bf16)], tm=min(256, q_raw.shape[0]), name=name)


def _qk_bwd(q_raw, kv_raw, dqr, dkr, qg, kg, cos2, sin2, name):
    def body(q_ref, k_ref, dq_ref, dk_ref, qg_ref, kg_ref, c_ref, s_ref, dqo_ref, dko_ref, dqg_ref, dkg_ref):
        c, s = c_ref[...], s_ref[...]

        def head(src, dsrc, gain, dst, h):
            cols = slice(h * HEAD_DIM, (h + 1) * HEAD_DIM)
            r, xn = _rms_stats(src[:, cols].astype(f32))
            do = dsrc[:, cols].astype(f32)
            dy = do * c + pltpu.roll(do * s, HEAD_DIM // 2, 1)
            dxn = dy * gain
            dst[:, cols] = (r * (dxn - xn * jnp.mean(dxn * xn, axis=-1, keepdims=True))).astype(dst.dtype)
            return jnp.sum(dy * xn, axis=0, keepdims=True)

        dq_gain = head(q_ref, dq_ref, qg_ref[...], dqo_ref, 0)
        for h in range(1, N_Q):
            dq_gain = dq_gain + head(q_ref, dq_ref, qg_ref[...], dqo_ref, h)
        dk_gain = head(k_ref, dk_ref, kg_ref[...], dko_ref, 0)
        for h in range(1, N_KV):
            dk_gain = dk_gain + head(k_ref, dk_ref, kg_ref[...], dko_ref, h)
        _accumulate(dqg_ref, dq_gain)
        _accumulate(dkg_ref, dk_gain)

    return _rows(body, [("r", q_raw), ("r", kv_raw, KV_COLS, 0), ("r", dqr), ("r", dkr), ("f", qg), ("f", kg),
                        ("r", cos2), ("r", sin2)],
                 [("r", Q_COLS, bf16), ("r", KV_COLS, bf16), ("a", (1, HEAD_DIM), f32), ("a", (1, HEAD_DIM), f32)],
                 tm=min(256, q_raw.shape[0]), name=name)


def _softmax_rows(q, k):
    s = lax.dot_general(q, k, (((1,), (1,)), ((), ())), preferred_element_type=f32) * SCALE
    p = jnp.exp(s - jnp.max(s, axis=-1, keepdims=True))
    return p * (1.0 / jnp.sum(p, axis=-1, keepdims=True))


def _head_cols(g):
    return slice(g * HEAD_DIM, (g + 1) * HEAD_DIM)


def _attn_fwd(qr, kr, kv_raw, name):
    S = qr.shape[0]
    tq = min(256, S)

    def body(q_ref, k_ref, v_ref, o_ref):
        k, v = k_ref[...], v_ref[...]
        for g in range(GRP):
            p = _softmax_rows(q_ref[:, _head_cols(g)], k)
            o_ref[:, _head_cols(g)] = jnp.dot(p.astype(bf16), v, preferred_element_type=f32).astype(o_ref.dtype)

    return _call(
        body, name=name, grid=(N_KV, S // tq),
        in_specs=[pl.BlockSpec((tq, GRP * HEAD_DIM), lambda kv, i: (i, kv)),
                  pl.BlockSpec((S, HEAD_DIM), lambda kv, i: (0, kv)),
                  pl.BlockSpec((S, HEAD_DIM), lambda kv, i: (0, N_KV + kv))],
        out_specs=pl.BlockSpec((tq, GRP * HEAD_DIM), lambda kv, i: (i, kv)),
        out_shape=jax.ShapeDtypeStruct((S, Q_COLS), bf16),
        compiler_params=_params(("parallel", "arbitrary"), 4 * GRP * tq * S * 4 + (8 << 20)),
    )(qr, kr, kv_raw)


def _attn_bwd(qr, kr, kv_raw, do, name):
    S = qr.shape[0]
    tq = min(256, S)

    def body(q_ref, k_ref, v_ref, do_ref, dq_ref, dk_ref, dv_ref):
        first = pl.program_id(1) == 0
        k, v = k_ref[...], v_ref[...]
        dv_part = dk_part = None
        for g in range(GRP):
            q, do_ = q_ref[:, _head_cols(g)], do_ref[:, _head_cols(g)]
            p = _softmax_rows(q, k)
            dp = lax.dot_general(do_, v, (((1,), (1,)), ((), ())), preferred_element_type=f32)
            ds = (p * (dp - jnp.sum(dp * p, axis=-1, keepdims=True)) * SCALE).astype(bf16)
            dq_ref[:, _head_cols(g)] = jnp.dot(ds, k, preferred_element_type=f32).astype(dq_ref.dtype)
            dv_g = lax.dot_general(p.astype(bf16), do_, (((0,), (0,)), ((), ())), preferred_element_type=f32)
            dk_g = lax.dot_general(ds, q, (((0,), (0,)), ((), ())), preferred_element_type=f32)
            dv_part = dv_g if g == 0 else dv_part + dv_g
            dk_part = dk_g if g == 0 else dk_part + dk_g

        @pl.when(first)
        def _():
            dv_ref[...] = dv_part
            dk_ref[...] = dk_part

        @pl.when(jnp.logical_not(first))
        def _():
            dv_ref[...] += dv_part
            dk_ref[...] += dk_part

    qspec = pl.BlockSpec((tq, GRP * HEAD_DIM), lambda kv, i: (i, kv))
    return _call(
        body, name=name, grid=(N_KV, S // tq),
        in_specs=[qspec, pl.BlockSpec((S, HEAD_DIM), lambda kv, i: (0, kv)),
                  pl.BlockSpec((S, HEAD_DIM), lambda kv, i: (0, N_KV + kv)), qspec],
        out_specs=[qspec, pl.BlockSpec((S, HEAD_DIM), lambda kv, i: (0, kv)),
                   pl.BlockSpec((S, HEAD_DIM), lambda kv, i: (0, kv))],
        out_shape=[jax.ShapeDtypeStruct((S, Q_COLS), bf16), jax.ShapeDtypeStruct((S, KV_COLS), f32),
                   jax.ShapeDtypeStruct((S, KV_COLS), f32)],
        compiler_params=_params(("parallel", "arbitrary"), 6 * GRP * tq * S * 4 + (8 << 20)),
    )(qr, kr, kv_raw, do)


CONV_HALO = 16


def _fill_padded(pad_ref, val, S):
    pad_ref[pl.ds(0, CONV_HALO), :] = jnp.zeros((CONV_HALO, LANES), f32)
    pad_ref[pl.ds(CONV_HALO + S, CONV_HALO), :] = jnp.zeros((CONV_HALO, LANES), f32)
    pad_ref[pl.ds(CONV_HALO, S), :] = val


def _group_specs(S, n_groups, second_half):
    return pl.BlockSpec((S, LANES), functools.partial(lambda g, o: (0, g + o), o=n_groups if second_half else 0))


def _conv1_fwd(conv_in, wdw, b_dw, name):
    S = conv_in.shape[0]
    ng = CONV_CH // LANES
    R = min(256, S)

    def body(a_ref, g_ref, w_ref, b_ref, o_ref, pad_ref):
        z = a_ref[...].astype(f32) * jax.nn.sigmoid(g_ref[...].astype(f32))
        _fill_padded(pad_ref, z, S)
        for r in range(S // R):
            acc = jnp.zeros((R, LANES), f32) + b_ref[...]
            for j in range(CONV_W):
                acc = acc + w_ref[pl.ds(j, 1), :] * pad_ref[pl.ds(r * R + CONV_HALO - CONV_PAD + j, R), :]
            o_ref[pl.ds(r * R, R), :] = acc

    return _call(
        body, name=name, grid=(ng,),
        in_specs=[_group_specs(S, ng, False), _group_specs(S, ng, True),
                  pl.BlockSpec((CONV_WP, LANES), lambda g: (g, 0)), pl.BlockSpec((1, LANES), lambda g: (0, g))],
        out_specs=pl.BlockSpec((S, LANES), lambda g: (0, g)),
        out_shape=jax.ShapeDtypeStruct((S, CONV_CH), f32),
        scratch_shapes=[pltpu.VMEM((S + 2 * CONV_HALO, LANES), f32)],
        compiler_params=_params(("parallel",), 24 << 20),
    )(conv_in, conv_in, wdw, b_dw)


def _conv1_bwd(conv_in, dc, wdw, name):
    S = conv_in.shape[0]
    ng = CONV_CH // LANES
    R = min(256, S)

    def body(a_ref, g_ref, w_ref, dc_ref, da_ref, dg_ref, dw_ref, db_ref, padz_ref, padd_ref):
        a = a_ref[...].astype(f32)
        sg = jax.nn.sigmoid(g_ref[...].astype(f32))
        _fill_padded(padz_ref, a * sg, S)
        _fill_padded(padd_ref, dc_ref[...], S)
        for r in range(S // R):
            dz = jnp.zeros((R, LANES), f32)
            for j in range(CONV_W):
                dz = dz + w_ref[pl.ds(j, 1), :] * padd_ref[pl.ds(r * R + CONV_HALO + CONV_PAD - j, R), :]
            rows = pl.ds(r * R, R)
            ar, sr = a_ref[rows, :].astype(f32), jax.nn.sigmoid(g_ref[rows, :].astype(f32))
            da_ref[rows, :] = (dz * sr).astype(da_ref.dtype)
            dg_ref[rows, :] = (dz * ar * sr * (1.0 - sr)).astype(dg_ref.dtype)
        for j in range(CONV_W):
            tot = jnp.zeros((1, LANES), f32)
            for r in range(S // R):
                tot = tot + jnp.sum(dc_ref[pl.ds(r * R, R), :] * padz_ref[pl.ds(r * R + CONV_HALO - CONV_PAD + j, R), :],
                                    axis=0, keepdims=True)
            dw_ref[pl.ds(j, 1), :] = tot
        dw_ref[pl.ds(CONV_W, CONV_WP - CONV_W), :] = jnp.zeros((CONV_WP - CONV_W, LANES), f32)
        db_ref[...] = jnp.sum(dc_ref[...], axis=0, keepdims=True)

    return _call(
        body, name=name, grid=(ng,),
        in_specs=[_group_specs(S, ng, False), _group_specs(S, ng, True),
                  pl.BlockSpec((CONV_WP, LANES), lambda g: (g, 0)), pl.BlockSpec((S, LANES), lambda g: (0, g))],
        out_specs=[pl.BlockSpec((S, LANES), lambda g: (0, g)), pl.BlockSpec((S, LANES), lambda g: (0, g)),
                   pl.BlockSpec((CONV_WP, LANES), lambda g: (g, 0)), pl.BlockSpec((1, LANES), lambda g: (0, g))],
        out_shape=[jax.ShapeDtypeStruct((S, CONV_CH), bf16), jax.ShapeDtypeStruct((S, CONV_CH), bf16),
                   jax.ShapeDtypeStruct((ng * CONV_WP, LANES), f32), jax.ShapeDtypeStruct((1, CONV_CH), f32)],
        scratch_shapes=[pltpu.VMEM((S + 2 * CONV_HALO, LANES), f32), pltpu.VMEM((S + 2 * CONV_HALO, LANES), f32)],
        compiler_params=_params(("parallel",), 24 << 20),
    )(conv_in, conv_in, wdw, dc)


def _ln_stats(x, eps=1e-5):
    xc = x - jnp.mean(x, axis=-1, keepdims=True)
    r = lax.rsqrt(jnp.mean(xc * xc, axis=-1, keepdims=True) + eps)
    return r, xc * r


def _ln_bwd(r, xh, dxh):
    return r * (dxh - jnp.mean(dxh, axis=-1, keepdims=True) - xh * jnp.mean(dxh * xh, axis=-1, keepdims=True))


def _conv2_fwd(c, ln_g, ln_b, name):
    def body(c_ref, g_ref, b_ref, o_ref):
        _, xh = _ln_stats(c_ref[...])
        y = xh * g_ref[...] + b_ref[...]
        o_ref[...] = (y * jax.nn.sigmoid(y)).astype(o_ref.dtype)

    return _rows(body, [("r", c), ("f", ln_g), ("f", ln_b)], [("r", CONV_CH, bf16)], tm=min(256, c.shape[0]), name=name)[0]


def _conv2_bwd(c, dcz, ln_g, ln_b, name):
    def body(c_ref, d_ref, g_ref, b_ref, dc_ref, dg_ref, db_ref):
        r, xh = _ln_stats(c_ref[...])
        y = xh * g_ref[...] + b_ref[...]
        sg = jax.nn.sigmoid(y)
        dy = d_ref[...].astype(f32) * (sg * (1.0 + y * (1.0 - sg)))
        dc_ref[...] = _ln_bwd(r, xh, dy * g_ref[...])
        _accumulate(dg_ref, jnp.sum(dy * xh, axis=0, keepdims=True))
        _accumulate(db_ref, jnp.sum(dy, axis=0, keepdims=True))

    return _rows(body, [("r", c), ("r", dcz), ("f", ln_g), ("f", ln_b)],
                 [("r", CONV_CH, f32), ("a", (1, CONV_CH), f32), ("a", (1, CONV_CH), f32)],
                 tm=min(256, c.shape[0]), name=name)


GELU_K = math.sqrt(2.0 / math.pi)
GELU_C = 0.044715


def _gelu(x):
    return 0.5 * x * (1.0 + jnp.tanh(GELU_K * (x + GELU_C * x * x * x)))


def _gelu_and_grad(x):
    x2 = x * x
    th = jnp.tanh(GELU_K * (x + GELU_C * x2 * x))
    half = 0.5 * (1.0 + th)
    return x * half, half + 0.5 * x * (1.0 - th * th) * (GELU_K * (1.0 + 3.0 * GELU_C * x2))


def _chunk_rows(n):
    return pl.ds(pl.multiple_of(n * SG_CHUNK, SG_CHUNK), SG_CHUNK)


def _sgu_fwd(sg_in, ln_g, ln_b, w_s, b_s, name):
    S = sg_in.shape[0]

    def body(u_ref, v_ref, lg_ref, lb_ref, w_ref, b_ref, o_ref):
        wb = w_ref[...].astype(bf16)

        def chunk(n, carry):
            rows = _chunk_rows(n)
            gu = _gelu(u_ref[rows, :].astype(f32))
            _, xh = _ln_stats(_gelu(v_ref[rows, :].astype(f32)))
            vl = xh * lg_ref[...] + lb_ref[...]
            t = jnp.dot(wb, vl.astype(bf16), preferred_element_type=f32) + b_ref[...]
            o_ref[rows, :] = (gu * t).astype(o_ref.dtype)
            return carry

        lax.fori_loop(0, S // SG_CHUNK, chunk, 0, unroll=2)

    return _call(
        body, name=name, grid=(SG_G,),
        in_specs=[_group_specs(S, SG_G, False), _group_specs(S, SG_G, True),
                  pl.BlockSpec((1, LANES), lambda g: (0, g)), pl.BlockSpec((1, LANES), lambda g: (0, g)),
                  pl.BlockSpec((None, SG_CHUNK, SG_CHUNK), lambda g: (g, 0, 0)),
                  pl.BlockSpec((None, SG_CHUNK, 1), lambda g: (g, 0, 0))],
        out_specs=pl.BlockSpec((S, LANES), lambda g: (0, g)),
        out_shape=jax.ShapeDtypeStruct((S, SG_CH), bf16),
        compiler_params=_params(("parallel",), 24 << 20),
    )(sg_in, sg_in, ln_g, ln_b, w_s, b_s)


def _sgu_bwd(sg_in, dsz, ln_g, ln_b, w_s, w_s_t, b_s, name):
    S = sg_in.shape[0]

    def body(u_ref, v_ref, lg_ref, lb_ref, w_ref, wt_ref, b_ref, d_ref, du_ref, dv_ref, dw_ref, db_ref, dlg_ref, dlb_ref):
        wb = w_ref[...].astype(bf16)
        wtb = wt_ref[...].astype(bf16)

        def chunk(n, carry):
            dwa, dba, dlga, dlba = carry
            rows = _chunk_rows(n)
            u = u_ref[rows, :].astype(f32)
            v = v_ref[rows, :].astype(f32)
            gu, gu_grad = _gelu_and_grad(u)
            gv, gv_grad = _gelu_and_grad(v)
            r, xh = _ln_stats(gv)
            vlb = (xh * lg_ref[...] + lb_ref[...]).astype(bf16)
            t = jnp.dot(wb, vlb, preferred_element_type=f32) + b_ref[...]
            d = d_ref[rows, :].astype(f32)
            dt = d * gu
            dtb = dt.astype(bf16)
            dwa = dwa + lax.dot_general(dtb, vlb, (((1,), (1,)), ((), ())), preferred_element_type=f32)
            dba = dba + jnp.sum(dt, axis=1, keepdims=True)
            dvl = jnp.dot(wtb, dtb, preferred_element_type=f32)
            dlga = dlga + jnp.sum(dvl * xh, axis=0, keepdims=True)
            dlba = dlba + jnp.sum(dvl, axis=0, keepdims=True)
            dgv = _ln_bwd(r, xh, dvl * lg_ref[...])
            du_ref[rows, :] = (d * t * gu_grad).astype(du_ref.dtype)
            dv_ref[rows, :] = (dgv * gv_grad).astype(dv_ref.dtype)
            return dwa, dba, dlga, dlba

        init = (jnp.zeros((SG_CHUNK, SG_CHUNK), f32), jnp.zeros((SG_CHUNK, 1), f32),
                jnp.zeros((1, LANES), f32), jnp.zeros((1, LANES), f32))
        dwa, dba, dlga, dlba = lax.fori_loop(0, S // SG_CHUNK, chunk, init, unroll=2)
        dw_ref[...] = dwa
        db_ref[...] = dba
        dlg_ref[...] = dlga
        dlb_ref[...] = dlba

    wspec = pl.BlockSpec((None, SG_CHUNK, SG_CHUNK), lambda g: (g, 0, 0))
    bspec = pl.BlockSpec((None, SG_CHUNK, 1), lambda g: (g, 0, 0))
    lspec = pl.BlockSpec((1, LANES), lambda g: (0, g))
    cspec = pl.BlockSpec((S, LANES), lambda g: (0, g))
    return _call(
        body, name=name, grid=(SG_G,),
        in_specs=[_group_specs(S, SG_G, False), _group_specs(S, SG_G, True), lspec, lspec, wspec, wspec, bspec, cspec],
        out_specs=[cspec, cspec, wspec, bspec, lspec, lspec],
        out_shape=[jax.ShapeDtypeStruct((S, SG_CH), bf16), jax.ShapeDtypeStruct((S, SG_CH), bf16),
                   jax.ShapeDtypeStruct((SG_G, SG_CHUNK, SG_CHUNK), f32), jax.ShapeDtypeStruct((SG_G, SG_CHUNK, 1), f32),
                   jax.ShapeDtypeStruct((1, SG_CH), f32), jax.ShapeDtypeStruct((1, SG_CH), f32)],
        compiler_params=_params(("parallel",), 24 << 20),
    )(sg_in, sg_in, ln_g, ln_b, w_s, w_s_t, b_s, dsz)


def _row_tile(r, c, n_arrays, itemsize=4):
    fits = [tm for tm in range(16, r + 1, 16) if r % tm == 0 and 2 * n_arrays * tm * c * itemsize <= (24 << 20)]
    return fits[-1] if fits else r


def _sum_slots(slots, name):
    n, r, c = slots.shape
    tm = _row_tile(r, c, n + 2)

    def body(s_ref, o_ref):
        acc = s_ref[0].astype(f32)
        for k in range(1, n):
            acc = acc + s_ref[k].astype(f32)
        o_ref[...] = acc

    return _call(body, name=name, grid=(r // tm,),
                 in_specs=[pl.BlockSpec((n, tm, c), lambda i: (0, i, 0))],
                 out_specs=pl.BlockSpec((tm, c), lambda i: (i, 0)),
                 out_shape=jax.ShapeDtypeStruct((r, c), f32),
                 compiler_params=_params(("parallel",), 40 << 20))(slots)


def _add_sibling(g4, recv, core, name):
    _, _, r, c = g4.shape
    tm = _row_tile(r, c, 3, 2)

    def body(core_ref, g_ref, r_ref, o_ref):
        o_ref[...] = (g_ref[...].astype(f32) + r_ref[...].astype(f32)).astype(o_ref.dtype)

    grid_spec = pltpu.PrefetchScalarGridSpec(
        num_scalar_prefetch=1, grid=(N_CHIP, r // tm),
        in_specs=[pl.BlockSpec((None, None, tm, c), lambda k, i, core_ref: (k, core_ref[0], i, 0)),
                  pl.BlockSpec((None, tm, c), lambda k, i, core_ref: (k, i, 0))],
        out_specs=pl.BlockSpec((None, tm, c), lambda k, i, core_ref: (k, i, 0)))
    return _call(body, name=name, grid_spec=grid_spec, out_shape=jax.ShapeDtypeStruct((N_CHIP, r, c), bf16),
                 compiler_params=_params(("parallel", "parallel"), 40 << 20))(core, g4, recv)


def _adamw(w, g, m, v, name):
    L, r, c = w.shape
    tm = _row_tile(r, c, 7)
    c1 = 1.0 - ADAM_B1 ** ADAM_STEP
    c2 = 1.0 - ADAM_B2 ** ADAM_STEP

    def body(w_ref, g_ref, m_ref, v_ref, d_ref, mo_ref, vo_ref):
        g_ = g_ref[...]
        m_ = ADAM_B1 * m_ref[...] + (1.0 - ADAM_B1) * g_
        v_ = ADAM_B2 * v_ref[...] + (1.0 - ADAM_B2) * (g_ * g_)
        d_ref[...] = -ADAM_LR * ((m_ / c1) / (jnp.sqrt(v_ / c2) + ADAM_EPS) + ADAM_WD * w_ref[...])
        mo_ref[...] = m_
        vo_ref[...] = v_

    spec = pl.BlockSpec((None, tm, c), lambda l, i: (l, i, 0))
    shp = jax.ShapeDtypeStruct((L, r, c), f32)
    return _call(body, name=name, grid=(L, r // tm), in_specs=[spec] * 4, out_specs=[spec] * 3,
                 out_shape=[shp] * 3, compiler_params=_params(("parallel", "parallel"), 40 << 20))(w, g, m, v)


def _mesh_pos():
    return lax.axis_index("x"), lax.axis_index("y"), lax.axis_index("c")


def _all_gather(shards, after, name):
    n = len(shards)

    def body(*refs):
        x_refs, o_refs = refs[:n], refs[n + 1:2 * n + 1]
        send_sems, recv_sems, local_sems = refs[2 * n + 1:]
        x, y, c = _mesh_pos()
        me, sibling = (x, y, c), (x, y, 1 - c)
        chips = [(1 - x, y), (x, 1 - y), (1 - x, 1 - y)]

        def rows(k, px, py, pc):
            return o_refs[k].at[4 * px + 2 * py + pc]

        def copy(k, s, block, to, src=None):
            return pltpu.make_async_remote_copy(
                src_ref=rows(k, *block) if src is None else src, dst_ref=rows(k, *block),
                send_sem=send_sems.at[k, s], recv_sem=recv_sems.at[k, s], device_id=to, device_id_type=MESH)

        mine = [pltpu.make_async_copy(x_refs[k], rows(k, *me), local_sems.at[k]) for k in range(n)]
        for cp in mine:
            cp.start()
        first = [copy(k, 0, me, sibling, src=x_refs[k]) for k in range(n)]
        for j, chip in enumerate(chips):
            first += [copy(k, 1 + j, me, (*chip, c), src=x_refs[k]) for k in range(n)]
        for cp in first:
            cp.start()
        passed = []
        for j, chip in enumerate(chips):
            for k in range(n):
                copy(k, 1 + j, (*chip, c), me).wait_recv()
                fwd = copy(k, 4 + j, (*chip, c), sibling)
                fwd.start()
                passed.append(fwd)
        for k in range(n):
            copy(k, 0, sibling, me).wait_recv()
        for j, chip in enumerate(chips):
            for k in range(n):
                copy(k, 4 + j, (*chip, 1 - c), me).wait_recv()
        for cp in first + passed:
            cp.wait_send()
        for cp in mine:
            cp.wait()

    return _call(
        body, name=name, in_specs=[HBM] * n + [ANY], out_specs=[HBM] * n,
        out_shape=[jax.ShapeDtypeStruct((N_DEV,) + s.shape, s.dtype) for s in shards],
        scratch_shapes=[pltpu.SemaphoreType.DMA((n, 7)), pltpu.SemaphoreType.DMA((n, 7)), pltpu.SemaphoreType.DMA((n,))],
    )(*shards, after)


SEM =pl.BlockSpec(memory_space=pltpu.SEMAPHORE)
ANY = pl.BlockSpec(memory_space=pl.ANY)
EFFECT = pltpu.SideEffectType.DATAFLOW_SIDE_EFFECTING


def _other_chips(x, y):
    return [(1 - x, y), (x, 1 - y), (1 - x, 1 - y)]


def _peers(kind, x, y):
    return [(x, y)] if kind == "sibling" else _other_chips(x, y)


def _ici_copy(kind, src_ref, land_ref, send_sem, recv_sem, sender, target, c):
    (sx, sy), (tx, ty) = sender, target
    if kind == "sibling":
        return pltpu.make_async_remote_copy(src_ref=src_ref.at[:, 1 - c], dst_ref=land_ref, send_sem=send_sem,
                                            recv_sem=recv_sem, device_id=(tx, ty, 1 - c), device_id_type=MESH)
    if kind == "gather":
        src, dst = src_ref, land_ref.at[4 * sx + 2 * sy + c]
    else:
        src, dst = src_ref.at[2 * tx + ty], land_ref.at[2 * sx + sy]
    return pltpu.make_async_remote_copy(src_ref=src, dst_ref=dst, send_sem=send_sem, recv_sem=recv_sem,
                                        device_id=(tx, ty, c), device_id_type=MESH)


def _ici_start(kind, srcs, lands, after, name):
    n = len(srcs)
    npeer = 1 if kind == "sibling" else 3

    def body(*refs):
        src_refs, land_refs = refs[:n], refs[n:2 * n]
        send_sems, recv_sems = refs[2 * n + 1], refs[2 * n + 2]
        token = refs[-1]
        x, y, c = _mesh_pos()
        for j, chip in enumerate(_peers(kind, x, y)):
            for k in range(n):
                _ici_copy(kind, src_refs[k], land_refs[k], send_sems.at[npeer * k + j], recv_sems.at[npeer * k + j],(x, y), chip, c).start()
        token[...] = jnp.zeros_like(token)

    bufs = list(srcs) + list(lands)
    return _call(
        body, name=name,
        out_shape=(pltpu.SemaphoreType.DMA((npeer * n,)), pltpu.SemaphoreType.DMA((npeer * n,)),
                   *[pltpu.HBM(b.shape, b.dtype) for b in bufs], jax.ShapeDtypeStruct((8, LANES), f32)),
        in_specs=[HBM] * (2 * n) + [ANY], out_specs=(SEM, SEM, *[HBM] * (2 * n), pl.BlockSpec(memory_space=pltpu.VMEM)),
        input_output_aliases={i: 2 + i for i in range(2 * n)},
        compiler_params=pltpu.CompilerParams(has_side_effects=EFFECT),
    )(*[pltpu.with_memory_space_constraint(b, pltpu.HBM) for b in bufs], after)


def _ici_wait(kind, started, after, name):
    send_sems, recv_sems, *bufs = started[:-1]
    n = len(bufs) // 2
    npeer = 1 if kind == "sibling" else 3

    def body(*refs):
        src_refs, land_refs = refs[:n], refs[n:2 * n]
        send_sems, recv_sems = refs[2 * n], refs[2 * n + 1]
        x, y, c = _mesh_pos()
        for j, chip in enumerate(_peers(kind, x, y)):
            for k in range(n):
                _ici_copy(kind, src_refs[k], land_refs[k], send_sems.at[npeer * k + j], recv_sems.at[npeer * k + j],(x, y), chip, c).wait_send()
                _ici_copy(kind, src_refs[k], land_refs[k], send_sems.at[npeer * k + j], recv_sems.at[npeer * k + j],chip, (x, y), c).wait_recv()

    out = _call(
        body, name=name, out_shape=[pltpu.HBM(b.shape, b.dtype) for b in bufs],
        in_specs=[HBM] * (2 * n) + [SEM, SEM, ANY], out_specs=[HBM] * (2 * n),
        input_output_aliases={i: i for i in range(2 * n)},
        compiler_params=pltpu.CompilerParams(has_side_effects=EFFECT),
    )(*bufs, send_sems, recv_sems, after)
    return out[:n], out[n:]


def _d2d_gather(lands, after, name):
    n = len(lands)

    def body(*refs):
        in_refs, o_refs = refs[:n], refs[n + 1:2 * n + 1]
        send_sems, recv_sems = refs[2 * n + 1:]
        x, y, c = _mesh_pos()
        copies = [pltpu.make_async_remote_copy(
            src_ref=in_refs[k].at[:, c], dst_ref=o_refs[k].at[:, c], send_sem=send_sems.at[k], recv_sem=recv_sems.at[k],
            device_id=(x, y, 1 - c), device_id_type=MESH) for k in range(n)]
        for cp in copies:
            cp.start()
        for k, cp in enumerate(copies):
            cp.wait_send()
            pltpu.make_async_remote_copy(
                src_ref=in_refs[k].at[:, c], dst_ref=o_refs[k].at[:, 1 - c], send_sem=send_sems.at[k],
                recv_sem=recv_sems.at[k], device_id=(x, y, 1 - c), device_id_type=MESH).wait_recv()

    return _call(
        body, name=name, in_specs=[HBM] * n + [ANY], out_specs=[HBM] * n,
        out_shape=[jax.ShapeDtypeStruct(b.shape, b.dtype) for b in lands],
        input_output_aliases={k: k for k in range(n)},
        scratch_shapes=[pltpu.SemaphoreType.DMA((n,)), pltpu.SemaphoreType.DMA((n,))],
    )(*lands, after)


def _sum_chip_slots(lands, sums, chip, name):
    _, r, c = lands.shape
    tm = _row_tile(r, c, 10, 2)

    def body(chip_ref, l_ref, s_ref, o_ref):
        acc = None
        for k in range(N_CHIP):
            part = jnp.where(chip_ref[0] == k, s_ref[k], l_ref[k]).astype(f32)
            acc = part if acc is None else acc + part
        o_ref[...] = acc

    grid_spec = pltpu.PrefetchScalarGridSpec(
        num_scalar_prefetch=1, grid=(r // tm,),
        in_specs=[pl.BlockSpec((N_CHIP, tm, c), lambda i, chip_ref: (0, i, 0)),
                  pl.BlockSpec((N_CHIP, tm, c), lambda i, chip_ref: (0, i, 0))],
        out_specs=pl.BlockSpec((tm, c), lambda i, chip_ref: (i, 0)))
    return _call(body, name=name, grid_spec=grid_spec, out_shape=jax.ShapeDtypeStruct((r, c), f32),
                 compiler_params=_params(("parallel",), 40 << 20))(chip, lands, sums)


def _reduce_begin(grads, core, tag):
    g4s = [g.reshape(N_CHIP, 2, g.shape[0] // N_DEV, g.shape[1]) for g in grads]
    recvs = [lax.empty((N_CHIP,) + g.shape[2:], g.dtype) for g in g4s]
    return _ici_start("sibling", g4s, recvs, core, name="rs_d2d_start_" + tag)


def _reduce_continue(begun, core, after, tag):
    g4s, recvs = _ici_wait("sibling", begun, after, name="rs_d2d_wait_" + tag)
    sums = [_add_sibling(g4, rv, core, name="rs_add_" + tag) for g4, rv in zip(g4s, recvs)]
    lands = [lax.empty(s.shape, s.dtype) for s in sums]
    return _ici_start("reduce", sums, lands, core, name="rs_start_" + tag)


def _adamw_reduced(layer, w, m, v, lands, sums, chip, prev, name):
    L, r, c = w.shape
    tm = _row_tile(r, c, 11)
    c1 = 1.0 - ADAM_B1 ** ADAM_STEP
    c2 = 1.0 - ADAM_B2 ** ADAM_STEP
    n_prev = 0 if prev is None else 4

    def body(chip_ref, w_ref, m_ref, v_ref, l_ref, s_ref, *refs):
        g_ref, d_ref, mo_ref, vo_ref = refs[n_prev:]
        g_ = None
        for k in range(N_CHIP):
            part = jnp.where(chip_ref[0] == k, s_ref[k], l_ref[k]).astype(f32)
            g_ = part if g_ is None else g_ + part
        m_ = ADAM_B1 * m_ref[...] + (1.0 - ADAM_B1) * g_
        v_ = ADAM_B2 * v_ref[...] + (1.0 - ADAM_B2) * (g_ * g_)
        g_ref[...] = g_
        d_ref[...] = -ADAM_LR * ((m_ / c1) / (jnp.sqrt(v_ / c2) + ADAM_EPS) + ADAM_WD * w_ref[...])
        mo_ref[...] = m_
        vo_ref[...] = v_

    wspec = pl.BlockSpec((None, tm, c), lambda i, chip_ref: (layer, i, 0))
    sspec = pl.BlockSpec((N_CHIP, tm, c), lambda i, chip_ref: (0, i, 0))
    grid_spec = pltpu.PrefetchScalarGridSpec(
        num_scalar_prefetch=1, grid=(r // tm,), in_specs=[wspec] * 3 + [sspec] * 2 + [ANY] * n_prev, out_specs=[wspec] * 4)
    return _call(body, name=name, grid_spec=grid_spec, out_shape=[jax.ShapeDtypeStruct((L, r, c), f32)] * 4,
                 input_output_aliases={6 + i: i for i in range(n_prev)},
                 compiler_params=_params(("parallel",), 40 << 20))(chip, w, m, v, lands, sums, *(prev or ()))


def _rope_tables(S):
    rows = S // GRID_W
    row = jnp.repeat(jnp.arange(rows, dtype=f32), GRID_W)
    col = jnp.tile(jnp.arange(GRID_W, dtype=f32), rows)
    nf = HEAD_DIM // 4
    inv = ROPE_THETA ** (-jnp.arange(nf, dtype=f32) / nf)
    ang = jnp.concatenate([row[:, None] * inv, col[:, None] * inv], axis=-1)
    cos, sin = jnp.cos(ang), jnp.sin(ang)
    return jnp.concatenate([cos, cos], axis=-1), jnp.concatenate([-sin, sin], axis=-1)


def _layer_fwd(xin, p, w, more_weights, cos2, sin2):
    sv = {"xin": xin}
    h = sv["h"] = _rms_fwd(xin, p["g_mix"], name="rms_mix")
    proj = functools.partial(_mm, h, w["in"], "nt", bf16)
    q_raw = sv["q_raw"] = proj(n=Q_COLS, b_off=0, name="proj_q")
    kv_raw = sv["kv_raw"] = proj(n=2 * KV_COLS, b_off=OFF_KV, name="proj_kv")
    conv_in = sv["conv_in"] = proj(n=2 * CONV_CH, b_off=OFF_CONV, name="proj_conv")
    sg_in = sv["sg_in"] = proj(n=2 * SG_CH, b_off=OFF_SG, name="proj_sg")
    gl = sv["gl"] = proj(n=3 * D_MODEL, b_off=OFF_GATE, name="proj_gate")
    qr, kr = sv["qr"], sv["kr"] = _qk_fwd(q_raw, kv_raw, p["q_norm_g"], p["k_norm_g"], cos2, sin2, name="qk_fwd")
    o = sv["o"] = _attn_fwd(qr, kr, kv_raw, name="attn_fwd")
    c = sv["c"] = _conv1_fwd(conv_in, w["dw"], p["b_dw"], name="conv1_fwd")
    cz = sv["cz"] = _conv2_fwd(c, p["conv_ln_g"], p["conv_ln_b"], name="conv2_fwd")
    sz = sv["sz"] = _sgu_fwd(sg_in, p["sg_ln_g"], p["sg_ln_b"], p["w_s"], p["b_s"], name="sgu_fwd")
    w = {**w, **more_weights(1, sz)}
    sv["ya"], sv["yc"], sv["ys"], merged = _mixer_out([o, cz, sz], [w["attn_o"], w["conv_o"], w["sg_o"]], gl, p["b_gate"],
                                                      name="mixer_out")
    sv["merged"] = merged
    x1 =sv["x1"] = _mm(merged, w["out"], "nn", f32, res=xin, name="out_proj")
    w = {**w, **more_weights(2, x1)}
    hf = sv["hf"] = _rms_fwd(x1, p["g_ffn"], name="rms_ffn")
    sv["fg"], sv["fu"], act = _ffn_up(hf, w["ff_gate"], w["ff_up"], name="ffn_up")
    sv["act"] = act
    x2 =_mm(act, w["ff_down"], "nn", f32, res=x1, name="ff_down")
    return x2, sv, w


def _layer_bwd(dx2, dx2b, sv, p, w, cos2, sin2, reduce_begin, reduce_continue, last):
    small = {}
    dfg, dfu = _ffn_down_bwd(dx2b, w["ff_down"], sv["fg"], sv["fu"], name="ffn_down_bwd")
    g_down = _mm(sv["act"], dx2b, "tn", bf16, name="g_ff_down")
    dhf = _mm(dfg, w["ff_gate"], "nn", f32, name="d_hf_gate")
    dhf = _mm(dfu, w["ff_up"], "nn", f32, res=dhf, name="d_hf_up")
    g_gate = _mm(dfg, sv["hf"], "tn", bf16, name="g_ff_gate")
    g_up = _mm(dfu, sv["hf"], "tn", bf16, name="g_ff_up")
    zero = reduce_begin("ffn", dict(w_ff_gate=g_gate, w_ff_up=g_up, w_ff_down=g_down))[0, 0]
    dx1, dx1b, small["g_ffn"] = _rms_bwd(sv["x1"], p["g_ffn"] + zero, dhf, dx2, name="rms_ffn_bwd")
    g_out = _mm(sv["merged"], dx1b, "tn", bf16, name="g_out")
    *dgl, dya, dyc, dys, db0, db1, db2 = _merge_bwd_fused(dx1b, w["out"], sv["gl"], p["b_gate"], sv["ya"], sv["yc"], sv["ys"],
                                                        name="merge_bwd")
    small["b_gate"] = jnp.concatenate([db0, db1, db2], axis=1)
    do = _mm(dya, w["attn_o"], "nn", bf16, after=reduce_continue("ffn", dya), name="d_o")
    g_ao = _mm(dya, sv["o"], "tn", bf16, name="g_attn_o")
    dcz = _mm(dyc, w["conv_o"], "nn", bf16, name="d_cz")
    g_co = _mm(dyc, sv["cz"], "tn", bf16, name="g_conv_o")
    dsz = _mm(dys, w["sg_o"], "nn", bf16, name="d_sz")
    g_so = _mm(dys, sv["sz"], "tn", bf16, name="g_sg_o")
    zero = reduce_begin("mix", dict(w_attn_o=g_ao, w_conv_o=g_co, w_sg_o=g_so, w_out=g_out))[0, 0]
    dsu, dsv, small["w_s"], small["b_s"], small["sg_ln_g"], small["sg_ln_b"] = _sgu_bwd(
        sv["sg_in"], dsz, p["sg_ln_g"] + zero, p["sg_ln_b"], p["w_s"], p["w_s_t"], p["b_s"], name="sgu_bwd")
    dc, small["conv_ln_g"], small["conv_ln_b"] = _conv2_bwd(sv["c"], dcz, p["conv_ln_g"], p["conv_ln_b"], name="conv2_bwd")
    da, dgt, small["w_dw"], small["b_dw"] = _conv1_bwd(sv["conv_in"], dc, w["dw"], name="conv1_bwd")
    zero = reduce_continue("mix", da)[0, 0]
    dqr, dkr, dv = _attn_bwd(sv["qr"], sv["kr"], sv["kv_raw"], do, name="attn_bwd")
    dq_raw, dk_raw, small["q_norm_g"], small["k_norm_g"] = _qk_bwd(
        sv["q_raw"], sv["kv_raw"], dqr, dkr, p["q_norm_g"] + zero, p["k_norm_g"], cos2, sin2, name="qk_bwd")
    dproj = jnp.concatenate([dq_raw, dk_raw, dv.astype(bf16), da, dgt, dsu, dsv, *dgl], axis=1)
    g_in = _mm(dproj, sv["h"], "tn", bf16, name="g_in")
    begun = reduce_begin("in", dict(w_in=g_in))
    if last:
        begun = reduce_continue("in", begun)
    dh = _mm(dproj, w["in"], "nn", f32, after=begun, name="d_h")
    zero = begun[0, 0] if last else reduce_continue("in", dh)[0, 0]
    dx, dxb, small["g_mix"] = _rms_bwd(sv["xin"], p["g_mix"] + zero, dh, dx1, name="rms_mix_bwd")
    return dx, dxb, small


SMALL = ("g_mix", "b_gate", "q_norm_g", "k_norm_g", "b_dw", "conv_ln_g", "conv_ln_b", "sg_ln_g", "sg_ln_b",
         "w_s", "b_s", "g_ffn")
PACK_ALIGN = 8 * LANES


def _pack(parts):
    flat = jnp.concatenate([a.reshape(-1).astype(f32) for a in parts])
    pad = -flat.shape[0] % PACK_ALIGN
    return jnp.pad(flat, (0, pad)).reshape(-1, LANES)


def _unpack(buf, shapes):
    flat = buf.reshape(-1)
    out, pos = [], 0
    for shp in shapes:
        size = math.prod(shp)
        out.append(flat[pos:pos + size].reshape(shp))
        pos += size
    return out


def kernel(x, g_mix, w_in, b_gate, q_norm_g, k_norm_g, w_attn_o, w_dw, b_dw, conv_ln_g, conv_ln_b, w_conv_o, sg_ln_g, sg_ln_b, w_s, b_s, w_sg_o, w_out, g_ffn, w_ff_gate, w_ff_up, w_ff_down, g_final, loss_target, m_g_mix, m_w_in, m_b_gate, m_q_norm_g, m_k_norm_g, m_w_attn_o, m_w_dw, m_b_dw, m_conv_ln_g, m_conv_ln_b, m_w_conv_o, m_sg_ln_g, m_sg_ln_b, m_w_s, m_b_s, m_w_sg_o, m_w_out, m_g_ffn, m_w_ff_gate, m_w_ff_up, m_w_ff_down, m_g_final, v_g_mix, v_w_in, v_b_gate, v_q_norm_g, v_k_norm_g, v_w_attn_o, v_w_dw, v_b_dw, v_conv_ln_g, v_conv_ln_b, v_w_conv_o, v_sg_ln_g, v_sg_ln_b, v_w_s, v_b_s, v_w_sg_o, v_w_out, v_g_ffn, v_w_ff_gate, v_w_ff_up, v_w_ff_down, v_g_final):
    weights = dict(g_mix=g_mix, w_in=w_in, b_gate=b_gate, q_norm_g=q_norm_g, k_norm_g=k_norm_g, w_attn_o=w_attn_o,
                   w_dw=w_dw, b_dw=b_dw, conv_ln_g=conv_ln_g, conv_ln_b=conv_ln_b, w_conv_o=w_conv_o, sg_ln_g=sg_ln_g,
                   sg_ln_b=sg_ln_b, w_s=w_s, b_s=b_s, w_sg_o=w_sg_o, w_out=w_out, g_ffn=g_ffn, w_ff_gate=w_ff_gate,
                   w_ff_up=w_ff_up, w_ff_down=w_ff_down, g_final=g_final)
    mom_m = dict(g_mix=m_g_mix, w_in=m_w_in, b_gate=m_b_gate, q_norm_g=m_q_norm_g, k_norm_g=m_k_norm_g,
                 w_attn_o=m_w_attn_o, w_dw=m_w_dw, b_dw=m_b_dw, conv_ln_g=m_conv_ln_g, conv_ln_b=m_conv_ln_b,
                 w_conv_o=m_w_conv_o, sg_ln_g=m_sg_ln_g, sg_ln_b=m_sg_ln_b, w_s=m_w_s, b_s=m_b_s, w_sg_o=m_w_sg_o,
                 w_out=m_w_out, g_ffn=m_g_ffn, w_ff_gate=m_w_ff_gate, w_ff_up=m_w_ff_up, w_ff_down=m_w_ff_down,
                 g_final=m_g_final)
    mom_v = dict(g_mix=v_g_mix, w_in=v_w_in, b_gate=v_b_gate, q_norm_g=v_q_norm_g, k_norm_g=v_k_norm_g,
                 w_attn_o=v_w_attn_o, w_dw=v_w_dw, b_dw=v_b_dw, conv_ln_g=v_conv_ln_g, conv_ln_b=v_conv_ln_b,
                 w_conv_o=v_w_conv_o, sg_ln_g=v_sg_ln_g, sg_ln_b=v_sg_ln_b, w_s=v_w_s, b_s=v_b_s, w_sg_o=v_w_sg_o,
                 w_out=v_w_out, g_ffn=v_g_ffn, w_ff_gate=v_w_ff_gate, w_ff_up=v_w_ff_up, w_ff_down=v_w_ff_down,
                 g_final=v_g_final)
    S, D = x.shape[1], x.shape[2]
    xi, yi, ci = _mesh_pos()
    me = 4 * xi + 2 * yi + ci
    core = jnp.reshape(ci, (1,)).astype(jnp.int32)
    cos2, sin2 = _rope_tables(S)

    big = ("w_in", "w_attn_o", "w_conv_o", "w_sg_o", "w_out", "w_ff_gate", "w_ff_up", "w_ff_down")
    transposed = {"w_in", "w_attn_o", "w_conv_o", "w_sg_o", "w_ff_gate", "w_ff_up"}
    chip = jnp.reshape(2 * xi + yi, (1,)).astype(jnp.int32)
    groups = (("in", "dw"), ("attn_o", "conv_o", "sg_o", "out"), ("ff_gate", "ff_up", "ff_down"))
    P, shards = [], []
    for l in range(DEPTH):
        sh = {n[2:]: (weights[n][l].T if n in transposed else weights[n][l]).astype(bf16) for n in big}
        sh["dw"] = jnp.pad(w_dw[l].reshape(CONV_W, LANES), ((0, CONV_WP - CONV_W), (0, 0)))
        shards.append(sh)
        p = {n: weights[n][l].reshape(1, -1) for n in SMALL if n not in ("w_s", "b_s")}
        p["w_s"] = w_s[l]
        p["w_s_t"] = jnp.swapaxes(w_s[l], 1, 2)
        p["b_s"] = b_s[l].reshape(SG_G, SG_CHUNK, 1)
        P.append(p)

    gathers = {}

    def start_gather(l, gi, after):
        srcs = [shards[l][n] for n in groups[gi]]
        lands = [lax.dynamic_update_index_in_dim(lax.empty((N_DEV,) + s.shape, s.dtype), s, me, 0) for s in srcs]
        gathers[l, gi] = _ici_start("gather", srcs, lands, after, name=f"ag_start_{l}{gi}")
        return gathers[l, gi][-1]

    def gathered(l, gi, after):
        srcs, lands = _ici_wait("gather", gathers[l, gi], after, name=f"ag_wait_{l}{gi}")
        after = srcs[0]
        if gi == len(groups) - 1 and l + 1 < DEPTH:
            for gj in range(len(groups)):
                after = start_gather(l + 1, gj, after)
        full = _d2d_gather([b.reshape(N_CHIP, 2, *b.shape[1:]) for b in lands], after, name=f"ag_d2d_{gi}")
        return {n: f.reshape(-1, f.shape[3]) for n, f in zip(groups[gi], full)}

    all_started = cos2
    for gi in range(len(groups)):
        all_started = start_gather(0, gi, all_started)

    h = x.reshape(S, D)
    saved, W = [], []
    for l in range(DEPTH):
        first = gathered(l, 0, all_started if l == 0 else h)
        if l == 0:
            P[l]["g_mix"] = P[l]["g_mix"] + all_started[0, 0]
        h, sv, w = _layer_fwd(h, P[l], first, functools.partial(lambda gi, z, l: gathered(l, gi, z), l=l), cos2, sin2)
        saved.append(sv)
        W.append(w)
    dx, dxb, sq, g_final_part = _final_loss(h, g_final.reshape(1, D), loss_target.reshape(S, D), name="final_loss")
    loss = lax.psum(0.5 * jnp.sum(sq) / D, ("x", "y", "c"))

    begun, reductions, small_grads = {}, {}, [None] * DEPTH
    for l in reversed(range(DEPTH)):
        def reduce_begin(group, grads, l=l):
            begun[l, group] = (tuple(grads), _reduce_begin(list(grads.values()), core, tag=f"{group}{l}"))
            return begun[l, group][1][-1]

        def reduce_continue(group, after, l=l):
            names, started = begun[l, group]
            reductions[l, group] = (names, _reduce_continue(started, core, after, tag=f"{group}{l}"))
            return reductions[l, group][1][-1]

        dx, dxb, small_grads[l] = _layer_bwd(dx, dxb, saved[l], P[l], W[l], cos2, sin2, reduce_begin, reduce_continue,
                                            last=(l == 0))
    grad_x = dx.reshape(x.shape)

    grads_out, delta, new_m, new_v = {}, {}, {}, {}
    swap = lambda a: jnp.swapaxes(a, 1, 2)

    def update(n, lands, sums):
        as_arrives = n not in transposed or weights[n].shape[2] % LANES != 0
        if as_arrives:
            to_arrival = swap if n in transposed else (lambda a: a)
            out = None
            for l in reversed(range(DEPTH)):
                out = _adamw_reduced(l, to_arrival(weights[n]), to_arrival(mom_m[n]), to_arrival(mom_v[n]),
                                     lands[l], sums[l], chip, out, name=f"adamw_{n}_{l}")
            grads_out[n], delta[n], new_m[n], new_v[n] = [to_arrival(o) for o in out]
        else:
            g = jnp.stack([_sum_chip_slots(lands[l], sums[l], chip, name="rs_sum_" + n) for l in range(DEPTH)])
            grads_out[n] = swap(g)
            delta[n], new_m[n], new_v[n] = _adamw(weights[n], grads_out[n], mom_m[n], mom_v[n], name="adamw_" + n)
        return delta[n]

    after = dx
    for group in ("ffn", "mix", "in"):
        names = reductions[0, group][0]
        arrived = [_ici_wait("reduce", reductions[l, group][1], after, name=f"rs_wait_{group}{l}") for l in range(DEPTH)]
        for i, n in enumerate(names):
            after = update(n, [arrived[l][1][i] for l in range(DEPTH)], [arrived[l][0][i] for l in range(DEPTH)])

    small_shapes = [weights[n].shape for n in SMALL] + [g_final.shape, (DEPTH, CONV_CH // LANES, CONV_WP, LANES)]
    parts = [jnp.stack([small_grads[l][n].reshape(weights[n].shape[1:]) for l in range(DEPTH)]) for n in SMALL]
    parts += [g_final_part.reshape(g_final.shape), jnp.stack([small_grads[l]["w_dw"] for l in range(DEPTH)])]
    packed = _pack(parts)
    gathered = _all_gather([packed], after, name="gather_small")[0]
    total = _sum_slots(gathered, name="sum_small")
    small_total = _unpack(total, small_shapes)
    grads_out.update(zip(SMALL + ("g_final",), small_total[:-1]))
    dw_full = small_total[-1]
    grads_out["w_dw"] = lax.dynamic_index_in_dim(dw_full, me, axis=1, keepdims=False)[:, :CONV_W].reshape(w_dw.shape)

    rep = tuple(n for n in SMALL if n != "w_s") + ("g_final",)
    rep_shapes = [weights[n].shape for n in rep]
    packs = [_pack([src[n] for n in rep])[None] for src in (weights, grads_out, mom_m, mom_v)]
    for dst, buf in zip((delta, new_m, new_v), _adamw(*packs, name="adamw_small")):
        dst.update(zip(rep, _unpack(buf[0], rep_shapes)))
    for n, shp in (("w_dw", (1, DEPTH * CONV_W, LANES)), ("w_s", (DEPTH, SG_G * SG_CHUNK, SG_CHUNK))):
        upd = _adamw(*[src[n].reshape(shp) for src in (weights, grads_out, mom_m, mom_v)], name="adamw_" + n)
        for dst, buf in zip((delta, new_m, new_v), upd):
            dst[n] = buf.reshape(weights[n].shape)

    order = ("g_mix", "w_in", "b_gate", "q_norm_g", "k_norm_g", "w_attn_o", "w_dw", "b_dw", "conv_ln_g", "conv_ln_b",
             "w_conv_o", "sg_ln_g", "sg_ln_b", "w_s", "b_s", "w_sg_o", "w_out", "g_ffn", "w_ff_gate", "w_ff_up",
             "w_ff_down", "g_final")
    return (loss, grad_x, *[grads_out[n] for n in order], *[delta[n] for n in order],
            *[new_m[n] for n in order], *[new_v[n] for n in order])
```

```python
import functools
import math

import jax
import jax.numpy as jnp
from jax import lax
from jax.experimental import pallas as pl
from jax.experimental.pallas import tpu as pltpu

f32, bf16 = jnp.float32, jnp.bfloat16

D_MODEL = 2048
SEQ = 2048
DEPTH = 2
GRID_W = 64
HEAD_DIM = 128
LANES = 128
N_Q = (D_MODEL // 2) // HEAD_DIM
N_KV = N_Q // 4
GRP = N_Q // N_KV
Q_COLS = N_Q * HEAD_DIM
KV_COLS = N_KV * HEAD_DIM
CONV_CH = D_MODEL // 2
CONV_W = 31
CONV_PAD = CONV_W // 2
CONV_WP = 32
SG_CH = D_MODEL // 2
SG_G = SG_CH // LANES
SG_CHUNK = 128
D_FF = -(-8 * D_MODEL // (3 * 256)) * 256
OFF_KV = Q_COLS
OFF_CONV = OFF_KV + 2 * KV_COLS
OFF_SG = OFF_CONV + 2 * CONV_CH
OFF_GATE = OFF_SG + 2 * SG_CH
IN_COLS = OFF_GATE + 3 * D_MODEL
ROPE_THETA = 10000.0
SCALE = HEAD_DIM ** -0.5
N_DEV = 8
N_CHIP = 4

ADAM_LR, ADAM_B1, ADAM_B2, ADAM_EPS, ADAM_WD, ADAM_STEP = 0.001, 0.9, 0.999, 1e-08, 0.01, 10

VMEM_BYTES_V7X = 64 << 20
VMEM_CAP = VMEM_BYTES_V7X - (6 << 20)
MESH = pl.DeviceIdType.MESH
HBM = pl.BlockSpec(memory_space=pltpu.HBM)


def _in_hbm(a):
    if isinstance(a, jax.Array) and jnp.issubdtype(a.dtype, jnp.floating) and a.size * a.dtype.itemsize >= (1 << 20):
        return pltpu.with_memory_space_constraint(a, pltpu.HBM)
    return a


def _out_hbm(s):
    if isinstance(s, jax.ShapeDtypeStruct) and math.prod(s.shape) * jnp.dtype(s.dtype).itemsize >= (1 << 20):
        return pltpu.HBM(s.shape, s.dtype)
    return s


def _call(body, **kw):
    shapes = kw.pop("out_shape")
    shapes = type(shapes)(_out_hbm(s) for s in shapes) if isinstance(shapes, (list, tuple)) else _out_hbm(shapes)
    call = pl.pallas_call(body, out_shape=shapes, **kw)
    return lambda *args: call(*[_in_hbm(a) for a in args])


def _pick(n, cands):
    for c in cands:
        if n % c == 0:
            return c
    raise ValueError((n, cands))


def _params(sem, vmem_bytes):
    return pltpu.CompilerParams(dimension_semantics=sem, vmem_limit_bytes=int(min(max(vmem_bytes, 16 << 20), VMEM_CAP)))


def _mm(a, b, form, out_dtype, *, n=None, b_off=0, res=None, after=None, name):
    if form == "tn":
        K, M = a.shape
    else:
        M, K = a.shape
    N = n if n is not None else (b.shape[0] if form == "nt" else b.shape[1])
    if K <= 2048:
        tk = K
        if form == "tn":
            tm = _pick(M, (512, 256, 128))
            tn = N if N <= 2048 else _pick(N, (1024, 512, 256, 128))
        else:
            tm = M if M <= 2048 else _pick(M, (2048, 1024, 512))
            tn = _pick(math.gcd(N, b_off) if b_off else N, (256, 128) if res is not None else (512, 256, 128))
    else:
        tk = max(t for t in range(LANES, 3072 + 1, LANES) if K % t == 0)
        tm = _pick(M, (1024, 512, 256, 128))
        tn = _pick(math.gcd(N, b_off) if b_off else N, (1024, 512, 256, 128))
    assert b_off % tn == 0
    off = b_off // tn
    nk = K // tk
    if form == "tn":
        a_spec = pl.BlockSpec((tk, tm), lambda i, j, k: (k, i))
    else:
        a_spec = pl.BlockSpec((tm, tk), lambda i, j, k: (i, k))
    if form == "nt":
        b_spec = pl.BlockSpec((tn, tk), lambda i, j, k: (j + off, k))
    else:
        b_spec = pl.BlockSpec((tk, tn), lambda i, j, k: (k, j + off))
    dims = {"nn": ((1,), (0,)), "nt": ((1,), (1,)), "tn": ((0,), (0,))}[form]
    has_res = res is not None

    def body(*refs):
        if after is not None:
            refs = refs[1:]
        if has_res:
            a_ref, b_ref, r_ref, o_ref = refs[:4]
        else:
            a_ref, b_ref, o_ref = refs[:3]
        p = lax.dot_general(a_ref[...], b_ref[...], (dims, ((), ())), preferred_element_type=f32)

        def finish(acc):
            if has_res:
                acc = acc + r_ref[...].astype(f32)
            o_ref[...] = acc.astype(o_ref.dtype)

        if nk == 1:
            finish(p)
        else:
            acc_ref = refs[-1]
            k = pl.program_id(2)

            @pl.when(k == 0)
            def _():
                acc_ref[...] = p

            @pl.when(k > 0)
            def _():
                acc_ref[...] += p

            @pl.when(k == nk - 1)
            def _():
                finish(acc_ref[...])

    in_specs = [a_spec, b_spec]
    args = [a, b]
    osz = jnp.dtype(out_dtype).itemsize
    vmem = 2 * (tm * tk * 2 + tk * tn * 2 + tm * tn * osz) + 2 * tm * tn * 4
    if has_res:
        in_specs.append(pl.BlockSpec((tm, tn), lambda i, j, k: (i, j)))
        args.append(res)
        vmem += 2 * tm * tn * res.dtype.itemsize
    scratch = []
    if nk > 1:
        scratch.append(pltpu.VMEM((tm, tn), f32))
        vmem += tm * tn * 4
    if after is not None:
        in_specs.insert(0, pl.BlockSpec(memory_space=pl.ANY))
        args.insert(0, after)
    return _call(
        body, name=name, grid=(M // tm, N // tn, nk),
        in_specs=in_specs, out_specs=pl.BlockSpec((tm, tn), lambda i, j, k: (i, j)),
        out_shape=jax.ShapeDtypeStruct((M, N), out_dtype), scratch_shapes=scratch,
        compiler_params=_params(("parallel", "parallel", "arbitrary"), vmem + (8 << 20)),
    )(*args)


EPI_TN = 256


def _mm_epi(a, bs, form, extras, out_dtypes, n_sums, fn, name):
    a_list = list(a) if isinstance(a, (list, tuple)) else [a]
    M, K = a_list[0].shape
    N = bs[0].shape[0] if form == "nt" else bs[0].shape[1]
    tn = EPI_TN
    assert K <= 2048 and N % tn == 0 and len(a_list) in (1, len(bs))
    dims = ((1,), (1,)) if form == "nt" else ((1,), (0,))
    na, nb, ne = len(a_list), len(bs), len(extras)

    def body(*refs):
        a_refs, b_refs = refs[:na], refs[na:na + nb]
        e_refs, o_refs = refs[na + nb:na + nb + ne], refs[na + nb + ne:]
        avs = [r[...] for r in a_refs] * (nb // na)
        ps = [lax.dot_general(av, b[...], (dims, ((), ())), preferred_element_type=f32) for av, b in zip(avs, b_refs)]
        for o_ref, o in zip(o_refs, fn(ps, [e[...] for e in e_refs])):
            o_ref[...] = o.astype(o_ref.dtype)

    in_specs = [pl.BlockSpec((M, K), lambda j: (0, 0), pipeline_mode=pl.Buffered(1)) for _ in a_list]
    in_specs += [pl.BlockSpec((tn, K), lambda j: (j, 0)) if form == "nt" else pl.BlockSpec((K, tn), lambda j: (0, j))
                 for _ in bs]
    for arr, first in extras:
        assert first % tn == 0
        in_specs.append(pl.BlockSpec((arr.shape[0], tn), functools.partial(lambda j, o: (0, j + o), o=first // tn)))
    out_specs = [pl.BlockSpec((M, tn), lambda j: (0, j))] * len(out_dtypes) + [pl.BlockSpec((1, tn), lambda j: (0, j))] * n_sums
    out_shape = [jax.ShapeDtypeStruct((M, N), dt) for dt in out_dtypes] + [jax.ShapeDtypeStruct((1, N), f32)] * n_sums
    tiles = sum(arr.shape[0] * tn * arr.dtype.itemsize for arr, _ in extras) + sum(M * tn * jnp.dtype(dt).itemsize for dt in out_dtypes)
    vmem = na * M * K * 2 + 2 * nb * tn * K * 2 + 2 * tiles + (nb + 6) * M * tn * 4
    return _call(body, name=name, grid=(N // tn,), in_specs=in_specs, out_specs=out_specs, out_shape=out_shape,
                 compiler_params=_params(("parallel",), vmem + (8 << 20)))(*a_list, *bs, *[arr for arr, _ in extras])


def _ffn_up(hf, wt_gate, wt_up, name):
    def fn(ps, _):
        g, u = ps[0].astype(bf16), ps[1].astype(bf16)
        gf = g.astype(f32)
        return g, u, gf * jax.nn.sigmoid(gf) * u.astype(f32)

    return _mm_epi(hf, [wt_gate, wt_up], "nt", [], [bf16] * 3, 0, fn, name)


def _ffn_down_bwd(dx2b, w_down, fg, fu, name):
    def fn(ps, es):
        d, g = ps[0], es[0].astype(f32)
        sg = jax.nn.sigmoid(g)
        return d * es[1].astype(f32) * sg * (1.0 + g * (1.0 - sg)), d * g * sg

    return _mm_epi(dx2b, [w_down], "nt", [(fg, 0), (fu, 0)], [bf16] * 2, 0, fn, name)


def _mixer_out(branches, wts, gl, b_gate, name):
    D = wts[0].shape[0]

    def fn(ps, es):
        ys = [p_.astype(bf16) for p_ in ps]
        merged = None
        for i in range(3):
            term = jax.nn.sigmoid(es[i].astype(f32) + es[3 + i]) * ys[i].astype(f32)
            merged = term if merged is None else merged + term
        return ys + [merged]

    extras = [(gl, i * D) for i in range(3)] + [(b_gate, i * D) for i in range(3)]
    return _mm_epi(branches, wts, "nt", extras, [bf16] * 4, 0, fn, name)


def _merge_bwd_fused(dx1b, w_out, gl, b_gate, ya, yc, ys, name):
    D = ya.shape[1]

    def fn(ps, es):
        dm_, outs, sums = ps[0], [], []
        for i in range(3):
            gate = jax.nn.sigmoid(es[i].astype(f32) + es[3 + i])
            dlog = dm_ * es[6 + i].astype(f32) * gate * (1.0 - gate)
            outs.append((dlog, dm_ * gate))
            sums.append(jnp.sum(dlog, axis=0, keepdims=True))
        return [o[0] for o in outs] + [o[1] for o in outs] + sums

    extras = [(gl, i * D) for i in range(3)] + [(b_gate, i * D) for i in range(3)] + [(ya, 0), (yc, 0), (ys, 0)]
    return _mm_epi(dx1b, [w_out], "nt", extras, [bf16] * 6, 3, fn, name)


def _rows(body, ins, outs, *, tm, name, vmem=40 << 20):
    nrows = next(s[1].shape[0] for s in ins if s[0] == "r")
    in_specs, args = [], []
    for s in ins:
        arr = s[1]
        if s[0] == "r":
            w = s[2] if len(s) > 2 else arr.shape[1]
            cb = s[3] if len(s) > 3 else 0
            in_specs.append(pl.BlockSpec((tm, w), functools.partial(lambda i, cb: (i, cb), cb=cb)))
        else:
            in_specs.append(pl.BlockSpec(arr.shape, functools.partial(lambda i, nd: (0,) * nd, nd=arr.ndim)))
        args.append(arr)
    out_specs, out_shape = [], []
    for s in outs:
        if s[0] == "r":
            out_specs.append(pl.BlockSpec((tm, s[1]), lambda i: (i, 0)))
            out_shape.append(jax.ShapeDtypeStruct((nrows, s[1]), s[2]))
        else:
            out_specs.append(pl.BlockSpec(s[1], functools.partial(lambda i, nd: (0,) * nd, nd=len(s[1]))))
            out_shape.append(jax.ShapeDtypeStruct(s[1], s[2]))
    return _call(body, name=name, grid=(nrows // tm,), in_specs=in_specs, out_specs=out_specs,
                 out_shape=out_shape, compiler_params=_params(("arbitrary",), vmem))(*args)


def _accumulate(ref, part):
    i = pl.program_id(0)

    @pl.when(i == 0)
    def _():
        ref[...] = part

    @pl.when(i > 0)
    def _():
        ref[...] += part


def _rms_stats(x):
    r = lax.rsqrt(jnp.mean(x * x, axis=-1, keepdims=True) + 1e-6)
    return r, x * r


def _rms_fwd(x, g, name):
    def body(x_ref, g_ref, o_ref):
        _, xn = _rms_stats(x_ref[...])
        o_ref[...] = (xn * g_ref[...]).astype(o_ref.dtype)

    return _rows(body, [("r", x), ("f", g)], [("r", x.shape[1], bf16)], tm=min(256, x.shape[0]), name=name)[0]


def _rms_bwd(x, g, dh, dres, name):
    D = x.shape[1]

    def body(x_ref, g_ref, dh_ref, dr_ref, dx_ref, dxb_ref, dg_ref):
        r, xn = _rms_stats(x_ref[...])
        dy = dh_ref[...].astype(f32)
        dxn = dy * g_ref[...]
        dx = dr_ref[...] + r * (dxn - xn * jnp.mean(dxn * xn, axis=-1, keepdims=True))
        dx_ref[...] = dx
        dxb_ref[...] = dx.astype(bf16)
        _accumulate(dg_ref, jnp.sum(dy * xn, axis=0, keepdims=True))

    return _rows(body, [("r", x), ("f", g), ("r", dh), ("r", dres)],
                 [("r", D, f32), ("r", D, bf16), ("a", (1, D), f32)], tm=min(256, x.shape[0]), name=name)


def _final_loss(x, g, tgt, name):
    D = x.shape[1]

    def body(x_ref, g_ref, t_ref, dx_ref, dxb_ref, sq_ref, dg_ref):
        r, xn = _rms_stats(x_ref[...])
        gain = g_ref[...]
        diff = xn * gain - t_ref[...]
        dy = diff * (1.0 / D)
        dxn = dy * gain
        dx = r * (dxn - xn * jnp.mean(dxn * xn, axis=-1, keepdims=True))
        dx_ref[...] = dx
        dxb_ref[...] = dx.astype(bf16)
        _accumulate(sq_ref, jnp.sum(diff * diff, axis=0, keepdims=True))
        _accumulate(dg_ref, jnp.sum(dy * xn, axis=0, keepdims=True))

    return _rows(body, [("r", x), ("f", g), ("r", tgt)],
                 [("r", D, f32), ("r", D, bf16), ("a", (1, D), f32), ("a", (1, D), f32)],
                 tm=min(256, x.shape[0]), name=name)


def _qk_fwd(q_raw, kv_raw, qg, kg, cos2, sin2, name):
    def body(q_ref, k_ref, qg_ref, kg_ref, c_ref, s_ref, qo_ref, ko_ref):
        c, s = c_ref[...], s_ref[...]

        def head(src, gain, dst, h):
            cols = slice(h * HEAD_DIM, (h + 1) * HEAD_DIM)
            _, xn = _rms_stats(src[:, cols].astype(f32))
            y = xn * gain
            dst[:, cols] = (y * c + pltpu.roll(y, HEAD_DIM // 2, 1) * s).astype(dst.dtype)

        for h in range(N_Q):
            head(q_ref, qg_ref[...], qo_ref, h)
        for h in range(N_KV):
            head(k_ref, kg_ref[...], ko_ref, h)

    return _rows(body, [("r", q_raw), ("r", kv_raw, KV_COLS, 0), ("f", qg), ("f", kg), ("r", cos2), ("r", sin2)],
                 [("r", Q_COLS, bf16), ("r", KV_COLS, bf16)], tm=min(256, q_raw.shape[0]), name=name)


def _qk_bwd(q_raw, kv_raw, dqr, dkr, qg, kg, cos2, sin2, name):
    def body(q_ref, k_ref, dq_ref, dk_ref, qg_ref, kg_ref, c_ref, s_ref, dqo_ref, dko_ref, dqg_ref, dkg_ref):
        c, s = c_ref[...], s_ref[...]

        def head(src, dsrc, gain, dst, h):
            cols = slice(h * HEAD_DIM, (h + 1) * HEAD_DIM)
            r, xn = _rms_stats(src[:, cols].astype(f32))
            do = dsrc[:, cols].astype(f32)
            dy = do * c + pltpu.roll(do * s, HEAD_DIM // 2, 1)
            dxn = dy * gain
            dst[:, cols] = (r * (dxn - xn * jnp.mean(dxn * xn, axis=-1, keepdims=True))).astype(dst.dtype)
            return jnp.sum(dy * xn, axis=0, keepdims=True)

        dq_gain = head(q_ref, dq_ref, qg_ref[...], dqo_ref, 0)
        for h in range(1, N_Q):
            dq_gain = dq_gain + head(q_ref, dq_ref, qg_ref[...], dqo_ref, h)
        dk_gain = head(k_ref, dk_ref, kg_ref[...], dko_ref, 0)
        for h in range(1, N_KV):
            dk_gain = dk_gain + head(k_ref, dk_ref, kg_ref[...], dko_ref, h)
        _accumulate(dqg_ref, dq_gain)
        _accumulate(dkg_ref, dk_gain)

    return _rows(body, [("r", q_raw), ("r", kv_raw, KV_COLS, 0), ("r", dqr), ("r", dkr), ("f", qg), ("f", kg),
                        ("r", cos2), ("r", sin2)],
                 [("r", Q_COLS, bf16), ("r", KV_COLS, bf16), ("a", (1, HEAD_DIM), f32), ("a", (1, HEAD_DIM), f32)],
                 tm=min(256, q_raw.shape[0]), name=name)


def _softmax_rows(q, k):
    s = lax.dot_general(q, k, (((1,), (1,)), ((), ())), preferred_element_type=f32) * SCALE
    p = jnp.exp(s - jnp.max(s, axis=-1, keepdims=True))
    return p * (1.0 / jnp.sum(p, axis=-1, keepdims=True))


def _head_cols(g):
    return slice(g * HEAD_DIM, (g + 1) * HEAD_DIM)


def _attn_fwd(qr, kr, kv_raw, name):
    S = qr.shape[0]
    tq = min(256, S)

    def body(q_ref, k_ref, v_ref, o_ref):
        k, v = k_ref[...], v_ref[...]
        for g in range(GRP):
            p = _softmax_rows(q_ref[:, _head_cols(g)], k)
            o_ref[:, _head_cols(g)] = jnp.dot(p.astype(bf16), v, preferred_element_type=f32).astype(o_ref.dtype)

    return _call(
        body, name=name, grid=(N_KV, S // tq),
        in_specs=[pl.BlockSpec((tq, GRP * HEAD_DIM), lambda kv, i: (i, kv)),
                  pl.BlockSpec((S, HEAD_DIM), lambda kv, i: (0, kv)),
                  pl.BlockSpec((S, HEAD_DIM), lambda kv, i: (0, N_KV + kv))],
        out_specs=pl.BlockSpec((tq, GRP * HEAD_DIM), lambda kv, i: (i, kv)),
        out_shape=jax.ShapeDtypeStruct((S, Q_COLS), bf16),
        compiler_params=_params(("parallel", "arbitrary"), 4 * GRP * tq * S * 4 + (8 << 20)),
    )(qr, kr, kv_raw)


def _attn_bwd(qr, kr, kv_raw, do, name):
    S = qr.shape[0]
    tq = min(256, S)

    def body(q_ref, k_ref, v_ref, do_ref, dq_ref, dk_ref, dv_ref):
        first = pl.program_id(1) == 0
        k, v = k_ref[...], v_ref[...]
        dv_part = dk_part = None
        for g in range(GRP):
            q, do_ = q_ref[:, _head_cols(g)], do_ref[:, _head_cols(g)]
            p = _softmax_rows(q, k)
            dp = lax.dot_general(do_, v, (((1,), (1,)), ((), ())), preferred_element_type=f32)
            ds = (p * (dp - jnp.sum(dp * p, axis=-1, keepdims=True)) * SCALE).astype(bf16)
            dq_ref[:, _head_cols(g)] = jnp.dot(ds, k, preferred_element_type=f32).astype(dq_ref.dtype)
            dv_g = lax.dot_general(p.astype(bf16), do_, (((0,), (0,)), ((), ())), preferred_element_type=f32)
            dk_g = lax.dot_general(ds, q, (((0,), (0,)), ((), ())), preferred_element_type=f32)
            dv_part = dv_g if g == 0 else dv_part + dv_g
            dk_part = dk_g if g == 0 else dk_part + dk_g

        @pl.when(first)
        def _():
            dv_ref[...] = dv_part
            dk_ref[...] = dk_part

        @pl.when(jnp.logical_not(first))
        def _():
            dv_ref[...] += dv_part
            dk_ref[...] += dk_part

    qspec = pl.BlockSpec((tq, GRP * HEAD_DIM), lambda kv, i: (i, kv))
    return _call(
        body, name=name, grid=(N_KV, S // tq),
        in_specs=[qspec, pl.BlockSpec((S, HEAD_DIM), lambda kv, i: (0, kv)),
                  pl.BlockSpec((S, HEAD_DIM), lambda kv, i: (0, N_KV + kv)), qspec],
        out_specs=[qspec, pl.BlockSpec((S, HEAD_DIM), lambda kv, i: (0, kv)),
                   pl.BlockSpec((S, HEAD_DIM), lambda kv, i: (0, kv))],
        out_shape=[jax.ShapeDtypeStruct((S, Q_COLS), bf16), jax.ShapeDtypeStruct((S, KV_COLS), f32),
                   jax.ShapeDtypeStruct((S, KV_COLS), f32)],
        compiler_params=_params(("parallel", "arbitrary"), 6 * GRP * tq * S * 4 + (8 << 20)),
    )(qr, kr, kv_raw, do)


CONV_HALO = 16


def _fill_padded(pad_ref, val, S):
    pad_ref[pl.ds(0, CONV_HALO), :] = jnp.zeros((CONV_HALO, LANES), f32)
    pad_ref[pl.ds(CONV_HALO + S, CONV_HALO), :] = jnp.zeros((CONV_HALO, LANES), f32)
    pad_ref[pl.ds(CONV_HALO, S), :] = val


def _group_specs(S, n_groups, second_half):
    return pl.BlockSpec((S, LANES), functools.partial(lambda g, o: (0, g + o), o=n_groups if second_half else 0))


def _conv1_fwd(conv_in, wdw, b_dw, name):
    S = conv_in.shape[0]
    ng = CONV_CH // LANES
    R = min(256, S)

    def body(a_ref, g_ref, w_ref, b_ref, o_ref, pad_ref):
        z = a_ref[...].astype(f32) * jax.nn.sigmoid(g_ref[...].astype(f32))
        _fill_padded(pad_ref, z, S)
        for r in range(S // R):
            acc = jnp.zeros((R, LANES), f32) + b_ref[...]
            for j in range(CONV_W):
                acc = acc + w_ref[pl.ds(j, 1), :] * pad_ref[pl.ds(r * R + CONV_HALO - CONV_PAD + j, R), :]
            o_ref[pl.ds(r * R, R), :] = acc

    return _call(
        body, name=name, grid=(ng,),
        in_specs=[_group_specs(S, ng, False), _group_specs(S, ng, True),
                  pl.BlockSpec((CONV_WP, LANES), lambda g: (g, 0)), pl.BlockSpec((1, LANES), lambda g: (0, g))],
        out_specs=pl.BlockSpec((S, LANES), lambda g: (0, g)),
        out_shape=jax.ShapeDtypeStruct((S, CONV_CH), f32),
        scratch_shapes=[pltpu.VMEM((S + 2 * CONV_HALO, LANES), f32)],
        compiler_params=_params(("parallel",), 24 << 20),
    )(conv_in, conv_in, wdw, b_dw)


def _conv1_bwd(conv_in, dc, wdw, name):
    S = conv_in.shape[0]
    ng = CONV_CH // LANES
    R = min(256, S)

    def body(a_ref, g_ref, w_ref, dc_ref, da_ref, dg_ref, dw_ref, db_ref, padz_ref, padd_ref):
        a = a_ref[...].astype(f32)
        sg = jax.nn.sigmoid(g_ref[...].astype(f32))
        _fill_padded(padz_ref, a * sg, S)
        _fill_padded(padd_ref, dc_ref[...], S)
        for r in range(S // R):
            dz = jnp.zeros((R, LANES), f32)
            for j in range(CONV_W):
                dz = dz + w_ref[pl.ds(j, 1), :] * padd_ref[pl.ds(r * R + CONV_HALO + CONV_PAD - j, R), :]
            rows = pl.ds(r * R, R)
            ar, sr = a_ref[rows, :].astype(f32), jax.nn.sigmoid(g_ref[rows, :].astype(f32))
            da_ref[rows, :] = (dz * sr).astype(da_ref.dtype)
            dg_ref[rows, :] = (dz * ar * sr * (1.0 - sr)).astype(dg_ref.dtype)
        for j in range(CONV_W):
            tot = jnp.zeros((1, LANES), f32)
            for r in range(S // R):
                tot = tot + jnp.sum(dc_ref[pl.ds(r * R, R), :] * padz_ref[pl.ds(r * R + CONV_HALO - CONV_PAD + j, R), :],
                                    axis=0, keepdims=True)
            dw_ref[pl.ds(j, 1), :] = tot
        dw_ref[pl.ds(CONV_W, CONV_WP - CONV_W), :] = jnp.zeros((CONV_WP - CONV_W, LANES), f32)
        db_ref[...] = jnp.sum(dc_ref[...], axis=0, keepdims=True)

    return _call(
        body, name=name, grid=(ng,),
        in_specs=[_group_specs(S, ng, False), _group_specs(S, ng, True),
                  pl.BlockSpec((CONV_WP, LANES), lambda g: (g, 0)), pl.BlockSpec((S, LANES), lambda g: (0, g))],
        out_specs=[pl.BlockSpec((S, LANES), lambda g: (0, g)), pl.BlockSpec((S, LANES), lambda g: (0, g)),
                   pl.BlockSpec((CONV_WP, LANES), lambda g: (g, 0)), pl.BlockSpec((1, LANES), lambda g: (0, g))],
        out_shape=[jax.ShapeDtypeStruct((S, CONV_CH), bf16), jax.ShapeDtypeStruct((S, CONV_CH), bf16),
                   jax.ShapeDtypeStruct((ng * CONV_WP, LANES), f32), jax.ShapeDtypeStruct((1, CONV_CH), f32)],
        scratch_shapes=[pltpu.VMEM((S + 2 * CONV_HALO, LANES), f32), pltpu.VMEM((S + 2 * CONV_HALO, LANES), f32)],
        compiler_params=_params(("parallel",), 24 << 20),
    )(conv_in, conv_in, wdw, dc)


def _ln_stats(x, eps=1e-5):
    xc = x - jnp.mean(x, axis=-1, keepdims=True)
    r = lax.rsqrt(jnp.mean(xc * xc, axis=-1, keepdims=True) + eps)
    return r, xc * r


def _ln_bwd(r, xh, dxh):
    return r * (dxh - jnp.mean(dxh, axis=-1, keepdims=True) - xh * jnp.mean(dxh * xh, axis=-1, keepdims=True))


def _conv2_fwd(c, ln_g, ln_b, name):
    def body(c_ref, g_ref, b_ref, o_ref):
        _, xh = _ln_stats(c_ref[...])
        y = xh * g_ref[...] + b_ref[...]
        o_ref[...] = (y * jax.nn.sigmoid(y)).astype(o_ref.dtype)

    return _rows(body, [("r", c), ("f", ln_g), ("f", ln_b)], [("r", CONV_CH, bf16)], tm=min(256, c.shape[0]), name=name)[0]


def _conv2_bwd(c, dcz, ln_g, ln_b, name):
    def body(c_ref, d_ref, g_ref, b_ref, dc_ref, dg_ref, db_ref):
        r, xh = _ln_stats(c_ref[...])
        y = xh * g_ref[...] + b_ref[...]
        sg = jax.nn.sigmoid(y)
        dy = d_ref[...].astype(f32) * (sg * (1.0 + y * (1.0 - sg)))
        dc_ref[...] = _ln_bwd(r, xh, dy * g_ref[...])
        _accumulate(dg_ref, jnp.sum(dy * xh, axis=0, keepdims=True))
        _accumulate(db_ref, jnp.sum(dy, axis=0, keepdims=True))

    return _rows(body, [("r", c), ("r", dcz), ("f", ln_g), ("f", ln_b)],
                 [("r", CONV_CH, f32), ("a", (1, CONV_CH), f32), ("a", (1, CONV_CH), f32)],
                 tm=min(256, c.shape[0]), name=name)


GELU_K = math.sqrt(2.0 / math.pi)
GELU_C = 0.044715


def _gelu(x):
    return 0.5 * x * (1.0 + jnp.tanh(GELU_K * (x + GELU_C * x * x * x)))


def _gelu_and_grad(x):
    x2 = x * x
    th = jnp.tanh(GELU_K * (x + GELU_C * x2 * x))
    half = 0.5 * (1.0 + th)
    return x * half, half + 0.5 * x * (1.0 - th * th) * (GELU_K * (1.0 + 3.0 * GELU_C * x2))


def _chunk_rows(n):
    return pl.ds(pl.multiple_of(n * SG_CHUNK, SG_CHUNK), SG_CHUNK)


def _sgu_fwd(sg_in, ln_g, ln_b, w_s, b_s, name):
    S = sg_in.shape[0]

    def body(u_ref, v_ref, lg_ref, lb_ref, w_ref, b_ref, o_ref):
        wb = w_ref[...].astype(bf16)

        def chunk(n, carry):
            rows = _chunk_rows(n)
            gu = _gelu(u_ref[rows, :].astype(f32))
            _, xh = _ln_stats(_gelu(v_ref[rows, :].astype(f32)))
            vl = xh * lg_ref[...] + lb_ref[...]
            t = jnp.dot(wb, vl.astype(bf16), preferred_element_type=f32) + b_ref[...]
            o_ref[rows, :] = (gu * t).astype(o_ref.dtype)
            return carry

        lax.fori_loop(0, S // SG_CHUNK, chunk, 0, unroll=2)

    return _call(
        body, name=name, grid=(SG_G,),
        in_specs=[_group_specs(S, SG_G, False), _group_specs(S, SG_G, True),
                  pl.BlockSpec((1, LANES), lambda g: (0, g)), pl.BlockSpec((1, LANES), lambda g: (0, g)),
                  pl.BlockSpec((None, SG_CHUNK, SG_CHUNK), lambda g: (g, 0, 0)),
                  pl.BlockSpec((None, SG_CHUNK, 1), lambda g: (g, 0, 0))],
        out_specs=pl.BlockSpec((S, LANES), lambda g: (0, g)),
        out_shape=jax.ShapeDtypeStruct((S, SG_CH), bf16),
        compiler_params=_params(("parallel",), 24 << 20),
    )(sg_in, sg_in, ln_g, ln_b, w_s, b_s)


def _sgu_bwd(sg_in, dsz, ln_g, ln_b, w_s, w_s_t, b_s, name):
    S = sg_in.shape[0]

    def body(u_ref, v_ref, lg_ref, lb_ref, w_ref, wt_ref, b_ref, d_ref, du_ref, dv_ref, dw_ref, db_ref, dlg_ref, dlb_ref):
        wb = w_ref[...].astype(bf16)
        wtb = wt_ref[...].astype(bf16)

        def chunk(n, carry):
            dwa, dba, dlga, dlba = carry
            rows = _chunk_rows(n)
            u = u_ref[rows, :].astype(f32)
            v = v_ref[rows, :].astype(f32)
            gu, gu_grad = _gelu_and_grad(u)
            gv, gv_grad = _gelu_and_grad(v)
            r, xh = _ln_stats(gv)
            vlb = (xh * lg_ref[...] + lb_ref[...]).astype(bf16)
            t = jnp.dot(wb, vlb, preferred_element_type=f32) + b_ref[...]
            d = d_ref[rows, :].astype(f32)
            dt = d * gu
            dtb = dt.astype(bf16)
            dwa = dwa + lax.dot_general(dtb, vlb, (((1,), (1,)), ((), ())), preferred_element_type=f32)
            dba = dba + jnp.sum(dt, axis=1, keepdims=True)
            dvl = jnp.dot(wtb, dtb, preferred_element_type=f32)
            dlga = dlga + jnp.sum(dvl * xh, axis=0, keepdims=True)
            dlba = dlba + jnp.sum(dvl, axis=0, keepdims=True)
            dgv = _ln_bwd(r, xh, dvl * lg_ref[...])
            du_ref[rows, :] = (d * t * gu_grad).astype(du_ref.dtype)
            dv_ref[rows, :] = (dgv * gv_grad).astype(dv_ref.dtype)
            return dwa, dba, dlga, dlba

        init = (jnp.zeros((SG_CHUNK, SG_CHUNK), f32), jnp.zeros((SG_CHUNK, 1), f32),
                jnp.zeros((1, LANES), f32), jnp.zeros((1, LANES), f32))
        dwa, dba, dlga, dlba = lax.fori_loop(0, S // SG_CHUNK, chunk, init, unroll=2)
        dw_ref[...] = dwa
        db_ref[...] = dba
        dlg_ref[...] = dlga
        dlb_ref[...] = dlba

    wspec = pl.BlockSpec((None, SG_CHUNK, SG_CHUNK), lambda g: (g, 0, 0))
    bspec = pl.BlockSpec((None, SG_CHUNK, 1), lambda g: (g, 0, 0))
    lspec = pl.BlockSpec((1, LANES), lambda g: (0, g))
    cspec = pl.BlockSpec((S, LANES), lambda g: (0, g))
    return _call(
        body, name=name, grid=(SG_G,),
        in_specs=[_group_specs(S, SG_G, False), _group_specs(S, SG_G, True), lspec, lspec, wspec, wspec, bspec, cspec],
        out_specs=[cspec, cspec, wspec, bspec, lspec, lspec],
        out_shape=[jax.ShapeDtypeStruct((S, SG_CH), bf16), jax.ShapeDtypeStruct((S, SG_CH), bf16),
                   jax.ShapeDtypeStruct((SG_G, SG_CHUNK, SG_CHUNK), f32), jax.ShapeDtypeStruct((SG_G, SG_CHUNK, 1), f32),
                   jax.ShapeDtypeStruct((1, SG_CH), f32), jax.ShapeDtypeStruct((1, SG_CH), f32)],
        compiler_params=_params(("parallel",), 24 << 20),
    )(sg_in, sg_in, ln_g, ln_b, w_s, w_s_t, b_s, dsz)


def _row_tile(r, c, n_arrays, itemsize=4):
    fits = [tm for tm in range(16, r + 1, 16) if r % tm == 0 and 2 * n_arrays * tm * c * itemsize <= (24 << 20)]
    return fits[-1] if fits else r


def _sum_slots(slots, name):
    n, r, c = slots.shape
    tm = _row_tile(r, c, n + 2)

    def body(s_ref, o_ref):
        acc = s_ref[0].astype(f32)
        for k in range(1, n):
            acc = acc + s_ref[k].astype(f32)
        o_ref[...] = acc

    return _call(body, name=name, grid=(r // tm,),
                 in_specs=[pl.BlockSpec((n, tm, c), lambda i: (0, i, 0))],
                 out_specs=pl.BlockSpec((tm, c), lambda i: (i, 0)),
                 out_shape=jax.ShapeDtypeStruct((r, c), f32),
                 compiler_params=_params(("parallel",), 40 << 20))(slots)


def _add_sibling(g4, recv, core, name):
    _, _, r, c = g4.shape
    tm = _row_tile(r, c, 3, 2)

    def body(core_ref, g_ref, r_ref, o_ref):
        o_ref[...] = g_ref[...] + r_ref[...]

    grid_spec = pltpu.PrefetchScalarGridSpec(
        num_scalar_prefetch=1, grid=(N_CHIP, r // tm),
        in_specs=[pl.BlockSpec((None, None, tm, c), lambda k, i, core_ref: (k, core_ref[0], i, 0)),
                  pl.BlockSpec((None, tm, c), lambda k, i, core_ref: (k, i, 0))],
        out_specs=pl.BlockSpec((None, tm, c), lambda k, i, core_ref: (k, i, 0)))
    return _call(body, name=name, grid_spec=grid_spec, out_shape=jax.ShapeDtypeStruct((N_CHIP, r, c), bf16),
                 compiler_params=_params(("parallel", "parallel"), 40 << 20))(core, g4, recv)


def _adamw(w, g, m, v, name):
    L, r, c = w.shape
    tm = _row_tile(r, c, 7)
    c1 = 1.0 - ADAM_B1 ** ADAM_STEP
    c2 = 1.0 - ADAM_B2 ** ADAM_STEP

    def body(w_ref, g_ref, m_ref, v_ref, d_ref, mo_ref, vo_ref):
        g_ = g_ref[...]
        m_ = ADAM_B1 * m_ref[...] + (1.0 - ADAM_B1) * g_
        v_ = ADAM_B2 * v_ref[...] + (1.0 - ADAM_B2) * (g_ * g_)
        d_ref[...] = -ADAM_LR * ((m_ / c1) / (jnp.sqrt(v_ / c2) + ADAM_EPS) + ADAM_WD * w_ref[...])
        mo_ref[...] = m_
        vo_ref[...] = v_

    spec = pl.BlockSpec((None, tm, c), lambda l, i: (l, i, 0))
    shp = jax.ShapeDtypeStruct((L, r, c), f32)
    return _call(body, name=name, grid=(L, r // tm), in_specs=[spec] * 4, out_specs=[spec] * 3,
                 out_shape=[shp] * 3, compiler_params=_params(("parallel", "parallel"), 40 << 20))(w, g, m, v)


def _mesh_pos():
    return lax.axis_index("x"), lax.axis_index("y"), lax.axis_index("c")


def _all_gather(shards, after, name):
    n = len(shards)

    def body(*refs):
        x_refs, o_refs = refs[:n], refs[n + 1:2 * n + 1]
        send_sems, recv_sems, local_sems = refs[2 * n + 1:]
        x, y, c = _mesh_pos()
        me, sibling = (x, y, c), (x, y, 1 - c)
        chips = [(1 - x, y), (x, 1 - y), (1 - x, 1 - y)]

        def rows(k, px, py, pc):
            return o_refs[k].at[4 * px + 2 * py + pc]

        def copy(k, s, block, to, src=None):
            return pltpu.make_async_remote_copy(
                src_ref=rows(k, *block) if src is None else src, dst_ref=rows(k, *block),
                send_sem=send_sems.at[k, s], recv_sem=recv_sems.at[k, s], device_id=to, device_id_type=MESH)

        mine = [pltpu.make_async_copy(x_refs[k], rows(k, *me), local_sems.at[k]) for k in range(n)]
        for cp in mine:
            cp.start()
        first = [copy(k, 0, me, sibling, src=x_refs[k]) for k in range(n)]
        for j, chip in enumerate(chips):
            first += [copy(k, 1 + j, me, (*chip, c), src=x_refs[k]) for k in range(n)]
        for cp in first:
            cp.start()
        passed = []
        for j, chip in enumerate(chips):
            for k in range(n):
                copy(k, 1 + j, (*chip, c), me).wait_recv()
                fwd = copy(k, 4 + j, (*chip, c), sibling)
                fwd.start()
                passed.append(fwd)
        for k in range(n):
            copy(k, 0, sibling, me).wait_recv()
        for j, chip in enumerate(chips):
            for k in range(n):
                copy(k, 4 + j, (*chip, 1 - c), me).wait_recv()
        for cp in first + passed:
            cp.wait_send()
        for cp in mine:
            cp.wait()

    return _call(
        body, name=name, in_specs=[HBM] * n + [ANY], out_specs=[HBM] * n,
        out_shape=[jax.ShapeDtypeStruct((N_DEV,) + s.shape, s.dtype) for s in shards],
        scratch_shapes=[pltpu.SemaphoreType.DMA((n, 7)), pltpu.SemaphoreType.DMA((n, 7)), pltpu.SemaphoreType.DMA((n,))],
    )(*shards, after)


SEM =pl.BlockSpec(memory_space=pltpu.SEMAPHORE)
ANY = pl.BlockSpec(memory_space=pl.ANY)
EFFECT = pltpu.SideEffectType.DATAFLOW_SIDE_EFFECTING


def _other_chips(x, y):
    return [(1 - x, y), (x, 1 - y), (1 - x, 1 - y)]


TO_SIBLING = ("sibling", "halves")


def _peers(kind, x, y):
    return [(x, y)] if kind in TO_SIBLING else _other_chips(x, y)


def _ici_copy(kind, src_ref, land_ref, send_sem, recv_sem, sender, target, c, incoming=False):
    (sx, sy), (tx, ty) = sender, target
    if kind == "sibling":
        return pltpu.make_async_remote_copy(src_ref=src_ref.at[:, 1 - c], dst_ref=land_ref, send_sem=send_sem,
                                            recv_sem=recv_sem, device_id=(tx, ty, 1 - c), device_id_type=MESH)
    if kind == "halves":
        return pltpu.make_async_remote_copy(src_ref=land_ref.at[:, c], dst_ref=land_ref.at[:, 1 - c if incoming else c],
                                            send_sem=send_sem, recv_sem=recv_sem, device_id=(tx, ty, 1 - c),
                                            device_id_type=MESH)
    if kind == "gather":
        src, dst = src_ref, land_ref.at[4 * sx + 2 * sy + c]
    else:
        src, dst = src_ref.at[2 * tx + ty], land_ref.at[2 * sx + sy]
    return pltpu.make_async_remote_copy(src_ref=src, dst_ref=dst, send_sem=send_sem, recv_sem=recv_sem,
                                        device_id=(tx, ty, c), device_id_type=MESH)


def _ici_start(kind, srcs, lands, after, name):
    n = len(srcs)
    npeer = 1 if kind in TO_SIBLING else 3

    def body(*refs):
        src_refs, land_refs = refs[:n], refs[n:2 * n]
        send_sems, recv_sems = refs[2 * n + 1], refs[2 * n + 2]
        token = refs[-1]
        x, y, c = _mesh_pos()
        for j, chip in enumerate(_peers(kind, x, y)):
            for k in range(n):
                _ici_copy(kind, src_refs[k], land_refs[k], send_sems.at[npeer * k + j], recv_sems.at[npeer * k + j],(x, y), chip, c).start()
        token[...] = jnp.zeros_like(token)

    bufs = list(srcs) + list(lands)
    return _call(
        body, name=name,
        out_shape=(pltpu.SemaphoreType.DMA((npeer * n,)), pltpu.SemaphoreType.DMA((npeer * n,)),
                   *[pltpu.HBM(b.shape, b.dtype) for b in bufs], jax.ShapeDtypeStruct((8, LANES), f32)),
        in_specs=[HBM] * (2 * n) + [ANY], out_specs=(SEM, SEM, *[HBM] * (2 * n), pl.BlockSpec(memory_space=pltpu.VMEM)),
        input_output_aliases={i: 2 + i for i in range(2 * n)},
        compiler_params=pltpu.CompilerParams(has_side_effects=EFFECT),
    )(*[pltpu.with_memory_space_constraint(b, pltpu.HBM) for b in bufs], after)


def _ici_wait(kind, started, after, name):
    send_sems, recv_sems, *bufs = started[:-1]
    n = len(bufs) // 2
    npeer = 1 if kind in TO_SIBLING else 3

    def body(*refs):
        src_refs, land_refs = refs[:n], refs[n:2 * n]
        send_sems, recv_sems = refs[2 * n], refs[2 * n + 1]
        x, y, c = _mesh_pos()
        for j, chip in enumerate(_peers(kind, x, y)):
            for k in range(n):
                _ici_copy(kind, src_refs[k], land_refs[k], send_sems.at[npeer * k + j], recv_sems.at[npeer * k + j],(x, y), chip, c).wait_send()
                _ici_copy(kind, src_refs[k], land_refs[k], send_sems.at[npeer * k + j], recv_sems.at[npeer * k + j],chip, (x, y), c, incoming=True).wait_recv()

    out = _call(
        body, name=name, out_shape=[pltpu.HBM(b.shape, b.dtype) for b in bufs],
        in_specs=[HBM] * (2 * n) + [SEM, SEM, ANY], out_specs=[HBM] * (2 * n),
        input_output_aliases={i: i for i in range(2 * n)},
        compiler_params=pltpu.CompilerParams(has_side_effects=EFFECT),
    )(*bufs, send_sems, recv_sems, after)
    return out[:n], out[n:]


def _sum_chip_slots(lands, sums, chip, name):
    _, r, c = lands.shape
    tm = _row_tile(r, c, 10, 2)

    def body(chip_ref, l_ref, s_ref, o_ref):
        acc = None
        for k in range(N_CHIP):
            part = jnp.where(chip_ref[0] == k, s_ref[k], l_ref[k]).astype(f32)
            acc = part if acc is None else acc + part
        o_ref[...] = acc

    grid_spec = pltpu.PrefetchScalarGridSpec(
        num_scalar_prefetch=1, grid=(r // tm,),
        in_specs=[pl.BlockSpec((N_CHIP, tm, c), lambda i, chip_ref: (0, i, 0)),
                  pl.BlockSpec((N_CHIP, tm, c), lambda i, chip_ref: (0, i, 0))],
        out_specs=pl.BlockSpec((tm, c), lambda i, chip_ref: (i, 0)))
    return _call(body, name=name, grid_spec=grid_spec, out_shape=jax.ShapeDtypeStruct((r, c), f32),
                 compiler_params=_params(("parallel",), 40 << 20))(chip, lands, sums)


def _reduce_begin(grads, core, tag):
    g4s = [g.reshape(N_CHIP, 2, g.shape[0] // N_DEV, g.shape[1]) for g in grads]
    recvs = [lax.empty((N_CHIP,) + g.shape[2:], g.dtype) for g in g4s]
    return _ici_start("sibling", g4s, recvs, core, name="rs_d2d_start_" + tag)


def _reduce_continue(begun, core, after, tag):
    g4s, recvs = _ici_wait("sibling", begun, after, name="rs_d2d_wait_" + tag)
    sums = [_add_sibling(g4, rv, core, name="rs_add_" + tag) for g4, rv in zip(g4s, recvs)]
    lands = [lax.empty(s.shape, s.dtype) for s in sums]
    return _ici_start("reduce", sums, lands, core, name="rs_start_" + tag)


def _adamw_reduced(layer, w, m, v, lands, sums, chip, prev, name):
    L, r, c = w.shape
    tm = _row_tile(r, c, 11)
    c1 = 1.0 - ADAM_B1 ** ADAM_STEP
    c2 = 1.0 - ADAM_B2 ** ADAM_STEP
    n_prev = 0 if prev is None else 4

    def body(chip_ref, w_ref, m_ref, v_ref, l_ref, s_ref, *refs):
        g_ref, d_ref, mo_ref, vo_ref = refs[n_prev:]
        g_ = None
        for k in range(N_CHIP):
            part = jnp.where(chip_ref[0] == k, s_ref[k], l_ref[k]).astype(f32)
            g_ = part if g_ is None else g_ + part
        m_ = ADAM_B1 * m_ref[...] + (1.0 - ADAM_B1) * g_
        v_ = ADAM_B2 * v_ref[...] + (1.0 - ADAM_B2) * (g_ * g_)
        g_ref[...] = g_
        d_ref[...] = -ADAM_LR * ((m_ / c1) / (jnp.sqrt(v_ / c2) + ADAM_EPS) + ADAM_WD * w_ref[...])
        mo_ref[...] = m_
        vo_ref[...] = v_

    wspec = pl.BlockSpec((None, tm, c), lambda i, chip_ref: (layer, i, 0))
    sspec = pl.BlockSpec((N_CHIP, tm, c), lambda i, chip_ref: (0, i, 0))
    grid_spec = pltpu.PrefetchScalarGridSpec(
        num_scalar_prefetch=1, grid=(r // tm,), in_specs=[wspec] * 3 + [sspec] * 2 + [ANY] * n_prev, out_specs=[wspec] * 4)
    return _call(body, name=name, grid_spec=grid_spec, out_shape=[jax.ShapeDtypeStruct((L, r, c), f32)] * 4,
                 input_output_aliases={6 + i: i for i in range(n_prev)},
                 compiler_params=_params(("parallel",), 40 << 20))(chip, w, m, v, lands, sums, *(prev or ()))


def _rope_tables(S):
    rows = S // GRID_W
    row = jnp.repeat(jnp.arange(rows, dtype=f32), GRID_W)
    col = jnp.tile(jnp.arange(GRID_W, dtype=f32), rows)
    nf = HEAD_DIM // 4
    inv = ROPE_THETA ** (-jnp.arange(nf, dtype=f32) / nf)
    ang = jnp.concatenate([row[:, None] * inv, col[:, None] * inv], axis=-1)
    cos, sin = jnp.cos(ang), jnp.sin(ang)
    return jnp.concatenate([cos, cos], axis=-1), jnp.concatenate([-sin, sin], axis=-1)


def _layer_fwd(xin, p, w, expect_weights, more_weights, cos2, sin2):
    sv = {"xin": xin}
    h = sv["h"] = _rms_fwd(xin, p["g_mix"], name="rms_mix")
    proj = functools.partial(_mm, h, w["in"], "nt", bf16)
    q_raw = sv["q_raw"] = proj(n=Q_COLS, b_off=0, name="proj_q")
    kv_raw = sv["kv_raw"] = proj(n=2 * KV_COLS, b_off=OFF_KV, name="proj_kv")
    conv_in = sv["conv_in"] = proj(n=2 * CONV_CH, b_off=OFF_CONV, name="proj_conv")
    sg_in = sv["sg_in"] = proj(n=2 * SG_CH, b_off=OFF_SG, name="proj_sg")
    gl = sv["gl"] = proj(n=3 * D_MODEL, b_off=OFF_GATE, name="proj_gate")
    zero = expect_weights(1, gl)[0, 0]
    qr, kr = sv["qr"], sv["kr"] = _qk_fwd(q_raw, kv_raw, p["q_norm_g"] + zero, p["k_norm_g"], cos2, sin2, name="qk_fwd")
    o = sv["o"] = _attn_fwd(qr, kr, kv_raw, name="attn_fwd")
    c = sv["c"] = _conv1_fwd(conv_in, w["dw"], p["b_dw"], name="conv1_fwd")
    cz = sv["cz"] = _conv2_fwd(c, p["conv_ln_g"], p["conv_ln_b"], name="conv2_fwd")
    sz = sv["sz"] = _sgu_fwd(sg_in, p["sg_ln_g"], p["sg_ln_b"], p["w_s"], p["b_s"], name="sgu_fwd")
    w = {**w, **more_weights(1, sz)}
    zero = expect_weights(2, w["out"])[0, 0]
    sv["ya"], sv["yc"], sv["ys"], merged = _mixer_out([o, cz, sz], [w["attn_o"], w["conv_o"], w["sg_o"]], gl,
                                                      p["b_gate"] + zero, name="mixer_out")
    sv["merged"] = merged
    x1 =sv["x1"] = _mm(merged, w["out"], "nn", f32, res=xin, name="out_proj")
    w = {**w, **more_weights(2, x1)}
    hf = sv["hf"] = _rms_fwd(x1, p["g_ffn"], name="rms_ffn")
    sv["fg"], sv["fu"], act = _ffn_up(hf, w["ff_gate"], w["ff_up"], name="ffn_up")
    sv["act"] = act
    x2 =_mm(act, w["ff_down"], "nn", f32, res=x1, name="ff_down")
    return x2, sv, w


def _layer_bwd(dx2, dx2b, sv, p, w, cos2, sin2, reduce_begin, reduce_continue, last):
    small = {}
    dfg, dfu = _ffn_down_bwd(dx2b, w["ff_down"], sv["fg"], sv["fu"], name="ffn_down_bwd")
    g_down = _mm(sv["act"], dx2b, "tn", bf16, name="g_ff_down")
    dhf = _mm(dfg, w["ff_gate"], "nn", f32, name="d_hf_gate")
    dhf = _mm(dfu, w["ff_up"], "nn", f32, res=dhf, name="d_hf_up")
    g_gate = _mm(dfg, sv["hf"], "tn", bf16, name="g_ff_gate")
    g_up = _mm(dfu, sv["hf"], "tn", bf16, name="g_ff_up")
    zero = reduce_begin("ffn", dict(w_ff_gate=g_gate, w_ff_up=g_up, w_ff_down=g_down))[0, 0]
    dx1, dx1b, small["g_ffn"] = _rms_bwd(sv["x1"], p["g_ffn"] + zero, dhf, dx2, name="rms_ffn_bwd")
    g_out = _mm(sv["merged"], dx1b, "tn", bf16, name="g_out")
    *dgl, dya, dyc, dys, db0, db1, db2 = _merge_bwd_fused(dx1b, w["out"], sv["gl"], p["b_gate"], sv["ya"], sv["yc"], sv["ys"],
                                                        name="merge_bwd")
    small["b_gate"] = jnp.concatenate([db0, db1, db2], axis=1)
    do = _mm(dya, w["attn_o"], "nn", bf16, after=reduce_continue("ffn", dya), name="d_o")
    g_ao = _mm(dya, sv["o"], "tn", bf16, name="g_attn_o")
    dcz = _mm(dyc, w["conv_o"], "nn", bf16, name="d_cz")
    g_co = _mm(dyc, sv["cz"], "tn", bf16, name="g_conv_o")
    dsz = _mm(dys, w["sg_o"], "nn", bf16, name="d_sz")
    g_so = _mm(dys, sv["sz"], "tn", bf16, name="g_sg_o")
    zero = reduce_begin("mix", dict(w_attn_o=g_ao, w_conv_o=g_co, w_sg_o=g_so, w_out=g_out))[0, 0]
    dsu, dsv, small["w_s"], small["b_s"], small["sg_ln_g"], small["sg_ln_b"] = _sgu_bwd(
        sv["sg_in"], dsz, p["sg_ln_g"] + zero, p["sg_ln_b"], p["w_s"], p["w_s_t"], p["b_s"], name="sgu_bwd")
    dc, small["conv_ln_g"], small["conv_ln_b"] = _conv2_bwd(sv["c"], dcz, p["conv_ln_g"], p["conv_ln_b"], name="conv2_bwd")
    da, dgt, small["w_dw"], small["b_dw"] = _conv1_bwd(sv["conv_in"], dc, w["dw"], name="conv1_bwd")
    zero = reduce_continue("mix", da)[0, 0]
    dqr, dkr, dv = _attn_bwd(sv["qr"], sv["kr"], sv["kv_raw"], do, name="attn_bwd")
    dq_raw, dk_raw, small["q_norm_g"], small["k_norm_g"] = _qk_bwd(
        sv["q_raw"], sv["kv_raw"], dqr, dkr, p["q_norm_g"] + zero, p["k_norm_g"], cos2, sin2, name="qk_bwd")
    dproj = jnp.concatenate([dq_raw, dk_raw, dv.astype(bf16), da, dgt, dsu, dsv, *dgl], axis=1)
    g_in = _mm(dproj, sv["h"], "tn", bf16, name="g_in")
    begun = reduce_begin("in", dict(w_in=g_in))
    if last:
        begun = reduce_continue("in", begun)
    dh = _mm(dproj, w["in"], "nn", f32, after=begun, name="d_h")
    zero = begun[0, 0] if last else reduce_continue("in", dh)[0, 0]
    dx, dxb, small["g_mix"] = _rms_bwd(sv["xin"], p["g_mix"] + zero, dh, dx1, name="rms_mix_bwd")
    return dx, dxb, small


SMALL = ("g_mix", "b_gate", "q_norm_g", "k_norm_g", "b_dw", "conv_ln_g", "conv_ln_b", "sg_ln_g", "sg_ln_b",
         "w_s", "b_s", "g_ffn")
PACK_ALIGN = 8 * LANES


def _pack(parts):
    flat = jnp.concatenate([a.reshape(-1).astype(f32) for a in parts])
    pad = -flat.shape[0] % PACK_ALIGN
    return jnp.pad(flat, (0, pad)).reshape(-1, LANES)


def _unpack(buf, shapes):
    flat = buf.reshape(-1)
    out, pos = [], 0
    for shp in shapes:
        size = math.prod(shp)
        out.append(flat[pos:pos + size].reshape(shp))
        pos += size
    return out


def kernel(x, g_mix, w_in, b_gate, q_norm_g, k_norm_g, w_attn_o, w_dw, b_dw, conv_ln_g, conv_ln_b, w_conv_o, sg_ln_g, sg_ln_b, w_s, b_s, w_sg_o, w_out, g_ffn, w_ff_gate, w_ff_up, w_ff_down, g_final, loss_target, m_g_mix, m_w_in, m_b_gate, m_q_norm_g, m_k_norm_g, m_w_attn_o, m_w_dw, m_b_dw, m_conv_ln_g, m_conv_ln_b, m_w_conv_o, m_sg_ln_g, m_sg_ln_b, m_w_s, m_b_s, m_w_sg_o, m_w_out, m_g_ffn, m_w_ff_gate, m_w_ff_up, m_w_ff_down, m_g_final, v_g_mix, v_w_in, v_b_gate, v_q_norm_g, v_k_norm_g, v_w_attn_o, v_w_dw, v_b_dw, v_conv_ln_g, v_conv_ln_b, v_w_conv_o, v_sg_ln_g, v_sg_ln_b, v_w_s, v_b_s, v_w_sg_o, v_w_out, v_g_ffn, v_w_ff_gate, v_w_ff_up, v_w_ff_down, v_g_final):
    weights = dict(g_mix=g_mix, w_in=w_in, b_gate=b_gate, q_norm_g=q_norm_g, k_norm_g=k_norm_g, w_attn_o=w_attn_o,
                   w_dw=w_dw, b_dw=b_dw, conv_ln_g=conv_ln_g, conv_ln_b=conv_ln_b, w_conv_o=w_conv_o, sg_ln_g=sg_ln_g,
                   sg_ln_b=sg_ln_b, w_s=w_s, b_s=b_s, w_sg_o=w_sg_o, w_out=w_out, g_ffn=g_ffn, w_ff_gate=w_ff_gate,
                   w_ff_up=w_ff_up, w_ff_down=w_ff_down, g_final=g_final)
    mom_m = dict(g_mix=m_g_mix, w_in=m_w_in, b_gate=m_b_gate, q_norm_g=m_q_norm_g, k_norm_g=m_k_norm_g,
                 w_attn_o=m_w_attn_o, w_dw=m_w_dw, b_dw=m_b_dw, conv_ln_g=m_conv_ln_g, conv_ln_b=m_conv_ln_b,
                 w_conv_o=m_w_conv_o, sg_ln_g=m_sg_ln_g, sg_ln_b=m_sg_ln_b, w_s=m_w_s, b_s=m_b_s, w_sg_o=m_w_sg_o,
                 w_out=m_w_out, g_ffn=m_g_ffn, w_ff_gate=m_w_ff_gate, w_ff_up=m_w_ff_up, w_ff_down=m_w_ff_down,
                 g_final=m_g_final)
    mom_v = dict(g_mix=v_g_mix, w_in=v_w_in, b_gate=v_b_gate, q_norm_g=v_q_norm_g, k_norm_g=v_k_norm_g,
                 w_attn_o=v_w_attn_o, w_dw=v_w_dw, b_dw=v_b_dw, conv_ln_g=v_conv_ln_g, conv_ln_b=v_conv_ln_b,
                 w_conv_o=v_w_conv_o, sg_ln_g=v_sg_ln_g, sg_ln_b=v_sg_ln_b, w_s=v_w_s, b_s=v_b_s, w_sg_o=v_w_sg_o,
                 w_out=v_w_out, g_ffn=v_g_ffn, w_ff_gate=v_w_ff_gate, w_ff_up=v_w_ff_up, w_ff_down=v_w_ff_down,
                 g_final=v_g_final)
    S, D = x.shape[1], x.shape[2]
    xi, yi, ci = _mesh_pos()
    me = 4 * xi + 2 * yi + ci
    core = jnp.reshape(ci, (1,)).astype(jnp.int32)
    cos2, sin2 = _rope_tables(S)

    big = ("w_in", "w_attn_o", "w_conv_o", "w_sg_o", "w_out", "w_ff_gate", "w_ff_up", "w_ff_down")
    transposed = {"w_in", "w_attn_o", "w_conv_o", "w_sg_o", "w_ff_gate", "w_ff_up"}
    chip = jnp.reshape(2 * xi + yi, (1,)).astype(jnp.int32)
    groups = (("in", "dw"), ("attn_o", "conv_o", "sg_o", "out"), ("ff_gate", "ff_up", "ff_down"))
    P, shards = [], []
    for l in range(DEPTH):
        sh = {n[2:]: (weights[n][l].T if n in transposed else weights[n][l]).astype(bf16) for n in big}
        sh["dw"] = jnp.pad(w_dw[l].reshape(CONV_W, LANES), ((0, CONV_WP - CONV_W), (0, 0)))
        shards.append(sh)
        p = {n: weights[n][l].reshape(1, -1) for n in SMALL if n not in ("w_s", "b_s")}
        p["w_s"] = w_s[l]
        p["w_s_t"] = jnp.swapaxes(w_s[l], 1, 2)
        p["b_s"] = b_s[l].reshape(SG_G, SG_CHUNK, 1)
        P.append(p)

    gathers = {}

    def start_gather(l, gi, after):
        srcs = [shards[l][n] for n in groups[gi]]
        lands = [lax.dynamic_update_index_in_dim(lax.empty((N_DEV,) + s.shape, s.dtype), s, me, 0) for s in srcs]
        gathers[l, gi] = _ici_start("gather", srcs, lands, after, name=f"ag_start_{l}{gi}")
        return gathers[l, gi][-1]

    halves = {}

    def arrive(l, gi, after):
        srcs, lands = _ici_wait("gather", gathers[l, gi], after, name=f"ag_wait_{l}{gi}")
        after = srcs[0]
        if gi == len(groups) - 1 and l + 1 < DEPTH:
            for gj in range(len(groups)):
                after = start_gather(l + 1, gj, after)
        lands4 = [b.reshape(N_CHIP, 2, *b.shape[1:]) for b in lands]
        halves[l, gi] = _ici_start("halves", srcs, lands4, after, name=f"ag_d2d_start_{l}{gi}")
        return halves[l, gi][-1]

    def gathered(l, gi, after):
        if (l, gi) not in halves:
            after = arrive(l, gi, after)
        _, full = _ici_wait("halves", halves[l, gi], after, name=f"ag_d2d_wait_{l}{gi}")
        return {n: f.reshape(-1, f.shape[3]) for n, f in zip(groups[gi], full)}

    all_started = cos2
    for gi in range(len(groups)):
        all_started = start_gather(0, gi, all_started)

    h = x.reshape(S, D)
    saved, W = [], []
    for l in range(DEPTH):
        first = gathered(l, 0, all_started if l == 0 else h)
        if l == 0:
            P[l]["g_mix"] = P[l]["g_mix"] + all_started[0, 0]
        h, sv, w = _layer_fwd(h, P[l], first, functools.partial(lambda gi, z, l: arrive(l, gi, z), l=l),
                              functools.partial(lambda gi, z, l: gathered(l, gi, z), l=l), cos2, sin2)
        saved.append(sv)
        W.append(w)
    dx, dxb, sq, g_final_part = _final_loss(h, g_final.reshape(1, D), loss_target.reshape(S, D), name="final_loss")
    loss = lax.psum(0.5 * jnp.sum(sq) / D, ("x", "y", "c"))

    begun, reductions, small_grads = {}, {}, [None] * DEPTH
    for l in reversed(range(DEPTH)):
        def reduce_begin(group, grads, l=l):
            begun[l, group] = (tuple(grads), _reduce_begin(list(grads.values()), core, tag=f"{group}{l}"))
            return begun[l, group][1][-1]

        def reduce_continue(group, after, l=l):
            names, started = begun[l, group]
            reductions[l, group] = (names, _reduce_continue(started, core, after, tag=f"{group}{l}"))
            return reductions[l, group][1][-1]

        dx, dxb, small_grads[l] = _layer_bwd(dx, dxb, saved[l], P[l], W[l], cos2, sin2, reduce_begin, reduce_continue,
                                            last=(l == 0))
    grad_x = dx.reshape(x.shape)

    grads_out, delta, new_m, new_v = {}, {}, {}, {}
    swap = lambda a: jnp.swapaxes(a, 1, 2)

    def update(n, lands, sums):
        as_arrives = n not in transposed or weights[n].shape[2] % LANES != 0
        if as_arrives:
            to_arrival = swap if n in transposed else (lambda a: a)
            out = None
            for l in reversed(range(DEPTH)):
                out = _adamw_reduced(l, to_arrival(weights[n]), to_arrival(mom_m[n]), to_arrival(mom_v[n]),
                                     lands[l], sums[l], chip, out, name=f"adamw_{n}_{l}")
            grads_out[n], delta[n], new_m[n], new_v[n] = [to_arrival(o) for o in out]
        else:
            g = jnp.stack([_sum_chip_slots(lands[l], sums[l], chip, name="rs_sum_" + n) for l in range(DEPTH)])
            grads_out[n] = swap(g)
            delta[n], new_m[n], new_v[n] = _adamw(weights[n], grads_out[n], mom_m[n], mom_v[n], name="adamw_" + n)
        return delta[n]

    after = dx
    for group in ("ffn", "mix", "in"):
        names = reductions[0, group][0]
        arrived = [_ici_wait("reduce", reductions[l, group][1], after, name=f"rs_wait_{group}{l}") for l in range(DEPTH)]
        for i, n in enumerate(names):
            after = update(n, [arrived[l][1][i] for l in range(DEPTH)], [arrived[l][0][i] for l in range(DEPTH)])

    small_shapes = [weights[n].shape for n in SMALL] + [g_final.shape, (DEPTH, CONV_CH // LANES, CONV_WP, LANES)]
    parts = [jnp.stack([small_grads[l][n].reshape(weights[n].shape[1:]) for l in range(DEPTH)]) for n in SMALL]
    parts += [g_final_part.reshape(g_final.shape), jnp.stack([small_grads[l]["w_dw"] for l in range(DEPTH)])]
    packed = _pack(parts)
    gathered = _all_gather([packed], after, name="gather_small")[0]
    total = _sum_slots(gathered, name="sum_small")
    small_total = _unpack(total, small_shapes)
    grads_out.update(zip(SMALL + ("g_final",), small_total[:-1]))
    dw_full = small_total[-1]
    grads_out["w_dw"] = lax.dynamic_index_in_dim(dw_full, me, axis=1, keepdims=False)[:, :CONV_W].reshape(w_dw.shape)

    rep = tuple(n for n in SMALL if n != "w_s") + ("g_final",)
    rep_shapes = [weights[n].shape for n in rep]
    packs = [_pack([src[n] for n in rep])[None] for src in (weights, grads_out, mom_m, mom_v)]
    for dst, buf in zip((delta, new_m, new_v), _adamw(*packs, name="adamw_small")):
        dst.update(zip(rep, _unpack(buf[0], rep_shapes)))
    for n, shp in (("w_dw", (1, DEPTH * CONV_W, LANES)), ("w_s", (DEPTH, SG_G * SG_CHUNK, SG_CHUNK))):
        upd = _adamw(*[src[n].reshape(shp) for src in (weights, grads_out, mom_m, mom_v)], name="adamw_" + n)
        for dst, buf in zip((delta, new_m, new_v), upd):
            dst[n] = buf.reshape(weights[n].shape)

    order = ("g_mix", "w_in", "b_gate", "q_norm_g", "k_norm_g", "w_attn_o", "w_dw", "b_dw", "conv_ln_g", "conv_ln_b",
             "w_conv_o", "sg_ln_g", "sg_ln_b", "w_s", "b_s", "w_sg_o", "w_out", "g_ffn", "w_ff_gate", "w_ff_up",
             "w_ff_down", "g_final")
    return (loss, grad_x, *[grads_out[n] for n in order], *[delta[n] for n in order],
            *[new_m[n] for n in order], *[new_v[n] for n in order])
```

```python
import functools
import math

import jax
import jax.numpy as jnp
from jax import lax
from jax.experimental import pallas as pl
from jax.experimental.pallas import tpu as pltpu

f32, bf16 = jnp.float32, jnp.bfloat16

D_MODEL = 2048
SEQ = 2048
DEPTH = 2
GRID_W = 64
HEAD_DIM = 128
LANES = 128
N_Q = (D_MODEL // 2) // HEAD_DIM
N_KV = N_Q // 4
GRP = N_Q // N_KV
Q_COLS = N_Q * HEAD_DIM
KV_COLS = N_KV * HEAD_DIM
CONV_CH = D_MODEL // 2
CONV_W = 31
CONV_PAD = CONV_W // 2
CONV_WP = 32
SG_CH = D_MODEL // 2
SG_G = SG_CH // LANES
SG_CHUNK = 128
D_FF = -(-8 * D_MODEL // (3 * 256)) * 256
OFF_KV = Q_COLS
OFF_CONV = OFF_KV + 2 * KV_COLS
OFF_SG = OFF_CONV + 2 * CONV_CH
OFF_GATE = OFF_SG + 2 * SG_CH
IN_COLS = OFF_GATE + 3 * D_MODEL
ROPE_THETA = 10000.0
SCALE = HEAD_DIM ** -0.5
N_DEV = 8
N_CHIP = 4

ADAM_LR, ADAM_B1, ADAM_B2, ADAM_EPS, ADAM_WD, ADAM_STEP = 0.001, 0.9, 0.999, 1e-08, 0.01, 10

VMEM_BYTES_V7X = 64 << 20
VMEM_CAP = VMEM_BYTES_V7X - (6 << 20)
MESH = pl.DeviceIdType.MESH
HBM = pl.BlockSpec(memory_space=pltpu.HBM)


def _in_hbm(a):
    if isinstance(a, jax.Array) and jnp.issubdtype(a.dtype, jnp.floating) and a.size * a.dtype.itemsize >= (1 << 20):
        return pltpu.with_memory_space_constraint(a, pltpu.HBM)
    return a


def _out_hbm(s):
    if isinstance(s, jax.ShapeDtypeStruct) and math.prod(s.shape) * jnp.dtype(s.dtype).itemsize >= (1 << 20):
        return pltpu.HBM(s.shape, s.dtype)
    return s


def _call(body, **kw):
    shapes = kw.pop("out_shape")
    shapes = type(shapes)(_out_hbm(s) for s in shapes) if isinstance(shapes, (list, tuple)) else _out_hbm(shapes)
    call = pl.pallas_call(body, out_shape=shapes, **kw)
    return lambda *args: call(*[_in_hbm(a) for a in args])


def _pick(n, cands):
    for c in cands:
        if n % c == 0:
            return c
    raise ValueError((n, cands))


def _params(sem, vmem_bytes):
    return pltpu.CompilerParams(dimension_semantics=sem, vmem_limit_bytes=int(min(max(vmem_bytes, 16 << 20), VMEM_CAP)))


def _mm(a, b, form, out_dtype, *, n=None, b_off=0, res=None, after=None, name):
    if form == "tn":
        K, M = a.shape
    else:
        M, K = a.shape
    N = n if n is not None else (b.shape[0] if form == "nt" else b.shape[1])
    if K <= 2048:
        tk = K
        if form == "tn":
            tm = _pick(M, (512, 256, 128))
            tn = N if N <= 2048 else _pick(N, (1024, 512, 256, 128))
        else:
            tm = M if M <= 2048 else _pick(M, (2048, 1024, 512))
            tn = _pick(math.gcd(N, b_off) if b_off else N, (256, 128) if res is not None else (512, 256, 128))
    else:
        tk = max(t for t in range(LANES, 3072 + 1, LANES) if K % t == 0)
        tm = _pick(M, (1024, 512, 256, 128))
        tn = _pick(math.gcd(N, b_off) if b_off else N, (1024, 512, 256, 128))
    assert b_off % tn == 0
    off = b_off // tn
    nk = K // tk
    if form == "tn":
        a_spec = pl.BlockSpec((tk, tm), lambda i, j, k: (k, i))
    else:
        a_spec = pl.BlockSpec((tm, tk), lambda i, j, k: (i, k))
    if form == "nt":
        b_spec = pl.BlockSpec((tn, tk), lambda i, j, k: (j + off, k))
    else:
        b_spec = pl.BlockSpec((tk, tn), lambda i, j, k: (k, j + off))
    dims = {"nn": ((1,), (0,)), "nt": ((1,), (1,)), "tn": ((0,), (0,))}[form]
    has_res = res is not None

    def body(*refs):
        if after is not None:
            refs = refs[1:]
        if has_res:
            a_ref, b_ref, r_ref, o_ref = refs[:4]
        else:
            a_ref, b_ref, o_ref = refs[:3]
        p = lax.dot_general(a_ref[...], b_ref[...], (dims, ((), ())), preferred_element_type=f32)

        def finish(acc):
            if has_res:
                acc = acc + r_ref[...].astype(f32)
            o_ref[...] = acc.astype(o_ref.dtype)

        if nk == 1:
            finish(p)
        else:
            acc_ref = refs[-1]
            k = pl.program_id(2)

            @pl.when(k == 0)
            def _():
                acc_ref[...] = p

            @pl.when(k > 0)
            def _():
                acc_ref[...] += p

            @pl.when(k == nk - 1)
            def _():
                finish(acc_ref[...])

    in_specs = [a_spec, b_spec]
    args = [a, b]
    osz = jnp.dtype(out_dtype).itemsize
    vmem = 2 * (tm * tk * 2 + tk * tn * 2 + tm * tn * osz) + 2 * tm * tn * 4
    if has_res:
        in_specs.append(pl.BlockSpec((tm, tn), lambda i, j, k: (i, j)))
        args.append(res)
        vmem += 2 * tm * tn * res.dtype.itemsize
    scratch = []
    if nk > 1:
        scratch.append(pltpu.VMEM((tm, tn), f32))
        vmem += tm * tn * 4
    if after is not None:
        in_specs.insert(0, pl.BlockSpec(memory_space=pl.ANY))
        args.insert(0, after)
    return _call(
        body, name=name, grid=(M // tm, N // tn, nk),
        in_specs=in_specs, out_specs=pl.BlockSpec((tm, tn), lambda i, j, k: (i, j)),
        out_shape=jax.ShapeDtypeStruct((M, N), out_dtype), scratch_shapes=scratch,
        compiler_params=_params(("parallel", "parallel", "arbitrary"), vmem + (8 << 20)),
    )(*args)


EPI_TN = 256


def _mm_epi(a, bs, form, extras, out_dtypes, n_sums, fn, name):
    a_list = list(a) if isinstance(a, (list, tuple)) else [a]
    M, K = a_list[0].shape
    N = bs[0].shape[0] if form == "nt" else bs[0].shape[1]
    tn = EPI_TN
    assert K <= 2048 and N % tn == 0 and len(a_list) in (1, len(bs))
    dims = ((1,), (1,)) if form == "nt" else ((1,), (0,))
    na, nb, ne = len(a_list), len(bs), len(extras)

    def body(*refs):
        a_refs, b_refs = refs[:na], refs[na:na + nb]
        e_refs, o_refs = refs[na + nb:na + nb + ne], refs[na + nb + ne:]
        avs = [r[...] for r in a_refs] * (nb // na)
        ps = [lax.dot_general(av, b[...], (dims, ((), ())), preferred_element_type=f32) for av, b in zip(avs, b_refs)]
        for o_ref, o in zip(o_refs, fn(ps, [e[...] for e in e_refs])):
            o_ref[...] = o.astype(o_ref.dtype)

    in_specs = [pl.BlockSpec((M, K), lambda j: (0, 0), pipeline_mode=pl.Buffered(1)) for _ in a_list]
    in_specs += [pl.BlockSpec((tn, K), lambda j: (j, 0)) if form == "nt" else pl.BlockSpec((K, tn), lambda j: (0, j))
                 for _ in bs]
    for arr, first in extras:
        assert first % tn == 0
        in_specs.append(pl.BlockSpec((arr.shape[0], tn), functools.partial(lambda j, o: (0, j + o), o=first // tn)))
    out_specs = [pl.BlockSpec((M, tn), lambda j: (0, j))] * len(out_dtypes) + [pl.BlockSpec((1, tn), lambda j: (0, j))] * n_sums
    out_shape = [jax.ShapeDtypeStruct((M, N), dt) for dt in out_dtypes] + [jax.ShapeDtypeStruct((1, N), f32)] * n_sums
    tiles = sum(arr.shape[0] * tn * arr.dtype.itemsize for arr, _ in extras) + sum(M * tn * jnp.dtype(dt).itemsize for dt in out_dtypes)
    vmem = na * M * K * 2 + 2 * nb * tn * K * 2 + 2 * tiles + (nb + 6) * M * tn * 4
    return _call(body, name=name, grid=(N // tn,), in_specs=in_specs, out_specs=out_specs, out_shape=out_shape,
                 compiler_params=_params(("parallel",), vmem + (8 << 20)))(*a_list, *bs, *[arr for arr, _ in extras])


def _ffn_up(hf, wt_gate, wt_up, name):
    def fn(ps, _):
        g, u = ps[0].astype(bf16), ps[1].astype(bf16)
        gf = g.astype(f32)
        return g, u, gf * jax.nn.sigmoid(gf) * u.astype(f32)

    return _mm_epi(hf, [wt_gate, wt_up], "nt", [], [bf16] * 3, 0, fn, name)


def _ffn_down_bwd(dx2b, w_down, fg, fu, name):
    def fn(ps, es):
        d, g = ps[0], es[0].astype(f32)
        sg = jax.nn.sigmoid(g)
        return d * es[1].astype(f32) * sg * (1.0 + g * (1.0 - sg)), d * g * sg

    return _mm_epi(dx2b, [w_down], "nt", [(fg, 0), (fu, 0)], [bf16] * 2, 0, fn, name)


def _mixer_out(branches, wts, gl, b_gate, name):
    D = wts[0].shape[0]

    def fn(ps, es):
        ys = [p_.astype(bf16) for p_ in ps]
        merged = None
        for i in range(3):
            term = jax.nn.sigmoid(es[i].astype(f32) + es[3 + i]) * ys[i].astype(f32)
            merged = term if merged is None else merged + term
        return ys + [merged]

    extras = [(gl, i * D) for i in range(3)] + [(b_gate, i * D) for i in range(3)]
    return _mm_epi(branches, wts, "nt", extras, [bf16] * 4, 0, fn, name)


def _merge_bwd_fused(dx1b, w_out, gl, b_gate, ya, yc, ys, name):
    D = ya.shape[1]

    def fn(ps, es):
        dm_, outs, sums = ps[0], [], []
        for i in range(3):
            gate = jax.nn.sigmoid(es[i].astype(f32) + es[3 + i])
            dlog = dm_ * es[6 + i].astype(f32) * gate * (1.0 - gate)
            outs.append((dlog, dm_ * gate))
            sums.append(jnp.sum(dlog, axis=0, keepdims=True))
        return [o[0] for o in outs] + [o[1] for o in outs] + sums

    extras = [(gl, i * D) for i in range(3)] + [(b_gate, i * D) for i in range(3)] + [(ya, 0), (yc, 0), (ys, 0)]
    return _mm_epi(dx1b, [w_out], "nt", extras, [bf16] * 6, 3, fn, name)


def _rows(body, ins, outs, *, tm, name, vmem=40 << 20):
    nrows = next(s[1].shape[0] for s in ins if s[0] == "r")
    in_specs, args = [], []
    for s in ins:
        arr = s[1]
        if s[0] == "r":
            w = s[2] if len(s) > 2 else arr.shape[1]
            cb = s[3] if len(s) > 3 else 0
            in_specs.append(pl.BlockSpec((tm, w), functools.partial(lambda i, cb: (i, cb), cb=cb)))
        else:
            in_specs.append(pl.BlockSpec(arr.shape, functools.partial(lambda i, nd: (0,) * nd, nd=arr.ndim)))
        args.append(arr)
    out_specs, out_shape = [], []
    for s in outs:
        if s[0] == "r":
            out_specs.append(pl.BlockSpec((tm, s[1]), lambda i: (i, 0)))
            out_shape.append(jax.ShapeDtypeStruct((nrows, s[1]), s[2]))
        else:
            out_specs.append(pl.BlockSpec(s[1], functools.partial(lambda i, nd: (0,) * nd, nd=len(s[1]))))
            out_shape.append(jax.ShapeDtypeStruct(s[1], s[2]))
    return _call(body, name=name, grid=(nrows // tm,), in_specs=in_specs, out_specs=out_specs,
                 out_shape=out_shape, compiler_params=_params(("arbitrary",), vmem))(*args)


def _accumulate(ref, part):
    i = pl.program_id(0)

    @pl.when(i == 0)
    def _():
        ref[...] = part

    @pl.when(i > 0)
    def _():
        ref[...] += part


def _rms_stats(x):
    r = lax.rsqrt(jnp.mean(x * x, axis=-1, keepdims=True) + 1e-6)
    return r, x * r


def _rms_fwd(x, g, name):
    def body(x_ref, g_ref, o_ref):
        _, xn = _rms_stats(x_ref[...])
        o_ref[...] = (xn * g_ref[...]).astype(o_ref.dtype)

    return _rows(body, [("r", x), ("f", g)], [("r", x.shape[1], bf16)], tm=min(256, x.shape[0]), name=name)[0]


def _rms_bwd(x, g, dh, dres, name):
    D = x.shape[1]

    def body(x_ref, g_ref, dh_ref, dr_ref, dx_ref, dxb_ref, dg_ref):
        r, xn = _rms_stats(x_ref[...])
        dy = dh_ref[...].astype(f32)
        dxn = dy * g_ref[...]
        dx = dr_ref[...] + r * (dxn - xn * jnp.mean(dxn * xn, axis=-1, keepdims=True))
        dx_ref[...] = dx
        dxb_ref[...] = dx.astype(bf16)
        _accumulate(dg_ref, jnp.sum(dy * xn, axis=0, keepdims=True))

    return _rows(body, [("r", x), ("f", g), ("r", dh), ("r", dres)],
                 [("r", D, f32), ("r", D, bf16), ("a", (1, D), f32)], tm=min(256, x.shape[0]), name=name)


def _final_loss(x, g, tgt, name):
    D = x.shape[1]

    def body(x_ref, g_ref, t_ref, dx_ref, dxb_ref, sq_ref, dg_ref):
        r, xn = _rms_stats(x_ref[...])
        gain = g_ref[...]
        diff = xn * gain - t_ref[...]
        dy = diff * (1.0 / D)
        dxn = dy * gain
        dx = r * (dxn - xn * jnp.mean(dxn * xn, axis=-1, keepdims=True))
        dx_ref[...] = dx
        dxb_ref[...] = dx.astype(bf16)
        _accumulate(sq_ref, jnp.sum(diff * diff, axis=0, keepdims=True))
        _accumulate(dg_ref, jnp.sum(dy * xn, axis=0, keepdims=True))

    return _rows(body, [("r", x), ("f", g), ("r", tgt)],
                 [("r", D, f32), ("r", D, bf16), ("a", (1, D), f32), ("a", (1, D), f32)],
                 tm=min(256, x.shape[0]), name=name)


def _qk_fwd(q_raw, kv_raw, qg, kg, cos2, sin2, name):
    def body(q_ref, k_ref, qg_ref, kg_ref, c_ref, s_ref, qo_ref, ko_ref):
        c, s = c_ref[...], s_ref[...]

        def head(src, gain, dst, h):
            cols = slice(h * HEAD_DIM, (h + 1) * HEAD_DIM)
            _, xn = _rms_stats(src[:, cols].astype(f32))
            y = xn * gain
            dst[:, cols] = (y * c + pltpu.roll(y, HEAD_DIM // 2, 1) * s).astype(dst.dtype)

        for h in range(N_Q):
            head(q_ref, qg_ref[...], qo_ref, h)
        for h in range(N_KV):
            head(k_ref, kg_ref[...], ko_ref, h)

    return _rows(body, [("r", q_raw), ("r", kv_raw, KV_COLS, 0), ("f", qg), ("f", kg), ("r", cos2), ("r", sin2)],
                 [("r", Q_COLS, bf16), ("r", KV_COLS, bf16)], tm=min(256, q_raw.shape[0]), name=name)


def _qk_bwd(q_raw, kv_raw, dqr, dkr, qg, kg, cos2, sin2, name):
    def body(q_ref, k_ref, dq_ref, dk_ref, qg_ref, kg_ref, c_ref, s_ref, dqo_ref, dko_ref, dqg_ref, dkg_ref):
        c, s = c_ref[...], s_ref[...]

        def head(src, dsrc, gain, dst, h):
            cols = slice(h * HEAD_DIM, (h + 1) * HEAD_DIM)
            r, xn = _rms_stats(src[:, cols].astype(f32))
            do = dsrc[:, cols].astype(f32)
            dy = do * c + pltpu.roll(do * s, HEAD_DIM // 2, 1)
            dxn = dy * gain
            dst[:, cols] = (r * (dxn - xn * jnp.mean(dxn * xn, axis=-1, keepdims=True))).astype(dst.dtype)
            return jnp.sum(dy * xn, axis=0, keepdims=True)

        dq_gain = head(q_ref, dq_ref, qg_ref[...], dqo_ref, 0)
        for h in range(1, N_Q):
            dq_gain = dq_gain + head(q_ref, dq_ref, qg_ref[...], dqo_ref, h)
        dk_gain = head(k_ref, dk_ref, kg_ref[...], dko_ref, 0)
        for h in range(1, N_KV):
            dk_gain = dk_gain + head(k_ref, dk_ref, kg_ref[...], dko_ref, h)
        _accumulate(dqg_ref, dq_gain)
        _accumulate(dkg_ref, dk_gain)

    return _rows(body, [("r", q_raw), ("r", kv_raw, KV_COLS, 0), ("r", dqr), ("r", dkr), ("f", qg), ("f", kg),
                        ("r", cos2), ("r", sin2)],
                 [("r", Q_COLS, bf16), ("r", KV_COLS, bf16), ("a", (1, HEAD_DIM), f32), ("a", (1, HEAD_DIM), f32)],
                 tm=min(256, q_raw.shape[0]), name=name)


def _softmax_rows(q, k):
    s = lax.dot_general(q, k, (((1,), (1,)), ((), ())), preferred_element_type=f32) * SCALE
    p = jnp.exp(s - jnp.max(s, axis=-1, keepdims=True))
    return p * (1.0 / jnp.sum(p, axis=-1, keepdims=True))


def _head_cols(g):
    return slice(g * HEAD_DIM, (g + 1) * HEAD_DIM)


def _attn_fwd(qr, kr, kv_raw, name):
    S = qr.shape[0]
    tq = min(256, S)

    def body(q_ref, k_ref, v_ref, o_ref):
        k, v = k_ref[...], v_ref[...]
        for g in range(GRP):
            p = _softmax_rows(q_ref[:, _head_cols(g)], k)
            o_ref[:, _head_cols(g)] = jnp.dot(p.astype(bf16), v, preferred_element_type=f32).astype(o_ref.dtype)

    return _call(
        body, name=name, grid=(N_KV, S // tq),
        in_specs=[pl.BlockSpec((tq, GRP * HEAD_DIM), lambda kv, i: (i, kv)),
                  pl.BlockSpec((S, HEAD_DIM), lambda kv, i: (0, kv)),
                  pl.BlockSpec((S, HEAD_DIM), lambda kv, i: (0, N_KV + kv))],
        out_specs=pl.BlockSpec((tq, GRP * HEAD_DIM), lambda kv, i: (i, kv)),
        out_shape=jax.ShapeDtypeStruct((S, Q_COLS), bf16),
        compiler_params=_params(("parallel", "arbitrary"), 4 * GRP * tq * S * 4 + (8 << 20)),
    )(qr, kr, kv_raw)


def _attn_bwd(qr, kr, kv_raw, do, name):
    S = qr.shape[0]
    tq = min(256, S)

    def body(q_ref, k_ref, v_ref, do_ref, dq_ref, dk_ref, dv_ref):
        first = pl.program_id(1) == 0
        k, v = k_ref[...], v_ref[...]
        dv_part = dk_part = None
        for g in range(GRP):
            q, do_ = q_ref[:, _head_cols(g)], do_ref[:, _head_cols(g)]
            p = _softmax_rows(q, k)
            dp = lax.dot_general(do_, v, (((1,), (1,)), ((), ())), preferred_element_type=f32)
            ds = (p * (dp - jnp.sum(dp * p, axis=-1, keepdims=True)) * SCALE).astype(bf16)
            dq_ref[:, _head_cols(g)] = jnp.dot(ds, k, preferred_element_type=f32).astype(dq_ref.dtype)
            dv_g = lax.dot_general(p.astype(bf16), do_, (((0,), (0,)), ((), ())), preferred_element_type=f32)
            dk_g = lax.dot_general(ds, q, (((0,), (0,)), ((), ())), preferred_element_type=f32)
            dv_part = dv_g if g == 0 else dv_part + dv_g
            dk_part = dk_g if g == 0 else dk_part + dk_g

        @pl.when(first)
        def _():
            dv_ref[...] = dv_part
            dk_ref[...] = dk_part

        @pl.when(jnp.logical_not(first))
        def _():
            dv_ref[...] += dv_part
            dk_ref[...] += dk_part

    qspec = pl.BlockSpec((tq, GRP * HEAD_DIM), lambda kv, i: (i, kv))
    return _call(
        body, name=name, grid=(N_KV, S // tq),
        in_specs=[qspec, pl.BlockSpec((S, HEAD_DIM), lambda kv, i: (0, kv)),
                  pl.BlockSpec((S, HEAD_DIM), lambda kv, i: (0, N_KV + kv)), qspec],
        out_specs=[qspec, pl.BlockSpec((S, HEAD_DIM), lambda kv, i: (0, kv)),
                   pl.BlockSpec((S, HEAD_DIM), lambda kv, i: (0, kv))],
        out_shape=[jax.ShapeDtypeStruct((S, Q_COLS), bf16), jax.ShapeDtypeStruct((S, KV_COLS), f32),
                   jax.ShapeDtypeStruct((S, KV_COLS), f32)],
        compiler_params=_params(("parallel", "arbitrary"), 6 * GRP * tq * S * 4 + (8 << 20)),
    )(qr, kr, kv_raw, do)


CONV_HALO = 16


def _fill_padded(pad_ref, val, S):
    pad_ref[pl.ds(0, CONV_HALO), :] = jnp.zeros((CONV_HALO, LANES), f32)
    pad_ref[pl.ds(CONV_HALO + S, CONV_HALO), :] = jnp.zeros((CONV_HALO, LANES), f32)
    pad_ref[pl.ds(CONV_HALO, S), :] = val


def _group_specs(S, n_groups, second_half):
    return pl.BlockSpec((S, LANES), functools.partial(lambda g, o: (0, g + o), o=n_groups if second_half else 0))


def _conv1_fwd(conv_in, wdw, b_dw, name):
    S = conv_in.shape[0]
    ng = CONV_CH // LANES
    R = min(256, S)

    def body(a_ref, g_ref, w_ref, b_ref, o_ref, pad_ref):
        z = a_ref[...].astype(f32) * jax.nn.sigmoid(g_ref[...].astype(f32))
        _fill_padded(pad_ref, z, S)
        for r in range(S // R):
            acc = jnp.zeros((R, LANES), f32) + b_ref[...]
            for j in range(CONV_W):
                acc = acc + w_ref[pl.ds(j, 1), :] * pad_ref[pl.ds(r * R + CONV_HALO - CONV_PAD + j, R), :]
            o_ref[pl.ds(r * R, R), :] = acc

    return _call(
        body, name=name, grid=(ng,),
        in_specs=[_group_specs(S, ng, False), _group_specs(S, ng, True),
                  pl.BlockSpec((CONV_WP, LANES), lambda g: (g, 0)), pl.BlockSpec((1, LANES), lambda g: (0, g))],
        out_specs=pl.BlockSpec((S, LANES), lambda g: (0, g)),
        out_shape=jax.ShapeDtypeStruct((S, CONV_CH), f32),
        scratch_shapes=[pltpu.VMEM((S + 2 * CONV_HALO, LANES), f32)],
        compiler_params=_params(("parallel",), 24 << 20),
    )(conv_in, conv_in, wdw, b_dw)


def _conv1_bwd(conv_in, dc, wdw, name):
    S = conv_in.shape[0]
    ng = CONV_CH // LANES
    R = min(256, S)

    def body(a_ref, g_ref, w_ref, dc_ref, da_ref, dg_ref, dw_ref, db_ref, padz_ref, padd_ref):
        a = a_ref[...].astype(f32)
        sg = jax.nn.sigmoid(g_ref[...].astype(f32))
        _fill_padded(padz_ref, a * sg, S)
        _fill_padded(padd_ref, dc_ref[...], S)
        for r in range(S // R):
            dz = jnp.zeros((R, LANES), f32)
            for j in range(CONV_W):
                dz = dz + w_ref[pl.ds(j, 1), :] * padd_ref[pl.ds(r * R + CONV_HALO + CONV_PAD - j, R), :]
            rows = pl.ds(r * R, R)
            ar, sr = a_ref[rows, :].astype(f32), jax.nn.sigmoid(g_ref[rows, :].astype(f32))
            da_ref[rows, :] = (dz * sr).astype(da_ref.dtype)
            dg_ref[rows, :] = (dz * ar * sr * (1.0 - sr)).astype(dg_ref.dtype)
        for j in range(CONV_W):
            tot = jnp.zeros((1, LANES), f32)
            for r in range(S // R):
                tot = tot + jnp.sum(dc_ref[pl.ds(r * R, R), :] * padz_ref[pl.ds(r * R + CONV_HALO - CONV_PAD + j, R), :],
                                    axis=0, keepdims=True)
            dw_ref[pl.ds(j, 1), :] = tot
        dw_ref[pl.ds(CONV_W, CONV_WP - CONV_W), :] = jnp.zeros((CONV_WP - CONV_W, LANES), f32)
        db_ref[...] = jnp.sum(dc_ref[...], axis=0, keepdims=True)

    return _call(
        body, name=name, grid=(ng,),
        in_specs=[_group_specs(S, ng, False), _group_specs(S, ng, True),
                  pl.BlockSpec((CONV_WP, LANES), lambda g: (g, 0)), pl.BlockSpec((S, LANES), lambda g: (0, g))],
        out_specs=[pl.BlockSpec((S, LANES), lambda g: (0, g)), pl.BlockSpec((S, LANES), lambda g: (0, g)),
                   pl.BlockSpec((CONV_WP, LANES), lambda g: (g, 0)), pl.BlockSpec((1, LANES), lambda g: (0, g))],
        out_shape=[jax.ShapeDtypeStruct((S, CONV_CH), bf16), jax.ShapeDtypeStruct((S, CONV_CH), bf16),
                   jax.ShapeDtypeStruct((ng * CONV_WP, LANES), f32), jax.ShapeDtypeStruct((1, CONV_CH), f32)],
        scratch_shapes=[pltpu.VMEM((S + 2 * CONV_HALO, LANES), f32), pltpu.VMEM((S + 2 * CONV_HALO, LANES), f32)],
        compiler_params=_params(("parallel",), 24 << 20),
    )(conv_in, conv_in, wdw, dc)


def _ln_stats(x, eps=1e-5):
    xc = x - jnp.mean(x, axis=-1, keepdims=True)
    r = lax.rsqrt(jnp.mean(xc * xc, axis=-1, keepdims=True) + eps)
    return r, xc * r


def _ln_bwd(r, xh, dxh):
    return r * (dxh - jnp.mean(dxh, axis=-1, keepdims=True) - xh * jnp.mean(dxh * xh, axis=-1, keepdims=True))


def _conv2_fwd(c, ln_g, ln_b, name):
    def body(c_ref, g_ref, b_ref, o_ref):
        _, xh = _ln_stats(c_ref[...])
        y = xh * g_ref[...] + b_ref[...]
        o_ref[...] = (y * jax.nn.sigmoid(y)).astype(o_ref.dtype)

    return _rows(body, [("r", c), ("f", ln_g), ("f", ln_b)], [("r", CONV_CH, bf16)], tm=min(256, c.shape[0]), name=name)[0]


def _conv2_bwd(c, dcz, ln_g, ln_b, name):
    def body(c_ref, d_ref, g_ref, b_ref, dc_ref, dg_ref, db_ref):
        r, xh = _ln_stats(c_ref[...])
        y = xh * g_ref[...] + b_ref[...]
        sg = jax.nn.sigmoid(y)
        dy = d_ref[...].astype(f32) * (sg * (1.0 + y * (1.0 - sg)))
        dc_ref[...] = _ln_bwd(r, xh, dy * g_ref[...])
        _accumulate(dg_ref, jnp.sum(dy * xh, axis=0, keepdims=True))
        _accumulate(db_ref, jnp.sum(dy, axis=0, keepdims=True))

    return _rows(body, [("r", c), ("r", dcz), ("f", ln_g), ("f", ln_b)],
                 [("r", CONV_CH, f32), ("a", (1, CONV_CH), f32), ("a", (1, CONV_CH), f32)],
                 tm=min(256, c.shape[0]), name=name)


GELU_K = math.sqrt(2.0 / math.pi)
GELU_C = 0.044715


def _gelu(x):
    return 0.5 * x * (1.0 + jnp.tanh(GELU_K * (x + GELU_C * x * x * x)))


def _gelu_and_grad(x):
    x2 = x * x
    th = jnp.tanh(GELU_K * (x + GELU_C * x2 * x))
    half = 0.5 * (1.0 + th)
    return x * half, half + 0.5 * x * (1.0 - th * th) * (GELU_K * (1.0 + 3.0 * GELU_C * x2))


def _chunk_rows(n):
    return pl.ds(pl.multiple_of(n * SG_CHUNK, SG_CHUNK), SG_CHUNK)


def _sgu_fwd(sg_in, ln_g, ln_b, w_s, b_s, name):
    S = sg_in.shape[0]

    def body(u_ref, v_ref, lg_ref, lb_ref, w_ref, b_ref, o_ref):
        wb = w_ref[...].astype(bf16)

        def chunk(n, carry):
            rows = _chunk_rows(n)
            gu = _gelu(u_ref[rows, :].astype(f32))
            _, xh = _ln_stats(_gelu(v_ref[rows, :].astype(f32)))
            vl = xh * lg_ref[...] + lb_ref[...]
            t = jnp.dot(wb, vl.astype(bf16), preferred_element_type=f32) + b_ref[...]
            o_ref[rows, :] = (gu * t).astype(o_ref.dtype)
            return carry

        lax.fori_loop(0, S // SG_CHUNK, chunk, 0, unroll=2)

    return _call(
        body, name=name, grid=(SG_G,),
        in_specs=[_group_specs(S, SG_G, False), _group_specs(S, SG_G, True),
                  pl.BlockSpec((1, LANES), lambda g: (0, g)), pl.BlockSpec((1, LANES), lambda g: (0, g)),
                  pl.BlockSpec((None, SG_CHUNK, SG_CHUNK), lambda g: (g, 0, 0)),
                  pl.BlockSpec((None, SG_CHUNK, 1), lambda g: (g, 0, 0))],
        out_specs=pl.BlockSpec((S, LANES), lambda g: (0, g)),
        out_shape=jax.ShapeDtypeStruct((S, SG_CH), bf16),
        compiler_params=_params(("parallel",), 24 << 20),
    )(sg_in, sg_in, ln_g, ln_b, w_s, b_s)


def _sgu_bwd(sg_in, dsz, ln_g, ln_b, w_s, w_s_t, b_s, name):
    S = sg_in.shape[0]

    def body(u_ref, v_ref, lg_ref, lb_ref, w_ref, wt_ref, b_ref, d_ref, du_ref, dv_ref, dw_ref, db_ref, dlg_ref, dlb_ref):
        wb = w_ref[...].astype(bf16)
        wtb = wt_ref[...].astype(bf16)

        def chunk(n, carry):
            dwa, dba, dlga, dlba = carry
            rows = _chunk_rows(n)
            u = u_ref[rows, :].astype(f32)
            v = v_ref[rows, :].astype(f32)
            gu, gu_grad = _gelu_and_grad(u)
            gv, gv_grad = _gelu_and_grad(v)
            r, xh = _ln_stats(gv)
            vlb = (xh * lg_ref[...] + lb_ref[...]).astype(bf16)
            t = jnp.dot(wb, vlb, preferred_element_type=f32) + b_ref[...]
            d = d_ref[rows, :].astype(f32)
            dt = d * gu
            dtb = dt.astype(bf16)
            dwa = dwa + lax.dot_general(dtb, vlb, (((1,), (1,)), ((), ())), preferred_element_type=f32)
            dba = dba + jnp.sum(dt, axis=1, keepdims=True)
            dvl = jnp.dot(wtb, dtb, preferred_element_type=f32)
            dlga = dlga + jnp.sum(dvl * xh, axis=0, keepdims=True)
            dlba = dlba + jnp.sum(dvl, axis=0, keepdims=True)
            dgv = _ln_bwd(r, xh, dvl * lg_ref[...])
            du_ref[rows, :] = (d * t * gu_grad).astype(du_ref.dtype)
            dv_ref[rows, :] = (dgv * gv_grad).astype(dv_ref.dtype)
            return dwa, dba, dlga, dlba

        init = (jnp.zeros((SG_CHUNK, SG_CHUNK), f32), jnp.zeros((SG_CHUNK, 1), f32),
                jnp.zeros((1, LANES), f32), jnp.zeros((1, LANES), f32))
        dwa, dba, dlga, dlba = lax.fori_loop(0, S // SG_CHUNK, chunk, init, unroll=2)
        dw_ref[...] = dwa
        db_ref[...] = dba
        dlg_ref[...] = dlga
        dlb_ref[...] = dlba

    wspec = pl.BlockSpec((None, SG_CHUNK, SG_CHUNK), lambda g: (g, 0, 0))
    bspec = pl.BlockSpec((None, SG_CHUNK, 1), lambda g: (g, 0, 0))
    lspec = pl.BlockSpec((1, LANES), lambda g: (0, g))
    cspec = pl.BlockSpec((S, LANES), lambda g: (0, g))
    return _call(
        body, name=name, grid=(SG_G,),
        in_specs=[_group_specs(S, SG_G, False), _group_specs(S, SG_G, True), lspec, lspec, wspec, wspec, bspec, cspec],
        out_specs=[cspec, cspec, wspec, bspec, lspec, lspec],
        out_shape=[jax.ShapeDtypeStruct((S, SG_CH), bf16), jax.ShapeDtypeStruct((S, SG_CH), bf16),
                   jax.ShapeDtypeStruct((SG_G, SG_CHUNK, SG_CHUNK), f32), jax.ShapeDtypeStruct((SG_G, SG_CHUNK, 1), f32),
                   jax.ShapeDtypeStruct((1, SG_CH), f32), jax.ShapeDtypeStruct((1, SG_CH), f32)],
        compiler_params=_params(("parallel",), 24 << 20),
    )(sg_in, sg_in, ln_g, ln_b, w_s, w_s_t, b_s, dsz)


def _row_tile(r, c, n_arrays, itemsize=4):
    fits = [tm for tm in range(16, r + 1, 16) if r % tm == 0 and 2 * n_arrays * tm * c * itemsize <= (24 << 20)]
    return fits[-1] if fits else r


def _sum_slots(slots, name):
    n, r, c = slots.shape
    tm = _row_tile(r, c, n + 2)

    def body(s_ref, o_ref):
        acc = s_ref[0].astype(f32)
        for k in range(1, n):
            acc = acc + s_ref[k].astype(f32)
        o_ref[...] = acc

    return _call(body, name=name, grid=(r // tm,),
                 in_specs=[pl.BlockSpec((n, tm, c), lambda i: (0, i, 0))],
                 out_specs=pl.BlockSpec((tm, c), lambda i: (i, 0)),
                 out_shape=jax.ShapeDtypeStruct((r, c), f32),
                 compiler_params=_params(("parallel",), 40 << 20))(slots)


def _add_sibling(g4, recv, core, name):
    _, _, r, c = g4.shape
    tm = _row_tile(r, c, 3, 2)

    def body(core_ref, g_ref, r_ref, o_ref):
        o_ref[...] = g_ref[...] + r_ref[...]

    grid_spec = pltpu.PrefetchScalarGridSpec(
        num_scalar_prefetch=1, grid=(N_CHIP, r // tm),
        in_specs=[pl.BlockSpec((None, None, tm, c), lambda k, i, core_ref: (k, core_ref[0], i, 0)),
                  pl.BlockSpec((None, tm, c), lambda k, i, core_ref: (k, i, 0))],
        out_specs=pl.BlockSpec((None, tm, c), lambda k, i, core_ref: (k, i, 0)))
    return _call(body, name=name, grid_spec=grid_spec, out_shape=jax.ShapeDtypeStruct((N_CHIP, r, c), bf16),
                 compiler_params=_params(("parallel", "parallel"), 40 << 20))(core, g4, recv)


def _adamw(w, g, m, v, name):
    L, r, c = w.shape
    tm = _row_tile(r, c, 7)
    c1 = 1.0 - ADAM_B1 ** ADAM_STEP
    c2 = 1.0 - ADAM_B2 ** ADAM_STEP

    def body(w_ref, g_ref, m_ref, v_ref, d_ref, mo_ref, vo_ref):
        g_ = g_ref[...]
        m_ = ADAM_B1 * m_ref[...] + (1.0 - ADAM_B1) * g_
        v_ = ADAM_B2 * v_ref[...] + (1.0 - ADAM_B2) * (g_ * g_)
        d_ref[...] = -ADAM_LR * ((m_ / c1) / (jnp.sqrt(v_ / c2) + ADAM_EPS) + ADAM_WD * w_ref[...])
        mo_ref[...] = m_
        vo_ref[...] = v_

    spec = pl.BlockSpec((None, tm, c), lambda l, i: (l, i, 0))
    shp = jax.ShapeDtypeStruct((L, r, c), f32)
    return _call(body, name=name, grid=(L, r // tm), in_specs=[spec] * 4, out_specs=[spec] * 3,
                 out_shape=[shp] * 3, compiler_params=_params(("parallel", "parallel"), 40 << 20))(w, g, m, v)


def _mesh_pos():
    return lax.axis_index("x"), lax.axis_index("y"), lax.axis_index("c")


def _all_gather(shards, after, name):
    n = len(shards)

    def body(*refs):
        x_refs, o_refs = refs[:n], refs[n + 1:2 * n + 1]
        send_sems, recv_sems, local_sems = refs[2 * n + 1:]
        x, y, c = _mesh_pos()
        me, sibling = (x, y, c), (x, y, 1 - c)
        chips = [(1 - x, y), (x, 1 - y), (1 - x, 1 - y)]

        def rows(k, px, py, pc):
            return o_refs[k].at[4 * px + 2 * py + pc]

        def copy(k, s, block, to, src=None):
            return pltpu.make_async_remote_copy(
                src_ref=rows(k, *block) if src is None else src, dst_ref=rows(k, *block),
                send_sem=send_sems.at[k, s], recv_sem=recv_sems.at[k, s], device_id=to, device_id_type=MESH)

        mine = [pltpu.make_async_copy(x_refs[k], rows(k, *me), local_sems.at[k]) for k in range(n)]
        for cp in mine:
            cp.start()
        first = [copy(k, 0, me, sibling, src=x_refs[k]) for k in range(n)]
        for j, chip in enumerate(chips):
            first += [copy(k, 1 + j, me, (*chip, c), src=x_refs[k]) for k in range(n)]
        for cp in first:
            cp.start()
        passed = []
        for j, chip in enumerate(chips):
            for k in range(n):
                copy(k, 1 + j, (*chip, c), me).wait_recv()
                fwd = copy(k, 4 + j, (*chip, c), sibling)
                fwd.start()
                passed.append(fwd)
        for k in range(n):
            copy(k, 0, sibling, me).wait_recv()
        for j, chip in enumerate(chips):
            for k in range(n):
                copy(k, 4 + j, (*chip, 1 - c), me).wait_recv()
        for cp in first + passed:
            cp.wait_send()
        for cp in mine:
            cp.wait()

    return _call(
        body, name=name, in_specs=[HBM] * n + [ANY], out_specs=[HBM] * n,
        out_shape=[jax.ShapeDtypeStruct((N_DEV,) + s.shape, s.dtype) for s in shards],
        scratch_shapes=[pltpu.SemaphoreType.DMA((n, 7)), pltpu.SemaphoreType.DMA((n, 7)), pltpu.SemaphoreType.DMA((n,))],
    )(*shards, after)


SEM =pl.BlockSpec(memory_space=pltpu.SEMAPHORE)
ANY = pl.BlockSpec(memory_space=pl.ANY)
EFFECT = pltpu.SideEffectType.DATAFLOW_SIDE_EFFECTING


def _other_chips(x, y):
    return [(1 - x, y), (x, 1 - y), (1 - x, 1 - y)]


TO_SIBLING = ("sibling", "halves")


def _peers(kind, x, y):
    return [(x, y)] if kind in TO_SIBLING else _other_chips(x, y)


def _ici_copy(kind, src_ref, land_ref, send_sem, recv_sem, sender, target, c, incoming=False):
    (sx, sy), (tx, ty) = sender, target
    if kind == "sibling":
        return pltpu.make_async_remote_copy(src_ref=src_ref.at[:, 1 - c], dst_ref=land_ref, send_sem=send_sem,
                                            recv_sem=recv_sem, device_id=(tx, ty, 1 - c), device_id_type=MESH)
    if kind == "halves":
        return pltpu.make_async_remote_copy(src_ref=land_ref.at[:, c], dst_ref=land_ref.at[:, 1 - c if incoming else c],
                                            send_sem=send_sem, recv_sem=recv_sem, device_id=(tx, ty, 1 - c),
                                            device_id_type=MESH)
    if kind == "gather":
        src, dst = src_ref, land_ref.at[4 * sx + 2 * sy + c]
    else:
        src, dst = src_ref.at[2 * tx + ty], land_ref.at[2 * sx + sy]
    return pltpu.make_async_remote_copy(src_ref=src, dst_ref=dst, send_sem=send_sem, recv_sem=recv_sem,
                                        device_id=(tx, ty, c), device_id_type=MESH)


def _ici_start(kind, srcs, lands, after, name):
    n = len(srcs)
    npeer = 1 if kind in TO_SIBLING else 3

    def body(*refs):
        src_refs, land_refs = refs[:n], refs[n:2 * n]
        send_sems, recv_sems = refs[2 * n + 1], refs[2 * n + 2]
        token = refs[-1]
        x, y, c = _mesh_pos()
        for j, chip in enumerate(_peers(kind, x, y)):
            for k in range(n):
                _ici_copy(kind, src_refs[k], land_refs[k], send_sems.at[npeer * k + j], recv_sems.at[npeer * k + j],(x, y), chip, c).start()
        token[...] = jnp.zeros_like(token)

    bufs = list(srcs) + list(lands)
    return _call(
        body, name=name,
        out_shape=(pltpu.SemaphoreType.DMA((npeer * n,)), pltpu.SemaphoreType.DMA((npeer * n,)),
                   *[pltpu.HBM(b.shape, b.dtype) for b in bufs], jax.ShapeDtypeStruct((8, LANES), f32)),
        in_specs=[HBM] * (2 * n) + [ANY], out_specs=(SEM, SEM, *[HBM] * (2 * n), pl.BlockSpec(memory_space=pltpu.VMEM)),
        input_output_aliases={i: 2 + i for i in range(2 * n)},
        compiler_params=pltpu.CompilerParams(has_side_effects=EFFECT),
    )(*[pltpu.with_memory_space_constraint(b, pltpu.HBM) for b in bufs], after)


def _ici_wait(kind, started, after, name):
    send_sems, recv_sems, *bufs = started[:-1]
    n = len(bufs) // 2
    npeer = 1 if kind in TO_SIBLING else 3

    def body(*refs):
        src_refs, land_refs = refs[:n], refs[n:2 * n]
        send_sems, recv_sems = refs[2 * n], refs[2 * n + 1]
        x, y, c = _mesh_pos()
        for j, chip in enumerate(_peers(kind, x, y)):
            for k in range(n):
                _ici_copy(kind, src_refs[k], land_refs[k], send_sems.at[npeer * k + j], recv_sems.at[npeer * k + j],(x, y), chip, c).wait_send()
                _ici_copy(kind, src_refs[k], land_refs[k], send_sems.at[npeer * k + j], recv_sems.at[npeer * k + j],chip, (x, y), c, incoming=True).wait_recv()

    out = _call(
        body, name=name, out_shape=[pltpu.HBM(b.shape, b.dtype) for b in bufs],
        in_specs=[HBM] * (2 * n) + [SEM, SEM, ANY], out_specs=[HBM] * (2 * n),
        input_output_aliases={i: i for i in range(2 * n)},
        compiler_params=pltpu.CompilerParams(has_side_effects=EFFECT),
    )(*bufs, send_sems, recv_sems, after)
    return out[:n], out[n:]


def _sum_chip_slots(lands, sums, chip, name):
    _, r, c = lands.shape
    tm = _row_tile(r, c, 10, 2)

    def body(chip_ref, l_ref, s_ref, o_ref):
        acc = None
        for k in range(N_CHIP):
            part = jnp.where(chip_ref[0] == k, s_ref[k], l_ref[k]).astype(f32)
            acc = part if acc is None else acc + part
        o_ref[...] = acc

    grid_spec = pltpu.PrefetchScalarGridSpec(
        num_scalar_prefetch=1, grid=(r // tm,),
        in_specs=[pl.BlockSpec((N_CHIP, tm, c), lambda i, chip_ref: (0, i, 0)),
                  pl.BlockSpec((N_CHIP, tm, c), lambda i, chip_ref: (0, i, 0))],
        out_specs=pl.BlockSpec((tm, c), lambda i, chip_ref: (i, 0)))
    return _call(body, name=name, grid_spec=grid_spec, out_shape=jax.ShapeDtypeStruct((r, c), f32),
                 compiler_params=_params(("parallel",), 40 << 20))(chip, lands, sums)


def _reduce_begin(grads, core, tag):
    g4s = [g.reshape(N_CHIP, 2, g.shape[0] // N_DEV, g.shape[1]) for g in grads]
    recvs = [lax.empty((N_CHIP,) + g.shape[2:], g.dtype) for g in g4s]
    return _ici_start("sibling", g4s, recvs, core, name="rs_d2d_start_" + tag)


def _reduce_continue(begun, core, after, tag):
    g4s, recvs = _ici_wait("sibling", begun, after, name="rs_d2d_wait_" + tag)
    sums = [_add_sibling(g4, rv, core, name="rs_add_" + tag) for g4, rv in zip(g4s, recvs)]
    lands = [lax.empty(s.shape, s.dtype) for s in sums]
    return _ici_start("reduce", sums, lands, core, name="rs_start_" + tag)


def _adamw_reduced(layer, w, m, v, lands, sums, chip, prev, name):
    L, r, c = w.shape
    tm = _row_tile(r, c, 11)
    c1 = 1.0 - ADAM_B1 ** ADAM_STEP
    c2 = 1.0 - ADAM_B2 ** ADAM_STEP
    n_prev = 0 if prev is None else 4

    def body(chip_ref, w_ref, m_ref, v_ref, l_ref, s_ref, *refs):
        g_ref, d_ref, mo_ref, vo_ref = refs[n_prev:]
        g_ = None
        for k in range(N_CHIP):
            part = jnp.where(chip_ref[0] == k, s_ref[k], l_ref[k]).astype(f32)
            g_ = part if g_ is None else g_ + part
        m_ = ADAM_B1 * m_ref[...] + (1.0 - ADAM_B1) * g_
        v_ = ADAM_B2 * v_ref[...] + (1.0 - ADAM_B2) * (g_ * g_)
        g_ref[...] = g_
        d_ref[...] = -ADAM_LR * ((m_ / c1) / (jnp.sqrt(v_ / c2) + ADAM_EPS) + ADAM_WD * w_ref[...])
        mo_ref[...] = m_
        vo_ref[...] = v_

    wspec = pl.BlockSpec((None, tm, c), lambda i, chip_ref: (layer, i, 0))
    sspec = pl.BlockSpec((N_CHIP, tm, c), lambda i, chip_ref: (0, i, 0))
    grid_spec = pltpu.PrefetchScalarGridSpec(
        num_scalar_prefetch=1, grid=(r // tm,), in_specs=[wspec] * 3 + [sspec] * 2 + [ANY] * n_prev, out_specs=[wspec] * 4)
    return _call(body, name=name, grid_spec=grid_spec, out_shape=[jax.ShapeDtypeStruct((L, r, c), f32)] * 4,
                 input_output_aliases={6 + i: i for i in range(n_prev)},
                 compiler_params=_params(("parallel",), 40 << 20))(chip, w, m, v, lands, sums, *(prev or ()))


def _rope_tables(S):
    rows = S // GRID_W
    row = jnp.repeat(jnp.arange(rows, dtype=f32), GRID_W)
    col = jnp.tile(jnp.arange(GRID_W, dtype=f32), rows)
    nf = HEAD_DIM // 4
    inv = ROPE_THETA ** (-jnp.arange(nf, dtype=f32) / nf)
    ang = jnp.concatenate([row[:, None] * inv, col[:, None] * inv], axis=-1)
    cos, sin = jnp.cos(ang), jnp.sin(ang)
    return jnp.concatenate([cos, cos], axis=-1), jnp.concatenate([-sin, sin], axis=-1)


def _after(x, dep):
    return lax.optimization_barrier((x, dep))[0]


def _layer_fwd(xin, p, w, expect_weights, more_weights, ffn_weights_early, cos2, sin2):
    sv = {"xin": xin}
    h = sv["h"] = _rms_fwd(xin, p["g_mix"], name="rms_mix")
    proj = functools.partial(_mm, h, w["in"], "nt", bf16)
    q_raw = sv["q_raw"] = proj(n=Q_COLS, b_off=0, name="proj_q")
    kv_raw = sv["kv_raw"] = proj(n=2 * KV_COLS, b_off=OFF_KV, name="proj_kv")
    conv_in = sv["conv_in"] = proj(n=2 * CONV_CH, b_off=OFF_CONV, name="proj_conv")
    sg_in = sv["sg_in"] = proj(n=2 * SG_CH, b_off=OFF_SG, name="proj_sg")
    gl = sv["gl"] = proj(n=3 * D_MODEL, b_off=OFF_GATE, name="proj_gate")
    zero = expect_weights(1, gl)[0, 0]
    qr, kr = sv["qr"], sv["kr"] = _qk_fwd(q_raw, kv_raw, p["q_norm_g"] + zero, p["k_norm_g"], cos2, sin2, name="qk_fwd")
    o = sv["o"] = _attn_fwd(qr, kr, kv_raw, name="attn_fwd")
    c = sv["c"] = _conv1_fwd(conv_in, w["dw"], _after(p["b_dw"], o), name="conv1_fwd")
    cz = sv["cz"] = _conv2_fwd(c, p["conv_ln_g"], p["conv_ln_b"], name="conv2_fwd")
    sz = sv["sz"] = _sgu_fwd(sg_in, _after(p["sg_ln_g"], cz), p["sg_ln_b"], p["w_s"], p["b_s"], name="sgu_fwd")
    w = {**w, **more_weights(1, sz)}
    b_gate = p["b_gate"]
    if ffn_weights_early:
        b_gate = b_gate + expect_weights(2, w["out"])[0, 0]
    sv["ya"], sv["yc"], sv["ys"], merged = _mixer_out([o, cz, sz], [w["attn_o"], w["conv_o"], w["sg_o"]], gl, b_gate,
                                                      name="mixer_out")
    sv["merged"] = merged
    x1 =sv["x1"] = _mm(merged, w["out"], "nn", f32, res=xin, name="out_proj")
    w = {**w, **more_weights(2, x1)}
    hf = sv["hf"] = _rms_fwd(x1, p["g_ffn"], name="rms_ffn")
    sv["fg"], sv["fu"], act = _ffn_up(hf, w["ff_gate"], w["ff_up"], name="ffn_up")
    sv["act"] = act
    x2 =_mm(act, w["ff_down"], "nn", f32, res=x1, name="ff_down")
    return x2, sv, w


def _layer_bwd(dx2, dx2b, sv, p, w, cos2, sin2, reduce_begin, reduce_continue, last):
    small = {}
    dfg, dfu = _ffn_down_bwd(dx2b, w["ff_down"], sv["fg"], sv["fu"], name="ffn_down_bwd")
    g_down = _mm(sv["act"], dx2b, "tn", bf16, name="g_ff_down")
    dhf = _mm(dfg, w["ff_gate"], "nn", f32, name="d_hf_gate")
    dhf = _mm(dfu, w["ff_up"], "nn", f32, res=dhf, name="d_hf_up")
    g_gate = _mm(dfg, sv["hf"], "tn", bf16, name="g_ff_gate")
    g_up = _mm(dfu, sv["hf"], "tn", bf16, name="g_ff_up")
    zero = reduce_begin("ffn", dict(w_ff_gate=g_gate, w_ff_up=g_up, w_ff_down=g_down))[0, 0]
    dx1, dx1b, small["g_ffn"] = _rms_bwd(sv["x1"], p["g_ffn"] + zero, dhf, dx2, name="rms_ffn_bwd")
    g_out = _mm(sv["merged"], dx1b, "tn", bf16, name="g_out")
    *dgl, dya, dyc, dys, db0, db1, db2 = _merge_bwd_fused(dx1b, w["out"], sv["gl"], p["b_gate"], sv["ya"], sv["yc"], sv["ys"],
                                                        name="merge_bwd")
    small["b_gate"] = jnp.concatenate([db0, db1, db2], axis=1)
    do = _mm(dya, w["attn_o"], "nn", bf16, after=reduce_continue("ffn", dya), name="d_o")
    g_ao = _mm(dya, sv["o"], "tn", bf16, name="g_attn_o")
    dcz = _mm(dyc, w["conv_o"], "nn", bf16, name="d_cz")
    g_co = _mm(dyc, sv["cz"], "tn", bf16, name="g_conv_o")
    dsz = _mm(dys, w["sg_o"], "nn", bf16, name="d_sz")
    g_so = _mm(dys, sv["sz"], "tn", bf16, name="g_sg_o")
    zero = reduce_begin("mix", dict(w_attn_o=g_ao, w_conv_o=g_co, w_sg_o=g_so, w_out=g_out))[0, 0]
    dsu, dsv, small["w_s"], small["b_s"], small["sg_ln_g"], small["sg_ln_b"] = _sgu_bwd(
        sv["sg_in"], dsz, p["sg_ln_g"] + zero, p["sg_ln_b"], p["w_s"], p["w_s_t"], p["b_s"], name="sgu_bwd")
    dc, small["conv_ln_g"], small["conv_ln_b"] = _conv2_bwd(sv["c"], dcz, p["conv_ln_g"], p["conv_ln_b"], name="conv2_bwd")
    da, dgt, small["w_dw"], small["b_dw"] = _conv1_bwd(sv["conv_in"], dc, w["dw"], name="conv1_bwd")
    zero = reduce_continue("mix", da)[0, 0]
    dqr, dkr, dv = _attn_bwd(sv["qr"], sv["kr"], sv["kv_raw"], do, name="attn_bwd")
    dq_raw, dk_raw, small["q_norm_g"], small["k_norm_g"] = _qk_bwd(
        sv["q_raw"], sv["kv_raw"], dqr, dkr, p["q_norm_g"] + zero, p["k_norm_g"], cos2, sin2, name="qk_bwd")
    dproj = jnp.concatenate([dq_raw, dk_raw, dv.astype(bf16), da, dgt, dsu, dsv, *dgl], axis=1)
    g_in = _mm(dproj, sv["h"], "tn", bf16, name="g_in")
    begun = reduce_begin("in", dict(w_in=g_in))
    if last:
        begun = reduce_continue("in", begun)
    dh = _mm(dproj, w["in"], "nn", f32, after=begun, name="d_h")
    zero = begun[0, 0] if last else reduce_continue("in", dh)[0, 0]
    dx, dxb, small["g_mix"] = _rms_bwd(sv["xin"], p["g_mix"] + zero, dh, dx1, name="rms_mix_bwd")
    return dx, dxb, small


SMALL = ("g_mix", "b_gate", "q_norm_g", "k_norm_g", "b_dw", "conv_ln_g", "conv_ln_b", "sg_ln_g", "sg_ln_b",
         "w_s", "b_s", "g_ffn")
PACK_ALIGN = 8 * LANES


def _pack(parts):
    flat = jnp.concatenate([a.reshape(-1).astype(f32) for a in parts])
    pad = -flat.shape[0] % PACK_ALIGN
    return jnp.pad(flat, (0, pad)).reshape(-1, LANES)


def _unpack(buf, shapes):
    flat = buf.reshape(-1)
    out, pos = [], 0
    for shp in shapes:
        size = math.prod(shp)
        out.append(flat[pos:pos + size].reshape(shp))
        pos += size
    return out


def kernel(x, g_mix, w_in, b_gate, q_norm_g, k_norm_g, w_attn_o, w_dw, b_dw, conv_ln_g, conv_ln_b, w_conv_o, sg_ln_g, sg_ln_b, w_s, b_s, w_sg_o, w_out, g_ffn, w_ff_gate, w_ff_up, w_ff_down, g_final, loss_target, m_g_mix, m_w_in, m_b_gate, m_q_norm_g, m_k_norm_g, m_w_attn_o, m_w_dw, m_b_dw, m_conv_ln_g, m_conv_ln_b, m_w_conv_o, m_sg_ln_g, m_sg_ln_b, m_w_s, m_b_s, m_w_sg_o, m_w_out, m_g_ffn, m_w_ff_gate, m_w_ff_up, m_w_ff_down, m_g_final, v_g_mix, v_w_in, v_b_gate, v_q_norm_g, v_k_norm_g, v_w_attn_o, v_w_dw, v_b_dw, v_conv_ln_g, v_conv_ln_b, v_w_conv_o, v_sg_ln_g, v_sg_ln_b, v_w_s, v_b_s, v_w_sg_o, v_w_out, v_g_ffn, v_w_ff_gate, v_w_ff_up, v_w_ff_down, v_g_final):
    weights = dict(g_mix=g_mix, w_in=w_in, b_gate=b_gate, q_norm_g=q_norm_g, k_norm_g=k_norm_g, w_attn_o=w_attn_o,
                   w_dw=w_dw, b_dw=b_dw, conv_ln_g=conv_ln_g, conv_ln_b=conv_ln_b, w_conv_o=w_conv_o, sg_ln_g=sg_ln_g,
                   sg_ln_b=sg_ln_b, w_s=w_s, b_s=b_s, w_sg_o=w_sg_o, w_out=w_out, g_ffn=g_ffn, w_ff_gate=w_ff_gate,
                   w_ff_up=w_ff_up, w_ff_down=w_ff_down, g_final=g_final)
    mom_m = dict(g_mix=m_g_mix, w_in=m_w_in, b_gate=m_b_gate, q_norm_g=m_q_norm_g, k_norm_g=m_k_norm_g,
                 w_attn_o=m_w_attn_o, w_dw=m_w_dw, b_dw=m_b_dw, conv_ln_g=m_conv_ln_g, conv_ln_b=m_conv_ln_b,
                 w_conv_o=m_w_conv_o, sg_ln_g=m_sg_ln_g, sg_ln_b=m_sg_ln_b, w_s=m_w_s, b_s=m_b_s, w_sg_o=m_w_sg_o,
                 w_out=m_w_out, g_ffn=m_g_ffn, w_ff_gate=m_w_ff_gate, w_ff_up=m_w_ff_up, w_ff_down=m_w_ff_down,
                 g_final=m_g_final)
    mom_v = dict(g_mix=v_g_mix, w_in=v_w_in, b_gate=v_b_gate, q_norm_g=v_q_norm_g, k_norm_g=v_k_norm_g,
                 w_attn_o=v_w_attn_o, w_dw=v_w_dw, b_dw=v_b_dw, conv_ln_g=v_conv_ln_g, conv_ln_b=v_conv_ln_b,
                 w_conv_o=v_w_conv_o, sg_ln_g=v_sg_ln_g, sg_ln_b=v_sg_ln_b, w_s=v_w_s, b_s=v_b_s, w_sg_o=v_w_sg_o,
                 w_out=v_w_out, g_ffn=v_g_ffn, w_ff_gate=v_w_ff_gate, w_ff_up=v_w_ff_up, w_ff_down=v_w_ff_down,
                 g_final=v_g_final)
    S, D = x.shape[1], x.shape[2]
    xi, yi, ci = _mesh_pos()
    me = 4 * xi + 2 * yi + ci
    core = jnp.reshape(ci, (1,)).astype(jnp.int32)
    cos2, sin2 = _rope_tables(S)

    big = ("w_in", "w_attn_o", "w_conv_o", "w_sg_o", "w_out", "w_ff_gate", "w_ff_up", "w_ff_down")
    transposed = {"w_in", "w_attn_o", "w_conv_o", "w_sg_o", "w_ff_gate", "w_ff_up"}
    chip = jnp.reshape(2 * xi + yi, (1,)).astype(jnp.int32)
    groups = (("in", "dw"), ("attn_o", "conv_o", "sg_o", "out"), ("ff_gate", "ff_up", "ff_down"))
    P, shards = [], []
    for l in range(DEPTH):
        sh = {n[2:]: (weights[n][l].T if n in transposed else weights[n][l]).astype(bf16) for n in big}
        sh["dw"] = jnp.pad(w_dw[l].reshape(CONV_W, LANES), ((0, CONV_WP - CONV_W), (0, 0)))
        shards.append(sh)
        p = {n: weights[n][l].reshape(1, -1) for n in SMALL if n not in ("w_s", "b_s")}
        p["w_s"] = w_s[l]
        p["w_s_t"] = jnp.swapaxes(w_s[l], 1, 2)
        p["b_s"] = b_s[l].reshape(SG_G, SG_CHUNK, 1)
        P.append(p)

    gathers = {}

    def start_gather(l, gi, after):
        srcs = [shards[l][n] for n in groups[gi]]
        lands = [lax.dynamic_update_index_in_dim(lax.empty((N_DEV,) + s.shape, s.dtype), s, me, 0) for s in srcs]
        gathers[l, gi] = _ici_start("gather", srcs, lands, after, name=f"ag_start_{l}{gi}")
        return gathers[l, gi][-1]

    halves = {}

    def arrive(l, gi, after):
        srcs, lands = _ici_wait("gather", gathers[l, gi], after, name=f"ag_wait_{l}{gi}")
        after = core
        if gi == len(groups) - 1 and l + 1 < DEPTH:
            after = srcs[0]
            for gj in range(len(groups)):
                after = start_gather(l + 1, gj, after)
        lands4 = [b.reshape(N_CHIP, 2, *b.shape[1:]) for b in lands]
        halves[l, gi] = _ici_start("halves", srcs, lands4, after, name=f"ag_d2d_start_{l}{gi}")
        return halves[l, gi][-1]

    def gathered(l, gi, after):
        if (l, gi) not in halves:
            after = arrive(l, gi, after)
        _, full = _ici_wait("halves", halves[l, gi], after, name=f"ag_d2d_wait_{l}{gi}")
        return {n: f.reshape(-1, f.shape[3]) for n, f in zip(groups[gi], full)}

    all_started = cos2
    for gi in range(len(groups)):
        all_started = start_gather(0, gi, all_started)

    h = x.reshape(S, D)
    saved, W = [], []
    for l in range(DEPTH):
        first = gathered(l, 0, all_started if l == 0 else h)
        if l == 0:
            P[l]["g_mix"] = P[l]["g_mix"] + all_started[0, 0]
        h, sv, w = _layer_fwd(h, P[l], first, functools.partial(lambda gi, z, l: arrive(l, gi, z), l=l),
                              functools.partial(lambda gi, z, l: gathered(l, gi, z), l=l), l > 0, cos2, sin2)
        saved.append(sv)
        W.append(w)
    dx, dxb, sq, g_final_part = _final_loss(h, g_final.reshape(1, D), loss_target.reshape(S, D), name="final_loss")
    loss = lax.psum(0.5 * jnp.sum(sq) / D, ("x", "y", "c"))

    begun, reductions, small_grads = {}, {}, [None] * DEPTH
    for l in reversed(range(DEPTH)):
        def reduce_begin(group, grads, l=l):
            begun[l, group] = (tuple(grads), _reduce_begin(list(grads.values()), core, tag=f"{group}{l}"))
            return begun[l, group][1][-1]

        def reduce_continue(group, after, l=l):
            names, started = begun[l, group]
            reductions[l, group] = (names, _reduce_continue(started, core, after, tag=f"{group}{l}"))
            return reductions[l, group][1][-1]

        dx, dxb, small_grads[l] = _layer_bwd(dx, dxb, saved[l], P[l], W[l], cos2, sin2, reduce_begin, reduce_continue,
                                            last=(l == 0))
    grad_x = dx.reshape(x.shape)

    grads_out, delta, new_m, new_v = {}, {}, {}, {}
    swap = lambda a: jnp.swapaxes(a, 1, 2)

    def update(n, lands, sums):
        as_arrives = n not in transposed or weights[n].shape[2] % LANES != 0
        if as_arrives:
            to_arrival = swap if n in transposed else (lambda a: a)
            out = None
            for l in reversed(range(DEPTH)):
                out = _adamw_reduced(l, to_arrival(weights[n]), to_arrival(mom_m[n]), to_arrival(mom_v[n]),
                                     lands[l], sums[l], chip, out, name=f"adamw_{n}_{l}")
            grads_out[n], delta[n], new_m[n], new_v[n] = [to_arrival(o) for o in out]
        else:
            g = jnp.stack([_sum_chip_slots(lands[l], sums[l], chip, name="rs_sum_" + n) for l in range(DEPTH)])
            grads_out[n] = swap(g)
            delta[n], new_m[n], new_v[n] = _adamw(weights[n], grads_out[n], mom_m[n], mom_v[n], name="adamw_" + n)
        return delta[n]

    after = dx
    for group in ("ffn", "mix", "in"):
        names = reductions[0, group][0]
        arrived = [_ici_wait("reduce", reductions[l, group][1], after, name=f"rs_wait_{group}{l}") for l in range(DEPTH)]
        for i, n in enumerate(names):
            after = update(n, [arrived[l][1][i] for l in range(DEPTH)], [arrived[l][0][i] for l in range(DEPTH)])

    small_shapes = [weights[n].shape for n in SMALL] + [g_final.shape, (DEPTH, CONV_CH // LANES, CONV_WP, LANES)]
    parts = [jnp.stack([small_grads[l][n].reshape(weights[n].shape[1:]) for l in range(DEPTH)]) for n in SMALL]
    parts += [g_final_part.reshape(g_final.shape), jnp.stack([small_grads[l]["w_dw"] for l in range(DEPTH)])]
    packed = _pack(parts)
    gathered = _all_gather([packed], after, name="gather_small")[0]
    total = _sum_slots(gathered, name="sum_small")
    small_total = _unpack(total, small_shapes)
    grads_out.update(zip(SMALL + ("g_final",), small_total[:-1]))
    dw_full = small_total[-1]
    grads_out["w_dw"] = lax.dynamic_index_in_dim(dw_full, me, axis=1, keepdims=False)[:, :CONV_W].reshape(w_dw.shape)

    rep = tuple(n for n in SMALL if n != "w_s") + ("g_final",)
    rep_shapes = [weights[n].shape for n in rep]
    packs = [_pack([src[n] for n in rep])[None] for src in (weights, grads_out, mom_m, mom_v)]
    for dst, buf in zip((delta, new_m, new_v), _adamw(*packs, name="adamw_small")):
        dst.update(zip(rep, _unpack(buf[0], rep_shapes)))
    for n, shp in (("w_dw", (1, DEPTH * CONV_W, LANES)), ("w_s", (DEPTH, SG_G * SG_CHUNK, SG_CHUNK))):
        upd = _adamw(*[src[n].reshape(shp) for src in (weights, grads_out, mom_m, mom_v)], name="adamw_" + n)
        for dst, buf in zip((delta, new_m, new_v), upd):
            dst[n] = buf.reshape(weights[n].shape)

    order = ("g_mix", "w_in", "b_gate", "q_norm_g", "k_norm_g", "w_attn_o", "w_dw", "b_dw", "conv_ln_g", "conv_ln_b",
             "w_conv_o", "sg_ln_g", "sg_ln_b", "w_s", "b_s", "w_sg_o", "w_out", "g_ffn", "w_ff_gate", "w_ff_up",
             "w_ff_down", "g_final")
    return (loss, grad_x, *[grads_out[n] for n in order], *[delta[n] for n in order],
            *[new_m[n] for n in order], *[new_v[n] for n in order])
```

```python
import functools
import math

import jax
import jax.numpy as jnp
from jax import lax
from jax.experimental import pallas as pl
from jax.experimental.pallas import tpu as pltpu

f32, bf16 = jnp.float32, jnp.bfloat16

D_MODEL = 2048
SEQ = 2048
DEPTH = 2
GRID_W = 64
HEAD_DIM = 128
LANES = 128
N_Q = (D_MODEL // 2) // HEAD_DIM
N_KV = N_Q // 4
GRP = N_Q // N_KV
Q_COLS = N_Q * HEAD_DIM
KV_COLS = N_KV * HEAD_DIM
CONV_CH = D_MODEL // 2
CONV_W = 31
CONV_PAD = CONV_W // 2
CONV_WP = 32
SG_CH = D_MODEL // 2
SG_G = SG_CH // LANES
SG_CHUNK = 128
D_FF = -(-8 * D_MODEL // (3 * 256)) * 256
OFF_KV = Q_COLS
OFF_CONV = OFF_KV + 2 * KV_COLS
OFF_SG = OFF_CONV + 2 * CONV_CH
OFF_GATE = OFF_SG + 2 * SG_CH
IN_COLS = OFF_GATE + 3 * D_MODEL
ROPE_THETA = 10000.0
SCALE = HEAD_DIM ** -0.5
N_DEV = 8
N_CHIP = 4

ADAM_LR, ADAM_B1, ADAM_B2, ADAM_EPS, ADAM_WD, ADAM_STEP = 0.001, 0.9, 0.999, 1e-08, 0.01, 10

VMEM_BYTES_V7X = 64 << 20
VMEM_CAP = VMEM_BYTES_V7X - (6 << 20)
MESH = pl.DeviceIdType.MESH
HBM = pl.BlockSpec(memory_space=pltpu.HBM)


def _in_hbm(a):
    if isinstance(a, jax.Array) and jnp.issubdtype(a.dtype, jnp.floating) and a.size * a.dtype.itemsize >= (1 << 20):
        return pltpu.with_memory_space_constraint(a, pltpu.HBM)
    return a


def _out_hbm(s):
    if isinstance(s, jax.ShapeDtypeStruct) and math.prod(s.shape) * jnp.dtype(s.dtype).itemsize >= (1 << 20):
        return pltpu.HBM(s.shape, s.dtype)
    return s


def _call(body, **kw):
    shapes = kw.pop("out_shape")
    shapes = type(shapes)(_out_hbm(s) for s in shapes) if isinstance(shapes, (list, tuple)) else _out_hbm(shapes)
    call = pl.pallas_call(body, out_shape=shapes, **kw)
    return lambda *args: call(*[_in_hbm(a) for a in args])


def _pick(n, cands):
    for c in cands:
        if n % c == 0:
            return c
    raise ValueError((n, cands))


def _params(sem, vmem_bytes):
    return pltpu.CompilerParams(dimension_semantics=sem, vmem_limit_bytes=int(min(max(vmem_bytes, 16 << 20), VMEM_CAP)))


def _mm(a, b, form, out_dtype, *, n=None, b_off=0, res=None, after=None, name):
    if form == "tn":
        K, M = a.shape
    else:
        M, K = a.shape
    N = n if n is not None else (b.shape[0] if form == "nt" else b.shape[1])
    if K <= 2048:
        tk = K
        if form == "tn":
            tm = _pick(M, (512, 256, 128))
            tn = N if N <= 2048 else _pick(N, (1024, 512, 256, 128))
        else:
            tm = M if M <= 2048 else _pick(M, (2048, 1024, 512))
            tn = _pick(math.gcd(N, b_off) if b_off else N, (256, 128) if res is not None else (512, 256, 128))
    else:
        tk = max(t for t in range(LANES, 3072 + 1, LANES) if K % t == 0)
        tm = _pick(M, (1024, 512, 256, 128))
        tn = _pick(math.gcd(N, b_off) if b_off else N, (1024, 512, 256, 128))
    assert b_off % tn == 0
    off = b_off // tn
    nk = K // tk
    if form == "tn":
        a_spec = pl.BlockSpec((tk, tm), lambda i, j, k: (k, i))
    else:
        a_spec = pl.BlockSpec((tm, tk), lambda i, j, k: (i, k))
    if form == "nt":
        b_spec = pl.BlockSpec((tn, tk), lambda i, j, k: (j + off, k))
    else:
        b_spec = pl.BlockSpec((tk, tn), lambda i, j, k: (k, j + off))
    dims = {"nn": ((1,), (0,)), "nt": ((1,), (1,)), "tn": ((0,), (0,))}[form]
    has_res = res is not None

    def body(*refs):
        if after is not None:
            refs = refs[1:]
        if has_res:
            a_ref, b_ref, r_ref, o_ref = refs[:4]
        else:
            a_ref, b_ref, o_ref = refs[:3]
        p = lax.dot_general(a_ref[...], b_ref[...], (dims, ((), ())), preferred_element_type=f32)

        def finish(acc):
            if has_res:
                acc = acc + r_ref[...].astype(f32)
            o_ref[...] = acc.astype(o_ref.dtype)

        if nk == 1:
            finish(p)
        else:
            acc_ref = refs[-1]
            k = pl.program_id(2)

            @pl.when(k == 0)
            def _():
                acc_ref[...] = p

            @pl.when(k > 0)
            def _():
                acc_ref[...] += p

            @pl.when(k == nk - 1)
            def _():
                finish(acc_ref[...])

    in_specs = [a_spec, b_spec]
    args = [a, b]
    osz = jnp.dtype(out_dtype).itemsize
    vmem = 2 * (tm * tk * 2 + tk * tn * 2 + tm * tn * osz) + 2 * tm * tn * 4
    if has_res:
        in_specs.append(pl.BlockSpec((tm, tn), lambda i, j, k: (i, j)))
        args.append(res)
        vmem += 2 * tm * tn * res.dtype.itemsize
    scratch = []
    if nk > 1:
        scratch.append(pltpu.VMEM((tm, tn), f32))
        vmem += tm * tn * 4
    if after is not None:
        in_specs.insert(0, pl.BlockSpec(memory_space=pl.ANY))
        args.insert(0, after)
    return _call(
        body, name=name, grid=(M // tm, N // tn, nk),
        in_specs=in_specs, out_specs=pl.BlockSpec((tm, tn), lambda i, j, k: (i, j)),
        out_shape=jax.ShapeDtypeStruct((M, N), out_dtype), scratch_shapes=scratch,
        compiler_params=_params(("parallel", "parallel", "arbitrary"), vmem + (8 << 20)),
    )(*args)


EPI_TN = 256


def _mm_epi(a, bs, form, extras, out_dtypes, n_sums, fn, name):
    a_list = list(a) if isinstance(a, (list, tuple)) else [a]
    M, K = a_list[0].shape
    N = bs[0].shape[0] if form == "nt" else bs[0].shape[1]
    tn = EPI_TN
    assert K <= 2048 and N % tn == 0 and len(a_list) in (1, len(bs))
    dims = ((1,), (1,)) if form == "nt" else ((1,), (0,))
    na, nb, ne = len(a_list), len(bs), len(extras)

    def body(*refs):
        a_refs, b_refs = refs[:na], refs[na:na + nb]
        e_refs, o_refs = refs[na + nb:na + nb + ne], refs[na + nb + ne:]
        avs = [r[...] for r in a_refs] * (nb // na)
        ps = [lax.dot_general(av, b[...], (dims, ((), ())), preferred_element_type=f32) for av, b in zip(avs, b_refs)]
        for o_ref, o in zip(o_refs, fn(ps, [e[...] for e in e_refs])):
            o_ref[...] = o.astype(o_ref.dtype)

    in_specs = [pl.BlockSpec((M, K), lambda j: (0, 0), pipeline_mode=pl.Buffered(1)) for _ in a_list]
    in_specs += [pl.BlockSpec((tn, K), lambda j: (j, 0)) if form == "nt" else pl.BlockSpec((K, tn), lambda j: (0, j))
                 for _ in bs]
    for arr, first in extras:
        assert first % tn == 0
        in_specs.append(pl.BlockSpec((arr.shape[0], tn), functools.partial(lambda j, o: (0, j + o), o=first // tn)))
    out_specs = [pl.BlockSpec((M, tn), lambda j: (0, j))] * len(out_dtypes) + [pl.BlockSpec((1, tn), lambda j: (0, j))] * n_sums
    out_shape = [jax.ShapeDtypeStruct((M, N), dt) for dt in out_dtypes] + [jax.ShapeDtypeStruct((1, N), f32)] * n_sums
    tiles = sum(arr.shape[0] * tn * arr.dtype.itemsize for arr, _ in extras) + sum(M * tn * jnp.dtype(dt).itemsize for dt in out_dtypes)
    vmem = na * M * K * 2 + 2 * nb * tn * K * 2 + 2 * tiles + (nb + 6) * M * tn * 4
    return _call(body, name=name, grid=(N // tn,), in_specs=in_specs, out_specs=out_specs, out_shape=out_shape,
                 compiler_params=_params(("parallel",), vmem + (8 << 20)))(*a_list, *bs, *[arr for arr, _ in extras])


def _ffn_up(hf, wt_gate, wt_up, name):
    def fn(ps, _):
        g, u = ps[0].astype(bf16), ps[1].astype(bf16)
        gf = g.astype(f32)
        return g, u, gf * jax.nn.sigmoid(gf) * u.astype(f32)

    return _mm_epi(hf, [wt_gate, wt_up], "nt", [], [bf16] * 3, 0, fn, name)


def _ffn_down_bwd(dx2b, w_down, fg, fu, name):
    def fn(ps, es):
        d, g = ps[0], es[0].astype(f32)
        sg = jax.nn.sigmoid(g)
        return d * es[1].astype(f32) * sg * (1.0 + g * (1.0 - sg)), d * g * sg

    return _mm_epi(dx2b, [w_down], "nt", [(fg, 0), (fu, 0)], [bf16] * 2, 0, fn, name)


def _mixer_out(branches, wts, gl, b_gate, name):
    D = wts[0].shape[0]

    def fn(ps, es):
        ys = [p_.astype(bf16) for p_ in ps]
        merged = None
        for i in range(3):
            term = jax.nn.sigmoid(es[i].astype(f32) + es[3 + i]) * ys[i].astype(f32)
            merged = term if merged is None else merged + term
        return ys + [merged]

    extras = [(gl, i * D) for i in range(3)] + [(b_gate, i * D) for i in range(3)]
    return _mm_epi(branches, wts, "nt", extras, [bf16] * 4, 0, fn, name)


def _merge_bwd_fused(dx1b, w_out, gl, b_gate, ya, yc, ys, name):
    D = ya.shape[1]

    def fn(ps, es):
        dm_, outs, sums = ps[0], [], []
        for i in range(3):
            gate = jax.nn.sigmoid(es[i].astype(f32) + es[3 + i])
            dlog = dm_ * es[6 + i].astype(f32) * gate * (1.0 - gate)
            outs.append((dlog, dm_ * gate))
            sums.append(jnp.sum(dlog, axis=0, keepdims=True))
        return [o[0] for o in outs] + [o[1] for o in outs] + sums

    extras = [(gl, i * D) for i in range(3)] + [(b_gate, i * D) for i in range(3)] + [(ya, 0), (yc, 0), (ys, 0)]
    return _mm_epi(dx1b, [w_out], "nt", extras, [bf16] * 6, 3, fn, name)


def _rows(body, ins, outs, *, tm, name, vmem=40 << 20):
    nrows = next(s[1].shape[0] for s in ins if s[0] == "r")
    in_specs, args = [], []
    for s in ins:
        arr = s[1]
        if s[0] == "r":
            w = s[2] if len(s) > 2 else arr.shape[1]
            cb = s[3] if len(s) > 3 else 0
            in_specs.append(pl.BlockSpec((tm, w), functools.partial(lambda i, cb: (i, cb), cb=cb)))
        else:
            in_specs.append(pl.BlockSpec(arr.shape, functools.partial(lambda i, nd: (0,) * nd, nd=arr.ndim)))
        args.append(arr)
    out_specs, out_shape = [], []
    for s in outs:
        if s[0] == "r":
            out_specs.append(pl.BlockSpec((tm, s[1]), lambda i: (i, 0)))
            out_shape.append(jax.ShapeDtypeStruct((nrows, s[1]), s[2]))
        else:
            out_specs.append(pl.BlockSpec(s[1], functools.partial(lambda i, nd: (0,) * nd, nd=len(s[1]))))
            out_shape.append(jax.ShapeDtypeStruct(s[1], s[2]))
    return _call(body, name=name, grid=(nrows // tm,), in_specs=in_specs, out_specs=out_specs,
                 out_shape=out_shape, compiler_params=_params(("arbitrary",), vmem))(*args)


def _accumulate(ref, part):
    i = pl.program_id(0)

    @pl.when(i == 0)
    def _():
        ref[...] = part

    @pl.when(i > 0)
    def _():
        ref[...] += part


def _rms_stats(x):
    r = lax.rsqrt(jnp.mean(x * x, axis=-1, keepdims=True) + 1e-6)
    return r, x * r


def _rms_fwd(x, g, name):
    def body(x_ref, g_ref, o_ref):
        _, xn = _rms_stats(x_ref[...])
        o_ref[...] = (xn * g_ref[...]).astype(o_ref.dtype)

    return _rows(body, [("r", x), ("f", g)], [("r", x.shape[1], bf16)], tm=min(256, x.shape[0]), name=name)[0]


def _rms_bwd(x, g, dh, dres, name):
    D = x.shape[1]

    def body(x_ref, g_ref, dh_ref, dr_ref, dx_ref, dxb_ref, dg_ref):
        r, xn = _rms_stats(x_ref[...])
        dy = dh_ref[...].astype(f32)
        dxn = dy * g_ref[...]
        dx = dr_ref[...] + r * (dxn - xn * jnp.mean(dxn * xn, axis=-1, keepdims=True))
        dx_ref[...] = dx
        dxb_ref[...] = dx.astype(bf16)
        _accumulate(dg_ref, jnp.sum(dy * xn, axis=0, keepdims=True))

    return _rows(body, [("r", x), ("f", g), ("r", dh), ("r", dres)],
                 [("r", D, f32), ("r", D, bf16), ("a", (1, D), f32)], tm=min(256, x.shape[0]), name=name)


def _final_loss(x, g, tgt, name):
    D = x.shape[1]

    def body(x_ref, g_ref, t_ref, dx_ref, dxb_ref, sq_ref, dg_ref):
        r, xn = _rms_stats(x_ref[...])
        gain = g_ref[...]
        diff = xn * gain - t_ref[...]
        dy = diff * (1.0 / D)
        dxn = dy * gain
        dx = r * (dxn - xn * jnp.mean(dxn * xn, axis=-1, keepdims=True))
        dx_ref[...] = dx
        dxb_ref[...] = dx.astype(bf16)
        _accumulate(sq_ref, jnp.sum(diff * diff, axis=0, keepdims=True))
        _accumulate(dg_ref, jnp.sum(dy * xn, axis=0, keepdims=True))

    return _rows(body, [("r", x), ("f", g), ("r", tgt)],
                 [("r", D, f32), ("r", D, bf16), ("a", (1, D), f32), ("a", (1, D), f32)],
                 tm=min(256, x.shape[0]), name=name)


def _qk_fwd(q_raw, kv_raw, qg, kg, cos2, sin2, name):
    def body(q_ref, k_ref, qg_ref, kg_ref, c_ref, s_ref, qo_ref, ko_ref):
        c, s = c_ref[...], s_ref[...]

        def head(src, gain, dst, h):
            cols = slice(h * HEAD_DIM, (h + 1) * HEAD_DIM)
            _, xn = _rms_stats(src[:, cols].astype(f32))
            y = xn * gain
            dst[:, cols] = (y * c + pltpu.roll(y, HEAD_DIM // 2, 1) * s).astype(dst.dtype)

        for h in range(N_Q):
            head(q_ref, qg_ref[...], qo_ref, h)
        for h in range(N_KV):
            head(k_ref, kg_ref[...], ko_ref, h)

    return _rows(body, [("r", q_raw), ("r", kv_raw, KV_COLS, 0), ("f", qg), ("f", kg), ("r", cos2), ("r", sin2)],
                 [("r", Q_COLS, bf16), ("r", KV_COLS, bf16)], tm=min(256, q_raw.shape[0]), name=name)


def _qk_bwd(q_raw, kv_raw, dqr, dkr, qg, kg, cos2, sin2, name):
    def body(q_ref, k_ref, dq_ref, dk_ref, qg_ref, kg_ref, c_ref, s_ref, dqo_ref, dko_ref, dqg_ref, dkg_ref):
        c, s = c_ref[...], s_ref[...]

        def head(src, dsrc, gain, dst, h):
            cols = slice(h * HEAD_DIM, (h + 1) * HEAD_DIM)
            r, xn = _rms_stats(src[:, cols].astype(f32))
            do = dsrc[:, cols].astype(f32)
            dy = do * c + pltpu.roll(do * s, HEAD_DIM // 2, 1)
            dxn = dy * gain
            dst[:, cols] = (r * (dxn - xn * jnp.mean(dxn * xn, axis=-1, keepdims=True))).astype(dst.dtype)
            return jnp.sum(dy * xn, axis=0, keepdims=True)

        dq_gain = head(q_ref, dq_ref, qg_ref[...], dqo_ref, 0)
        for h in range(1, N_Q):
            dq_gain = dq_gain + head(q_ref, dq_ref, qg_ref[...], dqo_ref, h)
        dk_gain = head(k_ref, dk_ref, kg_ref[...], dko_ref, 0)
        for h in range(1, N_KV):
            dk_gain = dk_gain + head(k_ref, dk_ref, kg_ref[...], dko_ref, h)
        _accumulate(dqg_ref, dq_gain)
        _accumulate(dkg_ref, dk_gain)

    return _rows(body, [("r", q_raw), ("r", kv_raw, KV_COLS, 0), ("r", dqr), ("r", dkr), ("f", qg), ("f", kg),
                        ("r", cos2), ("r", sin2)],
                 [("r", Q_COLS, bf16), ("r", KV_COLS, bf16), ("a", (1, HEAD_DIM), f32), ("a", (1, HEAD_DIM), f32)],
                 tm=min(256, q_raw.shape[0]), name=name)


def _softmax_rows(q, k):
    s = lax.dot_general(q, k, (((1,), (1,)), ((), ())), preferred_element_type=f32) * SCALE
    p = jnp.exp(s - jnp.max(s, axis=-1, keepdims=True))
    return p * (1.0 / jnp.sum(p, axis=-1, keepdims=True))


def _head_cols(g):
    return slice(g * HEAD_DIM, (g + 1) * HEAD_DIM)


def _attn_fwd(qr, kr, kv_raw, name):
    S = qr.shape[0]
    tq = min(256, S)

    def body(q_ref, k_ref, v_ref, o_ref):
        k, v = k_ref[...], v_ref[...]
        for g in range(GRP):
            p = _softmax_rows(q_ref[:, _head_cols(g)], k)
            o_ref[:, _head_cols(g)] = jnp.dot(p.astype(bf16), v, preferred_element_type=f32).astype(o_ref.dtype)

    return _call(
        body, name=name, grid=(N_KV, S // tq),
        in_specs=[pl.BlockSpec((tq, GRP * HEAD_DIM), lambda kv, i: (i, kv)),
                  pl.BlockSpec((S, HEAD_DIM), lambda kv, i: (0, kv)),
                  pl.BlockSpec((S, HEAD_DIM), lambda kv, i: (0, N_KV + kv))],
        out_specs=pl.BlockSpec((tq, GRP * HEAD_DIM), lambda kv, i: (i, kv)),
        out_shape=jax.ShapeDtypeStruct((S, Q_COLS), bf16),
        compiler_params=_params(("parallel", "arbitrary"), 4 * GRP * tq * S * 4 + (8 << 20)),
    )(qr, kr, kv_raw)


def _attn_bwd(qr, kr, kv_raw, do, name):
    S = qr.shape[0]
    tq = min(256, S)

    def body(q_ref, k_ref, v_ref, do_ref, dq_ref, dk_ref, dv_ref):
        first = pl.program_id(1) == 0
        k, v = k_ref[...], v_ref[...]
        dv_part = dk_part = None
        for g in range(GRP):
            q, do_ = q_ref[:, _head_cols(g)], do_ref[:, _head_cols(g)]
            p = _softmax_rows(q, k)
            dp = lax.dot_general(do_, v, (((1,), (1,)), ((), ())), preferred_element_type=f32)
            ds = (p * (dp - jnp.sum(dp * p, axis=-1, keepdims=True)) * SCALE).astype(bf16)
            dq_ref[:, _head_cols(g)] = jnp.dot(ds, k, preferred_element_type=f32).astype(dq_ref.dtype)
            dv_g = lax.dot_general(p.astype(bf16), do_, (((0,), (0,)), ((), ())), preferred_element_type=f32)
            dk_g = lax.dot_general(ds, q, (((0,), (0,)), ((), ())), preferred_element_type=f32)
            dv_part = dv_g if g == 0 else dv_part + dv_g
            dk_part = dk_g if g == 0 else dk_part + dk_g

        @pl.when(first)
        def _():
            dv_ref[...] = dv_part
            dk_ref[...] = dk_part

        @pl.when(jnp.logical_not(first))
        def _():
            dv_ref[...] += dv_part
            dk_ref[...] += dk_part

    qspec = pl.BlockSpec((tq, GRP * HEAD_DIM), lambda kv, i: (i, kv))
    return _call(
        body, name=name, grid=(N_KV, S // tq),
        in_specs=[qspec, pl.BlockSpec((S, HEAD_DIM), lambda kv, i: (0, kv)),
                  pl.BlockSpec((S, HEAD_DIM), lambda kv, i: (0, N_KV + kv)), qspec],
        out_specs=[qspec, pl.BlockSpec((S, HEAD_DIM), lambda kv, i: (0, kv)),
                   pl.BlockSpec((S, HEAD_DIM), lambda kv, i: (0, kv))],
        out_shape=[jax.ShapeDtypeStruct((S, Q_COLS), bf16), jax.ShapeDtypeStruct((S, KV_COLS), f32),
                   jax.ShapeDtypeStruct((S, KV_COLS), f32)],
        compiler_params=_params(("parallel", "arbitrary"), 6 * GRP * tq * S * 4 + (8 << 20)),
    )(qr, kr, kv_raw, do)


CONV_HALO = 16


def _fill_padded(pad_ref, val, S):
    pad_ref[pl.ds(0, CONV_HALO), :] = jnp.zeros((CONV_HALO, LANES), f32)
    pad_ref[pl.ds(CONV_HALO + S, CONV_HALO), :] = jnp.zeros((CONV_HALO, LANES), f32)
    pad_ref[pl.ds(CONV_HALO, S), :] = val


def _group_specs(S, n_groups, second_half):
    return pl.BlockSpec((S, LANES), functools.partial(lambda g, o: (0, g + o), o=n_groups if second_half else 0))


def _conv1_fwd(conv_in, wdw, b_dw, name):
    S = conv_in.shape[0]
    ng = CONV_CH // LANES
    R = min(256, S)

    def body(a_ref, g_ref, w_ref, b_ref, o_ref, pad_ref):
        z = a_ref[...].astype(f32) * jax.nn.sigmoid(g_ref[...].astype(f32))
        _fill_padded(pad_ref, z, S)
        for r in range(S // R):
            acc = jnp.zeros((R, LANES), f32) + b_ref[...]
            for j in range(CONV_W):
                acc = acc + w_ref[pl.ds(j, 1), :] * pad_ref[pl.ds(r * R + CONV_HALO - CONV_PAD + j, R), :]
            o_ref[pl.ds(r * R, R), :] = acc

    return _call(
        body, name=name, grid=(ng,),
        in_specs=[_group_specs(S, ng, False), _group_specs(S, ng, True),
                  pl.BlockSpec((CONV_WP, LANES), lambda g: (g, 0)), pl.BlockSpec((1, LANES), lambda g: (0, g))],
        out_specs=pl.BlockSpec((S, LANES), lambda g: (0, g)),
        out_shape=jax.ShapeDtypeStruct((S, CONV_CH), f32),
        scratch_shapes=[pltpu.VMEM((S + 2 * CONV_HALO, LANES), f32)],
        compiler_params=_params(("parallel",), 24 << 20),
    )(conv_in, conv_in, wdw, b_dw)


def _conv1_bwd(conv_in, dc, wdw, name):
    S = conv_in.shape[0]
    ng = CONV_CH // LANES
    R = min(256, S)

    def body(a_ref, g_ref, w_ref, dc_ref, da_ref, dg_ref, dw_ref, db_ref, padz_ref, padd_ref):
        a = a_ref[...].astype(f32)
        sg = jax.nn.sigmoid(g_ref[...].astype(f32))
        _fill_padded(padz_ref, a * sg, S)
        _fill_padded(padd_ref, dc_ref[...], S)
        for r in range(S // R):
            dz = jnp.zeros((R, LANES), f32)
            for j in range(CONV_W):
                dz = dz + w_ref[pl.ds(j, 1), :] * padd_ref[pl.ds(r * R + CONV_HALO + CONV_PAD - j, R), :]
            rows = pl.ds(r * R, R)
            ar, sr = a_ref[rows, :].astype(f32), jax.nn.sigmoid(g_ref[rows, :].astype(f32))
            da_ref[rows, :] = (dz * sr).astype(da_ref.dtype)
            dg_ref[rows, :] = (dz * ar * sr * (1.0 - sr)).astype(dg_ref.dtype)
        for j in range(CONV_W):
            tot = jnp.zeros((1, LANES), f32)
            for r in range(S // R):
                tot = tot + jnp.sum(dc_ref[pl.ds(r * R, R), :] * padz_ref[pl.ds(r * R + CONV_HALO - CONV_PAD + j, R), :],
                                    axis=0, keepdims=True)
            dw_ref[pl.ds(j, 1), :] = tot
        dw_ref[pl.ds(CONV_W, CONV_WP - CONV_W), :] = jnp.zeros((CONV_WP - CONV_W, LANES), f32)
        db_ref[...] = jnp.sum(dc_ref[...], axis=0, keepdims=True)

    return _call(
        body, name=name, grid=(ng,),
        in_specs=[_group_specs(S, ng, False), _group_specs(S, ng, True),
                  pl.BlockSpec((CONV_WP, LANES), lambda g: (g, 0)), pl.BlockSpec((S, LANES), lambda g: (0, g))],
        out_specs=[pl.BlockSpec((S, LANES), lambda g: (0, g)), pl.BlockSpec((S, LANES), lambda g: (0, g)),
                   pl.BlockSpec((CONV_WP, LANES), lambda g: (g, 0)), pl.BlockSpec((1, LANES), lambda g: (0, g))],
        out_shape=[jax.ShapeDtypeStruct((S, CONV_CH), bf16), jax.ShapeDtypeStruct((S, CONV_CH), bf16),
                   jax.ShapeDtypeStruct((ng * CONV_WP, LANES), f32), jax.ShapeDtypeStruct((1, CONV_CH), f32)],
        scratch_shapes=[pltpu.VMEM((S + 2 * CONV_HALO, LANES), f32), pltpu.VMEM((S + 2 * CONV_HALO, LANES), f32)],
        compiler_params=_params(("parallel",), 24 << 20),
    )(conv_in, conv_in, wdw, dc)


def _ln_stats(x, eps=1e-5):
    xc = x - jnp.mean(x, axis=-1, keepdims=True)
    r = lax.rsqrt(jnp.mean(xc * xc, axis=-1, keepdims=True) + eps)
    return r, xc * r


def _ln_bwd(r, xh, dxh):
    return r * (dxh - jnp.mean(dxh, axis=-1, keepdims=True) - xh * jnp.mean(dxh * xh, axis=-1, keepdims=True))


def _conv2_fwd(c, ln_g, ln_b, name):
    def body(c_ref, g_ref, b_ref, o_ref):
        _, xh = _ln_stats(c_ref[...])
        y = xh * g_ref[...] + b_ref[...]
        o_ref[...] = (y * jax.nn.sigmoid(y)).astype(o_ref.dtype)

    return _rows(body, [("r", c), ("f", ln_g), ("f", ln_b)], [("r", CONV_CH, bf16)], tm=min(256, c.shape[0]), name=name)[0]


def _conv2_bwd(c, dcz, ln_g, ln_b, name):
    def body(c_ref, d_ref, g_ref, b_ref, dc_ref, dg_ref, db_ref):
        r, xh = _ln_stats(c_ref[...])
        y = xh * g_ref[...] + b_ref[...]
        sg = jax.nn.sigmoid(y)
        dy = d_ref[...].astype(f32) * (sg * (1.0 + y * (1.0 - sg)))
        dc_ref[...] = _ln_bwd(r, xh, dy * g_ref[...])
        _accumulate(dg_ref, jnp.sum(dy * xh, axis=0, keepdims=True))
        _accumulate(db_ref, jnp.sum(dy, axis=0, keepdims=True))

    return _rows(body, [("r", c), ("r", dcz), ("f", ln_g), ("f", ln_b)],
                 [("r", CONV_CH, f32), ("a", (1, CONV_CH), f32), ("a", (1, CONV_CH), f32)],
                 tm=min(256, c.shape[0]), name=name)


GELU_K = math.sqrt(2.0 / math.pi)
GELU_C = 0.044715


def _gelu(x):
    return 0.5 * x * (1.0 + jnp.tanh(GELU_K * (x + GELU_C * x * x * x)))


def _gelu_and_grad(x):
    x2 = x * x
    th = jnp.tanh(GELU_K * (x + GELU_C * x2 * x))
    half = 0.5 * (1.0 + th)
    return x * half, half + 0.5 * x * (1.0 - th * th) * (GELU_K * (1.0 + 3.0 * GELU_C * x2))


def _chunk_rows(n):
    return pl.ds(pl.multiple_of(n * SG_CHUNK, SG_CHUNK), SG_CHUNK)


def _sgu_fwd(sg_in, ln_g, ln_b, w_s, b_s, name):
    S = sg_in.shape[0]

    def body(u_ref, v_ref, lg_ref, lb_ref, w_ref, b_ref, o_ref):
        wb = w_ref[...].astype(bf16)

        def chunk(n, carry):
            rows = _chunk_rows(n)
            gu = _gelu(u_ref[rows, :].astype(f32))
            _, xh = _ln_stats(_gelu(v_ref[rows, :].astype(f32)))
            vl = xh * lg_ref[...] + lb_ref[...]
            t = jnp.dot(wb, vl.astype(bf16), preferred_element_type=f32) + b_ref[...]
            o_ref[rows, :] = (gu * t).astype(o_ref.dtype)
            return carry

        lax.fori_loop(0, S // SG_CHUNK, chunk, 0, unroll=2)

    return _call(
        body, name=name, grid=(SG_G,),
        in_specs=[_group_specs(S, SG_G, False), _group_specs(S, SG_G, True),
                  pl.BlockSpec((1, LANES), lambda g: (0, g)), pl.BlockSpec((1, LANES), lambda g: (0, g)),
                  pl.BlockSpec((None, SG_CHUNK, SG_CHUNK), lambda g: (g, 0, 0)),
                  pl.BlockSpec((None, SG_CHUNK, 1), lambda g: (g, 0, 0))],
        out_specs=pl.BlockSpec((S, LANES), lambda g: (0, g)),
        out_shape=jax.ShapeDtypeStruct((S, SG_CH), bf16),
        compiler_params=_params(("parallel",), 24 << 20),
    )(sg_in, sg_in, ln_g, ln_b, w_s, b_s)


def _sgu_bwd(sg_in, dsz, ln_g, ln_b, w_s, w_s_t, b_s, name):
    S = sg_in.shape[0]

    def body(u_ref, v_ref, lg_ref, lb_ref, w_ref, wt_ref, b_ref, d_ref, du_ref, dv_ref, dw_ref, db_ref, dlg_ref, dlb_ref):
        wb = w_ref[...].astype(bf16)
        wtb = wt_ref[...].astype(bf16)

        def chunk(n, carry):
            dwa, dba, dlga, dlba = carry
            rows = _chunk_rows(n)
            u = u_ref[rows, :].astype(f32)
            v = v_ref[rows, :].astype(f32)
            gu, gu_grad = _gelu_and_grad(u)
            gv, gv_grad = _gelu_and_grad(v)
            r, xh = _ln_stats(gv)
            vlb = (xh * lg_ref[...] + lb_ref[...]).astype(bf16)
            t = jnp.dot(wb, vlb, preferred_element_type=f32) + b_ref[...]
            d = d_ref[rows, :].astype(f32)
            dt = d * gu
            dtb = dt.astype(bf16)
            dwa = dwa + lax.dot_general(dtb, vlb, (((1,), (1,)), ((), ())), preferred_element_type=f32)
            dba = dba + jnp.sum(dt, axis=1, keepdims=True)
            dvl = jnp.dot(wtb, dtb, preferred_element_type=f32)
            dlga = dlga + jnp.sum(dvl * xh, axis=0, keepdims=True)
            dlba = dlba + jnp.sum(dvl, axis=0, keepdims=True)
            dgv = _ln_bwd(r, xh, dvl * lg_ref[...])
            du_ref[rows, :] = (d * t * gu_grad).astype(du_ref.dtype)
            dv_ref[rows, :] = (dgv * gv_grad).astype(dv_ref.dtype)
            return dwa, dba, dlga, dlba

        init = (jnp.zeros((SG_CHUNK, SG_CHUNK), f32), jnp.zeros((SG_CHUNK, 1), f32),
                jnp.zeros((1, LANES), f32), jnp.zeros((1, LANES), f32))
        dwa, dba, dlga, dlba = lax.fori_loop(0, S // SG_CHUNK, chunk, init, unroll=2)
        dw_ref[...] = dwa
        db_ref[...] = dba
        dlg_ref[...] = dlga
        dlb_ref[...] = dlba

    wspec = pl.BlockSpec((None, SG_CHUNK, SG_CHUNK), lambda g: (g, 0, 0))
    bspec = pl.BlockSpec((None, SG_CHUNK, 1), lambda g: (g, 0, 0))
    lspec = pl.BlockSpec((1, LANES), lambda g: (0, g))
    cspec = pl.BlockSpec((S, LANES), lambda g: (0, g))
    return _call(
        body, name=name, grid=(SG_G,),
        in_specs=[_group_specs(S, SG_G, False), _group_specs(S, SG_G, True), lspec, lspec, wspec, wspec, bspec, cspec],
        out_specs=[cspec, cspec, wspec, bspec, lspec, lspec],
        out_shape=[jax.ShapeDtypeStruct((S, SG_CH), bf16), jax.ShapeDtypeStruct((S, SG_CH), bf16),
                   jax.ShapeDtypeStruct((SG_G, SG_CHUNK, SG_CHUNK), f32), jax.ShapeDtypeStruct((SG_G, SG_CHUNK, 1), f32),
                   jax.ShapeDtypeStruct((1, SG_CH), f32), jax.ShapeDtypeStruct((1, SG_CH), f32)],
        compiler_params=_params(("parallel",), 24 << 20),
    )(sg_in, sg_in, ln_g, ln_b, w_s, w_s_t, b_s, dsz)


def _row_tile(r, c, n_arrays, itemsize=4):
    fits = [tm for tm in range(16, r + 1, 16) if r % tm == 0 and 2 * n_arrays * tm * c * itemsize <= (24 << 20)]
    return fits[-1] if fits else r


def _sum_slots(slots, name):
    n, r, c = slots.shape
    tm = _row_tile(r, c, n + 2)

    def body(s_ref, o_ref):
        acc = s_ref[0].astype(f32)
        for k in range(1, n):
            acc = acc + s_ref[k].astype(f32)
        o_ref[...] = acc

    return _call(body, name=name, grid=(r // tm,),
                 in_specs=[pl.BlockSpec((n, tm, c), lambda i: (0, i, 0))],
                 out_specs=pl.BlockSpec((tm, c), lambda i: (i, 0)),
                 out_shape=jax.ShapeDtypeStruct((r, c), f32),
                 compiler_params=_params(("parallel",), 40 << 20))(slots)


def _add_sibling(g4, recv, core, name):
    _, _, r, c = g4.shape
    tm = _row_tile(r, c, 3, 2)

    def body(core_ref, g_ref, r_ref, o_ref):
        o_ref[...] = (g_ref[...].astype(f32) + r_ref[...].astype(f32)).astype(o_ref.dtype)

    grid_spec = pltpu.PrefetchScalarGridSpec(
        num_scalar_prefetch=1, grid=(N_CHIP, r // tm),
        in_specs=[pl.BlockSpec((None, None, tm, c), lambda k, i, core_ref: (k, core_ref[0], i, 0)),
                  pl.BlockSpec((None, tm, c), lambda k, i, core_ref: (k, i, 0))],
        out_specs=pl.BlockSpec((None, tm, c), lambda k, i, core_ref: (k, i, 0)))
    return _call(body, name=name, grid_spec=grid_spec, out_shape=jax.ShapeDtypeStruct((N_CHIP, r, c), bf16),
                 compiler_params=_params(("parallel", "parallel"), 40 << 20))(core, g4, recv)


def _adamw(w, g, m, v, name):
    L, r, c = w.shape
    tm = _row_tile(r, c, 7)
    c1 = 1.0 - ADAM_B1 ** ADAM_STEP
    c2 = 1.0 - ADAM_B2 ** ADAM_STEP

    def body(w_ref, g_ref, m_ref, v_ref, d_ref, mo_ref, vo_ref):
        g_ = g_ref[...]
        m_ = ADAM_B1 * m_ref[...] + (1.0 - ADAM_B1) * g_
        v_ = ADAM_B2 * v_ref[...] + (1.0 - ADAM_B2) * (g_ * g_)
        d_ref[...] = -ADAM_LR * ((m_ / c1) / (jnp.sqrt(v_ / c2) + ADAM_EPS) + ADAM_WD * w_ref[...])
        mo_ref[...] = m_
        vo_ref[...] = v_

    spec = pl.BlockSpec((None, tm, c), lambda l, i: (l, i, 0))
    shp = jax.ShapeDtypeStruct((L, r, c), f32)
    return _call(body, name=name, grid=(L, r // tm), in_specs=[spec] * 4, out_specs=[spec] * 3,
                 out_shape=[shp] * 3, compiler_params=_params(("parallel", "parallel"), 40 << 20))(w, g, m, v)


def _mesh_pos():
    return lax.axis_index("x"), lax.axis_index("y"), lax.axis_index("c")


SEM = pl.BlockSpec(memory_space=pltpu.SEMAPHORE)
ANY = pl.BlockSpec(memory_space=pl.ANY)
EFFECT = pltpu.SideEffectType.DATAFLOW_SIDE_EFFECTING


def _other_chips(x, y):
    return [(1 - x, y), (x, 1 - y), (1 - x, 1 - y)]


def _peers(kind, x, y):
    return [(x, y)] if kind == "sibling" else _other_chips(x, y)


def _ici_copy(kind, src_ref, land_ref, send_sem, recv_sem, sender, target, c):
    (sx, sy), (tx, ty) = sender, target
    if kind == "sibling":
        return pltpu.make_async_remote_copy(src_ref=src_ref.at[:, 1 - c], dst_ref=land_ref, send_sem=send_sem,
                                            recv_sem=recv_sem, device_id=(tx, ty, 1 - c), device_id_type=MESH)
    if kind == "gather":
        src, dst = src_ref, land_ref.at[4 * sx + 2 * sy + c]
    else:
        src, dst = src_ref.at[2 * tx + ty], land_ref.at[2 * sx + sy]
    return pltpu.make_async_remote_copy(src_ref=src, dst_ref=dst, send_sem=send_sem, recv_sem=recv_sem,
                                        device_id=(tx, ty, c), device_id_type=MESH)


def _ici_start(kind, srcs, lands, after, name):
    n = len(srcs)
    npeer = 1 if kind == "sibling" else 3

    def body(*refs):
        src_refs, land_refs = refs[:n], refs[n:2 * n]
        send_sems, recv_sems = refs[2 * n + 1], refs[2 * n + 2]
        token = refs[-1]
        x, y, c = _mesh_pos()
        for j, chip in enumerate(_peers(kind, x, y)):
            for k in range(n):
                _ici_copy(kind, src_refs[k], land_refs[k], send_sems.at[npeer * k + j], recv_sems.at[npeer * k + j],(x, y), chip, c).start()
        token[...] = jnp.zeros_like(token)

    bufs = list(srcs) + list(lands)
    return _call(
        body, name=name,
        out_shape=(pltpu.SemaphoreType.DMA((npeer * n,)), pltpu.SemaphoreType.DMA((npeer * n,)),
                   *[pltpu.HBM(b.shape, b.dtype) for b in bufs], jax.ShapeDtypeStruct((8, LANES), f32)),
        in_specs=[HBM] * (2 * n) + [ANY], out_specs=(SEM, SEM, *[HBM] * (2 * n), pl.BlockSpec(memory_space=pltpu.VMEM)),
        input_output_aliases={i: 2 + i for i in range(2 * n)},
        compiler_params=pltpu.CompilerParams(has_side_effects=EFFECT),
    )(*[pltpu.with_memory_space_constraint(b, pltpu.HBM) for b in bufs], after)


def _ici_wait(kind, started, after, name):
    send_sems, recv_sems, *bufs = started[:-1]
    n = len(bufs) // 2
    npeer = 1 if kind == "sibling" else 3

    def body(*refs):
        src_refs, land_refs = refs[:n], refs[n:2 * n]
        send_sems, recv_sems = refs[2 * n], refs[2 * n + 1]
        x, y, c = _mesh_pos()
        for j, chip in enumerate(_peers(kind, x, y)):
            for k in range(n):
                _ici_copy(kind, src_refs[k], land_refs[k], send_sems.at[npeer * k + j], recv_sems.at[npeer * k + j],(x, y), chip, c).wait_send()
                _ici_copy(kind, src_refs[k], land_refs[k], send_sems.at[npeer * k + j], recv_sems.at[npeer * k + j],chip, (x, y), c).wait_recv()

    out = _call(
        body, name=name, out_shape=[pltpu.HBM(b.shape, b.dtype) for b in bufs],
        in_specs=[HBM] * (2 * n) + [SEM, SEM, ANY], out_specs=[HBM] * (2 * n),
        input_output_aliases={i: i for i in range(2 * n)},
        compiler_params=pltpu.CompilerParams(has_side_effects=EFFECT),
    )(*bufs, send_sems, recv_sems, after)
    return out[:n], out[n:]


def _d2d_gather(lands, after, name):
    n = len(lands)

    def body(*refs):
        in_refs, o_refs = refs[:n], refs[n + 1:2 * n + 1]
        send_sems, recv_sems = refs[2 * n + 1:]
        x, y, c = _mesh_pos()
        copies = [pltpu.make_async_remote_copy(
            src_ref=in_refs[k].at[:, c], dst_ref=o_refs[k].at[:, c], send_sem=send_sems.at[k], recv_sem=recv_sems.at[k],
            device_id=(x, y, 1 - c), device_id_type=MESH) for k in range(n)]
        for cp in copies:
            cp.start()
        for k, cp in enumerate(copies):
            cp.wait_send()
            pltpu.make_async_remote_copy(
                src_ref=in_refs[k].at[:, c], dst_ref=o_refs[k].at[:, 1 - c], send_sem=send_sems.at[k],
                recv_sem=recv_sems.at[k], device_id=(x, y, 1 - c), device_id_type=MESH).wait_recv()

    return _call(
        body, name=name, in_specs=[HBM] * n + [ANY], out_specs=[HBM] * n,
        out_shape=[jax.ShapeDtypeStruct(b.shape, b.dtype) for b in lands],
        input_output_aliases={k: k for k in range(n)},
        scratch_shapes=[pltpu.SemaphoreType.DMA((n,)), pltpu.SemaphoreType.DMA((n,))],
    )(*lands, after)


def _sum_chip_slots(lands, sums, chip, name):
    _, r, c = lands.shape
    tm = _row_tile(r, c, 10, 2)

    def body(chip_ref, l_ref, s_ref, o_ref):
        acc = None
        for k in range(N_CHIP):
            part = jnp.where(chip_ref[0] == k, s_ref[k], l_ref[k]).astype(f32)
            acc = part if acc is None else acc + part
        o_ref[...] = acc

    grid_spec = pltpu.PrefetchScalarGridSpec(
        num_scalar_prefetch=1, grid=(r // tm,),
        in_specs=[pl.BlockSpec((N_CHIP, tm, c), lambda i, chip_ref: (0, i, 0)),
                  pl.BlockSpec((N_CHIP, tm, c), lambda i, chip_ref: (0, i, 0))],
        out_specs=pl.BlockSpec((tm, c), lambda i, chip_ref: (i, 0)))
    return _call(body, name=name, grid_spec=grid_spec, out_shape=jax.ShapeDtypeStruct((r, c), f32),
                 compiler_params=_params(("parallel",), 40 << 20))(chip, lands, sums)


def _reduce_begin(grads, core, tag):
    g4s = [g.reshape(N_CHIP, 2, g.shape[0] // N_DEV, g.shape[1]) for g in grads]
    recvs = [lax.empty((N_CHIP,) + g.shape[2:], g.dtype) for g in g4s]
    return _ici_start("sibling", g4s, recvs, core, name="rs_d2d_start_" + tag)


def _reduce_continue(begun, core, after, tag):
    g4s, recvs = _ici_wait("sibling", begun, after, name="rs_d2d_wait_" + tag)
    sums = [_add_sibling(g4, rv, core, name="rs_add_" + tag) for g4, rv in zip(g4s, recvs)]
    lands = [lax.empty(s.shape, s.dtype) for s in sums]
    return _ici_start("reduce", sums, lands, core, name="rs_start_" + tag)


def _adamw_reduced(layer, w, m, v, lands, sums, chip, prev, name):
    L, r, c = w.shape
    tm = _row_tile(r, c, 11)
    c1 = 1.0 - ADAM_B1 ** ADAM_STEP
    c2 = 1.0 - ADAM_B2 ** ADAM_STEP
    n_prev = 0 if prev is None else 4

    def body(chip_ref, w_ref, m_ref, v_ref, l_ref, s_ref, *refs):
        g_ref, d_ref, mo_ref, vo_ref = refs[n_prev:]
        g_ = None
        for k in range(N_CHIP):
            part = jnp.where(chip_ref[0] == k, s_ref[k], l_ref[k]).astype(f32)
            g_ = part if g_ is None else g_ + part
        m_ = ADAM_B1 * m_ref[...] + (1.0 - ADAM_B1) * g_
        v_ = ADAM_B2 * v_ref[...] + (1.0 - ADAM_B2) * (g_ * g_)
        g_ref[...] = g_
        d_ref[...] = -ADAM_LR * ((m_ / c1) / (jnp.sqrt(v_ / c2) + ADAM_EPS) + ADAM_WD * w_ref[...])
        mo_ref[...] = m_
        vo_ref[...] = v_

    wspec = pl.BlockSpec((None, tm, c), lambda i, chip_ref: (layer, i, 0))
    sspec = pl.BlockSpec((N_CHIP, tm, c), lambda i, chip_ref: (0, i, 0))
    grid_spec = pltpu.PrefetchScalarGridSpec(
        num_scalar_prefetch=1, grid=(r // tm,), in_specs=[wspec] * 3 + [sspec] * 2 + [ANY] * n_prev, out_specs=[wspec] * 4)
    return _call(body, name=name, grid_spec=grid_spec, out_shape=[jax.ShapeDtypeStruct((L, r, c), f32)] * 4,
                 input_output_aliases={6 + i: i for i in range(n_prev)},
                 compiler_params=_params(("parallel",), 40 << 20))(chip, w, m, v, lands, sums, *(prev or ()))


def _rope_tables(S):
    rows = S // GRID_W
    row = jnp.repeat(jnp.arange(rows, dtype=f32), GRID_W)
    col = jnp.tile(jnp.arange(GRID_W, dtype=f32), rows)
    nf = HEAD_DIM // 4
    inv = ROPE_THETA ** (-jnp.arange(nf, dtype=f32) / nf)
    ang = jnp.concatenate([row[:, None] * inv, col[:, None] * inv], axis=-1)
    cos, sin = jnp.cos(ang), jnp.sin(ang)
    return jnp.concatenate([cos, cos], axis=-1), jnp.concatenate([-sin, sin], axis=-1)


def _layer_fwd(xin, p, w, more_weights, cos2, sin2):
    sv = {"xin": xin}
    h = sv["h"] = _rms_fwd(xin, p["g_mix"], name="rms_mix")
    proj = functools.partial(_mm, h, w["in"], "nt", bf16)
    q_raw = sv["q_raw"] = proj(n=Q_COLS, b_off=0, name="proj_q")
    kv_raw = sv["kv_raw"] = proj(n=2 * KV_COLS, b_off=OFF_KV, name="proj_kv")
    conv_in = sv["conv_in"] = proj(n=2 * CONV_CH, b_off=OFF_CONV, name="proj_conv")
    sg_in = sv["sg_in"] = proj(n=2 * SG_CH, b_off=OFF_SG, name="proj_sg")
    gl = sv["gl"] = proj(n=3 * D_MODEL, b_off=OFF_GATE, name="proj_gate")
    qr, kr = sv["qr"], sv["kr"] = _qk_fwd(q_raw, kv_raw, p["q_norm_g"], p["k_norm_g"], cos2, sin2, name="qk_fwd")
    o = sv["o"] = _attn_fwd(qr, kr, kv_raw, name="attn_fwd")
    c = sv["c"] = _conv1_fwd(conv_in, w["dw"], p["b_dw"], name="conv1_fwd")
    cz = sv["cz"] = _conv2_fwd(c, p["conv_ln_g"], p["conv_ln_b"], name="conv2_fwd")
    sz = sv["sz"] = _sgu_fwd(sg_in, p["sg_ln_g"], p["sg_ln_b"], p["w_s"], p["b_s"], name="sgu_fwd")
    w = {**w, **more_weights(1, sz)}
    sv["ya"], sv["yc"], sv["ys"], merged = _mixer_out([o, cz, sz], [w["attn_o"], w["conv_o"], w["sg_o"]], gl, p["b_gate"],
                                                      name="mixer_out")
    sv["merged"] = merged
    x1 = sv["x1"] = _mm(merged, w["out"], "nn", f32, res=xin, name="out_proj")
    w = {**w, **more_weights(2, x1)}
    hf = sv["hf"] = _rms_fwd(x1, p["g_ffn"], name="rms_ffn")
    sv["fg"], sv["fu"], act = _ffn_up(hf, w["ff_gate"], w["ff_up"], name="ffn_up")
    sv["act"] = act
    x2 = _mm(act, w["ff_down"], "nn", f32, res=x1, name="ff_down")
    return x2, sv, w


def _layer_bwd(dx2, dx2b, sv, p, w, cos2, sin2, reduce_begin, reduce_continue, last):
    small = {}
    dfg, dfu = _ffn_down_bwd(dx2b, w["ff_down"], sv["fg"], sv["fu"], name="ffn_down_bwd")
    g_down = _mm(sv["act"], dx2b, "tn", bf16, name="g_ff_down")
    dhf = _mm(dfg, w["ff_gate"], "nn", f32, name="d_hf_gate")
    dhf = _mm(dfu, w["ff_up"], "nn", f32, res=dhf, name="d_hf_up")
    g_gate = _mm(dfg, sv["hf"], "tn", bf16, name="g_ff_gate")
    g_up = _mm(dfu, sv["hf"], "tn", bf16, name="g_ff_up")
    zero = reduce_begin("ffn", dict(w_ff_gate=g_gate, w_ff_up=g_up, w_ff_down=g_down))[0, 0]
    dx1, dx1b, small["g_ffn"] = _rms_bwd(sv["x1"], p["g_ffn"] + zero, dhf, dx2, name="rms_ffn_bwd")
    g_out = _mm(sv["merged"], dx1b, "tn", bf16, name="g_out")
    *dgl, dya, dyc, dys, db0, db1, db2 = _merge_bwd_fused(dx1b, w["out"], sv["gl"], p["b_gate"], sv["ya"], sv["yc"], sv["ys"],
                                                        name="merge_bwd")
    small["b_gate"] = jnp.concatenate([db0, db1, db2], axis=1)
    do = _mm(dya, w["attn_o"], "nn", bf16, after=reduce_continue("ffn", dya), name="d_o")
    g_ao = _mm(dya, sv["o"], "tn", bf16, name="g_attn_o")
    dcz = _mm(dyc, w["conv_o"], "nn", bf16, name="d_cz")
    g_co = _mm(dyc, sv["cz"], "tn", bf16, name="g_conv_o")
    dsz = _mm(dys, w["sg_o"], "nn", bf16, name="d_sz")
    g_so = _mm(dys, sv["sz"], "tn", bf16, name="g_sg_o")
    zero = reduce_begin("mix", dict(w_attn_o=g_ao, w_conv_o=g_co, w_sg_o=g_so, w_out=g_out))[0, 0]
    dsu, dsv, small["w_s"], small["b_s"], small["sg_ln_g"], small["sg_ln_b"] = _sgu_bwd(
        sv["sg_in"], dsz, p["sg_ln_g"] + zero, p["sg_ln_b"], p["w_s"], p["w_s_t"], p["b_s"], name="sgu_bwd")
    dc, small["conv_ln_g"], small["conv_ln_b"] = _conv2_bwd(sv["c"], dcz, p["conv_ln_g"], p["conv_ln_b"], name="conv2_bwd")
    da, dgt, small["w_dw"], small["b_dw"] = _conv1_bwd(sv["conv_in"], dc, w["dw"], name="conv1_bwd")
    zero = reduce_continue("mix", da)[0, 0]
    dqr, dkr, dv = _attn_bwd(sv["qr"], sv["kr"], sv["kv_raw"], do, name="attn_bwd")
    dq_raw, dk_raw, small["q_norm_g"], small["k_norm_g"] = _qk_bwd(
        sv["q_raw"], sv["kv_raw"], dqr, dkr, p["q_norm_g"] + zero, p["k_norm_g"], cos2, sin2, name="qk_bwd")
    dproj = jnp.concatenate([dq_raw, dk_raw, dv.astype(bf16), da, dgt, dsu, dsv, *dgl], axis=1)
    g_in = _mm(dproj, sv["h"], "tn", bf16, name="g_in")
    begun = reduce_begin("in", dict(w_in=g_in))
    if last:
        begun = reduce_continue("in", begun)
    dh = _mm(dproj, w["in"], "nn", f32, after=begun, name="d_h")
    zero = begun[0, 0] if last else reduce_continue("in", dh)[0, 0]
    dx, dxb, small["g_mix"] = _rms_bwd(sv["xin"], p["g_mix"] + zero, dh, dx1, name="rms_mix_bwd")
    return dx, dxb, small


SMALL = ("g_mix", "b_gate", "q_norm_g", "k_norm_g", "b_dw", "conv_ln_g", "conv_ln_b", "sg_ln_g", "sg_ln_b",
         "w_s", "b_s", "g_ffn")
PACK_ALIGN = 8 * LANES


def _pack(parts):
    flat = jnp.concatenate([a.reshape(-1).astype(f32) for a in parts])
    pad = -flat.shape[0] % PACK_ALIGN
    return jnp.pad(flat, (0, pad)).reshape(-1, LANES)


def _unpack(buf, shapes):
    flat = buf.reshape(-1)
    out, pos = [], 0
    for shp in shapes:
        size = math.prod(shp)
        out.append(flat[pos:pos + size].reshape(shp))
        pos += size
    return out


def kernel(x, g_mix, w_in, b_gate, q_norm_g, k_norm_g, w_attn_o, w_dw, b_dw, conv_ln_g, conv_ln_b, w_conv_o, sg_ln_g, sg_ln_b, w_s, b_s, w_sg_o, w_out, g_ffn, w_ff_gate, w_ff_up, w_ff_down, g_final, loss_target, m_g_mix, m_w_in, m_b_gate, m_q_norm_g, m_k_norm_g, m_w_attn_o, m_w_dw, m_b_dw, m_conv_ln_g, m_conv_ln_b, m_w_conv_o, m_sg_ln_g, m_sg_ln_b, m_w_s, m_b_s, m_w_sg_o, m_w_out, m_g_ffn, m_w_ff_gate, m_w_ff_up, m_w_ff_down, m_g_final, v_g_mix, v_w_in, v_b_gate, v_q_norm_g, v_k_norm_g, v_w_attn_o, v_w_dw, v_b_dw, v_conv_ln_g, v_conv_ln_b, v_w_conv_o, v_sg_ln_g, v_sg_ln_b, v_w_s, v_b_s, v_w_sg_o, v_w_out, v_g_ffn, v_w_ff_gate, v_w_ff_up, v_w_ff_down, v_g_final):
    weights = dict(g_mix=g_mix, w_in=w_in, b_gate=b_gate, q_norm_g=q_norm_g, k_norm_g=k_norm_g, w_attn_o=w_attn_o,
                   w_dw=w_dw, b_dw=b_dw, conv_ln_g=conv_ln_g, conv_ln_b=conv_ln_b, w_conv_o=w_conv_o, sg_ln_g=sg_ln_g,
                   sg_ln_b=sg_ln_b, w_s=w_s, b_s=b_s, w_sg_o=w_sg_o, w_out=w_out, g_ffn=g_ffn, w_ff_gate=w_ff_gate,
                   w_ff_up=w_ff_up, w_ff_down=w_ff_down, g_final=g_final)
    mom_m = dict(g_mix=m_g_mix, w_in=m_w_in, b_gate=m_b_gate, q_norm_g=m_q_norm_g, k_norm_g=m_k_norm_g,
                 w_attn_o=m_w_attn_o, w_dw=m_w_dw, b_dw=m_b_dw, conv_ln_g=m_conv_ln_g, conv_ln_b=m_conv_ln_b,
                 w_conv_o=m_w_conv_o, sg_ln_g=m_sg_ln_g, sg_ln_b=m_sg_ln_b, w_s=m_w_s, b_s=m_b_s, w_sg_o=m_w_sg_o,
                 w_out=m_w_out, g_ffn=m_g_ffn, w_ff_gate=m_w_ff_gate, w_ff_up=m_w_ff_up, w_ff_down=m_w_ff_down,
                 g_final=m_g_final)
    mom_v = dict(g_mix=v_g_mix, w_in=v_w_in, b_gate=v_b_gate, q_norm_g=v_q_norm_g, k_norm_g=v_k_norm_g,
                 w_attn_o=v_w_attn_o, w_dw=v_w_dw, b_dw=v_b_dw, conv_ln_g=v_conv_ln_g, conv_ln_b=v_conv_ln_b,
                 w_conv_o=v_w_conv_o, sg_ln_g=v_sg_ln_g, sg_ln_b=v_sg_ln_b, w_s=v_w_s, b_s=v_b_s, w_sg_o=v_w_sg_o,
                 w_out=v_w_out, g_ffn=v_g_ffn, w_ff_gate=v_w_ff_gate, w_ff_up=v_w_ff_up, w_ff_down=v_w_ff_down,
                 g_final=v_g_final)
    S, D = x.shape[1], x.shape[2]
    xi, yi, ci = _mesh_pos()
    me = 4 * xi + 2 * yi + ci
    core = jnp.reshape(ci, (1,)).astype(jnp.int32)
    cos2, sin2 = _rope_tables(S)

    big = ("w_in", "w_attn_o", "w_conv_o", "w_sg_o", "w_out", "w_ff_gate", "w_ff_up", "w_ff_down")
    transposed = {"w_in", "w_attn_o", "w_conv_o", "w_sg_o", "w_ff_gate", "w_ff_up"}
    chip = jnp.reshape(2 * xi + yi, (1,)).astype(jnp.int32)
    groups = (("in", "dw"), ("attn_o", "conv_o", "sg_o", "out"), ("ff_gate", "ff_up", "ff_down"))
    P, shards = [], []
    for l in range(DEPTH):
        sh = {n[2:]: (weights[n][l].T if n in transposed else weights[n][l]).astype(bf16) for n in big}
        sh["dw"] = jnp.pad(w_dw[l].reshape(CONV_W, LANES), ((0, CONV_WP - CONV_W), (0, 0)))
        shards.append(sh)
        p = {n: weights[n][l].reshape(1, -1) for n in SMALL if n not in ("w_s", "b_s")}
        p["w_s"] = w_s[l]
        p["w_s_t"] = jnp.swapaxes(w_s[l], 1, 2)
        p["b_s"] = b_s[l].reshape(SG_G, SG_CHUNK, 1)
        P.append(p)

    gathers = {}

    def start_gather(l, gi, after):
        srcs = [shards[l][n] for n in groups[gi]]
        lands = [lax.dynamic_update_index_in_dim(lax.empty((N_DEV,) + s.shape, s.dtype), s, me, 0) for s in srcs]
        gathers[l, gi] = _ici_start("gather", srcs, lands, after, name=f"ag_start_{l}{gi}")
        return gathers[l, gi][-1]

    def gathered(l, gi, after):
        srcs, lands = _ici_wait("gather", gathers[l, gi], after, name=f"ag_wait_{l}{gi}")
        after = srcs[0]
        if gi == len(groups) - 1 and l + 1 < DEPTH:
            for gj in range(len(groups)):
                after = start_gather(l + 1, gj, after)
        full = _d2d_gather([b.reshape(N_CHIP, 2, *b.shape[1:]) for b in lands], after, name=f"ag_d2d_{gi}")
        return {n: f.reshape(-1, f.shape[3]) for n, f in zip(groups[gi], full)}

    all_started = cos2
    for gi in range(len(groups)):
        all_started = start_gather(0, gi, all_started)

    h = x.reshape(S, D)
    saved, W = [], []
    for l in range(DEPTH):
        first = gathered(l, 0, all_started if l == 0 else h)
        if l == 0:
            P[l]["g_mix"] = P[l]["g_mix"] + all_started[0, 0]
        h, sv, w = _layer_fwd(h, P[l], first, functools.partial(lambda gi, z, l: gathered(l, gi, z), l=l), cos2, sin2)
        saved.append(sv)
        W.append(w)
    dx, dxb, sq, g_final_part = _final_loss(h, g_final.reshape(1, D), loss_target.reshape(S, D), name="final_loss")
    loss = lax.psum(0.5 * jnp.sum(sq) / D, ("x", "y", "c"))

    begun, reductions, small_grads = {}, {}, [None] * DEPTH
    for l in reversed(range(DEPTH)):
        def reduce_begin(group, grads, l=l):
            begun[l, group] = (tuple(grads), _reduce_begin(list(grads.values()), core, tag=f"{group}{l}"))
            return begun[l, group][1][-1]

        def reduce_continue(group, after, l=l):
            names, started = begun[l, group]
            reductions[l, group] = (names, _reduce_continue(started, core, after, tag=f"{group}{l}"))
            return reductions[l, group][1][-1]

        dx, dxb, small_grads[l] = _layer_bwd(dx, dxb, saved[l], P[l], W[l], cos2, sin2, reduce_begin, reduce_continue,
                                            last=(l == 0))
    grad_x = dx.reshape(x.shape)

    small_shapes = [weights[n].shape for n in SMALL] + [g_final.shape, (DEPTH, CONV_CH // LANES, CONV_WP, LANES)]
    parts = [jnp.stack([small_grads[l][n].reshape(weights[n].shape[1:]) for l in range(DEPTH)]) for n in SMALL]
    parts += [g_final_part.reshape(g_final.shape), jnp.stack([small_grads[l]["w_dw"] for l in range(DEPTH)])]
    packed = _pack(parts)
    packed_land = lax.dynamic_update_index_in_dim(lax.empty((N_DEV,) + packed.shape, f32), packed, me, 0)
    small_started = _ici_start("gather", [packed], [packed_land], dx, name="gather_small_start")

    grads_out, delta, new_m, new_v = {}, {}, {}, {}
    swap = lambda a: jnp.swapaxes(a, 1, 2)

    def update(n, lands, sums):
        as_arrives = n not in transposed or weights[n].shape[2] % LANES != 0
        if as_arrives:
            to_arrival = swap if n in transposed else (lambda a: a)
            out = None
            for l in reversed(range(DEPTH)):
                out = _adamw_reduced(l, to_arrival(weights[n]), to_arrival(mom_m[n]), to_arrival(mom_v[n]),
                                     lands[l], sums[l], chip, out, name=f"adamw_{n}_{l}")
            grads_out[n], delta[n], new_m[n], new_v[n] = [to_arrival(o) for o in out]
            return out[1]
        g = jnp.stack([_sum_chip_slots(lands[l], sums[l], chip, name="rs_sum_" + n) for l in range(DEPTH)])
        grads_out[n] = swap(g)
        delta[n], new_m[n], new_v[n] = _adamw(weights[n], grads_out[n], mom_m[n], mom_v[n], name="adamw_" + n)
        return delta[n]

    after = small_started[-1]
    for group in ("ffn", "mix", "in"):
        names = reductions[0, group][0]
        arrived = [_ici_wait("reduce", reductions[l, group][1], after, name=f"rs_wait_{group}{l}") for l in range(DEPTH)]
        for i, n in enumerate(names):
            after = update(n, [arrived[l][1][i] for l in range(DEPTH)], [arrived[l][0][i] for l in range(DEPTH)])

    _, small_lands = _ici_wait("gather", small_started, after, name="gather_small_wait")
    small_full = _d2d_gather([small_lands[0].reshape(N_CHIP, 2, *packed.shape)], after, name="gather_small_d2d")[0]
    total = _sum_slots(small_full.reshape(N_DEV, *packed.shape), name="sum_small")
    small_total = _unpack(total, small_shapes)
    grads_out.update(zip(SMALL + ("g_final",), small_total[:-1]))
    dw_full = small_total[-1]
    grads_out["w_dw"] = lax.dynamic_index_in_dim(dw_full, me, axis=1, keepdims=False)[:, :CONV_W].reshape(w_dw.shape)

    rep = tuple(n for n in SMALL if n != "w_s") + ("g_final",)
    rep_shapes = [weights[n].shape for n in rep]
    packs = [_pack([src[n] for n in rep])[None] for src in (weights, grads_out, mom_m, mom_v)]
    for dst, buf in zip((delta, new_m, new_v), _adamw(*packs, name="adamw_small")):
        dst.update(zip(rep, _unpack(buf[0], rep_shapes)))
    for n, shp in (("w_dw", (1, DEPTH * CONV_W, LANES)), ("w_s", (DEPTH, SG_G * SG_CHUNK, SG_CHUNK))):
        upd = _adamw(*[src[n].reshape(shp) for src in (weights, grads_out, mom_m, mom_v)], name="adamw_" + n)
        for dst, buf in zip((delta, new_m, new_v), upd):
            dst[n] = buf.reshape(weights[n].shape)

    order = ("g_mix", "w_in", "b_gate", "q_norm_g", "k_norm_g", "w_attn_o", "w_dw", "b_dw", "conv_ln_g", "conv_ln_b",
             "w_conv_o", "sg_ln_g", "sg_ln_b", "w_s", "b_s", "w_sg_o", "w_out", "g_ffn", "w_ff_gate", "w_ff_up",
             "w_ff_down", "g_final")
    return (loss, grad_x, *[grads_out[n] for n in order], *[delta[n] for n in order],
            *[new_m[n] for n in order], *[new_v[n] for n in order])
```

```python
import functools
import math

import jax
import jax.numpy as jnp
from jax import lax
from jax.experimental import pallas as pl
from jax.experimental.pallas import tpu as pltpu

f32, bf16 = jnp.float32, jnp.bfloat16

D_MODEL = 2048
SEQ = 2048
DEPTH = 2
GRID_W = 64
HEAD_DIM = 128
LANES = 128
N_Q = (D_MODEL // 2) // HEAD_DIM
N_KV = N_Q // 4
GRP = N_Q // N_KV
Q_COLS = N_Q * HEAD_DIM
KV_COLS = N_KV * HEAD_DIM
CONV_CH = D_MODEL // 2
CONV_W = 31
CONV_PAD = CONV_W // 2
CONV_WP = 32
SG_CH = D_MODEL // 2
SG_G = SG_CH // LANES
SG_CHUNK = 128
D_FF = -(-8 * D_MODEL // (3 * 256)) * 256
OFF_KV = Q_COLS
OFF_CONV = OFF_KV + 2 * KV_COLS
OFF_SG = OFF_CONV + 2 * CONV_CH
OFF_GATE = OFF_SG + 2 * SG_CH
IN_COLS = OFF_GATE + 3 * D_MODEL
ROPE_THETA = 10000.0
SCALE = HEAD_DIM ** -0.5
N_DEV = 8
N_CHIP = 4

ADAM_LR, ADAM_B1, ADAM_B2, ADAM_EPS, ADAM_WD, ADAM_STEP = 0.001, 0.9, 0.999, 1e-08, 0.01, 10

VMEM_BYTES_V7X = 64 << 20
VMEM_CAP = VMEM_BYTES_V7X - (6 << 20)
MESH = pl.DeviceIdType.MESH
HBM = pl.BlockSpec(memory_space=pltpu.HBM)


def _in_hbm(a):
    if isinstance(a, jax.Array) and jnp.issubdtype(a.dtype, jnp.floating) and a.size * a.dtype.itemsize >= (1 << 20):
        return pltpu.with_memory_space_constraint(a, pltpu.HBM)
    return a


def _out_hbm(s):
    if isinstance(s, jax.ShapeDtypeStruct) and math.prod(s.shape) * jnp.dtype(s.dtype).itemsize >= (1 << 20):
        return pltpu.HBM(s.shape, s.dtype)
    return s


def _call(body, **kw):
    shapes = kw.pop("out_shape")
    shapes = type(shapes)(_out_hbm(s) for s in shapes) if isinstance(shapes, (list, tuple)) else _out_hbm(shapes)
    call = pl.pallas_call(body, out_shape=shapes, **kw)
    return lambda *args: call(*[_in_hbm(a) for a in args])


def _pick(n, cands):
    for c in cands:
        if n % c == 0:
            return c
    raise ValueError((n, cands))


def _params(sem, vmem_bytes):
    return pltpu.CompilerParams(dimension_semantics=sem, vmem_limit_bytes=int(min(max(vmem_bytes, 16 << 20), VMEM_CAP)))


def _mm(a, b, form, out_dtype, *, n=None, b_off=0, res=None, after=None, name):
    if form == "tn":
        K, M = a.shape
    else:
        M, K = a.shape
    N = n if n is not None else (b.shape[0] if form == "nt" else b.shape[1])
    if K <= 2048:
        tk = K
        if form == "tn":
            tm = _pick(M, (512, 256, 128))
            tn = N if N <= 2048 else _pick(N, (1024, 512, 256, 128))
        else:
            tm = M if M <= 2048 else _pick(M, (2048, 1024, 512))
            tn = _pick(math.gcd(N, b_off) if b_off else N, (256, 128) if res is not None else (512, 256, 128))
    else:
        tk = max(t for t in range(LANES, 3072 + 1, LANES) if K % t == 0)
        tm = _pick(M, (1024, 512, 256, 128))
        tn = _pick(math.gcd(N, b_off) if b_off else N, (1024, 512, 256, 128))
    assert b_off % tn == 0
    off = b_off // tn
    nk = K // tk
    if form == "tn":
        a_spec = pl.BlockSpec((tk, tm), lambda i, j, k: (k, i))
    else:
        a_spec = pl.BlockSpec((tm, tk), lambda i, j, k: (i, k))
    if form == "nt":
        b_spec = pl.BlockSpec((tn, tk), lambda i, j, k: (j + off, k))
    else:
        b_spec = pl.BlockSpec((tk, tn), lambda i, j, k: (k, j + off))
    dims = {"nn": ((1,), (0,)), "nt": ((1,), (1,)), "tn": ((0,), (0,))}[form]
    has_res = res is not None

    def body(*refs):
        if after is not None:
            refs = refs[1:]
        if has_res:
            a_ref, b_ref, r_ref, o_ref = refs[:4]
        else:
            a_ref, b_ref, o_ref = refs[:3]
        p = lax.dot_general(a_ref[...], b_ref[...], (dims, ((), ())), preferred_element_type=f32)

        def finish(acc):
            if has_res:
                acc = acc + r_ref[...].astype(f32)
            o_ref[...] = acc.astype(o_ref.dtype)

        if nk == 1:
            finish(p)
        else:
            acc_ref = refs[-1]
            k = pl.program_id(2)

            @pl.when(k == 0)
            def _():
                acc_ref[...] = p

            @pl.when(k > 0)
            def _():
                acc_ref[...] += p

            @pl.when(k == nk - 1)
            def _():
                finish(acc_ref[...])

    in_specs = [a_spec, b_spec]
    args = [a, b]
    osz = jnp.dtype(out_dtype).itemsize
    vmem = 2 * (tm * tk * 2 + tk * tn * 2 + tm * tn * osz) + 2 * tm * tn * 4
    if has_res:
        in_specs.append(pl.BlockSpec((tm, tn), lambda i, j, k: (i, j)))
        args.append(res)
        vmem += 2 * tm * tn * res.dtype.itemsize
    scratch = []
    if nk > 1:
        scratch.append(pltpu.VMEM((tm, tn), f32))
        vmem += tm * tn * 4
    if after is not None:
        in_specs.insert(0, pl.BlockSpec(memory_space=pl.ANY))
        args.insert(0, after)
    return _call(
        body, name=name, grid=(M // tm, N // tn, nk),
        in_specs=in_specs, out_specs=pl.BlockSpec((tm, tn), lambda i, j, k: (i, j)),
        out_shape=jax.ShapeDtypeStruct((M, N), out_dtype), scratch_shapes=scratch,
        compiler_params=_params(("parallel", "parallel", "arbitrary"), vmem + (8 << 20)),
    )(*args)


EPI_TN = 256


def _mm_epi(a, bs, form, extras, out_dtypes, n_sums, fn, name):
    a_list = list(a) if isinstance(a, (list, tuple)) else [a]
    M, K = a_list[0].shape
    N = bs[0].shape[0] if form == "nt" else bs[0].shape[1]
    tn = EPI_TN
    assert K <= 2048 and N % tn == 0 and len(a_list) in (1, len(bs))
    dims = ((1,), (1,)) if form == "nt" else ((1,), (0,))
    na, nb, ne = len(a_list), len(bs), len(extras)

    def body(*refs):
        a_refs, b_refs = refs[:na], refs[na:na + nb]
        e_refs, o_refs = refs[na + nb:na + nb + ne], refs[na + nb + ne:]
        avs = [r[...] for r in a_refs] * (nb // na)
        ps = [lax.dot_general(av, b[...], (dims, ((), ())), preferred_element_type=f32) for av, b in zip(avs, b_refs)]
        for o_ref, o in zip(o_refs, fn(ps, [e[...] for e in e_refs])):
            o_ref[...] = o.astype(o_ref.dtype)

    in_specs = [pl.BlockSpec((M, K), lambda j: (0, 0), pipeline_mode=pl.Buffered(1)) for _ in a_list]
    in_specs += [pl.BlockSpec((tn, K), lambda j: (j, 0)) if form == "nt" else pl.BlockSpec((K, tn), lambda j: (0, j))
                 for _ in bs]
    for arr, first in extras:
        assert first % tn == 0
        in_specs.append(pl.BlockSpec((arr.shape[0], tn), functools.partial(lambda j, o: (0, j + o), o=first // tn)))
    out_specs = [pl.BlockSpec((M, tn), lambda j: (0, j))] * len(out_dtypes) + [pl.BlockSpec((1, tn), lambda j: (0, j))] * n_sums
    out_shape = [jax.ShapeDtypeStruct((M, N), dt) for dt in out_dtypes] + [jax.ShapeDtypeStruct((1, N), f32)] * n_sums
    tiles = sum(arr.shape[0] * tn * arr.dtype.itemsize for arr, _ in extras) + sum(M * tn * jnp.dtype(dt).itemsize for dt in out_dtypes)
    vmem = na * M * K * 2 + 2 * nb * tn * K * 2 + 2 * tiles + (nb + 6) * M * tn * 4
    return _call(body, name=name, grid=(N // tn,), in_specs=in_specs, out_specs=out_specs, out_shape=out_shape,
                 compiler_params=_params(("parallel",), vmem + (8 << 20)))(*a_list, *bs, *[arr for arr, _ in extras])


def _ffn_up(hf, wt_gate, wt_up, name):
    def fn(ps, _):
        g, u = ps[0].astype(bf16), ps[1].astype(bf16)
        gf = g.astype(f32)
        return g, u, gf * jax.nn.sigmoid(gf) * u.astype(f32)

    return _mm_epi(hf, [wt_gate, wt_up], "nt", [], [bf16] * 3, 0, fn, name)


def _ffn_down_bwd(dx2b, w_down, fg, fu, name):
    def fn(ps, es):
        d, g = ps[0], es[0].astype(f32)
        sg = jax.nn.sigmoid(g)
        return d * es[1].astype(f32) * sg * (1.0 + g * (1.0 - sg)), d * g * sg

    return _mm_epi(dx2b, [w_down], "nt", [(fg, 0), (fu, 0)], [bf16] * 2, 0, fn, name)


def _mixer_out(branches, wts, gl, b_gate, name):
    D = wts[0].shape[0]

    def fn(ps, es):
        ys = [p_.astype(bf16) for p_ in ps]
        merged = None
        for i in range(3):
            term = jax.nn.sigmoid(es[i].astype(f32) + es[3 + i]) * ys[i].astype(f32)
            merged = term if merged is None else merged + term
        return ys + [merged]

    extras = [(gl, i * D) for i in range(3)] + [(b_gate, i * D) for i in range(3)]
    return _mm_epi(branches, wts, "nt", extras, [bf16] * 4, 0, fn, name)


def _merge_bwd_fused(dx1b, w_out, gl, b_gate, ya, yc, ys, name):
    D = ya.shape[1]

    def fn(ps, es):
        dm_, outs, sums = ps[0], [], []
        for i in range(3):
            gate = jax.nn.sigmoid(es[i].astype(f32) + es[3 + i])
            dlog = dm_ * es[6 + i].astype(f32) * gate * (1.0 - gate)
            outs.append((dlog, dm_ * gate))
            sums.append(jnp.sum(dlog, axis=0, keepdims=True))
        return [o[0] for o in outs] + [o[1] for o in outs] + sums

    extras = [(gl, i * D) for i in range(3)] + [(b_gate, i * D) for i in range(3)] + [(ya, 0), (yc, 0), (ys, 0)]
    return _mm_epi(dx1b, [w_out], "nt", extras, [bf16] * 6, 3, fn, name)


def _rows(body, ins, outs, *, tm, name, vmem=40 << 20):
    nrows = next(s[1].shape[0] for s in ins if s[0] == "r")
    in_specs, args = [], []
    for s in ins:
        arr = s[1]
        if s[0] == "r":
            w = s[2] if len(s) > 2 else arr.shape[1]
            cb = s[3] if len(s) > 3 else 0
            in_specs.append(pl.BlockSpec((tm, w), functools.partial(lambda i, cb: (i, cb), cb=cb)))
        else:
            in_specs.append(pl.BlockSpec(arr.shape, functools.partial(lambda i, nd: (0,) * nd, nd=arr.ndim)))
        args.append(arr)
    out_specs, out_shape = [], []
    for s in outs:
        if s[0] == "r":
            out_specs.append(pl.BlockSpec((tm, s[1]), lambda i: (i, 0)))
            out_shape.append(jax.ShapeDtypeStruct((nrows, s[1]), s[2]))
        else:
            out_specs.append(pl.BlockSpec(s[1], functools.partial(lambda i, nd: (0,) * nd, nd=len(s[1]))))
            out_shape.append(jax.ShapeDtypeStruct(s[1], s[2]))
    return _call(body, name=name, grid=(nrows // tm,), in_specs=in_specs, out_specs=out_specs,
                 out_shape=out_shape, compiler_params=_params(("arbitrary",), vmem))(*args)


def _accumulate(ref, part):
    i = pl.program_id(0)

    @pl.when(i == 0)
    def _():
        ref[...] = part

    @pl.when(i > 0)
    def _():
        ref[...] += part


def _rms_stats(x):
    r = lax.rsqrt(jnp.mean(x * x, axis=-1, keepdims=True) + 1e-6)
    return r, x * r


def _rms_fwd(x, g, name):
    def body(x_ref, g_ref, o_ref):
        _, xn = _rms_stats(x_ref[...])
        o_ref[...] = (xn * g_ref[...]).astype(o_ref.dtype)

    return _rows(body, [("r", x), ("f", g)], [("r", x.shape[1], bf16)], tm=min(256, x.shape[0]), name=name)[0]


def _rms_bwd(x, g, dh, dres, name):
    D = x.shape[1]

    def body(x_ref, g_ref, dh_ref, dr_ref, dx_ref, dxb_ref, dg_ref):
        r, xn = _rms_stats(x_ref[...])
        dy = dh_ref[...].astype(f32)
        dxn = dy * g_ref[...]
        dx = dr_ref[...] + r * (dxn - xn * jnp.mean(dxn * xn, axis=-1, keepdims=True))
        dx_ref[...] = dx
        dxb_ref[...] = dx.astype(bf16)
        _accumulate(dg_ref, jnp.sum(dy * xn, axis=0, keepdims=True))

    return _rows(body, [("r", x), ("f", g), ("r", dh), ("r", dres)],
                 [("r", D, f32), ("r", D, bf16), ("a", (1, D), f32)], tm=min(256, x.shape[0]), name=name)


def _final_loss(x, g, tgt, name):
    D = x.shape[1]

    def body(x_ref, g_ref, t_ref, dx_ref, dxb_ref, sq_ref, dg_ref):
        r, xn = _rms_stats(x_ref[...])
        gain = g_ref[...]
        diff = xn * gain - t_ref[...]
        dy = diff * (1.0 / D)
        dxn = dy * gain
        dx = r * (dxn - xn * jnp.mean(dxn * xn, axis=-1, keepdims=True))
        dx_ref[...] = dx
        dxb_ref[...] = dx.astype(bf16)
        _accumulate(sq_ref, jnp.sum(diff * diff, axis=0, keepdims=True))
        _accumulate(dg_ref, jnp.sum(dy * xn, axis=0, keepdims=True))

    return _rows(body, [("r", x), ("f", g), ("r", tgt)],
                 [("r", D, f32), ("r", D, bf16), ("a", (1, D), f32), ("a", (1, D), f32)],
                 tm=min(256, x.shape[0]), name=name)


def _qk_fwd(q_raw, kv_raw, qg, kg, cos2, sin2, name):
    def body(q_ref, k_ref, qg_ref, kg_ref, c_ref, s_ref, qo_ref, ko_ref):
        c, s = c_ref[...], s_ref[...]

        def head(src, gain, dst, h):
            cols = slice(h * HEAD_DIM, (h + 1) * HEAD_DIM)
            _, xn = _rms_stats(src[:, cols].astype(f32))
            y = xn * gain
            dst[:, cols] = (y * c + pltpu.roll(y, HEAD_DIM // 2, 1) * s).astype(dst.dtype)

        for h in range(N_Q):
            head(q_ref, qg_ref[...], qo_ref, h)
        for h in range(N_KV):
            head(k_ref, kg_ref[...], ko_ref, h)

    return _rows(body, [("r", q_raw), ("r", kv_raw, KV_COLS, 0), ("f", qg), ("f", kg), ("r", cos2), ("r", sin2)],
                 [("r", Q_COLS, bf16), ("r", KV_COLS, bf16)], tm=min(256, q_raw.shape[0]), name=name)


def _qk_bwd(q_raw, kv_raw, dqr, dkr, qg, kg, cos2, sin2, name):
    def body(q_ref, k_ref, dq_ref, dk_ref, qg_ref, kg_ref, c_ref, s_ref, dqo_ref, dko_ref, dqg_ref, dkg_ref):
        c, s = c_ref[...], s_ref[...]

        def head(src, dsrc, gain, dst, h):
            cols = slice(h * HEAD_DIM, (h + 1) * HEAD_DIM)
            r, xn = _rms_stats(src[:, cols].astype(f32))
            do = dsrc[:, cols].astype(f32)
            dy = do * c + pltpu.roll(do * s, HEAD_DIM // 2, 1)
            dxn = dy * gain
            dst[:, cols] = (r * (dxn - xn * jnp.mean(dxn * xn, axis=-1, keepdims=True))).astype(dst.dtype)
            return jnp.sum(dy * xn, axis=0, keepdims=True)

        dq_gain = head(q_ref, dq_ref, qg_ref[...], dqo_ref, 0)
        for h in range(1, N_Q):
            dq_gain = dq_gain + head(q_ref, dq_ref, qg_ref[...], dqo_ref, h)
        dk_gain = head(k_ref, dk_ref, kg_ref[...], dko_ref, 0)
        for h in range(1, N_KV):
            dk_gain = dk_gain + head(k_ref, dk_ref, kg_ref[...], dko_ref, h)
        _accumulate(dqg_ref, dq_gain)
        _accumulate(dkg_ref, dk_gain)

    return _rows(body, [("r", q_raw), ("r", kv_raw, KV_COLS, 0), ("r", dqr), ("r", dkr), ("f", qg), ("f", kg),
                        ("r", cos2), ("r", sin2)],
                 [("r", Q_COLS, bf16), ("r", KV_COLS, bf16), ("a", (1, HEAD_DIM), f32), ("a", (1, HEAD_DIM), f32)],
                 tm=min(256, q_raw.shape[0]), name=name)


def _softmax_rows(q, k):
    s = lax.dot_general(q, k, (((1,), (1,)), ((), ())), preferred_element_type=f32) * (SCALE * math.log2(math.e))
    p = jnp.exp2(s - jnp.max(s, axis=-1, keepdims=True))
    return p * (1.0 / jnp.sum(p, axis=-1, keepdims=True))


def _head_cols(g):
    return slice(g * HEAD_DIM, (g + 1) * HEAD_DIM)


def _attn_fwd(qr, kr, kv_raw, name):
    S = qr.shape[0]
    tq = min(256, S)

    def body(q_ref, k_ref, v_ref, o_ref):
        k, v = k_ref[...], v_ref[...]
        for g in range(GRP):
            p = _softmax_rows(q_ref[:, _head_cols(g)], k)
            o_ref[:, _head_cols(g)] = jnp.dot(p.astype(bf16), v, preferred_element_type=f32).astype(o_ref.dtype)

    return _call(
        body, name=name, grid=(N_KV, S // tq),
        in_specs=[pl.BlockSpec((tq, GRP * HEAD_DIM), lambda kv, i: (i, kv)),
                  pl.BlockSpec((S, HEAD_DIM), lambda kv, i: (0, kv)),
                  pl.BlockSpec((S, HEAD_DIM), lambda kv, i: (0, N_KV + kv))],
        out_specs=pl.BlockSpec((tq, GRP * HEAD_DIM), lambda kv, i: (i, kv)),
        out_shape=jax.ShapeDtypeStruct((S, Q_COLS), bf16),
        compiler_params=_params(("parallel", "arbitrary"), 4 * GRP * tq * S * 4 + (8 << 20)),
    )(qr, kr, kv_raw)


def _attn_bwd(qr, kr, kv_raw, do, name):
    S = qr.shape[0]
    tq = min(256, S)

    def body(q_ref, k_ref, v_ref, do_ref, dq_ref, dk_ref, dv_ref):
        first = pl.program_id(1) == 0
        k, v = k_ref[...], v_ref[...]
        dv_part = dk_part = None
        for g in range(GRP):
            q, do_ = q_ref[:, _head_cols(g)], do_ref[:, _head_cols(g)]
            p = _softmax_rows(q, k)
            dp = lax.dot_general(do_, v, (((1,), (1,)), ((), ())), preferred_element_type=f32)
            ds = (p * (dp - jnp.sum(dp * p, axis=-1, keepdims=True)) * SCALE).astype(bf16)
            dq_ref[:, _head_cols(g)] = jnp.dot(ds, k, preferred_element_type=f32).astype(dq_ref.dtype)
            dv_g = lax.dot_general(p.astype(bf16), do_, (((0,), (0,)), ((), ())), preferred_element_type=f32)
            dk_g = lax.dot_general(ds, q, (((0,), (0,)), ((), ())), preferred_element_type=f32)
            dv_part = dv_g if g == 0 else dv_part + dv_g
            dk_part = dk_g if g == 0 else dk_part + dk_g

        @pl.when(first)
        def _():
            dv_ref[...] = dv_part
            dk_ref[...] = dk_part

        @pl.when(jnp.logical_not(first))
        def _():
            dv_ref[...] += dv_part
            dk_ref[...] += dk_part

    qspec = pl.BlockSpec((tq, GRP * HEAD_DIM), lambda kv, i: (i, kv))
    return _call(
        body, name=name, grid=(N_KV, S // tq),
        in_specs=[qspec, pl.BlockSpec((S, HEAD_DIM), lambda kv, i: (0, kv)),
                  pl.BlockSpec((S, HEAD_DIM), lambda kv, i: (0, N_KV + kv)), qspec],
        out_specs=[qspec, pl.BlockSpec((S, HEAD_DIM), lambda kv, i: (0, kv)),
                   pl.BlockSpec((S, HEAD_DIM), lambda kv, i: (0, kv))],
        out_shape=[jax.ShapeDtypeStruct((S, Q_COLS), bf16), jax.ShapeDtypeStruct((S, KV_COLS), f32),
                   jax.ShapeDtypeStruct((S, KV_COLS), f32)],
        compiler_params=_params(("parallel", "arbitrary"), 6 * GRP * tq * S * 4 + (8 << 20)),
    )(qr, kr, kv_raw, do)


CONV_HALO = 16


def _fill_padded(pad_ref, val, S):
    pad_ref[pl.ds(0, CONV_HALO), :] = jnp.zeros((CONV_HALO, LANES), f32)
    pad_ref[pl.ds(CONV_HALO + S, CONV_HALO), :] = jnp.zeros((CONV_HALO, LANES), f32)
    pad_ref[pl.ds(CONV_HALO, S), :] = val


def _group_specs(S, n_groups, second_half):
    return pl.BlockSpec((S, LANES), functools.partial(lambda g, o: (0, g + o), o=n_groups if second_half else 0))


def _conv1_fwd(conv_in, wdw, b_dw, name):
    S = conv_in.shape[0]
    ng = CONV_CH // LANES
    R = min(256, S)

    def body(a_ref, g_ref, w_ref, b_ref, o_ref, pad_ref):
        z = a_ref[...].astype(f32) * jax.nn.sigmoid(g_ref[...].astype(f32))
        _fill_padded(pad_ref, z, S)
        for r in range(S // R):
            acc = jnp.zeros((R, LANES), f32) + b_ref[...]
            for j in range(CONV_W):
                acc = acc + w_ref[pl.ds(j, 1), :] * pad_ref[pl.ds(r * R + CONV_HALO - CONV_PAD + j, R), :]
            o_ref[pl.ds(r * R, R), :] = acc

    return _call(
        body, name=name, grid=(ng,),
        in_specs=[_group_specs(S, ng, False), _group_specs(S, ng, True),
                  pl.BlockSpec((CONV_WP, LANES), lambda g: (g, 0)), pl.BlockSpec((1, LANES), lambda g: (0, g))],
        out_specs=pl.BlockSpec((S, LANES), lambda g: (0, g)),
        out_shape=jax.ShapeDtypeStruct((S, CONV_CH), f32),
        scratch_shapes=[pltpu.VMEM((S + 2 * CONV_HALO, LANES), f32)],
        compiler_params=_params(("parallel",), 24 << 20),
    )(conv_in, conv_in, wdw, b_dw)


def _conv1_bwd(conv_in, dc, wdw, name):
    S = conv_in.shape[0]
    ng = CONV_CH // LANES
    R = min(256, S)

    def body(a_ref, g_ref, w_ref, dc_ref, da_ref, dg_ref, dw_ref, db_ref, padz_ref, padd_ref):
        a = a_ref[...].astype(f32)
        sg = jax.nn.sigmoid(g_ref[...].astype(f32))
        _fill_padded(padz_ref, a * sg, S)
        _fill_padded(padd_ref, dc_ref[...], S)
        for r in range(S // R):
            dz = jnp.zeros((R, LANES), f32)
            for j in range(CONV_W):
                dz = dz + w_ref[pl.ds(j, 1), :] * padd_ref[pl.ds(r * R + CONV_HALO + CONV_PAD - j, R), :]
            rows = pl.ds(r * R, R)
            ar, sr = a_ref[rows, :].astype(f32), jax.nn.sigmoid(g_ref[rows, :].astype(f32))
            da_ref[rows, :] = (dz * sr).astype(da_ref.dtype)
            dg_ref[rows, :] = (dz * ar * sr * (1.0 - sr)).astype(dg_ref.dtype)
        for j in range(CONV_W):
            tot = jnp.zeros((1, LANES), f32)
            for r in range(S // R):
                tot = tot + jnp.sum(dc_ref[pl.ds(r * R, R), :] * padz_ref[pl.ds(r * R + CONV_HALO - CONV_PAD + j, R), :],
                                    axis=0, keepdims=True)
            dw_ref[pl.ds(j, 1), :] = tot
        dw_ref[pl.ds(CONV_W, CONV_WP - CONV_W), :] = jnp.zeros((CONV_WP - CONV_W, LANES), f32)
        db_ref[...] = jnp.sum(dc_ref[...], axis=0, keepdims=True)

    return _call(
        body, name=name, grid=(ng,),
        in_specs=[_group_specs(S, ng, False), _group_specs(S, ng, True),
                  pl.BlockSpec((CONV_WP, LANES), lambda g: (g, 0)), pl.BlockSpec((S, LANES), lambda g: (0, g))],
        out_specs=[pl.BlockSpec((S, LANES), lambda g: (0, g)), pl.BlockSpec((S, LANES), lambda g: (0, g)),
                   pl.BlockSpec((CONV_WP, LANES), lambda g: (g, 0)), pl.BlockSpec((1, LANES), lambda g: (0, g))],
        out_shape=[jax.ShapeDtypeStruct((S, CONV_CH), bf16), jax.ShapeDtypeStruct((S, CONV_CH), bf16),
                   jax.ShapeDtypeStruct((ng * CONV_WP, LANES), f32), jax.ShapeDtypeStruct((1, CONV_CH), f32)],
        scratch_shapes=[pltpu.VMEM((S + 2 * CONV_HALO, LANES), f32), pltpu.VMEM((S + 2 * CONV_HALO, LANES), f32)],
        compiler_params=_params(("parallel",), 24 << 20),
    )(conv_in, conv_in, wdw, dc)


def _ln_stats(x, eps=1e-5):
    xc = x - jnp.mean(x, axis=-1, keepdims=True)
    r = lax.rsqrt(jnp.mean(xc * xc, axis=-1, keepdims=True) + eps)
    return r, xc * r


def _ln_bwd(r, xh, dxh):
    return r * (dxh - jnp.mean(dxh, axis=-1, keepdims=True) - xh * jnp.mean(dxh * xh, axis=-1, keepdims=True))


def _conv2_fwd(c, ln_g, ln_b, name):
    def body(c_ref, g_ref, b_ref, o_ref):
        _, xh = _ln_stats(c_ref[...])
        y = xh * g_ref[...] + b_ref[...]
        o_ref[...] = (y * jax.nn.sigmoid(y)).astype(o_ref.dtype)

    return _rows(body, [("r", c), ("f", ln_g), ("f", ln_b)], [("r", CONV_CH, bf16)], tm=min(256, c.shape[0]), name=name)[0]


def _conv2_bwd(c, dcz, ln_g, ln_b, name):
    def body(c_ref, d_ref, g_ref, b_ref, dc_ref, dg_ref, db_ref):
        r, xh = _ln_stats(c_ref[...])
        y = xh * g_ref[...] + b_ref[...]
        sg = jax.nn.sigmoid(y)
        dy = d_ref[...].astype(f32) * (sg * (1.0 + y * (1.0 - sg)))
        dc_ref[...] = _ln_bwd(r, xh, dy * g_ref[...])
        _accumulate(dg_ref, jnp.sum(dy * xh, axis=0, keepdims=True))
        _accumulate(db_ref, jnp.sum(dy, axis=0, keepdims=True))

    return _rows(body, [("r", c), ("r", dcz), ("f", ln_g), ("f", ln_b)],
                 [("r", CONV_CH, f32), ("a", (1, CONV_CH), f32), ("a", (1, CONV_CH), f32)],
                 tm=min(256, c.shape[0]), name=name)


GELU_K = math.sqrt(2.0 / math.pi)
GELU_C = 0.044715


def _gelu(x):
    return 0.5 * x * (1.0 + jnp.tanh(GELU_K * (x + GELU_C * x * x * x)))


def _gelu_and_grad(x):
    x2 = x * x
    th = jnp.tanh(GELU_K * (x + GELU_C * x2 * x))
    half = 0.5 * (1.0 + th)
    return x * half, half + 0.5 * x * (1.0 - th * th) * (GELU_K * (1.0 + 3.0 * GELU_C * x2))


def _chunk_rows(n):
    return pl.ds(pl.multiple_of(n * SG_CHUNK, SG_CHUNK), SG_CHUNK)


def _sgu_fwd(sg_in, ln_g, ln_b, w_s, b_s, name):
    S = sg_in.shape[0]

    def body(u_ref, v_ref, lg_ref, lb_ref, w_ref, b_ref, o_ref):
        wb = w_ref[...].astype(bf16)

        def chunk(n, carry):
            rows = _chunk_rows(n)
            gu = _gelu(u_ref[rows, :].astype(f32))
            _, xh = _ln_stats(_gelu(v_ref[rows, :].astype(f32)))
            vl = xh * lg_ref[...] + lb_ref[...]
            t = jnp.dot(wb, vl.astype(bf16), preferred_element_type=f32) + b_ref[...]
            o_ref[rows, :] = (gu * t).astype(o_ref.dtype)
            return carry

        lax.fori_loop(0, S // SG_CHUNK, chunk, 0, unroll=2)

    return _call(
        body, name=name, grid=(SG_G,),
        in_specs=[_group_specs(S, SG_G, False), _group_specs(S, SG_G, True),
                  pl.BlockSpec((1, LANES), lambda g: (0, g)), pl.BlockSpec((1, LANES), lambda g: (0, g)),
                  pl.BlockSpec((None, SG_CHUNK, SG_CHUNK), lambda g: (g, 0, 0)),
                  pl.BlockSpec((None, SG_CHUNK, 1), lambda g: (g, 0, 0))],
        out_specs=pl.BlockSpec((S, LANES), lambda g: (0, g)),
        out_shape=jax.ShapeDtypeStruct((S, SG_CH), bf16),
        compiler_params=_params(("parallel",), 24 << 20),
    )(sg_in, sg_in, ln_g, ln_b, w_s, b_s)


def _sgu_bwd(sg_in, dsz, ln_g, ln_b, w_s, w_s_t, b_s, name):
    S = sg_in.shape[0]

    def body(u_ref, v_ref, lg_ref, lb_ref, w_ref, wt_ref, b_ref, d_ref, du_ref, dv_ref, dw_ref, db_ref, dlg_ref, dlb_ref):
        wb = w_ref[...].astype(bf16)
        wtb = wt_ref[...].astype(bf16)

        def chunk(n, carry):
            dwa, dba, dlga, dlba = carry
            rows = _chunk_rows(n)
            u = u_ref[rows, :].astype(f32)
            v = v_ref[rows, :].astype(f32)
            gu, gu_grad = _gelu_and_grad(u)
            gv, gv_grad = _gelu_and_grad(v)
            r, xh = _ln_stats(gv)
            vlb = (xh * lg_ref[...] + lb_ref[...]).astype(bf16)
            t = jnp.dot(wb, vlb, preferred_element_type=f32) + b_ref[...]
            d = d_ref[rows, :].astype(f32)
            dt = d * gu
            dtb = dt.astype(bf16)
            dwa = dwa + lax.dot_general(dtb, vlb, (((1,), (1,)), ((), ())), preferred_element_type=f32)
            dba = dba + jnp.sum(dt, axis=1, keepdims=True)
            dvl = jnp.dot(wtb, dtb, preferred_element_type=f32)
            dlga = dlga + jnp.sum(dvl * xh, axis=0, keepdims=True)
            dlba = dlba + jnp.sum(dvl, axis=0, keepdims=True)
            dgv = _ln_bwd(r, xh, dvl * lg_ref[...])
            du_ref[rows, :] = (d * t * gu_grad).astype(du_ref.dtype)
            dv_ref[rows, :] = (dgv * gv_grad).astype(dv_ref.dtype)
            return dwa, dba, dlga, dlba

        init = (jnp.zeros((SG_CHUNK, SG_CHUNK), f32), jnp.zeros((SG_CHUNK, 1), f32),
                jnp.zeros((1, LANES), f32), jnp.zeros((1, LANES), f32))
        dwa, dba, dlga, dlba = lax.fori_loop(0, S // SG_CHUNK, chunk, init, unroll=2)
        dw_ref[...] = dwa
        db_ref[...] = dba
        dlg_ref[...] = dlga
        dlb_ref[...] = dlba

    wspec = pl.BlockSpec((None, SG_CHUNK, SG_CHUNK), lambda g: (g, 0, 0))
    bspec = pl.BlockSpec((None, SG_CHUNK, 1), lambda g: (g, 0, 0))
    lspec = pl.BlockSpec((1, LANES), lambda g: (0, g))
    cspec = pl.BlockSpec((S, LANES), lambda g: (0, g))
    return _call(
        body, name=name, grid=(SG_G,),
        in_specs=[_group_specs(S, SG_G, False), _group_specs(S, SG_G, True), lspec, lspec, wspec, wspec, bspec, cspec],
        out_specs=[cspec, cspec, wspec, bspec, lspec, lspec],
        out_shape=[jax.ShapeDtypeStruct((S, SG_CH), bf16), jax.ShapeDtypeStruct((S, SG_CH), bf16),
                   jax.ShapeDtypeStruct((SG_G, SG_CHUNK, SG_CHUNK), f32), jax.ShapeDtypeStruct((SG_G, SG_CHUNK, 1), f32),
                   jax.ShapeDtypeStruct((1, SG_CH), f32), jax.ShapeDtypeStruct((1, SG_CH), f32)],
        compiler_params=_params(("parallel",), 24 << 20),
    )(sg_in, sg_in, ln_g, ln_b, w_s, w_s_t, b_s, dsz)


def _row_tile(r, c, n_arrays, itemsize=4):
    fits = [tm for tm in range(16, r + 1, 16) if r % tm == 0 and 2 * n_arrays * tm * c * itemsize <= (24 << 20)]
    return fits[-1] if fits else r


def _sum_slots(slots, name):
    n, r, c = slots.shape
    tm = _row_tile(r, c, n + 2)

    def body(s_ref, o_ref):
        acc = s_ref[0].astype(f32)
        for k in range(1, n):
            acc = acc + s_ref[k].astype(f32)
        o_ref[...] = acc

    return _call(body, name=name, grid=(r // tm,),
                 in_specs=[pl.BlockSpec((n, tm, c), lambda i: (0, i, 0))],
                 out_specs=pl.BlockSpec((tm, c), lambda i: (i, 0)),
                 out_shape=jax.ShapeDtypeStruct((r, c), f32),
                 compiler_params=_params(("parallel",), 40 << 20))(slots)


def _add_sibling(g4, recv, pos, name):
    _, _, r, c = g4.shape
    tm = _row_tile(r, c, 4, 2)

    def body(pos_ref, g_ref, r_ref, o_ref, own_ref):
        s = (g_ref[...].astype(f32) + r_ref[...].astype(f32)).astype(o_ref.dtype)
        o_ref[...] = s

        @pl.when(pl.program_id(1) == pos_ref[1])
        def _():
            own_ref[...] = s

    grid_spec = pltpu.PrefetchScalarGridSpec(
        num_scalar_prefetch=1, grid=(r // tm, N_CHIP),
        in_specs=[pl.BlockSpec((None, None, tm, c), lambda i, k, pos_ref: (k, pos_ref[0], i, 0)),
                  pl.BlockSpec((None, tm, c), lambda i, k, pos_ref: (k, i, 0))],
        out_specs=[pl.BlockSpec((None, tm, c), lambda i, k, pos_ref: (k, i, 0)),
                   pl.BlockSpec((None, tm, c), lambda i, k, pos_ref: (pos_ref[1], i, 0))])
    return _call(body, name=name, grid_spec=grid_spec, out_shape=[jax.ShapeDtypeStruct((N_CHIP, r, c), bf16)] * 2,
                 compiler_params=_params(("parallel", "arbitrary"), 40 << 20))(pos, g4, recv)


def _adamw(w, g, m, v, name):
    L, r, c = w.shape
    tm = _row_tile(r, c, 7)
    c1 = 1.0 - ADAM_B1 ** ADAM_STEP
    c2 = 1.0 - ADAM_B2 ** ADAM_STEP

    def body(w_ref, g_ref, m_ref, v_ref, d_ref, mo_ref, vo_ref):
        g_ = g_ref[...]
        m_ = ADAM_B1 * m_ref[...] + (1.0 - ADAM_B1) * g_
        v_ = ADAM_B2 * v_ref[...] + (1.0 - ADAM_B2) * (g_ * g_)
        d_ref[...] = -ADAM_LR * ((m_ / c1) / (jnp.sqrt(v_ / c2) + ADAM_EPS) + ADAM_WD * w_ref[...])
        mo_ref[...] = m_
        vo_ref[...] = v_

    spec = pl.BlockSpec((None, tm, c), lambda l, i: (l, i, 0))
    shp = jax.ShapeDtypeStruct((L, r, c), f32)
    return _call(body, name=name, grid=(L, r // tm), in_specs=[spec] * 4, out_specs=[spec] * 3,
                 out_shape=[shp] * 3, compiler_params=_params(("parallel", "parallel"), 40 << 20))(w, g, m, v)


def _mesh_pos():
    return lax.axis_index("x"), lax.axis_index("y"), lax.axis_index("c")


SEM = pl.BlockSpec(memory_space=pltpu.SEMAPHORE)
ANY = pl.BlockSpec(memory_space=pl.ANY)
EFFECT = pltpu.SideEffectType.DATAFLOW_SIDE_EFFECTING


def _other_chips(x, y):
    return [(1 - x, y), (x, 1 - y), (1 - x, 1 - y)]


def _peers(kind, x, y):
    return [(x, y)] if kind == "sibling" else _other_chips(x, y)


def _ici_copy(kind, src_ref, land_ref, send_sem, recv_sem, sender, target, c):
    (sx, sy), (tx, ty) = sender, target
    if kind == "sibling":
        return pltpu.make_async_remote_copy(src_ref=src_ref.at[:, 1 - c], dst_ref=land_ref, send_sem=send_sem,
                                            recv_sem=recv_sem, device_id=(tx, ty, 1 - c), device_id_type=MESH)
    if kind == "gather":
        src, dst = src_ref, land_ref.at[4 * sx + 2 * sy + c]
    else:
        src, dst = src_ref.at[2 * tx + ty], land_ref.at[2 * sx + sy]
    return pltpu.make_async_remote_copy(src_ref=src, dst_ref=dst, send_sem=send_sem, recv_sem=recv_sem,
                                        device_id=(tx, ty, c), device_id_type=MESH)


def _ici_start(kind, srcs, lands, after, name):
    n = len(srcs)
    npeer = 1 if kind == "sibling" else 3

    def body(*refs):
        src_refs, land_refs = refs[:n], refs[n:2 * n]
        send_sems, recv_sems = refs[2 * n + 1], refs[2 * n + 2]
        token = refs[-1]
        x, y, c = _mesh_pos()
        for j, chip in enumerate(_peers(kind, x, y)):
            for k in range(n):
                _ici_copy(kind, src_refs[k], land_refs[k], send_sems.at[npeer * k + j], recv_sems.at[npeer * k + j],(x, y), chip, c).start()
        token[...] = jnp.zeros_like(token)

    bufs = list(srcs) + list(lands)
    return _call(
        body, name=name,
        out_shape=(pltpu.SemaphoreType.DMA((npeer * n,)), pltpu.SemaphoreType.DMA((npeer * n,)),
                   *[pltpu.HBM(b.shape, b.dtype) for b in bufs], jax.ShapeDtypeStruct((8, LANES), f32)),
        in_specs=[HBM] * (2 * n) + [ANY], out_specs=(SEM, SEM, *[HBM] * (2 * n), pl.BlockSpec(memory_space=pltpu.VMEM)),
        input_output_aliases={i: 2 + i for i in range(2 * n)},
        compiler_params=pltpu.CompilerParams(has_side_effects=EFFECT),
    )(*[pltpu.with_memory_space_constraint(b, pltpu.HBM) for b in bufs], after)


def _ici_wait(kind, started, after, name):
    send_sems, recv_sems, *bufs = started[:-1]
    n = len(bufs) // 2
    npeer = 1 if kind == "sibling" else 3

    def body(*refs):
        src_refs, land_refs = refs[:n], refs[n:2 * n]
        send_sems, recv_sems = refs[2 * n], refs[2 * n + 1]
        x, y, c = _mesh_pos()
        for j, chip in enumerate(_peers(kind, x, y)):
            for k in range(n):
                _ici_copy(kind, src_refs[k], land_refs[k], send_sems.at[npeer * k + j], recv_sems.at[npeer * k + j],(x, y), chip, c).wait_send()
                _ici_copy(kind, src_refs[k], land_refs[k], send_sems.at[npeer * k + j], recv_sems.at[npeer * k + j],chip, (x, y), c).wait_recv()

    out = _call(
        body, name=name, out_shape=[pltpu.HBM(b.shape, b.dtype) for b in bufs],
        in_specs=[HBM] * (2 * n) + [SEM, SEM, ANY], out_specs=[HBM] * (2 * n),
        input_output_aliases={i: i for i in range(2 * n)},
        compiler_params=pltpu.CompilerParams(has_side_effects=EFFECT),
    )(*bufs, send_sems, recv_sems, after)
    return out[:n], out[n:]


def _d2d_gather(lands, after, name):
    n = len(lands)

    def body(*refs):
        in_refs, o_refs = refs[:n], refs[n + 1:2 * n + 1]
        send_sems, recv_sems = refs[2 * n + 1:]
        x, y, c = _mesh_pos()
        copies = [pltpu.make_async_remote_copy(
            src_ref=in_refs[k].at[:, c], dst_ref=o_refs[k].at[:, c], send_sem=send_sems.at[k], recv_sem=recv_sems.at[k],
            device_id=(x, y, 1 - c), device_id_type=MESH) for k in range(n)]
        for cp in copies:
            cp.start()
        for k, cp in enumerate(copies):
            cp.wait_send()
            pltpu.make_async_remote_copy(
                src_ref=in_refs[k].at[:, c], dst_ref=o_refs[k].at[:, 1 - c], send_sem=send_sems.at[k],
                recv_sem=recv_sems.at[k], device_id=(x, y, 1 - c), device_id_type=MESH).wait_recv()

    return _call(
        body, name=name, in_specs=[HBM] * n + [ANY], out_specs=[HBM] * n,
        out_shape=[jax.ShapeDtypeStruct(b.shape, b.dtype) for b in lands],
        input_output_aliases={k: k for k in range(n)},
        scratch_shapes=[pltpu.SemaphoreType.DMA((n,)), pltpu.SemaphoreType.DMA((n,))],
    )(*lands, after)


def _reduce_begin(grads, pos, tag):
    g4s = [g.reshape(N_CHIP, 2, g.shape[0] // N_DEV, g.shape[1]) for g in grads]
    recvs = [lax.empty((N_CHIP,) + g.shape[2:], g.dtype) for g in g4s]
    return _ici_start("sibling", g4s, recvs, pos, name="rs_d2d_start_" + tag)


def _reduce_continue(begun, pos, after, tag):
    g4s, recvs = _ici_wait("sibling", begun, after, name="rs_d2d_wait_" + tag)
    sums, lands = zip(*[_add_sibling(g4, rv, pos, name="rs_add_" + tag) for g4, rv in zip(g4s, recvs)])
    return _ici_start("reduce", sums, lands, pos, name="rs_start_" + tag)


def _adamw_reduced(layer, w, m, v, lands, prev, name):
    L, r, c = w.shape
    tm = _row_tile(r, c, 9)
    c1 = 1.0 - ADAM_B1 ** ADAM_STEP
    c2 = 1.0 - ADAM_B2 ** ADAM_STEP
    n_prev = 0 if prev is None else 4

    def body(w_ref, m_ref, v_ref, l_ref, *refs):
        g_ref, d_ref, mo_ref, vo_ref = refs[n_prev:]
        g_ = l_ref[0].astype(f32)
        for k in range(1, N_CHIP):
            g_ = g_ + l_ref[k].astype(f32)
        m_ = ADAM_B1 * m_ref[...] + (1.0 - ADAM_B1) * g_
        v_ = ADAM_B2 * v_ref[...] + (1.0 - ADAM_B2) * (g_ * g_)
        g_ref[...] = g_
        d_ref[...] = -ADAM_LR * ((m_ / c1) / (jnp.sqrt(v_ / c2) + ADAM_EPS) + ADAM_WD * w_ref[...])
        mo_ref[...] = m_
        vo_ref[...] = v_

    wspec = pl.BlockSpec((None, tm, c), lambda i: (layer, i, 0))
    sspec = pl.BlockSpec((N_CHIP, tm, c), lambda i: (0, i, 0))
    return _call(body, name=name, grid=(r // tm,), in_specs=[wspec] * 3 + [sspec] + [ANY] * n_prev, out_specs=[wspec] * 4,
                 out_shape=[jax.ShapeDtypeStruct((L, r, c), f32)] * 4,
                 input_output_aliases={4 + i: i for i in range(n_prev)},
                 compiler_params=_params(("parallel",), 40 << 20))(w, m, v, lands, *(prev or ()))


def _rope_tables(S):
    rows = S // GRID_W
    row = jnp.repeat(jnp.arange(rows, dtype=f32), GRID_W)
    col = jnp.tile(jnp.arange(GRID_W, dtype=f32), rows)
    nf = HEAD_DIM // 4
    inv = ROPE_THETA ** (-jnp.arange(nf, dtype=f32) / nf)
    ang = jnp.concatenate([row[:, None] * inv, col[:, None] * inv], axis=-1)
    cos, sin = jnp.cos(ang), jnp.sin(ang)
    return jnp.concatenate([cos, cos], axis=-1), jnp.concatenate([-sin, sin], axis=-1)


def _layer_fwd(xin, p, w, more_weights, cos2, sin2):
    sv = {"xin": xin}
    h = sv["h"] = _rms_fwd(xin, p["g_mix"], name="rms_mix")
    proj = functools.partial(_mm, h, w["in"], "nt", bf16)
    q_raw = sv["q_raw"] = proj(n=Q_COLS, b_off=0, name="proj_q")
    kv_raw = sv["kv_raw"] = proj(n=2 * KV_COLS, b_off=OFF_KV, name="proj_kv")
    conv_in = sv["conv_in"] = proj(n=2 * CONV_CH, b_off=OFF_CONV, name="proj_conv")
    sg_in = sv["sg_in"] = proj(n=2 * SG_CH, b_off=OFF_SG, name="proj_sg")
    gl = sv["gl"] = proj(n=3 * D_MODEL, b_off=OFF_GATE, name="proj_gate")
    qr, kr = sv["qr"], sv["kr"] = _qk_fwd(q_raw, kv_raw, p["q_norm_g"], p["k_norm_g"], cos2, sin2, name="qk_fwd")
    o = sv["o"] = _attn_fwd(qr, kr, kv_raw, name="attn_fwd")
    c = sv["c"] = _conv1_fwd(conv_in, w["dw"], p["b_dw"], name="conv1_fwd")
    cz = sv["cz"] = _conv2_fwd(c, p["conv_ln_g"], p["conv_ln_b"], name="conv2_fwd")
    sz = sv["sz"] = _sgu_fwd(sg_in, p["sg_ln_g"], p["sg_ln_b"], p["w_s"], p["b_s"], name="sgu_fwd")
    w = {**w, **more_weights(1, sz)}
    sv["ya"], sv["yc"], sv["ys"], merged = _mixer_out([o, cz, sz], [w["attn_o"], w["conv_o"], w["sg_o"]], gl, p["b_gate"],
                                                      name="mixer_out")
    sv["merged"] = merged
    x1 = sv["x1"] = _mm(merged, w["out"], "nn", f32, res=xin, name="out_proj")
    w = {**w, **more_weights(2, x1)}
    hf = sv["hf"] = _rms_fwd(x1, p["g_ffn"], name="rms_ffn")
    sv["fg"], sv["fu"], act = _ffn_up(hf, w["ff_gate"], w["ff_up"], name="ffn_up")
    sv["act"] = act
    x2 = _mm(act, w["ff_down"], "nn", f32, res=x1, name="ff_down")
    return x2, sv, w


def _layer_bwd(dx2, dx2b, sv, p, w, cos2, sin2, reduce_begin, reduce_continue, last):
    small = {}
    dfg, dfu = _ffn_down_bwd(dx2b, w["ff_down"], sv["fg"], sv["fu"], name="ffn_down_bwd")
    g_down = _mm(sv["act"], dx2b, "tn", bf16, name="g_ff_down")
    dhf = _mm(dfg, w["ff_gate"], "nn", f32, name="d_hf_gate")
    dhf = _mm(dfu, w["ff_up"], "nn", f32, res=dhf, name="d_hf_up")
    g_gate = _mm(dfg, sv["hf"], "tn", bf16, name="g_ff_gate")
    g_up = _mm(dfu, sv["hf"], "tn", bf16, name="g_ff_up")
    zero = reduce_begin("ffn", dict(w_ff_gate=g_gate, w_ff_up=g_up, w_ff_down=g_down))[0, 0]
    dx1, dx1b, small["g_ffn"] = _rms_bwd(sv["x1"], p["g_ffn"] + zero, dhf, dx2, name="rms_ffn_bwd")
    g_out = _mm(sv["merged"], dx1b, "tn", bf16, name="g_out")
    *dgl, dya, dyc, dys, db0, db1, db2 = _merge_bwd_fused(dx1b, w["out"], sv["gl"], p["b_gate"], sv["ya"], sv["yc"], sv["ys"],
                                                        name="merge_bwd")
    small["b_gate"] = jnp.concatenate([db0, db1, db2], axis=1)
    do = _mm(dya, w["attn_o"], "nn", bf16, after=reduce_continue("ffn", dya), name="d_o")
    g_ao = _mm(dya, sv["o"], "tn", bf16, name="g_attn_o")
    dcz = _mm(dyc, w["conv_o"], "nn", bf16, name="d_cz")
    g_co = _mm(dyc, sv["cz"], "tn", bf16, name="g_conv_o")
    dsz = _mm(dys, w["sg_o"], "nn", bf16, name="d_sz")
    g_so = _mm(dys, sv["sz"], "tn", bf16, name="g_sg_o")
    zero = reduce_begin("mix", dict(w_attn_o=g_ao, w_conv_o=g_co, w_sg_o=g_so, w_out=g_out))[0, 0]
    dsu, dsv, small["w_s"], small["b_s"], small["sg_ln_g"], small["sg_ln_b"] = _sgu_bwd(
        sv["sg_in"], dsz, p["sg_ln_g"] + zero, p["sg_ln_b"], p["w_s"], p["w_s_t"], p["b_s"], name="sgu_bwd")
    dc, small["conv_ln_g"], small["conv_ln_b"] = _conv2_bwd(sv["c"], dcz, p["conv_ln_g"], p["conv_ln_b"], name="conv2_bwd")
    da, dgt, small["w_dw"], small["b_dw"] = _conv1_bwd(sv["conv_in"], dc, w["dw"], name="conv1_bwd")
    zero = reduce_continue("mix", da)[0, 0]
    dqr, dkr, dv = _attn_bwd(sv["qr"], sv["kr"], sv["kv_raw"], do, name="attn_bwd")
    dq_raw, dk_raw, small["q_norm_g"], small["k_norm_g"] = _qk_bwd(
        sv["q_raw"], sv["kv_raw"], dqr, dkr, p["q_norm_g"] + zero, p["k_norm_g"], cos2, sin2, name="qk_bwd")
    dproj = jnp.concatenate([dq_raw, dk_raw, dv.astype(bf16), da, dgt, dsu, dsv, *dgl], axis=1)
    g_in = _mm(dproj, sv["h"], "tn", bf16, name="g_in")
    begun = reduce_begin("in", dict(w_in=g_in))
    if last:
        begun = reduce_continue("in", begun)
    dh = _mm(dproj, w["in"], "nn", f32, after=begun, name="d_h")
    zero = begun[0, 0] if last else reduce_continue("in", dh)[0, 0]
    dx, dxb, small["g_mix"] = _rms_bwd(sv["xin"], p["g_mix"] + zero, dh, dx1, name="rms_mix_bwd")
    return dx, dxb, small


SMALL = ("g_mix", "b_gate", "q_norm_g", "k_norm_g", "b_dw", "conv_ln_g", "conv_ln_b", "sg_ln_g", "sg_ln_b",
         "w_s", "b_s", "g_ffn")
PACK_ALIGN = 8 * LANES


def _pack(parts):
    flat = jnp.concatenate([a.reshape(-1).astype(f32) for a in parts])
    pad = -flat.shape[0] % PACK_ALIGN
    return jnp.pad(flat, (0, pad)).reshape(-1, LANES)


def _unpack(buf, shapes):
    flat = buf.reshape(-1)
    out, pos = [], 0
    for shp in shapes:
        size = math.prod(shp)
        out.append(flat[pos:pos + size].reshape(shp))
        pos += size
    return out


def kernel(x, g_mix, w_in, b_gate, q_norm_g, k_norm_g, w_attn_o, w_dw, b_dw, conv_ln_g, conv_ln_b, w_conv_o, sg_ln_g, sg_ln_b, w_s, b_s, w_sg_o, w_out, g_ffn, w_ff_gate, w_ff_up, w_ff_down, g_final, loss_target, m_g_mix, m_w_in, m_b_gate, m_q_norm_g, m_k_norm_g, m_w_attn_o, m_w_dw, m_b_dw, m_conv_ln_g, m_conv_ln_b, m_w_conv_o, m_sg_ln_g, m_sg_ln_b, m_w_s, m_b_s, m_w_sg_o, m_w_out, m_g_ffn, m_w_ff_gate, m_w_ff_up, m_w_ff_down, m_g_final, v_g_mix, v_w_in, v_b_gate, v_q_norm_g, v_k_norm_g, v_w_attn_o, v_w_dw, v_b_dw, v_conv_ln_g, v_conv_ln_b, v_w_conv_o, v_sg_ln_g, v_sg_ln_b, v_w_s, v_b_s, v_w_sg_o, v_w_out, v_g_ffn, v_w_ff_gate, v_w_ff_up, v_w_ff_down, v_g_final):
    weights = dict(g_mix=g_mix, w_in=w_in, b_gate=b_gate, q_norm_g=q_norm_g, k_norm_g=k_norm_g, w_attn_o=w_attn_o,
                   w_dw=w_dw, b_dw=b_dw, conv_ln_g=conv_ln_g, conv_ln_b=conv_ln_b, w_conv_o=w_conv_o, sg_ln_g=sg_ln_g,
                   sg_ln_b=sg_ln_b, w_s=w_s, b_s=b_s, w_sg_o=w_sg_o, w_out=w_out, g_ffn=g_ffn, w_ff_gate=w_ff_gate,
                   w_ff_up=w_ff_up, w_ff_down=w_ff_down, g_final=g_final)
    mom_m = dict(g_mix=m_g_mix, w_in=m_w_in, b_gate=m_b_gate, q_norm_g=m_q_norm_g, k_norm_g=m_k_norm_g,
                 w_attn_o=m_w_attn_o, w_dw=m_w_dw, b_dw=m_b_dw, conv_ln_g=m_conv_ln_g, conv_ln_b=m_conv_ln_b,
                 w_conv_o=m_w_conv_o, sg_ln_g=m_sg_ln_g, sg_ln_b=m_sg_ln_b, w_s=m_w_s, b_s=m_b_s, w_sg_o=m_w_sg_o,
                 w_out=m_w_out, g_ffn=m_g_ffn, w_ff_gate=m_w_ff_gate, w_ff_up=m_w_ff_up, w_ff_down=m_w_ff_down,
                 g_final=m_g_final)
    mom_v = dict(g_mix=v_g_mix, w_in=v_w_in, b_gate=v_b_gate, q_norm_g=v_q_norm_g, k_norm_g=v_k_norm_g,
                 w_attn_o=v_w_attn_o, w_dw=v_w_dw, b_dw=v_b_dw, conv_ln_g=v_conv_ln_g, conv_ln_b=v_conv_ln_b,
                 w_conv_o=v_w_conv_o, sg_ln_g=v_sg_ln_g, sg_ln_b=v_sg_ln_b, w_s=v_w_s, b_s=v_b_s, w_sg_o=v_w_sg_o,
                 w_out=v_w_out, g_ffn=v_g_ffn, w_ff_gate=v_w_ff_gate, w_ff_up=v_w_ff_up, w_ff_down=v_w_ff_down,
                 g_final=v_g_final)
    S, D = x.shape[1], x.shape[2]
    xi, yi, ci = _mesh_pos()
    me = 4 * xi + 2 * yi + ci
    pos = jnp.stack([ci, 2 * xi + yi]).astype(jnp.int32)
    cos2, sin2 = _rope_tables(S)

    big = ("w_in", "w_attn_o", "w_conv_o", "w_sg_o", "w_out", "w_ff_gate", "w_ff_up", "w_ff_down")
    transposed = {"w_in", "w_attn_o", "w_conv_o", "w_sg_o", "w_ff_gate", "w_ff_up"}
    groups = (("in", "dw"), ("attn_o", "conv_o", "sg_o", "out"), ("ff_gate", "ff_up", "ff_down"))
    P, shards = [], []
    for l in range(DEPTH):
        sh = {n[2:]: (weights[n][l].T if n in transposed else weights[n][l]).astype(bf16) for n in big}
        sh["dw"] = jnp.pad(w_dw[l].reshape(CONV_W, LANES), ((0, CONV_WP - CONV_W), (0, 0)))
        shards.append(sh)
        p = {n: weights[n][l].reshape(1, -1) for n in SMALL if n not in ("w_s", "b_s")}
        p["w_s"] = w_s[l]
        p["w_s_t"] = jnp.swapaxes(w_s[l], 1, 2)
        p["b_s"] = b_s[l].reshape(SG_G, SG_CHUNK, 1)
        P.append(p)

    gathers = {}

    def start_gather(l, gi, after):
        srcs = [shards[l][n] for n in groups[gi]]
        lands = [lax.dynamic_update_index_in_dim(lax.empty((N_DEV,) + s.shape, s.dtype), s, me, 0) for s in srcs]
        gathers[l, gi] = _ici_start("gather", srcs, lands, after, name=f"ag_start_{l}{gi}")
        return gathers[l, gi][-1]

    def gathered(l, gi, after):
        srcs, lands = _ici_wait("gather", gathers[l, gi], after, name=f"ag_wait_{l}{gi}")
        after = srcs[0]
        if gi == len(groups) - 1 and l + 1 < DEPTH:
            for gj in range(len(groups)):
                after = start_gather(l + 1, gj, after)
        full = _d2d_gather([b.reshape(N_CHIP, 2, *b.shape[1:]) for b in lands], after, name=f"ag_d2d_{gi}")
        return {n: f.reshape(-1, f.shape[3]) for n, f in zip(groups[gi], full)}

    all_started = cos2
    for gi in range(len(groups)):
        all_started = start_gather(0, gi, all_started)

    h = x.reshape(S, D)
    saved, W = [], []
    for l in range(DEPTH):
        first = gathered(l, 0, all_started if l == 0 else h)
        if l == 0:
            P[l]["g_mix"] = P[l]["g_mix"] + all_started[0, 0]
        h, sv, w = _layer_fwd(h, P[l], first, functools.partial(lambda gi, z, l: gathered(l, gi, z), l=l), cos2, sin2)
        saved.append(sv)
        W.append(w)
    dx, dxb, sq, g_final_part = _final_loss(h, g_final.reshape(1, D), loss_target.reshape(S, D), name="final_loss")
    loss = lax.psum(0.5 * jnp.sum(sq) / D, ("x", "y", "c"))

    begun, reductions, small_grads = {}, {}, [None] * DEPTH
    for l in reversed(range(DEPTH)):
        def reduce_begin(group, grads, l=l):
            begun[l, group] = (tuple(grads), _reduce_begin(list(grads.values()), pos, tag=f"{group}{l}"))
            return begun[l, group][1][-1]

        def reduce_continue(group, after, l=l):
            names, started = begun[l, group]
            reductions[l, group] = (names, _reduce_continue(started, pos, after, tag=f"{group}{l}"))
            return reductions[l, group][1][-1]

        dx, dxb, small_grads[l] = _layer_bwd(dx, dxb, saved[l], P[l], W[l], cos2, sin2, reduce_begin, reduce_continue,
                                            last=(l == 0))
    grad_x = dx.reshape(x.shape)

    small_shapes = [weights[n].shape for n in SMALL] + [g_final.shape, (DEPTH, CONV_CH // LANES, CONV_WP, LANES)]
    parts = [jnp.stack([small_grads[l][n].reshape(weights[n].shape[1:]) for l in range(DEPTH)]) for n in SMALL]
    parts += [g_final_part.reshape(g_final.shape), jnp.stack([small_grads[l]["w_dw"] for l in range(DEPTH)])]
    packed = _pack(parts)
    packed_land = lax.dynamic_update_index_in_dim(lax.empty((N_DEV,) + packed.shape, f32), packed, me, 0)
    small_started = _ici_start("gather", [packed], [packed_land], dx, name="gather_small_start")

    grads_out, delta, new_m, new_v = {}, {}, {}, {}
    swap = lambda a: jnp.swapaxes(a, 1, 2)

    def update(n, lands):
        as_arrives = n not in transposed or weights[n].shape[2] % LANES != 0
        if as_arrives:
            to_arrival = swap if n in transposed else (lambda a: a)
            out = None
            for l in reversed(range(DEPTH)):
                out = _adamw_reduced(l, to_arrival(weights[n]), to_arrival(mom_m[n]), to_arrival(mom_v[n]),
                                     lands[l], out, name=f"adamw_{n}_{l}")
            grads_out[n], delta[n], new_m[n], new_v[n] = [to_arrival(o) for o in out]
            return out[1]
        g = jnp.stack([_sum_slots(lands[l], name="rs_sum_" + n) for l in range(DEPTH)])
        grads_out[n] = swap(g)
        delta[n], new_m[n], new_v[n] = _adamw(weights[n], grads_out[n], mom_m[n], mom_v[n], name="adamw_" + n)
        return delta[n]

    after = small_started[-1]
    for group in ("ffn", "mix", "in"):
        names = reductions[0, group][0]
        arrived = [_ici_wait("reduce", reductions[l, group][1], after, name=f"rs_wait_{group}{l}") for l in range(DEPTH)]
        for i, n in enumerate(names):
            after = update(n, [arrived[l][1][i] for l in range(DEPTH)])

    _, small_lands = _ici_wait("gather", small_started, after, name="gather_small_wait")
    small_full = _d2d_gather([small_lands[0].reshape(N_CHIP, 2, *packed.shape)], after, name="gather_small_d2d")[0]
    total = _sum_slots(small_full.reshape(N_DEV, *packed.shape), name="sum_small")
    small_total = _unpack(total, small_shapes)
    grads_out.update(zip(SMALL + ("g_final",), small_total[:-1]))
    dw_full = small_total[-1]
    grads_out["w_dw"] = lax.dynamic_index_in_dim(dw_full, me, axis=1, keepdims=False)[:, :CONV_W].reshape(w_dw.shape)

    rep = tuple(n for n in SMALL if n != "w_s") + ("g_final",)
    rep_shapes = [weights[n].shape for n in rep]
    packs = [_pack([src[n] for n in rep])[None] for src in (weights, grads_out, mom_m, mom_v)]
    for dst, buf in zip((delta, new_m, new_v), _adamw(*packs, name="adamw_small")):
        dst.update(zip(rep, _unpack(buf[0], rep_shapes)))
    for n, shp in (("w_dw", (1, DEPTH * CONV_W, LANES)), ("w_s", (DEPTH, SG_G * SG_CHUNK, SG_CHUNK))):
        upd = _adamw(*[src[n].reshape(shp) for src in (weights, grads_out, mom_m, mom_v)], name="adamw_" + n)
        for dst, buf in zip((delta, new_m, new_v), upd):
            dst[n] = buf.reshape(weights[n].shape)

    order = ("g_mix", "w_in", "b_gate", "q_norm_g", "k_norm_g", "w_attn_o", "w_dw", "b_dw", "conv_ln_g", "conv_ln_b",
             "w_conv_o", "sg_ln_g", "sg_ln_b", "w_s", "b_s", "w_sg_o", "w_out", "g_ffn", "w_ff_gate", "w_ff_up",
             "w_ff_down", "g_final")
    return (loss, grad_x, *[grads_out[n] for n in order], *[delta[n] for n in order],
            *[new_m[n] for n in order], *[new_v[n] for n in order])
```

```python
import functools
import math

import jax
import jax.numpy as jnp
from jax import lax
from jax.experimental import pallas as pl
from jax.experimental.pallas import tpu as pltpu

f32, bf16 = jnp.float32, jnp.bfloat16

D_MODEL = 2048
SEQ = 2048
DEPTH = 2
GRID_W = 64
HEAD_DIM = 128
LANES = 128
N_Q = (D_MODEL // 2) // HEAD_DIM
N_KV = N_Q // 4
GRP = N_Q // N_KV
Q_COLS = N_Q * HEAD_DIM
KV_COLS = N_KV * HEAD_DIM
CONV_CH = D_MODEL // 2
CONV_W = 31
CONV_PAD = CONV_W // 2
CONV_WP = 32
SG_CH = D_MODEL // 2
SG_G = SG_CH // LANES
SG_CHUNK = 128
D_FF = -(-8 * D_MODEL // (3 * 256)) * 256
OFF_KV = Q_COLS
OFF_CONV = OFF_KV + 2 * KV_COLS
OFF_SG = OFF_CONV + 2 * CONV_CH
OFF_GATE = OFF_SG + 2 * SG_CH
IN_COLS = OFF_GATE + 3 * D_MODEL
ROPE_THETA = 10000.0
SCALE = HEAD_DIM ** -0.5
N_DEV = 8
N_CHIP = 4

ADAM_LR, ADAM_B1, ADAM_B2, ADAM_EPS, ADAM_WD, ADAM_STEP = 0.001, 0.9, 0.999, 1e-08, 0.01, 10

VMEM_BYTES_V7X = 64 << 20
VMEM_CAP = VMEM_BYTES_V7X - (6 << 20)
MESH = pl.DeviceIdType.MESH
HBM = pl.BlockSpec(memory_space=pltpu.HBM)


def _in_hbm(a):
    if isinstance(a, jax.Array) and jnp.issubdtype(a.dtype, jnp.floating) and a.size * a.dtype.itemsize >= (1 << 20):
        return pltpu.with_memory_space_constraint(a, pltpu.HBM)
    return a


def _out_hbm(s):
    if isinstance(s, jax.ShapeDtypeStruct) and math.prod(s.shape) * jnp.dtype(s.dtype).itemsize >= (1 << 20):
        return pltpu.HBM(s.shape, s.dtype)
    return s


def _call(body, **kw):
    shapes = kw.pop("out_shape")
    shapes = type(shapes)(_out_hbm(s) for s in shapes) if isinstance(shapes, (list, tuple)) else _out_hbm(shapes)
    call = pl.pallas_call(body, out_shape=shapes, **kw)
    return lambda *args: call(*[_in_hbm(a) for a in args])


def _pick(n, cands):
    for c in cands:
        if n % c == 0:
            return c
    raise ValueError((n, cands))


def _params(sem, vmem_bytes):
    return pltpu.CompilerParams(dimension_semantics=sem, vmem_limit_bytes=int(min(max(vmem_bytes, 16 << 20), VMEM_CAP)))


def _mm(a, b, form, out_dtype, *, n=None, b_off=0, res=None, after=None, name):
    if form == "tn":
        K, M = a.shape
    else:
        M, K = a.shape
    N = n if n is not None else (b.shape[0] if form == "nt" else b.shape[1])
    if K <= 2048:
        tk = K
        if form == "tn":
            tm = _pick(M, (512, 256, 128))
            tn = N if N <= 2048 else _pick(N, (1024, 512, 256, 128))
        else:
            tm = M if M <= 2048 else _pick(M, (2048, 1024, 512))
            tn = _pick(math.gcd(N, b_off) if b_off else N, (256, 128) if res is not None else (512, 256, 128))
    else:
        tk = max(t for t in range(LANES, 3072 + 1, LANES) if K % t == 0)
        tm = _pick(M, (1024, 512, 256, 128))
        tn = _pick(math.gcd(N, b_off) if b_off else N, (1024, 512, 256, 128))
    assert b_off % tn == 0
    off = b_off // tn
    nk = K // tk
    if form == "tn":
        a_spec = pl.BlockSpec((tk, tm), lambda i, j, k: (k, i))
    else:
        a_spec = pl.BlockSpec((tm, tk), lambda i, j, k: (i, k))
    if form == "nt":
        b_spec = pl.BlockSpec((tn, tk), lambda i, j, k: (j + off, k))
    else:
        b_spec = pl.BlockSpec((tk, tn), lambda i, j, k: (k, j + off))
    dims = {"nn": ((1,), (0,)), "nt": ((1,), (1,)), "tn": ((0,), (0,))}[form]
    has_res = res is not None

    def body(*refs):
        if after is not None:
            refs = refs[1:]
        if has_res:
            a_ref, b_ref, r_ref, o_ref = refs[:4]
        else:
            a_ref, b_ref, o_ref = refs[:3]
        p = lax.dot_general(a_ref[...], b_ref[...], (dims, ((), ())), preferred_element_type=f32)

        def finish(acc):
            if has_res:
                acc = acc + r_ref[...].astype(f32)
            o_ref[...] = acc.astype(o_ref.dtype)

        if nk == 1:
            finish(p)
        else:
            acc_ref = refs[-1]
            k = pl.program_id(2)

            @pl.when(k == 0)
            def _():
                acc_ref[...] = p

            @pl.when(k > 0)
            def _():
                acc_ref[...] += p

            @pl.when(k == nk - 1)
            def _():
                finish(acc_ref[...])

    in_specs = [a_spec, b_spec]
    args = [a, b]
    osz = jnp.dtype(out_dtype).itemsize
    vmem = 2 * (tm * tk * 2 + tk * tn * 2 + tm * tn * osz) + 2 * tm * tn * 4
    if has_res:
        in_specs.append(pl.BlockSpec((tm, tn), lambda i, j, k: (i, j)))
        args.append(res)
        vmem += 2 * tm * tn * res.dtype.itemsize
    scratch = []
    if nk > 1:
        scratch.append(pltpu.VMEM((tm, tn), f32))
        vmem += tm * tn * 4
    if after is not None:
        in_specs.insert(0, pl.BlockSpec(memory_space=pl.ANY))
        args.insert(0, after)
    return _call(
        body, name=name, grid=(M // tm, N // tn, nk),
        in_specs=in_specs, out_specs=pl.BlockSpec((tm, tn), lambda i, j, k: (i, j)),
        out_shape=jax.ShapeDtypeStruct((M, N), out_dtype), scratch_shapes=scratch,
        compiler_params=_params(("parallel", "parallel", "arbitrary"), vmem + (8 << 20)),
    )(*args)


EPI_TN = 256


def _mm_epi(a, bs, form, extras, out_dtypes, n_sums, fn, name):
    a_list = list(a) if isinstance(a, (list, tuple)) else [a]
    M, K = a_list[0].shape
    N = bs[0].shape[0] if form == "nt" else bs[0].shape[1]
    tn = EPI_TN
    assert K <= 2048 and N % tn == 0 and len(a_list) in (1, len(bs))
    dims = ((1,), (1,)) if form == "nt" else ((1,), (0,))
    na, nb, ne = len(a_list), len(bs), len(extras)

    def body(*refs):
        a_refs, b_refs = refs[:na], refs[na:na + nb]
        e_refs, o_refs = refs[na + nb:na + nb + ne], refs[na + nb + ne:]
        avs = [r[...] for r in a_refs] * (nb // na)
        ps = [lax.dot_general(av, b[...], (dims, ((), ())), preferred_element_type=f32) for av, b in zip(avs, b_refs)]
        for o_ref, o in zip(o_refs, fn(ps, [e[...] for e in e_refs])):
            o_ref[...] = o.astype(o_ref.dtype)

    in_specs = [pl.BlockSpec((M, K), lambda j: (0, 0), pipeline_mode=pl.Buffered(1)) for _ in a_list]
    in_specs += [pl.BlockSpec((tn, K), lambda j: (j, 0)) if form == "nt" else pl.BlockSpec((K, tn), lambda j: (0, j))
                 for _ in bs]
    for arr, first in extras:
        assert first % tn == 0
        in_specs.append(pl.BlockSpec((arr.shape[0], tn), functools.partial(lambda j, o: (0, j + o), o=first // tn)))
    out_specs = [pl.BlockSpec((M, tn), lambda j: (0, j))] * len(out_dtypes) + [pl.BlockSpec((1, tn), lambda j: (0, j))] * n_sums
    out_shape = [jax.ShapeDtypeStruct((M, N), dt) for dt in out_dtypes] + [jax.ShapeDtypeStruct((1, N), f32)] * n_sums
    tiles = sum(arr.shape[0] * tn * arr.dtype.itemsize for arr, _ in extras) + sum(M * tn * jnp.dtype(dt).itemsize for dt in out_dtypes)
    vmem = na * M * K * 2 + 2 * nb * tn * K * 2 + 2 * tiles + (nb + 6) * M * tn * 4
    return _call(body, name=name, grid=(N // tn,), in_specs=in_specs, out_specs=out_specs, out_shape=out_shape,
                 compiler_params=_params(("parallel",), vmem + (8 << 20)))(*a_list, *bs, *[arr for arr, _ in extras])


def _ffn_up(hf, wt_gate, wt_up, name):
    def fn(ps, _):
        g, u = ps[0].astype(bf16), ps[1].astype(bf16)
        gf = g.astype(f32)
        return g, u, gf * jax.nn.sigmoid(gf) * u.astype(f32)

    return _mm_epi(hf, [wt_gate, wt_up], "nt", [], [bf16] * 3, 0, fn, name)


def _ffn_down_bwd(dx2b, w_down, fg, fu, name):
    def fn(ps, es):
        d, g = ps[0], es[0].astype(f32)
        sg = jax.nn.sigmoid(g)
        return d * es[1].astype(f32) * sg * (1.0 + g * (1.0 - sg)), d * g * sg

    return _mm_epi(dx2b, [w_down], "nt", [(fg, 0), (fu, 0)], [bf16] * 2, 0, fn, name)


def _mixer_out(branches, wts, gl, b_gate, name):
    D = wts[0].shape[0]

    def fn(ps, es):
        ys = [p_.astype(bf16) for p_ in ps]
        merged = None
        for i in range(3):
            term = jax.nn.sigmoid(es[i].astype(f32) + es[3 + i]) * ys[i].astype(f32)
            merged = term if merged is None else merged + term
        return ys + [merged]

    extras = [(gl, i * D) for i in range(3)] + [(b_gate, i * D) for i in range(3)]
    return _mm_epi(branches, wts, "nt", extras, [bf16] * 4, 0, fn, name)


def _merge_bwd_fused(dx1b, w_out, gl, b_gate, ya, yc, ys, name):
    D = ya.shape[1]

    def fn(ps, es):
        dm_, outs, sums = ps[0], [], []
        for i in range(3):
            gate = jax.nn.sigmoid(es[i].astype(f32) + es[3 + i])
            dlog = dm_ * es[6 + i].astype(f32) * gate * (1.0 - gate)
            outs.append((dlog, dm_ * gate))
            sums.append(jnp.sum(dlog, axis=0, keepdims=True))
        return [o[0] for o in outs] + [o[1] for o in outs] + sums

    extras = [(gl, i * D) for i in range(3)] + [(b_gate, i * D) for i in range(3)] + [(ya, 0), (yc, 0), (ys, 0)]
    return _mm_epi(dx1b, [w_out], "nt", extras, [bf16] * 6, 3, fn, name)


def _rows(body, ins, outs, *, tm, name, vmem=40 << 20):
    nrows = next(s[1].shape[0] for s in ins if s[0] == "r")
    in_specs, args = [], []
    for s in ins:
        arr = s[1]
        if s[0] == "r":
            w = s[2] if len(s) > 2 else arr.shape[1]
            cb = s[3] if len(s) > 3 else 0
            in_specs.append(pl.BlockSpec((tm, w), functools.partial(lambda i, cb: (i, cb), cb=cb)))
        else:
            in_specs.append(pl.BlockSpec(arr.shape, functools.partial(lambda i, nd: (0,) * nd, nd=arr.ndim)))
        args.append(arr)
    out_specs, out_shape = [], []
    for s in outs:
        if s[0] == "r":
            out_specs.append(pl.BlockSpec((tm, s[1]), lambda i: (i, 0)))
            out_shape.append(jax.ShapeDtypeStruct((nrows, s[1]), s[2]))
        else:
            out_specs.append(pl.BlockSpec(s[1], functools.partial(lambda i, nd: (0,) * nd, nd=len(s[1]))))
            out_shape.append(jax.ShapeDtypeStruct(s[1], s[2]))
    return _call(body, name=name, grid=(nrows // tm,), in_specs=in_specs, out_specs=out_specs,
                 out_shape=out_shape, compiler_params=_params(("arbitrary",), vmem))(*args)


def _accumulate(ref, part):
    i = pl.program_id(0)

    @pl.when(i == 0)
    def _():
        ref[...] = part

    @pl.when(i > 0)
    def _():
        ref[...] += part


def _rms_stats(x):
    r = lax.rsqrt(jnp.mean(x * x, axis=-1, keepdims=True) + 1e-6)
    return r, x * r


def _rms_fwd(x, g, name):
    def body(x_ref, g_ref, o_ref):
        _, xn = _rms_stats(x_ref[...])
        o_ref[...] = (xn * g_ref[...]).astype(o_ref.dtype)

    return _rows(body, [("r", x), ("f", g)], [("r", x.shape[1], bf16)], tm=min(256, x.shape[0]), name=name)[0]


def _rms_bwd(x, g, dh, dres, name):
    D = x.shape[1]

    def body(x_ref, g_ref, dh_ref, dr_ref, dx_ref, dxb_ref, dg_ref):
        r, xn = _rms_stats(x_ref[...])
        dy = dh_ref[...].astype(f32)
        dxn = dy * g_ref[...]
        dx = dr_ref[...] + r * (dxn - xn * jnp.mean(dxn * xn, axis=-1, keepdims=True))
        dx_ref[...] = dx
        dxb_ref[...] = dx.astype(bf16)
        _accumulate(dg_ref, jnp.sum(dy * xn, axis=0, keepdims=True))

    return _rows(body, [("r", x), ("f", g), ("r", dh), ("r", dres)],
                 [("r", D, f32), ("r", D, bf16), ("a", (1, D), f32)], tm=min(256, x.shape[0]), name=name)


def _final_loss(x, g, tgt, name):
    D = x.shape[1]

    def body(x_ref, g_ref, t_ref, dx_ref, dxb_ref, sq_ref, dg_ref):
        r, xn = _rms_stats(x_ref[...])
        gain = g_ref[...]
        diff = xn * gain - t_ref[...]
        dy = diff * (1.0 / D)
        dxn = dy * gain
        dx = r * (dxn - xn * jnp.mean(dxn * xn, axis=-1, keepdims=True))
        dx_ref[...] = dx
        dxb_ref[...] = dx.astype(bf16)
        _accumulate(sq_ref, jnp.sum(diff * diff, axis=0, keepdims=True))
        _accumulate(dg_ref, jnp.sum(dy * xn, axis=0, keepdims=True))

    return _rows(body, [("r", x), ("f", g), ("r", tgt)],
                 [("r", D, f32), ("r", D, bf16), ("a", (1, D), f32), ("a", (1, D), f32)],
                 tm=min(256, x.shape[0]), name=name)


def _qk_fwd(q_raw, kv_raw, qg, kg, cos2, sin2, name):
    def body(q_ref, k_ref, qg_ref, kg_ref, c_ref, s_ref, qo_ref, ko_ref):
        c, s = c_ref[...], s_ref[...]

        def head(src, gain, dst, h):
            cols = slice(h * HEAD_DIM, (h + 1) * HEAD_DIM)
            _, xn = _rms_stats(src[:, cols].astype(f32))
            y = xn * gain
            dst[:, cols] = (y * c + pltpu.roll(y, HEAD_DIM // 2, 1) * s).astype(dst.dtype)

        for h in range(N_Q):
            head(q_ref, qg_ref[...], qo_ref, h)
        for h in range(N_KV):
            head(k_ref, kg_ref[...], ko_ref, h)

    return _rows(body, [("r", q_raw), ("r", kv_raw, KV_COLS, 0), ("f", qg), ("f", kg), ("r", cos2), ("r", sin2)],
                 [("r", Q_COLS, bf16), ("r", KV_COLS, bf16)], tm=min(256, q_raw.shape[0]), name=name)


def _qk_bwd(q_raw, kv_raw, dqr, dkr, qg, kg, cos2, sin2, name):
    def body(q_ref, k_ref, dq_ref, dk_ref, qg_ref, kg_ref, c_ref, s_ref, dqo_ref, dko_ref, dqg_ref, dkg_ref):
        c, s = c_ref[...], s_ref[...]

        def head(src, dsrc, gain, dst, h):
            cols = slice(h * HEAD_DIM, (h + 1) * HEAD_DIM)
            r, xn = _rms_stats(src[:, cols].astype(f32))
            do = dsrc[:, cols].astype(f32)
            dy = do * c + pltpu.roll(do * s, HEAD_DIM // 2, 1)
            dxn = dy * gain
            dst[:, cols] = (r * (dxn - xn * jnp.mean(dxn * xn, axis=-1, keepdims=True))).astype(dst.dtype)
            return jnp.sum(dy * xn, axis=0, keepdims=True)

        dq_gain = head(q_ref, dq_ref, qg_ref[...], dqo_ref, 0)
        for h in range(1, N_Q):
            dq_gain = dq_gain + head(q_ref, dq_ref, qg_ref[...], dqo_ref, h)
        dk_gain = head(k_ref, dk_ref, kg_ref[...], dko_ref, 0)
        for h in range(1, N_KV):
            dk_gain = dk_gain + head(k_ref, dk_ref, kg_ref[...], dko_ref, h)
        _accumulate(dqg_ref, dq_gain)
        _accumulate(dkg_ref, dk_gain)

    return _rows(body, [("r", q_raw), ("r", kv_raw, KV_COLS, 0), ("r", dqr), ("r", dkr), ("f", qg), ("f", kg),
                        ("r", cos2), ("r", sin2)],
                 [("r", Q_COLS, bf16), ("r", KV_COLS, bf16), ("a", (1, HEAD_DIM), f32), ("a", (1, HEAD_DIM), f32)],
                 tm=min(256, q_raw.shape[0]), name=name)


def _softmax_rows(q, k):
    s = lax.dot_general(q, k, (((1,), (1,)), ((), ())), preferred_element_type=f32) * (SCALE * math.log2(math.e))
    p = jnp.exp2(s - jnp.max(s, axis=-1, keepdims=True))
    return p * (1.0 / jnp.sum(p, axis=-1, keepdims=True))


def _head_cols(g):
    return slice(g * HEAD_DIM, (g + 1) * HEAD_DIM)


def _attn_fwd(qr, kr, kv_raw, name):
    S = qr.shape[0]
    tq = min(256, S)

    def body(q_ref, k_ref, v_ref, o_ref):
        k, v = k_ref[...], v_ref[...]
        for g in range(GRP):
            p = _softmax_rows(q_ref[:, _head_cols(g)], k)
            o_ref[:, _head_cols(g)] = jnp.dot(p.astype(bf16), v, preferred_element_type=f32).astype(o_ref.dtype)

    return _call(
        body, name=name, grid=(N_KV, S // tq),
        in_specs=[pl.BlockSpec((tq, GRP * HEAD_DIM), lambda kv, i: (i, kv)),
                  pl.BlockSpec((S, HEAD_DIM), lambda kv, i: (0, kv)),
                  pl.BlockSpec((S, HEAD_DIM), lambda kv, i: (0, N_KV + kv))],
        out_specs=pl.BlockSpec((tq, GRP * HEAD_DIM), lambda kv, i: (i, kv)),
        out_shape=jax.ShapeDtypeStruct((S, Q_COLS), bf16),
        compiler_params=_params(("parallel", "arbitrary"), 4 * GRP * tq * S * 4 + (8 << 20)),
    )(qr, kr, kv_raw)


def _attn_bwd(qr, kr, kv_raw, do, name):
    S = qr.shape[0]
    tq = min(256, S)

    def body(q_ref, k_ref, v_ref, do_ref, dq_ref, dk_ref, dv_ref):
        first = pl.program_id(1) == 0
        k, v = k_ref[...], v_ref[...]
        dv_part = dk_part = None
        for g in range(GRP):
            q, do_ = q_ref[:, _head_cols(g)], do_ref[:, _head_cols(g)]
            p = _softmax_rows(q, k)
            dp = lax.dot_general(do_, v, (((1,), (1,)), ((), ())), preferred_element_type=f32)
            ds = (p * (dp - jnp.sum(dp * p, axis=-1, keepdims=True)) * SCALE).astype(bf16)
            dq_ref[:, _head_cols(g)] = jnp.dot(ds, k, preferred_element_type=f32).astype(dq_ref.dtype)
            dv_g = lax.dot_general(p.astype(bf16), do_, (((0,), (0,)), ((), ())), preferred_element_type=f32)
            dk_g = lax.dot_general(ds, q, (((0,), (0,)), ((), ())), preferred_element_type=f32)
            dv_part = dv_g if g == 0 else dv_part + dv_g
            dk_part = dk_g if g == 0 else dk_part + dk_g

        @pl.when(first)
        def _():
            dv_ref[...] = dv_part
            dk_ref[...] = dk_part

        @pl.when(jnp.logical_not(first))
        def _():
            dv_ref[...] += dv_part
            dk_ref[...] += dk_part

    qspec = pl.BlockSpec((tq, GRP * HEAD_DIM), lambda kv, i: (i, kv))
    return _call(
        body, name=name, grid=(N_KV, S // tq),
        in_specs=[qspec, pl.BlockSpec((S, HEAD_DIM), lambda kv, i: (0, kv)),
                  pl.BlockSpec((S, HEAD_DIM), lambda kv, i: (0, N_KV + kv)), qspec],
        out_specs=[qspec, pl.BlockSpec((S, HEAD_DIM), lambda kv, i: (0, kv)),
                   pl.BlockSpec((S, HEAD_DIM), lambda kv, i: (0, kv))],
        out_shape=[jax.ShapeDtypeStruct((S, Q_COLS), bf16), jax.ShapeDtypeStruct((S, KV_COLS), f32),
                   jax.ShapeDtypeStruct((S, KV_COLS), f32)],
        compiler_params=_params(("parallel", "arbitrary"), 6 * GRP * tq * S * 4 + (8 << 20)),
    )(qr, kr, kv_raw, do)


CONV_HALO = 16


def _fill_padded(pad_ref, val, S):
    pad_ref[pl.ds(0, CONV_HALO), :] = jnp.zeros((CONV_HALO, LANES), f32)
    pad_ref[pl.ds(CONV_HALO + S, CONV_HALO), :] = jnp.zeros((CONV_HALO, LANES), f32)
    pad_ref[pl.ds(CONV_HALO, S), :] = val


def _group_specs(S, n_groups, second_half):
    return pl.BlockSpec((S, LANES), functools.partial(lambda g, o: (0, g + o), o=n_groups if second_half else 0))


def _conv1_fwd(conv_in, wdw, b_dw, name):
    S = conv_in.shape[0]
    ng = CONV_CH // LANES
    R = min(256, S)

    def body(a_ref, g_ref, w_ref, b_ref, o_ref, pad_ref):
        z = a_ref[...].astype(f32) * jax.nn.sigmoid(g_ref[...].astype(f32))
        _fill_padded(pad_ref, z, S)
        for r in range(S // R):
            acc = jnp.zeros((R, LANES), f32) + b_ref[...]
            for j in range(CONV_W):
                acc = acc + w_ref[pl.ds(j, 1), :] * pad_ref[pl.ds(r * R + CONV_HALO - CONV_PAD + j, R), :]
            o_ref[pl.ds(r * R, R), :] = acc

    return _call(
        body, name=name, grid=(ng,),
        in_specs=[_group_specs(S, ng, False), _group_specs(S, ng, True),
                  pl.BlockSpec((CONV_WP, LANES), lambda g: (g, 0)), pl.BlockSpec((1, LANES), lambda g: (0, g))],
        out_specs=pl.BlockSpec((S, LANES), lambda g: (0, g)),
        out_shape=jax.ShapeDtypeStruct((S, CONV_CH), f32),
        scratch_shapes=[pltpu.VMEM((S + 2 * CONV_HALO, LANES), f32)],
        compiler_params=_params(("parallel",), 24 << 20),
    )(conv_in, conv_in, wdw, b_dw)


def _conv1_bwd(conv_in, dc, wdw, name):
    S = conv_in.shape[0]
    ng = CONV_CH // LANES
    R = min(256, S)

    def body(a_ref, g_ref, w_ref, dc_ref, da_ref, dg_ref, dw_ref, db_ref, padz_ref, padd_ref):
        a = a_ref[...].astype(f32)
        sg = jax.nn.sigmoid(g_ref[...].astype(f32))
        _fill_padded(padz_ref, a * sg, S)
        _fill_padded(padd_ref, dc_ref[...], S)
        for r in range(S // R):
            dz = jnp.zeros((R, LANES), f32)
            for j in range(CONV_W):
                dz = dz + w_ref[pl.ds(j, 1), :] * padd_ref[pl.ds(r * R + CONV_HALO + CONV_PAD - j, R), :]
            rows = pl.ds(r * R, R)
            ar, sr = a_ref[rows, :].astype(f32), jax.nn.sigmoid(g_ref[rows, :].astype(f32))
            da_ref[rows, :] = (dz * sr).astype(da_ref.dtype)
            dg_ref[rows, :] = (dz * ar * sr * (1.0 - sr)).astype(dg_ref.dtype)
        for j in range(CONV_W):
            tot = jnp.zeros((1, LANES), f32)
            for r in range(S // R):
                tot = tot + jnp.sum(dc_ref[pl.ds(r * R, R), :] * padz_ref[pl.ds(r * R + CONV_HALO - CONV_PAD + j, R), :],
                                    axis=0, keepdims=True)
            dw_ref[pl.ds(j, 1), :] = tot
        dw_ref[pl.ds(CONV_W, CONV_WP - CONV_W), :] = jnp.zeros((CONV_WP - CONV_W, LANES), f32)
        db_ref[...] = jnp.sum(dc_ref[...], axis=0, keepdims=True)

    return _call(
        body, name=name, grid=(ng,),
        in_specs=[_group_specs(S, ng, False), _group_specs(S, ng, True),
                  pl.BlockSpec((CONV_WP, LANES), lambda g: (g, 0)), pl.BlockSpec((S, LANES), lambda g: (0, g))],
        out_specs=[pl.BlockSpec((S, LANES), lambda g: (0, g)), pl.BlockSpec((S, LANES), lambda g: (0, g)),
                   pl.BlockSpec((CONV_WP, LANES), lambda g: (g, 0)), pl.BlockSpec((1, LANES), lambda g: (0, g))],
        out_shape=[jax.ShapeDtypeStruct((S, CONV_CH), bf16), jax.ShapeDtypeStruct((S, CONV_CH), bf16),
                   jax.ShapeDtypeStruct((ng * CONV_WP, LANES), f32), jax.ShapeDtypeStruct((1, CONV_CH), f32)],
        scratch_shapes=[pltpu.VMEM((S + 2 * CONV_HALO, LANES), f32), pltpu.VMEM((S + 2 * CONV_HALO, LANES), f32)],
        compiler_params=_params(("parallel",), 24 << 20),
    )(conv_in, conv_in, wdw, dc)


def _ln_stats(x, eps=1e-5):
    xc = x - jnp.mean(x, axis=-1, keepdims=True)
    r = lax.rsqrt(jnp.mean(xc * xc, axis=-1, keepdims=True) + eps)
    return r, xc * r


def _ln_bwd(r, xh, dxh):
    return r * (dxh - jnp.mean(dxh, axis=-1, keepdims=True) - xh * jnp.mean(dxh * xh, axis=-1, keepdims=True))


def _conv2_fwd(c, ln_g, ln_b, name):
    def body(c_ref, g_ref, b_ref, o_ref):
        _, xh = _ln_stats(c_ref[...])
        y = xh * g_ref[...] + b_ref[...]
        o_ref[...] = (y * jax.nn.sigmoid(y)).astype(o_ref.dtype)

    return _rows(body, [("r", c), ("f", ln_g), ("f", ln_b)], [("r", CONV_CH, bf16)], tm=min(256, c.shape[0]), name=name)[0]


def _conv2_bwd(c, dcz, ln_g, ln_b, name):
    def body(c_ref, d_ref, g_ref, b_ref, dc_ref, dg_ref, db_ref):
        r, xh = _ln_stats(c_ref[...])
        y = xh * g_ref[...] + b_ref[...]
        sg = jax.nn.sigmoid(y)
        dy = d_ref[...].astype(f32) * (sg * (1.0 + y * (1.0 - sg)))
        dc_ref[...] = _ln_bwd(r, xh, dy * g_ref[...])
        _accumulate(dg_ref, jnp.sum(dy * xh, axis=0, keepdims=True))
        _accumulate(db_ref, jnp.sum(dy, axis=0, keepdims=True))

    return _rows(body, [("r", c), ("r", dcz), ("f", ln_g), ("f", ln_b)],
                 [("r", CONV_CH, f32), ("a", (1, CONV_CH), f32), ("a", (1, CONV_CH), f32)],
                 tm=min(256, c.shape[0]), name=name)


GELU_K = math.sqrt(2.0 / math.pi)
GELU_C = 0.044715


def _gelu(x):
    return 0.5 * x * (1.0 + jnp.tanh(GELU_K * (x + GELU_C * x * x * x)))


def _gelu_and_grad(x):
    x2 = x * x
    th = jnp.tanh(GELU_K * (x + GELU_C * x2 * x))
    half = 0.5 * (1.0 + th)
    return x * half, half + 0.5 * x * (1.0 - th * th) * (GELU_K * (1.0 + 3.0 * GELU_C * x2))


def _chunk_rows(n):
    return pl.ds(pl.multiple_of(n * SG_CHUNK, SG_CHUNK), SG_CHUNK)


def _sgu_fwd(sg_in, ln_g, ln_b, w_s, b_s, name):
    S = sg_in.shape[0]

    def body(u_ref, v_ref, lg_ref, lb_ref, w_ref, b_ref, o_ref):
        wb = w_ref[...].astype(bf16)

        def chunk(n, carry):
            rows = _chunk_rows(n)
            gu = _gelu(u_ref[rows, :].astype(f32))
            _, xh = _ln_stats(_gelu(v_ref[rows, :].astype(f32)))
            vl = xh * lg_ref[...] + lb_ref[...]
            t = jnp.dot(wb, vl.astype(bf16), preferred_element_type=f32) + b_ref[...]
            o_ref[rows, :] = (gu * t).astype(o_ref.dtype)
            return carry

        lax.fori_loop(0, S // SG_CHUNK, chunk, 0, unroll=2)

    return _call(
        body, name=name, grid=(SG_G,),
        in_specs=[_group_specs(S, SG_G, False), _group_specs(S, SG_G, True),
                  pl.BlockSpec((1, LANES), lambda g: (0, g)), pl.BlockSpec((1, LANES), lambda g: (0, g)),
                  pl.BlockSpec((None, SG_CHUNK, SG_CHUNK), lambda g: (g, 0, 0)),
                  pl.BlockSpec((None, SG_CHUNK, 1), lambda g: (g, 0, 0))],
        out_specs=pl.BlockSpec((S, LANES), lambda g: (0, g)),
        out_shape=jax.ShapeDtypeStruct((S, SG_CH), bf16),
        compiler_params=_params(("parallel",), 24 << 20),
    )(sg_in, sg_in, ln_g, ln_b, w_s, b_s)


def _sgu_bwd(sg_in, dsz, ln_g, ln_b, w_s, w_s_t, b_s, name):
    S = sg_in.shape[0]

    def body(u_ref, v_ref, lg_ref, lb_ref, w_ref, wt_ref, b_ref, d_ref, du_ref, dv_ref, dw_ref, db_ref, dlg_ref, dlb_ref):
        wb = w_ref[...].astype(bf16)
        wtb = wt_ref[...].astype(bf16)

        def chunk(n, carry):
            dwa, dba, dlga, dlba = carry
            rows = _chunk_rows(n)
            u = u_ref[rows, :].astype(f32)
            v = v_ref[rows, :].astype(f32)
            gu, gu_grad = _gelu_and_grad(u)
            gv, gv_grad = _gelu_and_grad(v)
            r, xh = _ln_stats(gv)
            vlb = (xh * lg_ref[...] + lb_ref[...]).astype(bf16)
            t = jnp.dot(wb, vlb, preferred_element_type=f32) + b_ref[...]
            d = d_ref[rows, :].astype(f32)
            dt = d * gu
            dtb = dt.astype(bf16)
            dwa = dwa + lax.dot_general(dtb, vlb, (((1,), (1,)), ((), ())), preferred_element_type=f32)
            dba = dba + jnp.sum(dt, axis=1, keepdims=True)
            dvl = jnp.dot(wtb, dtb, preferred_element_type=f32)
            dlga = dlga + jnp.sum(dvl * xh, axis=0, keepdims=True)
            dlba = dlba + jnp.sum(dvl, axis=0, keepdims=True)
            dgv = _ln_bwd(r, xh, dvl * lg_ref[...])
            du_ref[rows, :] = (d * t * gu_grad).astype(du_ref.dtype)
            dv_ref[rows, :] = (dgv * gv_grad).astype(dv_ref.dtype)
            return dwa, dba, dlga, dlba

        init = (jnp.zeros((SG_CHUNK, SG_CHUNK), f32), jnp.zeros((SG_CHUNK, 1), f32),
                jnp.zeros((1, LANES), f32), jnp.zeros((1, LANES), f32))
        dwa, dba, dlga, dlba = lax.fori_loop(0, S // SG_CHUNK, chunk, init, unroll=2)
        dw_ref[...] = dwa
        db_ref[...] = dba
        dlg_ref[...] = dlga
        dlb_ref[...] = dlba

    wspec = pl.BlockSpec((None, SG_CHUNK, SG_CHUNK), lambda g: (g, 0, 0))
    bspec = pl.BlockSpec((None, SG_CHUNK, 1), lambda g: (g, 0, 0))
    lspec = pl.BlockSpec((1, LANES), lambda g: (0, g))
    cspec = pl.BlockSpec((S, LANES), lambda g: (0, g))
    return _call(
        body, name=name, grid=(SG_G,),
        in_specs=[_group_specs(S, SG_G, False), _group_specs(S, SG_G, True), lspec, lspec, wspec, wspec, bspec, cspec],
        out_specs=[cspec, cspec, wspec, bspec, lspec, lspec],
        out_shape=[jax.ShapeDtypeStruct((S, SG_CH), bf16), jax.ShapeDtypeStruct((S, SG_CH), bf16),
                   jax.ShapeDtypeStruct((SG_G, SG_CHUNK, SG_CHUNK), f32), jax.ShapeDtypeStruct((SG_G, SG_CHUNK, 1), f32),
                   jax.ShapeDtypeStruct((1, SG_CH), f32), jax.ShapeDtypeStruct((1, SG_CH), f32)],
        compiler_params=_params(("parallel",), 24 << 20),
    )(sg_in, sg_in, ln_g, ln_b, w_s, w_s_t, b_s, dsz)


def _row_tile(r, c, n_arrays, itemsize=4):
    fits = [tm for tm in range(16, r + 1, 16) if r % tm == 0 and 2 * n_arrays * tm * c * itemsize <= (24 << 20)]
    return fits[-1] if fits else r


def _sum_slots(slots, name):
    n, r, c = slots.shape
    tm = _row_tile(r, c, n + 2)

    def body(s_ref, o_ref):
        acc = s_ref[0].astype(f32)
        for k in range(1, n):
            acc = acc + s_ref[k].astype(f32)
        o_ref[...] = acc

    return _call(body, name=name, grid=(r // tm,),
                 in_specs=[pl.BlockSpec((n, tm, c), lambda i: (0, i, 0))],
                 out_specs=pl.BlockSpec((tm, c), lambda i: (i, 0)),
                 out_shape=jax.ShapeDtypeStruct((r, c), f32),
                 compiler_params=_params(("parallel",), 40 << 20))(slots)


def _add_sibling(g4, recv, pos, name):
    _, _, r, c = g4.shape
    tm = _row_tile(r, c, 4, 2)

    def body(pos_ref, g_ref, r_ref, o_ref, own_ref):
        s = (g_ref[...].astype(f32) + r_ref[...].astype(f32)).astype(o_ref.dtype)
        o_ref[...] = s

        @pl.when(pl.program_id(1) == pos_ref[1])
        def _():
            own_ref[...] = s

    grid_spec = pltpu.PrefetchScalarGridSpec(
        num_scalar_prefetch=1, grid=(r // tm, N_CHIP),
        in_specs=[pl.BlockSpec((None, None, tm, c), lambda i, k, pos_ref: (k, pos_ref[0], i, 0)),
                  pl.BlockSpec((None, tm, c), lambda i, k, pos_ref: (k, i, 0))],
        out_specs=[pl.BlockSpec((None, tm, c), lambda i, k, pos_ref: (k, i, 0)),
                   pl.BlockSpec((None, tm, c), lambda i, k, pos_ref: (pos_ref[1], i, 0))])
    return _call(body, name=name, grid_spec=grid_spec, out_shape=[jax.ShapeDtypeStruct((N_CHIP, r, c), bf16)] * 2,
                 compiler_params=_params(("parallel", "arbitrary"), 40 << 20))(pos, g4, recv)


def _adamw(w, g, m, v, name):
    L, r, c = w.shape
    tm = _row_tile(r, c, 7)
    c1 = 1.0 - ADAM_B1 ** ADAM_STEP
    c2 = 1.0 - ADAM_B2 ** ADAM_STEP

    def body(w_ref, g_ref, m_ref, v_ref, d_ref, mo_ref, vo_ref):
        g_ = g_ref[...]
        m_ = ADAM_B1 * m_ref[...] + (1.0 - ADAM_B1) * g_
        v_ = ADAM_B2 * v_ref[...] + (1.0 - ADAM_B2) * (g_ * g_)
        d_ref[...] = -ADAM_LR * ((m_ / c1) / (jnp.sqrt(v_ / c2) + ADAM_EPS) + ADAM_WD * w_ref[...])
        mo_ref[...] = m_
        vo_ref[...] = v_

    spec = pl.BlockSpec((None, tm, c), lambda l, i: (l, i, 0))
    shp = jax.ShapeDtypeStruct((L, r, c), f32)
    return _call(body, name=name, grid=(L, r // tm), in_specs=[spec] * 4, out_specs=[spec] * 3,
                 out_shape=[shp] * 3, compiler_params=_params(("parallel", "parallel"), 40 << 20))(w, g, m, v)


def _mesh_pos():
    return lax.axis_index("x"), lax.axis_index("y"), lax.axis_index("c")


SEM = pl.BlockSpec(memory_space=pltpu.SEMAPHORE)
ANY = pl.BlockSpec(memory_space=pl.ANY)
EFFECT = pltpu.SideEffectType.DATAFLOW_SIDE_EFFECTING


def _other_chips(x, y):
    return [(1 - x, y), (x, 1 - y), (1 - x, 1 - y)]


N_PEERS = {"sibling": 1, "neighbours": 2, "gather": 3, "reduce": 3}


def _peers(kind, x, y):
    return {1: [(x, y)], 2: [(1 - x, y), (x, 1 - y)], 3: _other_chips(x, y)}[N_PEERS[kind]]


def _ici_copy(kind, src_ref, land_ref, send_sem, recv_sem, sender, target, c):
    (sx, sy), (tx, ty) = sender, target
    if kind == "sibling":
        return pltpu.make_async_remote_copy(src_ref=src_ref.at[:, 1 - c], dst_ref=land_ref, send_sem=send_sem,
                                            recv_sem=recv_sem, device_id=(tx, ty, 1 - c), device_id_type=MESH)
    if kind in ("gather", "neighbours"):
        src, dst = src_ref, land_ref.at[4 * sx + 2 * sy + c]
    else:
        src, dst = src_ref.at[2 * tx + ty], land_ref.at[2 * sx + sy]
    return pltpu.make_async_remote_copy(src_ref=src, dst_ref=dst, send_sem=send_sem, recv_sem=recv_sem,
                                        device_id=(tx, ty, c), device_id_type=MESH)


def _ici_start(kind, srcs, lands, after, name):
    n = len(srcs)
    npeer = N_PEERS[kind]

    def body(*refs):
        src_refs, land_refs = refs[:n], refs[n:2 * n]
        send_sems, recv_sems = refs[2 * n + 1], refs[2 * n + 2]
        token = refs[-1]
        x, y, c = _mesh_pos()
        for j, chip in enumerate(_peers(kind, x, y)):
            for k in range(n):
                _ici_copy(kind, src_refs[k], land_refs[k], send_sems.at[npeer * k + j], recv_sems.at[npeer * k + j],(x, y), chip, c).start()
        token[...] = jnp.zeros_like(token)

    bufs = list(srcs) + list(lands)
    return _call(
        body, name=name,
        out_shape=(pltpu.SemaphoreType.DMA((npeer * n,)), pltpu.SemaphoreType.DMA((npeer * n,)),
                   *[pltpu.HBM(b.shape, b.dtype) for b in bufs], jax.ShapeDtypeStruct((8, LANES), f32)),
        in_specs=[HBM] * (2 * n) + [ANY], out_specs=(SEM, SEM, *[HBM] * (2 * n), pl.BlockSpec(memory_space=pltpu.VMEM)),
        input_output_aliases={i: 2 + i for i in range(2 * n)},
        compiler_params=pltpu.CompilerParams(has_side_effects=EFFECT),
    )(*[pltpu.with_memory_space_constraint(b, pltpu.HBM) for b in bufs], after)


def _ici_wait(kind, started, after, name):
    send_sems, recv_sems, *bufs = started[:-1]
    n = len(bufs) // 2
    npeer = N_PEERS[kind]

    def body(*refs):
        src_refs, land_refs = refs[:n], refs[n:2 * n]
        send_sems, recv_sems = refs[2 * n], refs[2 * n + 1]
        x, y, c = _mesh_pos()
        for j, chip in enumerate(_peers(kind, x, y)):
            for k in range(n):
                _ici_copy(kind, src_refs[k], land_refs[k], send_sems.at[npeer * k + j], recv_sems.at[npeer * k + j],(x, y), chip, c).wait_send()
                _ici_copy(kind, src_refs[k], land_refs[k], send_sems.at[npeer * k + j], recv_sems.at[npeer * k + j],chip, (x, y), c).wait_recv()

    out = _call(
        body, name=name, out_shape=[pltpu.HBM(b.shape, b.dtype) for b in bufs],
        in_specs=[HBM] * (2 * n) + [SEM, SEM, ANY], out_specs=[HBM] * (2 * n),
        input_output_aliases={i: i for i in range(2 * n)},
        compiler_params=pltpu.CompilerParams(has_side_effects=EFFECT),
    )(*bufs, send_sems, recv_sems, after)
    return out[:n], out[n:]


def _pass_on_copy(land_ref, send_sem, recv_sem, x, y, c, incoming=False):
    fx, fy = x * c + (1 - x) * (1 - c), (1 - y) * c + y * (1 - c)
    tx, ty = (1 - x) * c + x * (1 - c), y * c + (1 - y) * (1 - c)
    src = land_ref.at[4 * fx + 2 * fy + c]
    dst = land_ref.at[4 * (1 - x) + 2 * (1 - y) + c] if incoming else src
    return pltpu.make_async_remote_copy(src_ref=src, dst_ref=dst, send_sem=send_sem, recv_sem=recv_sem,
                                        device_id=(tx, ty, c), device_id_type=MESH)


def _pass_on_start(started, after, name):
    send1, recv1, *bufs = started[:-1]
    n = len(bufs) // 2

    def body(*refs):
        src_refs, land_refs = refs[:n], refs[n:2 * n]
        s1, r1 = refs[2 * n], refs[2 * n + 1]
        s2, r2 = refs[2 * n + 3], refs[2 * n + 4]
        token = refs[-1]
        x, y, c = _mesh_pos()
        for j, chip in enumerate(_peers("neighbours", x, y)):
            for k in range(n):
                _ici_copy("neighbours", src_refs[k], land_refs[k], s1.at[2 * k + j], r1.at[2 * k + j], (x, y), chip, c).wait_send()
                _ici_copy("neighbours", src_refs[k], land_refs[k], s1.at[2 * k + j], r1.at[2 * k + j], chip, (x, y), c).wait_recv()
        for k in range(n):
            _pass_on_copy(land_refs[k], s2.at[k], r2.at[k], x, y, c).start()
        token[...] = jnp.zeros_like(token)

    return _call(
        body, name=name,
        out_shape=(pltpu.SemaphoreType.DMA((n,)), pltpu.SemaphoreType.DMA((n,)),
                   *[pltpu.HBM(b.shape, b.dtype) for b in bufs], jax.ShapeDtypeStruct((8, LANES), f32)),
        in_specs=[HBM] * (2 * n) + [SEM, SEM, ANY],
        out_specs=(SEM, SEM, *[HBM] * (2 * n), pl.BlockSpec(memory_space=pltpu.VMEM)),
        input_output_aliases={i: 2 + i for i in range(2 * n)},
        compiler_params=pltpu.CompilerParams(has_side_effects=EFFECT),
    )(*bufs, send1, recv1, after)


def _pass_on_wait(passing, after, name):
    send2, recv2, *bufs = passing[:-1]
    n = len(bufs) // 2

    def body(*refs):
        land_refs = refs[n:2 * n]
        s2, r2 = refs[2 * n], refs[2 * n + 1]
        x, y, c = _mesh_pos()
        for k in range(n):
            _pass_on_copy(land_refs[k], s2.at[k], r2.at[k], x, y, c).wait_send()
            _pass_on_copy(land_refs[k], s2.at[k], r2.at[k], x, y, c, incoming=True).wait_recv()

    out = _call(
        body, name=name, out_shape=[pltpu.HBM(b.shape, b.dtype) for b in bufs],
        in_specs=[HBM] * (2 * n) + [SEM, SEM, ANY], out_specs=[HBM] * (2 * n),
        input_output_aliases={i: i for i in range(2 * n)},
        compiler_params=pltpu.CompilerParams(has_side_effects=EFFECT),
    )(*bufs, send2, recv2, after)
    return out[:n], out[n:]


def _d2d_gather(lands, after, name):
    n = len(lands)

    def body(*refs):
        in_refs, o_refs = refs[:n], refs[n + 1:2 * n + 1]
        send_sems, recv_sems = refs[2 * n + 1:]
        x, y, c = _mesh_pos()
        copies = [pltpu.make_async_remote_copy(
            src_ref=in_refs[k].at[:, c], dst_ref=o_refs[k].at[:, c], send_sem=send_sems.at[k], recv_sem=recv_sems.at[k],
            device_id=(x, y, 1 - c), device_id_type=MESH) for k in range(n)]
        for cp in copies:
            cp.start()
        for k, cp in enumerate(copies):
            cp.wait_send()
            pltpu.make_async_remote_copy(
                src_ref=in_refs[k].at[:, c], dst_ref=o_refs[k].at[:, 1 - c], send_sem=send_sems.at[k],
                recv_sem=recv_sems.at[k], device_id=(x, y, 1 - c), device_id_type=MESH).wait_recv()

    return _call(
        body, name=name, in_specs=[HBM] * n + [ANY], out_specs=[HBM] * n,
        out_shape=[jax.ShapeDtypeStruct(b.shape, b.dtype) for b in lands],
        input_output_aliases={k: k for k in range(n)},
        scratch_shapes=[pltpu.SemaphoreType.DMA((n,)), pltpu.SemaphoreType.DMA((n,))],
    )(*lands, after)


def _reduce_begin(grads, pos, tag):
    g4s = [g.reshape(N_CHIP, 2, g.shape[0] // N_DEV, g.shape[1]) for g in grads]
    recvs = [lax.empty((N_CHIP,) + g.shape[2:], g.dtype) for g in g4s]
    return _ici_start("sibling", g4s, recvs, pos, name="rs_d2d_start_" + tag)


def _reduce_continue(begun, pos, after, tag):
    g4s, recvs = _ici_wait("sibling", begun, after, name="rs_d2d_wait_" + tag)
    sums, lands = zip(*[_add_sibling(g4, rv, pos, name="rs_add_" + tag) for g4, rv in zip(g4s, recvs)])
    return _ici_start("reduce", sums, lands, pos, name="rs_start_" + tag)


def _adamw_reduced(layer, w, m, v, lands, prev, name):
    L, r, c = w.shape
    tm = _row_tile(r, c, 9)
    c1 = 1.0 - ADAM_B1 ** ADAM_STEP
    c2 = 1.0 - ADAM_B2 ** ADAM_STEP
    n_prev = 0 if prev is None else 4

    def body(w_ref, m_ref, v_ref, l_ref, *refs):
        g_ref, d_ref, mo_ref, vo_ref = refs[n_prev:]
        g_ = l_ref[0].astype(f32)
        for k in range(1, N_CHIP):
            g_ = g_ + l_ref[k].astype(f32)
        m_ = ADAM_B1 * m_ref[...] + (1.0 - ADAM_B1) * g_
        v_ = ADAM_B2 * v_ref[...] + (1.0 - ADAM_B2) * (g_ * g_)
        g_ref[...] = g_
        d_ref[...] = -ADAM_LR * ((m_ / c1) / (jnp.sqrt(v_ / c2) + ADAM_EPS) + ADAM_WD * w_ref[...])
        mo_ref[...] = m_
        vo_ref[...] = v_

    wspec = pl.BlockSpec((None, tm, c), lambda i: (layer, i, 0))
    sspec = pl.BlockSpec((N_CHIP, tm, c), lambda i: (0, i, 0))
    return _call(body, name=name, grid=(r // tm,), in_specs=[wspec] * 3 + [sspec] + [ANY] * n_prev, out_specs=[wspec] * 4,
                 out_shape=[jax.ShapeDtypeStruct((L, r, c), f32)] * 4,
                 input_output_aliases={4 + i: i for i in range(n_prev)},
                 compiler_params=_params(("parallel",), 40 << 20))(w, m, v, lands, *(prev or ()))


def _rope_tables(S):
    rows = S // GRID_W
    row = jnp.repeat(jnp.arange(rows, dtype=f32), GRID_W)
    col = jnp.tile(jnp.arange(GRID_W, dtype=f32), rows)
    nf = HEAD_DIM // 4
    inv = ROPE_THETA ** (-jnp.arange(nf, dtype=f32) / nf)
    ang = jnp.concatenate([row[:, None] * inv, col[:, None] * inv], axis=-1)
    cos, sin = jnp.cos(ang), jnp.sin(ang)
    return jnp.concatenate([cos, cos], axis=-1), jnp.concatenate([-sin, sin], axis=-1)


def _layer_fwd(xin, p, w, more_weights, projections_done, ffn_up_done, cos2, sin2):
    sv = {"xin": xin}
    h = sv["h"] = _rms_fwd(xin, p["g_mix"], name="rms_mix")
    proj = functools.partial(_mm, h, w["in"], "nt", bf16)
    q_raw = sv["q_raw"] = proj(n=Q_COLS, b_off=0, name="proj_q")
    kv_raw = sv["kv_raw"] = proj(n=2 * KV_COLS, b_off=OFF_KV, name="proj_kv")
    conv_in = sv["conv_in"] = proj(n=2 * CONV_CH, b_off=OFF_CONV, name="proj_conv")
    sg_in = sv["sg_in"] = proj(n=2 * SG_CH, b_off=OFF_SG, name="proj_sg")
    gl = sv["gl"] = proj(n=3 * D_MODEL, b_off=OFF_GATE, name="proj_gate")
    zero = projections_done(lax.optimization_barrier((gl, q_raw, kv_raw, conv_in, sg_in))[0])[0, 0]
    qr, kr = sv["qr"], sv["kr"] = _qk_fwd(q_raw, kv_raw, p["q_norm_g"] + zero, p["k_norm_g"], cos2, sin2, name="qk_fwd")
    o = sv["o"] = _attn_fwd(qr, kr, kv_raw, name="attn_fwd")
    c = sv["c"] = _conv1_fwd(conv_in, w["dw"], p["b_dw"], name="conv1_fwd")
    cz = sv["cz"] = _conv2_fwd(c, p["conv_ln_g"], p["conv_ln_b"], name="conv2_fwd")
    sz = sv["sz"] = _sgu_fwd(sg_in, p["sg_ln_g"], p["sg_ln_b"], p["w_s"], p["b_s"], name="sgu_fwd")
    w = {**w, **more_weights(1, sz)}
    sv["ya"], sv["yc"], sv["ys"], merged = _mixer_out([o, cz, sz], [w["attn_o"], w["conv_o"], w["sg_o"]], gl, p["b_gate"],
                                                      name="mixer_out")
    sv["merged"] = merged
    x1 = sv["x1"] = _mm(merged, w["out"], "nn", f32, res=xin, name="out_proj")
    w = {**w, **more_weights(2, x1)}
    hf = sv["hf"] = _rms_fwd(x1, p["g_ffn"], name="rms_ffn")
    sv["fg"], sv["fu"], act = _ffn_up(hf, w["ff_gate"], w["ff_up"], name="ffn_up")
    sv["act"] = act
    x2 = _mm(act, w["ff_down"], "nn", f32, res=x1, after=ffn_up_done(act), name="ff_down")
    return x2, sv, w


def _layer_bwd(dx2, dx2b, sv, p, w, cos2, sin2, reduce_begin, reduce_continue, last):
    small = {}
    dfg, dfu = _ffn_down_bwd(dx2b, w["ff_down"], sv["fg"], sv["fu"], name="ffn_down_bwd")
    g_down = _mm(sv["act"], dx2b, "tn", bf16, name="g_ff_down")
    dhf = _mm(dfg, w["ff_gate"], "nn", f32, name="d_hf_gate")
    dhf = _mm(dfu, w["ff_up"], "nn", f32, res=dhf, name="d_hf_up")
    g_gate = _mm(dfg, sv["hf"], "tn", bf16, name="g_ff_gate")
    g_up = _mm(dfu, sv["hf"], "tn", bf16, name="g_ff_up")
    zero = reduce_begin("ffn", dict(w_ff_gate=g_gate, w_ff_up=g_up, w_ff_down=g_down))[0, 0]
    dx1, dx1b, small["g_ffn"] = _rms_bwd(sv["x1"], p["g_ffn"] + zero, dhf, dx2, name="rms_ffn_bwd")
    g_out = _mm(sv["merged"], dx1b, "tn", bf16, name="g_out")
    *dgl, dya, dyc, dys, db0, db1, db2 = _merge_bwd_fused(dx1b, w["out"], sv["gl"], p["b_gate"], sv["ya"], sv["yc"], sv["ys"],
                                                        name="merge_bwd")
    small["b_gate"] = jnp.concatenate([db0, db1, db2], axis=1)
    do = _mm(dya, w["attn_o"], "nn", bf16, after=reduce_continue("ffn", dya), name="d_o")
    g_ao = _mm(dya, sv["o"], "tn", bf16, name="g_attn_o")
    dcz = _mm(dyc, w["conv_o"], "nn", bf16, name="d_cz")
    g_co = _mm(dyc, sv["cz"], "tn", bf16, name="g_conv_o")
    dsz = _mm(dys, w["sg_o"], "nn", bf16, name="d_sz")
    g_so = _mm(dys, sv["sz"], "tn", bf16, name="g_sg_o")
    zero = reduce_begin("mix", dict(w_attn_o=g_ao, w_conv_o=g_co, w_sg_o=g_so, w_out=g_out))[0, 0]
    dsu, dsv, small["w_s"], small["b_s"], small["sg_ln_g"], small["sg_ln_b"] = _sgu_bwd(
        sv["sg_in"], dsz, p["sg_ln_g"] + zero, p["sg_ln_b"], p["w_s"], p["w_s_t"], p["b_s"], name="sgu_bwd")
    dc, small["conv_ln_g"], small["conv_ln_b"] = _conv2_bwd(sv["c"], dcz, p["conv_ln_g"], p["conv_ln_b"], name="conv2_bwd")
    da, dgt, small["w_dw"], small["b_dw"] = _conv1_bwd(sv["conv_in"], dc, w["dw"], name="conv1_bwd")
    zero = reduce_continue("mix", da)[0, 0]
    dqr, dkr, dv = _attn_bwd(sv["qr"], sv["kr"], sv["kv_raw"], do, name="attn_bwd")
    dq_raw, dk_raw, small["q_norm_g"], small["k_norm_g"] = _qk_bwd(
        sv["q_raw"], sv["kv_raw"], dqr, dkr, p["q_norm_g"] + zero, p["k_norm_g"], cos2, sin2, name="qk_bwd")
    dproj = jnp.concatenate([dq_raw, dk_raw, dv.astype(bf16), da, dgt, dsu, dsv, *dgl], axis=1)
    g_in = _mm(dproj, sv["h"], "tn", bf16, name="g_in")
    begun = reduce_begin("in", dict(w_in=g_in))
    if last:
        begun = reduce_continue("in", begun)
    dh = _mm(dproj, w["in"], "nn", f32, after=begun, name="d_h")
    zero = begun[0, 0] if last else reduce_continue("in", dh)[0, 0]
    dx, dxb, small["g_mix"] = _rms_bwd(sv["xin"], p["g_mix"] + zero, dh, dx1, name="rms_mix_bwd")
    return dx, dxb, small


SMALL = ("g_mix", "b_gate", "q_norm_g", "k_norm_g", "b_dw", "conv_ln_g", "conv_ln_b", "sg_ln_g", "sg_ln_b",
         "w_s", "b_s", "g_ffn")
PACK_ALIGN = 8 * LANES


def _pack(parts):
    flat = jnp.concatenate([a.reshape(-1).astype(f32) for a in parts])
    pad = -flat.shape[0] % PACK_ALIGN
    return jnp.pad(flat, (0, pad)).reshape(-1, LANES)


def _unpack(buf, shapes):
    flat = buf.reshape(-1)
    out, pos = [], 0
    for shp in shapes:
        size = math.prod(shp)
        out.append(flat[pos:pos + size].reshape(shp))
        pos += size
    return out


def kernel(x, g_mix, w_in, b_gate, q_norm_g, k_norm_g, w_attn_o, w_dw, b_dw, conv_ln_g, conv_ln_b, w_conv_o, sg_ln_g, sg_ln_b, w_s, b_s, w_sg_o, w_out, g_ffn, w_ff_gate, w_ff_up, w_ff_down, g_final, loss_target, m_g_mix, m_w_in, m_b_gate, m_q_norm_g, m_k_norm_g, m_w_attn_o, m_w_dw, m_b_dw, m_conv_ln_g, m_conv_ln_b, m_w_conv_o, m_sg_ln_g, m_sg_ln_b, m_w_s, m_b_s, m_w_sg_o, m_w_out, m_g_ffn, m_w_ff_gate, m_w_ff_up, m_w_ff_down, m_g_final, v_g_mix, v_w_in, v_b_gate, v_q_norm_g, v_k_norm_g, v_w_attn_o, v_w_dw, v_b_dw, v_conv_ln_g, v_conv_ln_b, v_w_conv_o, v_sg_ln_g, v_sg_ln_b, v_w_s, v_b_s, v_w_sg_o, v_w_out, v_g_ffn, v_w_ff_gate, v_w_ff_up, v_w_ff_down, v_g_final):
    weights = dict(g_mix=g_mix, w_in=w_in, b_gate=b_gate, q_norm_g=q_norm_g, k_norm_g=k_norm_g, w_attn_o=w_attn_o,
                   w_dw=w_dw, b_dw=b_dw, conv_ln_g=conv_ln_g, conv_ln_b=conv_ln_b, w_conv_o=w_conv_o, sg_ln_g=sg_ln_g,
                   sg_ln_b=sg_ln_b, w_s=w_s, b_s=b_s, w_sg_o=w_sg_o, w_out=w_out, g_ffn=g_ffn, w_ff_gate=w_ff_gate,
                   w_ff_up=w_ff_up, w_ff_down=w_ff_down, g_final=g_final)
    mom_m = dict(g_mix=m_g_mix, w_in=m_w_in, b_gate=m_b_gate, q_norm_g=m_q_norm_g, k_norm_g=m_k_norm_g,
                 w_attn_o=m_w_attn_o, w_dw=m_w_dw, b_dw=m_b_dw, conv_ln_g=m_conv_ln_g, conv_ln_b=m_conv_ln_b,
                 w_conv_o=m_w_conv_o, sg_ln_g=m_sg_ln_g, sg_ln_b=m_sg_ln_b, w_s=m_w_s, b_s=m_b_s, w_sg_o=m_w_sg_o,
                 w_out=m_w_out, g_ffn=m_g_ffn, w_ff_gate=m_w_ff_gate, w_ff_up=m_w_ff_up, w_ff_down=m_w_ff_down,
                 g_final=m_g_final)
    mom_v = dict(g_mix=v_g_mix, w_in=v_w_in, b_gate=v_b_gate, q_norm_g=v_q_norm_g, k_norm_g=v_k_norm_g,
                 w_attn_o=v_w_attn_o, w_dw=v_w_dw, b_dw=v_b_dw, conv_ln_g=v_conv_ln_g, conv_ln_b=v_conv_ln_b,
                 w_conv_o=v_w_conv_o, sg_ln_g=v_sg_ln_g, sg_ln_b=v_sg_ln_b, w_s=v_w_s, b_s=v_b_s, w_sg_o=v_w_sg_o,
                 w_out=v_w_out, g_ffn=v_g_ffn, w_ff_gate=v_w_ff_gate, w_ff_up=v_w_ff_up, w_ff_down=v_w_ff_down,
                 g_final=v_g_final)
    S, D = x.shape[1], x.shape[2]
    xi, yi, ci = _mesh_pos()
    me = 4 * xi + 2 * yi + ci
    pos = jnp.stack([ci, 2 * xi + yi]).astype(jnp.int32)
    cos2, sin2 = _rope_tables(S)

    big = ("w_in", "w_attn_o", "w_conv_o", "w_sg_o", "w_out", "w_ff_gate", "w_ff_up", "w_ff_down")
    transposed = {"w_in", "w_attn_o", "w_conv_o", "w_sg_o", "w_ff_gate", "w_ff_up"}
    groups = (("in", "dw"), ("attn_o", "conv_o", "sg_o", "out"), ("ff_gate", "ff_up", "ff_down"))
    P, shards = [], []
    for l in range(DEPTH):
        sh = {n[2:]: (weights[n][l].T if n in transposed else weights[n][l]).astype(bf16) for n in big}
        sh["dw"] = jnp.pad(w_dw[l].reshape(CONV_W, LANES), ((0, CONV_WP - CONV_W), (0, 0)))
        shards.append(sh)
        p = {n: weights[n][l].reshape(1, -1) for n in SMALL if n not in ("w_s", "b_s")}
        p["w_s"] = w_s[l]
        p["w_s_t"] = jnp.swapaxes(w_s[l], 1, 2)
        p["b_s"] = b_s[l].reshape(SG_G, SG_CHUNK, 1)
        P.append(p)

    gathers, passing = {}, {}

    def start_gather(l, gi, after):
        srcs = [shards[l][n] for n in groups[gi]]
        lands = [lax.dynamic_update_index_in_dim(lax.empty((N_DEV,) + s.shape, s.dtype), s, me, 0) for s in srcs]
        gathers[l, gi] = _ici_start("neighbours", srcs, lands, after, name=f"ag_start_{l}{gi}")
        return gathers[l, gi][-1]

    def pass_on(l, gi, after):
        passing[l, gi] = _pass_on_start(gathers[l, gi], after, name=f"ag_pass_{l}{gi}")
        return passing[l, gi][-1]

    def gathered(l, gi, after):
        srcs, lands = _pass_on_wait(passing[l, gi], after, name=f"ag_wait_{l}{gi}")
        after = srcs[0]
        if gi == len(groups) - 1 and l + 1 < DEPTH:
            after = start_gather(l + 1, 0, after)
        full = _d2d_gather([b.reshape(N_CHIP, 2, *b.shape[1:]) for b in lands], after, name=f"ag_d2d_{gi}")
        return {n: f.reshape(-1, f.shape[3]) for n, f in zip(groups[gi], full)}

    def later_groups(l, after):
        for gi in range(1, len(groups)):
            after = start_gather(l, gi, after)
        return after

    all_started = later_groups(0, pass_on(0, 0, start_gather(0, 0, cos2)))

    h = x.reshape(S, D)
    saved, W = [], []
    for l in range(DEPTH):
        first = gathered(l, 0, all_started if l == 0 else h)
        if l == 0:
            P[l]["g_mix"] = P[l]["g_mix"] + all_started[0, 0]

        def projections_done(after, l=l):
            for gi in range(1, len(groups)):
                after = pass_on(l, gi, after)
            return after

        def ffn_up_done(after, l=l):
            return later_groups(l + 1, pass_on(l + 1, 0, after)) if l + 1 < DEPTH else None

        h, sv, w = _layer_fwd(h, P[l], first, functools.partial(lambda gi, z, l: gathered(l, gi, z), l=l),
                              projections_done, ffn_up_done, cos2, sin2)
        saved.append(sv)
        W.append(w)
    dx, dxb, sq, g_final_part = _final_loss(h, g_final.reshape(1, D), loss_target.reshape(S, D), name="final_loss")
    loss = lax.psum(0.5 * jnp.sum(sq) / D, ("x", "y", "c"))

    begun, reductions, small_grads = {}, {}, [None] * DEPTH
    for l in reversed(range(DEPTH)):
        def reduce_begin(group, grads, l=l):
            begun[l, group] = (tuple(grads), _reduce_begin(list(grads.values()), pos, tag=f"{group}{l}"))
            return begun[l, group][1][-1]

        def reduce_continue(group, after, l=l):
            names, started = begun[l, group]
            reductions[l, group] = (names, _reduce_continue(started, pos, after, tag=f"{group}{l}"))
            return reductions[l, group][1][-1]

        dx, dxb, small_grads[l] = _layer_bwd(dx, dxb, saved[l], P[l], W[l], cos2, sin2, reduce_begin, reduce_continue,
                                            last=(l == 0))
    grad_x = dx.reshape(x.shape)

    small_shapes = [weights[n].shape for n in SMALL] + [g_final.shape, (DEPTH, CONV_CH // LANES, CONV_WP, LANES)]
    parts = [jnp.stack([small_grads[l][n].reshape(weights[n].shape[1:]) for l in range(DEPTH)]) for n in SMALL]
    parts += [g_final_part.reshape(g_final.shape), jnp.stack([small_grads[l]["w_dw"] for l in range(DEPTH)])]
    packed = _pack(parts)
    packed_land = lax.dynamic_update_index_in_dim(lax.empty((N_DEV,) + packed.shape, f32), packed, me, 0)
    small_started = _ici_start("gather", [packed], [packed_land], dx, name="gather_small_start")

    grads_out, delta, new_m, new_v = {}, {}, {}, {}
    swap = lambda a: jnp.swapaxes(a, 1, 2)

    def update(n, lands):
        as_arrives = n not in transposed or weights[n].shape[2] % LANES != 0
        if as_arrives:
            to_arrival = swap if n in transposed else (lambda a: a)
            out = None
            for l in reversed(range(DEPTH)):
                out = _adamw_reduced(l, to_arrival(weights[n]), to_arrival(mom_m[n]), to_arrival(mom_v[n]),
                                     lands[l], out, name=f"adamw_{n}_{l}")
            grads_out[n], delta[n], new_m[n], new_v[n] = [to_arrival(o) for o in out]
            return out[1]
        g = jnp.stack([_sum_slots(lands[l], name="rs_sum_" + n) for l in range(DEPTH)])
        grads_out[n] = swap(g)
        delta[n], new_m[n], new_v[n] = _adamw(weights[n], grads_out[n], mom_m[n], mom_v[n], name="adamw_" + n)
        return delta[n]

    after = small_started[-1]
    for group in ("ffn", "mix", "in"):
        names = reductions[0, group][0]
        arrived = [_ici_wait("reduce", reductions[l, group][1], after, name=f"rs_wait_{group}{l}") for l in range(DEPTH)]
        for i, n in enumerate(names):
            after = update(n, [arrived[l][1][i] for l in range(DEPTH)])

    _, small_lands = _ici_wait("gather", small_started, after, name="gather_small_wait")
    small_full = _d2d_gather([small_lands[0].reshape(N_CHIP, 2, *packed.shape)], after, name="gather_small_d2d")[0]
    total = _sum_slots(small_full.reshape(N_DEV, *packed.shape), name="sum_small")
    small_total = _unpack(total, small_shapes)
    grads_out.update(zip(SMALL + ("g_final",), small_total[:-1]))
    dw_full = small_total[-1]
    grads_out["w_dw"] = lax.dynamic_index_in_dim(dw_full, me, axis=1, keepdims=False)[:, :CONV_W].reshape(w_dw.shape)

    rep = tuple(n for n in SMALL if n != "w_s") + ("g_final",)
    rep_shapes = [weights[n].shape for n in rep]
    packs = [_pack([src[n] for n in rep])[None] for src in (weights, grads_out, mom_m, mom_v)]
    for dst, buf in zip((delta, new_m, new_v), _adamw(*packs, name="adamw_small")):
        dst.update(zip(rep, _unpack(buf[0], rep_shapes)))
    for n, shp in (("w_dw", (1, DEPTH * CONV_W, LANES)), ("w_s", (DEPTH, SG_G * SG_CHUNK, SG_CHUNK))):
        upd = _adamw(*[src[n].reshape(shp) for src in (weights, grads_out, mom_m, mom_v)], name="adamw_" + n)
        for dst, buf in zip((delta, new_m, new_v), upd):
            dst[n] = buf.reshape(weights[n].shape)

    order = ("g_mix", "w_in", "b_gate", "q_norm_g", "k_norm_g", "w_attn_o", "w_dw", "b_dw", "conv_ln_g", "conv_ln_b",
             "w_conv_o", "sg_ln_g", "sg_ln_b", "w_s", "b_s", "w_sg_o", "w_out", "g_ffn", "w_ff_gate", "w_ff_up",
             "w_ff_down", "g_final")
    return (loss, grad_x, *[grads_out[n] for n in order], *[delta[n] for n in order],
            *[new_m[n] for n in order], *[new_v[n] for n in order])
```

```python
import functools
import math

import jax
import jax.numpy as jnp
from jax import lax
from jax.experimental import pallas as pl
from jax.experimental.pallas import tpu as pltpu

f32, bf16 = jnp.float32, jnp.bfloat16

D_MODEL = 2048
SEQ = 2048
DEPTH = 2
GRID_W = 64
HEAD_DIM = 128
LANES = 128
N_Q = (D_MODEL // 2) // HEAD_DIM
N_KV = N_Q // 4
GRP = N_Q // N_KV
Q_COLS = N_Q * HEAD_DIM
KV_COLS = N_KV * HEAD_DIM
CONV_CH = D_MODEL // 2
CONV_W = 31
CONV_PAD = CONV_W // 2
CONV_WP = 32
SG_CH = D_MODEL // 2
SG_G = SG_CH // LANES
SG_CHUNK = 128
D_FF = -(-8 * D_MODEL // (3 * 256)) * 256
OFF_KV = Q_COLS
OFF_CONV = OFF_KV + 2 * KV_COLS
OFF_SG = OFF_CONV + 2 * CONV_CH
OFF_GATE = OFF_SG + 2 * SG_CH
IN_COLS = OFF_GATE + 3 * D_MODEL
ROPE_THETA = 10000.0
SCALE = HEAD_DIM ** -0.5
N_DEV = 8
N_CHIP = 4

ADAM_LR, ADAM_B1, ADAM_B2, ADAM_EPS, ADAM_WD, ADAM_STEP = 0.001, 0.9, 0.999, 1e-08, 0.01, 10

VMEM_BYTES_V7X = 64 << 20
VMEM_CAP = VMEM_BYTES_V7X - (6 << 20)
MESH = pl.DeviceIdType.MESH
HBM = pl.BlockSpec(memory_space=pltpu.HBM)


def _in_hbm(a):
    if isinstance(a, jax.Array) and jnp.issubdtype(a.dtype, jnp.floating) and a.size * a.dtype.itemsize >= (1 << 20):
        return pltpu.with_memory_space_constraint(a, pltpu.HBM)
    return a


def _out_hbm(s):
    if isinstance(s, jax.ShapeDtypeStruct) and math.prod(s.shape) * jnp.dtype(s.dtype).itemsize >= (1 << 20):
        return pltpu.HBM(s.shape, s.dtype)
    return s


def _call(body, **kw):
    shapes = kw.pop("out_shape")
    shapes = type(shapes)(_out_hbm(s) for s in shapes) if isinstance(shapes, (list, tuple)) else _out_hbm(shapes)
    call = pl.pallas_call(body, out_shape=shapes, **kw)
    return lambda *args: call(*[_in_hbm(a) for a in args])


def _pick(n, cands):
    for c in cands:
        if n % c == 0:
            return c
    raise ValueError((n, cands))


def _params(sem, vmem_bytes):
    return pltpu.CompilerParams(dimension_semantics=sem, vmem_limit_bytes=int(min(max(vmem_bytes, 16 << 20), VMEM_CAP)))


def _mm(a, b, form, out_dtype, *, n=None, b_off=0, res=None, after=None, name):
    if form == "tn":
        K, M = a.shape
    else:
        M, K = a.shape
    N = n if n is not None else (b.shape[0] if form == "nt" else b.shape[1])
    if K <= 2048:
        tk = K
        if form == "tn":
            tm = _pick(M, (512, 256, 128))
            tn = N if N <= 2048 else _pick(N, (1024, 512, 256, 128))
        else:
            tm = M if M <= 2048 else _pick(M, (2048, 1024, 512))
            tn = _pick(math.gcd(N, b_off) if b_off else N, (256, 128) if res is not None else (512, 256, 128))
    else:
        tk = max(t for t in range(LANES, 3072 + 1, LANES) if K % t == 0)
        tm = _pick(M, (1024, 512, 256, 128))
        tn = _pick(math.gcd(N, b_off) if b_off else N, (1024, 512, 256, 128))
    assert b_off % tn == 0
    off = b_off // tn
    nk = K // tk
    if form == "tn":
        a_spec = pl.BlockSpec((tk, tm), lambda i, j, k: (k, i))
    else:
        a_spec = pl.BlockSpec((tm, tk), lambda i, j, k: (i, k))
    if form == "nt":
        b_spec = pl.BlockSpec((tn, tk), lambda i, j, k: (j + off, k))
    else:
        b_spec = pl.BlockSpec((tk, tn), lambda i, j, k: (k, j + off))
    dims = {"nn": ((1,), (0,)), "nt": ((1,), (1,)), "tn": ((0,), (0,))}[form]
    has_res = res is not None

    def body(*refs):
        if after is not None:
            refs = refs[1:]
        if has_res:
            a_ref, b_ref, r_ref, o_ref = refs[:4]
        else:
            a_ref, b_ref, o_ref = refs[:3]
        p = lax.dot_general(a_ref[...], b_ref[...], (dims, ((), ())), preferred_element_type=f32)

        def finish(acc):
            if has_res:
                acc = acc + r_ref[...].astype(f32)
            o_ref[...] = acc.astype(o_ref.dtype)

        if nk == 1:
            finish(p)
        else:
            acc_ref = refs[-1]
            k = pl.program_id(2)

            @pl.when(k == 0)
            def _():
                acc_ref[...] = p

            @pl.when(k > 0)
            def _():
                acc_ref[...] += p

            @pl.when(k == nk - 1)
            def _():
                finish(acc_ref[...])

    in_specs = [a_spec, b_spec]
    args = [a, b]
    osz = jnp.dtype(out_dtype).itemsize
    vmem = 2 * (tm * tk * 2 + tk * tn * 2 + tm * tn * osz) + 2 * tm * tn * 4
    if has_res:
        in_specs.append(pl.BlockSpec((tm, tn), lambda i, j, k: (i, j)))
        args.append(res)
        vmem += 2 * tm * tn * res.dtype.itemsize
    scratch = []
    if nk > 1:
        scratch.append(pltpu.VMEM((tm, tn), f32))
        vmem += tm * tn * 4
    if after is not None:
        in_specs.insert(0, pl.BlockSpec(memory_space=pl.ANY))
        args.insert(0, after)
    return _call(
        body, name=name, grid=(M // tm, N // tn, nk),
        in_specs=in_specs, out_specs=pl.BlockSpec((tm, tn), lambda i, j, k: (i, j)),
        out_shape=jax.ShapeDtypeStruct((M, N), out_dtype), scratch_shapes=scratch,
        compiler_params=_params(("parallel", "parallel", "arbitrary"), vmem + (8 << 20)),
    )(*args)


EPI_TN = 256


def _mm_epi(a, bs, form, extras, out_dtypes, n_sums, fn, name, tn=EPI_TN):
    a_list = list(a) if isinstance(a, (list, tuple)) else [a]
    M, K = a_list[0].shape
    N = bs[0].shape[0] if form == "nt" else bs[0].shape[1]
    assert K <= 2048 and N % tn == 0 and len(a_list) in (1, len(bs))
    dims = ((1,), (1,)) if form == "nt" else ((1,), (0,))
    na, nb, ne = len(a_list), len(bs), len(extras)

    def body(*refs):
        a_refs, b_refs = refs[:na], refs[na:na + nb]
        e_refs, o_refs = refs[na + nb:na + nb + ne], refs[na + nb + ne:]
        avs = [r[...] for r in a_refs] * (nb // na)
        ps = [lax.dot_general(av, b[...], (dims, ((), ())), preferred_element_type=f32) for av, b in zip(avs, b_refs)]
        for o_ref, o in zip(o_refs, fn(ps, [e[...] for e in e_refs])):
            o_ref[...] = o.astype(o_ref.dtype)

    in_specs = [pl.BlockSpec((M, K), lambda j: (0, 0), pipeline_mode=pl.Buffered(1)) for _ in a_list]
    in_specs += [pl.BlockSpec((tn, K), lambda j: (j, 0)) if form == "nt" else pl.BlockSpec((K, tn), lambda j: (0, j))
                 for _ in bs]
    for arr, first in extras:
        assert first % tn == 0
        in_specs.append(pl.BlockSpec((arr.shape[0], tn), functools.partial(lambda j, o: (0, j + o), o=first // tn)))
    out_specs = [pl.BlockSpec((M, tn), lambda j: (0, j))] * len(out_dtypes) + [pl.BlockSpec((1, tn), lambda j: (0, j))] * n_sums
    out_shape = [jax.ShapeDtypeStruct((M, N), dt) for dt in out_dtypes] + [jax.ShapeDtypeStruct((1, N), f32)] * n_sums
    tiles = sum(arr.shape[0] * tn * arr.dtype.itemsize for arr, _ in extras) + sum(M * tn * jnp.dtype(dt).itemsize for dt in out_dtypes)
    vmem = na * M * K * 2 + 2 * nb * tn * K * 2 + 2 * tiles + (nb + 6) * M * tn * 4
    return _call(body, name=name, grid=(N // tn,), in_specs=in_specs, out_specs=out_specs, out_shape=out_shape,
                 compiler_params=_params(("parallel",), vmem + (8 << 20)))(*a_list, *bs, *[arr for arr, _ in extras])


def _ffn_up(hf, wt_gate, wt_up, name):
    def fn(ps, _):
        g, u = ps[0].astype(bf16), ps[1].astype(bf16)
        gf = g.astype(f32)
        return g, u, gf * jax.nn.sigmoid(gf) * u.astype(f32)

    return _mm_epi(hf, [wt_gate, wt_up], "nt", [], [bf16] * 3, 0, fn, name, tn=2 * EPI_TN)


def _ffn_down_bwd(dx2b, w_down, fg, fu, name):
    def fn(ps, es):
        d, g = ps[0], es[0].astype(f32)
        sg = jax.nn.sigmoid(g)
        return d * es[1].astype(f32) * sg * (1.0 + g * (1.0 - sg)), d * g * sg

    return _mm_epi(dx2b, [w_down], "nt", [(fg, 0), (fu, 0)], [bf16] * 2, 0, fn, name, tn=2 * EPI_TN)


def _mixer_out(branches, wts, gl, b_gate, name):
    D = wts[0].shape[0]

    def fn(ps, es):
        ys = [p_.astype(bf16) for p_ in ps]
        merged = None
        for i in range(3):
            term = jax.nn.sigmoid(es[i].astype(f32) + es[3 + i]) * ys[i].astype(f32)
            merged = term if merged is None else merged + term
        return ys + [merged]

    extras = [(gl, i * D) for i in range(3)] + [(b_gate, i * D) for i in range(3)]
    return _mm_epi(branches, wts, "nt", extras, [bf16] * 4, 0, fn, name)


def _merge_bwd_fused(dx1b, w_out, gl, b_gate, ya, yc, ys, name):
    D = ya.shape[1]

    def fn(ps, es):
        dm_, outs, sums = ps[0], [], []
        for i in range(3):
            gate = jax.nn.sigmoid(es[i].astype(f32) + es[3 + i])
            dlog = dm_ * es[6 + i].astype(f32) * gate * (1.0 - gate)
            outs.append((dlog, dm_ * gate))
            sums.append(jnp.sum(dlog, axis=0, keepdims=True))
        return [o[0] for o in outs] + [o[1] for o in outs] + sums

    extras = [(gl, i * D) for i in range(3)] + [(b_gate, i * D) for i in range(3)] + [(ya, 0), (yc, 0), (ys, 0)]
    return _mm_epi(dx1b, [w_out], "nt", extras, [bf16] * 6, 3, fn, name)


def _rows(body, ins, outs, *, tm, name, vmem=40 << 20):
    nrows = next(s[1].shape[0] for s in ins if s[0] == "r")
    in_specs, args = [], []
    for s in ins:
        arr = s[1]
        if s[0] == "r":
            w = s[2] if len(s) > 2 else arr.shape[1]
            cb = s[3] if len(s) > 3 else 0
            in_specs.append(pl.BlockSpec((tm, w), functools.partial(lambda i, cb: (i, cb), cb=cb)))
        else:
            in_specs.append(pl.BlockSpec(arr.shape, functools.partial(lambda i, nd: (0,) * nd, nd=arr.ndim)))
        args.append(arr)
    out_specs, out_shape = [], []
    for s in outs:
        if s[0] == "r":
            out_specs.append(pl.BlockSpec((tm, s[1]), lambda i: (i, 0)))
            out_shape.append(jax.ShapeDtypeStruct((nrows, s[1]), s[2]))
        else:
            out_specs.append(pl.BlockSpec(s[1], functools.partial(lambda i, nd: (0,) * nd, nd=len(s[1]))))
            out_shape.append(jax.ShapeDtypeStruct(s[1], s[2]))
    return _call(body, name=name, grid=(nrows // tm,), in_specs=in_specs, out_specs=out_specs,
                 out_shape=out_shape, compiler_params=_params(("arbitrary",), vmem))(*args)


def _accumulate(ref, part):
    i = pl.program_id(0)

    @pl.when(i == 0)
    def _():
        ref[...] = part

    @pl.when(i > 0)
    def _():
        ref[...] += part


def _rms_stats(x):
    r = lax.rsqrt(jnp.mean(x * x, axis=-1, keepdims=True) + 1e-6)
    return r, x * r


def _rms_fwd(x, g, name):
    def body(x_ref, g_ref, o_ref):
        _, xn = _rms_stats(x_ref[...])
        o_ref[...] = (xn * g_ref[...]).astype(o_ref.dtype)

    return _rows(body, [("r", x), ("f", g)], [("r", x.shape[1], bf16)], tm=min(256, x.shape[0]), name=name)[0]


def _rms_bwd(x, g, dh, dres, name):
    D = x.shape[1]

    def body(x_ref, g_ref, dh_ref, dr_ref, dx_ref, dxb_ref, dg_ref):
        r, xn = _rms_stats(x_ref[...])
        dy = dh_ref[...].astype(f32)
        dxn = dy * g_ref[...]
        dx = dr_ref[...] + r * (dxn - xn * jnp.mean(dxn * xn, axis=-1, keepdims=True))
        dx_ref[...] = dx
        dxb_ref[...] = dx.astype(bf16)
        _accumulate(dg_ref, jnp.sum(dy * xn, axis=0, keepdims=True))

    return _rows(body, [("r", x), ("f", g), ("r", dh), ("r", dres)],
                 [("r", D, f32), ("r", D, bf16), ("a", (1, D), f32)], tm=min(256, x.shape[0]), name=name)


def _final_loss(x, g, tgt, name):
    D = x.shape[1]

    def body(x_ref, g_ref, t_ref, dx_ref, dxb_ref, sq_ref, dg_ref):
        r, xn = _rms_stats(x_ref[...])
        gain = g_ref[...]
        diff = xn * gain - t_ref[...]
        dy = diff * (1.0 / D)
        dxn = dy * gain
        dx = r * (dxn - xn * jnp.mean(dxn * xn, axis=-1, keepdims=True))
        dx_ref[...] = dx
        dxb_ref[...] = dx.astype(bf16)
        _accumulate(sq_ref, jnp.sum(diff * diff, axis=0, keepdims=True))
        _accumulate(dg_ref, jnp.sum(dy * xn, axis=0, keepdims=True))

    return _rows(body, [("r", x), ("f", g), ("r", tgt)],
                 [("r", D, f32), ("r", D, bf16), ("a", (1, D), f32), ("a", (1, D), f32)],
                 tm=min(256, x.shape[0]), name=name)


def _qk_fwd(q_raw, kv_raw, qg, kg, cos2, sin2, name):
    def body(q_ref, k_ref, qg_ref, kg_ref, c_ref, s_ref, qo_ref, ko_ref):
        c, s = c_ref[...], s_ref[...]

        def head(src, gain, dst, h):
            cols = slice(h * HEAD_DIM, (h + 1) * HEAD_DIM)
            _, xn = _rms_stats(src[:, cols].astype(f32))
            y = xn * gain
            dst[:, cols] = (y * c + pltpu.roll(y, HEAD_DIM // 2, 1) * s).astype(dst.dtype)

        for h in range(N_Q):
            head(q_ref, qg_ref[...], qo_ref, h)
        for h in range(N_KV):
            head(k_ref, kg_ref[...], ko_ref, h)

    return _rows(body, [("r", q_raw), ("r", kv_raw, KV_COLS, 0), ("f", qg), ("f", kg), ("r", cos2), ("r", sin2)],
                 [("r", Q_COLS, bf16), ("r", KV_COLS, bf16)], tm=min(256, q_raw.shape[0]), name=name)


def _qk_bwd(q_raw, kv_raw, dqr, dkr, qg, kg, cos2, sin2, name):
    def body(q_ref, k_ref, dq_ref, dk_ref, qg_ref, kg_ref, c_ref, s_ref, dqo_ref, dko_ref, dqg_ref, dkg_ref):
        c, s = c_ref[...], s_ref[...]

        def head(src, dsrc, gain, dst, h):
            cols = slice(h * HEAD_DIM, (h + 1) * HEAD_DIM)
            r, xn = _rms_stats(src[:, cols].astype(f32))
            do = dsrc[:, cols].astype(f32)
            dy = do * c + pltpu.roll(do * s, HEAD_DIM // 2, 1)
            dxn = dy * gain
            dst[:, cols] = (r * (dxn - xn * jnp.mean(dxn * xn, axis=-1, keepdims=True))).astype(dst.dtype)
            return jnp.sum(dy * xn, axis=0, keepdims=True)

        dq_gain = head(q_ref, dq_ref, qg_ref[...], dqo_ref, 0)
        for h in range(1, N_Q):
            dq_gain = dq_gain + head(q_ref, dq_ref, qg_ref[...], dqo_ref, h)
        dk_gain = head(k_ref, dk_ref, kg_ref[...], dko_ref, 0)
        for h in range(1, N_KV):
            dk_gain = dk_gain + head(k_ref, dk_ref, kg_ref[...], dko_ref, h)
        _accumulate(dqg_ref, dq_gain)
        _accumulate(dkg_ref, dk_gain)

    return _rows(body, [("r", q_raw), ("r", kv_raw, KV_COLS, 0), ("r", dqr), ("r", dkr), ("f", qg), ("f", kg),
                        ("r", cos2), ("r", sin2)],
                 [("r", Q_COLS, bf16), ("r", KV_COLS, bf16), ("a", (1, HEAD_DIM), f32), ("a", (1, HEAD_DIM), f32)],
                 tm=min(256, q_raw.shape[0]), name=name)


def _softmax_rows(q, k):
    s = lax.dot_general(q, k, (((1,), (1,)), ((), ())), preferred_element_type=f32) * (SCALE * math.log2(math.e))
    p = jnp.exp2(s - jnp.max(s, axis=-1, keepdims=True))
    return p * (1.0 / jnp.sum(p, axis=-1, keepdims=True))


def _head_cols(g):
    return slice(g * HEAD_DIM, (g + 1) * HEAD_DIM)


def _attn_fwd(qr, kr, kv_raw, name):
    S = qr.shape[0]
    tq = min(256, S)

    def body(q_ref, k_ref, v_ref, o_ref):
        k, v = k_ref[...], v_ref[...]
        for g in range(GRP):
            p = _softmax_rows(q_ref[:, _head_cols(g)], k)
            o_ref[:, _head_cols(g)] = jnp.dot(p.astype(bf16), v, preferred_element_type=f32).astype(o_ref.dtype)

    return _call(
        body, name=name, grid=(N_KV, S // tq),
        in_specs=[pl.BlockSpec((tq, GRP * HEAD_DIM), lambda kv, i: (i, kv)),
                  pl.BlockSpec((S, HEAD_DIM), lambda kv, i: (0, kv)),
                  pl.BlockSpec((S, HEAD_DIM), lambda kv, i: (0, N_KV + kv))],
        out_specs=pl.BlockSpec((tq, GRP * HEAD_DIM), lambda kv, i: (i, kv)),
        out_shape=jax.ShapeDtypeStruct((S, Q_COLS), bf16),
        compiler_params=_params(("parallel", "arbitrary"), 4 * GRP * tq * S * 4 + (8 << 20)),
    )(qr, kr, kv_raw)


def _attn_bwd(qr, kr, kv_raw, do, name):
    S = qr.shape[0]
    tq = min(256, S)

    def body(q_ref, k_ref, v_ref, do_ref, dq_ref, dk_ref, dv_ref):
        first = pl.program_id(1) == 0
        k, v = k_ref[...], v_ref[...]
        dv_part = dk_part = None
        for g in range(GRP):
            q, do_ = q_ref[:, _head_cols(g)], do_ref[:, _head_cols(g)]
            p = _softmax_rows(q, k)
            dp = lax.dot_general(do_, v, (((1,), (1,)), ((), ())), preferred_element_type=f32)
            ds = (p * (dp - jnp.sum(dp * p, axis=-1, keepdims=True)) * SCALE).astype(bf16)
            dq_ref[:, _head_cols(g)] = jnp.dot(ds, k, preferred_element_type=f32).astype(dq_ref.dtype)
            dv_g = lax.dot_general(p.astype(bf16), do_, (((0,), (0,)), ((), ())), preferred_element_type=f32)
            dk_g = lax.dot_general(ds, q, (((0,), (0,)), ((), ())), preferred_element_type=f32)
            dv_part = dv_g if g == 0 else dv_part + dv_g
            dk_part = dk_g if g == 0 else dk_part + dk_g

        @pl.when(first)
        def _():
            dv_ref[...] = dv_part
            dk_ref[...] = dk_part

        @pl.when(jnp.logical_not(first))
        def _():
            dv_ref[...] += dv_part
            dk_ref[...] += dk_part

    qspec = pl.BlockSpec((tq, GRP * HEAD_DIM), lambda kv, i: (i, kv))
    return _call(
        body, name=name, grid=(N_KV, S // tq),
        in_specs=[qspec, pl.BlockSpec((S, HEAD_DIM), lambda kv, i: (0, kv)),
                  pl.BlockSpec((S, HEAD_DIM), lambda kv, i: (0, N_KV + kv)), qspec],
        out_specs=[qspec, pl.BlockSpec((S, HEAD_DIM), lambda kv, i: (0, kv)),
                   pl.BlockSpec((S, HEAD_DIM), lambda kv, i: (0, kv))],
        out_shape=[jax.ShapeDtypeStruct((S, Q_COLS), bf16), jax.ShapeDtypeStruct((S, KV_COLS), f32),
                   jax.ShapeDtypeStruct((S, KV_COLS), f32)],
        compiler_params=_params(("parallel", "arbitrary"), 6 * GRP * tq * S * 4 + (8 << 20)),
    )(qr, kr, kv_raw, do)


CONV_HALO = 16


def _fill_padded(pad_ref, val, S):
    pad_ref[pl.ds(0, CONV_HALO), :] = jnp.zeros((CONV_HALO, LANES), f32)
    pad_ref[pl.ds(CONV_HALO + S, CONV_HALO), :] = jnp.zeros((CONV_HALO, LANES), f32)
    pad_ref[pl.ds(CONV_HALO, S), :] = val


def _group_specs(S, n_groups, second_half):
    return pl.BlockSpec((S, LANES), functools.partial(lambda g, o: (0, g + o), o=n_groups if second_half else 0))


def _conv1_fwd(conv_in, wdw, b_dw, name):
    S = conv_in.shape[0]
    ng = CONV_CH // LANES
    R = min(256, S)

    def body(a_ref, g_ref, w_ref, b_ref, o_ref, pad_ref):
        z = a_ref[...].astype(f32) * jax.nn.sigmoid(g_ref[...].astype(f32))
        _fill_padded(pad_ref, z, S)
        for r in range(S // R):
            acc = jnp.zeros((R, LANES), f32) + b_ref[...]
            for j in range(CONV_W):
                acc = acc + w_ref[pl.ds(j, 1), :] * pad_ref[pl.ds(r * R + CONV_HALO - CONV_PAD + j, R), :]
            o_ref[pl.ds(r * R, R), :] = acc

    return _call(
        body, name=name, grid=(ng,),
        in_specs=[_group_specs(S, ng, False), _group_specs(S, ng, True),
                  pl.BlockSpec((CONV_WP, LANES), lambda g: (g, 0)), pl.BlockSpec((1, LANES), lambda g: (0, g))],
        out_specs=pl.BlockSpec((S, LANES), lambda g: (0, g)),
        out_shape=jax.ShapeDtypeStruct((S, CONV_CH), f32),
        scratch_shapes=[pltpu.VMEM((S + 2 * CONV_HALO, LANES), f32)],
        compiler_params=_params(("parallel",), 24 << 20),
    )(conv_in, conv_in, wdw, b_dw)


def _conv1_bwd(conv_in, dc, wdw, name):
    S = conv_in.shape[0]
    ng = CONV_CH // LANES
    R = min(256, S)

    def body(a_ref, g_ref, w_ref, dc_ref, da_ref, dg_ref, dw_ref, db_ref, padz_ref, padd_ref):
        a = a_ref[...].astype(f32)
        sg = jax.nn.sigmoid(g_ref[...].astype(f32))
        _fill_padded(padz_ref, a * sg, S)
        _fill_padded(padd_ref, dc_ref[...], S)
        for r in range(S // R):
            dz = jnp.zeros((R, LANES), f32)
            for j in range(CONV_W):
                dz = dz + w_ref[pl.ds(j, 1), :] * padd_ref[pl.ds(r * R + CONV_HALO + CONV_PAD - j, R), :]
            rows = pl.ds(r * R, R)
            ar, sr = a_ref[rows, :].astype(f32), jax.nn.sigmoid(g_ref[rows, :].astype(f32))
            da_ref[rows, :] = (dz * sr).astype(da_ref.dtype)
            dg_ref[rows, :] = (dz * ar * sr * (1.0 - sr)).astype(dg_ref.dtype)
        for j in range(CONV_W):
            tot = jnp.zeros((1, LANES), f32)
            for r in range(S // R):
                tot = tot + jnp.sum(dc_ref[pl.ds(r * R, R), :] * padz_ref[pl.ds(r * R + CONV_HALO - CONV_PAD + j, R), :],
                                    axis=0, keepdims=True)
            dw_ref[pl.ds(j, 1), :] = tot
        dw_ref[pl.ds(CONV_W, CONV_WP - CONV_W), :] = jnp.zeros((CONV_WP - CONV_W, LANES), f32)
        db_ref[...] = jnp.sum(dc_ref[...], axis=0, keepdims=True)

    return _call(
        body, name=name, grid=(ng,),
        in_specs=[_group_specs(S, ng, False), _group_specs(S, ng, True),
                  pl.BlockSpec((CONV_WP, LANES), lambda g: (g, 0)), pl.BlockSpec((S, LANES), lambda g: (0, g))],
        out_specs=[pl.BlockSpec((S, LANES), lambda g: (0, g)), pl.BlockSpec((S, LANES), lambda g: (0, g)),
                   pl.BlockSpec((CONV_WP, LANES), lambda g: (g, 0)), pl.BlockSpec((1, LANES), lambda g: (0, g))],
        out_shape=[jax.ShapeDtypeStruct((S, CONV_CH), bf16), jax.ShapeDtypeStruct((S, CONV_CH), bf16),
                   jax.ShapeDtypeStruct((ng * CONV_WP, LANES), f32), jax.ShapeDtypeStruct((1, CONV_CH), f32)],
        scratch_shapes=[pltpu.VMEM((S + 2 * CONV_HALO, LANES), f32), pltpu.VMEM((S + 2 * CONV_HALO, LANES), f32)],
        compiler_params=_params(("parallel",), 24 << 20),
    )(conv_in, conv_in, wdw, dc)


def _ln_stats(x, eps=1e-5):
    xc = x - jnp.mean(x, axis=-1, keepdims=True)
    r = lax.rsqrt(jnp.mean(xc * xc, axis=-1, keepdims=True) + eps)
    return r, xc * r


def _ln_bwd(r, xh, dxh):
    return r * (dxh - jnp.mean(dxh, axis=-1, keepdims=True) - xh * jnp.mean(dxh * xh, axis=-1, keepdims=True))


def _conv2_fwd(c, ln_g, ln_b, name):
    def body(c_ref, g_ref, b_ref, o_ref):
        _, xh = _ln_stats(c_ref[...])
        y = xh * g_ref[...] + b_ref[...]
        o_ref[...] = (y * jax.nn.sigmoid(y)).astype(o_ref.dtype)

    return _rows(body, [("r", c), ("f", ln_g), ("f", ln_b)], [("r", CONV_CH, bf16)], tm=min(256, c.shape[0]), name=name)[0]


def _conv2_bwd(c, dcz, ln_g, ln_b, name):
    def body(c_ref, d_ref, g_ref, b_ref, dc_ref, dg_ref, db_ref):
        r, xh = _ln_stats(c_ref[...])
        y = xh * g_ref[...] + b_ref[...]
        sg = jax.nn.sigmoid(y)
        dy = d_ref[...].astype(f32) * (sg * (1.0 + y * (1.0 - sg)))
        dc_ref[...] = _ln_bwd(r, xh, dy * g_ref[...])
        _accumulate(dg_ref, jnp.sum(dy * xh, axis=0, keepdims=True))
        _accumulate(db_ref, jnp.sum(dy, axis=0, keepdims=True))

    return _rows(body, [("r", c), ("r", dcz), ("f", ln_g), ("f", ln_b)],
                 [("r", CONV_CH, f32), ("a", (1, CONV_CH), f32), ("a", (1, CONV_CH), f32)],
                 tm=min(256, c.shape[0]), name=name)


GELU_K = math.sqrt(2.0 / math.pi)
GELU_C = 0.044715


def _gelu(x):
    return 0.5 * x * (1.0 + jnp.tanh(GELU_K * (x + GELU_C * x * x * x)))


def _gelu_and_grad(x):
    x2 = x * x
    th = jnp.tanh(GELU_K * (x + GELU_C * x2 * x))
    half = 0.5 * (1.0 + th)
    return x * half, half + 0.5 * x * (1.0 - th * th) * (GELU_K * (1.0 + 3.0 * GELU_C * x2))


def _chunk_rows(n):
    return pl.ds(pl.multiple_of(n * SG_CHUNK, SG_CHUNK), SG_CHUNK)


def _sgu_fwd(sg_in, ln_g, ln_b, w_s, b_s, name):
    S = sg_in.shape[0]

    def body(u_ref, v_ref, lg_ref, lb_ref, w_ref, b_ref, o_ref):
        wb = w_ref[...].astype(bf16)

        def chunk(n, carry):
            rows = _chunk_rows(n)
            gu = _gelu(u_ref[rows, :].astype(f32))
            _, xh = _ln_stats(_gelu(v_ref[rows, :].astype(f32)))
            vl = xh * lg_ref[...] + lb_ref[...]
            t = jnp.dot(wb, vl.astype(bf16), preferred_element_type=f32) + b_ref[...]
            o_ref[rows, :] = (gu * t).astype(o_ref.dtype)
            return carry

        lax.fori_loop(0, S // SG_CHUNK, chunk, 0, unroll=2)

    return _call(
        body, name=name, grid=(SG_G,),
        in_specs=[_group_specs(S, SG_G, False), _group_specs(S, SG_G, True),
                  pl.BlockSpec((1, LANES), lambda g: (0, g)), pl.BlockSpec((1, LANES), lambda g: (0, g)),
                  pl.BlockSpec((None, SG_CHUNK, SG_CHUNK), lambda g: (g, 0, 0)),
                  pl.BlockSpec((None, SG_CHUNK, 1), lambda g: (g, 0, 0))],
        out_specs=pl.BlockSpec((S, LANES), lambda g: (0, g)),
        out_shape=jax.ShapeDtypeStruct((S, SG_CH), bf16),
        compiler_params=_params(("parallel",), 24 << 20),
    )(sg_in, sg_in, ln_g, ln_b, w_s, b_s)


def _sgu_bwd(sg_in, dsz, ln_g, ln_b, w_s, w_s_t, b_s, name):
    S = sg_in.shape[0]

    def body(u_ref, v_ref, lg_ref, lb_ref, w_ref, wt_ref, b_ref, d_ref, du_ref, dv_ref, dw_ref, db_ref, dlg_ref, dlb_ref):
        wb = w_ref[...].astype(bf16)
        wtb = wt_ref[...].astype(bf16)

        def chunk(n, carry):
            dwa, dba, dlga, dlba = carry
            rows = _chunk_rows(n)
            u = u_ref[rows, :].astype(f32)
            v = v_ref[rows, :].astype(f32)
            gu, gu_grad = _gelu_and_grad(u)
            gv, gv_grad = _gelu_and_grad(v)
            r, xh = _ln_stats(gv)
            vlb = (xh * lg_ref[...] + lb_ref[...]).astype(bf16)
            t = jnp.dot(wb, vlb, preferred_element_type=f32) + b_ref[...]
            d = d_ref[rows, :].astype(f32)
            dt = d * gu
            dtb = dt.astype(bf16)
            dwa = dwa + lax.dot_general(dtb, vlb, (((1,), (1,)), ((), ())), preferred_element_type=f32)
            dba = dba + jnp.sum(dt, axis=1, keepdims=True)
            dvl = jnp.dot(wtb, dtb, preferred_element_type=f32)
            dlga = dlga + jnp.sum(dvl * xh, axis=0, keepdims=True)
            dlba = dlba + jnp.sum(dvl, axis=0, keepdims=True)
            dgv = _ln_bwd(r, xh, dvl * lg_ref[...])
            du_ref[rows, :] = (d * t * gu_grad).astype(du_ref.dtype)
            dv_ref[rows, :] = (dgv * gv_grad).astype(dv_ref.dtype)
            return dwa, dba, dlga, dlba

        init = (jnp.zeros((SG_CHUNK, SG_CHUNK), f32), jnp.zeros((SG_CHUNK, 1), f32),
                jnp.zeros((1, LANES), f32), jnp.zeros((1, LANES), f32))
        dwa, dba, dlga, dlba = lax.fori_loop(0, S // SG_CHUNK, chunk, init, unroll=2)
        dw_ref[...] = dwa
        db_ref[...] = dba
        dlg_ref[...] = dlga
        dlb_ref[...] = dlba

    wspec = pl.BlockSpec((None, SG_CHUNK, SG_CHUNK), lambda g: (g, 0, 0))
    bspec = pl.BlockSpec((None, SG_CHUNK, 1), lambda g: (g, 0, 0))
    lspec = pl.BlockSpec((1, LANES), lambda g: (0, g))
    cspec = pl.BlockSpec((S, LANES), lambda g: (0, g))
    return _call(
        body, name=name, grid=(SG_G,),
        in_specs=[_group_specs(S, SG_G, False), _group_specs(S, SG_G, True), lspec, lspec, wspec, wspec, bspec, cspec],
        out_specs=[cspec, cspec, wspec, bspec, lspec, lspec],
        out_shape=[jax.ShapeDtypeStruct((S, SG_CH), bf16), jax.ShapeDtypeStruct((S, SG_CH), bf16),
                   jax.ShapeDtypeStruct((SG_G, SG_CHUNK, SG_CHUNK), f32), jax.ShapeDtypeStruct((SG_G, SG_CHUNK, 1), f32),
                   jax.ShapeDtypeStruct((1, SG_CH), f32), jax.ShapeDtypeStruct((1, SG_CH), f32)],
        compiler_params=_params(("parallel",), 24 << 20),
    )(sg_in, sg_in, ln_g, ln_b, w_s, w_s_t, b_s, dsz)


def _row_tile(r, c, n_arrays, itemsize=4):
    fits = [tm for tm in range(16, r + 1, 16) if r % tm == 0 and 2 * n_arrays * tm * c * itemsize <= (24 << 20)]
    return fits[-1] if fits else r


def _sum_slots(slots, name):
    n, r, c = slots.shape
    tm = _row_tile(r, c, n + 2)

    def body(s_ref, o_ref):
        acc = s_ref[0].astype(f32)
        for k in range(1, n):
            acc = acc + s_ref[k].astype(f32)
        o_ref[...] = acc

    return _call(body, name=name, grid=(r // tm,),
                 in_specs=[pl.BlockSpec((n, tm, c), lambda i: (0, i, 0))],
                 out_specs=pl.BlockSpec((tm, c), lambda i: (i, 0)),
                 out_shape=jax.ShapeDtypeStruct((r, c), f32),
                 compiler_params=_params(("parallel",), 40 << 20))(slots)


def _add_sibling(g4, recv, core, name):
    _, _, r, c = g4.shape
    tm = _row_tile(r, c, 3, 2)

    def body(core_ref, g_ref, r_ref, o_ref):
        o_ref[...] = (g_ref[...].astype(f32) + r_ref[...].astype(f32)).astype(o_ref.dtype)

    grid_spec = pltpu.PrefetchScalarGridSpec(
        num_scalar_prefetch=1, grid=(N_CHIP, r // tm),
        in_specs=[pl.BlockSpec((None, None, tm, c), lambda k, i, core_ref: (k, core_ref[0], i, 0)),
                  pl.BlockSpec((None, tm, c), lambda k, i, core_ref: (k, i, 0))],
        out_specs=pl.BlockSpec((None, tm, c), lambda k, i, core_ref: (k, i, 0)))
    return _call(body, name=name, grid_spec=grid_spec, out_shape=jax.ShapeDtypeStruct((N_CHIP, r, c), bf16),
                 compiler_params=_params(("parallel", "parallel"), 40 << 20))(core, g4, recv)


def _adamw(w, g, m, v, name):
    L, r, c = w.shape
    tm = _row_tile(r, c, 7)
    c1 = 1.0 - ADAM_B1 ** ADAM_STEP
    c2 = 1.0 - ADAM_B2 ** ADAM_STEP

    def body(w_ref, g_ref, m_ref, v_ref, d_ref, mo_ref, vo_ref):
        g_ = g_ref[...]
        m_ = ADAM_B1 * m_ref[...] + (1.0 - ADAM_B1) * g_
        v_ = ADAM_B2 * v_ref[...] + (1.0 - ADAM_B2) * (g_ * g_)
        d_ref[...] = -ADAM_LR * ((m_ / c1) / (jnp.sqrt(v_ / c2) + ADAM_EPS) + ADAM_WD * w_ref[...])
        mo_ref[...] = m_
        vo_ref[...] = v_

    spec = pl.BlockSpec((None, tm, c), lambda l, i: (l, i, 0))
    shp = jax.ShapeDtypeStruct((L, r, c), f32)
    return _call(body, name=name, grid=(L, r // tm), in_specs=[spec] * 4, out_specs=[spec] * 3,
                 out_shape=[shp] * 3, compiler_params=_params(("parallel", "parallel"), 40 << 20))(w, g, m, v)


def _mesh_pos():
    return lax.axis_index("x"), lax.axis_index("y"), lax.axis_index("c")


SEM = pl.BlockSpec(memory_space=pltpu.SEMAPHORE)
ANY = pl.BlockSpec(memory_space=pl.ANY)
EFFECT = pltpu.SideEffectType.DATAFLOW_SIDE_EFFECTING


def _other_chips(x, y):
    return [(1 - x, y), (x, 1 - y), (1 - x, 1 - y)]


def _peers(kind, x, y):
    return [(x, y)] if kind == "sibling" else _other_chips(x, y)


def _ici_copy(kind, src_ref, land_ref, send_sem, recv_sem, sender, target, c):
    (sx, sy), (tx, ty) = sender, target
    if kind == "sibling":
        return pltpu.make_async_remote_copy(src_ref=src_ref.at[:, 1 - c], dst_ref=land_ref, send_sem=send_sem,
                                            recv_sem=recv_sem, device_id=(tx, ty, 1 - c), device_id_type=MESH)
    if kind == "gather":
        src, dst = src_ref, land_ref.at[4 * sx + 2 * sy + c]
    else:
        src, dst = src_ref.at[2 * tx + ty], land_ref.at[2 * sx + sy]
    return pltpu.make_async_remote_copy(src_ref=src, dst_ref=dst, send_sem=send_sem, recv_sem=recv_sem,
                                        device_id=(tx, ty, c), device_id_type=MESH)


def _ici_start(kind, srcs, lands, after, name):
    n = len(srcs)
    npeer = 1 if kind == "sibling" else 3

    def body(*refs):
        src_refs, land_refs = refs[:n], refs[n:2 * n]
        send_sems, recv_sems = refs[2 * n + 1], refs[2 * n + 2]
        token = refs[-1]
        x, y, c = _mesh_pos()
        for j, chip in enumerate(_peers(kind, x, y)):
            for k in range(n):
                _ici_copy(kind, src_refs[k], land_refs[k], send_sems.at[npeer * k + j], recv_sems.at[npeer * k + j],(x, y), chip, c).start()
        token[...] = jnp.zeros_like(token)

    bufs = list(srcs) + list(lands)
    return _call(
        body, name=name,
        out_shape=(pltpu.SemaphoreType.DMA((npeer * n,)), pltpu.SemaphoreType.DMA((npeer * n,)),
                   *[pltpu.HBM(b.shape, b.dtype) for b in bufs], jax.ShapeDtypeStruct((8, LANES), f32)),
        in_specs=[HBM] * (2 * n) + [ANY], out_specs=(SEM, SEM, *[HBM] * (2 * n), pl.BlockSpec(memory_space=pltpu.VMEM)),
        input_output_aliases={i: 2 + i for i in range(2 * n)},
        compiler_params=pltpu.CompilerParams(has_side_effects=EFFECT),
    )(*[pltpu.with_memory_space_constraint(b, pltpu.HBM) for b in bufs], after)


def _ici_wait(kind, started, after, name):
    send_sems, recv_sems, *bufs = started[:-1]
    n = len(bufs) // 2
    npeer = 1 if kind == "sibling" else 3

    def body(*refs):
        src_refs, land_refs = refs[:n], refs[n:2 * n]
        send_sems, recv_sems = refs[2 * n], refs[2 * n + 1]
        x, y, c = _mesh_pos()
        for j, chip in enumerate(_peers(kind, x, y)):
            for k in range(n):
                _ici_copy(kind, src_refs[k], land_refs[k], send_sems.at[npeer * k + j], recv_sems.at[npeer * k + j],(x, y), chip, c).wait_send()
                _ici_copy(kind, src_refs[k], land_refs[k], send_sems.at[npeer * k + j], recv_sems.at[npeer * k + j],chip, (x, y), c).wait_recv()

    out = _call(
        body, name=name, out_shape=[pltpu.HBM(b.shape, b.dtype) for b in bufs],
        in_specs=[HBM] * (2 * n) + [SEM, SEM, ANY], out_specs=[HBM] * (2 * n),
        input_output_aliases={i: i for i in range(2 * n)},
        compiler_params=pltpu.CompilerParams(has_side_effects=EFFECT),
    )(*bufs, send_sems, recv_sems, after)
    return out[:n], out[n:]


def _d2d_gather(lands, after, name):
    n = len(lands)

    def body(*refs):
        in_refs, o_refs = refs[:n], refs[n + 1:2 * n + 1]
        send_sems, recv_sems = refs[2 * n + 1:]
        x, y, c = _mesh_pos()
        copies = [pltpu.make_async_remote_copy(
            src_ref=in_refs[k].at[:, c], dst_ref=o_refs[k].at[:, c], send_sem=send_sems.at[k], recv_sem=recv_sems.at[k],
            device_id=(x, y, 1 - c), device_id_type=MESH) for k in range(n)]
        for cp in copies:
            cp.start()
        for k, cp in enumerate(copies):
            cp.wait_send()
            pltpu.make_async_remote_copy(
                src_ref=in_refs[k].at[:, c], dst_ref=o_refs[k].at[:, 1 - c], send_sem=send_sems.at[k],
                recv_sem=recv_sems.at[k], device_id=(x, y, 1 - c), device_id_type=MESH).wait_recv()

    return _call(
        body, name=name, in_specs=[HBM] * n + [ANY], out_specs=[HBM] * n,
        out_shape=[jax.ShapeDtypeStruct(b.shape, b.dtype) for b in lands],
        input_output_aliases={k: k for k in range(n)},
        scratch_shapes=[pltpu.SemaphoreType.DMA((n,)), pltpu.SemaphoreType.DMA((n,))],
    )(*lands, after)


def _sum_chip_slots(lands, sums, chip, name):
    _, r, c = lands.shape
    tm = _row_tile(r, c, 10, 2)

    def body(chip_ref, l_ref, s_ref, o_ref):
        acc = None
        for k in range(N_CHIP):
            part = jnp.where(chip_ref[0] == k, s_ref[k], l_ref[k]).astype(f32)
            acc = part if acc is None else acc + part
        o_ref[...] = acc

    grid_spec = pltpu.PrefetchScalarGridSpec(
        num_scalar_prefetch=1, grid=(r // tm,),
        in_specs=[pl.BlockSpec((N_CHIP, tm, c), lambda i, chip_ref: (0, i, 0)),
                  pl.BlockSpec((N_CHIP, tm, c), lambda i, chip_ref: (0, i, 0))],
        out_specs=pl.BlockSpec((tm, c), lambda i, chip_ref: (i, 0)))
    return _call(body, name=name, grid_spec=grid_spec, out_shape=jax.ShapeDtypeStruct((r, c), f32),
                 compiler_params=_params(("parallel",), 40 << 20))(chip, lands, sums)


def _reduce_begin(grads, core, tag):
    g4s = [g.reshape(N_CHIP, 2, g.shape[0] // N_DEV, g.shape[1]) for g in grads]
    recvs = [lax.empty((N_CHIP,) + g.shape[2:], g.dtype) for g in g4s]
    return _ici_start("sibling", g4s, recvs, core, name="rs_d2d_start_" + tag)


def _reduce_continue(begun, core, after, tag):
    g4s, recvs = _ici_wait("sibling", begun, after, name="rs_d2d_wait_" + tag)
    sums = [_add_sibling(g4, rv, core, name="rs_add_" + tag) for g4, rv in zip(g4s, recvs)]
    lands = [lax.empty(s.shape, s.dtype) for s in sums]
    return _ici_start("reduce", sums, lands, core, name="rs_start_" + tag)


def _adamw_reduced(layer, w, m, v, lands, sums, chip, prev, name):
    L, r, c = w.shape
    tm = _row_tile(r, c, 11)
    c1 = 1.0 - ADAM_B1 ** ADAM_STEP
    c2 = 1.0 - ADAM_B2 ** ADAM_STEP
    n_prev = 0 if prev is None else 4

    def body(chip_ref, w_ref, m_ref, v_ref, l_ref, s_ref, *refs):
        g_ref, d_ref, mo_ref, vo_ref = refs[n_prev:]
        g_ = None
        for k in range(N_CHIP):
            part = jnp.where(chip_ref[0] == k, s_ref[k], l_ref[k]).astype(f32)
            g_ = part if g_ is None else g_ + part
        m_ = ADAM_B1 * m_ref[...] + (1.0 - ADAM_B1) * g_
        v_ = ADAM_B2 * v_ref[...] + (1.0 - ADAM_B2) * (g_ * g_)
        g_ref[...] = g_
        d_ref[...] = -ADAM_LR * ((m_ / c1) / (jnp.sqrt(v_ / c2) + ADAM_EPS) + ADAM_WD * w_ref[...])
        mo_ref[...] = m_
        vo_ref[...] = v_

    wspec = pl.BlockSpec((None, tm, c), lambda i, chip_ref: (layer, i, 0))
    sspec = pl.BlockSpec((N_CHIP, tm, c), lambda i, chip_ref: (0, i, 0))
    grid_spec = pltpu.PrefetchScalarGridSpec(
        num_scalar_prefetch=1, grid=(r // tm,), in_specs=[wspec] * 3 + [sspec] * 2 + [ANY] * n_prev, out_specs=[wspec] * 4)
    return _call(body, name=name, grid_spec=grid_spec, out_shape=[jax.ShapeDtypeStruct((L, r, c), f32)] * 4,
                 input_output_aliases={6 + i: i for i in range(n_prev)},
                 compiler_params=_params(("parallel",), 40 << 20))(chip, w, m, v, lands, sums, *(prev or ()))


def _rope_tables(S):
    rows = S // GRID_W
    row = jnp.repeat(jnp.arange(rows, dtype=f32), GRID_W)
    col = jnp.tile(jnp.arange(GRID_W, dtype=f32), rows)
    nf = HEAD_DIM // 4
    inv = ROPE_THETA ** (-jnp.arange(nf, dtype=f32) / nf)
    ang = jnp.concatenate([row[:, None] * inv, col[:, None] * inv], axis=-1)
    cos, sin = jnp.cos(ang), jnp.sin(ang)
    return jnp.concatenate([cos, cos], axis=-1), jnp.concatenate([-sin, sin], axis=-1)


def _layer_fwd(xin, p, w, more_weights, cos2, sin2):
    sv = {"xin": xin}
    h = sv["h"] = _rms_fwd(xin, p["g_mix"], name="rms_mix")
    proj = functools.partial(_mm, h, w["in"], "nt", bf16)
    q_raw = sv["q_raw"] = proj(n=Q_COLS, b_off=0, name="proj_q")
    kv_raw = sv["kv_raw"] = proj(n=2 * KV_COLS, b_off=OFF_KV, name="proj_kv")
    conv_in = sv["conv_in"] = proj(n=2 * CONV_CH, b_off=OFF_CONV, name="proj_conv")
    sg_in = sv["sg_in"] = proj(n=2 * SG_CH, b_off=OFF_SG, name="proj_sg")
    gl = sv["gl"] = proj(n=3 * D_MODEL, b_off=OFF_GATE, name="proj_gate")
    qr, kr = sv["qr"], sv["kr"] = _qk_fwd(q_raw, kv_raw, p["q_norm_g"], p["k_norm_g"], cos2, sin2, name="qk_fwd")
    o = sv["o"] = _attn_fwd(qr, kr, kv_raw, name="attn_fwd")
    c = sv["c"] = _conv1_fwd(conv_in, w["dw"], p["b_dw"], name="conv1_fwd")
    cz = sv["cz"] = _conv2_fwd(c, p["conv_ln_g"], p["conv_ln_b"], name="conv2_fwd")
    sz = sv["sz"] = _sgu_fwd(sg_in, p["sg_ln_g"], p["sg_ln_b"], p["w_s"], p["b_s"], name="sgu_fwd")
    w = {**w, **more_weights(1, sz)}
    sv["ya"], sv["yc"], sv["ys"], merged = _mixer_out([o, cz, sz], [w["attn_o"], w["conv_o"], w["sg_o"]], gl, p["b_gate"],
                                                      name="mixer_out")
    sv["merged"] = merged
    x1 = sv["x1"] = _mm(merged, w["out"], "nn", f32, res=xin, name="out_proj")
    w = {**w, **more_weights(2, x1)}
    hf = sv["hf"] = _rms_fwd(x1, p["g_ffn"], name="rms_ffn")
    sv["fg"], sv["fu"], act = _ffn_up(hf, w["ff_gate"], w["ff_up"], name="ffn_up")
    sv["act"] = act
    x2 = _mm(act, w["ff_down"], "nn", f32, res=x1, name="ff_down")
    return x2, sv, w


def _layer_bwd(dx2, dx2b, sv, p, w, cos2, sin2, reduce_begin, reduce_continue, last):
    small = {}
    dfg, dfu = _ffn_down_bwd(dx2b, w["ff_down"], sv["fg"], sv["fu"], name="ffn_down_bwd")
    g_down = _mm(sv["act"], dx2b, "tn", bf16, name="g_ff_down")
    dhf = _mm(dfg, w["ff_gate"], "nn", f32, name="d_hf_gate")
    dhf = _mm(dfu, w["ff_up"], "nn", f32, res=dhf, name="d_hf_up")
    g_gate = _mm(dfg, sv["hf"], "tn", bf16, name="g_ff_gate")
    g_up = _mm(dfu, sv["hf"], "tn", bf16, name="g_ff_up")
    zero = reduce_begin("ffn", dict(w_ff_gate=g_gate, w_ff_up=g_up, w_ff_down=g_down))[0, 0]
    dx1, dx1b, small["g_ffn"] = _rms_bwd(sv["x1"], p["g_ffn"] + zero, dhf, dx2, name="rms_ffn_bwd")
    g_out = _mm(sv["merged"], dx1b, "tn", bf16, name="g_out")
    *dgl, dya, dyc, dys, db0, db1, db2 = _merge_bwd_fused(dx1b, w["out"], sv["gl"], p["b_gate"], sv["ya"], sv["yc"], sv["ys"],
                                                        name="merge_bwd")
    small["b_gate"] = jnp.concatenate([db0, db1, db2], axis=1)
    do = _mm(dya, w["attn_o"], "nn", bf16, after=reduce_continue("ffn", dya), name="d_o")
    g_ao = _mm(dya, sv["o"], "tn", bf16, name="g_attn_o")
    dcz = _mm(dyc, w["conv_o"], "nn", bf16, name="d_cz")
    g_co = _mm(dyc, sv["cz"], "tn", bf16, name="g_conv_o")
    dsz = _mm(dys, w["sg_o"], "nn", bf16, name="d_sz")
    g_so = _mm(dys, sv["sz"], "tn", bf16, name="g_sg_o")
    zero = reduce_begin("mix", dict(w_attn_o=g_ao, w_conv_o=g_co, w_sg_o=g_so, w_out=g_out))[0, 0]
    dsu, dsv, small["w_s"], small["b_s"], small["sg_ln_g"], small["sg_ln_b"] = _sgu_bwd(
        sv["sg_in"], dsz, p["sg_ln_g"] + zero, p["sg_ln_b"], p["w_s"], p["w_s_t"], p["b_s"], name="sgu_bwd")
    dc, small["conv_ln_g"], small["conv_ln_b"] = _conv2_bwd(sv["c"], dcz, p["conv_ln_g"], p["conv_ln_b"], name="conv2_bwd")
    da, dgt, small["w_dw"], small["b_dw"] = _conv1_bwd(sv["conv_in"], dc, w["dw"], name="conv1_bwd")
    zero = reduce_continue("mix", da)[0, 0]
    dqr, dkr, dv = _attn_bwd(sv["qr"], sv["kr"], sv["kv_raw"], do, name="attn_bwd")
    dq_raw, dk_raw, small["q_norm_g"], small["k_norm_g"] = _qk_bwd(
        sv["q_raw"], sv["kv_raw"], dqr, dkr, p["q_norm_g"] + zero, p["k_norm_g"], cos2, sin2, name="qk_bwd")
    dproj = jnp.concatenate([dq_raw, dk_raw, dv.astype(bf16), da, dgt, dsu, dsv, *dgl], axis=1)
    g_in = _mm(dproj, sv["h"], "tn", bf16, name="g_in")
    begun = reduce_begin("in", dict(w_in=g_in))
    if last:
        begun = reduce_continue("in", begun)
    dh = _mm(dproj, w["in"], "nn", f32, after=begun, name="d_h")
    zero = begun[0, 0] if last else reduce_continue("in", dh)[0, 0]
    dx, dxb, small["g_mix"] = _rms_bwd(sv["xin"], p["g_mix"] + zero, dh, dx1, name="rms_mix_bwd")
    return dx, dxb, small


SMALL = ("g_mix", "b_gate", "q_norm_g", "k_norm_g", "b_dw", "conv_ln_g", "conv_ln_b", "sg_ln_g", "sg_ln_b",
         "w_s", "b_s", "g_ffn")
PACK_ALIGN = 8 * LANES


def _pack(parts):
    flat = jnp.concatenate([a.reshape(-1).astype(f32) for a in parts])
    pad = -flat.shape[0] % PACK_ALIGN
    return jnp.pad(flat, (0, pad)).reshape(-1, LANES)


def _unpack(buf, shapes):
    flat = buf.reshape(-1)
    out, pos = [], 0
    for shp in shapes:
        size = math.prod(shp)
        out.append(flat[pos:pos + size].reshape(shp))
        pos += size
    return out


def kernel(x, g_mix, w_in, b_gate, q_norm_g, k_norm_g, w_attn_o, w_dw, b_dw, conv_ln_g, conv_ln_b, w_conv_o, sg_ln_g, sg_ln_b, w_s, b_s, w_sg_o, w_out, g_ffn, w_ff_gate, w_ff_up, w_ff_down, g_final, loss_target, m_g_mix, m_w_in, m_b_gate, m_q_norm_g, m_k_norm_g, m_w_attn_o, m_w_dw, m_b_dw, m_conv_ln_g, m_conv_ln_b, m_w_conv_o, m_sg_ln_g, m_sg_ln_b, m_w_s, m_b_s, m_w_sg_o, m_w_out, m_g_ffn, m_w_ff_gate, m_w_ff_up, m_w_ff_down, m_g_final, v_g_mix, v_w_in, v_b_gate, v_q_norm_g, v_k_norm_g, v_w_attn_o, v_w_dw, v_b_dw, v_conv_ln_g, v_conv_ln_b, v_w_conv_o, v_sg_ln_g, v_sg_ln_b, v_w_s, v_b_s, v_w_sg_o, v_w_out, v_g_ffn, v_w_ff_gate, v_w_ff_up, v_w_ff_down, v_g_final):
    weights = dict(g_mix=g_mix, w_in=w_in, b_gate=b_gate, q_norm_g=q_norm_g, k_norm_g=k_norm_g, w_attn_o=w_attn_o,
                   w_dw=w_dw, b_dw=b_dw, conv_ln_g=conv_ln_g, conv_ln_b=conv_ln_b, w_conv_o=w_conv_o, sg_ln_g=sg_ln_g,
                   sg_ln_b=sg_ln_b, w_s=w_s, b_s=b_s, w_sg_o=w_sg_o, w_out=w_out, g_ffn=g_ffn, w_ff_gate=w_ff_gate,
                   w_ff_up=w_ff_up, w_ff_down=w_ff_down, g_final=g_final)
    mom_m = dict(g_mix=m_g_mix, w_in=m_w_in, b_gate=m_b_gate, q_norm_g=m_q_norm_g, k_norm_g=m_k_norm_g,
                 w_attn_o=m_w_attn_o, w_dw=m_w_dw, b_dw=m_b_dw, conv_ln_g=m_conv_ln_g, conv_ln_b=m_conv_ln_b,
                 w_conv_o=m_w_conv_o, sg_ln_g=m_sg_ln_g, sg_ln_b=m_sg_ln_b, w_s=m_w_s, b_s=m_b_s, w_sg_o=m_w_sg_o,
                 w_out=m_w_out, g_ffn=m_g_ffn, w_ff_gate=m_w_ff_gate, w_ff_up=m_w_ff_up, w_ff_down=m_w_ff_down,
                 g_final=m_g_final)
    mom_v = dict(g_mix=v_g_mix, w_in=v_w_in, b_gate=v_b_gate, q_norm_g=v_q_norm_g, k_norm_g=v_k_norm_g,
                 w_attn_o=v_w_attn_o, w_dw=v_w_dw, b_dw=v_b_dw, conv_ln_g=v_conv_ln_g, conv_ln_b=v_conv_ln_b,
                 w_conv_o=v_w_conv_o, sg_ln_g=v_sg_ln_g, sg_ln_b=v_sg_ln_b, w_s=v_w_s, b_s=v_b_s, w_sg_o=v_w_sg_o,
                 w_out=v_w_out, g_ffn=v_g_ffn, w_ff_gate=v_w_ff_gate, w_ff_up=v_w_ff_up, w_ff_down=v_w_ff_down,
                 g_final=v_g_final)
    S, D = x.shape[1], x.shape[2]
    xi, yi, ci = _mesh_pos()
    me = 4 * xi + 2 * yi + ci
    core = jnp.reshape(ci, (1,)).astype(jnp.int32)
    cos2, sin2 = _rope_tables(S)

    big = ("w_in", "w_attn_o", "w_conv_o", "w_sg_o", "w_out", "w_ff_gate", "w_ff_up", "w_ff_down")
    transposed = {"w_in", "w_attn_o", "w_conv_o", "w_sg_o", "w_ff_gate", "w_ff_up"}
    chip = jnp.reshape(2 * xi + yi, (1,)).astype(jnp.int32)
    groups = (("in", "dw"), ("attn_o", "conv_o", "sg_o", "out"), ("ff_gate", "ff_up", "ff_down"))
    P, shards = [], []
    for l in range(DEPTH):
        sh = {n[2:]: (weights[n][l].T if n in transposed else weights[n][l]).astype(bf16) for n in big}
        sh["dw"] = jnp.pad(w_dw[l].reshape(CONV_W, LANES), ((0, CONV_WP - CONV_W), (0, 0)))
        shards.append(sh)
        p = {n: weights[n][l].reshape(1, -1) for n in SMALL if n not in ("w_s", "b_s")}
        p["w_s"] = w_s[l]
        p["w_s_t"] = jnp.swapaxes(w_s[l], 1, 2)
        p["b_s"] = b_s[l].reshape(SG_G, SG_CHUNK, 1)
        P.append(p)

    gathers = {}

    def start_gather(l, gi, after):
        srcs = [shards[l][n] for n in groups[gi]]
        lands = [lax.dynamic_update_index_in_dim(lax.empty((N_DEV,) + s.shape, s.dtype), s, me, 0) for s in srcs]
        gathers[l, gi] = _ici_start("gather", srcs, lands, after, name=f"ag_start_{l}{gi}")
        return gathers[l, gi][-1]

    def gathered(l, gi, after):
        srcs, lands = _ici_wait("gather", gathers[l, gi], after, name=f"ag_wait_{l}{gi}")
        after = srcs[0]
        if gi == len(groups) - 1 and l + 1 < DEPTH:
            for gj in range(len(groups)):
                after = start_gather(l + 1, gj, after)
        full = _d2d_gather([b.reshape(N_CHIP, 2, *b.shape[1:]) for b in lands], after, name=f"ag_d2d_{gi}")
        return {n: f.reshape(-1, f.shape[3]) for n, f in zip(groups[gi], full)}

    all_started = cos2
    for gi in range(len(groups)):
        all_started = start_gather(0, gi, all_started)

    h = x.reshape(S, D)
    saved, W = [], []
    for l in range(DEPTH):
        first = gathered(l, 0, all_started if l == 0 else h)
        if l == 0:
            P[l]["g_mix"] = P[l]["g_mix"] + all_started[0, 0]
        h, sv, w = _layer_fwd(h, P[l], first, functools.partial(lambda gi, z, l: gathered(l, gi, z), l=l), cos2, sin2)
        saved.append(sv)
        W.append(w)
    dx, dxb, sq, g_final_part = _final_loss(h, g_final.reshape(1, D), loss_target.reshape(S, D), name="final_loss")
    loss = lax.psum(0.5 * jnp.sum(sq) / D, ("x", "y", "c"))

    begun, reductions, small_grads = {}, {}, [None] * DEPTH
    for l in reversed(range(DEPTH)):
        def reduce_begin(group, grads, l=l):
            begun[l, group] = (tuple(grads), _reduce_begin(list(grads.values()), core, tag=f"{group}{l}"))
            return begun[l, group][1][-1]

        def reduce_continue(group, after, l=l):
            names, started = begun[l, group]
            reductions[l, group] = (names, _reduce_continue(started, core, after, tag=f"{group}{l}"))
            return reductions[l, group][1][-1]

        dx, dxb, small_grads[l] = _layer_bwd(dx, dxb, saved[l], P[l], W[l], cos2, sin2, reduce_begin, reduce_continue,
                                            last=(l == 0))
    grad_x = dx.reshape(x.shape)

    small_shapes = [weights[n].shape for n in SMALL] + [g_final.shape, (DEPTH, CONV_CH // LANES, CONV_WP, LANES)]
    parts = [jnp.stack([small_grads[l][n].reshape(weights[n].shape[1:]) for l in range(DEPTH)]) for n in SMALL]
    parts += [g_final_part.reshape(g_final.shape), jnp.stack([small_grads[l]["w_dw"] for l in range(DEPTH)])]
    packed = _pack(parts)
    packed_land = lax.dynamic_update_index_in_dim(lax.empty((N_DEV,) + packed.shape, f32), packed, me, 0)
    small_started = _ici_start("gather", [packed], [packed_land], dx, name="gather_small_start")

    grads_out, delta, new_m, new_v = {}, {}, {}, {}
    swap = lambda a: jnp.swapaxes(a, 1, 2)

    def update(n, lands, sums):
        as_arrives = n not in transposed or weights[n].shape[2] % LANES != 0
        if as_arrives:
            to_arrival = swap if n in transposed else (lambda a: a)
            out = None
            for l in reversed(range(DEPTH)):
                out = _adamw_reduced(l, to_arrival(weights[n]), to_arrival(mom_m[n]), to_arrival(mom_v[n]),
                                     lands[l], sums[l], chip, out, name=f"adamw_{n}_{l}")
            grads_out[n], delta[n], new_m[n], new_v[n] = [to_arrival(o) for o in out]
            return out[1]
        g = jnp.stack([_sum_chip_slots(lands[l], sums[l], chip, name="rs_sum_" + n) for l in range(DEPTH)])
        grads_out[n] = swap(g)
        delta[n], new_m[n], new_v[n] = _adamw(weights[n], grads_out[n], mom_m[n], mom_v[n], name="adamw_" + n)
        return delta[n]

    after = small_started[-1]
    for group in ("ffn", "mix", "in"):
        names = reductions[0, group][0]
        arrived = [_ici_wait("reduce", reductions[l, group][1], after, name=f"rs_wait_{group}{l}") for l in range(DEPTH)]
        for i, n in enumerate(names):
            after = update(n, [arrived[l][1][i] for l in range(DEPTH)], [arrived[l][0][i] for l in range(DEPTH)])

    _, small_lands = _ici_wait("gather", small_started, after, name="gather_small_wait")
    small_full = _d2d_gather([small_lands[0].reshape(N_CHIP, 2, *packed.shape)], after, name="gather_small_d2d")[0]
    total = _sum_slots(small_full.reshape(N_DEV, *packed.shape), name="sum_small")
    small_total = _unpack(total, small_shapes)
    grads_out.update(zip(SMALL + ("g_final",), small_total[:-1]))
    dw_full = small_total[-1]
    grads_out["w_dw"] = lax.dynamic_index_in_dim(dw_full, me, axis=1, keepdims=False)[:, :CONV_W].reshape(w_dw.shape)

    rep = tuple(n for n in SMALL if n != "w_s") + ("g_final",)
    rep_shapes = [weights[n].shape for n in rep]
    packs = [_pack([src[n] for n in rep])[None] for src in (weights, grads_out, mom_m, mom_v)]
    for dst, buf in zip((delta, new_m, new_v), _adamw(*packs, name="adamw_small")):
        dst.update(zip(rep, _unpack(buf[0], rep_shapes)))
    for n, shp in (("w_dw", (1, DEPTH * CONV_W, LANES)), ("w_s", (DEPTH, SG_G * SG_CHUNK, SG_CHUNK))):
        upd = _adamw(*[src[n].reshape(shp) for src in (weights, grads_out, mom_m, mom_v)], name="adamw_" + n)
        for dst, buf in zip((delta, new_m, new_v), upd):
            dst[n] = buf.reshape(weights[n].shape)

    order = ("g_mix", "w_in", "b_gate", "q_norm_g", "k_norm_g", "w_attn_o", "w_dw", "b_dw", "conv_ln_g", "conv_ln_b",
             "w_conv_o", "sg_ln_g", "sg_ln_b", "w_s", "b_s", "w_sg_o", "w_out", "g_ffn", "w_ff_gate", "w_ff_up",
             "w_ff_down", "g_final")
    return (loss, grad_x, *[grads_out[n] for n in order], *[delta[n] for n in order],
            *[new_m[n] for n in order], *[new_v[n] for n in order])
```

```python
import functools
import math

import jax
import jax.numpy as jnp
from jax import lax
from jax.experimental import pallas as pl
from jax.experimental.pallas import tpu as pltpu

f32, bf16 = jnp.float32, jnp.bfloat16

D_MODEL = 2048
SEQ = 2048
DEPTH = 2
GRID_W = 64
HEAD_DIM = 128
LANES = 128
N_Q = (D_MODEL // 2) // HEAD_DIM
N_KV = N_Q // 4
GRP = N_Q // N_KV
Q_COLS = N_Q * HEAD_DIM
KV_COLS = N_KV * HEAD_DIM
CONV_CH = D_MODEL // 2
CONV_W = 31
CONV_PAD = CONV_W // 2
CONV_WP = 32
SG_CH = D_MODEL // 2
SG_G = SG_CH // LANES
SG_CHUNK = 128
D_FF = -(-8 * D_MODEL // (3 * 256)) * 256
OFF_KV = Q_COLS
OFF_CONV = OFF_KV + 2 * KV_COLS
OFF_SG = OFF_CONV + 2 * CONV_CH
OFF_GATE = OFF_SG + 2 * SG_CH
IN_COLS = OFF_GATE + 3 * D_MODEL
ROPE_THETA = 10000.0
SCALE = HEAD_DIM ** -0.5
N_DEV = 8
N_CHIP = 4

ADAM_LR, ADAM_B1, ADAM_B2, ADAM_EPS, ADAM_WD, ADAM_STEP = 0.001, 0.9, 0.999, 1e-08, 0.01, 10

VMEM_BYTES_V7X = 64 << 20
VMEM_CAP = VMEM_BYTES_V7X - (6 << 20)
MESH = pl.DeviceIdType.MESH
HBM = pl.BlockSpec(memory_space=pltpu.HBM)


def _in_hbm(a):
    if isinstance(a, jax.Array) and jnp.issubdtype(a.dtype, jnp.floating) and a.size * a.dtype.itemsize >= (1 << 20):
        return pltpu.with_memory_space_constraint(a, pltpu.HBM)
    return a


def _out_hbm(s):
    if isinstance(s, jax.ShapeDtypeStruct) and math.prod(s.shape) * jnp.dtype(s.dtype).itemsize >= (1 << 20):
        return pltpu.HBM(s.shape, s.dtype)
    return s


def _call(body, **kw):
    shapes = kw.pop("out_shape")
    shapes = type(shapes)(_out_hbm(s) for s in shapes) if isinstance(shapes, (list, tuple)) else _out_hbm(shapes)
    call = pl.pallas_call(body, out_shape=shapes, **kw)
    return lambda *args: call(*[_in_hbm(a) for a in args])


def _pick(n, cands):
    for c in cands:
        if n % c == 0:
            return c
    raise ValueError((n, cands))


def _params(sem, vmem_bytes):
    return pltpu.CompilerParams(dimension_semantics=sem, vmem_limit_bytes=int(min(max(vmem_bytes, 16 << 20), VMEM_CAP)))


def _mm(a, b, form, out_dtype, *, n=None, b_off=0, res=None, after=None, rider=None, name):
    if form == "tn":
        K, M = a.shape
    else:
        M, K = a.shape
    N = n if n is not None else (b.shape[0] if form == "nt" else b.shape[1])
    if K <= 2048:
        tk = K
        if form == "tn":
            tm = _pick(M, (512, 256, 128))
            tn = N if N <= 2048 else _pick(N, (1024, 512, 256, 128))
        else:
            tm = M if M <= 2048 else _pick(M, (2048, 1024, 512))
            tn = _pick(math.gcd(N, b_off) if b_off else N, (256, 128) if res is not None else (512, 256, 128))
    else:
        tk = max(t for t in range(LANES, 3072 + 1, LANES) if K % t == 0)
        tm = _pick(M, (1024, 512, 256, 128))
        tn = _pick(math.gcd(N, b_off) if b_off else N, (1024, 512, 256, 128))
    assert b_off % tn == 0
    off = b_off // tn
    nk = K // tk
    if form == "tn":
        a_spec = pl.BlockSpec((tk, tm), lambda i, j, k: (k, i))
    else:
        a_spec = pl.BlockSpec((tm, tk), lambda i, j, k: (i, k))
    if form == "nt":
        b_spec = pl.BlockSpec((tn, tk), lambda i, j, k: (j + off, k))
    else:
        b_spec = pl.BlockSpec((tk, tn), lambda i, j, k: (k, j + off))
    dims = {"nn": ((1,), (0,)), "nt": ((1,), (1,)), "tn": ((0,), (0,))}[form]
    has_res = res is not None

    n_ride = len(rider["args"]) if rider else 0

    def body(*refs):
        if after is not None:
            refs = refs[1:]
        n_in = 3 if has_res else 2
        if rider:
            ride_refs = refs[n_in:n_in + n_ride] + refs[n_in + n_ride + 1:n_in + n_ride + 1 + len(rider["out_specs"])]

            @pl.when(jnp.logical_and(pl.program_id(1) == 0, pl.program_id(2) == 0))
            def _():
                rider["body"](*ride_refs)

            refs = refs[:n_in] + (refs[n_in + n_ride],) + refs[n_in + n_ride + 1 + len(rider["out_specs"]):]
        if has_res:
            a_ref, b_ref, r_ref, o_ref = refs[:4]
        else:
            a_ref, b_ref, o_ref = refs[:3]
        p = lax.dot_general(a_ref[...], b_ref[...], (dims, ((), ())), preferred_element_type=f32)

        def finish(acc):
            if has_res:
                acc = acc + r_ref[...].astype(f32)
            o_ref[...] = acc.astype(o_ref.dtype)

        if nk == 1:
            finish(p)
        else:
            acc_ref = refs[-1]
            k = pl.program_id(2)

            @pl.when(k == 0)
            def _():
                acc_ref[...] = p

            @pl.when(k > 0)
            def _():
                acc_ref[...] += p

            @pl.when(k == nk - 1)
            def _():
                finish(acc_ref[...])

    in_specs = [a_spec, b_spec]
    args = [a, b]
    osz = jnp.dtype(out_dtype).itemsize
    vmem = 2 * (tm * tk * 2 + tk * tn * 2 + tm * tn * osz) + 2 * tm * tn * 4
    if has_res:
        in_specs.append(pl.BlockSpec((tm, tn), lambda i, j, k: (i, j)))
        args.append(res)
        vmem += 2 * tm * tn * res.dtype.itemsize
    scratch = []
    if nk > 1:
        scratch.append(pltpu.VMEM((tm, tn), f32))
        vmem += tm * tn * 4
    out_specs = [pl.BlockSpec((tm, tn), lambda i, j, k: (i, j))]
    out_shape = [jax.ShapeDtypeStruct((M, N), out_dtype)]
    if rider:
        in_specs += rider["in_specs"]
        args += rider["args"]
        out_specs += rider["out_specs"]
        out_shape += rider["out_shape"]
        vmem += rider["vmem"]
    if after is not None:
        in_specs.insert(0, pl.BlockSpec(memory_space=pl.ANY))
        args.insert(0, after)
    out = _call(
        body, name=name, grid=(M // tm, N // tn, nk), in_specs=in_specs, out_specs=out_specs, out_shape=out_shape,
        scratch_shapes=scratch,
        compiler_params=_params(("arbitrary" if rider else "parallel", "parallel", "arbitrary"), vmem + (8 << 20)),
    )(*args)
    return (out[0], out[1:]) if rider else out[0]


EPI_TN = 256


def _mm_epi(a, bs, form, extras, out_dtypes, n_sums, fn, name, tn=EPI_TN):
    a_list = list(a) if isinstance(a, (list, tuple)) else [a]
    M, K = a_list[0].shape
    N = bs[0].shape[0] if form == "nt" else bs[0].shape[1]
    assert K <= 2048 and N % tn == 0 and len(a_list) in (1, len(bs))
    dims = ((1,), (1,)) if form == "nt" else ((1,), (0,))
    na, nb, ne = len(a_list), len(bs), len(extras)

    def body(*refs):
        a_refs, b_refs = refs[:na], refs[na:na + nb]
        e_refs, o_refs = refs[na + nb:na + nb + ne], refs[na + nb + ne:]
        avs = [r[...] for r in a_refs] * (nb // na)
        ps = [lax.dot_general(av, b[...], (dims, ((), ())), preferred_element_type=f32) for av, b in zip(avs, b_refs)]
        for o_ref, o in zip(o_refs, fn(ps, [e[...] for e in e_refs])):
            o_ref[...] = o.astype(o_ref.dtype)

    in_specs = [pl.BlockSpec((M, K), lambda j: (0, 0), pipeline_mode=pl.Buffered(1)) for _ in a_list]
    in_specs += [pl.BlockSpec((tn, K), lambda j: (j, 0)) if form == "nt" else pl.BlockSpec((K, tn), lambda j: (0, j))
                 for _ in bs]
    for arr, first in extras:
        assert first % tn == 0
        in_specs.append(pl.BlockSpec((arr.shape[0], tn), functools.partial(lambda j, o: (0, j + o), o=first // tn)))
    out_specs = [pl.BlockSpec((M, tn), lambda j: (0, j))] * len(out_dtypes) + [pl.BlockSpec((1, tn), lambda j: (0, j))] * n_sums
    out_shape = [jax.ShapeDtypeStruct((M, N), dt) for dt in out_dtypes] + [jax.ShapeDtypeStruct((1, N), f32)] * n_sums
    tiles = sum(arr.shape[0] * tn * arr.dtype.itemsize for arr, _ in extras) + sum(M * tn * jnp.dtype(dt).itemsize for dt in out_dtypes)
    vmem = na * M * K * 2 + 2 * nb * tn * K * 2 + 2 * tiles + (nb + 6) * M * tn * 4
    return _call(body, name=name, grid=(N // tn,), in_specs=in_specs, out_specs=out_specs, out_shape=out_shape,
                 compiler_params=_params(("parallel",), vmem + (8 << 20)))(*a_list, *bs, *[arr for arr, _ in extras])


def _ffn_up(hf, wt_gate, wt_up, name):
    def fn(ps, _):
        g, u = ps[0].astype(bf16), ps[1].astype(bf16)
        gf = g.astype(f32)
        return g, u, gf * jax.nn.sigmoid(gf) * u.astype(f32)

    return _mm_epi(hf, [wt_gate, wt_up], "nt", [], [bf16] * 3, 0, fn, name, tn=2 * EPI_TN)


def _ffn_down_bwd(dx2b, w_down, fg, fu, name):
    def fn(ps, es):
        d, g = ps[0], es[0].astype(f32)
        sg = jax.nn.sigmoid(g)
        return d * es[1].astype(f32) * sg * (1.0 + g * (1.0 - sg)), d * g * sg

    return _mm_epi(dx2b, [w_down], "nt", [(fg, 0), (fu, 0)], [bf16] * 2, 0, fn, name, tn=2 * EPI_TN)


def _mixer_out(branches, wts, gl, b_gate, name):
    D = wts[0].shape[0]

    def fn(ps, es):
        ys = [p_.astype(bf16) for p_ in ps]
        merged = None
        for i in range(3):
            term = jax.nn.sigmoid(es[i].astype(f32) + es[3 + i]) * ys[i].astype(f32)
            merged = term if merged is None else merged + term
        return ys + [merged]

    extras = [(gl, i * D) for i in range(3)] + [(b_gate, i * D) for i in range(3)]
    return _mm_epi(branches, wts, "nt", extras, [bf16] * 4, 0, fn, name)


def _merge_bwd_fused(dx1b, w_out, gl, b_gate, ya, yc, ys, name):
    D = ya.shape[1]

    def fn(ps, es):
        dm_, outs, sums = ps[0], [], []
        for i in range(3):
            gate = jax.nn.sigmoid(es[i].astype(f32) + es[3 + i])
            dlog = dm_ * es[6 + i].astype(f32) * gate * (1.0 - gate)
            outs.append((dlog, dm_ * gate))
            sums.append(jnp.sum(dlog, axis=0, keepdims=True))
        return [o[0] for o in outs] + [o[1] for o in outs] + sums

    extras = [(gl, i * D) for i in range(3)] + [(b_gate, i * D) for i in range(3)] + [(ya, 0), (yc, 0), (ys, 0)]
    return _mm_epi(dx1b, [w_out], "nt", extras, [bf16] * 6, 3, fn, name)


def _rows(body, ins, outs, *, tm, name, vmem=40 << 20):
    nrows = next(s[1].shape[0] for s in ins if s[0] == "r")
    in_specs, args = [], []
    for s in ins:
        arr = s[1]
        if s[0] == "r":
            w = s[2] if len(s) > 2 else arr.shape[1]
            cb = s[3] if len(s) > 3 else 0
            in_specs.append(pl.BlockSpec((tm, w), functools.partial(lambda i, cb: (i, cb), cb=cb)))
        else:
            in_specs.append(pl.BlockSpec(arr.shape, functools.partial(lambda i, nd: (0,) * nd, nd=arr.ndim)))
        args.append(arr)
    out_specs, out_shape = [], []
    for s in outs:
        if s[0] == "r":
            out_specs.append(pl.BlockSpec((tm, s[1]), lambda i: (i, 0)))
            out_shape.append(jax.ShapeDtypeStruct((nrows, s[1]), s[2]))
        else:
            out_specs.append(pl.BlockSpec(s[1], functools.partial(lambda i, nd: (0,) * nd, nd=len(s[1]))))
            out_shape.append(jax.ShapeDtypeStruct(s[1], s[2]))
    return _call(body, name=name, grid=(nrows // tm,), in_specs=in_specs, out_specs=out_specs,
                 out_shape=out_shape, compiler_params=_params(("arbitrary",), vmem))(*args)


def _accumulate(ref, part):
    i = pl.program_id(0)

    @pl.when(i == 0)
    def _():
        ref[...] = part

    @pl.when(i > 0)
    def _():
        ref[...] += part


def _rms_stats(x):
    r = lax.rsqrt(jnp.mean(x * x, axis=-1, keepdims=True) + 1e-6)
    return r, x * r


def _rms_fwd(x, g, name):
    def body(x_ref, g_ref, o_ref):
        _, xn = _rms_stats(x_ref[...])
        o_ref[...] = (xn * g_ref[...]).astype(o_ref.dtype)

    return _rows(body, [("r", x), ("f", g)], [("r", x.shape[1], bf16)], tm=min(256, x.shape[0]), name=name)[0]


def _rms_bwd(x, g, dh, dres, name):
    D = x.shape[1]

    def body(x_ref, g_ref, dh_ref, dr_ref, dx_ref, dxb_ref, dg_ref):
        r, xn = _rms_stats(x_ref[...])
        dy = dh_ref[...].astype(f32)
        dxn = dy * g_ref[...]
        dx = dr_ref[...] + r * (dxn - xn * jnp.mean(dxn * xn, axis=-1, keepdims=True))
        dx_ref[...] = dx
        dxb_ref[...] = dx.astype(bf16)
        _accumulate(dg_ref, jnp.sum(dy * xn, axis=0, keepdims=True))

    return _rows(body, [("r", x), ("f", g), ("r", dh), ("r", dres)],
                 [("r", D, f32), ("r", D, bf16), ("a", (1, D), f32)], tm=min(256, x.shape[0]), name=name)


def _final_loss(x, g, tgt, name):
    D = x.shape[1]

    def body(x_ref, g_ref, t_ref, dx_ref, dxb_ref, sq_ref, dg_ref):
        r, xn = _rms_stats(x_ref[...])
        gain = g_ref[...]
        diff = xn * gain - t_ref[...]
        dy = diff * (1.0 / D)
        dxn = dy * gain
        dx = r * (dxn - xn * jnp.mean(dxn * xn, axis=-1, keepdims=True))
        dx_ref[...] = dx
        dxb_ref[...] = dx.astype(bf16)
        _accumulate(sq_ref, jnp.sum(diff * diff, axis=0, keepdims=True))
        _accumulate(dg_ref, jnp.sum(dy * xn, axis=0, keepdims=True))

    return _rows(body, [("r", x), ("f", g), ("r", tgt)],
                 [("r", D, f32), ("r", D, bf16), ("a", (1, D), f32), ("a", (1, D), f32)],
                 tm=min(256, x.shape[0]), name=name)


def _qk_fwd(q_raw, kv_raw, qg, kg, cos2, sin2, name):
    def body(q_ref, k_ref, qg_ref, kg_ref, c_ref, s_ref, qo_ref, ko_ref):
        c, s = c_ref[...], s_ref[...]

        def head(src, gain, dst, h):
            cols = slice(h * HEAD_DIM, (h + 1) * HEAD_DIM)
            _, xn = _rms_stats(src[:, cols].astype(f32))
            y = xn * gain
            dst[:, cols] = (y * c + pltpu.roll(y, HEAD_DIM // 2, 1) * s).astype(dst.dtype)

        for h in range(N_Q):
            head(q_ref, qg_ref[...], qo_ref, h)
        for h in range(N_KV):
            head(k_ref, kg_ref[...], ko_ref, h)

    return _rows(body, [("r", q_raw), ("r", kv_raw, KV_COLS, 0), ("f", qg), ("f", kg), ("r", cos2), ("r", sin2)],
                 [("r", Q_COLS, bf16), ("r", KV_COLS, bf16)], tm=min(256, q_raw.shape[0]), name=name)


def _qk_bwd(q_raw, kv_raw, dqr, dkr, qg, kg, cos2, sin2, name):
    def body(q_ref, k_ref, dq_ref, dk_ref, qg_ref, kg_ref, c_ref, s_ref, dqo_ref, dko_ref, dqg_ref, dkg_ref):
        c, s = c_ref[...], s_ref[...]

        def head(src, dsrc, gain, dst, h):
            cols = slice(h * HEAD_DIM, (h + 1) * HEAD_DIM)
            r, xn = _rms_stats(src[:, cols].astype(f32))
            do = dsrc[:, cols].astype(f32)
            dy = do * c + pltpu.roll(do * s, HEAD_DIM // 2, 1)
            dxn = dy * gain
            dst[:, cols] = (r * (dxn - xn * jnp.mean(dxn * xn, axis=-1, keepdims=True))).astype(dst.dtype)
            return jnp.sum(dy * xn, axis=0, keepdims=True)

        dq_gain = head(q_ref, dq_ref, qg_ref[...], dqo_ref, 0)
        for h in range(1, N_Q):
            dq_gain = dq_gain + head(q_ref, dq_ref, qg_ref[...], dqo_ref, h)
        dk_gain = head(k_ref, dk_ref, kg_ref[...], dko_ref, 0)
        for h in range(1, N_KV):
            dk_gain = dk_gain + head(k_ref, dk_ref, kg_ref[...], dko_ref, h)
        _accumulate(dqg_ref, dq_gain)
        _accumulate(dkg_ref, dk_gain)

    return _rows(body, [("r", q_raw), ("r", kv_raw, KV_COLS, 0), ("r", dqr), ("r", dkr), ("f", qg), ("f", kg),
                        ("r", cos2), ("r", sin2)],
                 [("r", Q_COLS, bf16), ("r", KV_COLS, bf16), ("a", (1, HEAD_DIM), f32), ("a", (1, HEAD_DIM), f32)],
                 tm=min(256, q_raw.shape[0]), name=name)


def _softmax_rows(q, k):
    s = lax.dot_general(q, k, (((1,), (1,)), ((), ())), preferred_element_type=f32) * (SCALE * math.log2(math.e))
    p = jnp.exp2(s - jnp.max(s, axis=-1, keepdims=True))
    return p * (1.0 / jnp.sum(p, axis=-1, keepdims=True))


def _head_cols(g):
    return slice(g * HEAD_DIM, (g + 1) * HEAD_DIM)


def _attn_fwd(qr, kr, kv_raw, name):
    S = qr.shape[0]
    tq = min(256, S)

    def body(q_ref, k_ref, v_ref, o_ref):
        k, v = k_ref[...], v_ref[...]
        for g in range(GRP):
            p = _softmax_rows(q_ref[:, _head_cols(g)], k)
            o_ref[:, _head_cols(g)] = jnp.dot(p.astype(bf16), v, preferred_element_type=f32).astype(o_ref.dtype)

    return _call(
        body, name=name, grid=(N_KV, S // tq),
        in_specs=[pl.BlockSpec((tq, GRP * HEAD_DIM), lambda kv, i: (i, kv)),
                  pl.BlockSpec((S, HEAD_DIM), lambda kv, i: (0, kv)),
                  pl.BlockSpec((S, HEAD_DIM), lambda kv, i: (0, N_KV + kv))],
        out_specs=pl.BlockSpec((tq, GRP * HEAD_DIM), lambda kv, i: (i, kv)),
        out_shape=jax.ShapeDtypeStruct((S, Q_COLS), bf16),
        compiler_params=_params(("parallel", "arbitrary"), 4 * GRP * tq * S * 4 + (8 << 20)),
    )(qr, kr, kv_raw)


def _attn_bwd(qr, kr, kv_raw, do, name):
    S = qr.shape[0]
    tq = min(256, S)

    def body(q_ref, k_ref, v_ref, do_ref, dq_ref, dk_ref, dv_ref):
        first = pl.program_id(1) == 0
        k, v = k_ref[...], v_ref[...]
        dv_part = dk_part = None
        for g in range(GRP):
            q, do_ = q_ref[:, _head_cols(g)], do_ref[:, _head_cols(g)]
            p = _softmax_rows(q, k)
            dp = lax.dot_general(do_, v, (((1,), (1,)), ((), ())), preferred_element_type=f32)
            ds = (p * (dp - jnp.sum(dp * p, axis=-1, keepdims=True)) * SCALE).astype(bf16)
            dq_ref[:, _head_cols(g)] = jnp.dot(ds, k, preferred_element_type=f32).astype(dq_ref.dtype)
            dv_g = lax.dot_general(p.astype(bf16), do_, (((0,), (0,)), ((), ())), preferred_element_type=f32)
            dk_g = lax.dot_general(ds, q, (((0,), (0,)), ((), ())), preferred_element_type=f32)
            dv_part = dv_g if g == 0 else dv_part + dv_g
            dk_part = dk_g if g == 0 else dk_part + dk_g

        @pl.when(first)
        def _():
            dv_ref[...] = dv_part
            dk_ref[...] = dk_part

        @pl.when(jnp.logical_not(first))
        def _():
            dv_ref[...] += dv_part
            dk_ref[...] += dk_part

    qspec = pl.BlockSpec((tq, GRP * HEAD_DIM), lambda kv, i: (i, kv))
    return _call(
        body, name=name, grid=(N_KV, S // tq),
        in_specs=[qspec, pl.BlockSpec((S, HEAD_DIM), lambda kv, i: (0, kv)),
                  pl.BlockSpec((S, HEAD_DIM), lambda kv, i: (0, N_KV + kv)), qspec],
        out_specs=[qspec, pl.BlockSpec((S, HEAD_DIM), lambda kv, i: (0, kv)),
                   pl.BlockSpec((S, HEAD_DIM), lambda kv, i: (0, kv))],
        out_shape=[jax.ShapeDtypeStruct((S, Q_COLS), bf16), jax.ShapeDtypeStruct((S, KV_COLS), f32),
                   jax.ShapeDtypeStruct((S, KV_COLS), f32)],
        compiler_params=_params(("parallel", "arbitrary"), 6 * GRP * tq * S * 4 + (8 << 20)),
    )(qr, kr, kv_raw, do)


CONV_HALO = 16


def _fill_padded(pad_ref, val, S):
    pad_ref[pl.ds(0, CONV_HALO), :] = jnp.zeros((CONV_HALO, LANES), f32)
    pad_ref[pl.ds(CONV_HALO + S, CONV_HALO), :] = jnp.zeros((CONV_HALO, LANES), f32)
    pad_ref[pl.ds(CONV_HALO, S), :] = val


def _group_specs(S, n_groups, second_half):
    return pl.BlockSpec((S, LANES), functools.partial(lambda g, o: (0, g + o), o=n_groups if second_half else 0))


def _conv1_fwd(conv_in, wdw, b_dw, name):
    S = conv_in.shape[0]
    ng = CONV_CH // LANES
    R = min(256, S)

    def body(a_ref, g_ref, w_ref, b_ref, o_ref, pad_ref):
        z = a_ref[...].astype(f32) * jax.nn.sigmoid(g_ref[...].astype(f32))
        _fill_padded(pad_ref, z, S)
        for r in range(S // R):
            acc = jnp.zeros((R, LANES), f32) + b_ref[...]
            for j in range(CONV_W):
                acc = acc + w_ref[pl.ds(j, 1), :] * pad_ref[pl.ds(r * R + CONV_HALO - CONV_PAD + j, R), :]
            o_ref[pl.ds(r * R, R), :] = acc

    return _call(
        body, name=name, grid=(ng,),
        in_specs=[_group_specs(S, ng, False), _group_specs(S, ng, True),
                  pl.BlockSpec((CONV_WP, LANES), lambda g: (g, 0)), pl.BlockSpec((1, LANES), lambda g: (0, g))],
        out_specs=pl.BlockSpec((S, LANES), lambda g: (0, g)),
        out_shape=jax.ShapeDtypeStruct((S, CONV_CH), f32),
        scratch_shapes=[pltpu.VMEM((S + 2 * CONV_HALO, LANES), f32)],
        compiler_params=_params(("parallel",), 24 << 20),
    )(conv_in, conv_in, wdw, b_dw)


def _conv1_bwd(conv_in, dc, wdw, name):
    S = conv_in.shape[0]
    ng = CONV_CH // LANES
    R = min(256, S)

    def body(a_ref, g_ref, w_ref, dc_ref, da_ref, dg_ref, dw_ref, db_ref, padz_ref, padd_ref):
        a = a_ref[...].astype(f32)
        sg = jax.nn.sigmoid(g_ref[...].astype(f32))
        _fill_padded(padz_ref, a * sg, S)
        _fill_padded(padd_ref, dc_ref[...], S)
        for r in range(S // R):
            dz = jnp.zeros((R, LANES), f32)
            for j in range(CONV_W):
                dz = dz + w_ref[pl.ds(j, 1), :] * padd_ref[pl.ds(r * R + CONV_HALO + CONV_PAD - j, R), :]
            rows = pl.ds(r * R, R)
            ar, sr = a_ref[rows, :].astype(f32), jax.nn.sigmoid(g_ref[rows, :].astype(f32))
            da_ref[rows, :] = (dz * sr).astype(da_ref.dtype)
            dg_ref[rows, :] = (dz * ar * sr * (1.0 - sr)).astype(dg_ref.dtype)
        for j in range(CONV_W):
            tot = jnp.zeros((1, LANES), f32)
            for r in range(S // R):
                tot = tot + jnp.sum(dc_ref[pl.ds(r * R, R), :] * padz_ref[pl.ds(r * R + CONV_HALO - CONV_PAD + j, R), :],
                                    axis=0, keepdims=True)
            dw_ref[pl.ds(j, 1), :] = tot
        dw_ref[pl.ds(CONV_W, CONV_WP - CONV_W), :] = jnp.zeros((CONV_WP - CONV_W, LANES), f32)
        db_ref[...] = jnp.sum(dc_ref[...], axis=0, keepdims=True)

    return _call(
        body, name=name, grid=(ng,),
        in_specs=[_group_specs(S, ng, False), _group_specs(S, ng, True),
                  pl.BlockSpec((CONV_WP, LANES), lambda g: (g, 0)), pl.BlockSpec((S, LANES), lambda g: (0, g))],
        out_specs=[pl.BlockSpec((S, LANES), lambda g: (0, g)), pl.BlockSpec((S, LANES), lambda g: (0, g)),
                   pl.BlockSpec((CONV_WP, LANES), lambda g: (g, 0)), pl.BlockSpec((1, LANES), lambda g: (0, g))],
        out_shape=[jax.ShapeDtypeStruct((S, CONV_CH), bf16), jax.ShapeDtypeStruct((S, CONV_CH), bf16),
                   jax.ShapeDtypeStruct((ng * CONV_WP, LANES), f32), jax.ShapeDtypeStruct((1, CONV_CH), f32)],
        scratch_shapes=[pltpu.VMEM((S + 2 * CONV_HALO, LANES), f32), pltpu.VMEM((S + 2 * CONV_HALO, LANES), f32)],
        compiler_params=_params(("parallel",), 24 << 20),
    )(conv_in, conv_in, wdw, dc)


def _ln_stats(x, eps=1e-5):
    xc = x - jnp.mean(x, axis=-1, keepdims=True)
    r = lax.rsqrt(jnp.mean(xc * xc, axis=-1, keepdims=True) + eps)
    return r, xc * r


def _ln_bwd(r, xh, dxh):
    return r * (dxh - jnp.mean(dxh, axis=-1, keepdims=True) - xh * jnp.mean(dxh * xh, axis=-1, keepdims=True))


def _conv2_fwd(c, ln_g, ln_b, name):
    def body(c_ref, g_ref, b_ref, o_ref):
        _, xh = _ln_stats(c_ref[...])
        y = xh * g_ref[...] + b_ref[...]
        o_ref[...] = (y * jax.nn.sigmoid(y)).astype(o_ref.dtype)

    return _rows(body, [("r", c), ("f", ln_g), ("f", ln_b)], [("r", CONV_CH, bf16)], tm=min(256, c.shape[0]), name=name)[0]


def _conv2_bwd(c, dcz, ln_g, ln_b, name):
    def body(c_ref, d_ref, g_ref, b_ref, dc_ref, dg_ref, db_ref):
        r, xh = _ln_stats(c_ref[...])
        y = xh * g_ref[...] + b_ref[...]
        sg = jax.nn.sigmoid(y)
        dy = d_ref[...].astype(f32) * (sg * (1.0 + y * (1.0 - sg)))
        dc_ref[...] = _ln_bwd(r, xh, dy * g_ref[...])
        _accumulate(dg_ref, jnp.sum(dy * xh, axis=0, keepdims=True))
        _accumulate(db_ref, jnp.sum(dy, axis=0, keepdims=True))

    return _rows(body, [("r", c), ("r", dcz), ("f", ln_g), ("f", ln_b)],
                 [("r", CONV_CH, f32), ("a", (1, CONV_CH), f32), ("a", (1, CONV_CH), f32)],
                 tm=min(256, c.shape[0]), name=name)


GELU_K = math.sqrt(2.0 / math.pi)
GELU_C = 0.044715


def _gelu(x):
    return 0.5 * x * (1.0 + jnp.tanh(GELU_K * (x + GELU_C * x * x * x)))


def _gelu_and_grad(x):
    x2 = x * x
    th = jnp.tanh(GELU_K * (x + GELU_C * x2 * x))
    half = 0.5 * (1.0 + th)
    return x * half, half + 0.5 * x * (1.0 - th * th) * (GELU_K * (1.0 + 3.0 * GELU_C * x2))


def _chunk_rows(n):
    return pl.ds(pl.multiple_of(n * SG_CHUNK, SG_CHUNK), SG_CHUNK)


def _sgu_fwd(sg_in, ln_g, ln_b, w_s, b_s, name):
    S = sg_in.shape[0]

    def body(u_ref, v_ref, lg_ref, lb_ref, w_ref, b_ref, o_ref):
        wb = w_ref[...].astype(bf16)

        def chunk(n, carry):
            rows = _chunk_rows(n)
            gu = _gelu(u_ref[rows, :].astype(f32))
            _, xh = _ln_stats(_gelu(v_ref[rows, :].astype(f32)))
            vl = xh * lg_ref[...] + lb_ref[...]
            t = jnp.dot(wb, vl.astype(bf16), preferred_element_type=f32) + b_ref[...]
            o_ref[rows, :] = (gu * t).astype(o_ref.dtype)
            return carry

        lax.fori_loop(0, S // SG_CHUNK, chunk, 0, unroll=2)

    return _call(
        body, name=name, grid=(SG_G,),
        in_specs=[_group_specs(S, SG_G, False), _group_specs(S, SG_G, True),
                  pl.BlockSpec((1, LANES), lambda g: (0, g)), pl.BlockSpec((1, LANES), lambda g: (0, g)),
                  pl.BlockSpec((None, SG_CHUNK, SG_CHUNK), lambda g: (g, 0, 0)),
                  pl.BlockSpec((None, SG_CHUNK, 1), lambda g: (g, 0, 0))],
        out_specs=pl.BlockSpec((S, LANES), lambda g: (0, g)),
        out_shape=jax.ShapeDtypeStruct((S, SG_CH), bf16),
        compiler_params=_params(("parallel",), 24 << 20),
    )(sg_in, sg_in, ln_g, ln_b, w_s, b_s)


def _sgu_bwd(sg_in, dsz, ln_g, ln_b, w_s, w_s_t, b_s, name):
    S = sg_in.shape[0]

    def body(u_ref, v_ref, lg_ref, lb_ref, w_ref, wt_ref, b_ref, d_ref, du_ref, dv_ref, dw_ref, db_ref, dlg_ref, dlb_ref):
        wb = w_ref[...].astype(bf16)
        wtb = wt_ref[...].astype(bf16)

        def chunk(n, carry):
            dwa, dba, dlga, dlba = carry
            rows = _chunk_rows(n)
            u = u_ref[rows, :].astype(f32)
            v = v_ref[rows, :].astype(f32)
            gu, gu_grad = _gelu_and_grad(u)
            gv, gv_grad = _gelu_and_grad(v)
            r, xh = _ln_stats(gv)
            vlb = (xh * lg_ref[...] + lb_ref[...]).astype(bf16)
            t = jnp.dot(wb, vlb, preferred_element_type=f32) + b_ref[...]
            d = d_ref[rows, :].astype(f32)
            dt = d * gu
            dtb = dt.astype(bf16)
            dwa = dwa + lax.dot_general(dtb, vlb, (((1,), (1,)), ((), ())), preferred_element_type=f32)
            dba = dba + jnp.sum(dt, axis=1, keepdims=True)
            dvl = jnp.dot(wtb, dtb, preferred_element_type=f32)
            dlga = dlga + jnp.sum(dvl * xh, axis=0, keepdims=True)
            dlba = dlba + jnp.sum(dvl, axis=0, keepdims=True)
            dgv = _ln_bwd(r, xh, dvl * lg_ref[...])
            du_ref[rows, :] = (d * t * gu_grad).astype(du_ref.dtype)
            dv_ref[rows, :] = (dgv * gv_grad).astype(dv_ref.dtype)
            return dwa, dba, dlga, dlba

        init = (jnp.zeros((SG_CHUNK, SG_CHUNK), f32), jnp.zeros((SG_CHUNK, 1), f32),
                jnp.zeros((1, LANES), f32), jnp.zeros((1, LANES), f32))
        dwa, dba, dlga, dlba = lax.fori_loop(0, S // SG_CHUNK, chunk, init, unroll=2)
        dw_ref[...] = dwa
        db_ref[...] = dba
        dlg_ref[...] = dlga
        dlb_ref[...] = dlba

    wspec = pl.BlockSpec((None, SG_CHUNK, SG_CHUNK), lambda g: (g, 0, 0))
    bspec = pl.BlockSpec((None, SG_CHUNK, 1), lambda g: (g, 0, 0))
    lspec = pl.BlockSpec((1, LANES), lambda g: (0, g))
    cspec = pl.BlockSpec((S, LANES), lambda g: (0, g))
    return _call(
        body, name=name, grid=(SG_G,),
        in_specs=[_group_specs(S, SG_G, False), _group_specs(S, SG_G, True), lspec, lspec, wspec, wspec, bspec, cspec],
        out_specs=[cspec, cspec, wspec, bspec, lspec, lspec],
        out_shape=[jax.ShapeDtypeStruct((S, SG_CH), bf16), jax.ShapeDtypeStruct((S, SG_CH), bf16),
                   jax.ShapeDtypeStruct((SG_G, SG_CHUNK, SG_CHUNK), f32), jax.ShapeDtypeStruct((SG_G, SG_CHUNK, 1), f32),
                   jax.ShapeDtypeStruct((1, SG_CH), f32), jax.ShapeDtypeStruct((1, SG_CH), f32)],
        compiler_params=_params(("parallel",), 24 << 20),
    )(sg_in, sg_in, ln_g, ln_b, w_s, w_s_t, b_s, dsz)


def _row_tile(r, c, n_arrays, itemsize=4):
    fits = [tm for tm in range(16, r + 1, 16) if r % tm == 0 and 2 * n_arrays * tm * c * itemsize <= (24 << 20)]
    return fits[-1] if fits else r


def _sum_slots(slots, name):
    n, r, c = slots.shape
    tm = _row_tile(r, c, n + 2)

    def body(s_ref, o_ref):
        acc = s_ref[0].astype(f32)
        for k in range(1, n):
            acc = acc + s_ref[k].astype(f32)
        o_ref[...] = acc

    return _call(body, name=name, grid=(r // tm,),
                 in_specs=[pl.BlockSpec((n, tm, c), lambda i: (0, i, 0))],
                 out_specs=pl.BlockSpec((tm, c), lambda i: (i, 0)),
                 out_shape=jax.ShapeDtypeStruct((r, c), f32),
                 compiler_params=_params(("parallel",), 40 << 20))(slots)


def _add_sibling(g4, recv, core, name):
    _, _, r, c = g4.shape
    tm = _row_tile(r, c, 3, 2)

    def body(core_ref, g_ref, r_ref, o_ref):
        o_ref[...] = (g_ref[...].astype(f32) + r_ref[...].astype(f32)).astype(o_ref.dtype)

    grid_spec = pltpu.PrefetchScalarGridSpec(
        num_scalar_prefetch=1, grid=(N_CHIP, r // tm),
        in_specs=[pl.BlockSpec((None, None, tm, c), lambda k, i, core_ref: (k, core_ref[0], i, 0)),
                  pl.BlockSpec((None, tm, c), lambda k, i, core_ref: (k, i, 0))],
        out_specs=pl.BlockSpec((None, tm, c), lambda k, i, core_ref: (k, i, 0)))
    return _call(body, name=name, grid_spec=grid_spec, out_shape=jax.ShapeDtypeStruct((N_CHIP, r, c), bf16),
                 compiler_params=_params(("parallel", "parallel"), 40 << 20))(core, g4, recv)


def _adamw(w, g, m, v, name):
    L, r, c = w.shape
    tm = _row_tile(r, c, 7)
    c1 = 1.0 - ADAM_B1 ** ADAM_STEP
    c2 = 1.0 - ADAM_B2 ** ADAM_STEP

    def body(w_ref, g_ref, m_ref, v_ref, d_ref, mo_ref, vo_ref):
        g_ = g_ref[...]
        m_ = ADAM_B1 * m_ref[...] + (1.0 - ADAM_B1) * g_
        v_ = ADAM_B2 * v_ref[...] + (1.0 - ADAM_B2) * (g_ * g_)
        d_ref[...] = -ADAM_LR * ((m_ / c1) / (jnp.sqrt(v_ / c2) + ADAM_EPS) + ADAM_WD * w_ref[...])
        mo_ref[...] = m_
        vo_ref[...] = v_

    spec = pl.BlockSpec((None, tm, c), lambda l, i: (l, i, 0))
    shp = jax.ShapeDtypeStruct((L, r, c), f32)
    return _call(body, name=name, grid=(L, r // tm), in_specs=[spec] * 4, out_specs=[spec] * 3,
                 out_shape=[shp] * 3, compiler_params=_params(("parallel", "parallel"), 40 << 20))(w, g, m, v)


def _mesh_pos():
    return lax.axis_index("x"), lax.axis_index("y"), lax.axis_index("c")


SEM = pl.BlockSpec(memory_space=pltpu.SEMAPHORE)
ANY = pl.BlockSpec(memory_space=pl.ANY)
EFFECT = pltpu.SideEffectType.DATAFLOW_SIDE_EFFECTING


def _other_chips(x, y):
    return [(1 - x, y), (x, 1 - y), (1 - x, 1 - y)]


def _peers(kind, x, y):
    return [(x, y)] if kind == "sibling" else _other_chips(x, y)


def _ici_copy(kind, src_ref, land_ref, send_sem, recv_sem, sender, target, c):
    (sx, sy), (tx, ty) = sender, target
    if kind == "sibling":
        return pltpu.make_async_remote_copy(src_ref=src_ref.at[:, 1 - c], dst_ref=land_ref, send_sem=send_sem,
                                            recv_sem=recv_sem, device_id=(tx, ty, 1 - c), device_id_type=MESH)
    if kind == "gather":
        src, dst = src_ref, land_ref.at[4 * sx + 2 * sy + c]
    else:
        src, dst = src_ref.at[2 * tx + ty], land_ref.at[2 * sx + sy]
    return pltpu.make_async_remote_copy(src_ref=src, dst_ref=dst, send_sem=send_sem, recv_sem=recv_sem,
                                        device_id=(tx, ty, c), device_id_type=MESH)


def _ici_start(kind, srcs, lands, after, name):
    n = len(srcs)
    npeer = 1 if kind == "sibling" else 3

    def body(*refs):
        src_refs, land_refs = refs[:n], refs[n:2 * n]
        send_sems, recv_sems = refs[2 * n + 1], refs[2 * n + 2]
        token = refs[-1]
        x, y, c = _mesh_pos()
        for j, chip in enumerate(_peers(kind, x, y)):
            for k in range(n):
                _ici_copy(kind, src_refs[k], land_refs[k], send_sems.at[npeer * k + j], recv_sems.at[npeer * k + j],(x, y), chip, c).start()
        token[...] = jnp.zeros_like(token)

    bufs = list(srcs) + list(lands)
    return _call(
        body, name=name,
        out_shape=(pltpu.SemaphoreType.DMA((npeer * n,)), pltpu.SemaphoreType.DMA((npeer * n,)),
                   *[pltpu.HBM(b.shape, b.dtype) for b in bufs], jax.ShapeDtypeStruct((8, LANES), f32)),
        in_specs=[HBM] * (2 * n) + [ANY], out_specs=(SEM, SEM, *[HBM] * (2 * n), pl.BlockSpec(memory_space=pltpu.VMEM)),
        input_output_aliases={i: 2 + i for i in range(2 * n)},
        compiler_params=pltpu.CompilerParams(has_side_effects=EFFECT),
    )(*[pltpu.with_memory_space_constraint(b, pltpu.HBM) for b in bufs], after)


def _ici_wait(kind, started, after, name):
    send_sems, recv_sems, *bufs = started[:-1]
    n = len(bufs) // 2
    npeer = 1 if kind == "sibling" else 3

    def body(*refs):
        src_refs, land_refs = refs[:n], refs[n:2 * n]
        send_sems, recv_sems = refs[2 * n], refs[2 * n + 1]
        x, y, c = _mesh_pos()
        for j, chip in enumerate(_peers(kind, x, y)):
            for k in range(n):
                _ici_copy(kind, src_refs[k], land_refs[k], send_sems.at[npeer * k + j], recv_sems.at[npeer * k + j],(x, y), chip, c).wait_send()
                _ici_copy(kind, src_refs[k], land_refs[k], send_sems.at[npeer * k + j], recv_sems.at[npeer * k + j],chip, (x, y), c).wait_recv()

    out = _call(
        body, name=name, out_shape=[pltpu.HBM(b.shape, b.dtype) for b in bufs],
        in_specs=[HBM] * (2 * n) + [SEM, SEM, ANY], out_specs=[HBM] * (2 * n),
        input_output_aliases={i: i for i in range(2 * n)},
        compiler_params=pltpu.CompilerParams(has_side_effects=EFFECT),
    )(*bufs, send_sems, recv_sems, after)
    return out[:n], out[n:]


def _d2d_gather(lands, after, name):
    n = len(lands)

    def body(*refs):
        in_refs, o_refs = refs[:n], refs[n + 1:2 * n + 1]
        send_sems, recv_sems = refs[2 * n + 1:]
        x, y, c = _mesh_pos()
        copies = [pltpu.make_async_remote_copy(
            src_ref=in_refs[k].at[:, c], dst_ref=o_refs[k].at[:, c], send_sem=send_sems.at[k], recv_sem=recv_sems.at[k],
            device_id=(x, y, 1 - c), device_id_type=MESH) for k in range(n)]
        for cp in copies:
            cp.start()
        for k, cp in enumerate(copies):
            cp.wait_send()
            pltpu.make_async_remote_copy(
                src_ref=in_refs[k].at[:, c], dst_ref=o_refs[k].at[:, 1 - c], send_sem=send_sems.at[k],
                recv_sem=recv_sems.at[k], device_id=(x, y, 1 - c), device_id_type=MESH).wait_recv()

    return _call(
        body, name=name, in_specs=[HBM] * n + [ANY], out_specs=[HBM] * n,
        out_shape=[jax.ShapeDtypeStruct(b.shape, b.dtype) for b in lands],
        input_output_aliases={k: k for k in range(n)},
        scratch_shapes=[pltpu.SemaphoreType.DMA((n,)), pltpu.SemaphoreType.DMA((n,))],
    )(*lands, after)


def _sum_chip_slots(lands, sums, chip, name):
    _, r, c = lands.shape
    tm = _row_tile(r, c, 10, 2)

    def body(chip_ref, l_ref, s_ref, o_ref):
        acc = None
        for k in range(N_CHIP):
            part = jnp.where(chip_ref[0] == k, s_ref[k], l_ref[k]).astype(f32)
            acc = part if acc is None else acc + part
        o_ref[...] = acc

    grid_spec = pltpu.PrefetchScalarGridSpec(
        num_scalar_prefetch=1, grid=(r // tm,),
        in_specs=[pl.BlockSpec((N_CHIP, tm, c), lambda i, chip_ref: (0, i, 0)),
                  pl.BlockSpec((N_CHIP, tm, c), lambda i, chip_ref: (0, i, 0))],
        out_specs=pl.BlockSpec((tm, c), lambda i, chip_ref: (i, 0)))
    return _call(body, name=name, grid_spec=grid_spec, out_shape=jax.ShapeDtypeStruct((r, c), f32),
                 compiler_params=_params(("parallel",), 40 << 20))(chip, lands, sums)


def _reduce_begin(grads, core, tag):
    g4s = [g.reshape(N_CHIP, 2, g.shape[0] // N_DEV, g.shape[1]) for g in grads]
    recvs = [lax.empty((N_CHIP,) + g.shape[2:], g.dtype) for g in g4s]
    return _ici_start("sibling", g4s, recvs, core, name="rs_d2d_start_" + tag)


def _reduce_continue(begun, core, after, tag):
    g4s, recvs = _ici_wait("sibling", begun, after, name="rs_d2d_wait_" + tag)
    sums = [_add_sibling(g4, rv, core, name="rs_add_" + tag) for g4, rv in zip(g4s, recvs)]
    lands = [lax.empty(s.shape, s.dtype) for s in sums]
    return _ici_start("reduce", sums, lands, core, name="rs_start_" + tag)


def _adamw_tile(chip_ref, w_ref, m_ref, v_ref, l_ref, s_ref, g_ref, d_ref, mo_ref, vo_ref):
    c1 = 1.0 - ADAM_B1 ** ADAM_STEP
    c2 = 1.0 - ADAM_B2 ** ADAM_STEP
    g_ = None
    for k in range(N_CHIP):
        part = jnp.where(chip_ref[0] == k, s_ref[k], l_ref[k]).astype(f32)
        g_ = part if g_ is None else g_ + part
    m_ = ADAM_B1 * m_ref[...] + (1.0 - ADAM_B1) * g_
    v_ = ADAM_B2 * v_ref[...] + (1.0 - ADAM_B2) * (g_ * g_)
    g_ref[...] = g_
    d_ref[...] = -ADAM_LR * ((m_ / c1) / (jnp.sqrt(v_ / c2) + ADAM_EPS) + ADAM_WD * w_ref[...])
    mo_ref[...] = m_
    vo_ref[...] = v_


def _adamw_rider(layer, w, m, v, lands, sums, chip, steps):
    L, r, c = w.shape
    tm = r // steps
    assert tm * steps == r and tm % 16 == 0
    wspec = pl.BlockSpec((None, tm, c), lambda i, j, k: (layer, i, 0))
    sspec = pl.BlockSpec((N_CHIP, tm, c), lambda i, j, k: (0, i, 0))
    return dict(args=[chip, w, m, v, lands, sums],
                in_specs=[pl.BlockSpec(memory_space=pltpu.SMEM)] + [wspec] * 3 + [sspec] * 2,
                out_specs=[wspec] * 4, out_shape=[jax.ShapeDtypeStruct((L, r, c), f32)] * 4,
                body=_adamw_tile, vmem=2 * 11 * tm * c * 4)


def _adamw_reduced(layer, w, m, v, lands, sums, chip, prev, name):
    L, r, c = w.shape
    tm = _row_tile(r, c, 11)
    n_prev = 0 if prev is None else 4

    def body(chip_ref, w_ref, m_ref, v_ref, l_ref, s_ref, *refs):
        _adamw_tile(chip_ref, w_ref, m_ref, v_ref, l_ref, s_ref, *refs[n_prev:])

    wspec = pl.BlockSpec((None, tm, c), lambda i, chip_ref: (layer, i, 0))
    sspec = pl.BlockSpec((N_CHIP, tm, c), lambda i, chip_ref: (0, i, 0))
    grid_spec = pltpu.PrefetchScalarGridSpec(
        num_scalar_prefetch=1, grid=(r // tm,), in_specs=[wspec] * 3 + [sspec] * 2 + [ANY] * n_prev, out_specs=[wspec] * 4)
    return _call(body, name=name, grid_spec=grid_spec, out_shape=[jax.ShapeDtypeStruct((L, r, c), f32)] * 4,
                 input_output_aliases={6 + i: i for i in range(n_prev)},
                 compiler_params=_params(("parallel",), 40 << 20))(chip, w, m, v, lands, sums, *(prev or ()))


def _rope_tables(S):
    rows = S // GRID_W
    row = jnp.repeat(jnp.arange(rows, dtype=f32), GRID_W)
    col = jnp.tile(jnp.arange(GRID_W, dtype=f32), rows)
    nf = HEAD_DIM // 4
    inv = ROPE_THETA ** (-jnp.arange(nf, dtype=f32) / nf)
    ang = jnp.concatenate([row[:, None] * inv, col[:, None] * inv], axis=-1)
    cos, sin = jnp.cos(ang), jnp.sin(ang)
    return jnp.concatenate([cos, cos], axis=-1), jnp.concatenate([-sin, sin], axis=-1)


def _layer_fwd(xin, p, w, more_weights, cos2, sin2):
    sv = {"xin": xin}
    h = sv["h"] = _rms_fwd(xin, p["g_mix"], name="rms_mix")
    proj = functools.partial(_mm, h, w["in"], "nt", bf16)
    q_raw = sv["q_raw"] = proj(n=Q_COLS, b_off=0, name="proj_q")
    kv_raw = sv["kv_raw"] = proj(n=2 * KV_COLS, b_off=OFF_KV, name="proj_kv")
    conv_in = sv["conv_in"] = proj(n=2 * CONV_CH, b_off=OFF_CONV, name="proj_conv")
    sg_in = sv["sg_in"] = proj(n=2 * SG_CH, b_off=OFF_SG, name="proj_sg")
    gl = sv["gl"] = proj(n=3 * D_MODEL, b_off=OFF_GATE, name="proj_gate")
    qr, kr = sv["qr"], sv["kr"] = _qk_fwd(q_raw, kv_raw, p["q_norm_g"], p["k_norm_g"], cos2, sin2, name="qk_fwd")
    o = sv["o"] = _attn_fwd(qr, kr, kv_raw, name="attn_fwd")
    c = sv["c"] = _conv1_fwd(conv_in, w["dw"], p["b_dw"], name="conv1_fwd")
    cz = sv["cz"] = _conv2_fwd(c, p["conv_ln_g"], p["conv_ln_b"], name="conv2_fwd")
    sz = sv["sz"] = _sgu_fwd(sg_in, p["sg_ln_g"], p["sg_ln_b"], p["w_s"], p["b_s"], name="sgu_fwd")
    w = {**w, **more_weights(1, sz)}
    sv["ya"], sv["yc"], sv["ys"], merged = _mixer_out([o, cz, sz], [w["attn_o"], w["conv_o"], w["sg_o"]], gl, p["b_gate"],
                                                      name="mixer_out")
    sv["merged"] = merged
    x1 = sv["x1"] = _mm(merged, w["out"], "nn", f32, res=xin, name="out_proj")
    w = {**w, **more_weights(2, x1)}
    hf = sv["hf"] = _rms_fwd(x1, p["g_ffn"], name="rms_ffn")
    sv["fg"], sv["fu"], act = _ffn_up(hf, w["ff_gate"], w["ff_up"], name="ffn_up")
    sv["act"] = act
    x2 = _mm(act, w["ff_down"], "nn", f32, res=x1, name="ff_down")
    return x2, sv, w


def _layer_bwd(dx2, dx2b, sv, p, w, cos2, sin2, reduce_begin, reduce_continue, last, rider_for):
    small = {}

    def grad_mm(weight, a, b, name):
        fresh = b if a is sv["act"] else a
        ride = rider_for(weight, a.shape[1] // _pick(a.shape[1], (512, 256, 128)), fresh)
        if ride is None:
            return _mm(a, b, "tn", bf16, name=name)
        g, results = _mm(a, b, "tn", bf16, rider=ride[0], name=name + "_ridden")
        ride[1](results)
        return g

    dfg, dfu = _ffn_down_bwd(dx2b, w["ff_down"], sv["fg"], sv["fu"], name="ffn_down_bwd")
    g_down = grad_mm("w_ff_down", sv["act"], dx2b, name="g_ff_down")
    dhf = _mm(dfg, w["ff_gate"], "nn", f32, name="d_hf_gate")
    dhf = _mm(dfu, w["ff_up"], "nn", f32, res=dhf, name="d_hf_up")
    g_gate = grad_mm("w_ff_gate", dfg, sv["hf"], name="g_ff_gate")
    g_up = grad_mm("w_ff_up", dfu, sv["hf"], name="g_ff_up")
    zero = reduce_begin("ffn", dict(w_ff_gate=g_gate, w_ff_up=g_up, w_ff_down=g_down))[0, 0]
    dx1, dx1b, small["g_ffn"] = _rms_bwd(sv["x1"], p["g_ffn"] + zero, dhf, dx2, name="rms_ffn_bwd")
    g_out = _mm(sv["merged"], dx1b, "tn", bf16, name="g_out")
    *dgl, dya, dyc, dys, db0, db1, db2 = _merge_bwd_fused(dx1b, w["out"], sv["gl"], p["b_gate"], sv["ya"], sv["yc"], sv["ys"],
                                                        name="merge_bwd")
    small["b_gate"] = jnp.concatenate([db0, db1, db2], axis=1)
    do = _mm(dya, w["attn_o"], "nn", bf16, after=reduce_continue("ffn", dya), name="d_o")
    g_ao = _mm(dya, sv["o"], "tn", bf16, name="g_attn_o")
    dcz = _mm(dyc, w["conv_o"], "nn", bf16, name="d_cz")
    g_co = _mm(dyc, sv["cz"], "tn", bf16, name="g_conv_o")
    dsz = _mm(dys, w["sg_o"], "nn", bf16, name="d_sz")
    g_so = _mm(dys, sv["sz"], "tn", bf16, name="g_sg_o")
    zero = reduce_begin("mix", dict(w_attn_o=g_ao, w_conv_o=g_co, w_sg_o=g_so, w_out=g_out))[0, 0]
    dsu, dsv, small["w_s"], small["b_s"], small["sg_ln_g"], small["sg_ln_b"] = _sgu_bwd(
        sv["sg_in"], dsz, p["sg_ln_g"] + zero, p["sg_ln_b"], p["w_s"], p["w_s_t"], p["b_s"], name="sgu_bwd")
    dc, small["conv_ln_g"], small["conv_ln_b"] = _conv2_bwd(sv["c"], dcz, p["conv_ln_g"], p["conv_ln_b"], name="conv2_bwd")
    da, dgt, small["w_dw"], small["b_dw"] = _conv1_bwd(sv["conv_in"], dc, w["dw"], name="conv1_bwd")
    zero = reduce_continue("mix", da)[0, 0]
    dqr, dkr, dv = _attn_bwd(sv["qr"], sv["kr"], sv["kv_raw"], do, name="attn_bwd")
    dq_raw, dk_raw, small["q_norm_g"], small["k_norm_g"] = _qk_bwd(
        sv["q_raw"], sv["kv_raw"], dqr, dkr, p["q_norm_g"] + zero, p["k_norm_g"], cos2, sin2, name="qk_bwd")
    dproj = jnp.concatenate([dq_raw, dk_raw, dv.astype(bf16), da, dgt, dsu, dsv, *dgl], axis=1)
    g_in = grad_mm("w_in", dproj, sv["h"], name="g_in")
    begun = reduce_begin("in", dict(w_in=g_in))
    if last:
        begun = reduce_continue("in", begun)
    dh = _mm(dproj, w["in"], "nn", f32, after=begun, name="d_h")
    zero = begun[0, 0] if last else reduce_continue("in", dh)[0, 0]
    dx, dxb, small["g_mix"] = _rms_bwd(sv["xin"], p["g_mix"] + zero, dh, dx1, name="rms_mix_bwd")
    return dx, dxb, small


SMALL = ("g_mix", "b_gate", "q_norm_g", "k_norm_g", "b_dw", "conv_ln_g", "conv_ln_b", "sg_ln_g", "sg_ln_b",
         "w_s", "b_s", "g_ffn")
PACK_ALIGN = 8 * LANES


def _pack(parts):
    flat = jnp.concatenate([a.reshape(-1).astype(f32) for a in parts])
    pad = -flat.shape[0] % PACK_ALIGN
    return jnp.pad(flat, (0, pad)).reshape(-1, LANES)


def _unpack(buf, shapes):
    flat = buf.reshape(-1)
    out, pos = [], 0
    for shp in shapes:
        size = math.prod(shp)
        out.append(flat[pos:pos + size].reshape(shp))
        pos += size
    return out


def kernel(x, g_mix, w_in, b_gate, q_norm_g, k_norm_g, w_attn_o, w_dw, b_dw, conv_ln_g, conv_ln_b, w_conv_o, sg_ln_g, sg_ln_b, w_s, b_s, w_sg_o, w_out, g_ffn, w_ff_gate, w_ff_up, w_ff_down, g_final, loss_target, m_g_mix, m_w_in, m_b_gate, m_q_norm_g, m_k_norm_g, m_w_attn_o, m_w_dw, m_b_dw, m_conv_ln_g, m_conv_ln_b, m_w_conv_o, m_sg_ln_g, m_sg_ln_b, m_w_s, m_b_s, m_w_sg_o, m_w_out, m_g_ffn, m_w_ff_gate, m_w_ff_up, m_w_ff_down, m_g_final, v_g_mix, v_w_in, v_b_gate, v_q_norm_g, v_k_norm_g, v_w_attn_o, v_w_dw, v_b_dw, v_conv_ln_g, v_conv_ln_b, v_w_conv_o, v_sg_ln_g, v_sg_ln_b, v_w_s, v_b_s, v_w_sg_o, v_w_out, v_g_ffn, v_w_ff_gate, v_w_ff_up, v_w_ff_down, v_g_final):
    weights = dict(g_mix=g_mix, w_in=w_in, b_gate=b_gate, q_norm_g=q_norm_g, k_norm_g=k_norm_g, w_attn_o=w_attn_o,
                   w_dw=w_dw, b_dw=b_dw, conv_ln_g=conv_ln_g, conv_ln_b=conv_ln_b, w_conv_o=w_conv_o, sg_ln_g=sg_ln_g,
                   sg_ln_b=sg_ln_b, w_s=w_s, b_s=b_s, w_sg_o=w_sg_o, w_out=w_out, g_ffn=g_ffn, w_ff_gate=w_ff_gate,
                   w_ff_up=w_ff_up, w_ff_down=w_ff_down, g_final=g_final)
    mom_m = dict(g_mix=m_g_mix, w_in=m_w_in, b_gate=m_b_gate, q_norm_g=m_q_norm_g, k_norm_g=m_k_norm_g,
                 w_attn_o=m_w_attn_o, w_dw=m_w_dw, b_dw=m_b_dw, conv_ln_g=m_conv_ln_g, conv_ln_b=m_conv_ln_b,
                 w_conv_o=m_w_conv_o, sg_ln_g=m_sg_ln_g, sg_ln_b=m_sg_ln_b, w_s=m_w_s, b_s=m_b_s, w_sg_o=m_w_sg_o,
                 w_out=m_w_out, g_ffn=m_g_ffn, w_ff_gate=m_w_ff_gate, w_ff_up=m_w_ff_up, w_ff_down=m_w_ff_down,
                 g_final=m_g_final)
    mom_v = dict(g_mix=v_g_mix, w_in=v_w_in, b_gate=v_b_gate, q_norm_g=v_q_norm_g, k_norm_g=v_k_norm_g,
                 w_attn_o=v_w_attn_o, w_dw=v_w_dw, b_dw=v_b_dw, conv_ln_g=v_conv_ln_g, conv_ln_b=v_conv_ln_b,
                 w_conv_o=v_w_conv_o, sg_ln_g=v_sg_ln_g, sg_ln_b=v_sg_ln_b, w_s=v_w_s, b_s=v_b_s, w_sg_o=v_w_sg_o,
                 w_out=v_w_out, g_ffn=v_g_ffn, w_ff_gate=v_w_ff_gate, w_ff_up=v_w_ff_up, w_ff_down=v_w_ff_down,
                 g_final=v_g_final)
    S, D = x.shape[1], x.shape[2]
    xi, yi, ci = _mesh_pos()
    me = 4 * xi + 2 * yi + ci
    core = jnp.reshape(ci, (1,)).astype(jnp.int32)
    cos2, sin2 = _rope_tables(S)

    big = ("w_in", "w_attn_o", "w_conv_o", "w_sg_o", "w_out", "w_ff_gate", "w_ff_up", "w_ff_down")
    transposed = {"w_in", "w_attn_o", "w_conv_o", "w_sg_o", "w_ff_gate", "w_ff_up"}
    chip = jnp.reshape(2 * xi + yi, (1,)).astype(jnp.int32)
    groups = (("in", "dw"), ("attn_o", "conv_o", "sg_o", "out"), ("ff_gate", "ff_up", "ff_down"))
    P, shards = [], []
    for l in range(DEPTH):
        sh = {n[2:]: (weights[n][l].T if n in transposed else weights[n][l]).astype(bf16) for n in big}
        sh["dw"] = jnp.pad(w_dw[l].reshape(CONV_W, LANES), ((0, CONV_WP - CONV_W), (0, 0)))
        shards.append(sh)
        p = {n: weights[n][l].reshape(1, -1) for n in SMALL if n not in ("w_s", "b_s")}
        p["w_s"] = w_s[l]
        p["w_s_t"] = jnp.swapaxes(w_s[l], 1, 2)
        p["b_s"] = b_s[l].reshape(SG_G, SG_CHUNK, 1)
        P.append(p)

    gathers = {}

    def start_gather(l, gi, after):
        srcs = [shards[l][n] for n in groups[gi]]
        lands = [lax.dynamic_update_index_in_dim(lax.empty((N_DEV,) + s.shape, s.dtype), s, me, 0) for s in srcs]
        gathers[l, gi] = _ici_start("gather", srcs, lands, after, name=f"ag_start_{l}{gi}")
        return gathers[l, gi][-1]

    def gathered(l, gi, after):
        srcs, lands = _ici_wait("gather", gathers[l, gi], after, name=f"ag_wait_{l}{gi}")
        after = srcs[0]
        if gi == len(groups) - 1 and l + 1 < DEPTH:
            for gj in range(len(groups)):
                after = start_gather(l + 1, gj, after)
        full = _d2d_gather([b.reshape(N_CHIP, 2, *b.shape[1:]) for b in lands], after, name=f"ag_d2d_{gi}")
        return {n: f.reshape(-1, f.shape[3]) for n, f in zip(groups[gi], full)}

    all_started = cos2
    for gi in range(len(groups)):
        all_started = start_gather(0, gi, all_started)

    h = x.reshape(S, D)
    saved, W = [], []
    for l in range(DEPTH):
        first = gathered(l, 0, all_started if l == 0 else h)
        if l == 0:
            P[l]["g_mix"] = P[l]["g_mix"] + all_started[0, 0]
        h, sv, w = _layer_fwd(h, P[l], first, functools.partial(lambda gi, z, l: gathered(l, gi, z), l=l), cos2, sin2)
        saved.append(sv)
        W.append(w)
    dx, dxb, sq, g_final_part = _final_loss(h, g_final.reshape(1, D), loss_target.reshape(S, D), name="final_loss")
    loss = lax.psum(0.5 * jnp.sum(sq) / D, ("x", "y", "c"))

    begun, reductions, small_grads = {}, {}, [None] * DEPTH
    swap = lambda a: jnp.swapaxes(a, 1, 2)
    arrived, ridden = {}, {}

    def arrive(l, group, after):
        if (l, group) not in arrived:
            arrived[l, group] = _ici_wait("reduce", reductions[l, group][1], after, name=f"rs_wait_{group}{l}")
        return arrived[l, group]

    def rider_for(weight, steps, after, l):
        if l + 1 >= DEPTH:
            return None
        group = next(g for (ll, g), (names, _) in reductions.items() if ll == l + 1 and weight in names)
        sums, lands = arrive(l + 1, group, after)
        i = reductions[l + 1, group][0].index(weight)
        as_arrives = swap if weight in transposed else (lambda a: a)
        ride = _adamw_rider(l + 1, as_arrives(weights[weight]), as_arrives(mom_m[weight]), as_arrives(mom_v[weight]),
                            lands[i], sums[i], chip, steps)
        return ride, functools.partial(ridden.__setitem__, weight)

    for l in reversed(range(DEPTH)):
        def reduce_begin(group, grads, l=l):
            begun[l, group] = (tuple(grads), _reduce_begin(list(grads.values()), core, tag=f"{group}{l}"))
            return begun[l, group][1][-1]

        def reduce_continue(group, after, l=l):
            names, started = begun[l, group]
            reductions[l, group] = (names, _reduce_continue(started, core, after, tag=f"{group}{l}"))
            return reductions[l, group][1][-1]

        dx, dxb, small_grads[l] = _layer_bwd(dx, dxb, saved[l], P[l], W[l], cos2, sin2, reduce_begin, reduce_continue,
                                            last=(l == 0), rider_for=functools.partial(rider_for, l=l))
    grad_x = dx.reshape(x.shape)

    small_shapes = [weights[n].shape for n in SMALL] + [g_final.shape, (DEPTH, CONV_CH // LANES, CONV_WP, LANES)]
    parts = [jnp.stack([small_grads[l][n].reshape(weights[n].shape[1:]) for l in range(DEPTH)]) for n in SMALL]
    parts += [g_final_part.reshape(g_final.shape), jnp.stack([small_grads[l]["w_dw"] for l in range(DEPTH)])]
    packed = _pack(parts)
    packed_land = lax.dynamic_update_index_in_dim(lax.empty((N_DEV,) + packed.shape, f32), packed, me, 0)
    small_started = _ici_start("gather", [packed], [packed_land], dx, name="gather_small_start")

    grads_out, delta, new_m, new_v = {}, {}, {}, {}
    swap = lambda a: jnp.swapaxes(a, 1, 2)

    def update(n, lands, sums):
        as_arrives = n not in transposed or weights[n].shape[2] % LANES != 0
        if as_arrives:
            to_arrival = swap if n in transposed else (lambda a: a)
            out = ridden.get(n)
            for l in reversed(range(DEPTH if out is None else DEPTH - 1)):
                out = _adamw_reduced(l, to_arrival(weights[n]), to_arrival(mom_m[n]), to_arrival(mom_v[n]),
                                     lands[l], sums[l], chip, out, name=f"adamw_{n}_{l}")
            grads_out[n], delta[n], new_m[n], new_v[n] = [to_arrival(o) for o in out]
            return out[1]
        g = jnp.stack([_sum_chip_slots(lands[l], sums[l], chip, name="rs_sum_" + n) for l in range(DEPTH)])
        grads_out[n] = swap(g)
        delta[n], new_m[n], new_v[n] = _adamw(weights[n], grads_out[n], mom_m[n], mom_v[n], name="adamw_" + n)
        return delta[n]

    after = small_started[-1]
    for group in ("ffn", "mix", "in"):
        names = reductions[0, group][0]
        here = [arrive(l, group, after) for l in range(DEPTH)]
        for i, n in enumerate(names):
            after = update(n, [here[l][1][i] for l in range(DEPTH)], [here[l][0][i] for l in range(DEPTH)])

    _, small_lands = _ici_wait("gather", small_started, after, name="gather_small_wait")
    small_full = _d2d_gather([small_lands[0].reshape(N_CHIP, 2, *packed.shape)], after, name="gather_small_d2d")[0]
    total = _sum_slots(small_full.reshape(N_DEV, *packed.shape), name="sum_small")
    small_total = _unpack(total, small_shapes)
    grads_out.update(zip(SMALL + ("g_final",), small_total[:-1]))
    dw_full = small_total[-1]
    grads_out["w_dw"] = lax.dynamic_index_in_dim(dw_full, me, axis=1, keepdims=False)[:, :CONV_W].reshape(w_dw.shape)

    rep = tuple(n for n in SMALL if n != "w_s") + ("g_final",)
    rep_shapes = [weights[n].shape for n in rep]
    packs = [_pack([src[n] for n in rep])[None] for src in (weights, grads_out, mom_m, mom_v)]
    for dst, buf in zip((delta, new_m, new_v), _adamw(*packs, name="adamw_small")):
        dst.update(zip(rep, _unpack(buf[0], rep_shapes)))
    for n, shp in (("w_dw", (1, DEPTH * CONV_W, LANES)), ("w_s", (DEPTH, SG_G * SG_CHUNK, SG_CHUNK))):
        upd = _adamw(*[src[n].reshape(shp) for src in (weights, grads_out, mom_m, mom_v)], name="adamw_" + n)
        for dst, buf in zip((delta, new_m, new_v), upd):
            dst[n] = buf.reshape(weights[n].shape)

    order = ("g_mix", "w_in", "b_gate", "q_norm_g", "k_norm_g", "w_attn_o", "w_dw", "b_dw", "conv_ln_g", "conv_ln_b",
             "w_conv_o", "sg_ln_g", "sg_ln_b", "w_s", "b_s", "w_sg_o", "w_out", "g_ffn", "w_ff_gate", "w_ff_up",
             "w_ff_down", "g_final")
    return (loss, grad_x, *[grads_out[n] for n in order], *[delta[n] for n in order],
            *[new_m[n] for n in order], *[new_v[n] for n in order])
```

```python
import functools
import math

import jax
import jax.numpy as jnp
from jax import lax
from jax.experimental import pallas as pl
from jax.experimental.pallas import tpu as pltpu

f32, bf16 = jnp.float32, jnp.bfloat16

D_MODEL = 2048
SEQ = 2048
DEPTH = 2
GRID_W = 64
HEAD_DIM = 128
LANES = 128
N_Q = (D_MODEL // 2) // HEAD_DIM
N_KV = N_Q // 4
GRP = N_Q // N_KV
Q_COLS = N_Q * HEAD_DIM
KV_COLS = N_KV * HEAD_DIM
CONV_CH = D_MODEL // 2
CONV_W = 31
CONV_PAD = CONV_W // 2
CONV_WP = 32
SG_CH = D_MODEL // 2
SG_G = SG_CH // LANES
SG_CHUNK = 128
D_FF = -(-8 * D_MODEL // (3 * 256)) * 256
OFF_KV = Q_COLS
OFF_CONV = OFF_KV + 2 * KV_COLS
OFF_SG = OFF_CONV + 2 * CONV_CH
OFF_GATE = OFF_SG + 2 * SG_CH
IN_COLS = OFF_GATE + 3 * D_MODEL
ROPE_THETA = 10000.0
SCALE = HEAD_DIM ** -0.5
N_DEV = 8
N_CHIP = 4

ADAM_LR, ADAM_B1, ADAM_B2, ADAM_EPS, ADAM_WD, ADAM_STEP = 0.001, 0.9, 0.999, 1e-08, 0.01, 10

VMEM_BYTES_V7X = 64 << 20
VMEM_CAP = VMEM_BYTES_V7X - (6 << 20)
MESH = pl.DeviceIdType.MESH
HBM = pl.BlockSpec(memory_space=pltpu.HBM)


def _in_hbm(a):
    if isinstance(a, jax.Array) and jnp.issubdtype(a.dtype, jnp.floating) and a.size * a.dtype.itemsize >= (1 << 20):
        return pltpu.with_memory_space_constraint(a, pltpu.HBM)
    return a


def _out_hbm(s):
    if isinstance(s, jax.ShapeDtypeStruct) and math.prod(s.shape) * jnp.dtype(s.dtype).itemsize >= (1 << 20):
        return pltpu.HBM(s.shape, s.dtype)
    return s


def _call(body, **kw):
    shapes = kw.pop("out_shape")
    shapes = type(shapes)(_out_hbm(s) for s in shapes) if isinstance(shapes, (list, tuple)) else _out_hbm(shapes)
    call = pl.pallas_call(body, out_shape=shapes, **kw)
    return lambda *args: call(*[_in_hbm(a) for a in args])


def _pick(n, cands):
    for c in cands:
        if n % c == 0:
            return c
    raise ValueError((n, cands))


def _params(sem, vmem_bytes):
    return pltpu.CompilerParams(dimension_semantics=sem, vmem_limit_bytes=int(min(max(vmem_bytes, 16 << 20), VMEM_CAP)))


def _mm(a, b, form, out_dtype, *, n=None, b_off=0, res=None, after=None, rider=None, name):
    if form == "tn":
        K, M = a.shape
    else:
        M, K = a.shape
    N = n if n is not None else (b.shape[0] if form == "nt" else b.shape[1])
    if K <= 2048:
        tk = K
        if form == "tn":
            tm = _pick(M, (512, 256, 128))
            tn = N if N <= 2048 else _pick(N, (1024, 512, 256, 128))
        else:
            tm = M if M <= 2048 else _pick(M, (2048, 1024, 512))
            tn = _pick(math.gcd(N, b_off) if b_off else N, (256, 128) if res is not None else (512, 256, 128))
    else:
        tk = max(t for t in range(LANES, 3072 + 1, LANES) if K % t == 0)
        tm = _pick(M, (1024, 512, 256, 128))
        tn = _pick(math.gcd(N, b_off) if b_off else N, (1024, 512, 256, 128))
    assert b_off % tn == 0
    off = b_off // tn
    nk = K // tk
    if form == "tn":
        a_spec = pl.BlockSpec((tk, tm), lambda i, j, k: (k, i))
    else:
        a_spec = pl.BlockSpec((tm, tk), lambda i, j, k: (i, k))
    if form == "nt":
        b_spec = pl.BlockSpec((tn, tk), lambda i, j, k: (j + off, k))
    else:
        b_spec = pl.BlockSpec((tk, tn), lambda i, j, k: (k, j + off))
    dims = {"nn": ((1,), (0,)), "nt": ((1,), (1,)), "tn": ((0,), (0,))}[form]
    has_res = res is not None

    n_ride = len(rider["args"]) if rider else 0

    def body(*refs):
        if after is not None:
            refs = refs[1:]
        n_in = 3 if has_res else 2
        if rider:
            ride_refs = refs[n_in:n_in + n_ride] + refs[n_in + n_ride + 1:n_in + n_ride + 1 + len(rider["out_specs"])]

            @pl.when(jnp.logical_and(pl.program_id(1) == 0, pl.program_id(2) == 0))
            def _():
                rider["body"](*ride_refs)

            refs = refs[:n_in] + (refs[n_in + n_ride],) + refs[n_in + n_ride + 1 + len(rider["out_specs"]):]
        if has_res:
            a_ref, b_ref, r_ref, o_ref = refs[:4]
        else:
            a_ref, b_ref, o_ref = refs[:3]
        p = lax.dot_general(a_ref[...], b_ref[...], (dims, ((), ())), preferred_element_type=f32)

        def finish(acc):
            if has_res:
                acc = acc + r_ref[...].astype(f32)
            o_ref[...] = acc.astype(o_ref.dtype)

        if nk == 1:
            finish(p)
        else:
            acc_ref = refs[-1]
            k = pl.program_id(2)

            @pl.when(k == 0)
            def _():
                acc_ref[...] = p

            @pl.when(k > 0)
            def _():
                acc_ref[...] += p

            @pl.when(k == nk - 1)
            def _():
                finish(acc_ref[...])

    in_specs = [a_spec, b_spec]
    args = [a, b]
    osz = jnp.dtype(out_dtype).itemsize
    vmem = 2 * (tm * tk * 2 + tk * tn * 2 + tm * tn * osz) + 2 * tm * tn * 4
    if has_res:
        in_specs.append(pl.BlockSpec((tm, tn), lambda i, j, k: (i, j)))
        args.append(res)
        vmem += 2 * tm * tn * res.dtype.itemsize
    scratch = []
    if nk > 1:
        scratch.append(pltpu.VMEM((tm, tn), f32))
        vmem += tm * tn * 4
    out_specs = [pl.BlockSpec((tm, tn), lambda i, j, k: (i, j))]
    out_shape = [jax.ShapeDtypeStruct((M, N), out_dtype)]
    if rider:
        in_specs += rider["in_specs"]
        args += rider["args"]
        out_specs += rider["out_specs"]
        out_shape += rider["out_shape"]
        vmem += rider["vmem"]
    if after is not None:
        in_specs.insert(0, pl.BlockSpec(memory_space=pl.ANY))
        args.insert(0, after)
    out = _call(
        body, name=name, grid=(M // tm, N // tn, nk), in_specs=in_specs, out_specs=out_specs, out_shape=out_shape,
        scratch_shapes=scratch,
        compiler_params=_params(("arbitrary" if rider else "parallel", "parallel", "arbitrary"), vmem + (8 << 20)),
    )(*args)
    return (out[0], out[1:]) if rider else out[0]


EPI_TN = 256


def _mm_epi(a, bs, form, extras, out_dtypes, n_sums, fn, name, tn=EPI_TN):
    a_list = list(a) if isinstance(a, (list, tuple)) else [a]
    M, K = a_list[0].shape
    N = bs[0].shape[0] if form == "nt" else bs[0].shape[1]
    assert K <= 2048 and N % tn == 0 and len(a_list) in (1, len(bs))
    dims = ((1,), (1,)) if form == "nt" else ((1,), (0,))
    na, nb, ne = len(a_list), len(bs), len(extras)

    def body(*refs):
        a_refs, b_refs = refs[:na], refs[na:na + nb]
        e_refs, o_refs = refs[na + nb:na + nb + ne], refs[na + nb + ne:]
        avs = [r[...] for r in a_refs] * (nb // na)
        ps = [lax.dot_general(av, b[...], (dims, ((), ())), preferred_element_type=f32) for av, b in zip(avs, b_refs)]
        for o_ref, o in zip(o_refs, fn(ps, [e[...] for e in e_refs])):
            o_ref[...] = o.astype(o_ref.dtype)

    in_specs = [pl.BlockSpec((M, K), lambda j: (0, 0), pipeline_mode=pl.Buffered(1)) for _ in a_list]
    in_specs += [pl.BlockSpec((tn, K), lambda j: (j, 0)) if form == "nt" else pl.BlockSpec((K, tn), lambda j: (0, j))
                 for _ in bs]
    for arr, first in extras:
        assert first % tn == 0
        in_specs.append(pl.BlockSpec((arr.shape[0], tn), functools.partial(lambda j, o: (0, j + o), o=first // tn)))
    out_specs = [pl.BlockSpec((M, tn), lambda j: (0, j))] * len(out_dtypes) + [pl.BlockSpec((1, tn), lambda j: (0, j))] * n_sums
    out_shape = [jax.ShapeDtypeStruct((M, N), dt) for dt in out_dtypes] + [jax.ShapeDtypeStruct((1, N), f32)] * n_sums
    tiles = sum(arr.shape[0] * tn * arr.dtype.itemsize for arr, _ in extras) + sum(M * tn * jnp.dtype(dt).itemsize for dt in out_dtypes)
    vmem = na * M * K * 2 + 2 * nb * tn * K * 2 + 2 * tiles + (nb + 6) * M * tn * 4
    return _call(body, name=name, grid=(N // tn,), in_specs=in_specs, out_specs=out_specs, out_shape=out_shape,
                 compiler_params=_params(("parallel",), vmem + (8 << 20)))(*a_list, *bs, *[arr for arr, _ in extras])


def _ffn_up(hf, wt_gate, wt_up, name):
    def fn(ps, _):
        g, u = ps[0].astype(bf16), ps[1].astype(bf16)
        gf = g.astype(f32)
        return g, u, gf * jax.nn.sigmoid(gf) * u.astype(f32)

    return _mm_epi(hf, [wt_gate, wt_up], "nt", [], [bf16] * 3, 0, fn, name, tn=2 * EPI_TN)


def _ffn_down_bwd(dx2b, w_down, fg, fu, name):
    def fn(ps, es):
        d, g = ps[0], es[0].astype(f32)
        sg = jax.nn.sigmoid(g)
        return d * es[1].astype(f32) * sg * (1.0 + g * (1.0 - sg)), d * g * sg

    return _mm_epi(dx2b, [w_down], "nt", [(fg, 0), (fu, 0)], [bf16] * 2, 0, fn, name, tn=2 * EPI_TN)


def _mixer_out(branches, wts, gl, b_gate, name):
    D = wts[0].shape[0]

    def fn(ps, es):
        ys = [p_.astype(bf16) for p_ in ps]
        merged = None
        for i in range(3):
            term = jax.nn.sigmoid(es[i].astype(f32) + es[3 + i]) * ys[i].astype(f32)
            merged = term if merged is None else merged + term
        return ys + [merged]

    extras = [(gl, i * D) for i in range(3)] + [(b_gate, i * D) for i in range(3)]
    return _mm_epi(branches, wts, "nt", extras, [bf16] * 4, 0, fn, name)


def _merge_bwd_fused(dx1b, w_out, gl, b_gate, ya, yc, ys, name):
    D = ya.shape[1]

    def fn(ps, es):
        dm_, outs, sums = ps[0], [], []
        for i in range(3):
            gate = jax.nn.sigmoid(es[i].astype(f32) + es[3 + i])
            dlog = dm_ * es[6 + i].astype(f32) * gate * (1.0 - gate)
            outs.append((dlog, dm_ * gate))
            sums.append(jnp.sum(dlog, axis=0, keepdims=True))
        return [o[0] for o in outs] + [o[1] for o in outs] + sums

    extras = [(gl, i * D) for i in range(3)] + [(b_gate, i * D) for i in range(3)] + [(ya, 0), (yc, 0), (ys, 0)]
    return _mm_epi(dx1b, [w_out], "nt", extras, [bf16] * 6, 3, fn, name)


def _rows(body, ins, outs, *, tm, name, vmem=40 << 20):
    nrows = next(s[1].shape[0] for s in ins if s[0] == "r")
    in_specs, args = [], []
    for s in ins:
        arr = s[1]
        if s[0] == "r":
            w = s[2] if len(s) > 2 else arr.shape[1]
            cb = s[3] if len(s) > 3 else 0
            in_specs.append(pl.BlockSpec((tm, w), functools.partial(lambda i, cb: (i, cb), cb=cb)))
        else:
            in_specs.append(pl.BlockSpec(arr.shape, functools.partial(lambda i, nd: (0,) * nd, nd=arr.ndim)))
        args.append(arr)
    out_specs, out_shape = [], []
    for s in outs:
        if s[0] == "r":
            out_specs.append(pl.BlockSpec((tm, s[1]), lambda i: (i, 0)))
            out_shape.append(jax.ShapeDtypeStruct((nrows, s[1]), s[2]))
        else:
            out_specs.append(pl.BlockSpec(s[1], functools.partial(lambda i, nd: (0,) * nd, nd=len(s[1]))))
            out_shape.append(jax.ShapeDtypeStruct(s[1], s[2]))
    return _call(body, name=name, grid=(nrows // tm,), in_specs=in_specs, out_specs=out_specs,
                 out_shape=out_shape, compiler_params=_params(("arbitrary",), vmem))(*args)


def _accumulate(ref, part):
    i = pl.program_id(0)

    @pl.when(i == 0)
    def _():
        ref[...] = part

    @pl.when(i > 0)
    def _():
        ref[...] += part


def _rms_stats(x):
    r = lax.rsqrt(jnp.mean(x * x, axis=-1, keepdims=True) + 1e-6)
    return r, x * r


def _rms_fwd(x, g, name):
    def body(x_ref, g_ref, o_ref):
        _, xn = _rms_stats(x_ref[...])
        o_ref[...] = (xn * g_ref[...]).astype(o_ref.dtype)

    return _rows(body, [("r", x), ("f", g)], [("r", x.shape[1], bf16)], tm=min(256, x.shape[0]), name=name)[0]


def _rms_bwd(x, g, dh, dres, name):
    D = x.shape[1]

    def body(x_ref, g_ref, dh_ref, dr_ref, dx_ref, dxb_ref, dg_ref):
        r, xn = _rms_stats(x_ref[...])
        dy = dh_ref[...].astype(f32)
        dxn = dy * g_ref[...]
        dx = dr_ref[...] + r * (dxn - xn * jnp.mean(dxn * xn, axis=-1, keepdims=True))
        dx_ref[...] = dx
        dxb_ref[...] = dx.astype(bf16)
        _accumulate(dg_ref, jnp.sum(dy * xn, axis=0, keepdims=True))

    return _rows(body, [("r", x), ("f", g), ("r", dh), ("r", dres)],
                 [("r", D, f32), ("r", D, bf16), ("a", (1, D), f32)], tm=min(256, x.shape[0]), name=name)


def _final_loss(x, g, tgt, name):
    D = x.shape[1]

    def body(x_ref, g_ref, t_ref, dx_ref, dxb_ref, sq_ref, dg_ref):
        r, xn = _rms_stats(x_ref[...])
        gain = g_ref[...]
        diff = xn * gain - t_ref[...]
        dy = diff * (1.0 / D)
        dxn = dy * gain
        dx = r * (dxn - xn * jnp.mean(dxn * xn, axis=-1, keepdims=True))
        dx_ref[...] = dx
        dxb_ref[...] = dx.astype(bf16)
        _accumulate(sq_ref, jnp.sum(diff * diff, axis=0, keepdims=True))
        _accumulate(dg_ref, jnp.sum(dy * xn, axis=0, keepdims=True))

    return _rows(body, [("r", x), ("f", g), ("r", tgt)],
                 [("r", D, f32), ("r", D, bf16), ("a", (1, D), f32), ("a", (1, D), f32)],
                 tm=min(256, x.shape[0]), name=name)


def _qk_fwd(q_raw, kv_raw, qg, kg, cos2, sin2, name):
    def body(q_ref, k_ref, qg_ref, kg_ref, c_ref, s_ref, qo_ref, ko_ref):
        c, s = c_ref[...], s_ref[...]

        def head(src, gain, dst, h):
            cols = slice(h * HEAD_DIM, (h + 1) * HEAD_DIM)
            _, xn = _rms_stats(src[:, cols].astype(f32))
            y = xn * gain
            dst[:, cols] = (y * c + pltpu.roll(y, HEAD_DIM // 2, 1) * s).astype(dst.dtype)

        for h in range(N_Q):
            head(q_ref, qg_ref[...], qo_ref, h)
        for h in range(N_KV):
            head(k_ref, kg_ref[...], ko_ref, h)

    return _rows(body, [("r", q_raw), ("r", kv_raw, KV_COLS, 0), ("f", qg), ("f", kg), ("r", cos2), ("r", sin2)],
                 [("r", Q_COLS, bf16), ("r", KV_COLS, bf16)], tm=min(256, q_raw.shape[0]), name=name)


def _qk_bwd(q_raw, kv_raw, dqr, dkr, qg, kg, cos2, sin2, name):
    def body(q_ref, k_ref, dq_ref, dk_ref, qg_ref, kg_ref, c_ref, s_ref, dqo_ref, dko_ref, dqg_ref, dkg_ref):
        c, s = c_ref[...], s_ref[...]

        def head(src, dsrc, gain, dst, h):
            cols = slice(h * HEAD_DIM, (h + 1) * HEAD_DIM)
            r, xn = _rms_stats(src[:, cols].astype(f32))
            do = dsrc[:, cols].astype(f32)
            dy = do * c + pltpu.roll(do * s, HEAD_DIM // 2, 1)
            dxn = dy * gain
            dst[:, cols] = (r * (dxn - xn * jnp.mean(dxn * xn, axis=-1, keepdims=True))).astype(dst.dtype)
            return jnp.sum(dy * xn, axis=0, keepdims=True)

        dq_gain = head(q_ref, dq_ref, qg_ref[...], dqo_ref, 0)
        for h in range(1, N_Q):
            dq_gain = dq_gain + head(q_ref, dq_ref, qg_ref[...], dqo_ref, h)
        dk_gain = head(k_ref, dk_ref, kg_ref[...], dko_ref, 0)
        for h in range(1, N_KV):
            dk_gain = dk_gain + head(k_ref, dk_ref, kg_ref[...], dko_ref, h)
        _accumulate(dqg_ref, dq_gain)
        _accumulate(dkg_ref, dk_gain)

    return _rows(body, [("r", q_raw), ("r", kv_raw, KV_COLS, 0), ("r", dqr), ("r", dkr), ("f", qg), ("f", kg),
                        ("r", cos2), ("r", sin2)],
                 [("r", Q_COLS, bf16), ("r", KV_COLS, bf16), ("a", (1, HEAD_DIM), f32), ("a", (1, HEAD_DIM), f32)],
                 tm=min(256, q_raw.shape[0]), name=name)


def _softmax_rows(q, k):
    s = lax.dot_general(q, k, (((1,), (1,)), ((), ())), preferred_element_type=f32) * (SCALE * math.log2(math.e))
    p = jnp.exp2(s - jnp.max(s, axis=-1, keepdims=True))
    return p * (1.0 / jnp.sum(p, axis=-1, keepdims=True))


def _head_cols(g):
    return slice(g * HEAD_DIM, (g + 1) * HEAD_DIM)


def _attn_fwd(qr, kr, kv_raw, name):
    S = qr.shape[0]
    tq = min(256, S)

    def body(q_ref, k_ref, v_ref, o_ref):
        k, v = k_ref[...], v_ref[...]
        for g in range(GRP):
            p = _softmax_rows(q_ref[:, _head_cols(g)], k)
            o_ref[:, _head_cols(g)] = jnp.dot(p.astype(bf16), v, preferred_element_type=f32).astype(o_ref.dtype)

    return _call(
        body, name=name, grid=(N_KV, S // tq),
        in_specs=[pl.BlockSpec((tq, GRP * HEAD_DIM), lambda kv, i: (i, kv)),
                  pl.BlockSpec((S, HEAD_DIM), lambda kv, i: (0, kv)),
                  pl.BlockSpec((S, HEAD_DIM), lambda kv, i: (0, N_KV + kv))],
        out_specs=pl.BlockSpec((tq, GRP * HEAD_DIM), lambda kv, i: (i, kv)),
        out_shape=jax.ShapeDtypeStruct((S, Q_COLS), bf16),
        compiler_params=_params(("parallel", "arbitrary"), 4 * GRP * tq * S * 4 + (8 << 20)),
    )(qr, kr, kv_raw)


def _attn_bwd(qr, kr, kv_raw, do, name):
    S = qr.shape[0]
    tq = min(256, S)

    def body(q_ref, k_ref, v_ref, do_ref, dq_ref, dk_ref, dv_ref):
        first = pl.program_id(1) == 0
        k, v = k_ref[...], v_ref[...]
        dv_part = dk_part = None
        for g in range(GRP):
            q, do_ = q_ref[:, _head_cols(g)], do_ref[:, _head_cols(g)]
            p = _softmax_rows(q, k)
            dp = lax.dot_general(do_, v, (((1,), (1,)), ((), ())), preferred_element_type=f32)
            ds = (p * (dp - jnp.sum(dp * p, axis=-1, keepdims=True)) * SCALE).astype(bf16)
            dq_ref[:, _head_cols(g)] = jnp.dot(ds, k, preferred_element_type=f32).astype(dq_ref.dtype)
            dv_g = lax.dot_general(p.astype(bf16), do_, (((0,), (0,)), ((), ())), preferred_element_type=f32)
            dk_g = lax.dot_general(ds, q, (((0,), (0,)), ((), ())), preferred_element_type=f32)
            dv_part = dv_g if g == 0 else dv_part + dv_g
            dk_part = dk_g if g == 0 else dk_part + dk_g

        @pl.when(first)
        def _():
            dv_ref[...] = dv_part
            dk_ref[...] = dk_part

        @pl.when(jnp.logical_not(first))
        def _():
            dv_ref[...] += dv_part
            dk_ref[...] += dk_part

    qspec = pl.BlockSpec((tq, GRP * HEAD_DIM), lambda kv, i: (i, kv))
    return _call(
        body, name=name, grid=(N_KV, S // tq),
        in_specs=[qspec, pl.BlockSpec((S, HEAD_DIM), lambda kv, i: (0, kv)),
                  pl.BlockSpec((S, HEAD_DIM), lambda kv, i: (0, N_KV + kv)), qspec],
        out_specs=[qspec, pl.BlockSpec((S, HEAD_DIM), lambda kv, i: (0, kv)),
                   pl.BlockSpec((S, HEAD_DIM), lambda kv, i: (0, kv))],
        out_shape=[jax.ShapeDtypeStruct((S, Q_COLS), bf16), jax.ShapeDtypeStruct((S, KV_COLS), f32),
                   jax.ShapeDtypeStruct((S, KV_COLS), f32)],
        compiler_params=_params(("parallel", "arbitrary"), 6 * GRP * tq * S * 4 + (8 << 20)),
    )(qr, kr, kv_raw, do)


CONV_HALO = 16


def _fill_padded(pad_ref, val, S):
    pad_ref[pl.ds(0, CONV_HALO), :] = jnp.zeros((CONV_HALO, LANES), f32)
    pad_ref[pl.ds(CONV_HALO + S, CONV_HALO), :] = jnp.zeros((CONV_HALO, LANES), f32)
    pad_ref[pl.ds(CONV_HALO, S), :] = val


def _group_specs(S, n_groups, second_half):
    return pl.BlockSpec((S, LANES), functools.partial(lambda g, o: (0, g + o), o=n_groups if second_half else 0))


def _conv1_fwd(conv_in, wdw, b_dw, name):
    S = conv_in.shape[0]
    ng = CONV_CH // LANES
    R = min(256, S)

    def body(a_ref, g_ref, w_ref, b_ref, o_ref, pad_ref):
        z = a_ref[...].astype(f32) * jax.nn.sigmoid(g_ref[...].astype(f32))
        _fill_padded(pad_ref, z, S)
        for r in range(S // R):
            acc = jnp.zeros((R, LANES), f32) + b_ref[...]
            for j in range(CONV_W):
                acc = acc + w_ref[pl.ds(j, 1), :] * pad_ref[pl.ds(r * R + CONV_HALO - CONV_PAD + j, R), :]
            o_ref[pl.ds(r * R, R), :] = acc

    return _call(
        body, name=name, grid=(ng,),
        in_specs=[_group_specs(S, ng, False), _group_specs(S, ng, True),
                  pl.BlockSpec((CONV_WP, LANES), lambda g: (g, 0)), pl.BlockSpec((1, LANES), lambda g: (0, g))],
        out_specs=pl.BlockSpec((S, LANES), lambda g: (0, g)),
        out_shape=jax.ShapeDtypeStruct((S, CONV_CH), f32),
        scratch_shapes=[pltpu.VMEM((S + 2 * CONV_HALO, LANES), f32)],
        compiler_params=_params(("parallel",), 24 << 20),
    )(conv_in, conv_in, wdw, b_dw)


def _conv1_bwd(conv_in, dc, wdw, name):
    S = conv_in.shape[0]
    ng = CONV_CH // LANES
    R = min(256, S)

    def body(a_ref, g_ref, w_ref, dc_ref, da_ref, dg_ref, dw_ref, db_ref, padz_ref, padd_ref):
        a = a_ref[...].astype(f32)
        sg = jax.nn.sigmoid(g_ref[...].astype(f32))
        _fill_padded(padz_ref, a * sg, S)
        _fill_padded(padd_ref, dc_ref[...], S)
        for r in range(S // R):
            dz = jnp.zeros((R, LANES), f32)
            for j in range(CONV_W):
                dz = dz + w_ref[pl.ds(j, 1), :] * padd_ref[pl.ds(r * R + CONV_HALO + CONV_PAD - j, R), :]
            rows = pl.ds(r * R, R)
            ar, sr = a_ref[rows, :].astype(f32), jax.nn.sigmoid(g_ref[rows, :].astype(f32))
            da_ref[rows, :] = (dz * sr).astype(da_ref.dtype)
            dg_ref[rows, :] = (dz * ar * sr * (1.0 - sr)).astype(dg_ref.dtype)
        for j in range(CONV_W):
            tot = jnp.zeros((1, LANES), f32)
            for r in range(S // R):
                tot = tot + jnp.sum(dc_ref[pl.ds(r * R, R), :] * padz_ref[pl.ds(r * R + CONV_HALO - CONV_PAD + j, R), :],
                                    axis=0, keepdims=True)
            dw_ref[pl.ds(j, 1), :] = tot
        dw_ref[pl.ds(CONV_W, CONV_WP - CONV_W), :] = jnp.zeros((CONV_WP - CONV_W, LANES), f32)
        db_ref[...] = jnp.sum(dc_ref[...], axis=0, keepdims=True)

    return _call(
        body, name=name, grid=(ng,),
        in_specs=[_group_specs(S, ng, False), _group_specs(S, ng, True),
                  pl.BlockSpec((CONV_WP, LANES), lambda g: (g, 0)), pl.BlockSpec((S, LANES), lambda g: (0, g))],
        out_specs=[pl.BlockSpec((S, LANES), lambda g: (0, g)), pl.BlockSpec((S, LANES), lambda g: (0, g)),
                   pl.BlockSpec((CONV_WP, LANES), lambda g: (g, 0)), pl.BlockSpec((1, LANES), lambda g: (0, g))],
        out_shape=[jax.ShapeDtypeStruct((S, CONV_CH), bf16), jax.ShapeDtypeStruct((S, CONV_CH), bf16),
                   jax.ShapeDtypeStruct((ng * CONV_WP, LANES), f32), jax.ShapeDtypeStruct((1, CONV_CH), f32)],
        scratch_shapes=[pltpu.VMEM((S + 2 * CONV_HALO, LANES), f32), pltpu.VMEM((S + 2 * CONV_HALO, LANES), f32)],
        compiler_params=_params(("parallel",), 24 << 20),
    )(conv_in, conv_in, wdw, dc)


def _ln_stats(x, eps=1e-5):
    xc = x - jnp.mean(x, axis=-1, keepdims=True)
    r = lax.rsqrt(jnp.mean(xc * xc, axis=-1, keepdims=True) + eps)
    return r, xc * r


def _ln_bwd(r, xh, dxh):
    return r * (dxh - jnp.mean(dxh, axis=-1, keepdims=True) - xh * jnp.mean(dxh * xh, axis=-1, keepdims=True))


def _conv2_fwd(c, ln_g, ln_b, name):
    def body(c_ref, g_ref, b_ref, o_ref):
        _, xh = _ln_stats(c_ref[...])
        y = xh * g_ref[...] + b_ref[...]
        o_ref[...] = (y * jax.nn.sigmoid(y)).astype(o_ref.dtype)

    return _rows(body, [("r", c), ("f", ln_g), ("f", ln_b)], [("r", CONV_CH, bf16)], tm=min(256, c.shape[0]), name=name)[0]


def _conv2_bwd(c, dcz, ln_g, ln_b, name):
    def body(c_ref, d_ref, g_ref, b_ref, dc_ref, dg_ref, db_ref):
        r, xh = _ln_stats(c_ref[...])
        y = xh * g_ref[...] + b_ref[...]
        sg = jax.nn.sigmoid(y)
        dy = d_ref[...].astype(f32) * (sg * (1.0 + y * (1.0 - sg)))
        dc_ref[...] = _ln_bwd(r, xh, dy * g_ref[...])
        _accumulate(dg_ref, jnp.sum(dy * xh, axis=0, keepdims=True))
        _accumulate(db_ref, jnp.sum(dy, axis=0, keepdims=True))

    return _rows(body, [("r", c), ("r", dcz), ("f", ln_g), ("f", ln_b)],
                 [("r", CONV_CH, f32), ("a", (1, CONV_CH), f32), ("a", (1, CONV_CH), f32)],
                 tm=min(256, c.shape[0]), name=name)


GELU_K = math.sqrt(2.0 / math.pi)
GELU_C = 0.044715


def _gelu(x):
    return 0.5 * x * (1.0 + jnp.tanh(GELU_K * (x + GELU_C * x * x * x)))


def _gelu_and_grad(x):
    x2 = x * x
    th = jnp.tanh(GELU_K * (x + GELU_C * x2 * x))
    half = 0.5 * (1.0 + th)
    return x * half, half + 0.5 * x * (1.0 - th * th) * (GELU_K * (1.0 + 3.0 * GELU_C * x2))


def _chunk_rows(n):
    return pl.ds(pl.multiple_of(n * SG_CHUNK, SG_CHUNK), SG_CHUNK)


def _sgu_fwd(sg_in, ln_g, ln_b, w_s, b_s, name):
    S = sg_in.shape[0]

    def body(u_ref, v_ref, lg_ref, lb_ref, w_ref, b_ref, o_ref):
        wb = w_ref[...].astype(bf16)

        def chunk(n, carry):
            rows = _chunk_rows(n)
            gu = _gelu(u_ref[rows, :].astype(f32))
            _, xh = _ln_stats(_gelu(v_ref[rows, :].astype(f32)))
            vl = xh * lg_ref[...] + lb_ref[...]
            t = jnp.dot(wb, vl.astype(bf16), preferred_element_type=f32) + b_ref[...]
            o_ref[rows, :] = (gu * t).astype(o_ref.dtype)
            return carry

        lax.fori_loop(0, S // SG_CHUNK, chunk, 0, unroll=2)

    return _call(
        body, name=name, grid=(SG_G,),
        in_specs=[_group_specs(S, SG_G, False), _group_specs(S, SG_G, True),
                  pl.BlockSpec((1, LANES), lambda g: (0, g)), pl.BlockSpec((1, LANES), lambda g: (0, g)),
                  pl.BlockSpec((None, SG_CHUNK, SG_CHUNK), lambda g: (g, 0, 0)),
                  pl.BlockSpec((None, SG_CHUNK, 1), lambda g: (g, 0, 0))],
        out_specs=pl.BlockSpec((S, LANES), lambda g: (0, g)),
        out_shape=jax.ShapeDtypeStruct((S, SG_CH), bf16),
        compiler_params=_params(("parallel",), 24 << 20),
    )(sg_in, sg_in, ln_g, ln_b, w_s, b_s)


def _sgu_bwd(sg_in, dsz, ln_g, ln_b, w_s, w_s_t, b_s, name):
    S = sg_in.shape[0]

    def body(u_ref, v_ref, lg_ref, lb_ref, w_ref, wt_ref, b_ref, d_ref, du_ref, dv_ref, dw_ref, db_ref, dlg_ref, dlb_ref):
        wb = w_ref[...].astype(bf16)
        wtb = wt_ref[...].astype(bf16)

        def chunk(n, carry):
            dwa, dba, dlga, dlba = carry
            rows = _chunk_rows(n)
            u = u_ref[rows, :].astype(f32)
            v = v_ref[rows, :].astype(f32)
            gu, gu_grad = _gelu_and_grad(u)
            gv, gv_grad = _gelu_and_grad(v)
            r, xh = _ln_stats(gv)
            vlb = (xh * lg_ref[...] + lb_ref[...]).astype(bf16)
            t = jnp.dot(wb, vlb, preferred_element_type=f32) + b_ref[...]
            d = d_ref[rows, :].astype(f32)
            dt = d * gu
            dtb = dt.astype(bf16)
            dwa = dwa + lax.dot_general(dtb, vlb, (((1,), (1,)), ((), ())), preferred_element_type=f32)
            dba = dba + jnp.sum(dt, axis=1, keepdims=True)
            dvl = jnp.dot(wtb, dtb, preferred_element_type=f32)
            dlga = dlga + jnp.sum(dvl * xh, axis=0, keepdims=True)
            dlba = dlba + jnp.sum(dvl, axis=0, keepdims=True)
            dgv = _ln_bwd(r, xh, dvl * lg_ref[...])
            du_ref[rows, :] = (d * t * gu_grad).astype(du_ref.dtype)
            dv_ref[rows, :] = (dgv * gv_grad).astype(dv_ref.dtype)
            return dwa, dba, dlga, dlba

        init = (jnp.zeros((SG_CHUNK, SG_CHUNK), f32), jnp.zeros((SG_CHUNK, 1), f32),
                jnp.zeros((1, LANES), f32), jnp.zeros((1, LANES), f32))
        dwa, dba, dlga, dlba = lax.fori_loop(0, S // SG_CHUNK, chunk, init, unroll=2)
        dw_ref[...] = dwa
        db_ref[...] = dba
        dlg_ref[...] = dlga
        dlb_ref[...] = dlba

    wspec = pl.BlockSpec((None, SG_CHUNK, SG_CHUNK), lambda g: (g, 0, 0))
    bspec = pl.BlockSpec((None, SG_CHUNK, 1), lambda g: (g, 0, 0))
    lspec = pl.BlockSpec((1, LANES), lambda g: (0, g))
    cspec = pl.BlockSpec((S, LANES), lambda g: (0, g))
    return _call(
        body, name=name, grid=(SG_G,),
        in_specs=[_group_specs(S, SG_G, False), _group_specs(S, SG_G, True), lspec, lspec, wspec, wspec, bspec, cspec],
        out_specs=[cspec, cspec, wspec, bspec, lspec, lspec],
        out_shape=[jax.ShapeDtypeStruct((S, SG_CH), bf16), jax.ShapeDtypeStruct((S, SG_CH), bf16),
                   jax.ShapeDtypeStruct((SG_G, SG_CHUNK, SG_CHUNK), f32), jax.ShapeDtypeStruct((SG_G, SG_CHUNK, 1), f32),
                   jax.ShapeDtypeStruct((1, SG_CH), f32), jax.ShapeDtypeStruct((1, SG_CH), f32)],
        compiler_params=_params(("parallel",), 24 << 20),
    )(sg_in, sg_in, ln_g, ln_b, w_s, w_s_t, b_s, dsz)


def _row_tile(r, c, n_arrays, itemsize=4):
    fits = [tm for tm in range(16, r + 1, 16) if r % tm == 0 and 2 * n_arrays * tm * c * itemsize <= (24 << 20)]
    return fits[-1] if fits else r


def _sum_slots(slots, name):
    n, r, c = slots.shape
    tm = _row_tile(r, c, n + 2)

    def body(s_ref, o_ref):
        acc = s_ref[0].astype(f32)
        for k in range(1, n):
            acc = acc + s_ref[k].astype(f32)
        o_ref[...] = acc

    return _call(body, name=name, grid=(r // tm,),
                 in_specs=[pl.BlockSpec((n, tm, c), lambda i: (0, i, 0))],
                 out_specs=pl.BlockSpec((tm, c), lambda i: (i, 0)),
                 out_shape=jax.ShapeDtypeStruct((r, c), f32),
                 compiler_params=_params(("parallel",), 40 << 20))(slots)


def _add_sibling(g4, recv, core, name):
    _, _, r, c = g4.shape
    tm = _row_tile(r, c, 3, 2)

    def body(core_ref, g_ref, r_ref, o_ref):
        o_ref[...] = (g_ref[...].astype(f32) + r_ref[...].astype(f32)).astype(o_ref.dtype)

    grid_spec = pltpu.PrefetchScalarGridSpec(
        num_scalar_prefetch=1, grid=(N_CHIP, r // tm),
        in_specs=[pl.BlockSpec((None, None, tm, c), lambda k, i, core_ref: (k, core_ref[0], i, 0)),
                  pl.BlockSpec((None, tm, c), lambda k, i, core_ref: (k, i, 0))],
        out_specs=pl.BlockSpec((None, tm, c), lambda k, i, core_ref: (k, i, 0)))
    return _call(body, name=name, grid_spec=grid_spec, out_shape=jax.ShapeDtypeStruct((N_CHIP, r, c), bf16),
                 compiler_params=_params(("parallel", "parallel"), 40 << 20))(core, g4, recv)


def _adamw(w, g, m, v, name):
    L, r, c = w.shape
    tm = _row_tile(r, c, 7)
    c1 = 1.0 - ADAM_B1 ** ADAM_STEP
    c2 = 1.0 - ADAM_B2 ** ADAM_STEP

    def body(w_ref, g_ref, m_ref, v_ref, d_ref, mo_ref, vo_ref):
        g_ = g_ref[...]
        m_ = ADAM_B1 * m_ref[...] + (1.0 - ADAM_B1) * g_
        v_ = ADAM_B2 * v_ref[...] + (1.0 - ADAM_B2) * (g_ * g_)
        d_ref[...] = -ADAM_LR * ((m_ / c1) / (jnp.sqrt(v_ / c2) + ADAM_EPS) + ADAM_WD * w_ref[...])
        mo_ref[...] = m_
        vo_ref[...] = v_

    spec = pl.BlockSpec((None, tm, c), lambda l, i: (l, i, 0))
    shp = jax.ShapeDtypeStruct((L, r, c), f32)
    return _call(body, name=name, grid=(L, r // tm), in_specs=[spec] * 4, out_specs=[spec] * 3,
                 out_shape=[shp] * 3, compiler_params=_params(("parallel", "parallel"), 40 << 20))(w, g, m, v)


def _mesh_pos():
    return lax.axis_index("x"), lax.axis_index("y"), lax.axis_index("c")


SEM = pl.BlockSpec(memory_space=pltpu.SEMAPHORE)
ANY = pl.BlockSpec(memory_space=pl.ANY)
EFFECT = pltpu.SideEffectType.DATAFLOW_SIDE_EFFECTING


def _other_chips(x, y):
    return [(1 - x, y), (x, 1 - y), (1 - x, 1 - y)]


def _peers(kind, x, y):
    return [(x, y)] if kind == "sibling" else _other_chips(x, y)


def _ici_copy(kind, src_ref, land_ref, send_sem, recv_sem, sender, target, c):
    (sx, sy), (tx, ty) = sender, target
    if kind == "sibling":
        return pltpu.make_async_remote_copy(src_ref=src_ref.at[:, 1 - c], dst_ref=land_ref, send_sem=send_sem,
                                            recv_sem=recv_sem, device_id=(tx, ty, 1 - c), device_id_type=MESH)
    if kind == "gather":
        src, dst = src_ref, land_ref.at[4 * sx + 2 * sy + c]
    else:
        src, dst = src_ref.at[2 * tx + ty], land_ref.at[2 * sx + sy]
    return pltpu.make_async_remote_copy(src_ref=src, dst_ref=dst, send_sem=send_sem, recv_sem=recv_sem,
                                        device_id=(tx, ty, c), device_id_type=MESH)


def _ici_start(kind, srcs, lands, after, name):
    n = len(srcs)
    npeer = 1 if kind == "sibling" else 3

    def body(*refs):
        src_refs, land_refs = refs[:n], refs[n:2 * n]
        send_sems, recv_sems = refs[2 * n + 1], refs[2 * n + 2]
        token = refs[-1]
        x, y, c = _mesh_pos()
        for j, chip in enumerate(_peers(kind, x, y)):
            for k in range(n):
                _ici_copy(kind, src_refs[k], land_refs[k], send_sems.at[npeer * k + j], recv_sems.at[npeer * k + j],(x, y), chip, c).start()
        token[...] = jnp.zeros_like(token)

    bufs = list(srcs) + list(lands)
    return _call(
        body, name=name,
        out_shape=(pltpu.SemaphoreType.DMA((npeer * n,)), pltpu.SemaphoreType.DMA((npeer * n,)),
                   *[pltpu.HBM(b.shape, b.dtype) for b in bufs], jax.ShapeDtypeStruct((8, LANES), f32)),
        in_specs=[HBM] * (2 * n) + [ANY], out_specs=(SEM, SEM, *[HBM] * (2 * n), pl.BlockSpec(memory_space=pltpu.VMEM)),
        input_output_aliases={i: 2 + i for i in range(2 * n)},
        compiler_params=pltpu.CompilerParams(has_side_effects=EFFECT),
    )(*[pltpu.with_memory_space_constraint(b, pltpu.HBM) for b in bufs], after)


def _ici_wait(kind, started, after, name):
    send_sems, recv_sems, *bufs = started[:-1]
    n = len(bufs) // 2
    npeer = 1 if kind == "sibling" else 3

    def body(*refs):
        src_refs, land_refs = refs[:n], refs[n:2 * n]
        send_sems, recv_sems = refs[2 * n], refs[2 * n + 1]
        x, y, c = _mesh_pos()
        for j, chip in enumerate(_peers(kind, x, y)):
            for k in range(n):
                _ici_copy(kind, src_refs[k], land_refs[k], send_sems.at[npeer * k + j], recv_sems.at[npeer * k + j],(x, y), chip, c).wait_send()
                _ici_copy(kind, src_refs[k], land_refs[k], send_sems.at[npeer * k + j], recv_sems.at[npeer * k + j],chip, (x, y), c).wait_recv()

    out = _call(
        body, name=name, out_shape=[pltpu.HBM(b.shape, b.dtype) for b in bufs],
        in_specs=[HBM] * (2 * n) + [SEM, SEM, ANY], out_specs=[HBM] * (2 * n),
        input_output_aliases={i: i for i in range(2 * n)},
        compiler_params=pltpu.CompilerParams(has_side_effects=EFFECT),
    )(*bufs, send_sems, recv_sems, after)
    return out[:n], out[n:]


def _d2d_gather(lands, after, name):
    n = len(lands)

    def body(*refs):
        in_refs, o_refs = refs[:n], refs[n + 1:2 * n + 1]
        send_sems, recv_sems = refs[2 * n + 1:]
        x, y, c = _mesh_pos()
        copies = [pltpu.make_async_remote_copy(
            src_ref=in_refs[k].at[:, c], dst_ref=o_refs[k].at[:, c], send_sem=send_sems.at[k], recv_sem=recv_sems.at[k],
            device_id=(x, y, 1 - c), device_id_type=MESH) for k in range(n)]
        for cp in copies:
            cp.start()
        for k, cp in enumerate(copies):
            cp.wait_send()
            pltpu.make_async_remote_copy(
                src_ref=in_refs[k].at[:, c], dst_ref=o_refs[k].at[:, 1 - c], send_sem=send_sems.at[k],
                recv_sem=recv_sems.at[k], device_id=(x, y, 1 - c), device_id_type=MESH).wait_recv()

    return _call(
        body, name=name, in_specs=[HBM] * n + [ANY], out_specs=[HBM] * n,
        out_shape=[jax.ShapeDtypeStruct(b.shape, b.dtype) for b in lands],
        input_output_aliases={k: k for k in range(n)},
        scratch_shapes=[pltpu.SemaphoreType.DMA((n,)), pltpu.SemaphoreType.DMA((n,))],
    )(*lands, after)


def _sum_chip_slots(lands, sums, chip, name):
    _, r, c = lands.shape
    tm = _row_tile(r, c, 10, 2)

    def body(chip_ref, l_ref, s_ref, o_ref):
        acc = None
        for k in range(N_CHIP):
            part = jnp.where(chip_ref[0] == k, s_ref[k], l_ref[k]).astype(f32)
            acc = part if acc is None else acc + part
        o_ref[...] = acc

    grid_spec = pltpu.PrefetchScalarGridSpec(
        num_scalar_prefetch=1, grid=(r // tm,),
        in_specs=[pl.BlockSpec((N_CHIP, tm, c), lambda i, chip_ref: (0, i, 0)),
                  pl.BlockSpec((N_CHIP, tm, c), lambda i, chip_ref: (0, i, 0))],
        out_specs=pl.BlockSpec((tm, c), lambda i, chip_ref: (i, 0)))
    return _call(body, name=name, grid_spec=grid_spec, out_shape=jax.ShapeDtypeStruct((r, c), f32),
                 compiler_params=_params(("parallel",), 40 << 20))(chip, lands, sums)


def _reduce_begin(grads, core, tag):
    g4s = [g.reshape(N_CHIP, 2, g.shape[0] // N_DEV, g.shape[1]) for g in grads]
    recvs = [lax.empty((N_CHIP,) + g.shape[2:], g.dtype) for g in g4s]
    return _ici_start("sibling", g4s, recvs, core, name="rs_d2d_start_" + tag)


def _reduce_continue(begun, core, after, tag):
    g4s, recvs = _ici_wait("sibling", begun, after, name="rs_d2d_wait_" + tag)
    sums = [_add_sibling(g4, rv, core, name="rs_add_" + tag) for g4, rv in zip(g4s, recvs)]
    lands = [lax.empty(s.shape, s.dtype) for s in sums]
    return _ici_start("reduce", sums, lands, core, name="rs_start_" + tag)


def _adamw_tile(chip_ref, w_ref, m_ref, v_ref, l_ref, s_ref, g_ref, d_ref, mo_ref, vo_ref):
    c1 = 1.0 - ADAM_B1 ** ADAM_STEP
    c2 = 1.0 - ADAM_B2 ** ADAM_STEP
    g_ = None
    for k in range(N_CHIP):
        part = jnp.where(chip_ref[0] == k, s_ref[k], l_ref[k]).astype(f32)
        g_ = part if g_ is None else g_ + part
    m_ = ADAM_B1 * m_ref[...] + (1.0 - ADAM_B1) * g_
    v_ = ADAM_B2 * v_ref[...] + (1.0 - ADAM_B2) * (g_ * g_)
    g_ref[...] = g_
    d_ref[...] = -ADAM_LR * ((m_ / c1) / (jnp.sqrt(v_ / c2) + ADAM_EPS) + ADAM_WD * w_ref[...])
    mo_ref[...] = m_
    vo_ref[...] = v_


def _adamw_rider(layer, w, m, v, lands, sums, chip, steps):
    L, r, c = w.shape
    tm = r // steps
    assert tm * steps == r and tm % 16 == 0
    wspec = pl.BlockSpec((None, tm, c), lambda i, j, k: (layer, i, 0))
    sspec = pl.BlockSpec((N_CHIP, tm, c), lambda i, j, k: (0, i, 0))
    return dict(args=[chip, w, m, v, lands, sums],
                in_specs=[pl.BlockSpec(memory_space=pltpu.SMEM)] + [wspec] * 3 + [sspec] * 2,
                out_specs=[wspec] * 4, out_shape=[jax.ShapeDtypeStruct((L, r, c), f32)] * 4,
                body=_adamw_tile, vmem=2 * 11 * tm * c * 4)


def _adamw_reduced(layer, w, m, v, lands, sums, chip, prev, name):
    L, r, c = w.shape
    tm = _row_tile(r, c, 11)
    n_prev = 0 if prev is None else 4

    def body(chip_ref, w_ref, m_ref, v_ref, l_ref, s_ref, *refs):
        _adamw_tile(chip_ref, w_ref, m_ref, v_ref, l_ref, s_ref, *refs[n_prev:])

    wspec = pl.BlockSpec((None, tm, c), lambda i, chip_ref: (layer, i, 0))
    sspec = pl.BlockSpec((N_CHIP, tm, c), lambda i, chip_ref: (0, i, 0))
    grid_spec = pltpu.PrefetchScalarGridSpec(
        num_scalar_prefetch=1, grid=(r // tm,), in_specs=[wspec] * 3 + [sspec] * 2 + [ANY] * n_prev, out_specs=[wspec] * 4)
    return _call(body, name=name, grid_spec=grid_spec, out_shape=[jax.ShapeDtypeStruct((L, r, c), f32)] * 4,
                 input_output_aliases={6 + i: i for i in range(n_prev)},
                 compiler_params=_params(("parallel",), 40 << 20))(chip, w, m, v, lands, sums, *(prev or ()))


def _rope_tables(S):
    rows = S // GRID_W
    row = jnp.repeat(jnp.arange(rows, dtype=f32), GRID_W)
    col = jnp.tile(jnp.arange(GRID_W, dtype=f32), rows)
    nf = HEAD_DIM // 4
    inv = ROPE_THETA ** (-jnp.arange(nf, dtype=f32) / nf)
    ang = jnp.concatenate([row[:, None] * inv, col[:, None] * inv], axis=-1)
    cos, sin = jnp.cos(ang), jnp.sin(ang)
    return jnp.concatenate([cos, cos], axis=-1), jnp.concatenate([-sin, sin], axis=-1)


def _layer_fwd(xin, p, w, more_weights, cos2, sin2):
    sv = {"xin": xin}
    h = sv["h"] = _rms_fwd(xin, p["g_mix"], name="rms_mix")
    proj = functools.partial(_mm, h, w["in"], "nt", bf16)
    q_raw = sv["q_raw"] = proj(n=Q_COLS, b_off=0, name="proj_q")
    kv_raw = sv["kv_raw"] = proj(n=2 * KV_COLS, b_off=OFF_KV, name="proj_kv")
    conv_in = sv["conv_in"] = proj(n=2 * CONV_CH, b_off=OFF_CONV, name="proj_conv")
    sg_in = sv["sg_in"] = proj(n=2 * SG_CH, b_off=OFF_SG, name="proj_sg")
    gl = sv["gl"] = proj(n=3 * D_MODEL, b_off=OFF_GATE, name="proj_gate")
    qr, kr = sv["qr"], sv["kr"] = _qk_fwd(q_raw, kv_raw, p["q_norm_g"], p["k_norm_g"], cos2, sin2, name="qk_fwd")
    o = sv["o"] = _attn_fwd(qr, kr, kv_raw, name="attn_fwd")
    c = sv["c"] = _conv1_fwd(conv_in, w["dw"], p["b_dw"], name="conv1_fwd")
    cz = sv["cz"] = _conv2_fwd(c, p["conv_ln_g"], p["conv_ln_b"], name="conv2_fwd")
    sz = sv["sz"] = _sgu_fwd(sg_in, p["sg_ln_g"], p["sg_ln_b"], p["w_s"], p["b_s"], name="sgu_fwd")
    w = {**w, **more_weights(1, sz)}
    sv["ya"], sv["yc"], sv["ys"], merged = _mixer_out([o, cz, sz], [w["attn_o"], w["conv_o"], w["sg_o"]], gl, p["b_gate"],
                                                      name="mixer_out")
    sv["merged"] = merged
    x1 = sv["x1"] = _mm(merged, w["out"], "nn", f32, res=xin, name="out_proj")
    w = {**w, **more_weights(2, x1)}
    hf = sv["hf"] = _rms_fwd(x1, p["g_ffn"], name="rms_ffn")
    sv["fg"], sv["fu"], act = _ffn_up(hf, w["ff_gate"], w["ff_up"], name="ffn_up")
    sv["act"] = act
    x2 = _mm(act, w["ff_down"], "nn", f32, res=x1, name="ff_down")
    return x2, sv, w


def _layer_bwd(dx2, dx2b, sv, p, w, cos2, sin2, reduce_begin, reduce_continue, last, rider_for):
    small = {}

    def grad_mm(weight, a, b, name):
        fresh = b if a is sv["act"] else a
        ride = rider_for(weight, a.shape[1] // _pick(a.shape[1], (512, 256, 128)), fresh)
        if ride is None:
            return _mm(a, b, "tn", bf16, name=name)
        g, results = _mm(a, b, "tn", bf16, rider=ride[0], name=name + "_ridden")
        ride[1](results)
        return g

    dfg, dfu = _ffn_down_bwd(dx2b, w["ff_down"], sv["fg"], sv["fu"], name="ffn_down_bwd")
    g_down = grad_mm("w_ff_down", sv["act"], dx2b, name="g_ff_down")
    dhf = _mm(dfg, w["ff_gate"], "nn", f32, name="d_hf_gate")
    dhf = _mm(dfu, w["ff_up"], "nn", f32, res=dhf, name="d_hf_up")
    g_gate = grad_mm("w_ff_gate", dfg, sv["hf"], name="g_ff_gate")
    g_up = grad_mm("w_ff_up", dfu, sv["hf"], name="g_ff_up")
    zero = reduce_begin("ffn", dict(w_ff_gate=g_gate, w_ff_up=g_up, w_ff_down=g_down))[0, 0]
    dx1, dx1b, small["g_ffn"] = _rms_bwd(sv["x1"], p["g_ffn"] + zero, dhf, dx2, name="rms_ffn_bwd")
    g_out = _mm(sv["merged"], dx1b, "tn", bf16, name="g_out")
    *dgl, dya, dyc, dys, db0, db1, db2 = _merge_bwd_fused(dx1b, w["out"], sv["gl"], p["b_gate"], sv["ya"], sv["yc"], sv["ys"],
                                                        name="merge_bwd")
    small["b_gate"] = jnp.concatenate([db0, db1, db2], axis=1)
    do = _mm(dya, w["attn_o"], "nn", bf16, after=reduce_continue("ffn", dya), name="d_o")
    g_ao = _mm(dya, sv["o"], "tn", bf16, name="g_attn_o")
    dcz = _mm(dyc, w["conv_o"], "nn", bf16, name="d_cz")
    g_co = _mm(dyc, sv["cz"], "tn", bf16, name="g_conv_o")
    dsz = _mm(dys, w["sg_o"], "nn", bf16, name="d_sz")
    g_so = _mm(dys, sv["sz"], "tn", bf16, name="g_sg_o")
    zero = reduce_begin("mix", dict(w_attn_o=g_ao, w_conv_o=g_co, w_sg_o=g_so, w_out=g_out))[0, 0]
    dsu, dsv, small["w_s"], small["b_s"], small["sg_ln_g"], small["sg_ln_b"] = _sgu_bwd(
        sv["sg_in"], dsz, p["sg_ln_g"] + zero, p["sg_ln_b"], p["w_s"], p["w_s_t"], p["b_s"], name="sgu_bwd")
    dc, small["conv_ln_g"], small["conv_ln_b"] = _conv2_bwd(sv["c"], dcz, p["conv_ln_g"], p["conv_ln_b"], name="conv2_bwd")
    da, dgt, small["w_dw"], small["b_dw"] = _conv1_bwd(sv["conv_in"], dc, w["dw"], name="conv1_bwd")
    zero = reduce_continue("mix", da)[0, 0]
    dqr, dkr, dv = _attn_bwd(sv["qr"], sv["kr"], sv["kv_raw"], do, name="attn_bwd")
    dq_raw, dk_raw, small["q_norm_g"], small["k_norm_g"] = _qk_bwd(
        sv["q_raw"], sv["kv_raw"], dqr, dkr, p["q_norm_g"] + zero, p["k_norm_g"], cos2, sin2, name="qk_bwd")
    dproj = jnp.concatenate([dq_raw, dk_raw, dv.astype(bf16), da, dgt, dsu, dsv, *dgl], axis=1)
    g_in = grad_mm("w_in", dproj, sv["h"], name="g_in")
    begun = reduce_begin("in", dict(w_in=g_in))
    if last:
        begun = reduce_continue("in", begun)
    dh = _mm(dproj, w["in"], "nn", f32, after=begun, name="d_h")
    zero = begun[0, 0] if last else reduce_continue("in", dh)[0, 0]
    dx, dxb, small["g_mix"] = _rms_bwd(sv["xin"], p["g_mix"] + zero, dh, dx1, name="rms_mix_bwd")
    return dx, dxb, small


SMALL = ("g_mix", "b_gate", "q_norm_g", "k_norm_g", "b_dw", "conv_ln_g", "conv_ln_b", "sg_ln_g", "sg_ln_b",
         "w_s", "b_s", "g_ffn")
PACK_ALIGN = 8 * LANES


def _pack(parts):
    flat = jnp.concatenate([a.reshape(-1).astype(f32) for a in parts])
    pad = -flat.shape[0] % PACK_ALIGN
    return jnp.pad(flat, (0, pad)).reshape(-1, LANES)


def _unpack(buf, shapes):
    flat = buf.reshape(-1)
    out, pos = [], 0
    for shp in shapes:
        size = math.prod(shp)
        out.append(flat[pos:pos + size].reshape(shp))
        pos += size
    return out


def kernel(x, g_mix, w_in, b_gate, q_norm_g, k_norm_g, w_attn_o, w_dw, b_dw, conv_ln_g, conv_ln_b, w_conv_o, sg_ln_g, sg_ln_b, w_s, b_s, w_sg_o, w_out, g_ffn, w_ff_gate, w_ff_up, w_ff_down, g_final, loss_target, m_g_mix, m_w_in, m_b_gate, m_q_norm_g, m_k_norm_g, m_w_attn_o, m_w_dw, m_b_dw, m_conv_ln_g, m_conv_ln_b, m_w_conv_o, m_sg_ln_g, m_sg_ln_b, m_w_s, m_b_s, m_w_sg_o, m_w_out, m_g_ffn, m_w_ff_gate, m_w_ff_up, m_w_ff_down, m_g_final, v_g_mix, v_w_in, v_b_gate, v_q_norm_g, v_k_norm_g, v_w_attn_o, v_w_dw, v_b_dw, v_conv_ln_g, v_conv_ln_b, v_w_conv_o, v_sg_ln_g, v_sg_ln_b, v_w_s, v_b_s, v_w_sg_o, v_w_out, v_g_ffn, v_w_ff_gate, v_w_ff_up, v_w_ff_down, v_g_final):
    weights = dict(g_mix=g_mix, w_in=w_in, b_gate=b_gate, q_norm_g=q_norm_g, k_norm_g=k_norm_g, w_attn_o=w_attn_o,
                   w_dw=w_dw, b_dw=b_dw, conv_ln_g=conv_ln_g, conv_ln_b=conv_ln_b, w_conv_o=w_conv_o, sg_ln_g=sg_ln_g,
                   sg_ln_b=sg_ln_b, w_s=w_s, b_s=b_s, w_sg_o=w_sg_o, w_out=w_out, g_ffn=g_ffn, w_ff_gate=w_ff_gate,
                   w_ff_up=w_ff_up, w_ff_down=w_ff_down, g_final=g_final)
    mom_m = dict(g_mix=m_g_mix, w_in=m_w_in, b_gate=m_b_gate, q_norm_g=m_q_norm_g, k_norm_g=m_k_norm_g,
                 w_attn_o=m_w_attn_o, w_dw=m_w_dw, b_dw=m_b_dw, conv_ln_g=m_conv_ln_g, conv_ln_b=m_conv_ln_b,
                 w_conv_o=m_w_conv_o, sg_ln_g=m_sg_ln_g, sg_ln_b=m_sg_ln_b, w_s=m_w_s, b_s=m_b_s, w_sg_o=m_w_sg_o,
                 w_out=m_w_out, g_ffn=m_g_ffn, w_ff_gate=m_w_ff_gate, w_ff_up=m_w_ff_up, w_ff_down=m_w_ff_down,
                 g_final=m_g_final)
    mom_v = dict(g_mix=v_g_mix, w_in=v_w_in, b_gate=v_b_gate, q_norm_g=v_q_norm_g, k_norm_g=v_k_norm_g,
                 w_attn_o=v_w_attn_o, w_dw=v_w_dw, b_dw=v_b_dw, conv_ln_g=v_conv_ln_g, conv_ln_b=v_conv_ln_b,
                 w_conv_o=v_w_conv_o, sg_ln_g=v_sg_ln_g, sg_ln_b=v_sg_ln_b, w_s=v_w_s, b_s=v_b_s, w_sg_o=v_w_sg_o,
                 w_out=v_w_out, g_ffn=v_g_ffn, w_ff_gate=v_w_ff_gate, w_ff_up=v_w_ff_up, w_ff_down=v_w_ff_down,
                 g_final=v_g_final)
    S, D = x.shape[1], x.shape[2]
    xi, yi, ci = _mesh_pos()
    me = 4 * xi + 2 * yi + ci
    core = jnp.reshape(ci, (1,)).astype(jnp.int32)
    cos2, sin2 = _rope_tables(S)

    big = ("w_in", "w_attn_o", "w_conv_o", "w_sg_o", "w_out", "w_ff_gate", "w_ff_up", "w_ff_down")
    transposed = {"w_in", "w_attn_o", "w_conv_o", "w_sg_o", "w_ff_gate", "w_ff_up"}
    chip = jnp.reshape(2 * xi + yi, (1,)).astype(jnp.int32)
    groups = (("in", "dw"), ("attn_o", "conv_o", "sg_o", "out"), ("ff_gate", "ff_up", "ff_down"))
    P, shards = [], []
    for l in range(DEPTH):
        sh = {n[2:]: (weights[n][l].T if n in transposed else weights[n][l]).astype(bf16) for n in big}
        sh["dw"] = jnp.pad(w_dw[l].reshape(CONV_W, LANES), ((0, CONV_WP - CONV_W), (0, 0)))
        shards.append(sh)
        p = {n: weights[n][l].reshape(1, -1) for n in SMALL if n not in ("w_s", "b_s")}
        p["w_s"] = w_s[l]
        p["w_s_t"] = jnp.swapaxes(w_s[l], 1, 2)
        p["b_s"] = b_s[l].reshape(SG_G, SG_CHUNK, 1)
        P.append(p)

    gathers = {}

    def start_gather(l, gi, after):
        srcs = [shards[l][n] for n in groups[gi]]
        lands = [lax.dynamic_update_index_in_dim(lax.empty((N_DEV,) + s.shape, s.dtype), s, me, 0) for s in srcs]
        gathers[l, gi] = _ici_start("gather", srcs, lands, after, name=f"ag_start_{l}{gi}")
        return gathers[l, gi][-1]

    def gathered(l, gi, after):
        srcs, lands = _ici_wait("gather", gathers[l, gi], after, name=f"ag_wait_{l}{gi}")
        after = srcs[0]
        if gi == len(groups) - 1 and l + 1 < DEPTH:
            for gj in range(len(groups)):
                after = start_gather(l + 1, gj, after)
        full = _d2d_gather([b.reshape(N_CHIP, 2, *b.shape[1:]) for b in lands], after, name=f"ag_d2d_{gi}")
        return {n: f.reshape(-1, f.shape[3]) for n, f in zip(groups[gi], full)}

    all_started = cos2
    for gi in range(len(groups)):
        all_started = start_gather(0, gi, all_started)

    h = x.reshape(S, D)
    saved, W = [], []
    for l in range(DEPTH):
        first = gathered(l, 0, all_started if l == 0 else h)
        if l == 0:
            P[l]["g_mix"] = P[l]["g_mix"] + all_started[0, 0]
        h, sv, w = _layer_fwd(h, P[l], first, functools.partial(lambda gi, z, l: gathered(l, gi, z), l=l), cos2, sin2)
        saved.append(sv)
        W.append(w)
    dx, dxb, sq, g_final_part = _final_loss(h, g_final.reshape(1, D), loss_target.reshape(S, D), name="final_loss")
    loss = lax.psum(0.5 * jnp.sum(sq) / D, ("x", "y", "c"))

    begun, reductions, small_grads = {}, {}, [None] * DEPTH
    swap = lambda a: jnp.swapaxes(a, 1, 2)
    arrived, ridden = {}, {}

    def arrive(l, group, after):
        if (l, group) not in arrived:
            arrived[l, group] = _ici_wait("reduce", reductions[l, group][1], after, name=f"rs_wait_{group}{l}")
        return arrived[l, group]

    def rider_for(weight, steps, after, l):
        if l + 1 >= DEPTH or weight == "w_in":
            return None
        group = next(g for (ll, g), (names, _) in reductions.items() if ll == l + 1 and weight in names)
        sums, lands = arrive(l + 1, group, after)
        i = reductions[l + 1, group][0].index(weight)
        as_arrives = swap if weight in transposed else (lambda a: a)
        ride = _adamw_rider(l + 1, as_arrives(weights[weight]), as_arrives(mom_m[weight]), as_arrives(mom_v[weight]),
                            lands[i], sums[i], chip, steps)
        return ride, functools.partial(ridden.__setitem__, weight)

    for l in reversed(range(DEPTH)):
        def reduce_begin(group, grads, l=l):
            begun[l, group] = (tuple(grads), _reduce_begin(list(grads.values()), core, tag=f"{group}{l}"))
            return begun[l, group][1][-1]

        def reduce_continue(group, after, l=l):
            names, started = begun[l, group]
            reductions[l, group] = (names, _reduce_continue(started, core, after, tag=f"{group}{l}"))
            return reductions[l, group][1][-1]

        dx, dxb, small_grads[l] = _layer_bwd(dx, dxb, saved[l], P[l], W[l], cos2, sin2, reduce_begin, reduce_continue,
                                            last=(l == 0), rider_for=functools.partial(rider_for, l=l))
    grad_x = dx.reshape(x.shape)

    small_shapes = [weights[n].shape for n in SMALL] + [g_final.shape, (DEPTH, CONV_CH // LANES, CONV_WP, LANES)]
    parts = [jnp.stack([small_grads[l][n].reshape(weights[n].shape[1:]) for l in range(DEPTH)]) for n in SMALL]
    parts += [g_final_part.reshape(g_final.shape), jnp.stack([small_grads[l]["w_dw"] for l in range(DEPTH)])]
    packed = _pack(parts)
    packed_land = lax.dynamic_update_index_in_dim(lax.empty((N_DEV,) + packed.shape, f32), packed, me, 0)
    small_started = _ici_start("gather", [packed], [packed_land], dx, name="gather_small_start")

    grads_out, delta, new_m, new_v = {}, {}, {}, {}
    swap = lambda a: jnp.swapaxes(a, 1, 2)

    def update(n, lands, sums):
        as_arrives = n not in transposed or weights[n].shape[2] % LANES != 0
        if as_arrives:
            to_arrival = swap if n in transposed else (lambda a: a)
            out = ridden.get(n)
            for l in reversed(range(DEPTH if out is None else DEPTH - 1)):
                out = _adamw_reduced(l, to_arrival(weights[n]), to_arrival(mom_m[n]), to_arrival(mom_v[n]),
                                     lands[l], sums[l], chip, out, name=f"adamw_{n}_{l}")
            grads_out[n], delta[n], new_m[n], new_v[n] = [to_arrival(o) for o in out]
            return out[1]
        g = jnp.stack([_sum_chip_slots(lands[l], sums[l], chip, name="rs_sum_" + n) for l in range(DEPTH)])
        grads_out[n] = swap(g)
        delta[n], new_m[n], new_v[n] = _adamw(weights[n], grads_out[n], mom_m[n], mom_v[n], name="adamw_" + n)
        return delta[n]

    after = small_started[-1]
    for group in ("ffn", "mix", "in"):
        names = reductions[0, group][0]
        here = [arrive(l, group, after) for l in range(DEPTH)]
        for i, n in enumerate(names):
            after = update(n, [here[l][1][i] for l in range(DEPTH)], [here[l][0][i] for l in range(DEPTH)])

    _, small_lands = _ici_wait("gather", small_started, after, name="gather_small_wait")
    small_full = _d2d_gather([small_lands[0].reshape(N_CHIP, 2, *packed.shape)], after, name="gather_small_d2d")[0]
    total = _sum_slots(small_full.reshape(N_DEV, *packed.shape), name="sum_small")
    small_total = _unpack(total, small_shapes)
    grads_out.update(zip(SMALL + ("g_final",), small_total[:-1]))
    dw_full = small_total[-1]
    grads_out["w_dw"] = lax.dynamic_index_in_dim(dw_full, me, axis=1, keepdims=False)[:, :CONV_W].reshape(w_dw.shape)

    rep = tuple(n for n in SMALL if n != "w_s") + ("g_final",)
    rep_shapes = [weights[n].shape for n in rep]
    packs = [_pack([src[n] for n in rep])[None] for src in (weights, grads_out, mom_m, mom_v)]
    for dst, buf in zip((delta, new_m, new_v), _adamw(*packs, name="adamw_small")):
        dst.update(zip(rep, _unpack(buf[0], rep_shapes)))
    for n, shp in (("w_dw", (1, DEPTH * CONV_W, LANES)), ("w_s", (DEPTH, SG_G * SG_CHUNK, SG_CHUNK))):
        upd = _adamw(*[src[n].reshape(shp) for src in (weights, grads_out, mom_m, mom_v)], name="adamw_" + n)
        for dst, buf in zip((delta, new_m, new_v), upd):
            dst[n] = buf.reshape(weights[n].shape)

    order = ("g_mix", "w_in", "b_gate", "q_norm_g", "k_norm_g", "w_attn_o", "w_dw", "b_dw", "conv_ln_g", "conv_ln_b",
             "w_conv_o", "sg_ln_g", "sg_ln_b", "w_s", "b_s", "w_sg_o", "w_out", "g_ffn", "w_ff_gate", "w_ff_up",
             "w_ff_down", "g_final")
    return (loss, grad_x, *[grads_out[n] for n in order], *[delta[n] for n in order],
            *[new_m[n] for n in order], *[new_v[n] for n in order])
```

```python
import functools
import math

import jax
import jax.numpy as jnp
from jax import lax
from jax.experimental import pallas as pl
from jax.experimental.pallas import tpu as pltpu

f32, bf16 = jnp.float32, jnp.bfloat16

D_MODEL = 2048
SEQ = 2048
DEPTH = 2
GRID_W = 64
HEAD_DIM = 128
LANES = 128
N_Q = (D_MODEL // 2) // HEAD_DIM
N_KV = N_Q // 4
GRP = N_Q // N_KV
Q_COLS = N_Q * HEAD_DIM
KV_COLS = N_KV * HEAD_DIM
CONV_CH = D_MODEL // 2
CONV_W = 31
CONV_PAD = CONV_W // 2
CONV_WP = 32
SG_CH = D_MODEL // 2
SG_G = SG_CH // LANES
SG_CHUNK = 128
D_FF = -(-8 * D_MODEL // (3 * 256)) * 256
OFF_KV = Q_COLS
OFF_CONV = OFF_KV + 2 * KV_COLS
OFF_SG = OFF_CONV + 2 * CONV_CH
OFF_GATE = OFF_SG + 2 * SG_CH
IN_COLS = OFF_GATE + 3 * D_MODEL
ROPE_THETA = 10000.0
SCALE = HEAD_DIM ** -0.5
N_DEV = 8
N_CHIP = 4

ADAM_LR, ADAM_B1, ADAM_B2, ADAM_EPS, ADAM_WD, ADAM_STEP = 0.001, 0.9, 0.999, 1e-08, 0.01, 10

VMEM_BYTES_V7X = 64 << 20
VMEM_CAP = VMEM_BYTES_V7X - (6 << 20)
MESH = pl.DeviceIdType.MESH
HBM = pl.BlockSpec(memory_space=pltpu.HBM)


def _in_hbm(a):
    if isinstance(a, jax.Array) and jnp.issubdtype(a.dtype, jnp.floating) and a.size * a.dtype.itemsize >= (1 << 20):
        return pltpu.with_memory_space_constraint(a, pltpu.HBM)
    return a


def _out_hbm(s):
    if isinstance(s, jax.ShapeDtypeStruct) and math.prod(s.shape) * jnp.dtype(s.dtype).itemsize >= (1 << 20):
        return pltpu.HBM(s.shape, s.dtype)
    return s


def _call(body, **kw):
    shapes = kw.pop("out_shape")
    shapes = type(shapes)(_out_hbm(s) for s in shapes) if isinstance(shapes, (list, tuple)) else _out_hbm(shapes)
    call = pl.pallas_call(body, out_shape=shapes, **kw)
    return lambda *args: call(*[_in_hbm(a) for a in args])


def _pick(n, cands):
    for c in cands:
        if n % c == 0:
            return c
    raise ValueError((n, cands))


def _params(sem, vmem_bytes):
    return pltpu.CompilerParams(dimension_semantics=sem, vmem_limit_bytes=int(min(max(vmem_bytes, 16 << 20), VMEM_CAP)))


def _mm(a, b, form, out_dtype, *, n=None, b_off=0, res=None, after=None, rider=None, name):
    if form == "tn":
        K, M = a.shape
    else:
        M, K = a.shape
    N = n if n is not None else (b.shape[0] if form == "nt" else b.shape[1])
    if K <= 2048:
        tk = K
        if form == "tn":
            tm = _pick(M, (512, 256, 128))
            tn = N if N <= 2048 else _pick(N, (1024, 512, 256, 128))
        else:
            tm = M if M <= 2048 else _pick(M, (2048, 1024, 512))
            tn = _pick(math.gcd(N, b_off) if b_off else N, (256, 128) if res is not None else (512, 256, 128))
    else:
        tk = max(t for t in range(LANES, 3072 + 1, LANES) if K % t == 0)
        tm = _pick(M, (1024, 512, 256, 128))
        tn = _pick(math.gcd(N, b_off) if b_off else N, (1024, 512, 256, 128))
    assert b_off % tn == 0
    off = b_off // tn
    nk = K // tk
    if form == "tn":
        a_spec = pl.BlockSpec((tk, tm), lambda i, j, k: (k, i))
    else:
        a_spec = pl.BlockSpec((tm, tk), lambda i, j, k: (i, k))
    if form == "nt":
        b_spec = pl.BlockSpec((tn, tk), lambda i, j, k: (j + off, k))
    else:
        b_spec = pl.BlockSpec((tk, tn), lambda i, j, k: (k, j + off))
    dims = {"nn": ((1,), (0,)), "nt": ((1,), (1,)), "tn": ((0,), (0,))}[form]
    has_res = res is not None

    n_ride = len(rider["args"]) if rider else 0

    def body(*refs):
        if after is not None:
            refs = refs[1:]
        n_in = 3 if has_res else 2
        if rider:
            ride_refs = refs[n_in:n_in + n_ride] + refs[n_in + n_ride + 1:n_in + n_ride + 1 + len(rider["out_specs"])]

            @pl.when(jnp.logical_and(pl.program_id(1) == 0, pl.program_id(2) == 0))
            def _():
                rider["body"](*ride_refs)

            refs = refs[:n_in] + (refs[n_in + n_ride],) + refs[n_in + n_ride + 1 + len(rider["out_specs"]):]
        if has_res:
            a_ref, b_ref, r_ref, o_ref = refs[:4]
        else:
            a_ref, b_ref, o_ref = refs[:3]
        p = lax.dot_general(a_ref[...], b_ref[...], (dims, ((), ())), preferred_element_type=f32)

        def finish(acc):
            if has_res:
                acc = acc + r_ref[...].astype(f32)
            o_ref[...] = acc.astype(o_ref.dtype)

        if nk == 1:
            finish(p)
        else:
            acc_ref = refs[-1]
            k = pl.program_id(2)

            @pl.when(k == 0)
            def _():
                acc_ref[...] = p

            @pl.when(k > 0)
            def _():
                acc_ref[...] += p

            @pl.when(k == nk - 1)
            def _():
                finish(acc_ref[...])

    in_specs = [a_spec, b_spec]
    args = [a, b]
    osz = jnp.dtype(out_dtype).itemsize
    vmem = 2 * (tm * tk * 2 + tk * tn * 2 + tm * tn * osz) + 2 * tm * tn * 4
    if has_res:
        in_specs.append(pl.BlockSpec((tm, tn), lambda i, j, k: (i, j)))
        args.append(res)
        vmem += 2 * tm * tn * res.dtype.itemsize
    scratch = []
    if nk > 1:
        scratch.append(pltpu.VMEM((tm, tn), f32))
        vmem += tm * tn * 4
    out_specs = [pl.BlockSpec((tm, tn), lambda i, j, k: (i, j))]
    out_shape = [jax.ShapeDtypeStruct((M, N), out_dtype)]
    if rider:
        in_specs += rider["in_specs"]
        args += rider["args"]
        out_specs += rider["out_specs"]
        out_shape += rider["out_shape"]
        vmem += rider["vmem"]
    if after is not None:
        in_specs.insert(0, pl.BlockSpec(memory_space=pl.ANY))
        args.insert(0, after)
    out = _call(
        body, name=name, grid=(M // tm, N // tn, nk), in_specs=in_specs, out_specs=out_specs, out_shape=out_shape,
        scratch_shapes=scratch,
        compiler_params=_params(("arbitrary" if rider else "parallel", "parallel", "arbitrary"), vmem + (8 << 20)),
    )(*args)
    return (out[0], out[1:]) if rider else out[0]


EPI_TN = 256


def _mm_epi(a, bs, form, extras, out_dtypes, n_sums, fn, name, tn=EPI_TN):
    a_list = list(a) if isinstance(a, (list, tuple)) else [a]
    M, K = a_list[0].shape
    N = bs[0].shape[0] if form == "nt" else bs[0].shape[1]
    assert K <= 2048 and N % tn == 0 and len(a_list) in (1, len(bs))
    dims = ((1,), (1,)) if form == "nt" else ((1,), (0,))
    na, nb, ne = len(a_list), len(bs), len(extras)

    def body(*refs):
        a_refs, b_refs = refs[:na], refs[na:na + nb]
        e_refs, o_refs = refs[na + nb:na + nb + ne], refs[na + nb + ne:]
        avs = [r[...] for r in a_refs] * (nb // na)
        ps = [lax.dot_general(av, b[...], (dims, ((), ())), preferred_element_type=f32) for av, b in zip(avs, b_refs)]
        for o_ref, o in zip(o_refs, fn(ps, [e[...] for e in e_refs])):
            o_ref[...] = o.astype(o_ref.dtype)

    in_specs = [pl.BlockSpec((M, K), lambda j: (0, 0), pipeline_mode=pl.Buffered(1)) for _ in a_list]
    in_specs += [pl.BlockSpec((tn, K), lambda j: (j, 0)) if form == "nt" else pl.BlockSpec((K, tn), lambda j: (0, j))
                 for _ in bs]
    for arr, first in extras:
        assert first % tn == 0
        in_specs.append(pl.BlockSpec((arr.shape[0], tn), functools.partial(lambda j, o: (0, j + o), o=first // tn)))
    out_specs = [pl.BlockSpec((M, tn), lambda j: (0, j))] * len(out_dtypes) + [pl.BlockSpec((1, tn), lambda j: (0, j))] * n_sums
    out_shape = [jax.ShapeDtypeStruct((M, N), dt) for dt in out_dtypes] + [jax.ShapeDtypeStruct((1, N), f32)] * n_sums
    tiles = sum(arr.shape[0] * tn * arr.dtype.itemsize for arr, _ in extras) + sum(M * tn * jnp.dtype(dt).itemsize for dt in out_dtypes)
    vmem = na * M * K * 2 + 2 * nb * tn * K * 2 + 2 * tiles + (nb + 6) * M * tn * 4
    return _call(body, name=name, grid=(N // tn,), in_specs=in_specs, out_specs=out_specs, out_shape=out_shape,
                 compiler_params=_params(("parallel",), vmem + (8 << 20)))(*a_list, *bs, *[arr for arr, _ in extras])


def _ffn_up(hf, wt_gate, wt_up, name):
    def fn(ps, _):
        g, u = ps[0].astype(bf16), ps[1].astype(bf16)
        gf = g.astype(f32)
        return g, u, gf * jax.nn.sigmoid(gf) * u.astype(f32)

    return _mm_epi(hf, [wt_gate, wt_up], "nt", [], [bf16] * 3, 0, fn, name, tn=2 * EPI_TN)


def _ffn_down_bwd(dx2b, w_down, fg, fu, name):
    def fn(ps, es):
        d, g = ps[0], es[0].astype(f32)
        sg = jax.nn.sigmoid(g)
        return d * es[1].astype(f32) * sg * (1.0 + g * (1.0 - sg)), d * g * sg

    return _mm_epi(dx2b, [w_down], "nt", [(fg, 0), (fu, 0)], [bf16] * 2, 0, fn, name, tn=2 * EPI_TN)


def _mixer_out(branches, wts, gl, b_gate, name):
    D = wts[0].shape[0]

    def fn(ps, es):
        ys = [p_.astype(bf16) for p_ in ps]
        merged = None
        for i in range(3):
            term = jax.nn.sigmoid(es[i].astype(f32) + es[3 + i]) * ys[i].astype(f32)
            merged = term if merged is None else merged + term
        return ys + [merged]

    extras = [(gl, i * D) for i in range(3)] + [(b_gate, i * D) for i in range(3)]
    return _mm_epi(branches, wts, "nt", extras, [bf16] * 4, 0, fn, name)


def _merge_bwd_fused(dx1b, w_out, gl, b_gate, ya, yc, ys, name):
    D = ya.shape[1]

    def fn(ps, es):
        dm_, outs, sums = ps[0], [], []
        for i in range(3):
            gate = jax.nn.sigmoid(es[i].astype(f32) + es[3 + i])
            dlog = dm_ * es[6 + i].astype(f32) * gate * (1.0 - gate)
            outs.append((dlog, dm_ * gate))
            sums.append(jnp.sum(dlog, axis=0, keepdims=True))
        return [o[0] for o in outs] + [o[1] for o in outs] + sums

    extras = [(gl, i * D) for i in range(3)] + [(b_gate, i * D) for i in range(3)] + [(ya, 0), (yc, 0), (ys, 0)]
    return _mm_epi(dx1b, [w_out], "nt", extras, [bf16] * 6, 3, fn, name)


def _rows(body, ins, outs, *, tm, name, vmem=40 << 20):
    nrows = next(s[1].shape[0] for s in ins if s[0] == "r")
    in_specs, args = [], []
    for s in ins:
        arr = s[1]
        if s[0] == "r":
            w = s[2] if len(s) > 2 else arr.shape[1]
            cb = s[3] if len(s) > 3 else 0
            in_specs.append(pl.BlockSpec((tm, w), functools.partial(lambda i, cb: (i, cb), cb=cb)))
        else:
            in_specs.append(pl.BlockSpec(arr.shape, functools.partial(lambda i, nd: (0,) * nd, nd=arr.ndim)))
        args.append(arr)
    out_specs, out_shape = [], []
    for s in outs:
        if s[0] == "r":
            out_specs.append(pl.BlockSpec((tm, s[1]), lambda i: (i, 0)))
            out_shape.append(jax.ShapeDtypeStruct((nrows, s[1]), s[2]))
        else:
            out_specs.append(pl.BlockSpec(s[1], functools.partial(lambda i, nd: (0,) * nd, nd=len(s[1]))))
            out_shape.append(jax.ShapeDtypeStruct(s[1], s[2]))
    return _call(body, name=name, grid=(nrows // tm,), in_specs=in_specs, out_specs=out_specs,
                 out_shape=out_shape, compiler_params=_params(("arbitrary",), vmem))(*args)


def _accumulate(ref, part):
    i = pl.program_id(0)

    @pl.when(i == 0)
    def _():
        ref[...] = part

    @pl.when(i > 0)
    def _():
        ref[...] += part


def _rms_stats(x):
    r = lax.rsqrt(jnp.mean(x * x, axis=-1, keepdims=True) + 1e-6)
    return r, x * r


def _rms_fwd(x, g, name):
    def body(x_ref, g_ref, o_ref):
        _, xn = _rms_stats(x_ref[...])
        o_ref[...] = (xn * g_ref[...]).astype(o_ref.dtype)

    return _rows(body, [("r", x), ("f", g)], [("r", x.shape[1], bf16)], tm=min(256, x.shape[0]), name=name)[0]


def _rms_bwd(x, g, dh, dres, name):
    D = x.shape[1]

    def body(x_ref, g_ref, dh_ref, dr_ref, dx_ref, dxb_ref, dg_ref):
        r, xn = _rms_stats(x_ref[...])
        dy = dh_ref[...].astype(f32)
        dxn = dy * g_ref[...]
        dx = dr_ref[...] + r * (dxn - xn * jnp.mean(dxn * xn, axis=-1, keepdims=True))
        dx_ref[...] = dx
        dxb_ref[...] = dx.astype(bf16)
        _accumulate(dg_ref, jnp.sum(dy * xn, axis=0, keepdims=True))

    return _rows(body, [("r", x), ("f", g), ("r", dh), ("r", dres)],
                 [("r", D, f32), ("r", D, bf16), ("a", (1, D), f32)], tm=min(256, x.shape[0]), name=name)


def _final_loss(x, g, tgt, name):
    D = x.shape[1]

    def body(x_ref, g_ref, t_ref, dx_ref, dxb_ref, sq_ref, dg_ref):
        r, xn = _rms_stats(x_ref[...])
        gain = g_ref[...]
        diff = xn * gain - t_ref[...]
        dy = diff * (1.0 / D)
        dxn = dy * gain
        dx = r * (dxn - xn * jnp.mean(dxn * xn, axis=-1, keepdims=True))
        dx_ref[...] = dx
        dxb_ref[...] = dx.astype(bf16)
        _accumulate(sq_ref, jnp.sum(diff * diff, axis=0, keepdims=True))
        _accumulate(dg_ref, jnp.sum(dy * xn, axis=0, keepdims=True))

    return _rows(body, [("r", x), ("f", g), ("r", tgt)],
                 [("r", D, f32), ("r", D, bf16), ("a", (1, D), f32), ("a", (1, D), f32)],
                 tm=min(256, x.shape[0]), name=name)


def _qk_fwd(q_raw, kv_raw, qg, kg, cos2, sin2, name):
    def body(q_ref, k_ref, qg_ref, kg_ref, c_ref, s_ref, qo_ref, ko_ref):
        c, s = c_ref[...], s_ref[...]

        def head(src, gain, dst, h):
            cols = slice(h * HEAD_DIM, (h + 1) * HEAD_DIM)
            _, xn = _rms_stats(src[:, cols].astype(f32))
            y = xn * gain
            dst[:, cols] = (y * c + pltpu.roll(y, HEAD_DIM // 2, 1) * s).astype(dst.dtype)

        for h in range(N_Q):
            head(q_ref, qg_ref[...], qo_ref, h)
        for h in range(N_KV):
            head(k_ref, kg_ref[...], ko_ref, h)

    return _rows(body, [("r", q_raw), ("r", kv_raw, KV_COLS, 0), ("f", qg), ("f", kg), ("r", cos2), ("r", sin2)],
                 [("r", Q_COLS, bf16), ("r", KV_COLS, bf16)], tm=min(256, q_raw.shape[0]), name=name)


def _qk_bwd(q_raw, kv_raw, dqr, dkr, qg, kg, cos2, sin2, name):
    def body(q_ref, k_ref, dq_ref, dk_ref, qg_ref, kg_ref, c_ref, s_ref, dqo_ref, dko_ref, dqg_ref, dkg_ref):
        c, s = c_ref[...], s_ref[...]

        def head(src, dsrc, gain, dst, h):
            cols = slice(h * HEAD_DIM, (h + 1) * HEAD_DIM)
            r, xn = _rms_stats(src[:, cols].astype(f32))
            do = dsrc[:, cols].astype(f32)
            dy = do * c + pltpu.roll(do * s, HEAD_DIM // 2, 1)
            dxn = dy * gain
            dst[:, cols] = (r * (dxn - xn * jnp.mean(dxn * xn, axis=-1, keepdims=True))).astype(dst.dtype)
            return jnp.sum(dy * xn, axis=0, keepdims=True)

        dq_gain = head(q_ref, dq_ref, qg_ref[...], dqo_ref, 0)
        for h in range(1, N_Q):
            dq_gain = dq_gain + head(q_ref, dq_ref, qg_ref[...], dqo_ref, h)
        dk_gain = head(k_ref, dk_ref, kg_ref[...], dko_ref, 0)
        for h in range(1, N_KV):
            dk_gain = dk_gain + head(k_ref, dk_ref, kg_ref[...], dko_ref, h)
        _accumulate(dqg_ref, dq_gain)
        _accumulate(dkg_ref, dk_gain)

    return _rows(body, [("r", q_raw), ("r", kv_raw, KV_COLS, 0), ("r", dqr), ("r", dkr), ("f", qg), ("f", kg),
                        ("r", cos2), ("r", sin2)],
                 [("r", Q_COLS, bf16), ("r", KV_COLS, bf16), ("a", (1, HEAD_DIM), f32), ("a", (1, HEAD_DIM), f32)],
                 tm=min(256, q_raw.shape[0]), name=name)


def _softmax_rows(q, k):
    s = lax.dot_general(q, k, (((1,), (1,)), ((), ())), preferred_element_type=f32) * (SCALE * math.log2(math.e))
    p = jnp.exp2(s - jnp.max(s, axis=-1, keepdims=True))
    return p * (1.0 / jnp.sum(p, axis=-1, keepdims=True))


def _head_cols(g):
    return slice(g * HEAD_DIM, (g + 1) * HEAD_DIM)


def _attn_fwd(qr, kr, kv_raw, name):
    S = qr.shape[0]
    tq = min(256, S)

    def body(q_ref, k_ref, v_ref, o_ref):
        k, v = k_ref[...], v_ref[...]
        for g in range(GRP):
            p = _softmax_rows(q_ref[:, _head_cols(g)], k)
            o_ref[:, _head_cols(g)] = jnp.dot(p.astype(bf16), v, preferred_element_type=f32).astype(o_ref.dtype)

    return _call(
        body, name=name, grid=(N_KV, S // tq),
        in_specs=[pl.BlockSpec((tq, GRP * HEAD_DIM), lambda kv, i: (i, kv)),
                  pl.BlockSpec((S, HEAD_DIM), lambda kv, i: (0, kv)),
                  pl.BlockSpec((S, HEAD_DIM), lambda kv, i: (0, N_KV + kv))],
        out_specs=pl.BlockSpec((tq, GRP * HEAD_DIM), lambda kv, i: (i, kv)),
        out_shape=jax.ShapeDtypeStruct((S, Q_COLS), bf16),
        compiler_params=_params(("parallel", "arbitrary"), 4 * GRP * tq * S * 4 + (8 << 20)),
    )(qr, kr, kv_raw)


def _attn_bwd(qr, kr, kv_raw, do, name):
    S = qr.shape[0]
    tq = min(256, S)

    def body(q_ref, k_ref, v_ref, do_ref, dq_ref, dk_ref, dv_ref):
        first = pl.program_id(1) == 0
        k, v = k_ref[...], v_ref[...]
        dv_part = dk_part = None
        for g in range(GRP):
            q, do_ = q_ref[:, _head_cols(g)], do_ref[:, _head_cols(g)]
            p = _softmax_rows(q, k)
            dp = lax.dot_general(do_, v, (((1,), (1,)), ((), ())), preferred_element_type=f32)
            ds = (p * (dp - jnp.sum(dp * p, axis=-1, keepdims=True)) * SCALE).astype(bf16)
            dq_ref[:, _head_cols(g)] = jnp.dot(ds, k, preferred_element_type=f32).astype(dq_ref.dtype)
            dv_g = lax.dot_general(p.astype(bf16), do_, (((0,), (0,)), ((), ())), preferred_element_type=f32)
            dk_g = lax.dot_general(ds, q, (((0,), (0,)), ((), ())), preferred_element_type=f32)
            dv_part = dv_g if g == 0 else dv_part + dv_g
            dk_part = dk_g if g == 0 else dk_part + dk_g

        @pl.when(first)
        def _():
            dv_ref[...] = dv_part
            dk_ref[...] = dk_part

        @pl.when(jnp.logical_not(first))
        def _():
            dv_ref[...] += dv_part
            dk_ref[...] += dk_part

    qspec = pl.BlockSpec((tq, GRP * HEAD_DIM), lambda kv, i: (i, kv))
    return _call(
        body, name=name, grid=(N_KV, S // tq),
        in_specs=[qspec, pl.BlockSpec((S, HEAD_DIM), lambda kv, i: (0, kv)),
                  pl.BlockSpec((S, HEAD_DIM), lambda kv, i: (0, N_KV + kv)), qspec],
        out_specs=[qspec, pl.BlockSpec((S, HEAD_DIM), lambda kv, i: (0, kv)),
                   pl.BlockSpec((S, HEAD_DIM), lambda kv, i: (0, kv))],
        out_shape=[jax.ShapeDtypeStruct((S, Q_COLS), bf16), jax.ShapeDtypeStruct((S, KV_COLS), f32),
                   jax.ShapeDtypeStruct((S, KV_COLS), f32)],
        compiler_params=_params(("parallel", "arbitrary"), 6 * GRP * tq * S * 4 + (8 << 20)),
    )(qr, kr, kv_raw, do)


CONV_HALO = 16


def _fill_padded(pad_ref, val, S):
    pad_ref[pl.ds(0, CONV_HALO), :] = jnp.zeros((CONV_HALO, LANES), f32)
    pad_ref[pl.ds(CONV_HALO + S, CONV_HALO), :] = jnp.zeros((CONV_HALO, LANES), f32)
    pad_ref[pl.ds(CONV_HALO, S), :] = val


def _group_specs(S, n_groups, second_half):
    return pl.BlockSpec((S, LANES), functools.partial(lambda g, o: (0, g + o), o=n_groups if second_half else 0))


def _conv1_fwd(conv_in, wdw, b_dw, name):
    S = conv_in.shape[0]
    ng = CONV_CH // LANES
    R = min(256, S)

    def body(a_ref, g_ref, w_ref, b_ref, o_ref, pad_ref):
        z = a_ref[...].astype(f32) * jax.nn.sigmoid(g_ref[...].astype(f32))
        _fill_padded(pad_ref, z, S)
        for r in range(S // R):
            acc = jnp.zeros((R, LANES), f32) + b_ref[...]
            for j in range(CONV_W):
                acc = acc + w_ref[pl.ds(j, 1), :] * pad_ref[pl.ds(r * R + CONV_HALO - CONV_PAD + j, R), :]
            o_ref[pl.ds(r * R, R), :] = acc

    return _call(
        body, name=name, grid=(ng,),
        in_specs=[_group_specs(S, ng, False), _group_specs(S, ng, True),
                  pl.BlockSpec((CONV_WP, LANES), lambda g: (g, 0)), pl.BlockSpec((1, LANES), lambda g: (0, g))],
        out_specs=pl.BlockSpec((S, LANES), lambda g: (0, g)),
        out_shape=jax.ShapeDtypeStruct((S, CONV_CH), f32),
        scratch_shapes=[pltpu.VMEM((S + 2 * CONV_HALO, LANES), f32)],
        compiler_params=_params(("parallel",), 24 << 20),
    )(conv_in, conv_in, wdw, b_dw)


def _conv1_bwd(conv_in, dc, wdw, name):
    S = conv_in.shape[0]
    ng = CONV_CH // LANES
    R = min(256, S)

    def body(a_ref, g_ref, w_ref, dc_ref, da_ref, dg_ref, dw_ref, db_ref, padz_ref, padd_ref):
        a = a_ref[...].astype(f32)
        sg = jax.nn.sigmoid(g_ref[...].astype(f32))
        _fill_padded(padz_ref, a * sg, S)
        _fill_padded(padd_ref, dc_ref[...], S)
        for r in range(S // R):
            dz = jnp.zeros((R, LANES), f32)
            for j in range(CONV_W):
                dz = dz + w_ref[pl.ds(j, 1), :] * padd_ref[pl.ds(r * R + CONV_HALO + CONV_PAD - j, R), :]
            rows = pl.ds(r * R, R)
            ar, sr = a_ref[rows, :].astype(f32), jax.nn.sigmoid(g_ref[rows, :].astype(f32))
            da_ref[rows, :] = (dz * sr).astype(da_ref.dtype)
            dg_ref[rows, :] = (dz * ar * sr * (1.0 - sr)).astype(dg_ref.dtype)
        for j in range(CONV_W):
            tot = jnp.zeros((1, LANES), f32)
            for r in range(S // R):
                tot = tot + jnp.sum(dc_ref[pl.ds(r * R, R), :] * padz_ref[pl.ds(r * R + CONV_HALO - CONV_PAD + j, R), :],
                                    axis=0, keepdims=True)
            dw_ref[pl.ds(j, 1), :] = tot
        dw_ref[pl.ds(CONV_W, CONV_WP - CONV_W), :] = jnp.zeros((CONV_WP - CONV_W, LANES), f32)
        db_ref[...] = jnp.sum(dc_ref[...], axis=0, keepdims=True)

    return _call(
        body, name=name, grid=(ng,),
        in_specs=[_group_specs(S, ng, False), _group_specs(S, ng, True),
                  pl.BlockSpec((CONV_WP, LANES), lambda g: (g, 0)), pl.BlockSpec((S, LANES), lambda g: (0, g))],
        out_specs=[pl.BlockSpec((S, LANES), lambda g: (0, g)), pl.BlockSpec((S, LANES), lambda g: (0, g)),
                   pl.BlockSpec((CONV_WP, LANES), lambda g: (g, 0)), pl.BlockSpec((1, LANES), lambda g: (0, g))],
        out_shape=[jax.ShapeDtypeStruct((S, CONV_CH), bf16), jax.ShapeDtypeStruct((S, CONV_CH), bf16),
                   jax.ShapeDtypeStruct((ng * CONV_WP, LANES), f32), jax.ShapeDtypeStruct((1, CONV_CH), f32)],
        scratch_shapes=[pltpu.VMEM((S + 2 * CONV_HALO, LANES), f32), pltpu.VMEM((S + 2 * CONV_HALO, LANES), f32)],
        compiler_params=_params(("parallel",), 24 << 20),
    )(conv_in, conv_in, wdw, dc)


def _ln_stats(x, eps=1e-5):
    xc = x - jnp.mean(x, axis=-1, keepdims=True)
    r = lax.rsqrt(jnp.mean(xc * xc, axis=-1, keepdims=True) + eps)
    return r, xc * r


def _ln_bwd(r, xh, dxh):
    return r * (dxh - jnp.mean(dxh, axis=-1, keepdims=True) - xh * jnp.mean(dxh * xh, axis=-1, keepdims=True))


def _conv2_fwd(c, ln_g, ln_b, name):
    def body(c_ref, g_ref, b_ref, o_ref):
        _, xh = _ln_stats(c_ref[...])
        y = xh * g_ref[...] + b_ref[...]
        o_ref[...] = (y * jax.nn.sigmoid(y)).astype(o_ref.dtype)

    return _rows(body, [("r", c), ("f", ln_g), ("f", ln_b)], [("r", CONV_CH, bf16)], tm=min(256, c.shape[0]), name=name)[0]


def _conv2_bwd(c, dcz, ln_g, ln_b, name):
    def body(c_ref, d_ref, g_ref, b_ref, dc_ref, dg_ref, db_ref):
        r, xh = _ln_stats(c_ref[...])
        y = xh * g_ref[...] + b_ref[...]
        sg = jax.nn.sigmoid(y)
        dy = d_ref[...].astype(f32) * (sg * (1.0 + y * (1.0 - sg)))
        dc_ref[...] = _ln_bwd(r, xh, dy * g_ref[...])
        _accumulate(dg_ref, jnp.sum(dy * xh, axis=0, keepdims=True))
        _accumulate(db_ref, jnp.sum(dy, axis=0, keepdims=True))

    return _rows(body, [("r", c), ("r", dcz), ("f", ln_g), ("f", ln_b)],
                 [("r", CONV_CH, f32), ("a", (1, CONV_CH), f32), ("a", (1, CONV_CH), f32)],
                 tm=min(256, c.shape[0]), name=name)


GELU_K = math.sqrt(2.0 / math.pi)
GELU_C = 0.044715


def _gelu(x):
    return 0.5 * x * (1.0 + jnp.tanh(GELU_K * (x + GELU_C * x * x * x)))


def _gelu_and_grad(x):
    x2 = x * x
    th = jnp.tanh(GELU_K * (x + GELU_C * x2 * x))
    half = 0.5 * (1.0 + th)
    return x * half, half + 0.5 * x * (1.0 - th * th) * (GELU_K * (1.0 + 3.0 * GELU_C * x2))


def _chunk_rows(n):
    return pl.ds(pl.multiple_of(n * SG_CHUNK, SG_CHUNK), SG_CHUNK)


def _sgu_fwd(sg_in, ln_g, ln_b, w_s, b_s, name):
    S = sg_in.shape[0]

    def body(u_ref, v_ref, lg_ref, lb_ref, w_ref, b_ref, o_ref):
        wb = w_ref[...].astype(bf16)

        def chunk(n, carry):
            rows = _chunk_rows(n)
            gu = _gelu(u_ref[rows, :].astype(f32))
            _, xh = _ln_stats(_gelu(v_ref[rows, :].astype(f32)))
            vl = xh * lg_ref[...] + lb_ref[...]
            t = jnp.dot(wb, vl.astype(bf16), preferred_element_type=f32) + b_ref[...]
            o_ref[rows, :] = (gu * t).astype(o_ref.dtype)
            return carry

        lax.fori_loop(0, S // SG_CHUNK, chunk, 0, unroll=2)

    return _call(
        body, name=name, grid=(SG_G,),
        in_specs=[_group_specs(S, SG_G, False), _group_specs(S, SG_G, True),
                  pl.BlockSpec((1, LANES), lambda g: (0, g)), pl.BlockSpec((1, LANES), lambda g: (0, g)),
                  pl.BlockSpec((None, SG_CHUNK, SG_CHUNK), lambda g: (g, 0, 0)),
                  pl.BlockSpec((None, SG_CHUNK, 1), lambda g: (g, 0, 0))],
        out_specs=pl.BlockSpec((S, LANES), lambda g: (0, g)),
        out_shape=jax.ShapeDtypeStruct((S, SG_CH), bf16),
        compiler_params=_params(("parallel",), 24 << 20),
    )(sg_in, sg_in, ln_g, ln_b, w_s, b_s)


def _sgu_bwd(sg_in, dsz, ln_g, ln_b, w_s, w_s_t, b_s, name):
    S = sg_in.shape[0]

    def body(u_ref, v_ref, lg_ref, lb_ref, w_ref, wt_ref, b_ref, d_ref, du_ref, dv_ref, dw_ref, db_ref, dlg_ref, dlb_ref):
        wb = w_ref[...].astype(bf16)
        wtb = wt_ref[...].astype(bf16)

        def chunk(n, carry):
            dwa, dba, dlga, dlba = carry
            rows = _chunk_rows(n)
            u = u_ref[rows, :].astype(f32)
            v = v_ref[rows, :].astype(f32)
            gu, gu_grad = _gelu_and_grad(u)
            gv, gv_grad = _gelu_and_grad(v)
            r, xh = _ln_stats(gv)
            vlb = (xh * lg_ref[...] + lb_ref[...]).astype(bf16)
            t = jnp.dot(wb, vlb, preferred_element_type=f32) + b_ref[...]
            d = d_ref[rows, :].astype(f32)
            dt = d * gu
            dtb = dt.astype(bf16)
            dwa = dwa + lax.dot_general(dtb, vlb, (((1,), (1,)), ((), ())), preferred_element_type=f32)
            dba = dba + jnp.sum(dt, axis=1, keepdims=True)
            dvl = jnp.dot(wtb, dtb, preferred_element_type=f32)
            dlga = dlga + jnp.sum(dvl * xh, axis=0, keepdims=True)
            dlba = dlba + jnp.sum(dvl, axis=0, keepdims=True)
            dgv = _ln_bwd(r, xh, dvl * lg_ref[...])
            du_ref[rows, :] = (d * t * gu_grad).astype(du_ref.dtype)
            dv_ref[rows, :] = (dgv * gv_grad).astype(dv_ref.dtype)
            return dwa, dba, dlga, dlba

        init = (jnp.zeros((SG_CHUNK, SG_CHUNK), f32), jnp.zeros((SG_CHUNK, 1), f32),
                jnp.zeros((1, LANES), f32), jnp.zeros((1, LANES), f32))
        dwa, dba, dlga, dlba = lax.fori_loop(0, S // SG_CHUNK, chunk, init, unroll=2)
        dw_ref[...] = dwa
        db_ref[...] = dba
        dlg_ref[...] = dlga
        dlb_ref[...] = dlba

    wspec = pl.BlockSpec((None, SG_CHUNK, SG_CHUNK), lambda g: (g, 0, 0))
    bspec = pl.BlockSpec((None, SG_CHUNK, 1), lambda g: (g, 0, 0))
    lspec = pl.BlockSpec((1, LANES), lambda g: (0, g))
    cspec = pl.BlockSpec((S, LANES), lambda g: (0, g))
    return _call(
        body, name=name, grid=(SG_G,),
        in_specs=[_group_specs(S, SG_G, False), _group_specs(S, SG_G, True), lspec, lspec, wspec, wspec, bspec, cspec],
        out_specs=[cspec, cspec, wspec, bspec, lspec, lspec],
        out_shape=[jax.ShapeDtypeStruct((S, SG_CH), bf16), jax.ShapeDtypeStruct((S, SG_CH), bf16),
                   jax.ShapeDtypeStruct((SG_G, SG_CHUNK, SG_CHUNK), f32), jax.ShapeDtypeStruct((SG_G, SG_CHUNK, 1), f32),
                   jax.ShapeDtypeStruct((1, SG_CH), f32), jax.ShapeDtypeStruct((1, SG_CH), f32)],
        compiler_params=_params(("parallel",), 24 << 20),
    )(sg_in, sg_in, ln_g, ln_b, w_s, w_s_t, b_s, dsz)


def _row_tile(r, c, n_arrays, itemsize=4):
    fits = [tm for tm in range(16, r + 1, 16) if r % tm == 0 and 2 * n_arrays * tm * c * itemsize <= (24 << 20)]
    return fits[-1] if fits else r


def _sum_slots(slots, name):
    n, r, c = slots.shape
    tm = _row_tile(r, c, n + 2)

    def body(s_ref, o_ref):
        acc = s_ref[0].astype(f32)
        for k in range(1, n):
            acc = acc + s_ref[k].astype(f32)
        o_ref[...] = acc

    return _call(body, name=name, grid=(r // tm,),
                 in_specs=[pl.BlockSpec((n, tm, c), lambda i: (0, i, 0))],
                 out_specs=pl.BlockSpec((tm, c), lambda i: (i, 0)),
                 out_shape=jax.ShapeDtypeStruct((r, c), f32),
                 compiler_params=_params(("parallel",), 40 << 20))(slots)


def _add_sibling(g4, recv, core, name):
    _, _, r, c = g4.shape
    tm = _row_tile(r, c, 3, 2)

    def body(core_ref, g_ref, r_ref, o_ref):
        o_ref[...] = (g_ref[...].astype(f32) + r_ref[...].astype(f32)).astype(o_ref.dtype)

    grid_spec = pltpu.PrefetchScalarGridSpec(
        num_scalar_prefetch=1, grid=(N_CHIP, r // tm),
        in_specs=[pl.BlockSpec((None, None, tm, c), lambda k, i, core_ref: (k, core_ref[0], i, 0)),
                  pl.BlockSpec((None, tm, c), lambda k, i, core_ref: (k, i, 0))],
        out_specs=pl.BlockSpec((None, tm, c), lambda k, i, core_ref: (k, i, 0)))
    return _call(body, name=name, grid_spec=grid_spec, out_shape=jax.ShapeDtypeStruct((N_CHIP, r, c), bf16),
                 compiler_params=_params(("parallel", "parallel"), 40 << 20))(core, g4, recv)


def _adamw(w, g, m, v, name):
    L, r, c = w.shape
    tm = _row_tile(r, c, 7)
    c1 = 1.0 - ADAM_B1 ** ADAM_STEP
    c2 = 1.0 - ADAM_B2 ** ADAM_STEP

    def body(w_ref, g_ref, m_ref, v_ref, d_ref, mo_ref, vo_ref):
        g_ = g_ref[...]
        m_ = ADAM_B1 * m_ref[...] + (1.0 - ADAM_B1) * g_
        v_ = ADAM_B2 * v_ref[...] + (1.0 - ADAM_B2) * (g_ * g_)
        d_ref[...] = -ADAM_LR * ((m_ / c1) / (jnp.sqrt(v_ / c2) + ADAM_EPS) + ADAM_WD * w_ref[...])
        mo_ref[...] = m_
        vo_ref[...] = v_

    spec = pl.BlockSpec((None, tm, c), lambda l, i: (l, i, 0))
    shp = jax.ShapeDtypeStruct((L, r, c), f32)
    return _call(body, name=name, grid=(L, r // tm), in_specs=[spec] * 4, out_specs=[spec] * 3,
                 out_shape=[shp] * 3, compiler_params=_params(("parallel", "parallel"), 40 << 20))(w, g, m, v)


def _mesh_pos():
    return lax.axis_index("x"), lax.axis_index("y"), lax.axis_index("c")


SEM = pl.BlockSpec(memory_space=pltpu.SEMAPHORE)
ANY = pl.BlockSpec(memory_space=pl.ANY)
EFFECT = pltpu.SideEffectType.DATAFLOW_SIDE_EFFECTING


def _other_chips(x, y):
    return [(1 - x, y), (x, 1 - y), (1 - x, 1 - y)]


def _peers(kind, x, y):
    return [(x, y)] if kind == "sibling" else _other_chips(x, y)


def _ici_copy(kind, src_ref, land_ref, send_sem, recv_sem, sender, target, c):
    (sx, sy), (tx, ty) = sender, target
    if kind == "sibling":
        return pltpu.make_async_remote_copy(src_ref=src_ref.at[:, 1 - c], dst_ref=land_ref, send_sem=send_sem,
                                            recv_sem=recv_sem, device_id=(tx, ty, 1 - c), device_id_type=MESH)
    if kind == "gather":
        src, dst = src_ref, land_ref.at[4 * sx + 2 * sy + c]
    else:
        src, dst = src_ref.at[2 * tx + ty], land_ref.at[2 * sx + sy]
    return pltpu.make_async_remote_copy(src_ref=src, dst_ref=dst, send_sem=send_sem, recv_sem=recv_sem,
                                        device_id=(tx, ty, c), device_id_type=MESH)


def _ici_start(kind, srcs, lands, after, name):
    n = len(srcs)
    npeer = 1 if kind == "sibling" else 3

    def body(*refs):
        src_refs, land_refs = refs[:n], refs[n:2 * n]
        send_sems, recv_sems = refs[2 * n + 1], refs[2 * n + 2]
        token = refs[-1]
        x, y, c = _mesh_pos()
        for j, chip in enumerate(_peers(kind, x, y)):
            for k in range(n):
                _ici_copy(kind, src_refs[k], land_refs[k], send_sems.at[npeer * k + j], recv_sems.at[npeer * k + j],(x, y), chip, c).start()
        token[...] = jnp.zeros_like(token)

    bufs = list(srcs) + list(lands)
    return _call(
        body, name=name,
        out_shape=(pltpu.SemaphoreType.DMA((npeer * n,)), pltpu.SemaphoreType.DMA((npeer * n,)),
                   *[pltpu.HBM(b.shape, b.dtype) for b in bufs], jax.ShapeDtypeStruct((8, LANES), f32)),
        in_specs=[HBM] * (2 * n) + [ANY], out_specs=(SEM, SEM, *[HBM] * (2 * n), pl.BlockSpec(memory_space=pltpu.VMEM)),
        input_output_aliases={i: 2 + i for i in range(2 * n)},
        compiler_params=pltpu.CompilerParams(has_side_effects=EFFECT),
    )(*[pltpu.with_memory_space_constraint(b, pltpu.HBM) for b in bufs], after)


def _ici_wait(kind, started, after, name):
    send_sems, recv_sems, *bufs = started[:-1]
    n = len(bufs) // 2
    npeer = 1 if kind == "sibling" else 3

    def body(*refs):
        src_refs, land_refs = refs[:n], refs[n:2 * n]
        send_sems, recv_sems = refs[2 * n], refs[2 * n + 1]
        x, y, c = _mesh_pos()
        for j, chip in enumerate(_peers(kind, x, y)):
            for k in range(n):
                _ici_copy(kind, src_refs[k], land_refs[k], send_sems.at[npeer * k + j], recv_sems.at[npeer * k + j],(x, y), chip, c).wait_send()
                _ici_copy(kind, src_refs[k], land_refs[k], send_sems.at[npeer * k + j], recv_sems.at[npeer * k + j],chip, (x, y), c).wait_recv()

    out = _call(
        body, name=name, out_shape=[pltpu.HBM(b.shape, b.dtype) for b in bufs],
        in_specs=[HBM] * (2 * n) + [SEM, SEM, ANY], out_specs=[HBM] * (2 * n),
        input_output_aliases={i: i for i in range(2 * n)},
        compiler_params=pltpu.CompilerParams(has_side_effects=EFFECT),
    )(*bufs, send_sems, recv_sems, after)
    return out[:n], out[n:]


def _d2d_gather(lands, after, name):
    n = len(lands)

    def body(*refs):
        in_refs, o_refs = refs[:n], refs[n + 1:2 * n + 1]
        send_sems, recv_sems = refs[2 * n + 1:]
        x, y, c = _mesh_pos()
        copies = [pltpu.make_async_remote_copy(
            src_ref=in_refs[k].at[:, c], dst_ref=o_refs[k].at[:, c], send_sem=send_sems.at[k], recv_sem=recv_sems.at[k],
            device_id=(x, y, 1 - c), device_id_type=MESH) for k in range(n)]
        for cp in copies:
            cp.start()
        for k, cp in enumerate(copies):
            cp.wait_send()
            pltpu.make_async_remote_copy(
                src_ref=in_refs[k].at[:, c], dst_ref=o_refs[k].at[:, 1 - c], send_sem=send_sems.at[k],
                recv_sem=recv_sems.at[k], device_id=(x, y, 1 - c), device_id_type=MESH).wait_recv()

    return _call(
        body, name=name, in_specs=[HBM] * n + [ANY], out_specs=[HBM] * n,
        out_shape=[jax.ShapeDtypeStruct(b.shape, b.dtype) for b in lands],
        input_output_aliases={k: k for k in range(n)},
        scratch_shapes=[pltpu.SemaphoreType.DMA((n,)), pltpu.SemaphoreType.DMA((n,))],
    )(*lands, after)


def _sum_chip_slots(lands, sums, chip, name):
    _, r, c = lands.shape
    tm = _row_tile(r, c, 10, 2)

    def body(chip_ref, l_ref, s_ref, o_ref):
        acc = None
        for k in range(N_CHIP):
            part = jnp.where(chip_ref[0] == k, s_ref[k], l_ref[k]).astype(f32)
            acc = part if acc is None else acc + part
        o_ref[...] = acc

    grid_spec = pltpu.PrefetchScalarGridSpec(
        num_scalar_prefetch=1, grid=(r // tm,),
        in_specs=[pl.BlockSpec((N_CHIP, tm, c), lambda i, chip_ref: (0, i, 0)),
                  pl.BlockSpec((N_CHIP, tm, c), lambda i, chip_ref: (0, i, 0))],
        out_specs=pl.BlockSpec((tm, c), lambda i, chip_ref: (i, 0)))
    return _call(body, name=name, grid_spec=grid_spec, out_shape=jax.ShapeDtypeStruct((r, c), f32),
                 compiler_params=_params(("parallel",), 40 << 20))(chip, lands, sums)


def _reduce_begin(grads, core, tag):
    g4s = [g.reshape(N_CHIP, 2, g.shape[0] // N_DEV, g.shape[1]) for g in grads]
    recvs = [lax.empty((N_CHIP,) + g.shape[2:], g.dtype) for g in g4s]
    return _ici_start("sibling", g4s, recvs, core, name="rs_d2d_start_" + tag)


def _reduce_continue(begun, core, after, tag):
    g4s, recvs = _ici_wait("sibling", begun, after, name="rs_d2d_wait_" + tag)
    sums = [_add_sibling(g4, rv, core, name="rs_add_" + tag) for g4, rv in zip(g4s, recvs)]
    lands = [lax.empty(s.shape, s.dtype) for s in sums]
    return _ici_start("reduce", sums, lands, core, name="rs_start_" + tag)


def _adamw_tile(chip_ref, w_ref, m_ref, v_ref, l_ref, s_ref, g_ref, d_ref, mo_ref, vo_ref):
    c1 = 1.0 - ADAM_B1 ** ADAM_STEP
    c2 = 1.0 - ADAM_B2 ** ADAM_STEP
    g_ = None
    for k in range(N_CHIP):
        part = jnp.where(chip_ref[0] == k, s_ref[k], l_ref[k]).astype(f32)
        g_ = part if g_ is None else g_ + part
    m_ = ADAM_B1 * m_ref[...] + (1.0 - ADAM_B1) * g_
    v_ = ADAM_B2 * v_ref[...] + (1.0 - ADAM_B2) * (g_ * g_)
    g_ref[...] = g_
    d_ref[...] = -ADAM_LR * ((m_ / c1) / (jnp.sqrt(v_ / c2) + ADAM_EPS) + ADAM_WD * w_ref[...])
    mo_ref[...] = m_
    vo_ref[...] = v_


def _adamw_rider(layer, w, m, v, lands, sums, chip, steps):
    L, r, c = w.shape
    tm = r // steps
    assert tm * steps == r and tm % 16 == 0
    wspec = pl.BlockSpec((None, tm, c), lambda i, j, k: (layer, i, 0))
    sspec = pl.BlockSpec((N_CHIP, tm, c), lambda i, j, k: (0, i, 0))
    return dict(args=[chip, w, m, v, lands, sums],
                in_specs=[pl.BlockSpec(memory_space=pltpu.SMEM)] + [wspec] * 3 + [sspec] * 2,
                out_specs=[wspec] * 4, out_shape=[jax.ShapeDtypeStruct((L, r, c), f32)] * 4,
                body=_adamw_tile, vmem=2 * 11 * tm * c * 4)


def _adamw_reduced(layer, w, m, v, lands, sums, chip, prev, name):
    L, r, c = w.shape
    tm = _row_tile(r, c, 11)
    n_prev = 0 if prev is None else 4

    def body(chip_ref, w_ref, m_ref, v_ref, l_ref, s_ref, *refs):
        _adamw_tile(chip_ref, w_ref, m_ref, v_ref, l_ref, s_ref, *refs[n_prev:])

    wspec = pl.BlockSpec((None, tm, c), lambda i, chip_ref: (layer, i, 0))
    sspec = pl.BlockSpec((N_CHIP, tm, c), lambda i, chip_ref: (0, i, 0))
    grid_spec = pltpu.PrefetchScalarGridSpec(
        num_scalar_prefetch=1, grid=(r // tm,), in_specs=[wspec] * 3 + [sspec] * 2 + [ANY] * n_prev, out_specs=[wspec] * 4)
    return _call(body, name=name, grid_spec=grid_spec, out_shape=[jax.ShapeDtypeStruct((L, r, c), f32)] * 4,
                 input_output_aliases={6 + i: i for i in range(n_prev)},
                 compiler_params=_params(("parallel",), 40 << 20))(chip, w, m, v, lands, sums, *(prev or ()))


def _rope_tables(S):
    rows = S // GRID_W
    row = jnp.repeat(jnp.arange(rows, dtype=f32), GRID_W)
    col = jnp.tile(jnp.arange(GRID_W, dtype=f32), rows)
    nf = HEAD_DIM // 4
    inv = ROPE_THETA ** (-jnp.arange(nf, dtype=f32) / nf)
    ang = jnp.concatenate([row[:, None] * inv, col[:, None] * inv], axis=-1)
    cos, sin = jnp.cos(ang), jnp.sin(ang)
    return jnp.concatenate([cos, cos], axis=-1), jnp.concatenate([-sin, sin], axis=-1)


def _layer_fwd(xin, p, w, more_weights, cos2, sin2):
    sv = {"xin": xin}
    h = sv["h"] = _rms_fwd(xin, p["g_mix"], name="rms_mix")
    proj = functools.partial(_mm, h, w["in"], "nt", bf16)
    q_raw = sv["q_raw"] = proj(n=Q_COLS, b_off=0, name="proj_q")
    kv_raw = sv["kv_raw"] = proj(n=2 * KV_COLS, b_off=OFF_KV, name="proj_kv")
    conv_in = sv["conv_in"] = proj(n=2 * CONV_CH, b_off=OFF_CONV, name="proj_conv")
    sg_in = sv["sg_in"] = proj(n=2 * SG_CH, b_off=OFF_SG, name="proj_sg")
    gl = sv["gl"] = proj(n=3 * D_MODEL, b_off=OFF_GATE, name="proj_gate")
    qr, kr = sv["qr"], sv["kr"] = _qk_fwd(q_raw, kv_raw, p["q_norm_g"], p["k_norm_g"], cos2, sin2, name="qk_fwd")
    o = sv["o"] = _attn_fwd(qr, kr, kv_raw, name="attn_fwd")
    c = sv["c"] = _conv1_fwd(conv_in, w["dw"], p["b_dw"], name="conv1_fwd")
    cz = sv["cz"] = _conv2_fwd(c, p["conv_ln_g"], p["conv_ln_b"], name="conv2_fwd")
    sz = sv["sz"] = _sgu_fwd(sg_in, p["sg_ln_g"], p["sg_ln_b"], p["w_s"], p["b_s"], name="sgu_fwd")
    w = {**w, **more_weights(1, sz)}
    sv["ya"], sv["yc"], sv["ys"], merged = _mixer_out([o, cz, sz], [w["attn_o"], w["conv_o"], w["sg_o"]], gl, p["b_gate"],
                                                      name="mixer_out")
    sv["merged"] = merged
    x1 = sv["x1"] = _mm(merged, w["out"], "nn", f32, res=xin, name="out_proj")
    w = {**w, **more_weights(2, x1)}
    hf = sv["hf"] = _rms_fwd(x1, p["g_ffn"], name="rms_ffn")
    sv["fg"], sv["fu"], act = _ffn_up(hf, w["ff_gate"], w["ff_up"], name="ffn_up")
    sv["act"] = act
    x2 = _mm(act, w["ff_down"], "nn", f32, res=x1, name="ff_down")
    return x2, sv, w


def _layer_bwd(dx2, dx2b, sv, p, w, cos2, sin2, reduce_begin, reduce_continue, last, rider_for):
    small = {}

    def grad_mm(weight, a, b, name):
        fresh = b if a is sv["act"] else a
        ride = rider_for(weight, a.shape[1] // _pick(a.shape[1], (512, 256, 128)), fresh)
        if ride is None:
            return _mm(a, b, "tn", bf16, name=name)
        g, results = _mm(a, b, "tn", bf16, rider=ride[0], name=name + "_ridden")
        ride[1](results)
        return g

    dfg, dfu = _ffn_down_bwd(dx2b, w["ff_down"], sv["fg"], sv["fu"], name="ffn_down_bwd")
    g_down = grad_mm("w_ff_down", sv["act"], dx2b, name="g_ff_down")
    dhf = _mm(dfg, w["ff_gate"], "nn", f32, name="d_hf_gate")
    dhf = _mm(dfu, w["ff_up"], "nn", f32, res=dhf, name="d_hf_up")
    g_gate = grad_mm("w_ff_gate", dfg, sv["hf"], name="g_ff_gate")
    g_up = grad_mm("w_ff_up", dfu, sv["hf"], name="g_ff_up")
    zero = reduce_begin("ffn", dict(w_ff_gate=g_gate, w_ff_up=g_up, w_ff_down=g_down))[0, 0]
    dx1, dx1b, small["g_ffn"] = _rms_bwd(sv["x1"], p["g_ffn"] + zero, dhf, dx2, name="rms_ffn_bwd")
    g_out = _mm(sv["merged"], dx1b, "tn", bf16, name="g_out")
    *dgl, dya, dyc, dys, db0, db1, db2 = _merge_bwd_fused(dx1b, w["out"], sv["gl"], p["b_gate"], sv["ya"], sv["yc"], sv["ys"],
                                                        name="merge_bwd")
    small["b_gate"] = jnp.concatenate([db0, db1, db2], axis=1)
    do = _mm(dya, w["attn_o"], "nn", bf16, after=reduce_continue("ffn", dya), name="d_o")
    g_ao = _mm(dya, sv["o"], "tn", bf16, name="g_attn_o")
    dcz = _mm(dyc, w["conv_o"], "nn", bf16, name="d_cz")
    g_co = _mm(dyc, sv["cz"], "tn", bf16, name="g_conv_o")
    dsz = _mm(dys, w["sg_o"], "nn", bf16, name="d_sz")
    g_so = _mm(dys, sv["sz"], "tn", bf16, name="g_sg_o")
    zero = reduce_begin("mix", dict(w_attn_o=g_ao, w_conv_o=g_co, w_sg_o=g_so, w_out=g_out))[0, 0]
    dsu, dsv, small["w_s"], small["b_s"], small["sg_ln_g"], small["sg_ln_b"] = _sgu_bwd(
        sv["sg_in"], dsz, p["sg_ln_g"] + zero, p["sg_ln_b"], p["w_s"], p["w_s_t"], p["b_s"], name="sgu_bwd")
    dc, small["conv_ln_g"], small["conv_ln_b"] = _conv2_bwd(sv["c"], dcz, p["conv_ln_g"], p["conv_ln_b"], name="conv2_bwd")
    da, dgt, small["w_dw"], small["b_dw"] = _conv1_bwd(sv["conv_in"], dc, w["dw"], name="conv1_bwd")
    zero = reduce_continue("mix", da)[0, 0]
    dqr, dkr, dv = _attn_bwd(sv["qr"], sv["kr"], sv["kv_raw"], do, name="attn_bwd")
    dq_raw, dk_raw, small["q_norm_g"], small["k_norm_g"] = _qk_bwd(
        sv["q_raw"], sv["kv_raw"], dqr, dkr, p["q_norm_g"] + zero, p["k_norm_g"], cos2, sin2, name="qk_bwd")
    dproj = jnp.concatenate([dq_raw, dk_raw, dv.astype(bf16), da, dgt, dsu, dsv, *dgl], axis=1)
    g_in = grad_mm("w_in", dproj, sv["h"], name="g_in")
    begun = reduce_begin("in", dict(w_in=g_in))
    if last:
        begun = reduce_continue("in", begun)
    dh = _mm(dproj, w["in"], "nn", f32, after=begun, name="d_h")
    zero = begun[0, 0] if last else reduce_continue("in", dh)[0, 0]
    dx, dxb, small["g_mix"] = _rms_bwd(sv["xin"], p["g_mix"] + zero, dh, dx1, name="rms_mix_bwd")
    return dx, dxb, small


SMALL = ("g_mix", "b_gate", "q_norm_g", "k_norm_g", "b_dw", "conv_ln_g", "conv_ln_b", "sg_ln_g", "sg_ln_b",
         "w_s", "b_s", "g_ffn")
PACK_ALIGN = 8 * LANES


def _pack(parts):
    flat = jnp.concatenate([a.reshape(-1).astype(f32) for a in parts])
    pad = -flat.shape[0] % PACK_ALIGN
    return jnp.pad(flat, (0, pad)).reshape(-1, LANES)


def _unpack(buf, shapes):
    flat = buf.reshape(-1)
    out, pos = [], 0
    for shp in shapes:
        size = math.prod(shp)
        out.append(flat[pos:pos + size].reshape(shp))
        pos += size
    return out


def kernel(x, g_mix, w_in, b_gate, q_norm_g, k_norm_g, w_attn_o, w_dw, b_dw, conv_ln_g, conv_ln_b, w_conv_o, sg_ln_g, sg_ln_b, w_s, b_s, w_sg_o, w_out, g_ffn, w_ff_gate, w_ff_up, w_ff_down, g_final, loss_target, m_g_mix, m_w_in, m_b_gate, m_q_norm_g, m_k_norm_g, m_w_attn_o, m_w_dw, m_b_dw, m_conv_ln_g, m_conv_ln_b, m_w_conv_o, m_sg_ln_g, m_sg_ln_b, m_w_s, m_b_s, m_w_sg_o, m_w_out, m_g_ffn, m_w_ff_gate, m_w_ff_up, m_w_ff_down, m_g_final, v_g_mix, v_w_in, v_b_gate, v_q_norm_g, v_k_norm_g, v_w_attn_o, v_w_dw, v_b_dw, v_conv_ln_g, v_conv_ln_b, v_w_conv_o, v_sg_ln_g, v_sg_ln_b, v_w_s, v_b_s, v_w_sg_o, v_w_out, v_g_ffn, v_w_ff_gate, v_w_ff_up, v_w_ff_down, v_g_final):
    weights = dict(g_mix=g_mix, w_in=w_in, b_gate=b_gate, q_norm_g=q_norm_g, k_norm_g=k_norm_g, w_attn_o=w_attn_o,
                   w_dw=w_dw, b_dw=b_dw, conv_ln_g=conv_ln_g, conv_ln_b=conv_ln_b, w_conv_o=w_conv_o, sg_ln_g=sg_ln_g,
                   sg_ln_b=sg_ln_b, w_s=w_s, b_s=b_s, w_sg_o=w_sg_o, w_out=w_out, g_ffn=g_ffn, w_ff_gate=w_ff_gate,
                   w_ff_up=w_ff_up, w_ff_down=w_ff_down, g_final=g_final)
    mom_m = dict(g_mix=m_g_mix, w_in=m_w_in, b_gate=m_b_gate, q_norm_g=m_q_norm_g, k_norm_g=m_k_norm_g,
                 w_attn_o=m_w_attn_o, w_dw=m_w_dw, b_dw=m_b_dw, conv_ln_g=m_conv_ln_g, conv_ln_b=m_conv_ln_b,
                 w_conv_o=m_w_conv_o, sg_ln_g=m_sg_ln_g, sg_ln_b=m_sg_ln_b, w_s=m_w_s, b_s=m_b_s, w_sg_o=m_w_sg_o,
                 w_out=m_w_out, g_ffn=m_g_ffn, w_ff_gate=m_w_ff_gate, w_ff_up=m_w_ff_up, w_ff_down=m_w_ff_down,
                 g_final=m_g_final)
    mom_v = dict(g_mix=v_g_mix, w_in=v_w_in, b_gate=v_b_gate, q_norm_g=v_q_norm_g, k_norm_g=v_k_norm_g,
                 w_attn_o=v_w_attn_o, w_dw=v_w_dw, b_dw=v_b_dw, conv_ln_g=v_conv_ln_g, conv_ln_b=v_conv_ln_b,
                 w_conv_o=v_w_conv_o, sg_ln_g=v_sg_ln_g, sg_ln_b=v_sg_ln_b, w_s=v_w_s, b_s=v_b_s, w_sg_o=v_w_sg_o,
                 w_out=v_w_out, g_ffn=v_g_ffn, w_ff_gate=v_w_ff_gate, w_ff_up=v_w_ff_up, w_ff_down=v_w_ff_down,
                 g_final=v_g_final)
    S, D = x.shape[1], x.shape[2]
    xi, yi, ci = _mesh_pos()
    me = 4 * xi + 2 * yi + ci
    core = jnp.reshape(ci, (1,)).astype(jnp.int32)
    cos2, sin2 = _rope_tables(S)

    big = ("w_in", "w_attn_o", "w_conv_o", "w_sg_o", "w_out", "w_ff_gate", "w_ff_up", "w_ff_down")
    transposed = {"w_in", "w_attn_o", "w_conv_o", "w_sg_o", "w_ff_gate", "w_ff_up"}
    chip = jnp.reshape(2 * xi + yi, (1,)).astype(jnp.int32)
    groups = (("in", "dw"), ("attn_o", "conv_o", "sg_o", "out"), ("ff_gate", "ff_up", "ff_down"))
    P, shards = [], []
    for l in range(DEPTH):
        sh = {n[2:]: (weights[n][l].T if n in transposed else weights[n][l]).astype(bf16) for n in big}
        sh["dw"] = jnp.pad(w_dw[l].reshape(CONV_W, LANES), ((0, CONV_WP - CONV_W), (0, 0)))
        shards.append(sh)
        p = {n: weights[n][l].reshape(1, -1) for n in SMALL if n not in ("w_s", "b_s")}
        p["w_s"] = w_s[l]
        p["w_s_t"] = jnp.swapaxes(w_s[l], 1, 2)
        p["b_s"] = b_s[l].reshape(SG_G, SG_CHUNK, 1)
        P.append(p)

    gathers = {}

    def start_gather(l, gi, after):
        srcs = [shards[l][n] for n in groups[gi]]
        lands = [lax.dynamic_update_index_in_dim(lax.empty((N_DEV,) + s.shape, s.dtype), s, me, 0) for s in srcs]
        gathers[l, gi] = _ici_start("gather", srcs, lands, after, name=f"ag_start_{l}{gi}")
        return gathers[l, gi][-1]

    def gathered(l, gi, after):
        srcs, lands = _ici_wait("gather", gathers[l, gi], after, name=f"ag_wait_{l}{gi}")
        after = srcs[0]
        if gi == len(groups) - 1 and l + 1 < DEPTH:
            for gj in range(len(groups)):
                after = start_gather(l + 1, gj, after)
        full = _d2d_gather([b.reshape(N_CHIP, 2, *b.shape[1:]) for b in lands], after, name=f"ag_d2d_{gi}")
        return {n: f.reshape(-1, f.shape[3]) for n, f in zip(groups[gi], full)}

    all_started = cos2
    for gi in range(len(groups)):
        all_started = start_gather(0, gi, all_started)

    h = x.reshape(S, D)
    saved, W = [], []
    for l in range(DEPTH):
        first = gathered(l, 0, all_started if l == 0 else h)
        if l == 0:
            P[l]["g_mix"] = P[l]["g_mix"] + all_started[0, 0]
        h, sv, w = _layer_fwd(h, P[l], first, functools.partial(lambda gi, z, l: gathered(l, gi, z), l=l), cos2, sin2)
        saved.append(sv)
        W.append(w)
    dx, dxb, sq, g_final_part = _final_loss(h, g_final.reshape(1, D), loss_target.reshape(S, D), name="final_loss")
    loss = lax.psum(0.5 * jnp.sum(sq) / D, ("x", "y", "c"))

    begun, reductions, small_grads = {}, {}, [None] * DEPTH
    swap = lambda a: jnp.swapaxes(a, 1, 2)
    arrived, ridden = {}, {}

    def arrive(l, group, after):
        if (l, group) not in arrived:
            arrived[l, group] = _ici_wait("reduce", reductions[l, group][1], after, name=f"rs_wait_{group}{l}")
        return arrived[l, group]

    def rider_for(weight, steps, after, l):
        if l + 1 >= DEPTH or weight == "w_in":
            return None
        group = next(g for (ll, g), (names, _) in reductions.items() if ll == l + 1 and weight in names)
        sums, lands = arrive(l + 1, group, after)
        i = reductions[l + 1, group][0].index(weight)
        as_arrives = swap if weight in transposed else (lambda a: a)
        ride = _adamw_rider(l + 1, as_arrives(weights[weight]), as_arrives(mom_m[weight]), as_arrives(mom_v[weight]),
                            lands[i], sums[i], chip, steps)
        return ride, functools.partial(ridden.__setitem__, weight)

    for l in reversed(range(DEPTH)):
        def reduce_begin(group, grads, l=l):
            begun[l, group] = (tuple(grads), _reduce_begin(list(grads.values()), core, tag=f"{group}{l}"))
            return begun[l, group][1][-1]

        def reduce_continue(group, after, l=l):
            names, started = begun[l, group]
            reductions[l, group] = (names, _reduce_continue(started, core, after, tag=f"{group}{l}"))
            return reductions[l, group][1][-1]

        dx, dxb, small_grads[l] = _layer_bwd(dx, dxb, saved[l], P[l], W[l], cos2, sin2, reduce_begin, reduce_continue,
                                            last=(l == 0), rider_for=functools.partial(rider_for, l=l))
    grad_x = dx.reshape(x.shape)

    small_shapes = [weights[n].shape for n in SMALL] + [g_final.shape, (DEPTH, CONV_CH // LANES, CONV_WP, LANES)]
    parts = [jnp.stack([small_grads[l][n].reshape(weights[n].shape[1:]) for l in range(DEPTH)]) for n in SMALL]
    parts += [g_final_part.reshape(g_final.shape), jnp.stack([small_grads[l]["w_dw"] for l in range(DEPTH)])]
    packed = _pack(parts)
    packed_land = lax.dynamic_update_index_in_dim(lax.empty((N_DEV,) + packed.shape, f32), packed, me, 0)
    small_started = _ici_start("gather", [packed], [packed_land], dx, name="gather_small_start")

    grads_out, delta, new_m, new_v = {}, {}, {}, {}
    swap = lambda a: jnp.swapaxes(a, 1, 2)

    def update(n, group, i, after):
        as_arrives = n not in transposed or weights[n].shape[2] % LANES != 0
        if as_arrives:
            to_arrival = swap if n in transposed else (lambda a: a)
            out = ridden.get(n)
            for l in reversed(range(DEPTH if out is None else DEPTH - 1)):
                sums, lands = arrive(l, group, after if out is None else out[1])
                out = _adamw_reduced(l, to_arrival(weights[n]), to_arrival(mom_m[n]), to_arrival(mom_v[n]),
                                     lands[i], sums[i], chip, out, name=f"adamw_{n}_{l}")
            grads_out[n], delta[n], new_m[n], new_v[n] = [to_arrival(o) for o in out]
            return out[1]
        sums, lands = zip(*[[arr[i] for arr in arrive(l, group, after)] for l in range(DEPTH)])
        g = jnp.stack([_sum_chip_slots(lands[l], sums[l], chip, name="rs_sum_" + n) for l in range(DEPTH)])
        grads_out[n] = swap(g)
        delta[n], new_m[n], new_v[n] = _adamw(weights[n], grads_out[n], mom_m[n], mom_v[n], name="adamw_" + n)
        return delta[n]

    after = small_started[-1]
    for group in ("ffn", "mix", "in"):
        for i, n in enumerate(reductions[0, group][0]):
            after = update(n, group, i, after)

    _, small_lands = _ici_wait("gather", small_started, after, name="gather_small_wait")
    small_full = _d2d_gather([small_lands[0].reshape(N_CHIP, 2, *packed.shape)], after, name="gather_small_d2d")[0]
    total = _sum_slots(small_full.reshape(N_DEV, *packed.shape), name="sum_small")
    small_total = _unpack(total, small_shapes)
    grads_out.update(zip(SMALL + ("g_final",), small_total[:-1]))
    dw_full = small_total[-1]
    grads_out["w_dw"] = lax.dynamic_index_in_dim(dw_full, me, axis=1, keepdims=False)[:, :CONV_W].reshape(w_dw.shape)

    rep = tuple(n for n in SMALL if n != "w_s") + ("g_final",)
    rep_shapes = [weights[n].shape for n in rep]
    packs = [_pack([src[n] for n in rep])[None] for src in (weights, grads_out, mom_m, mom_v)]
    for dst, buf in zip((delta, new_m, new_v), _adamw(*packs, name="adamw_small")):
        dst.update(zip(rep, _unpack(buf[0], rep_shapes)))
    for n, shp in (("w_dw", (1, DEPTH * CONV_W, LANES)), ("w_s", (DEPTH, SG_G * SG_CHUNK, SG_CHUNK))):
        upd = _adamw(*[src[n].reshape(shp) for src in (weights, grads_out, mom_m, mom_v)], name="adamw_" + n)
        for dst, buf in zip((delta, new_m, new_v), upd):
            dst[n] = buf.reshape(weights[n].shape)

    order = ("g_mix", "w_in", "b_gate", "q_norm_g", "k_norm_g", "w_attn_o", "w_dw", "b_dw", "conv_ln_g", "conv_ln_b",
             "w_conv_o", "sg_ln_g", "sg_ln_b", "w_s", "b_s", "w_sg_o", "w_out", "g_ffn", "w_ff_gate", "w_ff_up",
             "w_ff_down", "g_final")
    return (loss, grad_x, *[grads_out[n] for n in order], *[delta[n] for n in order],
            *[new_m[n] for n in order], *[new_v[n] for n in order])
```
